```python
import math
import jax, jax.numpy as jnp
from jax import lax
import numpy as np

D_MODEL = 1024
BATCH = 8
SEQ = 2048
DEPTH = 2

MLA_HEADS = 8
QK_NOPE_DIM = 64
QK_ROPE_DIM = 32
QK_HEAD_DIM = QK_NOPE_DIM + QK_ROPE_DIM
V_HEAD_DIM = 64
Q_LORA_RANK = 384
KV_LORA_RANK = 256
ROPE_THETA = 10000.0
ATTN_BLOCK = 128
MLA_OUT = MLA_HEADS * V_HEAD_DIM
POOL_WINDOWS = (2, 4, 8, 16)
POOL_GROUP_DIM = 128
POOL_GROUPS = len(POOL_WINDOWS)
POOL_DIM = POOL_GROUPS * POOL_GROUP_DIM
SSD_HEADS = 16
SSD_HEAD_DIM = 64
SSD_INNER = SSD_HEADS * SSD_HEAD_DIM
SSD_GROUPS = 2
SSD_STATE = 128
SSD_CONV = 4
SSD_CHUNK = 128
SSD_CONV_DIM = SSD_INNER + 2 * SSD_GROUPS * SSD_STATE
FFN_DIM = 2816
FFN_CONV = 3
N_BRANCH = 3
BRANCH_DIM = MLA_OUT + POOL_DIM + SSD_INNER
IN_SIZES = (Q_LORA_RANK, KV_LORA_RANK + QK_ROPE_DIM, POOL_DIM, SSD_INNER, SSD_CONV_DIM, SSD_HEADS, N_BRANCH * D_MODEL)
IN_DIM = sum(IN_SIZES)
EPS = 1e-6

kernel_name = 'hybrid_mla_pool_ssd_gated_block'


def _split(x, sizes):
    outs, start = [], 0
    for size in sizes:
        outs.append(x[..., start:start + size])
        start += size
    return outs


def rms_norm(x, w):
    xf = x.astype(jnp.float32)
    y = xf * lax.rsqrt(jnp.mean(xf * xf, axis=-1, keepdims=True) + EPS)
    return (y * w.astype(jnp.float32)).astype(x.dtype)


def modulate(h, shift, scale):
    return h * (1 + scale[:, None, :]) + shift[:, None, :]


def causal_dwconv(x, w, b):
    k = w.shape[0]
    y = lax.conv_general_dilated(x, w[:, None, :].astype(x.dtype), window_strides=(1,), padding=[(k - 1, 0)],
                                 dimension_numbers=('NWC', 'WIO', 'NWC'), feature_group_count=x.shape[-1])
    return y + b.astype(x.dtype)


def rope_tables(positions):
    inv_freq = ROPE_THETA ** (-jnp.arange(0, QK_ROPE_DIM, 2, dtype=jnp.float32) / QK_ROPE_DIM)
    ang = positions.astype(jnp.float32)[..., None] * inv_freq
    return jnp.cos(ang), jnp.sin(ang)


def apply_rope(x, cos, sin):
    half = x.shape[-1] // 2
    x1, x2 = x[..., :half], x[..., half:]
    return jnp.concatenate([x1 * cos - x2 * sin, x2 * cos + x1 * sin], axis=-1).astype(x.dtype)


def mla_branch(q_lat, kv_lat, cos, sin, q_a_norm, w_q_b, kv_a_norm, w_kv_b, q_norm, k_norm):
    b, s, _ = q_lat.shape
    q = (rms_norm(q_lat, q_a_norm) @ w_q_b).reshape(b, s, MLA_HEADS, QK_HEAD_DIM)
    q_nope, q_rope = q[..., :QK_NOPE_DIM], q[..., QK_NOPE_DIM:]
    c_kv, k_rope = kv_lat[..., :KV_LORA_RANK], kv_lat[..., KV_LORA_RANK:]
    kv = (rms_norm(c_kv, kv_a_norm) @ w_kv_b).reshape(b, s, MLA_HEADS, QK_NOPE_DIM + V_HEAD_DIM)
    k_nope, v = kv[..., :QK_NOPE_DIM], kv[..., QK_NOPE_DIM:]
    q_nope = rms_norm(q_nope, q_norm[:QK_NOPE_DIM])
    q_rope = apply_rope(rms_norm(q_rope, q_norm[QK_NOPE_DIM:]), cos[:, :, None], sin[:, :, None])
    k_nope = rms_norm(k_nope, k_norm[:QK_NOPE_DIM])
    k_rope = apply_rope(rms_norm(k_rope, k_norm[QK_NOPE_DIM:]), cos, sin)
    n_blk = s // ATTN_BLOCK
    qn = q_nope.reshape(b, n_blk, ATTN_BLOCK, MLA_HEADS, QK_NOPE_DIM).swapaxes(0, 1)
    qr = q_rope.reshape(b, n_blk, ATTN_BLOCK, MLA_HEADS, QK_ROPE_DIM).swapaxes(0, 1)
    key_pos = jnp.arange(s)
    scale = QK_HEAD_DIM ** -0.5

    def attend(args):
        qn_b, qr_b, blk = args
        sc = jnp.einsum('bqhd,bkhd->bhqk', qn_b, k_nope) + jnp.einsum('bqhd,bkd->bhqk', qr_b, k_rope)
        sc = sc.astype(jnp.float32) * scale
        q_pos = blk * ATTN_BLOCK + jnp.arange(ATTN_BLOCK)
        sc = jnp.where(key_pos[None, :] <= q_pos[:, None], sc, -jnp.inf)
        p = jax.nn.softmax(sc, axis=-1).astype(v.dtype)
        return jnp.einsum('bhqk,bkhd->bqhd', p, v)

    o = lax.map(attend, (qn, qr, jnp.arange(n_blk)))
    return o.swapaxes(0, 1).reshape(b, s, MLA_OUT)


def pool_branch(u, pool_w, pool_scale):
    b, s, _ = u.shape
    uf = u.astype(jnp.float32).reshape(b, s, POOL_GROUPS, POOL_GROUP_DIM)
    cs = jnp.cumsum(uf, axis=1)
    t = jnp.arange(s)
    outs = []
    for g, w in enumerate(POOL_WINDOWS):
        csg = cs[:, :, g]
        lagged = jnp.pad(csg, ((0, 0), (w, 0), (0, 0)))[:, :s]
        count = jnp.minimum(t + 1, w).astype(jnp.float32)[None, :, None]
        outs.append((csg - lagged) / count - uf[:, :, g])
    pooled = jnp.stack(outs, axis=2).astype(u.dtype)
    mixed = jnp.einsum('bsgc,gcd->bsgd', pooled, pool_w)
    return mixed.reshape(b, s, POOL_DIM) * pool_scale


def ssd_branch(z, xbc, dt_raw, conv_w, conv_b, dt_bias, a_log, d_skip, norm_w):
    b, s, _ = z.shape
    nc, lc = s // SSD_CHUNK, SSD_CHUNK
    r = SSD_HEADS // SSD_GROUPS
    xbc = jax.nn.silu(causal_dwconv(xbc, conv_w, conv_b))
    xs, bm, cm = _split(xbc, (SSD_INNER, SSD_GROUPS * SSD_STATE, SSD_GROUPS * SSD_STATE))
    dt = jax.nn.softplus(dt_raw.astype(jnp.float32) + dt_bias.astype(jnp.float32))
    a = -jnp.exp(a_log.astype(jnp.float32))
    x_h = xs.reshape(b, s, SSD_HEADS, SSD_HEAD_DIM)
    xdt = (x_h.astype(jnp.float32) * dt[..., None]).reshape(b, nc, lc, SSD_GROUPS, r, SSD_HEAD_DIM)
    bc = bm.astype(jnp.float32).reshape(b, nc, lc, SSD_GROUPS, SSD_STATE)
    cc = cm.astype(jnp.float32).reshape(b, nc, lc, SSD_GROUPS, SSD_STATE)
    da = (dt * a).reshape(b, nc, lc, SSD_GROUPS, r).transpose(0, 3, 4, 1, 2)
    a_cs = jnp.cumsum(da, axis=-1)
    causal = jnp.tril(jnp.ones((lc, lc), dtype=bool))
    seg = a_cs[..., :, None] - a_cs[..., None, :]
    decay = jnp.exp(jnp.where(causal, seg, -jnp.inf))
    cb = jnp.einsum('bclgn,bcsgn->bcgls', cc, bc)
    y_diag = jnp.einsum('bcgls,bgrcls,bcsgrp->bclgrp', cb, decay, xdt)
    decay_states = jnp.exp(a_cs[..., -1:] - a_cs)
    states = jnp.einsum('bclgn,bgrcl,bclgrp->bcgrpn', bc, decay_states, xdt)
    chunk_decay = jnp.exp(a_cs[..., -1])

    def step(h, inp):
        st, dc = inp
        return h * dc[..., None, None] + st, h

    h0 = jnp.zeros((b, SSD_GROUPS, r, SSD_HEAD_DIM, SSD_STATE), jnp.float32)
    _, prev = lax.scan(step, h0, (states.transpose(1, 0, 2, 3, 4, 5), chunk_decay.transpose(3, 0, 1, 2)))
    y_off = jnp.einsum('bclgn,cbgrpn,bgrcl->bclgrp', cc, prev, jnp.exp(a_cs))
    y = (y_diag + y_off).reshape(b, s, SSD_HEADS, SSD_HEAD_DIM) + x_h * d_skip[:, None]
    gated = y.reshape(b, s, SSD_INNER) * jax.nn.silu(z.astype(jnp.float32))
    gated = gated.reshape(b, s, SSD_GROUPS, SSD_INNER // SSD_GROUPS)
    gated = gated * lax.rsqrt(jnp.mean(gated * gated, axis=-1, keepdims=True) + EPS)
    return (gated.reshape(b, s, SSD_INNER) * norm_w.astype(jnp.float32)).astype(z.dtype)


def _fwd_setup_inputs(seed: int = 0) -> dict:
    key = jax.random.key(seed)
    ks = iter(jax.random.split(key, 40))
    L, D = DEPTH, D_MODEL

    def nrm(shape, scale):
        return jax.random.normal(next(ks), shape, jnp.float32) * scale

    x = nrm((BATCH, SEQ, D), 1.0)
    c = nrm((BATCH, D), 1.0)
    offset = jax.random.randint(next(ks), (BATCH, 1), 0, 4096, dtype=jnp.int32)
    positions = offset + jnp.arange(SEQ, dtype=jnp.int32)[None, :]
    ada_w = nrm((L, D, 6 * D), D ** -0.5)
    ada_b = nrm((L, 6 * D), 0.02)
    norm1_w = 1.0 + nrm((L, D), 0.02)
    w_in = nrm((L, D, IN_DIM), D ** -0.5)
    q_a_norm = 1.0 + nrm((L, Q_LORA_RANK), 0.02)
    w_q_b = nrm((L, Q_LORA_RANK, MLA_HEADS * QK_HEAD_DIM), Q_LORA_RANK ** -0.5)
    kv_a_norm = 1.0 + nrm((L, KV_LORA_RANK), 0.02)
    w_kv_b = nrm((L, KV_LORA_RANK, MLA_HEADS * (QK_NOPE_DIM + V_HEAD_DIM)), KV_LORA_RANK ** -0.5)
    q_norm = 1.0 + nrm((L, QK_HEAD_DIM), 0.02)
    k_norm = 1.0 + nrm((L, QK_HEAD_DIM), 0.02)
    pool_w = nrm((L, POOL_GROUPS, POOL_GROUP_DIM, POOL_GROUP_DIM), POOL_GROUP_DIM ** -0.5)
    pool_scale = 1.0 + nrm((L, POOL_DIM), 0.1)
    ssd_conv_w = nrm((L, SSD_CONV, SSD_CONV_DIM), SSD_CONV ** -0.5)
    ssd_conv_b = nrm((L, SSD_CONV_DIM), 0.02)
    dt0 = jnp.exp(jax.random.uniform(next(ks), (L, SSD_HEADS), jnp.float32, math.log(1e-3), math.log(1e-1)))
    ssd_dt_bias = dt0 + jnp.log(-jnp.expm1(-dt0))
    ssd_a_log = jnp.log(jax.random.uniform(next(ks), (L, SSD_HEADS), jnp.float32, 1.0, 16.0))
    ssd_d = 1.0 + nrm((L, SSD_HEADS), 0.1)
    ssd_norm_w = 1.0 + nrm((L, SSD_INNER), 0.02)
    w_branch = nrm((L, BRANCH_DIM, D), MLA_OUT ** -0.5)
    w_out = nrm((L, D, D), D ** -0.5)
    norm2_w = 1.0 + nrm((L, D), 0.02)
    ffn_up = nrm((L, D, 2 * FFN_DIM), D ** -0.5)
    ffn_conv_w = nrm((L, FFN_CONV, 2 * FFN_DIM), FFN_CONV ** -0.5)
    ffn_conv_b = nrm((L, 2 * FFN_DIM), 0.02)
    ffn_down = nrm((L, FFN_DIM, D), FFN_DIM ** -0.5)
    return {'x': x, 'c': c, 'positions': positions, 'ada_w': ada_w, 'ada_b': ada_b, 'norm1_w': norm1_w,
            'w_in': w_in, 'q_a_norm': q_a_norm, 'w_q_b': w_q_b, 'kv_a_norm': kv_a_norm, 'w_kv_b': w_kv_b,
            'q_norm': q_norm, 'k_norm': k_norm, 'pool_w': pool_w, 'pool_scale': pool_scale,
            'ssd_conv_w': ssd_conv_w, 'ssd_conv_b': ssd_conv_b, 'ssd_dt_bias': ssd_dt_bias,
            'ssd_a_log': ssd_a_log, 'ssd_d': ssd_d, 'ssd_norm_w': ssd_norm_w, 'w_branch': w_branch,
            'w_out': w_out, 'norm2_w': norm2_w, 'ffn_up': ffn_up, 'ffn_conv_w': ffn_conv_w,
            'ffn_conv_b': ffn_conv_b, 'ffn_down': ffn_down}


def _fwd_reference(x, c, positions, ada_w, ada_b, norm1_w, w_in, q_a_norm, w_q_b, kv_a_norm, w_kv_b, q_norm, k_norm,
              pool_w, pool_scale, ssd_conv_w, ssd_conv_b, ssd_dt_bias, ssd_a_log, ssd_d, ssd_norm_w, w_branch,
              w_out, norm2_w, ffn_up, ffn_conv_w, ffn_conv_b, ffn_down):
    cos, sin = rope_tables(positions)
    c_act = jax.nn.silu(c)
    for l in range(DEPTH):
        mod = c_act @ ada_w[l] + ada_b[l]
        sh1, sc1, g1, sh2, sc2, g2 = jnp.split(mod, 6, axis=-1)
        h = modulate(rms_norm(x, norm1_w[l]), sh1, sc1)
        proj = h @ w_in[l]
        q_lat, kv_lat, u_pool, z, xbc, dt_raw, gate_logits = _split(proj, IN_SIZES)
        o_a = mla_branch(q_lat, kv_lat, cos, sin, q_a_norm[l], w_q_b[l], kv_a_norm[l], w_kv_b[l],
                         q_norm[l], k_norm[l])
        o_b = pool_branch(u_pool, pool_w[l], pool_scale[l])
        o_c = ssd_branch(z, xbc, dt_raw, ssd_conv_w[l], ssd_conv_b[l], ssd_dt_bias[l], ssd_a_log[l],
                         ssd_d[l], ssd_norm_w[l])
        wb = w_branch[l]
        y_a = o_a @ wb[:MLA_OUT]
        y_b = o_b @ wb[MLA_OUT:MLA_OUT + POOL_DIM]
        y_c = o_c @ wb[MLA_OUT + POOL_DIM:]
        gate_a, gate_b, gate_c = jnp.split(jax.nn.sigmoid(gate_logits), N_BRANCH, axis=-1)
        merged = gate_a * y_a + gate_b * y_b + gate_c * y_c
        x = x + g1[:, None, :] * (merged @ w_out[l])
        h = modulate(rms_norm(x, norm2_w[l]), sh2, sc2)
        up = causal_dwconv(h @ ffn_up[l], ffn_conv_w[l], ffn_conv_b[l])
        u_gate, u_val = jnp.split(up, 2, axis=-1)
        x = x + g2[:, None, :] * ((jax.nn.silu(u_gate) * u_val) @ ffn_down[l])
    return x


import jax as _jax
import jax.numpy as _jnp

TWIN_FORMAT = 'train_step'
FWD_PARAMS = ['x', 'c', 'positions', 'ada_w', 'ada_b', 'norm1_w', 'w_in', 'q_a_norm', 'w_q_b', 'kv_a_norm', 'w_kv_b', 'q_norm', 'k_norm', 'pool_w', 'pool_scale', 'ssd_conv_w', 'ssd_conv_b', 'ssd_dt_bias', 'ssd_a_log', 'ssd_d', 'ssd_norm_w', 'w_branch', 'w_out', 'norm2_w', 'ffn_up', 'ffn_conv_w', 'ffn_conv_b', 'ffn_down']
TWIN_WEIGHTS = ['ada_w', 'ada_b', 'norm1_w', 'w_in', 'q_a_norm', 'w_q_b', 'kv_a_norm', 'w_kv_b', 'q_norm', 'k_norm', 'pool_w', 'pool_scale', 'ssd_conv_w', 'ssd_conv_b', 'ssd_dt_bias', 'ssd_a_log', 'ssd_d', 'ssd_norm_w', 'w_branch', 'w_out', 'norm2_w', 'ffn_up', 'ffn_conv_w', 'ffn_conv_b', 'ffn_down']
TWIN_DIFF_INPUT = 'x'
TWIN_INPUTS = ['x', 'c', 'positions', 'ada_w', 'ada_b', 'norm1_w', 'w_in', 'q_a_norm', 'w_q_b', 'kv_a_norm', 'w_kv_b', 'q_norm', 'k_norm', 'pool_w', 'pool_scale', 'ssd_conv_w', 'ssd_conv_b', 'ssd_dt_bias', 'ssd_a_log', 'ssd_d', 'ssd_norm_w', 'w_branch', 'w_out', 'norm2_w', 'ffn_up', 'ffn_conv_w', 'ffn_conv_b', 'ffn_down', 'loss_target', 'm_ada_w', 'm_ada_b', 'm_norm1_w', 'm_w_in', 'm_q_a_norm', 'm_w_q_b', 'm_kv_a_norm', 'm_w_kv_b', 'm_q_norm', 'm_k_norm', 'm_pool_w', 'm_pool_scale', 'm_ssd_conv_w', 'm_ssd_conv_b', 'm_ssd_dt_bias', 'm_ssd_a_log', 'm_ssd_d', 'm_ssd_norm_w', 'm_w_branch', 'm_w_out', 'm_norm2_w', 'm_ffn_up', 'm_ffn_conv_w', 'm_ffn_conv_b', 'm_ffn_down', 'v_ada_w', 'v_ada_b', 'v_norm1_w', 'v_w_in', 'v_q_a_norm', 'v_w_q_b', 'v_kv_a_norm', 'v_w_kv_b', 'v_q_norm', 'v_k_norm', 'v_pool_w', 'v_pool_scale', 'v_ssd_conv_w', 'v_ssd_conv_b', 'v_ssd_dt_bias', 'v_ssd_a_log', 'v_ssd_d', 'v_ssd_norm_w', 'v_w_branch', 'v_w_out', 'v_norm2_w', 'v_ffn_up', 'v_ffn_conv_w', 'v_ffn_conv_b', 'v_ffn_down']
TWIN_OUTPUTS = ['loss', 'grad_x', 'grad_ada_w', 'grad_ada_b', 'grad_norm1_w', 'grad_w_in', 'grad_q_a_norm', 'grad_w_q_b', 'grad_kv_a_norm', 'grad_w_kv_b', 'grad_q_norm', 'grad_k_norm', 'grad_pool_w', 'grad_pool_scale', 'grad_ssd_conv_w', 'grad_ssd_conv_b', 'grad_ssd_dt_bias', 'grad_ssd_a_log', 'grad_ssd_d', 'grad_ssd_norm_w', 'grad_w_branch', 'grad_w_out', 'grad_norm2_w', 'grad_ffn_up', 'grad_ffn_conv_w', 'grad_ffn_conv_b', 'grad_ffn_down', 'delta_ada_w', 'delta_ada_b', 'delta_norm1_w', 'delta_w_in', 'delta_q_a_norm', 'delta_w_q_b', 'delta_kv_a_norm', 'delta_w_kv_b', 'delta_q_norm', 'delta_k_norm', 'delta_pool_w', 'delta_pool_scale', 'delta_ssd_conv_w', 'delta_ssd_conv_b', 'delta_ssd_dt_bias', 'delta_ssd_a_log', 'delta_ssd_d', 'delta_ssd_norm_w', 'delta_w_branch', 'delta_w_out', 'delta_norm2_w', 'delta_ffn_up', 'delta_ffn_conv_w', 'delta_ffn_conv_b', 'delta_ffn_down', 'new_m_ada_w', 'new_m_ada_b', 'new_m_norm1_w', 'new_m_w_in', 'new_m_q_a_norm', 'new_m_w_q_b', 'new_m_kv_a_norm', 'new_m_w_kv_b', 'new_m_q_norm', 'new_m_k_norm', 'new_m_pool_w', 'new_m_pool_scale', 'new_m_ssd_conv_w', 'new_m_ssd_conv_b', 'new_m_ssd_dt_bias', 'new_m_ssd_a_log', 'new_m_ssd_d', 'new_m_ssd_norm_w', 'new_m_w_branch', 'new_m_w_out', 'new_m_norm2_w', 'new_m_ffn_up', 'new_m_ffn_conv_w', 'new_m_ffn_conv_b', 'new_m_ffn_down', 'new_v_ada_w', 'new_v_ada_b', 'new_v_norm1_w', 'new_v_w_in', 'new_v_q_a_norm', 'new_v_w_q_b', 'new_v_kv_a_norm', 'new_v_w_kv_b', 'new_v_q_norm', 'new_v_k_norm', 'new_v_pool_w', 'new_v_pool_scale', 'new_v_ssd_conv_w', 'new_v_ssd_conv_b', 'new_v_ssd_dt_bias', 'new_v_ssd_a_log', 'new_v_ssd_d', 'new_v_ssd_norm_w', 'new_v_w_branch', 'new_v_w_out', 'new_v_norm2_w', 'new_v_ffn_up', 'new_v_ffn_conv_w', 'new_v_ffn_conv_b', 'new_v_ffn_down']
TWIN_LEAF_KINDS = {'loss': 'loss', 'grad_x': 'grad_x', 'grad_ada_w': 'grad_w', 'grad_ada_b': 'grad_w', 'grad_norm1_w': 'grad_w', 'grad_w_in': 'grad_w', 'grad_q_a_norm': 'grad_w', 'grad_w_q_b': 'grad_w', 'grad_kv_a_norm': 'grad_w', 'grad_w_kv_b': 'grad_w', 'grad_q_norm': 'grad_w', 'grad_k_norm': 'grad_w', 'grad_pool_w': 'grad_w', 'grad_pool_scale': 'grad_w', 'grad_ssd_conv_w': 'grad_w', 'grad_ssd_conv_b': 'grad_w', 'grad_ssd_dt_bias': 'grad_w', 'grad_ssd_a_log': 'grad_w', 'grad_ssd_d': 'grad_w', 'grad_ssd_norm_w': 'grad_w', 'grad_w_branch': 'grad_w', 'grad_w_out': 'grad_w', 'grad_norm2_w': 'grad_w', 'grad_ffn_up': 'grad_w', 'grad_ffn_conv_w': 'grad_w', 'grad_ffn_conv_b': 'grad_w', 'grad_ffn_down': 'grad_w', 'delta_ada_w': 'delta_w', 'delta_ada_b': 'delta_w', 'delta_norm1_w': 'delta_w', 'delta_w_in': 'delta_w', 'delta_q_a_norm': 'delta_w', 'delta_w_q_b': 'delta_w', 'delta_kv_a_norm': 'delta_w', 'delta_w_kv_b': 'delta_w', 'delta_q_norm': 'delta_w', 'delta_k_norm': 'delta_w', 'delta_pool_w': 'delta_w', 'delta_pool_scale': 'delta_w', 'delta_ssd_conv_w': 'delta_w', 'delta_ssd_conv_b': 'delta_w', 'delta_ssd_dt_bias': 'delta_w', 'delta_ssd_a_log': 'delta_w', 'delta_ssd_d': 'delta_w', 'delta_ssd_norm_w': 'delta_w', 'delta_w_branch': 'delta_w', 'delta_w_out': 'delta_w', 'delta_norm2_w': 'delta_w', 'delta_ffn_up': 'delta_w', 'delta_ffn_conv_w': 'delta_w', 'delta_ffn_conv_b': 'delta_w', 'delta_ffn_down': 'delta_w', 'new_m_ada_w': 'new_m', 'new_m_ada_b': 'new_m', 'new_m_norm1_w': 'new_m', 'new_m_w_in': 'new_m', 'new_m_q_a_norm': 'new_m', 'new_m_w_q_b': 'new_m', 'new_m_kv_a_norm': 'new_m', 'new_m_w_kv_b': 'new_m', 'new_m_q_norm': 'new_m', 'new_m_k_norm': 'new_m', 'new_m_pool_w': 'new_m', 'new_m_pool_scale': 'new_m', 'new_m_ssd_conv_w': 'new_m', 'new_m_ssd_conv_b': 'new_m', 'new_m_ssd_dt_bias': 'new_m', 'new_m_ssd_a_log': 'new_m', 'new_m_ssd_d': 'new_m', 'new_m_ssd_norm_w': 'new_m', 'new_m_w_branch': 'new_m', 'new_m_w_out': 'new_m', 'new_m_norm2_w': 'new_m', 'new_m_ffn_up': 'new_m', 'new_m_ffn_conv_w': 'new_m', 'new_m_ffn_conv_b': 'new_m', 'new_m_ffn_down': 'new_m', 'new_v_ada_w': 'new_v', 'new_v_ada_b': 'new_v', 'new_v_norm1_w': 'new_v', 'new_v_w_in': 'new_v', 'new_v_q_a_norm': 'new_v', 'new_v_w_q_b': 'new_v', 'new_v_kv_a_norm': 'new_v', 'new_v_w_kv_b': 'new_v', 'new_v_q_norm': 'new_v', 'new_v_k_norm': 'new_v', 'new_v_pool_w': 'new_v', 'new_v_pool_scale': 'new_v', 'new_v_ssd_conv_w': 'new_v', 'new_v_ssd_conv_b': 'new_v', 'new_v_ssd_dt_bias': 'new_v', 'new_v_ssd_a_log': 'new_v', 'new_v_ssd_d': 'new_v', 'new_v_ssd_norm_w': 'new_v', 'new_v_w_branch': 'new_v', 'new_v_w_out': 'new_v', 'new_v_norm2_w': 'new_v', 'new_v_ffn_up': 'new_v', 'new_v_ffn_conv_w': 'new_v', 'new_v_ffn_conv_b': 'new_v', 'new_v_ffn_down': 'new_v'}


def _forward(args):
    return _fwd_reference(*[args[k] for k in FWD_PARAMS])


def _output_shape():
    out = _jax.eval_shape(lambda: _forward(_fwd_setup_inputs(0)))
    return out.shape, out.dtype

N_MICROBATCH = 1
ADAM_LR = 0.001
ADAM_B1 = 0.9
ADAM_B2 = 0.999
ADAM_EPS = 1e-08
ADAM_WD = 0.01
ADAM_STEP = 10
PER_EXAMPLE_BATCH_AXIS = {'x': 0, 'c': 0, 'positions': 0, 'loss_target': 0}
SHARED_INPUTS = []
_WEIGHT_DTYPES = {'ada_w': _jnp.float32, 'ada_b': _jnp.float32, 'norm1_w': _jnp.float32, 'w_in': _jnp.float32, 'q_a_norm': _jnp.float32, 'w_q_b': _jnp.float32, 'kv_a_norm': _jnp.float32, 'w_kv_b': _jnp.float32, 'q_norm': _jnp.float32, 'k_norm': _jnp.float32, 'pool_w': _jnp.float32, 'pool_scale': _jnp.float32, 'ssd_conv_w': _jnp.float32, 'ssd_conv_b': _jnp.float32, 'ssd_dt_bias': _jnp.float32, 'ssd_a_log': _jnp.float32, 'ssd_d': _jnp.float32, 'ssd_norm_w': _jnp.float32, 'w_branch': _jnp.float32, 'w_out': _jnp.float32, 'norm2_w': _jnp.float32, 'ffn_up': _jnp.float32, 'ffn_conv_w': _jnp.float32, 'ffn_conv_b': _jnp.float32, 'ffn_down': _jnp.float32}
MOMENT_SCALE = {'ada_w': 3.524314e+00, 'ada_b': 7.233912e+00, 'norm1_w': 2.708022e+00, 'w_in': 7.961083e-01, 'q_a_norm': 8.452666e-02, 'w_q_b': 5.764541e-02, 'kv_a_norm': 3.210103e+00, 'w_kv_b': 1.310060e+00, 'q_norm': 1.895009e-01, 'k_norm': 1.880586e-01, 'pool_w': 4.691827e-01, 'pool_scale': 4.151556e+00, 'ssd_conv_w': 1.015410e+00, 'ssd_conv_b': 1.428871e+00, 'ssd_dt_bias': 1.945068e+00, 'ssd_a_log': 5.179517e+00, 'ssd_d': 4.631800e+00, 'ssd_norm_w': 8.174450e+00, 'w_branch': 9.982663e-01, 'w_out': 1.951080e+00, 'norm2_w': 1.587091e+01, 'ffn_up': 1.618452e+00, 'ffn_conv_w': 3.250044e+00, 'ffn_conv_b': 2.254034e+00, 'ffn_down': 1.216371e+00}


def _to_microbatches(a, axis):
    t = _jnp.moveaxis(a, axis, 0)
    t = t.reshape((N_MICROBATCH, t.shape[0] // N_MICROBATCH) + t.shape[1:])
    return _jnp.moveaxis(t, 1, axis + 1)


def setup_inputs(seed: int = 0) -> dict:
    inp = _fwd_setup_inputs(seed)
    key = _jax.random.fold_in(_jax.random.key(seed), 7919)
    shape, _ = _output_shape()
    out = dict(inp)
    out["loss_target"] = _jax.random.normal(_jax.random.fold_in(key, 0), shape, _jnp.float32)
    for i, name in enumerate(TWIN_WEIGHTS):
        w = inp[name].astype(_jnp.float32)
        if MOMENT_SCALE is None:
            s = _jnp.sqrt(_jnp.mean(_jnp.square(w)) + 1e-30)
        else:
            s = MOMENT_SCALE[name]
        km, kv = _jax.random.split(_jax.random.fold_in(key, i + 1))
        out[name] = w
        out["m_" + name] = s * _jax.random.normal(km, w.shape, _jnp.float32)
        out["v_" + name] = (s * s) * _jax.random.uniform(kv, w.shape, _jnp.float32, 0.5, 1.5)
    if N_MICROBATCH > 1:
        for name, axis in PER_EXAMPLE_BATCH_AXIS.items():
            out[name] = _to_microbatches(out[name], axis)
    return {'x': out['x'], 'c': out['c'], 'positions': out['positions'], 'ada_w': out['ada_w'], 'ada_b': out['ada_b'], 'norm1_w': out['norm1_w'], 'w_in': out['w_in'], 'q_a_norm': out['q_a_norm'], 'w_q_b': out['w_q_b'], 'kv_a_norm': out['kv_a_norm'], 'w_kv_b': out['w_kv_b'], 'q_norm': out['q_norm'], 'k_norm': out['k_norm'], 'pool_w': out['pool_w'], 'pool_scale': out['pool_scale'], 'ssd_conv_w': out['ssd_conv_w'], 'ssd_conv_b': out['ssd_conv_b'], 'ssd_dt_bias': out['ssd_dt_bias'], 'ssd_a_log': out['ssd_a_log'], 'ssd_d': out['ssd_d'], 'ssd_norm_w': out['ssd_norm_w'], 'w_branch': out['w_branch'], 'w_out': out['w_out'], 'norm2_w': out['norm2_w'], 'ffn_up': out['ffn_up'], 'ffn_conv_w': out['ffn_conv_w'], 'ffn_conv_b': out['ffn_conv_b'], 'ffn_down': out['ffn_down'], 'loss_target': out['loss_target'], 'm_ada_w': out['m_ada_w'], 'm_ada_b': out['m_ada_b'], 'm_norm1_w': out['m_norm1_w'], 'm_w_in': out['m_w_in'], 'm_q_a_norm': out['m_q_a_norm'], 'm_w_q_b': out['m_w_q_b'], 'm_kv_a_norm': out['m_kv_a_norm'], 'm_w_kv_b': out['m_w_kv_b'], 'm_q_norm': out['m_q_norm'], 'm_k_norm': out['m_k_norm'], 'm_pool_w': out['m_pool_w'], 'm_pool_scale': out['m_pool_scale'], 'm_ssd_conv_w': out['m_ssd_conv_w'], 'm_ssd_conv_b': out['m_ssd_conv_b'], 'm_ssd_dt_bias': out['m_ssd_dt_bias'], 'm_ssd_a_log': out['m_ssd_a_log'], 'm_ssd_d': out['m_ssd_d'], 'm_ssd_norm_w': out['m_ssd_norm_w'], 'm_w_branch': out['m_w_branch'], 'm_w_out': out['m_w_out'], 'm_norm2_w': out['m_norm2_w'], 'm_ffn_up': out['m_ffn_up'], 'm_ffn_conv_w': out['m_ffn_conv_w'], 'm_ffn_conv_b': out['m_ffn_conv_b'], 'm_ffn_down': out['m_ffn_down'], 'v_ada_w': out['v_ada_w'], 'v_ada_b': out['v_ada_b'], 'v_norm1_w': out['v_norm1_w'], 'v_w_in': out['v_w_in'], 'v_q_a_norm': out['v_q_a_norm'], 'v_w_q_b': out['v_w_q_b'], 'v_kv_a_norm': out['v_kv_a_norm'], 'v_w_kv_b': out['v_w_kv_b'], 'v_q_norm': out['v_q_norm'], 'v_k_norm': out['v_k_norm'], 'v_pool_w': out['v_pool_w'], 'v_pool_scale': out['v_pool_scale'], 'v_ssd_conv_w': out['v_ssd_conv_w'], 'v_ssd_conv_b': out['v_ssd_conv_b'], 'v_ssd_dt_bias': out['v_ssd_dt_bias'], 'v_ssd_a_log': out['v_ssd_a_log'], 'v_ssd_d': out['v_ssd_d'], 'v_ssd_norm_w': out['v_ssd_norm_w'], 'v_w_branch': out['v_w_branch'], 'v_w_out': out['v_w_out'], 'v_norm2_w': out['v_norm2_w'], 'v_ffn_up': out['v_ffn_up'], 'v_ffn_conv_w': out['v_ffn_conv_w'], 'v_ffn_conv_b': out['v_ffn_conv_b'], 'v_ffn_down': out['v_ffn_down']}


def _loss(weights, diff, rest, loss_target):
    with _jax.named_scope("forward"):
        args = {**rest, TWIN_DIFF_INPUT: diff, **{k: w.astype(_WEIGHT_DTYPES[k]) for k, w in weights.items()}}
        y = _forward(args)
    with _jax.named_scope("loss_head"):
        err = _jnp.square(y.astype(_jnp.float32) - loss_target)
        return 0.5 * _jnp.sum(_jnp.mean(err, axis=-1)) if err.ndim else 0.5 * err


def _adamw(w, g, m, v):
    m = ADAM_B1 * m + (1.0 - ADAM_B1) * g
    v = ADAM_B2 * v + (1.0 - ADAM_B2) * _jnp.square(g)
    m_hat = m / (1.0 - ADAM_B1 ** ADAM_STEP)
    v_hat = v / (1.0 - ADAM_B2 ** ADAM_STEP)
    delta = -ADAM_LR * (m_hat / (_jnp.sqrt(v_hat) + ADAM_EPS) + ADAM_WD * w)
    return delta, m, v


def reference(x, c, positions, ada_w, ada_b, norm1_w, w_in, q_a_norm, w_q_b, kv_a_norm, w_kv_b, q_norm, k_norm, pool_w, pool_scale, ssd_conv_w, ssd_conv_b, ssd_dt_bias, ssd_a_log, ssd_d, ssd_norm_w, w_branch, w_out, norm2_w, ffn_up, ffn_conv_w, ffn_conv_b, ffn_down, loss_target, m_ada_w, m_ada_b, m_norm1_w, m_w_in, m_q_a_norm, m_w_q_b, m_kv_a_norm, m_w_kv_b, m_q_norm, m_k_norm, m_pool_w, m_pool_scale, m_ssd_conv_w, m_ssd_conv_b, m_ssd_dt_bias, m_ssd_a_log, m_ssd_d, m_ssd_norm_w, m_w_branch, m_w_out, m_norm2_w, m_ffn_up, m_ffn_conv_w, m_ffn_conv_b, m_ffn_down, v_ada_w, v_ada_b, v_norm1_w, v_w_in, v_q_a_norm, v_w_q_b, v_kv_a_norm, v_w_kv_b, v_q_norm, v_k_norm, v_pool_w, v_pool_scale, v_ssd_conv_w, v_ssd_conv_b, v_ssd_dt_bias, v_ssd_a_log, v_ssd_d, v_ssd_norm_w, v_w_branch, v_w_out, v_norm2_w, v_ffn_up, v_ffn_conv_w, v_ffn_conv_b, v_ffn_down):
    given = dict(x=x, c=c, positions=positions, ada_w=ada_w, ada_b=ada_b, norm1_w=norm1_w, w_in=w_in, q_a_norm=q_a_norm, w_q_b=w_q_b, kv_a_norm=kv_a_norm, w_kv_b=w_kv_b, q_norm=q_norm, k_norm=k_norm, pool_w=pool_w, pool_scale=pool_scale, ssd_conv_w=ssd_conv_w, ssd_conv_b=ssd_conv_b, ssd_dt_bias=ssd_dt_bias, ssd_a_log=ssd_a_log, ssd_d=ssd_d, ssd_norm_w=ssd_norm_w, w_branch=w_branch, w_out=w_out, norm2_w=norm2_w, ffn_up=ffn_up, ffn_conv_w=ffn_conv_w, ffn_conv_b=ffn_conv_b, ffn_down=ffn_down, loss_target=loss_target, m_ada_w=m_ada_w, m_ada_b=m_ada_b, m_norm1_w=m_norm1_w, m_w_in=m_w_in, m_q_a_norm=m_q_a_norm, m_w_q_b=m_w_q_b, m_kv_a_norm=m_kv_a_norm, m_w_kv_b=m_w_kv_b, m_q_norm=m_q_norm, m_k_norm=m_k_norm, m_pool_w=m_pool_w, m_pool_scale=m_pool_scale, m_ssd_conv_w=m_ssd_conv_w, m_ssd_conv_b=m_ssd_conv_b, m_ssd_dt_bias=m_ssd_dt_bias, m_ssd_a_log=m_ssd_a_log, m_ssd_d=m_ssd_d, m_ssd_norm_w=m_ssd_norm_w, m_w_branch=m_w_branch, m_w_out=m_w_out, m_norm2_w=m_norm2_w, m_ffn_up=m_ffn_up, m_ffn_conv_w=m_ffn_conv_w, m_ffn_conv_b=m_ffn_conv_b, m_ffn_down=m_ffn_down, v_ada_w=v_ada_w, v_ada_b=v_ada_b, v_norm1_w=v_norm1_w, v_w_in=v_w_in, v_q_a_norm=v_q_a_norm, v_w_q_b=v_w_q_b, v_kv_a_norm=v_kv_a_norm, v_w_kv_b=v_w_kv_b, v_q_norm=v_q_norm, v_k_norm=v_k_norm, v_pool_w=v_pool_w, v_pool_scale=v_pool_scale, v_ssd_conv_w=v_ssd_conv_w, v_ssd_conv_b=v_ssd_conv_b, v_ssd_dt_bias=v_ssd_dt_bias, v_ssd_a_log=v_ssd_a_log, v_ssd_d=v_ssd_d, v_ssd_norm_w=v_ssd_norm_w, v_w_branch=v_w_branch, v_w_out=v_w_out, v_norm2_w=v_norm2_w, v_ffn_up=v_ffn_up, v_ffn_conv_w=v_ffn_conv_w, v_ffn_conv_b=v_ffn_conv_b, v_ffn_down=v_ffn_down)
    weights = {n: given[n] for n in TWIN_WEIGHTS}
    shared = {n: given[n] for n in SHARED_INPUTS}
    per_example = {n: given[n] for n in ['x', 'c', 'positions']}
    grad_fn = _jax.value_and_grad(_loss, argnums=(0, 1))

    def one_microbatch(ex, loss_target):
        ex = dict(ex)
        diff = ex.pop(TWIN_DIFF_INPUT)
        return grad_fn(weights, diff, {**shared, **ex}, loss_target)

    if N_MICROBATCH == 1:
        loss, (grad_w, grad_x) = one_microbatch(per_example, given["loss_target"])
    else:
        def body(carry, xs):
            loss_sum, grad_sum = carry
            l_k, (gw_k, gx_k) = one_microbatch(xs[0], xs[1])
            with _jax.named_scope("update"):
                return (loss_sum + l_k, _jax.tree.map(_jnp.add, grad_sum, gw_k)), gx_k

        init = (_jnp.zeros((), _jnp.float32), _jax.tree.map(_jnp.zeros_like, weights))
        (loss, grad_w), grad_x = _jax.lax.scan(body, init, (per_example, given["loss_target"]))
    with _jax.named_scope("update"):
        delta_w, new_m, new_v = {}, {}, {}
        for n in TWIN_WEIGHTS:
            delta_w[n], new_m[n], new_v[n] = _adamw(weights[n], grad_w[n], given["m_" + n], given["v_" + n])
    return (loss, grad_x, *[grad_w[n] for n in TWIN_WEIGHTS], *[delta_w[n] for n in TWIN_WEIGHTS],
            *[new_m[n] for n in TWIN_WEIGHTS], *[new_v[n] for n in TWIN_WEIGHTS])
```

```python
import functools
import math

import jax
import jax.numpy as jnp
from jax import lax
from jax.experimental import pallas as pl
from jax.experimental.pallas import tpu as pltpu

F32, BF16 = jnp.float32, jnp.bfloat16
EPS = 1e-6
D = 1024
NDEV = 8
LAYERS = 2
HEADS = 8
FFN = 2816
FFN_TILE = 1408
FFN_NT = FFN // FFN_TILE
ATT_SCALE = 96 ** -0.5
ROPE_THETA = 10000.0
LR, B1, B2, ADAM_EPS, WD, STEP = 0.001, 0.9, 0.999, 1e-08, 0.01, 10

O_G, O_XS, O_Z, O_PU, O_BC, O_CKV, O_KR, O_KRS, O_DT, O_QL = 0, 3072, 4096, 5120, 5632, 6144, 6400, 6528, 6656, 6912
NPROJ = 7296
CONST = dict(pipeline_mode=pl.Buffered(1))


def _pick(n, cap, mult=128):
    if n <= cap:
        return n
    best = None
    for t in range(mult, cap + 1, mult):
        if n % t == 0:
            best = t
    assert best is not None, (n, cap, mult)
    return best


def _sig(x):
    return 1.0 / (1.0 + jnp.exp(-x))


def _rms(x, w, n):
    return x * lax.rsqrt(jnp.sum(x * x, axis=-1, keepdims=True) / n + EPS) * w


def _raw(a, b, dims):
    return lax.dot_general(a.astype(BF16), b.astype(BF16), dims, preferred_element_type=F32)


_NN = (((1,), (0,)), ((), ()))
_NT = (((1,), (1,)), ((), ()))
_TN = (((0,), (0,)), ((), ()))
_BNN = (((2,), (1,)), ((0,), (0,)))
_BNT = (((2,), (2,)), ((0,), (0,)))
_BTN = (((1,), (1,)), ((0,), (0,)))


@jax.custom_vjp
def mm_nn(a, b):
    return _raw(a, b, _NN)


mm_nn.defvjp(lambda a, b: (_raw(a, b, _NN), (a, b)),
             lambda r, g: (_raw(g, r[1], _NT), _raw(r[0], g, _TN)))


@jax.custom_vjp
def mm_nc(a, b):
    return _raw(a, b, _NN)


mm_nc.defvjp(lambda a, b: (_raw(a, b, _NN), b),
             lambda b, g: (_raw(g, b, _NT), jnp.zeros_like(b)))


@jax.custom_vjp
def mm_nt(a, b):
    return _raw(a, b, _NT)


mm_nt.defvjp(lambda a, b: (_raw(a, b, _NT), (a, b)),
             lambda r, g: (_raw(g, r[1], _NN), _raw(g, r[0], _TN)))


@jax.custom_vjp
def bmm_nn(a, b):
    return _raw(a, b, _BNN)


bmm_nn.defvjp(lambda a, b: (_raw(a, b, _BNN), (a, b)),
              lambda r, g: (_raw(g, r[1], _BNT), _raw(r[0], g, _BTN)))


@jax.custom_vjp
def bmm_nt(a, b):
    return _raw(a, b, _BNT)


bmm_nt.defvjp(lambda a, b: (_raw(a, b, _BNT), (a, b)),
              lambda r, g: (_raw(g, r[1], _BNN), _raw(g, r[0], _BTN)))


@jax.custom_vjp
def softplus(x):
    t = jnp.exp(-jnp.abs(x))
    u = 1.0 + t
    one = u == 1.0
    l1p = jnp.where(one, t, jnp.log(u) * (t / jnp.where(one, 1.0, u - 1.0)))
    return jnp.maximum(x, 0.0) + l1p


softplus.defvjp(lambda x: (softplus(x), x), lambda x, g: (g * _sig(x),))


def _params(*sem):
    return pltpu.CompilerParams(dimension_semantics=sem, vmem_limit_bytes=56 * 1024 * 1024)


def all_to_all(arrs, bcast, name):
    n = len(arrs)
    out_shapes = [jax.ShapeDtypeStruct(((NDEV,) + a.shape) if b else a.shape, a.dtype) for a, b in zip(arrs, bcast)]

    def body(*refs):
        ins, outs = refs[:n], refs[n:2 * n]
        send_sems, recv_sems, local_sems = refs[2 * n:]
        x, y, c = lax.axis_index("x"), lax.axis_index("y"), lax.axis_index("c")
        me = 4 * x + 2 * y + c
        local = []
        for j in range(n):
            cp = pltpu.make_async_copy(ins[j] if bcast[j] else ins[j].at[me], outs[j].at[me], local_sems.at[j])
            cp.start()
            local.append(cp)
        remote = []
        for k in range(1, NDEV):
            px, py, pc = x ^ ((k >> 2) & 1), y ^ ((k >> 1) & 1), c ^ (k & 1)
            peer = 4 * px + 2 * py + pc
            for j in range(n):
                s = (k - 1) * n + j
                cp = pltpu.make_async_remote_copy(
                    src_ref=ins[j] if bcast[j] else ins[j].at[peer], dst_ref=outs[j].at[me],
                    send_sem=send_sems.at[s], recv_sem=recv_sems.at[s],
                    device_id=(px, py, pc), device_id_type=pl.DeviceIdType.MESH)
                cp.start()
                remote.append(cp)
        for cp in remote:
            cp.wait()
        for cp in local:
            cp.wait()

    any_spec = pl.BlockSpec(memory_space=pl.ANY)
    return pl.pallas_call(
        body, name=name, out_shape=out_shapes, in_specs=[any_spec] * n, out_specs=[any_spec] * n,
        scratch_shapes=[pltpu.SemaphoreType.DMA((7 * n,)), pltpu.SemaphoreType.DMA((7 * n,)),
                        pltpu.SemaphoreType.DMA((n,))],
        compiler_params=pltpu.CompilerParams(has_side_effects=True),
    )(*arrs)


def norm_proj_fwd(x, vec, w, name):
    s, n = x.shape[0], w.shape[1]
    tr, tn = _pick(s, 512), _pick(n, 2560)

    def body(x_ref, v_ref, w_ref, o_ref, h_ref, h_scr):
        @pl.when(pl.program_id(1) == 0)
        def _():
            h = _rms(x_ref[...], v_ref[0:1, :], D) * (1.0 + v_ref[2:3, :]) + v_ref[1:2, :]
            h_scr[...] = h.astype(BF16)
            h_ref[...] = h.astype(BF16)
        o_ref[...] = jnp.dot(h_scr[...], w_ref[...], preferred_element_type=F32)

    return pl.pallas_call(
        body, name=name, grid=(s // tr, n // tn),
        in_specs=[pl.BlockSpec((tr, D), lambda i, j: (i, 0)), pl.BlockSpec((8, D), lambda i, j: (0, 0)),
                  pl.BlockSpec((D, tn), lambda i, j: (0, j))],
        out_specs=[pl.BlockSpec((tr, tn), lambda i, j: (i, j)), pl.BlockSpec((tr, D), lambda i, j: (i, 0))],
        out_shape=[jax.ShapeDtypeStruct((s, n), F32), jax.ShapeDtypeStruct((s, D), BF16)],
        scratch_shapes=[pltpu.VMEM((tr, D), BF16)],
        compiler_params=_params("arbitrary", "arbitrary"),
    )(x, vec, w)


def norm_proj_bwd(x, vec, dp, w, dx_in, aux, name):
    s, n = dp.shape
    tr, tk = _pick(s, 512), _pick(n, 2560)
    nk, has_aux = n // tk, aux is not None

    def body(*refs):
        if has_aux:
            x_ref, v_ref, dp_ref, w_ref, dxin_ref, aux_ref, dx_ref, dv_ref, acc = refs
        else:
            x_ref, v_ref, dp_ref, w_ref, dxin_ref, dx_ref, dv_ref, acc = refs
        i, k = pl.program_id(0), pl.program_id(1)

        @pl.when(k == 0)
        def _():
            acc[...] = jnp.zeros_like(acc)

        acc[...] += _raw(dp_ref[...], w_ref[...], _NT)

        @pl.when(k == nk - 1)
        def _():
            f = lambda xx, nw, sh, sc: _rms(xx, nw, D) * (1.0 + sc) + sh
            _, vjp = jax.vjp(f, x_ref[...], v_ref[0:1, :], v_ref[1:2, :], v_ref[2:3, :])
            dx, dnw, dsh, dsc = vjp(acc[...])
            dx_ref[...] = dxin_ref[...] + dx

            @pl.when(i == 0)
            def _():
                dv_ref[...] = jnp.zeros_like(dv_ref)

            dv_ref[0:1, :] += dnw
            dv_ref[1:2, :] += dsh
            dv_ref[2:3, :] += dsc
            if has_aux:
                dv_ref[3:4, :] += jnp.sum(dxin_ref[...] * aux_ref[...], axis=0, keepdims=True)

    row = pl.BlockSpec((tr, D), lambda i, k: (i, 0))
    in_specs = [row, pl.BlockSpec((8, D), lambda i, k: (0, 0)), pl.BlockSpec((tr, tk), lambda i, k: (i, k)),
                pl.BlockSpec((D, tk), lambda i, k: (0, k)), row] + ([row] if has_aux else [])
    args = [x, vec, dp, w, dx_in] + ([aux] if has_aux else [])
    return pl.pallas_call(
        body, name=name, grid=(s // tr, nk), in_specs=in_specs,
        out_specs=[row, pl.BlockSpec((8, D), lambda i, k: (0, 0))],
        out_shape=[jax.ShapeDtypeStruct((s, D), F32), jax.ShapeDtypeStruct((8, D), F32)],
        scratch_shapes=[pltpu.VMEM((tr, D), F32)],
        compiler_params=_params("arbitrary", "arbitrary"),
    )(*args)


def tn_matmul(a, b, name, scale=None):
    s, m = a.shape
    n = b.shape[1]
    ts, tm, tn = _pick(s, 512, 16), _pick(m, 1408), _pick(n, 2560)
    ns, has_scale = s // ts, scale is not None

    def body(*refs):
        if has_scale:
            a_ref, b_ref, sc_ref, o_ref = refs
        else:
            a_ref, b_ref, o_ref = refs
        k = pl.program_id(2)

        @pl.when(k == 0)
        def _():
            o_ref[...] = jnp.zeros_like(o_ref)

        o_ref[...] += _raw(a_ref[...], b_ref[...], _TN)
        if has_scale:
            @pl.when(k == ns - 1)
            def _():
                o_ref[...] = o_ref[...] * sc_ref[...]

    in_specs = [pl.BlockSpec((ts, tm), lambda i, j, k: (k, i)), pl.BlockSpec((ts, tn), lambda i, j, k: (k, j))]
    if has_scale:
        in_specs.append(pl.BlockSpec((1, tn), lambda i, j, k: (0, j)))
    return pl.pallas_call(
        body, name=name, grid=(m // tm, n // tn, ns), in_specs=in_specs,
        out_specs=pl.BlockSpec((tm, tn), lambda i, j, k: (i, j)),
        out_shape=jax.ShapeDtypeStruct((m, n), F32),
        compiler_params=_params("arbitrary", "arbitrary", "arbitrary"),
    )(*([a, b] + ([scale] if has_scale else [])))


def ada_mod(c16, w):
    ncol = w.shape[2]

    def body(c_ref, w_ref, o_ref, a_ref):
        cc = c_ref[...]
        act = cc * _sig(cc)
        a_ref[...] = act
        o_ref[...] = _raw(act, w_ref[...], _NN)

    return pl.pallas_call(
        body, name="ada_mod", grid=(LAYERS,),
        in_specs=[pl.BlockSpec((16, D), lambda l: (0, 0)), pl.BlockSpec((None, D, ncol), lambda l: (l, 0, 0))],
        out_specs=[pl.BlockSpec((None, 16, ncol), lambda l: (l, 0, 0)), pl.BlockSpec((16, D), lambda l: (0, 0))],
        out_shape=[jax.ShapeDtypeStruct((LAYERS, 16, ncol), F32), jax.ShapeDtypeStruct((16, D), F32)],
        compiler_params=_params("arbitrary"),
    )(c16, w)


def _mla_head(q_lat, c_kv, kr, krs, wqn, wqr, wqrs, wkn, wv, qa_w, kva_w, qn_w, qr_w, qrs_w, kn_w, kr_w, krs_w,
              cos2, sin2):
    qn = _rms(q_lat, qa_w, 384.0)
    kvn = _rms(c_kv, kva_w, 256.0)
    qnope = _rms(mm_nn(qn, wqn), qn_w, 64.0)
    qr, qrs = mm_nn(qn, wqr), mm_nn(qn, wqrs)
    rq = lax.rsqrt(jnp.sum(qr * qr, axis=-1, keepdims=True) / 32.0 + EPS)
    qrope = rq * (qr * qr_w * cos2 + qrs * qrs_w * sin2)
    knope = _rms(mm_nn(kvn, wkn), kn_w, 64.0)
    v = mm_nn(kvn, wv)
    rk = lax.rsqrt(jnp.sum(kr * kr, axis=-1, keepdims=True) / 32.0 + EPS)
    krope = rk * (kr * kr_w * cos2 + krs * krs_w * sin2)
    return qnope, qrope, knope, krope, v


def _mla_vec_pieces(v_ref):
    return (v_ref[0:1, 0:384], v_ref[1:2, 0:256], v_ref[2:3, 0:128], v_ref[2:3, 128:256], v_ref[2:3, 256:384],
            v_ref[3:4, 0:128], v_ref[3:4, 128:256], v_ref[3:4, 256:384])


def _mla_in_specs(tr):
    return [pl.BlockSpec((tr, 384), lambda i: (i, O_QL // 384)), pl.BlockSpec((tr, 256), lambda i: (i, O_CKV // 256)),
            pl.BlockSpec((tr, 128), lambda i: (i, O_KR // 128)), pl.BlockSpec((tr, 128), lambda i: (i, O_KRS // 128)),
            pl.BlockSpec((HEADS, 384, 384), lambda i: (0, 0, 0), **CONST),
            pl.BlockSpec((HEADS, 256, 256), lambda i: (0, 0, 0), **CONST),
            pl.BlockSpec((8, 512), lambda i: (0, 0)),
            pl.BlockSpec((tr, 128), lambda i: (i, 0)), pl.BlockSpec((tr, 128), lambda i: (i, 0))]


def mla_pre_fwd(proj, wq, wkv, vec, cos2, sin2, name):
    s = proj.shape[0]
    tr = _pick(s, 256)

    def body(ql_ref, ckv_ref, kr_ref, krs_ref, wq_ref, wkv_ref, v_ref, cos_ref, sin_ref, q_out, k_out, v_out):
        acts = (ql_ref[...], ckv_ref[...], kr_ref[...], krs_ref[...])
        vp = _mla_vec_pieces(v_ref)
        for h in range(HEADS):
            ws = (wq_ref[h, :, 0:128], wq_ref[h, :, 128:256], wq_ref[h, :, 256:384],
                  wkv_ref[h, :, 0:128], wkv_ref[h, :, 128:256])
            qn, qr, kn, krp, v = _mla_head(*acts, *ws, *vp, cos_ref[...], sin_ref[...])
            q_out[h, :, 0:128] = qn.astype(BF16)
            q_out[h, :, 128:256] = qr.astype(BF16)
            k_out[h, :, 0:128] = kn.astype(BF16)
            k_out[h, :, 128:256] = krp.astype(BF16)
            v_out[h] = v.astype(BF16)

    return pl.pallas_call(
        body, name=name, grid=(s // tr,), in_specs=_mla_in_specs(tr),
        out_specs=[pl.BlockSpec((HEADS, tr, 256), lambda i: (0, i, 0)), pl.BlockSpec((HEADS, tr, 256), lambda i: (0, i, 0)),
                   pl.BlockSpec((HEADS, tr, 128), lambda i: (0, i, 0))],
        out_shape=[jax.ShapeDtypeStruct((HEADS, s, 256), BF16), jax.ShapeDtypeStruct((HEADS, s, 256), BF16),
                   jax.ShapeDtypeStruct((HEADS, s, 128), BF16)],
        compiler_params=_params("arbitrary"),
    )(proj, proj, proj, proj, wq, wkv, vec, cos2, sin2)


def mla_pre_bwd(proj, wq, wkv, vec, cos2, sin2, dq, dk, dv, name):
    s = proj.shape[0]
    tr = _pick(s, 256)

    def body(ql_ref, ckv_ref, kr_ref, krs_ref, wq_ref, wkv_ref, v_ref, cos_ref, sin_ref, dq_ref, dk_ref, dv_ref,
             dql_out, dckv_out, dkr_out, dkrs_out, dwq_out, dwkv_out, dvec_out):
        @pl.when(pl.program_id(0) == 0)
        def _():
            dwq_out[...] = jnp.zeros_like(dwq_out)
            dwkv_out[...] = jnp.zeros_like(dwkv_out)
            dvec_out[...] = jnp.zeros_like(dvec_out)

        acts = (ql_ref[...], ckv_ref[...], kr_ref[...], krs_ref[...])
        vp = _mla_vec_pieces(v_ref)
        cos2_, sin2_ = cos_ref[...], sin_ref[...]

        def head(h, carry):
            wq_h, wkv_h = wq_ref[h].astype(F32), wkv_ref[h].astype(F32)
            ws = (wq_h[:, 0:128], wq_h[:, 128:256], wq_h[:, 256:384], wkv_h[:, 0:128], wkv_h[:, 128:256])
            f = lambda *a: _mla_head(*a, cos2_, sin2_)
            _, vjp = jax.vjp(f, *acts, *ws, *vp)
            dq_h, dk_h = dq_ref[h], dk_ref[h]
            g = vjp((dq_h[:, 0:128], dq_h[:, 128:256], dk_h[:, 0:128], dk_h[:, 128:256], dv_ref[h]))
            dwq_out[h, :, 0:128] += g[4]
            dwq_out[h, :, 128:256] += g[5]
            dwq_out[h, :, 256:384] += g[6]
            dwkv_out[h, :, 0:128] += g[7]
            dwkv_out[h, :, 128:256] += g[8]
            dvec_out[0:1, 0:384] += g[9]
            dvec_out[1:2, 0:256] += g[10]
            dvec_out[2:3, 0:128] += g[11]
            dvec_out[2:3, 128:256] += g[12]
            dvec_out[2:3, 256:384] += g[13]
            dvec_out[3:4, 0:128] += g[14]
            dvec_out[3:4, 128:256] += g[15]
            dvec_out[3:4, 256:384] += g[16]
            return tuple(c + gg for c, gg in zip(carry, g[:4]))

        tot = lax.fori_loop(0, HEADS, head, tuple(jnp.zeros_like(a) for a in acts))
        dql_out[...] = tot[0].astype(BF16)
        dckv_out[...] = tot[1].astype(BF16)
        dkr_out[...] = tot[2].astype(BF16)
        dkrs_out[...] = tot[3].astype(BF16)

    hb = lambda w: pl.BlockSpec((HEADS, tr, w), lambda i: (0, i, 0))
    return pl.pallas_call(
        body, name=name, grid=(s // tr,), in_specs=_mla_in_specs(tr) + [hb(256), hb(256), hb(128)],
        out_specs=[pl.BlockSpec((tr, 384), lambda i: (i, 0)), pl.BlockSpec((tr, 256), lambda i: (i, 0)),
                   pl.BlockSpec((tr, 128), lambda i: (i, 0)), pl.BlockSpec((tr, 128), lambda i: (i, 0)),
                   pl.BlockSpec((HEADS, 384, 384), lambda i: (0, 0, 0)), pl.BlockSpec((HEADS, 256, 256), lambda i: (0, 0, 0)),
                   pl.BlockSpec((8, 512), lambda i: (0, 0))],
        out_shape=[jax.ShapeDtypeStruct((s, 384), BF16), jax.ShapeDtypeStruct((s, 256), BF16),
                   jax.ShapeDtypeStruct((s, 128), BF16), jax.ShapeDtypeStruct((s, 128), BF16),
                   jax.ShapeDtypeStruct((HEADS, 384, 384), F32), jax.ShapeDtypeStruct((HEADS, 256, 256), F32),
                   jax.ShapeDtypeStruct((8, 512), F32)],
        compiler_params=_params("arbitrary"),
    )(proj, proj, proj, proj, wq, wkv, vec, cos2, sin2, dq, dk, dv)


def _att_probs(q, kk, i, tq):
    sc = _raw(q, kk, _NT) * ATT_SCALE
    rows = lax.broadcasted_iota(jnp.int32, sc.shape, 0) + i * tq
    cols = lax.broadcasted_iota(jnp.int32, sc.shape, 1)
    sc = jnp.where(cols <= rows, sc, -jnp.inf)
    e = jnp.exp(sc - jnp.max(sc, axis=-1, keepdims=True))
    return e / jnp.sum(e, axis=-1, keepdims=True)


def mla_attn_fwd(q, k, v, name):
    s = q.shape[1]
    tq = _pick(s, 256)

    def body(q_ref, k_ref, v_ref, o_ref):
        for i in range(s // tq):
            n = (i + 1) * tq
            p = _att_probs(q_ref[i * tq:n, :], k_ref[0:n, :], i, tq)
            o_ref[i * tq:n, :] = _raw(p, v_ref[0:n, :], _NN)

    hs = lambda w: pl.BlockSpec((None, s, w), lambda h: (h, 0, 0))
    return pl.pallas_call(
        body, name=name, grid=(HEADS,), in_specs=[hs(256), hs(256), hs(128)],
        out_specs=pl.BlockSpec((s, 128), lambda h: (0, h)),
        out_shape=jax.ShapeDtypeStruct((s, HEADS * 128), F32),
        compiler_params=_params("arbitrary"),
    )(q, k, v)


def mla_attn_bwd(q, k, v, do, name):
    s = q.shape[1]
    tq = _pick(s, 256)

    def body(q_ref, k_ref, v_ref, do_ref, dq_ref, dk_ref, dv_ref):
        dk_ref[...] = jnp.zeros_like(dk_ref)
        dv_ref[...] = jnp.zeros_like(dv_ref)
        for i in range(s // tq):
            n = (i + 1) * tq
            qq, kk, vv = q_ref[i * tq:n, :], k_ref[0:n, :], v_ref[0:n, :]
            p = _att_probs(qq, kk, i, tq)
            o = _raw(p, vv, _NN)
            dout = do_ref[i * tq:n, :]
            delta = jnp.sum(dout * o, axis=-1, keepdims=True)
            dp = _raw(dout, vv, _NT)
            ds = p * (dp - delta) * ATT_SCALE
            dq_ref[i * tq:n, :] = _raw(ds, kk, _NN)
            dk_ref[0:n, :] += _raw(ds, qq, _TN)
            dv_ref[0:n, :] += _raw(p, dout, _TN)

    hs = lambda w: pl.BlockSpec((None, s, w), lambda h: (h, 0, 0))
    return pl.pallas_call(
        body, name=name, grid=(HEADS,),
        in_specs=[hs(256), hs(256), hs(128), pl.BlockSpec((s, 128), lambda h: (0, h))],
        out_specs=[hs(256), hs(256), hs(128)],
        out_shape=[jax.ShapeDtypeStruct((HEADS, s, 256), F32), jax.ShapeDtypeStruct((HEADS, s, 256), F32),
                   jax.ShapeDtypeStruct((HEADS, s, 128), F32)],
        compiler_params=_params("arbitrary"),
    )(q, k, v, do)


def _pool_windows(u, pad, s, g):
    pad[0:16, :] = jnp.zeros((16, 128), F32)
    cur, sel = u, None
    for j, k in enumerate((1, 2, 4, 8)):
        pad[16:16 + s, :] = cur
        cur = cur + pad[16 - k:16 - k + s, :]
        sel = cur if sel is None else jnp.where(g == j, cur, sel)
    return sel


def _pool_count(s, g):
    t = lax.broadcasted_iota(jnp.int32, (s, 1), 0)
    return jnp.minimum(t + 1, 2 << g).astype(F32)


def pool_fwd(proj, pw, ps, name):
    s = proj.shape[0]

    def body(u_ref, w_ref, s_ref, o_ref, pad):
        g = pl.program_id(0)
        u = u_ref[...]
        pooled = _pool_windows(u, pad, s, g) / _pool_count(s, g) - u
        o_ref[...] = _raw(pooled, w_ref[...], _NN) * s_ref[...]

    return pl.pallas_call(
        body, name=name, grid=(4,),
        in_specs=[pl.BlockSpec((s, 128), lambda g: (0, O_PU // 128 + g)), pl.BlockSpec((None, 128, 128), lambda g: (g, 0, 0)),
                  pl.BlockSpec((1, 128), lambda g: (0, g))],
        out_specs=pl.BlockSpec((s, 128), lambda g: (0, g)),
        out_shape=jax.ShapeDtypeStruct((s, 512), F32),
        scratch_shapes=[pltpu.VMEM((s + 16, 128), F32)],
        compiler_params=_params("arbitrary"),
    )(proj, pw, ps)


def pool_bwd(proj, pw, ps, do, name):
    s = proj.shape[0]

    def body(u_ref, w_ref, s_ref, do_ref, du_ref, dw_ref, ds_ref, pad):
        g = pl.program_id(0)
        u, w, dout = u_ref[...], w_ref[...], do_ref[...]
        cnt = _pool_count(s, g)
        pooled = _pool_windows(u, pad, s, g) / cnt - u
        mixed = _raw(pooled, w, _NN)
        ds_ref[...] = jnp.sum(dout * mixed, axis=0, keepdims=True)
        dmixed = dout * s_ref[...]
        dw_ref[...] = _raw(pooled, dmixed, _TN)
        dpooled = _raw(dmixed, w, _NT)
        dsel = dpooled / cnt
        pad[s:s + 16, :] = jnp.zeros((16, 128), F32)
        cur = jnp.where(g == 3, dsel, 0.0)
        for j, k in ((2, 8), (1, 4), (0, 2)):
            pad[0:s, :] = cur
            cur = cur + pad[k:k + s, :] + jnp.where(g == j, dsel, 0.0)
        pad[0:s, :] = cur
        cur = cur + pad[1:1 + s, :]
        du_ref[...] = (cur - dpooled).astype(BF16)

    return pl.pallas_call(
        body, name=name, grid=(4,),
        in_specs=[pl.BlockSpec((s, 128), lambda g: (0, O_PU // 128 + g)), pl.BlockSpec((None, 128, 128), lambda g: (g, 0, 0)),
                  pl.BlockSpec((1, 128), lambda g: (0, g)), pl.BlockSpec((s, 128), lambda g: (0, g))],
        out_specs=[pl.BlockSpec((s, 128), lambda g: (0, g)), pl.BlockSpec((None, 128, 128), lambda g: (g, 0, 0)),
                   pl.BlockSpec((1, 128), lambda g: (0, g))],
        out_shape=[jax.ShapeDtypeStruct((s, 512), BF16), jax.ShapeDtypeStruct((4, 128, 128), F32),
                   jax.ShapeDtypeStruct((1, 512), F32)],
        scratch_shapes=[pltpu.VMEM((s + 16, 128), F32)],
        compiler_params=_params("arbitrary"),
    )(proj, pw, ps, do)


def _xbc_col(i):
    return jnp.where(i < 2, O_XS // 512 + i, O_BC // 512)


def conv_fwd(proj, cw, cb, name):
    s = proj.shape[0]

    def body(x_ref, w_ref, b_ref, o_ref, pad):
        pad[0:8, :] = jnp.zeros((8, 512), F32)
        pad[8:8 + s, :] = x_ref[...]
        y = b_ref[...] + sum(w_ref[k:k + 1, :] * pad[5 + k:5 + k + s, :] for k in range(4))
        o_ref[...] = y * _sig(y)

    return pl.pallas_call(
        body, name=name, grid=(3,),
        in_specs=[pl.BlockSpec((s, 512), lambda i: (0, _xbc_col(i))), pl.BlockSpec((4, 512), lambda i: (0, i)),
                  pl.BlockSpec((1, 512), lambda i: (0, i))],
        out_specs=pl.BlockSpec((s, 512), lambda i: (0, i)),
        out_shape=jax.ShapeDtypeStruct((s, 1536), F32),
        scratch_shapes=[pltpu.VMEM((s + 8, 512), F32)],
        compiler_params=_params("arbitrary"),
    )(proj, cw, cb)


def conv_bwd(proj, cw, cb, dact, name):
    s = proj.shape[0]

    def body(x_ref, w_ref, b_ref, da_ref, dx_ref, dw_ref, db_ref, pad, pad2):
        pad[0:8, :] = jnp.zeros((8, 512), F32)
        pad[8:8 + s, :] = x_ref[...]
        y = b_ref[...] + sum(w_ref[k:k + 1, :] * pad[5 + k:5 + k + s, :] for k in range(4))
        sg = _sig(y)
        dy = da_ref[...] * (sg * (1.0 + y * (1.0 - sg)))
        db_ref[...] = jnp.sum(dy, axis=0, keepdims=True)
        for k in range(4):
            dw_ref[k:k + 1, :] = jnp.sum(dy * pad[5 + k:5 + k + s, :], axis=0, keepdims=True)
        pad2[s:s + 8, :] = jnp.zeros((8, 512), F32)
        pad2[0:s, :] = dy
        dx_ref[...] = sum(w_ref[k:k + 1, :] * pad2[3 - k:3 - k + s, :] for k in range(4)).astype(BF16)

    return pl.pallas_call(
        body, name=name, grid=(3,),
        in_specs=[pl.BlockSpec((s, 512), lambda i: (0, _xbc_col(i))), pl.BlockSpec((4, 512), lambda i: (0, i)),
                  pl.BlockSpec((1, 512), lambda i: (0, i)), pl.BlockSpec((s, 512), lambda i: (0, i))],
        out_specs=[pl.BlockSpec((s, 512), lambda i: (0, i)), pl.BlockSpec((4, 512), lambda i: (0, i)),
                   pl.BlockSpec((1, 512), lambda i: (0, i))],
        out_shape=[jax.ShapeDtypeStruct((s, 1536), BF16), jax.ShapeDtypeStruct((4, 1536), F32),
                   jax.ShapeDtypeStruct((1, 1536), F32)],
        scratch_shapes=[pltpu.VMEM((s + 8, 512), F32), pltpu.VMEM((s + 8, 512), F32)],
        compiler_params=_params("arbitrary"),
    )(proj, cw, cb, dact)


def _ssd_chunk(xt, dtr, dtc, bm, cm, hprev, alog, dbias, dskip):
    ln = 128
    a = -jnp.exp(alog)
    dt_r = softplus(dtr + dbias)
    da_r = dt_r * a
    da_c = softplus(dtc + dbias) * a
    li = lax.broadcasted_iota(jnp.int32, (1, ln, ln), 1)
    si = lax.broadcasted_iota(jnp.int32, (1, ln, ln), 2)
    causal = si <= li
    acs_c = jnp.sum(jnp.where(causal, da_r, 0.0), axis=2, keepdims=True)
    acs_r = jnp.sum(jnp.where(li <= si, da_c, 0.0), axis=1, keepdims=True)
    acs_last = jnp.sum(da_r, axis=2, keepdims=True)
    decay = jnp.exp(jnp.where(causal, acs_c - acs_r, -jnp.inf))
    m = mm_nt(cm, bm)[None] * decay
    xdt = xt * dt_r
    y_diag = bmm_nt(xdt, m)
    bb = jnp.broadcast_to(bm[None], (8, ln, ln))
    cc = jnp.broadcast_to(cm[None], (8, ln, ln))
    states = bmm_nn(xdt * jnp.exp(acs_last - acs_r), bb)
    y_off = bmm_nt(hprev, cc) * jnp.exp(acs_r)
    hnew = hprev * jnp.exp(acs_last) + states
    return y_diag + y_off + xt * dskip, hnew


def _ssd_specs(nc, rev):
    cix = (lambda c: nc - 1 - c) if rev else (lambda c: c)
    hv = pl.BlockSpec((8, 1, 1), lambda g, c: (g, 0, 0))
    return [pl.BlockSpec((8, 64, 128), lambda g, c: (g, 0, cix(c))), pl.BlockSpec((8, 1, 128), lambda g, c: (g, 0, cix(c))),
            pl.BlockSpec((8, 128, 1), lambda g, c: (g, cix(c), 0)), pl.BlockSpec((128, 128), lambda g, c: (cix(c), 8 + g)),
            pl.BlockSpec((128, 128), lambda g, c: (cix(c), 10 + g))], hv, cix


def ssd_fwd(xt, dtr, dtc, xbc, alog, dbias, dskip, name):
    s = xt.shape[2]
    nc = s // 128
    specs, hv, _ = _ssd_specs(nc, False)

    def body(x_ref, dr_ref, dc_ref, b_ref, c_ref, al_ref, db_ref, dk_ref, y_ref, hs_ref, h_scr):
        @pl.when(pl.program_id(1) == 0)
        def _():
            h_scr[...] = jnp.zeros_like(h_scr)
        hp = h_scr[...]
        hs_ref[...] = hp
        y, hn = _ssd_chunk(x_ref[...], dr_ref[...], dc_ref[...], b_ref[...], c_ref[...], hp,
                           al_ref[...], db_ref[...], dk_ref[...])
        y_ref[...] = y
        h_scr[...] = hn

    return pl.pallas_call(
        body, name=name, grid=(2, nc), in_specs=specs + [hv, hv, hv],
        out_specs=[pl.BlockSpec((8, 64, 128), lambda g, c: (g, 0, c)),
                   pl.BlockSpec((None, None, 8, 64, 128), lambda g, c: (g, c, 0, 0, 0))],
        out_shape=[jax.ShapeDtypeStruct((16, 64, s), F32), jax.ShapeDtypeStruct((2, nc, 8, 64, 128), F32)],
        scratch_shapes=[pltpu.VMEM((8, 64, 128), F32)],
        compiler_params=_params("arbitrary", "arbitrary"),
    )(xt, dtr, dtc, xbc, xbc, alog, dbias, dskip)


def ssd_bwd(xt, dtr, dtc, xbc, alog, dbias, dskip, hs, dyt, name):
    s = xt.shape[2]
    nc = s // 128
    specs, hv, cix = _ssd_specs(nc, True)

    def body(x_ref, dr_ref, dc_ref, b_ref, c_ref, al_ref, db_ref, dk_ref, hs_ref, dy_ref,
             dx_out, ddr_out, ddc_out, dbm_out, dcm_out, dal_out, ddb_out, ddk_out, dh_scr):
        @pl.when(pl.program_id(1) == 0)
        def _():
            dh_scr[...] = jnp.zeros_like(dh_scr)
            dal_out[...] = jnp.zeros_like(dal_out)
            ddb_out[...] = jnp.zeros_like(ddb_out)
            ddk_out[...] = jnp.zeros_like(ddk_out)
        _, vjp = jax.vjp(_ssd_chunk, x_ref[...], dr_ref[...], dc_ref[...], b_ref[...], c_ref[...], hs_ref[...],
                         al_ref[...], db_ref[...], dk_ref[...])
        g = vjp((dy_ref[...], dh_scr[...]))
        dx_out[...] = g[0]
        ddr_out[...] = g[1]
        ddc_out[...] = g[2]
        dbm_out[...] = g[3]
        dcm_out[...] = g[4]
        dh_scr[...] = g[5]
        dal_out[...] += g[6]
        ddb_out[...] += g[7]
        ddk_out[...] += g[8]

    return pl.pallas_call(
        body, name=name, grid=(2, nc),
        in_specs=specs + [hv, hv, hv, pl.BlockSpec((None, None, 8, 64, 128), lambda g, c: (g, cix(c), 0, 0, 0)),
                          pl.BlockSpec((8, 64, 128), lambda g, c: (g, 0, cix(c)))],
        out_specs=[pl.BlockSpec((8, 64, 128), lambda g, c: (g, 0, cix(c))), pl.BlockSpec((8, 1, 128), lambda g, c: (g, 0, cix(c))),
                   pl.BlockSpec((8, 128, 1), lambda g, c: (g, cix(c), 0)), pl.BlockSpec((128, 128), lambda g, c: (cix(c), g)),
                   pl.BlockSpec((128, 128), lambda g, c: (cix(c), g)), hv, hv, hv],
        out_shape=[jax.ShapeDtypeStruct((16, 64, s), F32), jax.ShapeDtypeStruct((16, 1, s), F32),
                   jax.ShapeDtypeStruct((16, s, 1), F32), jax.ShapeDtypeStruct((s, 256), F32),
                   jax.ShapeDtypeStruct((s, 256), F32)] + [jax.ShapeDtypeStruct((16, 1, 1), F32)] * 3,
        scratch_shapes=[pltpu.VMEM((8, 64, 128), F32)],
        compiler_params=_params("arbitrary", "arbitrary"),
    )(xt, dtr, dtc, xbc, xbc, alog, dbias, dskip, hs, dyt)


def _merge(oa, ob, y, z, gla, glb, glc, x, g1, nw, ea, eb, ec, eo, wba, wbb, wbc, wout):
    gated = y * (z * _sig(z))
    sq = gated * gated
    left = lax.broadcasted_iota(jnp.int32, (1, D), 1) < 512
    ms0 = jnp.sum(jnp.where(left, sq, 0.0), axis=-1, keepdims=True) / 512.0
    ms1 = jnp.sum(jnp.where(left, 0.0, sq), axis=-1, keepdims=True) / 512.0
    oc = gated * jnp.where(left, lax.rsqrt(ms0 + EPS), lax.rsqrt(ms1 + EPS)) * nw
    ya, yb, yc = mm_nc(oa, wba) + ea, mm_nc(ob, wbb) + eb, mm_nc(oc, wbc) + ec
    merged = _sig(gla) * ya + _sig(glb) * yb + _sig(glc) * yc
    x1 = x + g1 * (mm_nc(merged, wout) + eo)
    return x1, (oc, merged)


def _merge_specs(tr):
    row = lambda w: pl.BlockSpec((tr, w), lambda i: (i, 0))
    acts = [row(D), row(512), row(D), pl.BlockSpec((tr, D), lambda i: (i, O_Z // D)),
            pl.BlockSpec((tr, 3 * D), lambda i: (i, 0)), row(D), pl.BlockSpec((8, D), lambda i: (0, 0))]
    cst = lambda r: pl.BlockSpec((r, D), lambda i: (0, 0), **CONST)
    return acts, [cst(D), cst(512), cst(D), cst(D)], row


def merge_fwd(oa, ob, y, proj, x, mvec, wba, wbb, wbc, wout, name):
    s = x.shape[0]
    tr = _pick(s, 256)
    acts, wts, row = _merge_specs(tr)

    def body(oa_ref, ob_ref, y_ref, z_ref, gl_ref, x_ref, mv_ref, wba_ref, wbb_ref, wbc_ref, wout_ref, o_ref):
        zero = jnp.zeros((1, D), F32)
        x1, _ = _merge(oa_ref[...], ob_ref[...], y_ref[...], z_ref[...], gl_ref[:, 0:D], gl_ref[:, D:2 * D],
                       gl_ref[:, 2 * D:3 * D], x_ref[...], mv_ref[0:1, :], mv_ref[1:2, :], zero, zero, zero, zero,
                       wba_ref[...], wbb_ref[...], wbc_ref[...], wout_ref[...])
        o_ref[...] = x1

    return pl.pallas_call(
        body, name=name, grid=(s // tr,), in_specs=acts + wts, out_specs=row(D),
        out_shape=jax.ShapeDtypeStruct((s, D), F32), compiler_params=_params("arbitrary"),
    )(oa, ob, y, proj, proj, x, mvec, wba, wbb, wbc, wout)


def merge_bwd(oa, ob, y, proj, x, mvec, wba, wbb, wbc, wout, dx1, name):
    s = x.shape[0]
    tr = _pick(s, 128)
    acts, wts, row = _merge_specs(tr)

    def body(oa_ref, ob_ref, y_ref, z_ref, gl_ref, x_ref, mv_ref, wba_ref, wbb_ref, wbc_ref, wout_ref, dx1_ref,
             doa_o, dob_o, dy_o, dz_o, dgl_o, dx_o, dmv_o, dya_o, dyb_o, dyc_o, dpre_o, oc_o, mg_o):
        zero = jnp.zeros((tr, D), F32)
        wts_ = (wba_ref[...], wbb_ref[...], wbc_ref[...], wout_ref[...])
        f = lambda *a: _merge(*a, *wts_)
        _, vjp, (oc, merged) = jax.vjp(
            f, oa_ref[...], ob_ref[...], y_ref[...], z_ref[...], gl_ref[:, 0:D], gl_ref[:, D:2 * D],
            gl_ref[:, 2 * D:3 * D], x_ref[...], mv_ref[0:1, :], mv_ref[1:2, :], zero, zero, zero, zero, has_aux=True)
        g = vjp(dx1_ref[...])
        doa_o[...] = g[0]
        dob_o[...] = g[1]
        dy_o[...] = g[2]
        dz_o[...] = g[3].astype(BF16)
        dgl_o[:, 0:D] = g[4].astype(BF16)
        dgl_o[:, D:2 * D] = g[5].astype(BF16)
        dgl_o[:, 2 * D:3 * D] = g[6].astype(BF16)
        dx_o[...] = g[7]

        @pl.when(pl.program_id(0) == 0)
        def _():
            dmv_o[...] = jnp.zeros_like(dmv_o)

        dmv_o[0:1, :] += g[8]
        dmv_o[1:2, :] += g[9]
        dya_o[...] = g[10].astype(BF16)
        dyb_o[...] = g[11].astype(BF16)
        dyc_o[...] = g[12].astype(BF16)
        dpre_o[...] = g[13].astype(BF16)
        oc_o[...] = oc.astype(BF16)
        mg_o[...] = merged.astype(BF16)

    sd = lambda w, dt: jax.ShapeDtypeStruct((s, w), dt)
    return pl.pallas_call(
        body, name=name, grid=(s // tr,), in_specs=acts + wts + [row(D)],
        out_specs=[row(D), row(512), row(D), row(D), row(3 * D), row(D), pl.BlockSpec((8, D), lambda i: (0, 0))] + [row(D)] * 6,
        out_shape=[sd(D, F32), sd(512, F32), sd(D, F32), sd(D, BF16), sd(3 * D, BF16), sd(D, F32),
                   jax.ShapeDtypeStruct((8, D), F32)] + [sd(D, BF16)] * 6,
        compiler_params=_params("arbitrary"),
    )(oa, ob, y, proj, proj, x, mvec, wba, wbb, wbc, wout, dx1)


def _conv3(u_scr, w_ref, first, rows):
    return sum(w_ref[k:k + 1, :] * u_scr[first + k:first + k + rows, :] for k in range(3))


def ffn_fwd(x1, fvec, wup, cw, cb, wdn, name):
    s = x1.shape[0]
    tr, tf = _pick(s, 512), FFN_TILE

    def body(x_ref, v_ref, wu_ref, cw_ref, cb_ref, wd_ref, x2_ref, h_ref, pre_ref, h_scr, u_scr, acc):
        i, t = pl.program_id(0), pl.program_id(1)

        @pl.when(t == 0)
        def _():
            @pl.when(i == 0)
            def _():
                h_scr[0:16, :] = jnp.zeros((16, D), BF16)

            @pl.when(i > 0)
            def _():
                h_scr[0:16, :] = h_scr[tr:tr + 16, :]

            h = (_rms(x_ref[...], v_ref[0:1, :], D) * (1.0 + v_ref[2:3, :]) + v_ref[1:2, :]).astype(BF16)
            h_scr[16:16 + tr, :] = h
            h_ref[...] = h
            acc[...] = jnp.zeros_like(acc)

        u_scr[...] = jnp.dot(h_scr[...], wu_ref[...], preferred_element_type=F32)
        cv = _conv3(u_scr, cw_ref, 14, tr) + cb_ref[...]
        cg, cval = cv[:, 0:tf], cv[:, tf:2 * tf]
        acc[...] += _raw(cg * _sig(cg) * cval, wd_ref[...], _NN)

        @pl.when(t == FFN_NT - 1)
        def _():
            pre_ref[...] = acc[...]
            x2_ref[...] = x_ref[...] + v_ref[3:4, :] * acc[...]

    row = pl.BlockSpec((tr, D), lambda i, t: (i, 0))
    return pl.pallas_call(
        body, name=name, grid=(s // tr, FFN_NT),
        in_specs=[row, pl.BlockSpec((8, D), lambda i, t: (0, 0)), pl.BlockSpec((D, 2 * tf), lambda i, t: (0, t)),
                  pl.BlockSpec((3, 2 * tf), lambda i, t: (0, t)), pl.BlockSpec((1, 2 * tf), lambda i, t: (0, t)),
                  pl.BlockSpec((tf, D), lambda i, t: (t, 0))],
        out_specs=[row, row, row],
        out_shape=[jax.ShapeDtypeStruct((s, D), F32), jax.ShapeDtypeStruct((s, D), BF16), jax.ShapeDtypeStruct((s, D), F32)],
        scratch_shapes=[pltpu.VMEM((tr + 16, D), BF16), pltpu.VMEM((tr + 16, 2 * tf), F32), pltpu.VMEM((tr, D), F32)],
        compiler_params=_params("arbitrary", "arbitrary"),
    )(x1, fvec, wup, cw, cb, wdn)


def ffn_bwd(h2, dx2, fvec, wup, cw, cb, wdn, name):
    s = h2.shape[0]
    tr, tf = _pick(s, 512), FFN_TILE
    ni, nb = s // tr, s // 16

    def body(hp_ref, hm_ref, hn_ref, dm_ref, dn_ref, v_ref, wu_ref, cw_ref, cb_ref, wd_ref,
             dup_ref, act_ref, dcw_ref, u_scr, dc_scr):
        i = pl.program_id(1)
        hfull = jnp.concatenate([jnp.where(i > 0, hp_ref[...], jnp.zeros((16, D), BF16)), hm_ref[...],
                                 jnp.where(i < ni - 1, hn_ref[...], jnp.zeros((16, D), BF16))], axis=0)
        u_scr[...] = jnp.dot(hfull, wu_ref[...], preferred_element_type=F32)
        cv = _conv3(u_scr, cw_ref, 14, tr + 16) + cb_ref[...]
        cg, cval = cv[:, 0:tf], cv[:, tf:2 * tf]
        g2 = v_ref[3:4, :]
        dpre = jnp.concatenate([dm_ref[...] * g2, jnp.where(i < ni - 1, dn_ref[...], 0.0) * g2], axis=0)
        dact = _raw(dpre, wd_ref[...], _NT)
        sg = _sig(cg)
        sl = cg * sg
        dc_scr[:, 0:tf] = dact * cval * (sg * (1.0 + cg * (1.0 - sg)))
        dc_scr[:, tf:2 * tf] = dact * sl
        dup_ref[...] = sum(cw_ref[k:k + 1, :] * dc_scr[2 - k:2 - k + tr, :] for k in range(3)).astype(BF16)
        act_ref[...] = (sl * cval)[0:tr, :].astype(BF16)

        @pl.when(i == 0)
        def _():
            dcw_ref[...] = jnp.zeros_like(dcw_ref)

        dcm = dc_scr[0:tr, :]
        for k in range(3):
            dcw_ref[k:k + 1, :] += jnp.sum(dcm * u_scr[14 + k:14 + k + tr, :], axis=0, keepdims=True)
        dcw_ref[3:4, :] += jnp.sum(dcm, axis=0, keepdims=True)

    r16 = tr // 16
    prev = lambda t, i: (jnp.maximum(i * r16 - 1, 0), 0)
    nxt = lambda t, i: (jnp.minimum((i + 1) * r16, nb - 1), 0)
    main = lambda t, i: (i, 0)
    return pl.pallas_call(
        body, name=name, grid=(FFN_NT, ni),
        in_specs=[pl.BlockSpec((16, D), prev), pl.BlockSpec((tr, D), main), pl.BlockSpec((16, D), nxt),
                  pl.BlockSpec((tr, D), main), pl.BlockSpec((16, D), nxt), pl.BlockSpec((8, D), lambda t, i: (0, 0)),
                  pl.BlockSpec((D, 2 * tf), lambda t, i: (0, t)), pl.BlockSpec((3, 2 * tf), lambda t, i: (0, t)),
                  pl.BlockSpec((1, 2 * tf), lambda t, i: (0, t)), pl.BlockSpec((tf, D), lambda t, i: (t, 0))],
        out_specs=[pl.BlockSpec((tr, 2 * tf), lambda t, i: (i, t)), pl.BlockSpec((tr, tf), lambda t, i: (i, t)),
                   pl.BlockSpec((8, 2 * tf), lambda t, i: (0, t))],
        out_shape=[jax.ShapeDtypeStruct((s, 2 * FFN), BF16), jax.ShapeDtypeStruct((s, FFN), BF16),
                   jax.ShapeDtypeStruct((8, 2 * FFN), F32)],
        scratch_shapes=[pltpu.VMEM((tr + 32, 2 * tf), F32), pltpu.VMEM((tr + 16, 2 * tf), F32)],
        compiler_params=_params("arbitrary", "arbitrary"),
    )(h2, h2, h2, dx2, dx2, fvec, wup, cw, cb, wdn)


def loss_head(y, target):
    s = y.shape[0]
    tr = _pick(s, 512)

    def body(y_ref, t_ref, dx_ref, l_ref):
        @pl.when(pl.program_id(0) == 0)
        def _():
            l_ref[...] = jnp.zeros_like(l_ref)
        err = y_ref[...] - t_ref[...]
        dx_ref[...] = err / float(D)
        l_ref[...] += 0.5 * jnp.sum(jnp.sum(err * err, axis=-1, keepdims=True) / float(D), axis=0, keepdims=True)

    row = pl.BlockSpec((tr, D), lambda i: (i, 0))
    return pl.pallas_call(
        body, name="loss_head", grid=(s // tr,), in_specs=[row, row],
        out_specs=[row, pl.BlockSpec((8, 128), lambda i: (0, 0))],
        out_shape=[jax.ShapeDtypeStruct((s, D), F32), jax.ShapeDtypeStruct((8, 128), F32)],
        compiler_params=_params("arbitrary"),
    )(y, target)


def adamw(parts, w, m, v, name):
    p, r, c = parts.shape
    tr = _pick(r, 256, 8)

    def body(p_ref, w_ref, m_ref, v_ref, g_out, d_out, m_out, v_out):
        g = p_ref[0].astype(F32)
        for q in range(1, p):
            g = g + p_ref[q].astype(F32)
        mn = B1 * m_ref[...] + (1.0 - B1) * g
        vn = B2 * v_ref[...] + (1.0 - B2) * (g * g)
        m_hat = mn / (1.0 - B1 ** STEP)
        v_hat = vn / (1.0 - B2 ** STEP)
        g_out[...] = g
        d_out[...] = -LR * (m_hat / (jnp.sqrt(v_hat) + ADAM_EPS) + WD * w_ref[...])
        m_out[...] = mn
        v_out[...] = vn

    row = pl.BlockSpec((tr, c), lambda i: (i, 0))
    return pl.pallas_call(
        body, name=name, grid=(r // tr,), in_specs=[pl.BlockSpec((p, tr, c), lambda i: (0, i, 0)), row, row, row],
        out_specs=[row] * 4, out_shape=[jax.ShapeDtypeStruct((r, c), F32)] * 4,
        compiler_params=_params("arbitrary"),
    )(parts, w, m, v)


def _padc(a, n):
    return jnp.pad(a, [(0, 0)] * (a.ndim - 1) + [(0, n - a.shape[-1])])


def _swap16(a):
    return jnp.concatenate([a[..., 16:32], a[..., 0:16]], axis=-1)


def _win_layout(w):
    kr = w[:, 640:672]
    return jnp.concatenate([w[:, 3760:6832], w[:, 2208:3232], w[:, 1184:2208], w[:, 672:1184], w[:, 3232:3744],
                            w[:, 384:640], _padc(kr, 128), _padc(_swap16(kr), 128), _padc(w[:, 3744:3760], 128),
                            jnp.zeros((w.shape[0], 128), w.dtype), w[:, 0:384]], axis=1)


def _win_unlayout(g):
    kr = g[:, O_KR:O_KR + 32] + _swap16(g[:, O_KRS:O_KRS + 32])
    return jnp.concatenate([g[:, O_QL:O_QL + 384], g[:, O_CKV:O_CKV + 256], kr, g[:, O_PU:O_PU + 512], g[:, O_Z:O_Z + D],
                            g[:, O_XS:O_XS + D], g[:, O_BC:O_BC + 512], g[:, O_DT:O_DT + 16], g[:, O_G:O_G + 3 * D]], axis=1)


def _wq_layout(w):
    w = w.reshape(384, HEADS, 96).transpose(1, 0, 2)
    rope = w[:, :, 64:96]
    return jnp.concatenate([_padc(w[:, :, 0:64], 128), _padc(rope, 128), _padc(_swap16(rope), 128)], axis=2)


def _wq_unlayout(g):
    rope = g[:, :, 128:160] + _swap16(g[:, :, 256:288])
    return jnp.concatenate([g[:, :, 0:64], rope], axis=2).transpose(1, 0, 2).reshape(384, HEADS * 96)


def _wkv_layout(w):
    w = w.reshape(256, HEADS, 128).transpose(1, 0, 2)
    return jnp.concatenate([_padc(w[:, :, 0:64], 128), _padc(w[:, :, 64:128], 128)], axis=2)


def _wkv_unlayout(g):
    return jnp.concatenate([g[:, :, 0:64], g[:, :, 128:192]], axis=2).transpose(1, 0, 2).reshape(256, HEADS * 128)


def _wba_layout(w):
    return jnp.pad(w.reshape(HEADS, 64, D), ((0, 0), (0, 64), (0, 0))).reshape(HEADS * 128, D)


def _ffn_cols(a):
    lead = a.shape[:-1]
    g = a[..., :FFN].reshape(lead + (FFN_NT, 1, FFN_TILE))
    v = a[..., FFN:].reshape(lead + (FFN_NT, 1, FFN_TILE))
    return jnp.concatenate([g, v], axis=-2).reshape(lead + (2 * FFN,))


def _ffn_uncols(a):
    lead = a.shape[:-1]
    a = a.reshape(lead + (FFN_NT, 2, FFN_TILE))
    return jnp.concatenate([a[..., 0, :].reshape(lead + (FFN,)), a[..., 1, :].reshape(lead + (FFN,))], axis=-1)


def _rows8(rows, width):
    out = jnp.stack([_padc(r.astype(F32), width) for r in rows])
    return jnp.pad(out, ((0, 8 - out.shape[0]), (0, 0)))


def _mla_vec(qa, kva, qn, kn):
    def row(n):
        return jnp.concatenate([_padc(n[0:64], 128), _padc(n[64:96], 128), _padc(_swap16(n[64:96]), 128)])
    return _rows8([qa, kva, row(qn), row(kn)], 512)


def _mla_unvec(g):
    def un(r):
        return jnp.concatenate([r[0:64], r[128:160] + _swap16(r[256:288])])
    return g[0, 0:384], g[1, 0:256], un(g[2]), un(g[3])


SMALL = (("ada_b", (6 * D,)), ("norm1_w", (D,)), ("q_a_norm", (384,)), ("kv_a_norm", (256,)), ("q_norm", (96,)),
         ("k_norm", (96,)), ("pool_w", (4, 128, 128)), ("pool_scale", (512,)), ("ssd_conv_b", (1536,)),
         ("ssd_dt_bias", (16,)), ("ssd_a_log", (16,)), ("ssd_d", (16,)), ("ssd_norm_w", (D,)), ("norm2_w", (D,)),
         ("ffn_conv_b", (2 * FFN,)), ("ssd_conv_w", (4, 1536)), ("ffn_conv_w", (3, 2 * FFN)))
SMALL_REPL = SMALL[:15]
SMALL_ROWS = 208


def _pack(per_layer, names):
    flat = jnp.concatenate([per_layer[l][n].reshape(-1).astype(F32) for l in range(LAYERS) for n, _ in names])
    return jnp.pad(flat, (0, SMALL_ROWS * D - flat.shape[0])).reshape(SMALL_ROWS, D)


def _unpack(packed, names):
    flat, out, off = packed.reshape(-1), {}, 0
    for l in range(LAYERS):
        for n, shp in names:
            size = math.prod(shp)
            out.setdefault(n, []).append(flat[off:off + size].reshape(shp))
            off += size
    return {n: jnp.stack(v) for n, v in out.items()}


BIG = ("w_in", "w_q_b", "w_kv_b", "w_branch", "w_out", "ffn_up", "ffn_down")
COL_SHARDED = ("w_in", "w_q_b", "w_kv_b", "ffn_up")


def _gathered_full(g, name):
    if name in COL_SHARDED:
        return g.transpose(1, 0, 2).reshape(g.shape[1], NDEV * g.shape[2])
    return g.reshape(NDEV * g.shape[1], g.shape[2])


def _to_shards(full, name):
    if name in COL_SHARDED:
        r, c = full.shape
        return full.reshape(r, NDEV, c // NDEV).transpose(1, 0, 2).astype(BF16)
    r, c = full.shape
    return full.reshape(NDEV, r // NDEV, c).astype(BF16)


def _layer_fwd(x, lw, mod, cos2, sin2, l):
    sh1, sc1, g1, sh2, sc2, g2 = [mod[j * D:(j + 1) * D] for j in range(6)]
    vec1 = _rows8([lw["norm1_w"], sh1, sc1], D)
    proj, h1 = norm_proj_fwd(x, vec1, lw["win"], f"inproj_fwd{l}")
    q, k, v = mla_pre_fwd(proj, lw["wq"], lw["wkv"], lw["mla_vec"], cos2, sin2, f"mla_pre_fwd{l}")
    oa = mla_attn_fwd(q, k, v, f"mla_attn_fwd{l}")
    ob = pool_fwd(proj, lw["pool_w"], lw["pool_scale"].reshape(1, 512), f"pool_fwd{l}")
    xbc = conv_fwd(proj, lw["ssd_conv_w"], lw["ssd_conv_b"].reshape(1, 1536), f"conv_fwd{l}")
    s = x.shape[0]
    xt = xbc[:, 0:D].reshape(s, 16, 64).transpose(1, 2, 0)
    dt = proj[:, O_DT:O_DT + 16].T
    dtr, dtc = dt[:, None, :], dt[:, :, None]
    hv = lambda a: a.reshape(16, 1, 1)
    yt, hs = ssd_fwd(xt, dtr, dtc, xbc, hv(lw["ssd_a_log"]), hv(lw["ssd_dt_bias"]), hv(lw["ssd_d"]), f"ssd_fwd{l}")
    y = yt.transpose(2, 0, 1).reshape(s, D)
    mvec = _rows8([g1, lw["ssd_norm_w"]], D)
    x1 = merge_fwd(oa, ob, y, proj, x, mvec, lw["wba"], lw["wbb"], lw["wbc"], lw["wout"], f"merge_fwd{l}")
    fvec = _rows8([lw["norm2_w"], sh2, sc2, g2], D)
    x2, h2, pre = ffn_fwd(x1, fvec, lw["wup"], lw["ffn_conv_w"], lw["ffn_conv_b"].reshape(1, 2 * FFN), lw["wdn"],
                          f"ffn_fwd{l}")
    saved = dict(x=x, vec1=vec1, proj=proj, h1=h1, q=q, k=k, v=v, oa=oa, ob=ob, xbc=xbc, xt=xt, dtr=dtr, dtc=dtc,
                 hs=hs, y=y, mvec=mvec, x1=x1, fvec=fvec, h2=h2, pre=pre)
    return x2, saved


def _layer_bwd(dx2, lw, sv, cos2, sin2, l):
    s = dx2.shape[0]
    grads, small = {}, {}
    dup, act, dcw = ffn_bwd(sv["h2"], dx2, sv["fvec"], lw["wup"], lw["ffn_conv_w"], lw["ffn_conv_b"].reshape(1, 2 * FFN),
                            lw["wdn"], f"ffn_bwd{l}")
    grads["ffn_down"] = tn_matmul(act, dx2, f"dw_down{l}", scale=sv["fvec"][3:4])
    grads["ffn_up"] = _ffn_uncols(tn_matmul(sv["h2"], dup, f"dw_up{l}"))
    dx1, dfvec = norm_proj_bwd(sv["x1"], sv["fvec"], dup, lw["wup"], dx2, sv["pre"], f"ffn_norm_bwd{l}")
    small["ffn_conv_w"] = _ffn_uncols(dcw[0:3])
    small["ffn_conv_b"] = _ffn_uncols(dcw[3])
    small["norm2_w"] = dfvec[0]
    (doa, dob, dy, dz, dgl, dx, dmvec, dya, dyb, dyc, dpre, oc, merged) = merge_bwd(
        sv["oa"], sv["ob"], sv["y"], sv["proj"], sv["x"], sv["mvec"], lw["wba"], lw["wbb"], lw["wbc"], lw["wout"], dx1,
        f"merge_bwd{l}")
    dwba = tn_matmul(sv["oa"], dya, f"dw_ba{l}").reshape(HEADS, 128, D)[:, 0:64].reshape(512, D)
    grads["w_branch"] = jnp.concatenate([dwba, tn_matmul(sv["ob"], dyb, f"dw_bb{l}"), tn_matmul(oc, dyc, f"dw_bc{l}")])
    grads["w_out"] = tn_matmul(merged, dpre, f"dw_out{l}")
    small["ssd_norm_w"] = dmvec[1]
    dyt = dy.reshape(s, 16, 64).transpose(1, 2, 0)
    hv = lambda a: a.reshape(16, 1, 1)
    dxt, ddtr, ddtc, dbm, dcm, dal, ddb, ddk = ssd_bwd(
        sv["xt"], sv["dtr"], sv["dtc"], sv["xbc"], hv(lw["ssd_a_log"]), hv(lw["ssd_dt_bias"]), hv(lw["ssd_d"]), sv["hs"],
        dyt, f"ssd_bwd{l}")
    small["ssd_a_log"], small["ssd_dt_bias"], small["ssd_d"] = dal.reshape(16), ddb.reshape(16), ddk.reshape(16)
    dact = jnp.concatenate([dxt.transpose(2, 0, 1).reshape(s, D), dbm, dcm], axis=1)
    dxbc, dscw, dscb = conv_bwd(sv["proj"], lw["ssd_conv_w"], lw["ssd_conv_b"].reshape(1, 1536), dact, f"conv_bwd{l}")
    small["ssd_conv_w"], small["ssd_conv_b"] = dscw, dscb.reshape(1536)
    ddt = (ddtr[:, 0, :] + ddtc[:, :, 0]).T
    du, dpw, dps = pool_bwd(sv["proj"], lw["pool_w"], lw["pool_scale"].reshape(1, 512), dob, f"pool_bwd{l}")
    small["pool_w"], small["pool_scale"] = dpw, dps.reshape(512)
    dq, dk, dv = mla_attn_bwd(sv["q"], sv["k"], sv["v"], doa, f"mla_attn_bwd{l}")
    dql, dckv, dkr, dkrs, dwq, dwkv, dmv = mla_pre_bwd(sv["proj"], lw["wq"], lw["wkv"], lw["mla_vec"], cos2, sin2,
                                                       dq, dk, dv, f"mla_pre_bwd{l}")
    grads["w_q_b"], grads["w_kv_b"] = _wq_unlayout(dwq), _wkv_unlayout(dwkv)
    small["q_a_norm"], small["kv_a_norm"], small["q_norm"], small["k_norm"] = _mla_unvec(dmv)
    dproj = jnp.concatenate([dgl, dxbc[:, 0:D], dz, du, dxbc[:, D:1536], dckv, dkr, dkrs,
                             _padc(ddt, 128).astype(BF16), jnp.zeros((s, 128), BF16), dql], axis=1)
    grads["w_in"] = _win_unlayout(tn_matmul(sv["h1"], dproj, f"dw_in{l}"))
    dx0, dvec1 = norm_proj_bwd(sv["x"], sv["vec1"], dproj, lw["win"], dx, None, f"inproj_bwd{l}")
    small["norm1_w"] = dvec1[0]
    small["ada_b"] = jnp.concatenate([dvec1[1], dvec1[2], dmvec[0], dfvec[1], dfvec[2], dfvec[3]])
    return dx0, grads, small


def kernel(x, c, positions, ada_w, ada_b, norm1_w, w_in, q_a_norm, w_q_b, kv_a_norm, w_kv_b, q_norm, k_norm, pool_w, pool_scale, ssd_conv_w, ssd_conv_b, ssd_dt_bias, ssd_a_log, ssd_d, ssd_norm_w, w_branch, w_out, norm2_w, ffn_up, ffn_conv_w, ffn_conv_b, ffn_down, loss_target, m_ada_w, m_ada_b, m_norm1_w, m_w_in, m_q_a_norm, m_w_q_b, m_kv_a_norm, m_w_kv_b, m_q_norm, m_k_norm, m_pool_w, m_pool_scale, m_ssd_conv_w, m_ssd_conv_b, m_ssd_dt_bias, m_ssd_a_log, m_ssd_d, m_ssd_norm_w, m_w_branch, m_w_out, m_norm2_w, m_ffn_up, m_ffn_conv_w, m_ffn_conv_b, m_ffn_down, v_ada_w, v_ada_b, v_norm1_w, v_w_in, v_q_a_norm, v_w_q_b, v_kv_a_norm, v_w_kv_b, v_q_norm, v_k_norm, v_pool_w, v_pool_scale, v_ssd_conv_w, v_ssd_conv_b, v_ssd_dt_bias, v_ssd_a_log, v_ssd_d, v_ssd_norm_w, v_w_branch, v_w_out, v_norm2_w, v_ffn_up, v_ffn_conv_w, v_ffn_conv_b, v_ffn_down):
    p = dict(ada_w=ada_w, ada_b=ada_b, norm1_w=norm1_w, w_in=w_in, q_a_norm=q_a_norm, w_q_b=w_q_b, kv_a_norm=kv_a_norm,
             w_kv_b=w_kv_b, q_norm=q_norm, k_norm=k_norm, pool_w=pool_w, pool_scale=pool_scale, ssd_conv_w=ssd_conv_w,
             ssd_conv_b=ssd_conv_b, ssd_dt_bias=ssd_dt_bias, ssd_a_log=ssd_a_log, ssd_d=ssd_d, ssd_norm_w=ssd_norm_w,
             w_branch=w_branch, w_out=w_out, norm2_w=norm2_w, ffn_up=ffn_up, ffn_conv_w=ffn_conv_w, ffn_conv_b=ffn_conv_b,
             ffn_down=ffn_down)
    mom = dict(ada_w=m_ada_w, ada_b=m_ada_b, norm1_w=m_norm1_w, w_in=m_w_in, q_a_norm=m_q_a_norm, w_q_b=m_w_q_b,
               kv_a_norm=m_kv_a_norm, w_kv_b=m_w_kv_b, q_norm=m_q_norm, k_norm=m_k_norm, pool_w=m_pool_w,
               pool_scale=m_pool_scale, ssd_conv_w=m_ssd_conv_w, ssd_conv_b=m_ssd_conv_b, ssd_dt_bias=m_ssd_dt_bias,
               ssd_a_log=m_ssd_a_log, ssd_d=m_ssd_d, ssd_norm_w=m_ssd_norm_w, w_branch=m_w_branch, w_out=m_w_out,
               norm2_w=m_norm2_w, ffn_up=m_ffn_up, ffn_conv_w=m_ffn_conv_w, ffn_conv_b=m_ffn_conv_b, ffn_down=m_ffn_down)
    var = dict(ada_w=v_ada_w, ada_b=v_ada_b, norm1_w=v_norm1_w, w_in=v_w_in, q_a_norm=v_q_a_norm, w_q_b=v_w_q_b,
               kv_a_norm=v_kv_a_norm, w_kv_b=v_w_kv_b, q_norm=v_q_norm, k_norm=v_k_norm, pool_w=v_pool_w,
               pool_scale=v_pool_scale, ssd_conv_w=v_ssd_conv_w, ssd_conv_b=v_ssd_conv_b, ssd_dt_bias=v_ssd_dt_bias,
               ssd_a_log=v_ssd_a_log, ssd_d=v_ssd_d, ssd_norm_w=v_ssd_norm_w, w_branch=v_w_branch, w_out=v_w_out,
               norm2_w=v_norm2_w, ffn_up=v_ffn_up, ffn_conv_w=v_ffn_conv_w, ffn_conv_b=v_ffn_conv_b, ffn_down=v_ffn_down)
    names = list(p)
    me = 4 * lax.axis_index("x") + 2 * lax.axis_index("y") + lax.axis_index("c")
    xs, tgt = x[0], loss_target[0]
    s = xs.shape[0]

    inv_freq = ROPE_THETA ** (-jnp.arange(0, 32, 2, dtype=F32) / 32.0)
    ang = positions[0].astype(F32)[:, None] * inv_freq
    cos, sin = jnp.cos(ang), jnp.sin(ang)
    cos2 = _padc(jnp.concatenate([cos, cos], axis=1), 128)
    sin2 = _padc(jnp.concatenate([-sin, sin], axis=1), 128)

    (c_all,) = all_to_all([c], [True], "gather_c")
    modp, cact = ada_mod(jnp.pad(c_all.reshape(NDEV, D), ((0, 8), (0, 0))), ada_w)
    (mod_in,) = all_to_all([modp[:, 0:NDEV].transpose(1, 0, 2)], [False], "scatter_mod")
    mod = mod_in.transpose(1, 0, 2).reshape(LAYERS, 6 * D) + ada_b

    gathered = all_to_all([p[n].astype(BF16) for n in BIG], [True] * len(BIG), "gather_weights")
    gathered = dict(zip(BIG, gathered))
    lws = []
    for l in range(LAYERS):
        full = {n: _gathered_full(gathered[n][:, l], n) for n in BIG}
        lw = {n: p[n][l] for n in names}
        wb = full["w_branch"]
        lw.update(win=_win_layout(full["w_in"]), wq=jnp.pad(_wq_layout(full["w_q_b"]), ((0, 0), (0, 0), (0, 0))),
                  wkv=_wkv_layout(full["w_kv_b"]), wba=_wba_layout(wb[0:512]), wbb=wb[512:1024], wbc=wb[1024:2048],
                  wout=full["w_out"], wup=_ffn_cols(full["ffn_up"]), wdn=full["ffn_down"],
                  mla_vec=_mla_vec(lw["q_a_norm"], lw["kv_a_norm"], lw["q_norm"], lw["k_norm"]))
        lws.append(lw)
    conv_shards = jnp.concatenate([ssd_conv_w.reshape(-1), ffn_conv_w.reshape(-1)])
    (conv_all,) = all_to_all([conv_shards], [True], "gather_conv_w")
    n1 = LAYERS * 4 * 192
    scw = conv_all[:, :n1].reshape(NDEV, LAYERS, 4, 192).transpose(1, 2, 0, 3).reshape(LAYERS, 4, 1536)
    fcw = conv_all[:, n1:].reshape(NDEV, LAYERS, 3, 704).transpose(1, 2, 0, 3).reshape(LAYERS, 3, 2 * FFN)
    for l in range(LAYERS):
        lws[l]["ssd_conv_w"] = scw[l]
        lws[l]["ffn_conv_w"] = _ffn_cols(fcw[l])
        lws[l]["ffn_conv_b"] = _ffn_cols(p["ffn_conv_b"][l])

    h, saved = xs, []
    for l in range(LAYERS):
        h, sv = _layer_fwd(h, lws[l], mod[l], cos2, sin2, l)
        saved.append(sv)
    dx, lpart = loss_head(h, tgt)
    loss = lax.psum(lpart[0, 0], ("x", "y", "c"))

    grads, small = [None] * LAYERS, [None] * LAYERS
    for l in reversed(range(LAYERS)):
        dx, grads[l], small[l] = _layer_bwd(dx, lws[l], saved[l], cos2, sin2, l)

    dmod = jnp.stack([small[l]["ada_b"] for l in range(LAYERS)])
    (dmod_in,) = all_to_all([dmod.reshape(LAYERS, NDEV, 768).transpose(1, 0, 2)], [False], "scatter_dmod")
    dmod16 = jnp.pad(dmod_in, ((0, 8), (0, 0), (0, 0)))
    g_ada = jnp.stack([tn_matmul(cact, dmod16[:, l], f"dw_ada{l}") for l in range(LAYERS)])

    parts = all_to_all([jnp.stack([_to_shards(grads[l][n], n) for l in range(LAYERS)], axis=1) for n in BIG],
                       [False] * len(BIG), "scatter_grads")
    out = {}
    for n, pt in zip(BIG, parts):
        shp = p[n].shape
        flat = lambda a: a.reshape(shp[0] * shp[1], shp[2])
        res = adamw(pt.reshape(NDEV, shp[0] * shp[1], shp[2]), flat(p[n]), flat(mom[n]), flat(var[n]), f"adamw_{n}")
        out[n] = [r.reshape(shp) for r in res]
    flat = lambda a: a.reshape(LAYERS * D, 768)
    out["ada_w"] = [r.reshape(ada_w.shape) for r in
                    adamw(flat(g_ada)[None], flat(ada_w), flat(m_ada_w), flat(v_ada_w), "adamw_ada_w")]

    for l in range(LAYERS):
        small[l]["ffn_conv_w"] = small[l]["ffn_conv_w"]
    (small_all,) = all_to_all([_pack(small, SMALL)], [True], "gather_small")
    g_small = adamw(small_all, jnp.zeros((SMALL_ROWS, D), F32), jnp.zeros((SMALL_ROWS, D), F32),
                    jnp.zeros((SMALL_ROWS, D), F32), "sum_small")[0]
    g_small = _unpack(g_small, SMALL)
    per = lambda d, nm: [{n: d[n][l] for n, _ in nm} for l in range(LAYERS)]
    res = adamw(_pack(per(g_small, SMALL_REPL), SMALL_REPL)[None], _pack(per(p, SMALL_REPL), SMALL_REPL),
                _pack(per(mom, SMALL_REPL), SMALL_REPL), _pack(per(var, SMALL_REPL), SMALL_REPL), "adamw_small")
    res = [_unpack(r, SMALL_REPL) for r in res]
    for n, _ in SMALL_REPL:
        out[n] = [r[n] for r in res]
    for n, k, w in (("ssd_conv_w", 4, 192), ("ffn_conv_w", 3, 704)):
        g_mine = lax.dynamic_slice(g_small[n], (0, 0, me * w), (LAYERS, k, w))
        f2 = lambda a: jnp.pad(a.reshape(LAYERS * k, w), ((0, 8 - LAYERS * k), (0, 0)))
        res = adamw(f2(g_mine)[None], f2(p[n]), f2(mom[n]), f2(var[n]), f"adamw_{n}")
        out[n] = [r[0:LAYERS * k].reshape(LAYERS, k, w) for r in res]

    outs = [loss, dx[None]]
    for q in range(4):
        outs += [out[n][q] for n in names]
    return tuple(outs)
```

```python
import functools
import math

import jax
import jax.numpy as jnp
from jax import lax
from jax.experimental import pallas as pl
from jax.experimental.pallas import tpu as pltpu

F32, BF16 = jnp.float32, jnp.bfloat16
EPS = 1e-6
D = 1024
NDEV = 8
LAYERS = 2
HEADS = 8
FFN = 2816
FFN_TILE = 1408
FFN_NT = FFN // FFN_TILE
ATT_SCALE = 96 ** -0.5
ROPE_THETA = 10000.0
LR, B1, B2, ADAM_EPS, WD, STEP = 0.001, 0.9, 0.999, 1e-08, 0.01, 10

O_G, O_XS, O_Z, O_PU, O_BC, O_CKV, O_KR, O_KRS, O_DT, O_QL = 0, 3072, 4096, 5120, 5632, 6144, 6400, 6528, 6656, 6912
NPROJ = 7296
CONST = dict(pipeline_mode=pl.Buffered(1))


def _pick(n, cap, mult=128):
    if n <= cap:
        return n
    best = None
    for t in range(mult, cap + 1, mult):
        if n % t == 0:
            best = t
    assert best is not None, (n, cap, mult)
    return best


def _sig(x):
    return 1.0 / (1.0 + jnp.exp(-x))


def _rms(x, w, n):
    return x * lax.rsqrt(jnp.sum(x * x, axis=-1, keepdims=True) / n + EPS) * w


def _raw(a, b, dims):
    return lax.dot_general(a.astype(BF16), b.astype(BF16), dims, preferred_element_type=F32)


_NN = (((1,), (0,)), ((), ()))
_NT = (((1,), (1,)), ((), ()))
_TN = (((0,), (0,)), ((), ()))
_BNN = (((2,), (1,)), ((0,), (0,)))
_BNT = (((2,), (2,)), ((0,), (0,)))
_BTN = (((1,), (1,)), ((0,), (0,)))


@jax.custom_vjp
def mm_nn(a, b):
    return _raw(a, b, _NN)


mm_nn.defvjp(lambda a, b: (_raw(a, b, _NN), (a, b)),
             lambda r, g: (_raw(g, r[1], _NT), _raw(r[0], g, _TN)))


@jax.custom_vjp
def mm_nc(a, b):
    return _raw(a, b, _NN)


mm_nc.defvjp(lambda a, b: (_raw(a, b, _NN), b),
             lambda b, g: (_raw(g, b, _NT), jnp.zeros_like(b)))


@jax.custom_vjp
def mm_nt(a, b):
    return _raw(a, b, _NT)


mm_nt.defvjp(lambda a, b: (_raw(a, b, _NT), (a, b)),
             lambda r, g: (_raw(g, r[1], _NN), _raw(g, r[0], _TN)))


@jax.custom_vjp
def bmm_nn(a, b):
    return _raw(a, b, _BNN)


bmm_nn.defvjp(lambda a, b: (_raw(a, b, _BNN), (a, b)),
              lambda r, g: (_raw(g, r[1], _BNT), _raw(r[0], g, _BTN)))


@jax.custom_vjp
def bmm_nt(a, b):
    return _raw(a, b, _BNT)


bmm_nt.defvjp(lambda a, b: (_raw(a, b, _BNT), (a, b)),
              lambda r, g: (_raw(g, r[1], _BNN), _raw(g, r[0], _BTN)))


@jax.custom_vjp
def softplus(x):
    t = jnp.exp(-jnp.abs(x))
    u = 1.0 + t
    one = u == 1.0
    l1p = jnp.where(one, t, jnp.log(u) * (t / jnp.where(one, 1.0, u - 1.0)))
    return jnp.maximum(x, 0.0) + l1p


softplus.defvjp(lambda x: (softplus(x), x), lambda x, g: (g * _sig(x),))


def _params(*sem):
    return pltpu.CompilerParams(dimension_semantics=sem, vmem_limit_bytes=56 * 1024 * 1024)


def all_to_all(arrs, bcast, name):
    n = len(arrs)
    out_shapes = [jax.ShapeDtypeStruct(((NDEV,) + a.shape) if b else a.shape, a.dtype) for a, b in zip(arrs, bcast)]

    def body(*refs):
        ins, outs = refs[:n], refs[n:2 * n]
        send_sems, recv_sems, local_sems = refs[2 * n:]
        x, y, c = lax.axis_index("x"), lax.axis_index("y"), lax.axis_index("c")
        me = 4 * x + 2 * y + c
        local = []
        for j in range(n):
            cp = pltpu.make_async_copy(ins[j] if bcast[j] else ins[j].at[me], outs[j].at[me], local_sems.at[j])
            cp.start()
            local.append(cp)
        remote = []
        for k in range(1, NDEV):
            px, py, pc = x ^ ((k >> 2) & 1), y ^ ((k >> 1) & 1), c ^ (k & 1)
            peer = 4 * px + 2 * py + pc
            for j in range(n):
                s = (k - 1) * n + j
                cp = pltpu.make_async_remote_copy(
                    src_ref=ins[j] if bcast[j] else ins[j].at[peer], dst_ref=outs[j].at[me],
                    send_sem=send_sems.at[s], recv_sem=recv_sems.at[s],
                    device_id=(px, py, pc), device_id_type=pl.DeviceIdType.MESH)
                cp.start()
                remote.append(cp)
        for cp in remote:
            cp.wait()
        for cp in local:
            cp.wait()

    any_spec = pl.BlockSpec(memory_space=pl.ANY)
    return pl.pallas_call(
        body, name=name, out_shape=out_shapes, in_specs=[any_spec] * n, out_specs=[any_spec] * n,
        scratch_shapes=[pltpu.SemaphoreType.DMA((7 * n,)), pltpu.SemaphoreType.DMA((7 * n,)),
                        pltpu.SemaphoreType.DMA((n,))],
        compiler_params=pltpu.CompilerParams(has_side_effects=True),
    )(*arrs)


def _peers():
    x, y, c = lax.axis_index("x"), lax.axis_index("y"), lax.axis_index("c")
    out = []
    for k in range(1, NDEV):
        px, py, pc = x ^ ((k >> 2) & 1), y ^ ((k >> 1) & 1), c ^ (k & 1)
        out.append(((px, py, pc), 4 * px + 2 * py + pc))
    return 4 * x + 2 * y + c, out


def _exchange_copies(ins, lands, bcast, send_sems, recv_sems):
    me, peers = _peers()
    n, copies = len(ins), []
    for k, (dev, lin) in enumerate(peers):
        for j in range(n):
            copies.append(pltpu.make_async_remote_copy(
                src_ref=ins[j] if bcast[j] else ins[j].at[lin], dst_ref=lands[j].at[me],
                send_sem=send_sems.at[k * n + j], recv_sem=recv_sems.at[k * n + j],
                device_id=dev, device_id_type=pl.DeviceIdType.MESH))
    return me, copies


_HBM = pl.BlockSpec(memory_space=pltpu.HBM)
_SEM = pl.BlockSpec(memory_space=pltpu.SEMAPHORE)
_EFFECT = pltpu.SideEffectType.DATAFLOW_SIDE_EFFECTING


def exchange_start(arrs, bcast, name):
    n = len(arrs)
    land_shapes = [((NDEV,) + a.shape) if b else a.shape for a, b in zip(arrs, bcast)]

    def body(*refs):
        ins, lands = refs[:n], refs[n:2 * n]
        send_sems, recv_sems = refs[2 * n], refs[2 * n + 1]
        token = refs[-1]
        _, copies = _exchange_copies(ins, lands, bcast, send_sems, recv_sems)
        for cp in copies:
            cp.start()
        token[...] = jnp.zeros_like(token)

    hbm = lambda shp, a: pltpu.HBM(shp, a.dtype)
    res = pl.pallas_call(
        body, name=name,
        out_shape=[pltpu.SemaphoreType.DMA((7 * n,)), pltpu.SemaphoreType.DMA((7 * n,))]
                  + [hbm(a.shape, a) for a in arrs] + [hbm(s_, a) for s_, a in zip(land_shapes, arrs)]
                  + [jax.ShapeDtypeStruct((8, 128), F32)],
        in_specs=[_HBM] * (2 * n), out_specs=[_SEM, _SEM] + [_HBM] * (2 * n) + [pl.BlockSpec(memory_space=pltpu.VMEM)],
        input_output_aliases={i: 2 + i for i in range(2 * n)},
        compiler_params=pltpu.CompilerParams(has_side_effects=_EFFECT),
    )(*[pltpu.with_memory_space_constraint(a, pltpu.HBM) for a in arrs],
      *[pltpu.with_memory_space_constraint(lax.empty(s_, a.dtype), pltpu.HBM) for s_, a in zip(land_shapes, arrs)])
    return (res[0], res[1], res[2:2 + n], res[2 + n:2 + 2 * n], tuple(bcast)), res[-1]


def exchange_wait(state, after, name):
    send_sems, recv_sems, ins, lands, bcast = state
    n = len(ins)

    def body(*refs):
        in_refs, land_refs = refs[:n], refs[n:2 * n]
        s_sems, r_sems = refs[2 * n], refs[2 * n + 1]
        got = refs[2 * n + 3 + n:2 * n + 3 + 2 * n]
        local_sems = refs[-1]
        me, copies = _exchange_copies(in_refs, land_refs, bcast, s_sems, r_sems)
        own = [pltpu.make_async_copy(in_refs[j] if bcast[j] else in_refs[j].at[me], got[j].at[me], local_sems.at[j])
               for j in range(n)]
        for cp in own:
            cp.start()
        for cp in copies:
            cp.wait_send()
            cp.wait_recv()
        for cp in own:
            cp.wait()

    res = pl.pallas_call(
        body, name=name,
        out_shape=[pltpu.HBM(a.shape, a.dtype) for a in ins] + [pltpu.HBM(a.shape, a.dtype) for a in lands],
        in_specs=[_HBM] * (2 * n) + [_SEM, _SEM, pl.BlockSpec(memory_space=pl.ANY)], out_specs=[_HBM] * (2 * n),
        input_output_aliases={i: i for i in range(2 * n)},
        scratch_shapes=[pltpu.SemaphoreType.DMA((n,))],
        compiler_params=pltpu.CompilerParams(has_side_effects=_EFFECT),
    )(*ins, *lands, send_sems, recv_sems, after)
    return res[n:2 * n]


def norm_proj_fwd(x, vec, w, name):
    s, n = x.shape[0], w.shape[1]
    tr, tn = _pick(s, 512), _pick(n, 2560)

    def body(x_ref, v_ref, w_ref, o_ref, h_ref, h_scr):
        @pl.when(pl.program_id(1) == 0)
        def _():
            h = _rms(x_ref[...], v_ref[0:1, :], D) * (1.0 + v_ref[2:3, :]) + v_ref[1:2, :]
            h_scr[...] = h.astype(BF16)
            h_ref[...] = h.astype(BF16)
        o_ref[...] = jnp.dot(h_scr[...], w_ref[...], preferred_element_type=F32)

    return pl.pallas_call(
        body, name=name, grid=(s // tr, n // tn),
        in_specs=[pl.BlockSpec((tr, D), lambda i, j: (i, 0)), pl.BlockSpec((8, D), lambda i, j: (0, 0)),
                  pl.BlockSpec((D, tn), lambda i, j: (0, j))],
        out_specs=[pl.BlockSpec((tr, tn), lambda i, j: (i, j)), pl.BlockSpec((tr, D), lambda i, j: (i, 0))],
        out_shape=[jax.ShapeDtypeStruct((s, n), F32), jax.ShapeDtypeStruct((s, D), BF16)],
        scratch_shapes=[pltpu.VMEM((tr, D), BF16)],
        compiler_params=_params("arbitrary", "arbitrary"),
    )(x, vec, w)


def _col_tiles(arr, cap):
    if arr.ndim == 2:
        n = arr.shape[1]
        t = _pick(n, cap)
        return n, t, lambda rows, ix: pl.BlockSpec((rows, t), lambda *g: ix(*g))
    width = arr.shape[2]
    t = _pick(width, cap)
    per = width // t

    def spec(rows, ix):
        def index(*g):
            r, j = ix(*g)
            return (j // per, r, j % per)
        return pl.BlockSpec((None, rows, t), index)
    return arr.shape[0] * width, t, spec


def norm_proj_bwd(x, vec, dp, w, dx_in, aux, name):
    s = x.shape[0]
    tr = _pick(s, 512)
    n, tk, dp_spec = _col_tiles(dp, 2560)
    nk, has_aux = n // tk, aux is not None

    def body(*refs):
        if has_aux:
            x_ref, v_ref, dp_ref, w_ref, dxin_ref, aux_ref, dx_ref, dv_ref, acc = refs
        else:
            x_ref, v_ref, dp_ref, w_ref, dxin_ref, dx_ref, dv_ref, acc = refs
        i, k = pl.program_id(0), pl.program_id(1)

        @pl.when(k == 0)
        def _():
            acc[...] = jnp.zeros_like(acc)

        acc[...] += _raw(dp_ref[...], w_ref[...], _NT)

        @pl.when(k == nk - 1)
        def _():
            f = lambda xx, nw, sh, sc: _rms(xx, nw, D) * (1.0 + sc) + sh
            _, vjp = jax.vjp(f, x_ref[...], v_ref[0:1, :], v_ref[1:2, :], v_ref[2:3, :])
            dx, dnw, dsh, dsc = vjp(acc[...])
            dx_ref[...] = dxin_ref[...] + dx

            @pl.when(i == 0)
            def _():
                dv_ref[...] = jnp.zeros_like(dv_ref)

            dv_ref[0:1, :] += dnw
            dv_ref[1:2, :] += dsh
            dv_ref[2:3, :] += dsc
            if has_aux:
                dv_ref[3:4, :] += jnp.sum(dxin_ref[...] * aux_ref[...], axis=0, keepdims=True)

    row = pl.BlockSpec((tr, D), lambda i, k: (i, 0))
    in_specs = [row, pl.BlockSpec((8, D), lambda i, k: (0, 0)), dp_spec(tr, lambda i, k: (i, k)),
                pl.BlockSpec((D, tk), lambda i, k: (0, k)), row] + ([row] if has_aux else [])
    args = [x, vec, dp, w, dx_in] + ([aux] if has_aux else [])
    return pl.pallas_call(
        body, name=name, grid=(s // tr, nk), in_specs=in_specs,
        out_specs=[row, pl.BlockSpec((8, D), lambda i, k: (0, 0))],
        out_shape=[jax.ShapeDtypeStruct((s, D), F32), jax.ShapeDtypeStruct((8, D), F32)],
        scratch_shapes=[pltpu.VMEM((tr, D), F32)],
        compiler_params=_params("arbitrary", "arbitrary"),
    )(*args)


def tn_matmul(a, b, name, scale=None):
    s, m = a.shape
    ts, tm = _pick(s, 512, 16), _pick(m, 1408)
    n, tn, b_spec = _col_tiles(b, 2560)
    ns, has_scale = s // ts, scale is not None

    def body(*refs):
        if has_scale:
            a_ref, b_ref, sc_ref, o_ref = refs
        else:
            a_ref, b_ref, o_ref = refs
        k = pl.program_id(2)

        @pl.when(k == 0)
        def _():
            o_ref[...] = jnp.zeros_like(o_ref)

        o_ref[...] += _raw(a_ref[...], b_ref[...], _TN)
        if has_scale:
            @pl.when(k == ns - 1)
            def _():
                o_ref[...] = o_ref[...] * sc_ref[...]

    in_specs = [pl.BlockSpec((ts, tm), lambda i, j, k: (k, i)), b_spec(ts, lambda i, j, k: (k, j))]
    if has_scale:
        in_specs.append(pl.BlockSpec((1, tn), lambda i, j, k: (0, j)))
    return pl.pallas_call(
        body, name=name, grid=(m // tm, n // tn, ns), in_specs=in_specs,
        out_specs=pl.BlockSpec((tm, tn), lambda i, j, k: (i, j)),
        out_shape=jax.ShapeDtypeStruct((m, n), F32),
        compiler_params=_params("arbitrary", "arbitrary", "arbitrary"),
    )(*([a, b] + ([scale] if has_scale else [])))


def ada_mod(c16, w):
    ncol = w.shape[2]

    def body(c_ref, w_ref, o_ref, a_ref):
        cc = c_ref[...]
        act = cc * _sig(cc)
        a_ref[...] = act
        o_ref[...] = _raw(act, w_ref[...], _NN)

    return pl.pallas_call(
        body, name="ada_mod", grid=(LAYERS,),
        in_specs=[pl.BlockSpec((16, D), lambda l: (0, 0)), pl.BlockSpec((None, D, ncol), lambda l: (l, 0, 0))],
        out_specs=[pl.BlockSpec((None, 16, ncol), lambda l: (l, 0, 0)), pl.BlockSpec((16, D), lambda l: (0, 0))],
        out_shape=[jax.ShapeDtypeStruct((LAYERS, 16, ncol), F32), jax.ShapeDtypeStruct((16, D), F32)],
        compiler_params=_params("arbitrary"),
    )(c16, w)


def _mla_head(q_lat, c_kv, kr, krs, wqn, wqr, wqrs, wkn, wv, qa_w, kva_w, qn_w, qr_w, qrs_w, kn_w, kr_w, krs_w,
              cos2, sin2):
    qn = _rms(q_lat, qa_w, 384.0)
    kvn = _rms(c_kv, kva_w, 256.0)
    qnope = _rms(mm_nn(qn, wqn), qn_w, 64.0)
    qr, qrs = mm_nn(qn, wqr), mm_nn(qn, wqrs)
    rq = lax.rsqrt(jnp.sum(qr * qr, axis=-1, keepdims=True) / 32.0 + EPS)
    qrope = rq * (qr * qr_w * cos2 + qrs * qrs_w * sin2)
    knope = _rms(mm_nn(kvn, wkn), kn_w, 64.0)
    v = mm_nn(kvn, wv)
    rk = lax.rsqrt(jnp.sum(kr * kr, axis=-1, keepdims=True) / 32.0 + EPS)
    krope = rk * (kr * kr_w * cos2 + krs * krs_w * sin2)
    return qnope, qrope, knope, krope, v


def _mla_vec_pieces(v_ref):
    return (v_ref[0:1, 0:384], v_ref[1:2, 0:256], v_ref[2:3, 0:128], v_ref[2:3, 128:256], v_ref[2:3, 256:384],
            v_ref[3:4, 0:128], v_ref[3:4, 128:256], v_ref[3:4, 256:384])


def _mla_in_specs(tr):
    return [pl.BlockSpec((tr, 384), lambda i: (i, O_QL // 384)), pl.BlockSpec((tr, 256), lambda i: (i, O_CKV // 256)),
            pl.BlockSpec((tr, 128), lambda i: (i, O_KR // 128)), pl.BlockSpec((tr, 128), lambda i: (i, O_KRS // 128)),
            pl.BlockSpec((HEADS, 384, 384), lambda i: (0, 0, 0), **CONST),
            pl.BlockSpec((HEADS, 256, 256), lambda i: (0, 0, 0), **CONST),
            pl.BlockSpec((8, 512), lambda i: (0, 0)),
            pl.BlockSpec((tr, 128), lambda i: (i, 0)), pl.BlockSpec((tr, 128), lambda i: (i, 0))]


def mla_pre_fwd(proj, wq, wkv, vec, cos2, sin2, name):
    s = proj.shape[0]
    tr = _pick(s, 256)

    def body(ql_ref, ckv_ref, kr_ref, krs_ref, wq_ref, wkv_ref, v_ref, cos_ref, sin_ref, q_out, k_out, v_out):
        acts = (ql_ref[...], ckv_ref[...], kr_ref[...], krs_ref[...])
        vp = _mla_vec_pieces(v_ref)
        for h in range(HEADS):
            ws = (wq_ref[h, :, 0:128], wq_ref[h, :, 128:256], wq_ref[h, :, 256:384],
                  wkv_ref[h, :, 0:128], wkv_ref[h, :, 128:256])
            qn, qr, kn, krp, v = _mla_head(*acts, *ws, *vp, cos_ref[...], sin_ref[...])
            q_out[h, :, 0:128] = qn.astype(BF16)
            q_out[h, :, 128:256] = qr.astype(BF16)
            k_out[h, :, 0:128] = kn.astype(BF16)
            k_out[h, :, 128:256] = krp.astype(BF16)
            v_out[h] = v.astype(BF16)

    return pl.pallas_call(
        body, name=name, grid=(s // tr,), in_specs=_mla_in_specs(tr),
        out_specs=[pl.BlockSpec((HEADS, tr, 256), lambda i: (0, i, 0)), pl.BlockSpec((HEADS, tr, 256), lambda i: (0, i, 0)),
                   pl.BlockSpec((HEADS, tr, 128), lambda i: (0, i, 0))],
        out_shape=[jax.ShapeDtypeStruct((HEADS, s, 256), BF16), jax.ShapeDtypeStruct((HEADS, s, 256), BF16),
                   jax.ShapeDtypeStruct((HEADS, s, 128), BF16)],
        compiler_params=_params("arbitrary"),
    )(proj, proj, proj, proj, wq, wkv, vec, cos2, sin2)


def mla_pre_bwd(proj, wq, wkv, vec, cos2, sin2, dq, dk, dv, name):
    s = proj.shape[0]
    tr = _pick(s, 256)

    def body(ql_ref, ckv_ref, kr_ref, krs_ref, wq_ref, wkv_ref, v_ref, cos_ref, sin_ref, dq_ref, dk_ref, dv_ref,
             dql_out, dckv_out, dkr_out, dkrs_out, dwq_out, dwkv_out, dvec_out):
        @pl.when(pl.program_id(0) == 0)
        def _():
            dwq_out[...] = jnp.zeros_like(dwq_out)
            dwkv_out[...] = jnp.zeros_like(dwkv_out)
            dvec_out[...] = jnp.zeros_like(dvec_out)

        acts = (ql_ref[...], ckv_ref[...], kr_ref[...], krs_ref[...])
        vp = _mla_vec_pieces(v_ref)
        cos2_, sin2_ = cos_ref[...], sin_ref[...]

        def head(h, carry):
            wq_h, wkv_h = wq_ref[h].astype(F32), wkv_ref[h].astype(F32)
            ws = (wq_h[:, 0:128], wq_h[:, 128:256], wq_h[:, 256:384], wkv_h[:, 0:128], wkv_h[:, 128:256])
            f = lambda *a: _mla_head(*a, cos2_, sin2_)
            _, vjp = jax.vjp(f, *acts, *ws, *vp)
            dq_h, dk_h = dq_ref[h], dk_ref[h]
            g = vjp((dq_h[:, 0:128], dq_h[:, 128:256], dk_h[:, 0:128], dk_h[:, 128:256], dv_ref[h]))
            dwq_out[h, :, 0:128] += g[4]
            dwq_out[h, :, 128:256] += g[5]
            dwq_out[h, :, 256:384] += g[6]
            dwkv_out[h, :, 0:128] += g[7]
            dwkv_out[h, :, 128:256] += g[8]
            dvec_out[0:1, 0:384] += g[9]
            dvec_out[1:2, 0:256] += g[10]
            dvec_out[2:3, 0:128] += g[11]
            dvec_out[2:3, 128:256] += g[12]
            dvec_out[2:3, 256:384] += g[13]
            dvec_out[3:4, 0:128] += g[14]
            dvec_out[3:4, 128:256] += g[15]
            dvec_out[3:4, 256:384] += g[16]
            return tuple(c + gg for c, gg in zip(carry, g[:4]))

        tot = lax.fori_loop(0, HEADS, head, tuple(jnp.zeros_like(a) for a in acts))
        dql_out[...] = tot[0].astype(BF16)
        dckv_out[...] = tot[1].astype(BF16)
        dkr_out[...] = tot[2].astype(BF16)
        dkrs_out[...] = tot[3].astype(BF16)

    hb = lambda w: pl.BlockSpec((HEADS, tr, w), lambda i: (0, i, 0))
    return pl.pallas_call(
        body, name=name, grid=(s // tr,), in_specs=_mla_in_specs(tr) + [hb(256), hb(256), hb(128)],
        out_specs=[pl.BlockSpec((tr, 384), lambda i: (i, 0)), pl.BlockSpec((tr, 256), lambda i: (i, 0)),
                   pl.BlockSpec((tr, 128), lambda i: (i, 0)), pl.BlockSpec((tr, 128), lambda i: (i, 0)),
                   pl.BlockSpec((HEADS, 384, 384), lambda i: (0, 0, 0)), pl.BlockSpec((HEADS, 256, 256), lambda i: (0, 0, 0)),
                   pl.BlockSpec((8, 512), lambda i: (0, 0))],
        out_shape=[jax.ShapeDtypeStruct((s, 384), BF16), jax.ShapeDtypeStruct((s, 256), BF16),
                   jax.ShapeDtypeStruct((s, 128), BF16), jax.ShapeDtypeStruct((s, 128), BF16),
                   jax.ShapeDtypeStruct((HEADS, 384, 384), F32), jax.ShapeDtypeStruct((HEADS, 256, 256), F32),
                   jax.ShapeDtypeStruct((8, 512), F32)],
        compiler_params=_params("arbitrary"),
    )(proj, proj, proj, proj, wq, wkv, vec, cos2, sin2, dq, dk, dv)


def _att_probs(q, kk, i, tq):
    sc = _raw(q, kk, _NT) * ATT_SCALE
    rows = lax.broadcasted_iota(jnp.int32, sc.shape, 0) + i * tq
    cols = lax.broadcasted_iota(jnp.int32, sc.shape, 1)
    sc = jnp.where(cols <= rows, sc, -jnp.inf)
    e = jnp.exp(sc - jnp.max(sc, axis=-1, keepdims=True))
    return e / jnp.sum(e, axis=-1, keepdims=True)


def mla_attn_fwd(q, k, v, name):
    s = q.shape[1]
    tq = _pick(s, 256)

    def body(q_ref, k_ref, v_ref, o_ref):
        for i in range(s // tq):
            n = (i + 1) * tq
            p = _att_probs(q_ref[i * tq:n, :], k_ref[0:n, :], i, tq)
            o_ref[i * tq:n, :] = _raw(p, v_ref[0:n, :], _NN)

    hs = lambda w: pl.BlockSpec((None, s, w), lambda h: (h, 0, 0))
    return pl.pallas_call(
        body, name=name, grid=(HEADS,), in_specs=[hs(256), hs(256), hs(128)],
        out_specs=pl.BlockSpec((s, 128), lambda h: (0, h)),
        out_shape=jax.ShapeDtypeStruct((s, HEADS * 128), F32),
        compiler_params=_params("arbitrary"),
    )(q, k, v)


def mla_attn_bwd(q, k, v, do, name):
    s = q.shape[1]
    tq = _pick(s, 256)

    def body(q_ref, k_ref, v_ref, do_ref, dq_ref, dk_ref, dv_ref):
        dk_ref[...] = jnp.zeros_like(dk_ref)
        dv_ref[...] = jnp.zeros_like(dv_ref)
        for i in range(s // tq):
            n = (i + 1) * tq
            qq, kk, vv = q_ref[i * tq:n, :], k_ref[0:n, :], v_ref[0:n, :]
            p = _att_probs(qq, kk, i, tq)
            o = _raw(p, vv, _NN)
            dout = do_ref[i * tq:n, :]
            delta = jnp.sum(dout * o, axis=-1, keepdims=True)
            dp = _raw(dout, vv, _NT)
            ds = p * (dp - delta) * ATT_SCALE
            dq_ref[i * tq:n, :] = _raw(ds, kk, _NN)
            dk_ref[0:n, :] += _raw(ds, qq, _TN)
            dv_ref[0:n, :] += _raw(p, dout, _TN)

    hs = lambda w: pl.BlockSpec((None, s, w), lambda h: (h, 0, 0))
    return pl.pallas_call(
        body, name=name, grid=(HEADS,),
        in_specs=[hs(256), hs(256), hs(128), pl.BlockSpec((s, 128), lambda h: (0, h))],
        out_specs=[hs(256), hs(256), hs(128)],
        out_shape=[jax.ShapeDtypeStruct((HEADS, s, 256), F32), jax.ShapeDtypeStruct((HEADS, s, 256), F32),
                   jax.ShapeDtypeStruct((HEADS, s, 128), F32)],
        compiler_params=_params("arbitrary"),
    )(q, k, v, do)


def _pool_windows(u, pad, s, g):
    pad[0:16, :] = jnp.zeros((16, 128), F32)
    cur, sel = u, None
    for j, k in enumerate((1, 2, 4, 8)):
        pad[16:16 + s, :] = cur
        cur = cur + pad[16 - k:16 - k + s, :]
        sel = cur if sel is None else jnp.where(g == j, cur, sel)
    return sel


def _pool_count(s, g):
    t = lax.broadcasted_iota(jnp.int32, (s, 1), 0)
    return jnp.minimum(t + 1, 2 << g).astype(F32)


def pool_fwd(proj, pw, ps, name):
    s = proj.shape[0]

    def body(u_ref, w_ref, s_ref, o_ref, pad):
        g = pl.program_id(0)
        u = u_ref[...]
        pooled = _pool_windows(u, pad, s, g) / _pool_count(s, g) - u
        o_ref[...] = _raw(pooled, w_ref[...], _NN) * s_ref[...]

    return pl.pallas_call(
        body, name=name, grid=(4,),
        in_specs=[pl.BlockSpec((s, 128), lambda g: (0, O_PU // 128 + g)), pl.BlockSpec((None, 128, 128), lambda g: (g, 0, 0)),
                  pl.BlockSpec((1, 128), lambda g: (0, g))],
        out_specs=pl.BlockSpec((s, 128), lambda g: (0, g)),
        out_shape=jax.ShapeDtypeStruct((s, 512), F32),
        scratch_shapes=[pltpu.VMEM((s + 16, 128), F32)],
        compiler_params=_params("arbitrary"),
    )(proj, pw, ps)


def pool_bwd(proj, pw, ps, do, name):
    s = proj.shape[0]

    def body(u_ref, w_ref, s_ref, do_ref, du_ref, dw_ref, ds_ref, pad):
        g = pl.program_id(0)
        u, w, dout = u_ref[...], w_ref[...], do_ref[...]
        cnt = _pool_count(s, g)
        pooled = _pool_windows(u, pad, s, g) / cnt - u
        mixed = _raw(pooled, w, _NN)
        ds_ref[...] = jnp.sum(dout * mixed, axis=0, keepdims=True)
        dmixed = dout * s_ref[...]
        dw_ref[...] = _raw(pooled, dmixed, _TN)
        dpooled = _raw(dmixed, w, _NT)
        dsel = dpooled / cnt
        pad[s:s + 16, :] = jnp.zeros((16, 128), F32)
        cur = jnp.where(g == 3, dsel, 0.0)
        for j, k in ((2, 8), (1, 4), (0, 2)):
            pad[0:s, :] = cur
            cur = cur + pad[k:k + s, :] + jnp.where(g == j, dsel, 0.0)
        pad[0:s, :] = cur
        cur = cur + pad[1:1 + s, :]
        du_ref[...] = (cur - dpooled).astype(BF16)

    return pl.pallas_call(
        body, name=name, grid=(4,),
        in_specs=[pl.BlockSpec((s, 128), lambda g: (0, O_PU // 128 + g)), pl.BlockSpec((None, 128, 128), lambda g: (g, 0, 0)),
                  pl.BlockSpec((1, 128), lambda g: (0, g)), pl.BlockSpec((s, 128), lambda g: (0, g))],
        out_specs=[pl.BlockSpec((s, 128), lambda g: (0, g)), pl.BlockSpec((None, 128, 128), lambda g: (g, 0, 0)),
                   pl.BlockSpec((1, 128), lambda g: (0, g))],
        out_shape=[jax.ShapeDtypeStruct((s, 512), BF16), jax.ShapeDtypeStruct((4, 128, 128), F32),
                   jax.ShapeDtypeStruct((1, 512), F32)],
        scratch_shapes=[pltpu.VMEM((s + 16, 128), F32)],
        compiler_params=_params("arbitrary"),
    )(proj, pw, ps, do)


def _xbc_col(i):
    return jnp.where(i < 2, O_XS // 512 + i, O_BC // 512)


def conv_fwd(proj, cw, cb, name):
    s = proj.shape[0]

    def body(x_ref, w_ref, b_ref, o_ref, pad):
        pad[0:8, :] = jnp.zeros((8, 512), F32)
        pad[8:8 + s, :] = x_ref[...]
        y = b_ref[...] + sum(w_ref[k:k + 1, :] * pad[5 + k:5 + k + s, :] for k in range(4))
        o_ref[...] = y * _sig(y)

    return pl.pallas_call(
        body, name=name, grid=(3,),
        in_specs=[pl.BlockSpec((s, 512), lambda i: (0, _xbc_col(i))), pl.BlockSpec((4, 512), lambda i: (0, i)),
                  pl.BlockSpec((1, 512), lambda i: (0, i))],
        out_specs=pl.BlockSpec((s, 512), lambda i: (0, i)),
        out_shape=jax.ShapeDtypeStruct((s, 1536), F32),
        scratch_shapes=[pltpu.VMEM((s + 8, 512), F32)],
        compiler_params=_params("arbitrary"),
    )(proj, cw, cb)


def conv_bwd(proj, cw, cb, dact, name):
    s = proj.shape[0]

    def body(x_ref, w_ref, b_ref, da_ref, dx_ref, dw_ref, db_ref, pad, pad2):
        pad[0:8, :] = jnp.zeros((8, 512), F32)
        pad[8:8 + s, :] = x_ref[...]
        y = b_ref[...] + sum(w_ref[k:k + 1, :] * pad[5 + k:5 + k + s, :] for k in range(4))
        sg = _sig(y)
        dy = da_ref[...] * (sg * (1.0 + y * (1.0 - sg)))
        db_ref[...] = jnp.sum(dy, axis=0, keepdims=True)
        for k in range(4):
            dw_ref[k:k + 1, :] = jnp.sum(dy * pad[5 + k:5 + k + s, :], axis=0, keepdims=True)
        pad2[s:s + 8, :] = jnp.zeros((8, 512), F32)
        pad2[0:s, :] = dy
        dx_ref[...] = sum(w_ref[k:k + 1, :] * pad2[3 - k:3 - k + s, :] for k in range(4)).astype(BF16)

    return pl.pallas_call(
        body, name=name, grid=(3,),
        in_specs=[pl.BlockSpec((s, 512), lambda i: (0, _xbc_col(i))), pl.BlockSpec((4, 512), lambda i: (0, i)),
                  pl.BlockSpec((1, 512), lambda i: (0, i)), pl.BlockSpec((s, 512), lambda i: (0, i))],
        out_specs=[pl.BlockSpec((s, 512), lambda i: (0, i)), pl.BlockSpec((4, 512), lambda i: (0, i)),
                   pl.BlockSpec((1, 512), lambda i: (0, i))],
        out_shape=[jax.ShapeDtypeStruct((s, 1536), BF16), jax.ShapeDtypeStruct((4, 1536), F32),
                   jax.ShapeDtypeStruct((1, 1536), F32)],
        scratch_shapes=[pltpu.VMEM((s + 8, 512), F32), pltpu.VMEM((s + 8, 512), F32)],
        compiler_params=_params("arbitrary"),
    )(proj, cw, cb, dact)


def _ssd_chunk(xt, dtr, dtc, bm, cm, hprev, alog, dbias, dskip):
    ln = 128
    a = -jnp.exp(alog)
    dt_r = softplus(dtr + dbias)
    da_r = dt_r * a
    da_c = softplus(dtc + dbias) * a
    li = lax.broadcasted_iota(jnp.int32, (1, ln, ln), 1)
    si = lax.broadcasted_iota(jnp.int32, (1, ln, ln), 2)
    causal = si <= li
    acs_c = jnp.sum(jnp.where(causal, da_r, 0.0), axis=2, keepdims=True)
    acs_r = jnp.sum(jnp.where(li <= si, da_c, 0.0), axis=1, keepdims=True)
    acs_last = jnp.sum(da_r, axis=2, keepdims=True)
    decay = jnp.exp(jnp.where(causal, acs_c - acs_r, -jnp.inf))
    m = mm_nt(cm, bm)[None] * decay
    xdt = xt * dt_r
    y_diag = bmm_nt(xdt, m)
    bb = jnp.broadcast_to(bm[None], (8, ln, ln))
    cc = jnp.broadcast_to(cm[None], (8, ln, ln))
    states = bmm_nn(xdt * jnp.exp(acs_last - acs_r), bb)
    y_off = bmm_nt(hprev, cc) * jnp.exp(acs_r)
    hnew = hprev * jnp.exp(acs_last) + states
    return y_diag + y_off + xt * dskip, hnew


def _ssd_specs(nc, rev):
    cix = (lambda c: nc - 1 - c) if rev else (lambda c: c)
    hv = pl.BlockSpec((8, 1, 1), lambda g, c: (g, 0, 0))
    return [pl.BlockSpec((8, 64, 128), lambda g, c: (g, 0, cix(c))), pl.BlockSpec((8, 1, 128), lambda g, c: (g, 0, cix(c))),
            pl.BlockSpec((8, 128, 1), lambda g, c: (g, cix(c), 0)), pl.BlockSpec((128, 128), lambda g, c: (cix(c), 8 + g)),
            pl.BlockSpec((128, 128), lambda g, c: (cix(c), 10 + g))], hv, cix


def ssd_fwd(xt, dtr, dtc, xbc, alog, dbias, dskip, name):
    s = xt.shape[2]
    nc = s // 128
    specs, hv, _ = _ssd_specs(nc, False)

    def body(x_ref, dr_ref, dc_ref, b_ref, c_ref, al_ref, db_ref, dk_ref, y_ref, hs_ref, h_scr):
        @pl.when(pl.program_id(1) == 0)
        def _():
            h_scr[...] = jnp.zeros_like(h_scr)
        hp = h_scr[...]
        hs_ref[...] = hp
        y, hn = _ssd_chunk(x_ref[...], dr_ref[...], dc_ref[...], b_ref[...], c_ref[...], hp,
                           al_ref[...], db_ref[...], dk_ref[...])
        y_ref[...] = y
        h_scr[...] = hn

    return pl.pallas_call(
        body, name=name, grid=(2, nc), in_specs=specs + [hv, hv, hv],
        out_specs=[pl.BlockSpec((8, 64, 128), lambda g, c: (g, 0, c)),
                   pl.BlockSpec((None, None, 8, 64, 128), lambda g, c: (g, c, 0, 0, 0))],
        out_shape=[jax.ShapeDtypeStruct((16, 64, s), F32), jax.ShapeDtypeStruct((2, nc, 8, 64, 128), F32)],
        scratch_shapes=[pltpu.VMEM((8, 64, 128), F32)],
        compiler_params=_params("arbitrary", "arbitrary"),
    )(xt, dtr, dtc, xbc, xbc, alog, dbias, dskip)


def ssd_bwd(xt, dtr, dtc, xbc, alog, dbias, dskip, hs, dyt, name):
    s = xt.shape[2]
    nc = s // 128
    specs, hv, cix = _ssd_specs(nc, True)

    def body(x_ref, dr_ref, dc_ref, b_ref, c_ref, al_ref, db_ref, dk_ref, hs_ref, dy_ref,
             dx_out, ddr_out, ddc_out, dbm_out, dcm_out, dal_out, ddb_out, ddk_out, dh_scr):
        @pl.when(pl.program_id(1) == 0)
        def _():
            dh_scr[...] = jnp.zeros_like(dh_scr)
            dal_out[...] = jnp.zeros_like(dal_out)
            ddb_out[...] = jnp.zeros_like(ddb_out)
            ddk_out[...] = jnp.zeros_like(ddk_out)
        _, vjp = jax.vjp(_ssd_chunk, x_ref[...], dr_ref[...], dc_ref[...], b_ref[...], c_ref[...], hs_ref[...],
                         al_ref[...], db_ref[...], dk_ref[...])
        g = vjp((dy_ref[...], dh_scr[...]))
        dx_out[...] = g[0]
        ddr_out[...] = g[1]
        ddc_out[...] = g[2]
        dbm_out[...] = g[3]
        dcm_out[...] = g[4]
        dh_scr[...] = g[5]
        dal_out[...] += g[6]
        ddb_out[...] += g[7]
        ddk_out[...] += g[8]

    return pl.pallas_call(
        body, name=name, grid=(2, nc),
        in_specs=specs + [hv, hv, hv, pl.BlockSpec((None, None, 8, 64, 128), lambda g, c: (g, cix(c), 0, 0, 0)),
                          pl.BlockSpec((8, 64, 128), lambda g, c: (g, 0, cix(c)))],
        out_specs=[pl.BlockSpec((8, 64, 128), lambda g, c: (g, 0, cix(c))), pl.BlockSpec((8, 1, 128), lambda g, c: (g, 0, cix(c))),
                   pl.BlockSpec((8, 128, 1), lambda g, c: (g, cix(c), 0)), pl.BlockSpec((128, 128), lambda g, c: (cix(c), g)),
                   pl.BlockSpec((128, 128), lambda g, c: (cix(c), g)), hv, hv, hv],
        out_shape=[jax.ShapeDtypeStruct((16, 64, s), F32), jax.ShapeDtypeStruct((16, 1, s), F32),
                   jax.ShapeDtypeStruct((16, s, 1), F32), jax.ShapeDtypeStruct((s, 256), F32),
                   jax.ShapeDtypeStruct((s, 256), F32)] + [jax.ShapeDtypeStruct((16, 1, 1), F32)] * 3,
        scratch_shapes=[pltpu.VMEM((8, 64, 128), F32)],
        compiler_params=_params("arbitrary", "arbitrary"),
    )(xt, dtr, dtc, xbc, xbc, alog, dbias, dskip, hs, dyt)


def _merge(oa, ob, y, z, gla, glb, glc, x, g1, nw, ea, eb, ec, eo, wba, wbb, wbc, wout):
    gated = y * (z * _sig(z))
    sq = gated * gated
    left = lax.broadcasted_iota(jnp.int32, (1, D), 1) < 512
    ms0 = jnp.sum(jnp.where(left, sq, 0.0), axis=-1, keepdims=True) / 512.0
    ms1 = jnp.sum(jnp.where(left, 0.0, sq), axis=-1, keepdims=True) / 512.0
    oc = gated * jnp.where(left, lax.rsqrt(ms0 + EPS), lax.rsqrt(ms1 + EPS)) * nw
    ya, yb, yc = mm_nc(oa, wba) + ea, mm_nc(ob, wbb) + eb, mm_nc(oc, wbc) + ec
    merged = _sig(gla) * ya + _sig(glb) * yb + _sig(glc) * yc
    x1 = x + g1 * (mm_nc(merged, wout) + eo)
    return x1, (oc, merged)


def _merge_specs(tr):
    row = lambda w: pl.BlockSpec((tr, w), lambda i: (i, 0))
    acts = [row(D), row(512), row(D), pl.BlockSpec((tr, D), lambda i: (i, O_Z // D)),
            pl.BlockSpec((tr, 3 * D), lambda i: (i, 0)), row(D), pl.BlockSpec((8, D), lambda i: (0, 0))]
    cst = lambda r: pl.BlockSpec((r, D), lambda i: (0, 0), **CONST)
    return acts, [cst(D), cst(512), cst(D), cst(D)], row


def merge_fwd(oa, ob, y, proj, x, mvec, wba, wbb, wbc, wout, name):
    s = x.shape[0]
    tr = _pick(s, 256)
    acts, wts, row = _merge_specs(tr)

    def body(oa_ref, ob_ref, y_ref, z_ref, gl_ref, x_ref, mv_ref, wba_ref, wbb_ref, wbc_ref, wout_ref, o_ref):
        zero = jnp.zeros((1, D), F32)
        x1, _ = _merge(oa_ref[...], ob_ref[...], y_ref[...], z_ref[...], gl_ref[:, 0:D], gl_ref[:, D:2 * D],
                       gl_ref[:, 2 * D:3 * D], x_ref[...], mv_ref[0:1, :], mv_ref[1:2, :], zero, zero, zero, zero,
                       wba_ref[...], wbb_ref[...], wbc_ref[...], wout_ref[...])
        o_ref[...] = x1

    return pl.pallas_call(
        body, name=name, grid=(s // tr,), in_specs=acts + wts, out_specs=row(D),
        out_shape=jax.ShapeDtypeStruct((s, D), F32), compiler_params=_params("arbitrary"),
    )(oa, ob, y, proj, proj, x, mvec, wba, wbb, wbc, wout)


def merge_bwd(oa, ob, y, proj, x, mvec, wba, wbb, wbc, wout, dx1, name):
    s = x.shape[0]
    tr = _pick(s, 128)
    acts, wts, row = _merge_specs(tr)

    def body(oa_ref, ob_ref, y_ref, z_ref, gl_ref, x_ref, mv_ref, wba_ref, wbb_ref, wbc_ref, wout_ref, dx1_ref,
             doa_o, dob_o, dy_o, dz_o, dgl_o, dx_o, dmv_o, dya_o, dyb_o, dyc_o, dpre_o, oc_o, mg_o):
        zero = jnp.zeros((tr, D), F32)
        wts_ = (wba_ref[...], wbb_ref[...], wbc_ref[...], wout_ref[...])
        f = lambda *a: _merge(*a, *wts_)
        _, vjp, (oc, merged) = jax.vjp(
            f, oa_ref[...], ob_ref[...], y_ref[...], z_ref[...], gl_ref[:, 0:D], gl_ref[:, D:2 * D],
            gl_ref[:, 2 * D:3 * D], x_ref[...], mv_ref[0:1, :], mv_ref[1:2, :], zero, zero, zero, zero, has_aux=True)
        g = vjp(dx1_ref[...])
        doa_o[...] = g[0]
        dob_o[...] = g[1]
        dy_o[...] = g[2]
        dz_o[...] = g[3].astype(BF16)
        dgl_o[:, 0:D] = g[4].astype(BF16)
        dgl_o[:, D:2 * D] = g[5].astype(BF16)
        dgl_o[:, 2 * D:3 * D] = g[6].astype(BF16)
        dx_o[...] = g[7]

        @pl.when(pl.program_id(0) == 0)
        def _():
            dmv_o[...] = jnp.zeros_like(dmv_o)

        dmv_o[0:1, :] += g[8]
        dmv_o[1:2, :] += g[9]
        dya_o[...] = g[10].astype(BF16)
        dyb_o[...] = g[11].astype(BF16)
        dyc_o[...] = g[12].astype(BF16)
        dpre_o[...] = g[13].astype(BF16)
        oc_o[...] = oc.astype(BF16)
        mg_o[...] = merged.astype(BF16)

    sd = lambda w, dt: jax.ShapeDtypeStruct((s, w), dt)
    return pl.pallas_call(
        body, name=name, grid=(s // tr,), in_specs=acts + wts + [row(D)],
        out_specs=[row(D), row(512), row(D), row(D), row(3 * D), row(D), pl.BlockSpec((8, D), lambda i: (0, 0))] + [row(D)] * 6,
        out_shape=[sd(D, F32), sd(512, F32), sd(D, F32), sd(D, BF16), sd(3 * D, BF16), sd(D, F32),
                   jax.ShapeDtypeStruct((8, D), F32)] + [sd(D, BF16)] * 6,
        compiler_params=_params("arbitrary"),
    )(oa, ob, y, proj, proj, x, mvec, wba, wbb, wbc, wout, dx1)


def _conv3(u_scr, w_ref, first, rows, lanes):
    return sum(w_ref[k:k + 1, :] * u_scr[first + k:first + k + rows, lanes] for k in range(3))


def _ffn_tile_specs(tf, tile):
    def at(rows, off):
        return pl.BlockSpec((rows, tf), lambda *g: (0, off + tile(*g)))
    return [at(D, 0), at(D, FFN_NT), at(3, 0), at(3, FFN_NT), at(1, 0), at(1, FFN_NT)]


def ffn_fwd(x1, fvec, wup, cw, cb, wdn, name):
    s = x1.shape[0]
    tr, tf = _pick(s, 512), FFN_TILE
    lg, lv = slice(0, tf), slice(tf, 2 * tf)

    def body(x_ref, v_ref, wg_ref, wv_ref, cwg_ref, cwv_ref, cbg_ref, cbv_ref, wd_ref, x2_ref, h_ref, pre_ref,
             h_scr, u_scr, acc):
        i, t = pl.program_id(0), pl.program_id(1)

        @pl.when(t == 0)
        def _():
            @pl.when(i == 0)
            def _():
                h_scr[0:16, :] = jnp.zeros((16, D), BF16)

            @pl.when(i > 0)
            def _():
                h_scr[0:16, :] = h_scr[tr:tr + 16, :]

            h = (_rms(x_ref[...], v_ref[0:1, :], D) * (1.0 + v_ref[2:3, :]) + v_ref[1:2, :]).astype(BF16)
            h_scr[16:16 + tr, :] = h
            h_ref[...] = h
            acc[...] = jnp.zeros_like(acc)

        u_scr[:, lg] = jnp.dot(h_scr[...], wg_ref[...], preferred_element_type=F32)
        u_scr[:, lv] = jnp.dot(h_scr[...], wv_ref[...], preferred_element_type=F32)
        cg = _conv3(u_scr, cwg_ref, 14, tr, lg) + cbg_ref[...]
        cval = _conv3(u_scr, cwv_ref, 14, tr, lv) + cbv_ref[...]
        acc[...] += _raw(cg * _sig(cg) * cval, wd_ref[...], _NN)

        @pl.when(t == FFN_NT - 1)
        def _():
            pre_ref[...] = acc[...]
            x2_ref[...] = x_ref[...] + v_ref[3:4, :] * acc[...]

    row = pl.BlockSpec((tr, D), lambda i, t: (i, 0))
    return pl.pallas_call(
        body, name=name, grid=(s // tr, FFN_NT),
        in_specs=[row, pl.BlockSpec((8, D), lambda i, t: (0, 0))] + _ffn_tile_specs(tf, lambda i, t: t)
                 + [pl.BlockSpec((tf, D), lambda i, t: (t, 0))],
        out_specs=[row, row, row],
        out_shape=[jax.ShapeDtypeStruct((s, D), F32), jax.ShapeDtypeStruct((s, D), BF16), jax.ShapeDtypeStruct((s, D), F32)],
        scratch_shapes=[pltpu.VMEM((tr + 16, D), BF16), pltpu.VMEM((tr + 16, 2 * tf), F32), pltpu.VMEM((tr, D), F32)],
        compiler_params=_params("arbitrary", "arbitrary"),
    )(x1, fvec, wup, wup, cw, cw, cb, cb, wdn)


def ffn_bwd(h2, dx2, fvec, wup, cw, cb, wdn, name):
    s = h2.shape[0]
    tr, tf = _pick(s, 512), FFN_TILE
    ni, nb = s // tr, s // 16
    lg, lv = slice(0, tf), slice(tf, 2 * tf)

    def body(hp_ref, hm_ref, hn_ref, dm_ref, dn_ref, v_ref, wg_ref, wv_ref, cwg_ref, cwv_ref, cbg_ref, cbv_ref, wd_ref,
             dup_ref, act_ref, dcw_ref, u_scr, dc_scr):
        i = pl.program_id(1)
        hfull = jnp.concatenate([jnp.where(i > 0, hp_ref[...], jnp.zeros((16, D), BF16)), hm_ref[...],
                                 jnp.where(i < ni - 1, hn_ref[...], jnp.zeros((16, D), BF16))], axis=0)
        u_scr[:, lg] = jnp.dot(hfull, wg_ref[...], preferred_element_type=F32)
        u_scr[:, lv] = jnp.dot(hfull, wv_ref[...], preferred_element_type=F32)
        cg = _conv3(u_scr, cwg_ref, 14, tr + 16, lg) + cbg_ref[...]
        cval = _conv3(u_scr, cwv_ref, 14, tr + 16, lv) + cbv_ref[...]
        g2 = v_ref[3:4, :]
        dpre = jnp.concatenate([dm_ref[...] * g2, jnp.where(i < ni - 1, dn_ref[...], 0.0) * g2], axis=0)
        dact = _raw(dpre, wd_ref[...], _NT)
        sg = _sig(cg)
        sl = cg * sg
        dc_scr[:, lg] = dact * cval * (sg * (1.0 + cg * (1.0 - sg)))
        dc_scr[:, lv] = dact * sl
        act_ref[...] = (sl * cval)[0:tr, :].astype(BF16)

        @pl.when(i == 0)
        def _():
            dcw_ref[...] = jnp.zeros_like(dcw_ref)

        for half, lanes, cw_ref in ((0, lg, cwg_ref), (1, lv, cwv_ref)):
            dup_ref[half] = sum(cw_ref[k:k + 1, :] * dc_scr[2 - k:2 - k + tr, lanes] for k in range(3)).astype(BF16)
            dcm = dc_scr[0:tr, lanes]
            for k in range(3):
                dcw_ref[half, k:k + 1, :] += jnp.sum(dcm * u_scr[14 + k:14 + k + tr, lanes], axis=0, keepdims=True)
            dcw_ref[half, 3:4, :] += jnp.sum(dcm, axis=0, keepdims=True)

    r16 = tr // 16
    prev = lambda t, i: (jnp.maximum(i * r16 - 1, 0), 0)
    nxt = lambda t, i: (jnp.minimum((i + 1) * r16, nb - 1), 0)
    main = lambda t, i: (i, 0)
    return pl.pallas_call(
        body, name=name, grid=(FFN_NT, ni),
        in_specs=[pl.BlockSpec((16, D), prev), pl.BlockSpec((tr, D), main), pl.BlockSpec((16, D), nxt),
                  pl.BlockSpec((tr, D), main), pl.BlockSpec((16, D), nxt), pl.BlockSpec((8, D), lambda t, i: (0, 0))]
                 + _ffn_tile_specs(tf, lambda t, i: t) + [pl.BlockSpec((tf, D), lambda t, i: (t, 0))],
        out_specs=[pl.BlockSpec((2, tr, tf), lambda t, i: (0, i, t)), pl.BlockSpec((tr, tf), lambda t, i: (i, t)),
                   pl.BlockSpec((2, 8, tf), lambda t, i: (0, 0, t))],
        out_shape=[jax.ShapeDtypeStruct((2, s, FFN), BF16), jax.ShapeDtypeStruct((s, FFN), BF16),
                   jax.ShapeDtypeStruct((2, 8, FFN), F32)],
        scratch_shapes=[pltpu.VMEM((tr + 32, 2 * tf), F32), pltpu.VMEM((tr + 16, 2 * tf), F32)],
        compiler_params=_params("arbitrary", "arbitrary"),
    )(h2, h2, h2, dx2, dx2, fvec, wup, wup, cw, cw, cb, cb, wdn)


def loss_head(y, target):
    s = y.shape[0]
    tr = _pick(s, 512)

    def body(y_ref, t_ref, dx_ref, l_ref):
        @pl.when(pl.program_id(0) == 0)
        def _():
            l_ref[...] = jnp.zeros_like(l_ref)
        err = y_ref[...] - t_ref[...]
        dx_ref[...] = err / float(D)
        l_ref[...] += 0.5 * jnp.sum(jnp.sum(err * err, axis=-1, keepdims=True) / float(D), axis=0, keepdims=True)

    row = pl.BlockSpec((tr, D), lambda i: (i, 0))
    return pl.pallas_call(
        body, name="loss_head", grid=(s // tr,), in_specs=[row, row],
        out_specs=[row, pl.BlockSpec((8, 128), lambda i: (0, 0))],
        out_shape=[jax.ShapeDtypeStruct((s, D), F32), jax.ShapeDtypeStruct((8, 128), F32)],
        compiler_params=_params("arbitrary"),
    )(y, target)


def adamw(parts, w, m, v, name):
    nseg = len(parts)
    p, r, c = parts[0].shape
    tr = _pick(r, 256, 8)
    ni = r // tr

    def body(*refs):
        p_refs = refs[:nseg]
        w_ref, m_ref, v_ref, g_out, d_out, m_out, v_out, g_scr = refs[nseg:]
        for q in range(nseg):
            @pl.when(pl.program_id(0) == q)
            def _(q=q):
                g = p_refs[q][0].astype(F32)
                for j in range(1, p):
                    g = g + p_refs[q][j].astype(F32)
                g_scr[...] = g
        g = g_scr[...]
        mn = B1 * m_ref[...] + (1.0 - B1) * g
        vn = B2 * v_ref[...] + (1.0 - B2) * (g * g)
        m_hat = mn / (1.0 - B1 ** STEP)
        v_hat = vn / (1.0 - B2 ** STEP)
        g_out[...] = g
        d_out[...] = -LR * (m_hat / (jnp.sqrt(v_hat) + ADAM_EPS) + WD * w_ref[...])
        m_out[...] = mn
        v_out[...] = vn

    row = pl.BlockSpec((tr, c), lambda l, i: (l * ni + i, 0))
    part = lambda q: pl.BlockSpec((p, tr, c), lambda l, i: (0, jnp.clip((l - q) * ni + i, 0, ni - 1), 0))
    return pl.pallas_call(
        body, name=name, grid=(nseg, ni), in_specs=[part(q) for q in range(nseg)] + [row, row, row],
        out_specs=[row] * 4, out_shape=[jax.ShapeDtypeStruct((nseg * r, c), F32)] * 4,
        scratch_shapes=[pltpu.VMEM((tr, c), F32)],
        compiler_params=_params("arbitrary", "arbitrary"),
    )(*parts, w, m, v)


def _padc(a, n):
    return jnp.pad(a, [(0, 0)] * (a.ndim - 1) + [(0, n - a.shape[-1])])


def _swap16(a):
    return jnp.concatenate([a[..., 16:32], a[..., 0:16]], axis=-1)


def _win_layout(w):
    kr = w[:, 640:672]
    return jnp.concatenate([w[:, 3760:6832], w[:, 2208:3232], w[:, 1184:2208], w[:, 672:1184], w[:, 3232:3744],
                            w[:, 384:640], _padc(kr, 128), _padc(_swap16(kr), 128), _padc(w[:, 3744:3760], 128),
                            jnp.zeros((w.shape[0], 128), w.dtype), w[:, 0:384]], axis=1)


def _win_unlayout(g):
    kr = g[:, O_KR:O_KR + 32] + _swap16(g[:, O_KRS:O_KRS + 32])
    return jnp.concatenate([g[:, O_QL:O_QL + 384], g[:, O_CKV:O_CKV + 256], kr, g[:, O_PU:O_PU + 512], g[:, O_Z:O_Z + D],
                            g[:, O_XS:O_XS + D], g[:, O_BC:O_BC + 512], g[:, O_DT:O_DT + 16], g[:, O_G:O_G + 3 * D]], axis=1)


def _wq_layout(w):
    w = w.reshape(384, HEADS, 96).transpose(1, 0, 2)
    rope = w[:, :, 64:96]
    return jnp.concatenate([_padc(w[:, :, 0:64], 128), _padc(rope, 128), _padc(_swap16(rope), 128)], axis=2)


def _wq_unlayout(g):
    rope = g[:, :, 128:160] + _swap16(g[:, :, 256:288])
    return jnp.concatenate([g[:, :, 0:64], rope], axis=2).transpose(1, 0, 2).reshape(384, HEADS * 96)


def _wkv_layout(w):
    w = w.reshape(256, HEADS, 128).transpose(1, 0, 2)
    return jnp.concatenate([_padc(w[:, :, 0:64], 128), _padc(w[:, :, 64:128], 128)], axis=2)


def _wkv_unlayout(g):
    return jnp.concatenate([g[:, :, 0:64], g[:, :, 128:192]], axis=2).transpose(1, 0, 2).reshape(256, HEADS * 128)


def _wba_layout(w):
    return jnp.pad(w.reshape(HEADS, 64, D), ((0, 0), (0, 64), (0, 0))).reshape(HEADS * 128, D)


def _rows8(rows, width):
    out = jnp.stack([_padc(r.astype(F32), width) for r in rows])
    return jnp.pad(out, ((0, 8 - out.shape[0]), (0, 0)))


def _mla_vec(qa, kva, qn, kn):
    def row(n):
        return jnp.concatenate([_padc(n[0:64], 128), _padc(n[64:96], 128), _padc(_swap16(n[64:96]), 128)])
    return _rows8([qa, kva, row(qn), row(kn)], 512)


def _mla_unvec(g):
    def un(r):
        return jnp.concatenate([r[0:64], r[128:160] + _swap16(r[256:288])])
    return g[0, 0:384], g[1, 0:256], un(g[2]), un(g[3])


SMALL = (("ada_b", (6 * D,)), ("norm1_w", (D,)), ("q_a_norm", (384,)), ("kv_a_norm", (256,)), ("q_norm", (96,)),
         ("k_norm", (96,)), ("pool_w", (4, 128, 128)), ("pool_scale", (512,)), ("ssd_conv_b", (1536,)),
         ("ssd_dt_bias", (16,)), ("ssd_a_log", (16,)), ("ssd_d", (16,)), ("ssd_norm_w", (D,)), ("norm2_w", (D,)),
         ("ffn_conv_b", (2 * FFN,)), ("ssd_conv_w", (4, 1536)), ("ffn_conv_w", (3, 2 * FFN)))
SMALL_REPL = SMALL[:15]
SMALL_ROWS = 208


def _pack(per_layer, names):
    flat = jnp.concatenate([per_layer[l][n].reshape(-1).astype(F32) for n, _ in names for l in range(LAYERS)])
    return jnp.pad(flat, (0, SMALL_ROWS * D - flat.shape[0])).reshape(SMALL_ROWS, D)


def _unpack(packed, names):
    flat, out, off = packed.reshape(-1), {}, 0
    for n, shp in names:
        size = LAYERS * math.prod(shp)
        out[n] = flat[off:off + size].reshape((LAYERS,) + shp)
        off += size
    return out


BIG = ("w_in", "w_q_b", "w_kv_b", "w_branch", "w_out", "ffn_up", "ffn_down")
COL_SHARDED = ("w_in", "w_q_b", "w_kv_b", "ffn_up")


def _gathered_full(g, name):
    if name in COL_SHARDED:
        return g.transpose(1, 0, 2).reshape(g.shape[1], NDEV * g.shape[2])
    return g.reshape(NDEV * g.shape[1], g.shape[2])


def _to_shards(full, name):
    if name in COL_SHARDED:
        r, c = full.shape
        return full.reshape(r, NDEV, c // NDEV).transpose(1, 0, 2).astype(BF16)
    r, c = full.shape
    return full.reshape(NDEV, r // NDEV, c).astype(BF16)


def _layer_fwd(x, lw, mod, cos2, sin2, l):
    sh1, sc1, g1, sh2, sc2, g2 = [mod[j * D:(j + 1) * D] for j in range(6)]
    vec1 = _rows8([lw["norm1_w"], sh1, sc1], D)
    proj, h1 = norm_proj_fwd(x, vec1, lw["win"], f"inproj_fwd{l}")
    q, k, v = mla_pre_fwd(proj, lw["wq"], lw["wkv"], lw["mla_vec"], cos2, sin2, f"mla_pre_fwd{l}")
    oa = mla_attn_fwd(q, k, v, f"mla_attn_fwd{l}")
    ob = pool_fwd(proj, lw["pool_w"], lw["pool_scale"].reshape(1, 512), f"pool_fwd{l}")
    xbc = conv_fwd(proj, lw["ssd_conv_w"], lw["ssd_conv_b"].reshape(1, 1536), f"conv_fwd{l}")
    s = x.shape[0]
    xt = xbc[:, 0:D].reshape(s, 16, 64).transpose(1, 2, 0)
    dt = proj[:, O_DT:O_DT + 16].T
    dtr, dtc = dt[:, None, :], dt[:, :, None]
    hv = lambda a: a.reshape(16, 1, 1)
    yt, hs = ssd_fwd(xt, dtr, dtc, xbc, hv(lw["ssd_a_log"]), hv(lw["ssd_dt_bias"]), hv(lw["ssd_d"]), f"ssd_fwd{l}")
    y = yt.transpose(2, 0, 1).reshape(s, D)
    mvec = _rows8([g1, lw["ssd_norm_w"]], D)
    x1 = merge_fwd(oa, ob, y, proj, x, mvec, lw["wba"], lw["wbb"], lw["wbc"], lw["wout"], f"merge_fwd{l}")
    fvec = _rows8([lw["norm2_w"], sh2, sc2, g2], D)
    x2, h2, pre = ffn_fwd(x1, fvec, lw["wup"], lw["ffn_conv_w"], lw["ffn_conv_b"].reshape(1, 2 * FFN), lw["wdn"],
                          f"ffn_fwd{l}")
    saved = dict(x=x, vec1=vec1, proj=proj, h1=h1, q=q, k=k, v=v, oa=oa, ob=ob, xbc=xbc, xt=xt, dtr=dtr, dtc=dtc,
                 hs=hs, y=y, mvec=mvec, x1=x1, fvec=fvec, h2=h2, pre=pre)
    return x2, saved


def _layer_bwd(dx2, lw, sv, cos2, sin2, l):
    s = dx2.shape[0]
    grads, small = {}, {}
    dup, act, dcw = ffn_bwd(sv["h2"], dx2, sv["fvec"], lw["wup"], lw["ffn_conv_w"], lw["ffn_conv_b"].reshape(1, 2 * FFN),
                            lw["wdn"], f"ffn_bwd{l}")
    grads["ffn_down"] = tn_matmul(act, dx2, f"dw_down{l}", scale=sv["fvec"][3:4])
    grads["ffn_up"] = tn_matmul(sv["h2"], dup, f"dw_up{l}")
    dx1, dfvec = norm_proj_bwd(sv["x1"], sv["fvec"], dup, lw["wup"], dx2, sv["pre"], f"ffn_norm_bwd{l}")
    small["ffn_conv_w"] = jnp.concatenate([dcw[0, 0:3], dcw[1, 0:3]], axis=1)
    small["ffn_conv_b"] = jnp.concatenate([dcw[0, 3], dcw[1, 3]])
    small["norm2_w"] = dfvec[0]
    (doa, dob, dy, dz, dgl, dx, dmvec, dya, dyb, dyc, dpre, oc, merged) = merge_bwd(
        sv["oa"], sv["ob"], sv["y"], sv["proj"], sv["x"], sv["mvec"], lw["wba"], lw["wbb"], lw["wbc"], lw["wout"], dx1,
        f"merge_bwd{l}")
    dwba = tn_matmul(sv["oa"], dya, f"dw_ba{l}").reshape(HEADS, 128, D)[:, 0:64].reshape(512, D)
    grads["w_branch"] = jnp.concatenate([dwba, tn_matmul(sv["ob"], dyb, f"dw_bb{l}"), tn_matmul(oc, dyc, f"dw_bc{l}")])
    grads["w_out"] = tn_matmul(merged, dpre, f"dw_out{l}")
    small["ssd_norm_w"] = dmvec[1]
    dyt = dy.reshape(s, 16, 64).transpose(1, 2, 0)
    hv = lambda a: a.reshape(16, 1, 1)
    dxt, ddtr, ddtc, dbm, dcm, dal, ddb, ddk = ssd_bwd(
        sv["xt"], sv["dtr"], sv["dtc"], sv["xbc"], hv(lw["ssd_a_log"]), hv(lw["ssd_dt_bias"]), hv(lw["ssd_d"]), sv["hs"],
        dyt, f"ssd_bwd{l}")
    small["ssd_a_log"], small["ssd_dt_bias"], small["ssd_d"] = dal.reshape(16), ddb.reshape(16), ddk.reshape(16)
    dact = jnp.concatenate([dxt.transpose(2, 0, 1).reshape(s, D), dbm, dcm], axis=1)
    dxbc, dscw, dscb = conv_bwd(sv["proj"], lw["ssd_conv_w"], lw["ssd_conv_b"].reshape(1, 1536), dact, f"conv_bwd{l}")
    small["ssd_conv_w"], small["ssd_conv_b"] = dscw, dscb.reshape(1536)
    ddt = (ddtr[:, 0, :] + ddtc[:, :, 0]).T
    du, dpw, dps = pool_bwd(sv["proj"], lw["pool_w"], lw["pool_scale"].reshape(1, 512), dob, f"pool_bwd{l}")
    small["pool_w"], small["pool_scale"] = dpw, dps.reshape(512)
    dq, dk, dv = mla_attn_bwd(sv["q"], sv["k"], sv["v"], doa, f"mla_attn_bwd{l}")
    dql, dckv, dkr, dkrs, dwq, dwkv, dmv = mla_pre_bwd(sv["proj"], lw["wq"], lw["wkv"], lw["mla_vec"], cos2, sin2,
                                                       dq, dk, dv, f"mla_pre_bwd{l}")
    grads["w_q_b"], grads["w_kv_b"] = _wq_unlayout(dwq), _wkv_unlayout(dwkv)
    small["q_a_norm"], small["kv_a_norm"], small["q_norm"], small["k_norm"] = _mla_unvec(dmv)
    dproj = jnp.concatenate([dgl, dxbc[:, 0:D], dz, du, dxbc[:, D:1536], dckv, dkr, dkrs,
                             _padc(ddt, 128).astype(BF16), jnp.zeros((s, 128), BF16), dql], axis=1)
    grads["w_in"] = _win_unlayout(tn_matmul(sv["h1"], dproj, f"dw_in{l}"))
    dx0, dvec1 = norm_proj_bwd(sv["x"], sv["vec1"], dproj, lw["win"], dx, None, f"inproj_bwd{l}")
    small["norm1_w"] = dvec1[0]
    small["ada_b"] = jnp.concatenate([dvec1[1], dvec1[2], dmvec[0], dfvec[1], dfvec[2], dfvec[3]])
    return dx0, grads, small


def kernel(x, c, positions, ada_w, ada_b, norm1_w, w_in, q_a_norm, w_q_b, kv_a_norm, w_kv_b, q_norm, k_norm, pool_w, pool_scale, ssd_conv_w, ssd_conv_b, ssd_dt_bias, ssd_a_log, ssd_d, ssd_norm_w, w_branch, w_out, norm2_w, ffn_up, ffn_conv_w, ffn_conv_b, ffn_down, loss_target, m_ada_w, m_ada_b, m_norm1_w, m_w_in, m_q_a_norm, m_w_q_b, m_kv_a_norm, m_w_kv_b, m_q_norm, m_k_norm, m_pool_w, m_pool_scale, m_ssd_conv_w, m_ssd_conv_b, m_ssd_dt_bias, m_ssd_a_log, m_ssd_d, m_ssd_norm_w, m_w_branch, m_w_out, m_norm2_w, m_ffn_up, m_ffn_conv_w, m_ffn_conv_b, m_ffn_down, v_ada_w, v_ada_b, v_norm1_w, v_w_in, v_q_a_norm, v_w_q_b, v_kv_a_norm, v_w_kv_b, v_q_norm, v_k_norm, v_pool_w, v_pool_scale, v_ssd_conv_w, v_ssd_conv_b, v_ssd_dt_bias, v_ssd_a_log, v_ssd_d, v_ssd_norm_w, v_w_branch, v_w_out, v_norm2_w, v_ffn_up, v_ffn_conv_w, v_ffn_conv_b, v_ffn_down):
    p = dict(ada_w=ada_w, ada_b=ada_b, norm1_w=norm1_w, w_in=w_in, q_a_norm=q_a_norm, w_q_b=w_q_b, kv_a_norm=kv_a_norm,
             w_kv_b=w_kv_b, q_norm=q_norm, k_norm=k_norm, pool_w=pool_w, pool_scale=pool_scale, ssd_conv_w=ssd_conv_w,
             ssd_conv_b=ssd_conv_b, ssd_dt_bias=ssd_dt_bias, ssd_a_log=ssd_a_log, ssd_d=ssd_d, ssd_norm_w=ssd_norm_w,
             w_branch=w_branch, w_out=w_out, norm2_w=norm2_w, ffn_up=ffn_up, ffn_conv_w=ffn_conv_w, ffn_conv_b=ffn_conv_b,
             ffn_down=ffn_down)
    mom = dict(ada_w=m_ada_w, ada_b=m_ada_b, norm1_w=m_norm1_w, w_in=m_w_in, q_a_norm=m_q_a_norm, w_q_b=m_w_q_b,
               kv_a_norm=m_kv_a_norm, w_kv_b=m_w_kv_b, q_norm=m_q_norm, k_norm=m_k_norm, pool_w=m_pool_w,
               pool_scale=m_pool_scale, ssd_conv_w=m_ssd_conv_w, ssd_conv_b=m_ssd_conv_b, ssd_dt_bias=m_ssd_dt_bias,
               ssd_a_log=m_ssd_a_log, ssd_d=m_ssd_d, ssd_norm_w=m_ssd_norm_w, w_branch=m_w_branch, w_out=m_w_out,
               norm2_w=m_norm2_w, ffn_up=m_ffn_up, ffn_conv_w=m_ffn_conv_w, ffn_conv_b=m_ffn_conv_b, ffn_down=m_ffn_down)
    var = dict(ada_w=v_ada_w, ada_b=v_ada_b, norm1_w=v_norm1_w, w_in=v_w_in, q_a_norm=v_q_a_norm, w_q_b=v_w_q_b,
               kv_a_norm=v_kv_a_norm, w_kv_b=v_w_kv_b, q_norm=v_q_norm, k_norm=v_k_norm, pool_w=v_pool_w,
               pool_scale=v_pool_scale, ssd_conv_w=v_ssd_conv_w, ssd_conv_b=v_ssd_conv_b, ssd_dt_bias=v_ssd_dt_bias,
               ssd_a_log=v_ssd_a_log, ssd_d=v_ssd_d, ssd_norm_w=v_ssd_norm_w, w_branch=v_w_branch, w_out=v_w_out,
               norm2_w=v_norm2_w, ffn_up=v_ffn_up, ffn_conv_w=v_ffn_conv_w, ffn_conv_b=v_ffn_conv_b, ffn_down=v_ffn_down)
    names = list(p)
    me = 4 * lax.axis_index("x") + 2 * lax.axis_index("y") + lax.axis_index("c")
    xs, tgt = x[0], loss_target[0]
    s = xs.shape[0]

    inv_freq = ROPE_THETA ** (-jnp.arange(0, 32, 2, dtype=F32) / 32.0)
    ang = positions[0].astype(F32)[:, None] * inv_freq
    cos, sin = jnp.cos(ang), jnp.sin(ang)
    cos2 = _padc(jnp.concatenate([cos, cos], axis=1), 128)
    sin2 = _padc(jnp.concatenate([-sin, sin], axis=1), 128)

    (c_all,) = all_to_all([c], [True], "gather_c")
    modp, cact = ada_mod(jnp.pad(c_all.reshape(NDEV, D), ((0, 8), (0, 0))), ada_w)
    (mod_in,) = all_to_all([modp[:, 0:NDEV].transpose(1, 0, 2)], [False], "scatter_mod")

    all7 = [True] * len(BIG)
    gather0, _ = exchange_start([p[n][0].astype(BF16) for n in BIG], all7, "gather_w0_start")
    gather1, tok = exchange_start([p[n][1].astype(BF16) for n in BIG], all7, "gather_w1_start")
    mod = mod_in.transpose(1, 0, 2).reshape(LAYERS, 6 * D) + ada_b + tok[0, 0]

    conv_shards = jnp.concatenate([ssd_conv_w.reshape(-1), ffn_conv_w.reshape(-1)])
    (conv_all,) = all_to_all([conv_shards], [True], "gather_conv_w")
    n1 = LAYERS * 4 * 192
    scw = conv_all[:, :n1].reshape(NDEV, LAYERS, 4, 192).transpose(1, 2, 0, 3).reshape(LAYERS, 4, 1536)
    fcw = conv_all[:, n1:].reshape(NDEV, LAYERS, 3, 704).transpose(1, 2, 0, 3).reshape(LAYERS, 3, 2 * FFN)

    def layer_weights(gathered, l):
        full = {n: _gathered_full(g, n) for n, g in zip(BIG, gathered)}
        lw = {n: p[n][l] for n in names}
        wb = full["w_branch"]
        lw.update(win=_win_layout(full["w_in"]), wq=_wq_layout(full["w_q_b"]), wkv=_wkv_layout(full["w_kv_b"]),
                  wba=_wba_layout(wb[0:512]), wbb=wb[512:1024], wbc=wb[1024:2048], wout=full["w_out"],
                  wup=full["ffn_up"], wdn=full["ffn_down"], ssd_conv_w=scw[l], ffn_conv_w=fcw[l],
                  mla_vec=_mla_vec(lw["q_a_norm"], lw["kv_a_norm"], lw["q_norm"], lw["k_norm"]))
        return lw

    lws, saved = [None] * LAYERS, [None] * LAYERS
    lws[0] = layer_weights(exchange_wait(gather0, tok, "gather_w0_wait"), 0)
    h, saved[0] = _layer_fwd(xs, lws[0], mod[0], cos2, sin2, 0)
    lws[1] = layer_weights(exchange_wait(gather1, h, "gather_w1_wait"), 1)
    h, saved[1] = _layer_fwd(h, lws[1], mod[1], cos2, sin2, 1)
    dx, lpart = loss_head(h, tgt)
    loss = lax.psum(lpart[0, 0], ("x", "y", "c"))

    grads, small = [None] * LAYERS, [None] * LAYERS
    none7 = [False] * len(BIG)
    dx, grads[1], small[1] = _layer_bwd(dx, lws[1], saved[1], cos2, sin2, 1)
    scatter1, tok = exchange_start([_to_shards(grads[1][n], n) for n in BIG], none7, "scatter_g1_start")
    saved[0]["fvec"] = saved[0]["fvec"] + tok[0, 0]
    dx, grads[0], small[0] = _layer_bwd(dx, lws[0], saved[0], cos2, sin2, 0)
    parts1 = exchange_wait(scatter1, dx, "scatter_g1_wait")
    scatter0, tok = exchange_start([_to_shards(grads[0][n], n) for n in BIG], none7, "scatter_g0_start")

    dmod = jnp.stack([small[l]["ada_b"] for l in range(LAYERS)]) + tok[0, 0]
    (dmod_in,) = all_to_all([dmod.reshape(LAYERS, NDEV, 768).transpose(1, 0, 2)], [False], "scatter_dmod")
    dmod16 = jnp.pad(dmod_in, ((0, 8), (0, 0), (0, 0)))
    g_ada = jnp.stack([tn_matmul(cact, dmod16[:, l], f"dw_ada{l}") for l in range(LAYERS)])
    out = {}
    flat = lambda a: a.reshape(LAYERS * D, 768)
    out["ada_w"] = [r.reshape(ada_w.shape) for r in
                    adamw([flat(g_ada)[None]], flat(ada_w), flat(m_ada_w), flat(v_ada_w), "adamw_ada_w")]

    (small_all,) = all_to_all([_pack(small, SMALL) + tok[0, 0]], [True], "gather_small")
    zeros = jnp.zeros((SMALL_ROWS, D), F32)
    g_small = _unpack(adamw([small_all], zeros, zeros, zeros, "sum_small")[0], SMALL)
    per = lambda d, nm: [{n: d[n][l] for n, _ in nm} for l in range(LAYERS)]
    res = adamw([_pack(per(g_small, SMALL_REPL), SMALL_REPL)[None]], _pack(per(p, SMALL_REPL), SMALL_REPL),
                _pack(per(mom, SMALL_REPL), SMALL_REPL), _pack(per(var, SMALL_REPL), SMALL_REPL), "adamw_small")
    after = res[0][0:8, 0:128] + out["ada_w"][0][0, 0:8, 0:128]
    res = [_unpack(r, SMALL_REPL) for r in res]
    for n, _ in SMALL_REPL:
        out[n] = [r[n] for r in res]
    for n, k, w in (("ssd_conv_w", 4, 192), ("ffn_conv_w", 3, 704)):
        g_mine = lax.dynamic_slice(g_small[n], (0, 0, me * w), (LAYERS, k, w))
        f2 = lambda a: jnp.pad(a.reshape(LAYERS * k, w), ((0, 8 - LAYERS * k), (0, 0)))
        res = adamw([f2(g_mine)[None]], f2(p[n]), f2(mom[n]), f2(var[n]), f"adamw_{n}")
        out[n] = [r[0:LAYERS * k].reshape(LAYERS, k, w) for r in res]

    parts0 = exchange_wait(scatter0, after, "scatter_g0_wait")
    for n, p0, p1 in zip(BIG, parts0, parts1):
        shp = p[n].shape
        flat = lambda a: a.reshape(shp[0] * shp[1], shp[2])
        res = adamw([p0, p1], flat(p[n]), flat(mom[n]), flat(var[n]), f"adamw_{n}")
        out[n] = [r.reshape(shp) for r in res]

    outs = [loss, dx[None]]
    for q in range(4):
        outs += [out[n][q] for n in names]
    return tuple(outs)
```

```python
import functools
import math

import jax
import jax.numpy as jnp
from jax import lax
from jax.experimental import pallas as pl
from jax.experimental.pallas import tpu as pltpu

F32, BF16 = jnp.float32, jnp.bfloat16
EPS = 1e-6
D = 1024
NDEV = 8
LAYERS = 2
HEADS = 8
FFN = 2816
FFN_TILE = 1408
FFN_NT = FFN // FFN_TILE
ATT_SCALE = 96 ** -0.5
ROPE_THETA = 10000.0
LR, B1, B2, ADAM_EPS, WD, STEP = 0.001, 0.9, 0.999, 1e-08, 0.01, 10

O_G, O_XS, O_Z, O_PU, O_BC, O_CKV, O_KR, O_KRS, O_DT, O_QL = 0, 3072, 4096, 5120, 5632, 6144, 6400, 6528, 6656, 6912
NPROJ = 7296
CONST = dict(pipeline_mode=pl.Buffered(1))


def _pick(n, cap, mult=128):
    if n <= cap:
        return n
    best = None
    for t in range(mult, cap + 1, mult):
        if n % t == 0:
            best = t
    assert best is not None, (n, cap, mult)
    return best


def _sig(x):
    return 1.0 / (1.0 + jnp.exp(-x))


def _rms(x, w, n):
    return x * lax.rsqrt(jnp.sum(x * x, axis=-1, keepdims=True) / n + EPS) * w


def _raw(a, b, dims):
    return lax.dot_general(a.astype(BF16), b.astype(BF16), dims, preferred_element_type=F32)


_NN = (((1,), (0,)), ((), ()))
_NT = (((1,), (1,)), ((), ()))
_TN = (((0,), (0,)), ((), ()))
_BNN = (((2,), (1,)), ((0,), (0,)))
_BNT = (((2,), (2,)), ((0,), (0,)))
_BTN = (((1,), (1,)), ((0,), (0,)))


@jax.custom_vjp
def mm_nn(a, b):
    return _raw(a, b, _NN)


mm_nn.defvjp(lambda a, b: (_raw(a, b, _NN), (a, b)),
             lambda r, g: (_raw(g, r[1], _NT), _raw(r[0], g, _TN)))


@jax.custom_vjp
def mm_nc(a, b):
    return _raw(a, b, _NN)


mm_nc.defvjp(lambda a, b: (_raw(a, b, _NN), b),
             lambda b, g: (_raw(g, b, _NT), jnp.zeros_like(b)))


@jax.custom_vjp
def mm_nt(a, b):
    return _raw(a, b, _NT)


mm_nt.defvjp(lambda a, b: (_raw(a, b, _NT), (a, b)),
             lambda r, g: (_raw(g, r[1], _NN), _raw(g, r[0], _TN)))


@jax.custom_vjp
def bmm_nn(a, b):
    return _raw(a, b, _BNN)


bmm_nn.defvjp(lambda a, b: (_raw(a, b, _BNN), (a, b)),
              lambda r, g: (_raw(g, r[1], _BNT), _raw(r[0], g, _BTN)))


@jax.custom_vjp
def bmm_nt(a, b):
    return _raw(a, b, _BNT)


bmm_nt.defvjp(lambda a, b: (_raw(a, b, _BNT), (a, b)),
              lambda r, g: (_raw(g, r[1], _BNN), _raw(g, r[0], _BTN)))


@jax.custom_vjp
def softplus(x):
    t = jnp.exp(-jnp.abs(x))
    u = 1.0 + t
    one = u == 1.0
    l1p = jnp.where(one, t, jnp.log(u) * (t / jnp.where(one, 1.0, u - 1.0)))
    return jnp.maximum(x, 0.0) + l1p


softplus.defvjp(lambda x: (softplus(x), x), lambda x, g: (g * _sig(x),))


def _params(*sem):
    return pltpu.CompilerParams(dimension_semantics=sem, vmem_limit_bytes=56 * 1024 * 1024)


def all_to_all(arrs, bcast, name):
    n = len(arrs)
    out_shapes = [jax.ShapeDtypeStruct(((NDEV,) + a.shape) if b else a.shape, a.dtype) for a, b in zip(arrs, bcast)]

    def body(*refs):
        ins, outs = refs[:n], refs[n:2 * n]
        send_sems, recv_sems, local_sems = refs[2 * n:]
        x, y, c = lax.axis_index("x"), lax.axis_index("y"), lax.axis_index("c")
        me = 4 * x + 2 * y + c
        local = []
        for j in range(n):
            cp = pltpu.make_async_copy(ins[j] if bcast[j] else ins[j].at[me], outs[j].at[me], local_sems.at[j])
            cp.start()
            local.append(cp)
        remote = []
        for k in range(1, NDEV):
            px, py, pc = x ^ ((k >> 2) & 1), y ^ ((k >> 1) & 1), c ^ (k & 1)
            peer = 4 * px + 2 * py + pc
            for j in range(n):
                s = (k - 1) * n + j
                cp = pltpu.make_async_remote_copy(
                    src_ref=ins[j] if bcast[j] else ins[j].at[peer], dst_ref=outs[j].at[me],
                    send_sem=send_sems.at[s], recv_sem=recv_sems.at[s],
                    device_id=(px, py, pc), device_id_type=pl.DeviceIdType.MESH)
                cp.start()
                remote.append(cp)
        for cp in remote:
            cp.wait()
        for cp in local:
            cp.wait()

    any_spec = pl.BlockSpec(memory_space=pl.ANY)
    return pl.pallas_call(
        body, name=name, out_shape=out_shapes, in_specs=[any_spec] * n, out_specs=[any_spec] * n,
        scratch_shapes=[pltpu.SemaphoreType.DMA((7 * n,)), pltpu.SemaphoreType.DMA((7 * n,)),
                        pltpu.SemaphoreType.DMA((n,))],
        compiler_params=pltpu.CompilerParams(has_side_effects=True),
    )(*arrs)


def _peers():
    x, y, c = lax.axis_index("x"), lax.axis_index("y"), lax.axis_index("c")
    out = []
    for k in range(1, NDEV):
        px, py, pc = x ^ ((k >> 2) & 1), y ^ ((k >> 1) & 1), c ^ (k & 1)
        out.append(((px, py, pc), 4 * px + 2 * py + pc))
    return 4 * x + 2 * y + c, out


def _exchange_copies(ins, lands, bcast, send_sems, recv_sems):
    me, peers = _peers()
    n, copies = len(ins), []
    for k, (dev, lin) in enumerate(peers):
        for j in range(n):
            copies.append(pltpu.make_async_remote_copy(
                src_ref=ins[j] if bcast[j] else ins[j].at[lin], dst_ref=lands[j].at[me],
                send_sem=send_sems.at[k * n + j], recv_sem=recv_sems.at[k * n + j],
                device_id=dev, device_id_type=pl.DeviceIdType.MESH))
    return me, copies


_HBM = pl.BlockSpec(memory_space=pltpu.HBM)
_SEM = pl.BlockSpec(memory_space=pltpu.SEMAPHORE)
_EFFECT = pltpu.SideEffectType.DATAFLOW_SIDE_EFFECTING


def exchange_start(arrs, bcast, name):
    n = len(arrs)
    land_shapes = [((NDEV,) + a.shape) if b else a.shape for a, b in zip(arrs, bcast)]

    def body(*refs):
        ins, lands = refs[:n], refs[n:2 * n]
        send_sems, recv_sems = refs[2 * n], refs[2 * n + 1]
        token = refs[-1]
        _, copies = _exchange_copies(ins, lands, bcast, send_sems, recv_sems)
        for cp in copies:
            cp.start()
        token[...] = jnp.zeros_like(token)

    hbm = lambda shp, a: pltpu.HBM(shp, a.dtype)
    res = pl.pallas_call(
        body, name=name,
        out_shape=[pltpu.SemaphoreType.DMA((7 * n,)), pltpu.SemaphoreType.DMA((7 * n,))]
                  + [hbm(a.shape, a) for a in arrs] + [hbm(s_, a) for s_, a in zip(land_shapes, arrs)]
                  + [jax.ShapeDtypeStruct((8, 128), F32)],
        in_specs=[_HBM] * (2 * n), out_specs=[_SEM, _SEM] + [_HBM] * (2 * n) + [pl.BlockSpec(memory_space=pltpu.VMEM)],
        input_output_aliases={i: 2 + i for i in range(2 * n)},
        compiler_params=pltpu.CompilerParams(has_side_effects=_EFFECT),
    )(*[pltpu.with_memory_space_constraint(a, pltpu.HBM) for a in arrs],
      *[pltpu.with_memory_space_constraint(lax.empty(s_, a.dtype), pltpu.HBM) for s_, a in zip(land_shapes, arrs)])
    return (res[0], res[1], res[2:2 + n], res[2 + n:2 + 2 * n], tuple(bcast)), res[-1]


def exchange_wait(state, after, name):
    send_sems, recv_sems, ins, lands, bcast = state
    n = len(ins)

    def body(*refs):
        in_refs, land_refs = refs[:n], refs[n:2 * n]
        s_sems, r_sems = refs[2 * n], refs[2 * n + 1]
        token = refs[-1]
        _, copies = _exchange_copies(in_refs, land_refs, bcast, s_sems, r_sems)
        for cp in copies:
            cp.wait_send()
            cp.wait_recv()
        token[...] = jnp.zeros_like(token)

    res = pl.pallas_call(
        body, name=name,
        out_shape=[pltpu.HBM(a.shape, a.dtype) for a in ins] + [pltpu.HBM(a.shape, a.dtype) for a in lands]
                  + [jax.ShapeDtypeStruct((8, 128), F32)],
        in_specs=[_HBM] * (2 * n) + [_SEM, _SEM, pl.BlockSpec(memory_space=pl.ANY)],
        out_specs=[_HBM] * (2 * n) + [pl.BlockSpec(memory_space=pltpu.VMEM)],
        input_output_aliases={i: i for i in range(2 * n)},
        compiler_params=pltpu.CompilerParams(has_side_effects=_EFFECT),
    )(*ins, *lands, send_sems, recv_sems, after)
    me = 4 * lax.axis_index("x") + 2 * lax.axis_index("y") + lax.axis_index("c")
    got = []
    for j in range(n):
        own = res[j][None] if bcast[j] else lax.dynamic_index_in_dim(res[j], me, 0, keepdims=True)
        got.append(lax.dynamic_update_slice_in_dim(res[n + j], own, me, axis=0))
    return got, res[-1]


def norm_proj_fwd(x, vec, w, name):
    s, n = x.shape[0], w.shape[1]
    tr, tn = _pick(s, 512), _pick(n, 2560)

    def body(x_ref, v_ref, w_ref, o_ref, h_ref, h_scr):
        @pl.when(pl.program_id(1) == 0)
        def _():
            h = _rms(x_ref[...], v_ref[0:1, :], D) * (1.0 + v_ref[2:3, :]) + v_ref[1:2, :]
            h_scr[...] = h.astype(BF16)
            h_ref[...] = h.astype(BF16)
        o_ref[...] = jnp.dot(h_scr[...], w_ref[...], preferred_element_type=F32)

    return pl.pallas_call(
        body, name=name, grid=(s // tr, n // tn),
        in_specs=[pl.BlockSpec((tr, D), lambda i, j: (i, 0)), pl.BlockSpec((8, D), lambda i, j: (0, 0)),
                  pl.BlockSpec((D, tn), lambda i, j: (0, j))],
        out_specs=[pl.BlockSpec((tr, tn), lambda i, j: (i, j)), pl.BlockSpec((tr, D), lambda i, j: (i, 0))],
        out_shape=[jax.ShapeDtypeStruct((s, n), F32), jax.ShapeDtypeStruct((s, D), BF16)],
        scratch_shapes=[pltpu.VMEM((tr, D), BF16)],
        compiler_params=_params("arbitrary", "arbitrary"),
    )(x, vec, w)


def _col_tiles(arr, cap):
    if arr.ndim == 2:
        n = arr.shape[1]
        t = _pick(n, cap)
        return n, t, lambda rows, ix: pl.BlockSpec((rows, t), lambda *g: ix(*g))
    width = arr.shape[2]
    t = _pick(width, cap)
    per = width // t

    def spec(rows, ix):
        def index(*g):
            r, j = ix(*g)
            return (j // per, r, j % per)
        return pl.BlockSpec((None, rows, t), index)
    return arr.shape[0] * width, t, spec


def norm_proj_bwd(x, vec, dp, w, dx_in, aux, name):
    s = x.shape[0]
    tr = _pick(s, 512)
    n, tk, dp_spec = _col_tiles(dp, 2560)
    nk, has_aux = n // tk, aux is not None

    def body(*refs):
        if has_aux:
            x_ref, v_ref, dp_ref, w_ref, dxin_ref, aux_ref, dx_ref, dv_ref, acc = refs
        else:
            x_ref, v_ref, dp_ref, w_ref, dxin_ref, dx_ref, dv_ref, acc = refs
        i, k = pl.program_id(0), pl.program_id(1)

        @pl.when(k == 0)
        def _():
            acc[...] = jnp.zeros_like(acc)

        acc[...] += _raw(dp_ref[...], w_ref[...], _NT)

        @pl.when(k == nk - 1)
        def _():
            f = lambda xx, nw, sh, sc: _rms(xx, nw, D) * (1.0 + sc) + sh
            _, vjp = jax.vjp(f, x_ref[...], v_ref[0:1, :], v_ref[1:2, :], v_ref[2:3, :])
            dx, dnw, dsh, dsc = vjp(acc[...])
            dx_ref[...] = dxin_ref[...] + dx

            @pl.when(i == 0)
            def _():
                dv_ref[...] = jnp.zeros_like(dv_ref)

            dv_ref[0:1, :] += dnw
            dv_ref[1:2, :] += dsh
            dv_ref[2:3, :] += dsc
            if has_aux:
                dv_ref[3:4, :] += jnp.sum(dxin_ref[...] * aux_ref[...], axis=0, keepdims=True)

    row = pl.BlockSpec((tr, D), lambda i, k: (i, 0))
    in_specs = [row, pl.BlockSpec((8, D), lambda i, k: (0, 0)), dp_spec(tr, lambda i, k: (i, k)),
                pl.BlockSpec((D, tk), lambda i, k: (0, k)), row] + ([row] if has_aux else [])
    args = [x, vec, dp, w, dx_in] + ([aux] if has_aux else [])
    return pl.pallas_call(
        body, name=name, grid=(s // tr, nk), in_specs=in_specs,
        out_specs=[row, pl.BlockSpec((8, D), lambda i, k: (0, 0))],
        out_shape=[jax.ShapeDtypeStruct((s, D), F32), jax.ShapeDtypeStruct((8, D), F32)],
        scratch_shapes=[pltpu.VMEM((tr, D), F32)],
        compiler_params=_params("arbitrary", "arbitrary"),
    )(*args)


def tn_matmul(a, b, name, scale=None):
    s, m = a.shape
    ts, tm = _pick(s, 512, 16), _pick(m, 1408)
    n, tn, b_spec = _col_tiles(b, 2560)
    ns, has_scale = s // ts, scale is not None

    def body(*refs):
        if has_scale:
            a_ref, b_ref, sc_ref, o_ref = refs
        else:
            a_ref, b_ref, o_ref = refs
        k = pl.program_id(2)

        @pl.when(k == 0)
        def _():
            o_ref[...] = jnp.zeros_like(o_ref)

        o_ref[...] += _raw(a_ref[...], b_ref[...], _TN)
        if has_scale:
            @pl.when(k == ns - 1)
            def _():
                o_ref[...] = o_ref[...] * sc_ref[...]

    in_specs = [pl.BlockSpec((ts, tm), lambda i, j, k: (k, i)), b_spec(ts, lambda i, j, k: (k, j))]
    if has_scale:
        in_specs.append(pl.BlockSpec((1, tn), lambda i, j, k: (0, j)))
    return pl.pallas_call(
        body, name=name, grid=(m // tm, n // tn, ns), in_specs=in_specs,
        out_specs=pl.BlockSpec((tm, tn), lambda i, j, k: (i, j)),
        out_shape=jax.ShapeDtypeStruct((m, n), F32),
        compiler_params=_params("arbitrary", "arbitrary", "arbitrary"),
    )(*([a, b] + ([scale] if has_scale else [])))


def ada_mod(c16, w):
    ncol = w.shape[2]

    def body(c_ref, w_ref, o_ref, a_ref):
        cc = c_ref[...]
        act = cc * _sig(cc)
        a_ref[...] = act
        o_ref[...] = _raw(act, w_ref[...], _NN)

    return pl.pallas_call(
        body, name="ada_mod", grid=(LAYERS,),
        in_specs=[pl.BlockSpec((16, D), lambda l: (0, 0)), pl.BlockSpec((None, D, ncol), lambda l: (l, 0, 0))],
        out_specs=[pl.BlockSpec((None, 16, ncol), lambda l: (l, 0, 0)), pl.BlockSpec((16, D), lambda l: (0, 0))],
        out_shape=[jax.ShapeDtypeStruct((LAYERS, 16, ncol), F32), jax.ShapeDtypeStruct((16, D), F32)],
        compiler_params=_params("arbitrary"),
    )(c16, w)


def _mla_head(q_lat, c_kv, kr, krs, wqn, wqr, wqrs, wkn, wv, qa_w, kva_w, qn_w, qr_w, qrs_w, kn_w, kr_w, krs_w,
              cos2, sin2):
    qn = _rms(q_lat, qa_w, 384.0)
    kvn = _rms(c_kv, kva_w, 256.0)
    qnope = _rms(mm_nn(qn, wqn), qn_w, 64.0)
    qr, qrs = mm_nn(qn, wqr), mm_nn(qn, wqrs)
    rq = lax.rsqrt(jnp.sum(qr * qr, axis=-1, keepdims=True) / 32.0 + EPS)
    qrope = rq * (qr * qr_w * cos2 + qrs * qrs_w * sin2)
    knope = _rms(mm_nn(kvn, wkn), kn_w, 64.0)
    v = mm_nn(kvn, wv)
    rk = lax.rsqrt(jnp.sum(kr * kr, axis=-1, keepdims=True) / 32.0 + EPS)
    krope = rk * (kr * kr_w * cos2 + krs * krs_w * sin2)
    return qnope, qrope, knope, krope, v


def _mla_vec_pieces(v_ref):
    return (v_ref[0:1, 0:384], v_ref[1:2, 0:256], v_ref[2:3, 0:128], v_ref[2:3, 128:256], v_ref[2:3, 256:384],
            v_ref[3:4, 0:128], v_ref[3:4, 128:256], v_ref[3:4, 256:384])


def _mla_in_specs(tr):
    return [pl.BlockSpec((tr, 384), lambda i: (i, O_QL // 384)), pl.BlockSpec((tr, 256), lambda i: (i, O_CKV // 256)),
            pl.BlockSpec((tr, 128), lambda i: (i, O_KR // 128)), pl.BlockSpec((tr, 128), lambda i: (i, O_KRS // 128)),
            pl.BlockSpec((HEADS, 384, 384), lambda i: (0, 0, 0), **CONST),
            pl.BlockSpec((HEADS, 256, 256), lambda i: (0, 0, 0), **CONST),
            pl.BlockSpec((8, 512), lambda i: (0, 0)),
            pl.BlockSpec((tr, 128), lambda i: (i, 0)), pl.BlockSpec((tr, 128), lambda i: (i, 0))]


def mla_pre_fwd(proj, wq, wkv, vec, cos2, sin2, name):
    s = proj.shape[0]
    tr = _pick(s, 256)

    def body(ql_ref, ckv_ref, kr_ref, krs_ref, wq_ref, wkv_ref, v_ref, cos_ref, sin_ref, q_out, k_out, v_out):
        acts = (ql_ref[...], ckv_ref[...], kr_ref[...], krs_ref[...])
        vp = _mla_vec_pieces(v_ref)
        for h in range(HEADS):
            ws = (wq_ref[h, :, 0:128], wq_ref[h, :, 128:256], wq_ref[h, :, 256:384],
                  wkv_ref[h, :, 0:128], wkv_ref[h, :, 128:256])
            qn, qr, kn, krp, v = _mla_head(*acts, *ws, *vp, cos_ref[...], sin_ref[...])
            q_out[h, :, 0:128] = qn.astype(BF16)
            q_out[h, :, 128:256] = qr.astype(BF16)
            k_out[h, :, 0:128] = kn.astype(BF16)
            k_out[h, :, 128:256] = krp.astype(BF16)
            v_out[h] = v.astype(BF16)

    return pl.pallas_call(
        body, name=name, grid=(s // tr,), in_specs=_mla_in_specs(tr),
        out_specs=[pl.BlockSpec((HEADS, tr, 256), lambda i: (0, i, 0)), pl.BlockSpec((HEADS, tr, 256), lambda i: (0, i, 0)),
                   pl.BlockSpec((HEADS, tr, 128), lambda i: (0, i, 0))],
        out_shape=[jax.ShapeDtypeStruct((HEADS, s, 256), BF16), jax.ShapeDtypeStruct((HEADS, s, 256), BF16),
                   jax.ShapeDtypeStruct((HEADS, s, 128), BF16)],
        compiler_params=_params("arbitrary"),
    )(proj, proj, proj, proj, wq, wkv, vec, cos2, sin2)


def mla_pre_bwd(proj, wq, wkv, vec, cos2, sin2, dq, dk, dv, name):
    s = proj.shape[0]
    tr = _pick(s, 256)

    def body(ql_ref, ckv_ref, kr_ref, krs_ref, wq_ref, wkv_ref, v_ref, cos_ref, sin_ref, dq_ref, dk_ref, dv_ref,
             dql_out, dckv_out, dkr_out, dkrs_out, dwq_out, dwkv_out, dvec_out):
        @pl.when(pl.program_id(0) == 0)
        def _():
            dwq_out[...] = jnp.zeros_like(dwq_out)
            dwkv_out[...] = jnp.zeros_like(dwkv_out)
            dvec_out[...] = jnp.zeros_like(dvec_out)

        acts = (ql_ref[...], ckv_ref[...], kr_ref[...], krs_ref[...])
        vp = _mla_vec_pieces(v_ref)
        cos2_, sin2_ = cos_ref[...], sin_ref[...]

        def head(h, carry):
            wq_h, wkv_h = wq_ref[h].astype(F32), wkv_ref[h].astype(F32)
            ws = (wq_h[:, 0:128], wq_h[:, 128:256], wq_h[:, 256:384], wkv_h[:, 0:128], wkv_h[:, 128:256])
            f = lambda *a: _mla_head(*a, cos2_, sin2_)
            _, vjp = jax.vjp(f, *acts, *ws, *vp)
            dq_h, dk_h = dq_ref[h], dk_ref[h]
            g = vjp((dq_h[:, 0:128], dq_h[:, 128:256], dk_h[:, 0:128], dk_h[:, 128:256], dv_ref[h]))
            dwq_out[h, :, 0:128] += g[4]
            dwq_out[h, :, 128:256] += g[5]
            dwq_out[h, :, 256:384] += g[6]
            dwkv_out[h, :, 0:128] += g[7]
            dwkv_out[h, :, 128:256] += g[8]
            dvec_out[0:1, 0:384] += g[9]
            dvec_out[1:2, 0:256] += g[10]
            dvec_out[2:3, 0:128] += g[11]
            dvec_out[2:3, 128:256] += g[12]
            dvec_out[2:3, 256:384] += g[13]
            dvec_out[3:4, 0:128] += g[14]
            dvec_out[3:4, 128:256] += g[15]
            dvec_out[3:4, 256:384] += g[16]
            return tuple(c + gg for c, gg in zip(carry, g[:4]))

        tot = lax.fori_loop(0, HEADS, head, tuple(jnp.zeros_like(a) for a in acts))
        dql_out[...] = tot[0].astype(BF16)
        dckv_out[...] = tot[1].astype(BF16)
        dkr_out[...] = tot[2].astype(BF16)
        dkrs_out[...] = tot[3].astype(BF16)

    hb = lambda w: pl.BlockSpec((HEADS, tr, w), lambda i: (0, i, 0))
    return pl.pallas_call(
        body, name=name, grid=(s // tr,), in_specs=_mla_in_specs(tr) + [hb(256), hb(256), hb(128)],
        out_specs=[pl.BlockSpec((tr, 384), lambda i: (i, 0)), pl.BlockSpec((tr, 256), lambda i: (i, 0)),
                   pl.BlockSpec((tr, 128), lambda i: (i, 0)), pl.BlockSpec((tr, 128), lambda i: (i, 0)),
                   pl.BlockSpec((HEADS, 384, 384), lambda i: (0, 0, 0)), pl.BlockSpec((HEADS, 256, 256), lambda i: (0, 0, 0)),
                   pl.BlockSpec((8, 512), lambda i: (0, 0))],
        out_shape=[jax.ShapeDtypeStruct((s, 384), BF16), jax.ShapeDtypeStruct((s, 256), BF16),
                   jax.ShapeDtypeStruct((s, 128), BF16), jax.ShapeDtypeStruct((s, 128), BF16),
                   jax.ShapeDtypeStruct((HEADS, 384, 384), F32), jax.ShapeDtypeStruct((HEADS, 256, 256), F32),
                   jax.ShapeDtypeStruct((8, 512), F32)],
        compiler_params=_params("arbitrary"),
    )(proj, proj, proj, proj, wq, wkv, vec, cos2, sin2, dq, dk, dv)


def _att_probs(q, kk, i, tq):
    sc = _raw(q, kk, _NT) * ATT_SCALE
    rows = lax.broadcasted_iota(jnp.int32, sc.shape, 0) + i * tq
    cols = lax.broadcasted_iota(jnp.int32, sc.shape, 1)
    sc = jnp.where(cols <= rows, sc, -jnp.inf)
    e = jnp.exp(sc - jnp.max(sc, axis=-1, keepdims=True))
    return e / jnp.sum(e, axis=-1, keepdims=True)


def mla_attn_fwd(q, k, v, name):
    s = q.shape[1]
    tq = _pick(s, 256)

    def body(q_ref, k_ref, v_ref, o_ref):
        for i in range(s // tq):
            n = (i + 1) * tq
            p = _att_probs(q_ref[i * tq:n, :], k_ref[0:n, :], i, tq)
            o_ref[i * tq:n, :] = _raw(p, v_ref[0:n, :], _NN)

    hs = lambda w: pl.BlockSpec((None, s, w), lambda h: (h, 0, 0))
    return pl.pallas_call(
        body, name=name, grid=(HEADS,), in_specs=[hs(256), hs(256), hs(128)],
        out_specs=pl.BlockSpec((s, 128), lambda h: (0, h)),
        out_shape=jax.ShapeDtypeStruct((s, HEADS * 128), F32),
        compiler_params=_params("arbitrary"),
    )(q, k, v)


def mla_attn_bwd(q, k, v, do, name):
    s = q.shape[1]
    tq = _pick(s, 256)

    def body(q_ref, k_ref, v_ref, do_ref, dq_ref, dk_ref, dv_ref):
        dk_ref[...] = jnp.zeros_like(dk_ref)
        dv_ref[...] = jnp.zeros_like(dv_ref)
        for i in range(s // tq):
            n = (i + 1) * tq
            qq, kk, vv = q_ref[i * tq:n, :], k_ref[0:n, :], v_ref[0:n, :]
            p = _att_probs(qq, kk, i, tq)
            o = _raw(p, vv, _NN)
            dout = do_ref[i * tq:n, :]
            delta = jnp.sum(dout * o, axis=-1, keepdims=True)
            dp = _raw(dout, vv, _NT)
            ds = p * (dp - delta) * ATT_SCALE
            dq_ref[i * tq:n, :] = _raw(ds, kk, _NN)
            dk_ref[0:n, :] += _raw(ds, qq, _TN)
            dv_ref[0:n, :] += _raw(p, dout, _TN)

    hs = lambda w: pl.BlockSpec((None, s, w), lambda h: (h, 0, 0))
    return pl.pallas_call(
        body, name=name, grid=(HEADS,),
        in_specs=[hs(256), hs(256), hs(128), pl.BlockSpec((s, 128), lambda h: (0, h))],
        out_specs=[hs(256), hs(256), hs(128)],
        out_shape=[jax.ShapeDtypeStruct((HEADS, s, 256), F32), jax.ShapeDtypeStruct((HEADS, s, 256), F32),
                   jax.ShapeDtypeStruct((HEADS, s, 128), F32)],
        compiler_params=_params("arbitrary"),
    )(q, k, v, do)


def _pool_windows(u, pad, s, g):
    pad[0:16, :] = jnp.zeros((16, 128), F32)
    cur, sel = u, None
    for j, k in enumerate((1, 2, 4, 8)):
        pad[16:16 + s, :] = cur
        cur = cur + pad[16 - k:16 - k + s, :]
        sel = cur if sel is None else jnp.where(g == j, cur, sel)
    return sel


def _pool_count(s, g):
    t = lax.broadcasted_iota(jnp.int32, (s, 1), 0)
    return jnp.minimum(t + 1, 2 << g).astype(F32)


def pool_fwd(proj, pw, ps, name):
    s = proj.shape[0]

    def body(u_ref, w_ref, s_ref, o_ref, pad):
        g = pl.program_id(0)
        u = u_ref[...]
        pooled = _pool_windows(u, pad, s, g) / _pool_count(s, g) - u
        o_ref[...] = _raw(pooled, w_ref[...], _NN) * s_ref[...]

    return pl.pallas_call(
        body, name=name, grid=(4,),
        in_specs=[pl.BlockSpec((s, 128), lambda g: (0, O_PU // 128 + g)), pl.BlockSpec((None, 128, 128), lambda g: (g, 0, 0)),
                  pl.BlockSpec((1, 128), lambda g: (0, g))],
        out_specs=pl.BlockSpec((s, 128), lambda g: (0, g)),
        out_shape=jax.ShapeDtypeStruct((s, 512), F32),
        scratch_shapes=[pltpu.VMEM((s + 16, 128), F32)],
        compiler_params=_params("arbitrary"),
    )(proj, pw, ps)


def pool_bwd(proj, pw, ps, do, name):
    s = proj.shape[0]

    def body(u_ref, w_ref, s_ref, do_ref, du_ref, dw_ref, ds_ref, pad):
        g = pl.program_id(0)
        u, w, dout = u_ref[...], w_ref[...], do_ref[...]
        cnt = _pool_count(s, g)
        pooled = _pool_windows(u, pad, s, g) / cnt - u
        mixed = _raw(pooled, w, _NN)
        ds_ref[...] = jnp.sum(dout * mixed, axis=0, keepdims=True)
        dmixed = dout * s_ref[...]
        dw_ref[...] = _raw(pooled, dmixed, _TN)
        dpooled = _raw(dmixed, w, _NT)
        dsel = dpooled / cnt
        pad[s:s + 16, :] = jnp.zeros((16, 128), F32)
        cur = jnp.where(g == 3, dsel, 0.0)
        for j, k in ((2, 8), (1, 4), (0, 2)):
            pad[0:s, :] = cur
            cur = cur + pad[k:k + s, :] + jnp.where(g == j, dsel, 0.0)
        pad[0:s, :] = cur
        cur = cur + pad[1:1 + s, :]
        du_ref[...] = (cur - dpooled).astype(BF16)

    return pl.pallas_call(
        body, name=name, grid=(4,),
        in_specs=[pl.BlockSpec((s, 128), lambda g: (0, O_PU // 128 + g)), pl.BlockSpec((None, 128, 128), lambda g: (g, 0, 0)),
                  pl.BlockSpec((1, 128), lambda g: (0, g)), pl.BlockSpec((s, 128), lambda g: (0, g))],
        out_specs=[pl.BlockSpec((s, 128), lambda g: (0, g)), pl.BlockSpec((None, 128, 128), lambda g: (g, 0, 0)),
                   pl.BlockSpec((1, 128), lambda g: (0, g))],
        out_shape=[jax.ShapeDtypeStruct((s, 512), BF16), jax.ShapeDtypeStruct((4, 128, 128), F32),
                   jax.ShapeDtypeStruct((1, 512), F32)],
        scratch_shapes=[pltpu.VMEM((s + 16, 128), F32)],
        compiler_params=_params("arbitrary"),
    )(proj, pw, ps, do)


def _xbc_col(i):
    return jnp.where(i < 2, O_XS // 512 + i, O_BC // 512)


def conv_fwd(proj, cw, cb, name):
    s = proj.shape[0]

    def body(x_ref, w_ref, b_ref, o_ref, pad):
        pad[0:8, :] = jnp.zeros((8, 512), F32)
        pad[8:8 + s, :] = x_ref[...]
        y = b_ref[...] + sum(w_ref[k:k + 1, :] * pad[5 + k:5 + k + s, :] for k in range(4))
        o_ref[...] = y * _sig(y)

    return pl.pallas_call(
        body, name=name, grid=(3,),
        in_specs=[pl.BlockSpec((s, 512), lambda i: (0, _xbc_col(i))), pl.BlockSpec((4, 512), lambda i: (0, i)),
                  pl.BlockSpec((1, 512), lambda i: (0, i))],
        out_specs=pl.BlockSpec((s, 512), lambda i: (0, i)),
        out_shape=jax.ShapeDtypeStruct((s, 1536), F32),
        scratch_shapes=[pltpu.VMEM((s + 8, 512), F32)],
        compiler_params=_params("arbitrary"),
    )(proj, cw, cb)


def conv_bwd(proj, cw, cb, dact, name):
    s = proj.shape[0]

    def body(x_ref, w_ref, b_ref, da_ref, dx_ref, dw_ref, db_ref, pad, pad2):
        pad[0:8, :] = jnp.zeros((8, 512), F32)
        pad[8:8 + s, :] = x_ref[...]
        y = b_ref[...] + sum(w_ref[k:k + 1, :] * pad[5 + k:5 + k + s, :] for k in range(4))
        sg = _sig(y)
        dy = da_ref[...] * (sg * (1.0 + y * (1.0 - sg)))
        db_ref[...] = jnp.sum(dy, axis=0, keepdims=True)
        for k in range(4):
            dw_ref[k:k + 1, :] = jnp.sum(dy * pad[5 + k:5 + k + s, :], axis=0, keepdims=True)
        pad2[s:s + 8, :] = jnp.zeros((8, 512), F32)
        pad2[0:s, :] = dy
        dx_ref[...] = sum(w_ref[k:k + 1, :] * pad2[3 - k:3 - k + s, :] for k in range(4)).astype(BF16)

    return pl.pallas_call(
        body, name=name, grid=(3,),
        in_specs=[pl.BlockSpec((s, 512), lambda i: (0, _xbc_col(i))), pl.BlockSpec((4, 512), lambda i: (0, i)),
                  pl.BlockSpec((1, 512), lambda i: (0, i)), pl.BlockSpec((s, 512), lambda i: (0, i))],
        out_specs=[pl.BlockSpec((s, 512), lambda i: (0, i)), pl.BlockSpec((4, 512), lambda i: (0, i)),
                   pl.BlockSpec((1, 512), lambda i: (0, i))],
        out_shape=[jax.ShapeDtypeStruct((s, 1536), BF16), jax.ShapeDtypeStruct((4, 1536), F32),
                   jax.ShapeDtypeStruct((1, 1536), F32)],
        scratch_shapes=[pltpu.VMEM((s + 8, 512), F32), pltpu.VMEM((s + 8, 512), F32)],
        compiler_params=_params("arbitrary"),
    )(proj, cw, cb, dact)


def _ssd_chunk(xt, dtr, dtc, bm, cm, hprev, alog, dbias, dskip):
    ln = 128
    a = -jnp.exp(alog)
    dt_r = softplus(dtr + dbias)
    da_r = dt_r * a
    da_c = softplus(dtc + dbias) * a
    li = lax.broadcasted_iota(jnp.int32, (1, ln, ln), 1)
    si = lax.broadcasted_iota(jnp.int32, (1, ln, ln), 2)
    causal = si <= li
    acs_c = jnp.sum(jnp.where(causal, da_r, 0.0), axis=2, keepdims=True)
    acs_r = jnp.sum(jnp.where(li <= si, da_c, 0.0), axis=1, keepdims=True)
    acs_last = jnp.sum(da_r, axis=2, keepdims=True)
    decay = jnp.exp(jnp.where(causal, acs_c - acs_r, -jnp.inf))
    m = mm_nt(cm, bm)[None] * decay
    xdt = xt * dt_r
    y_diag = bmm_nt(xdt, m)
    bb = jnp.broadcast_to(bm[None], (8, ln, ln))
    cc = jnp.broadcast_to(cm[None], (8, ln, ln))
    states = bmm_nn(xdt * jnp.exp(acs_last - acs_r), bb)
    y_off = bmm_nt(hprev, cc) * jnp.exp(acs_r)
    hnew = hprev * jnp.exp(acs_last) + states
    return y_diag + y_off + xt * dskip, hnew


def _ssd_specs(nc, rev):
    cix = (lambda c: nc - 1 - c) if rev else (lambda c: c)
    hv = pl.BlockSpec((8, 1, 1), lambda g, c: (g, 0, 0))
    return [pl.BlockSpec((8, 64, 128), lambda g, c: (g, 0, cix(c))), pl.BlockSpec((8, 1, 128), lambda g, c: (g, 0, cix(c))),
            pl.BlockSpec((8, 128, 1), lambda g, c: (g, cix(c), 0)), pl.BlockSpec((128, 128), lambda g, c: (cix(c), 8 + g)),
            pl.BlockSpec((128, 128), lambda g, c: (cix(c), 10 + g))], hv, cix


def ssd_fwd(xt, dtr, dtc, xbc, alog, dbias, dskip, name):
    s = xt.shape[2]
    nc = s // 128
    specs, hv, _ = _ssd_specs(nc, False)

    def body(x_ref, dr_ref, dc_ref, b_ref, c_ref, al_ref, db_ref, dk_ref, y_ref, hs_ref, h_scr):
        @pl.when(pl.program_id(1) == 0)
        def _():
            h_scr[...] = jnp.zeros_like(h_scr)
        hp = h_scr[...]
        hs_ref[...] = hp
        y, hn = _ssd_chunk(x_ref[...], dr_ref[...], dc_ref[...], b_ref[...], c_ref[...], hp,
                           al_ref[...], db_ref[...], dk_ref[...])
        y_ref[...] = y
        h_scr[...] = hn

    return pl.pallas_call(
        body, name=name, grid=(2, nc), in_specs=specs + [hv, hv, hv],
        out_specs=[pl.BlockSpec((8, 64, 128), lambda g, c: (g, 0, c)),
                   pl.BlockSpec((None, None, 8, 64, 128), lambda g, c: (g, c, 0, 0, 0))],
        out_shape=[jax.ShapeDtypeStruct((16, 64, s), F32), jax.ShapeDtypeStruct((2, nc, 8, 64, 128), F32)],
        scratch_shapes=[pltpu.VMEM((8, 64, 128), F32)],
        compiler_params=_params("arbitrary", "arbitrary"),
    )(xt, dtr, dtc, xbc, xbc, alog, dbias, dskip)


def ssd_bwd(xt, dtr, dtc, xbc, alog, dbias, dskip, hs, dyt, name):
    s = xt.shape[2]
    nc = s // 128
    specs, hv, cix = _ssd_specs(nc, True)

    def body(x_ref, dr_ref, dc_ref, b_ref, c_ref, al_ref, db_ref, dk_ref, hs_ref, dy_ref,
             dx_out, ddr_out, ddc_out, dbm_out, dcm_out, dal_out, ddb_out, ddk_out, dh_scr):
        @pl.when(pl.program_id(1) == 0)
        def _():
            dh_scr[...] = jnp.zeros_like(dh_scr)
            dal_out[...] = jnp.zeros_like(dal_out)
            ddb_out[...] = jnp.zeros_like(ddb_out)
            ddk_out[...] = jnp.zeros_like(ddk_out)
        _, vjp = jax.vjp(_ssd_chunk, x_ref[...], dr_ref[...], dc_ref[...], b_ref[...], c_ref[...], hs_ref[...],
                         al_ref[...], db_ref[...], dk_ref[...])
        g = vjp((dy_ref[...], dh_scr[...]))
        dx_out[...] = g[0]
        ddr_out[...] = g[1]
        ddc_out[...] = g[2]
        dbm_out[...] = g[3]
        dcm_out[...] = g[4]
        dh_scr[...] = g[5]
        dal_out[...] += g[6]
        ddb_out[...] += g[7]
        ddk_out[...] += g[8]

    return pl.pallas_call(
        body, name=name, grid=(2, nc),
        in_specs=specs + [hv, hv, hv, pl.BlockSpec((None, None, 8, 64, 128), lambda g, c: (g, cix(c), 0, 0, 0)),
                          pl.BlockSpec((8, 64, 128), lambda g, c: (g, 0, cix(c)))],
        out_specs=[pl.BlockSpec((8, 64, 128), lambda g, c: (g, 0, cix(c))), pl.BlockSpec((8, 1, 128), lambda g, c: (g, 0, cix(c))),
                   pl.BlockSpec((8, 128, 1), lambda g, c: (g, cix(c), 0)), pl.BlockSpec((128, 128), lambda g, c: (cix(c), g)),
                   pl.BlockSpec((128, 128), lambda g, c: (cix(c), g)), hv, hv, hv],
        out_shape=[jax.ShapeDtypeStruct((16, 64, s), F32), jax.ShapeDtypeStruct((16, 1, s), F32),
                   jax.ShapeDtypeStruct((16, s, 1), F32), jax.ShapeDtypeStruct((s, 256), F32),
                   jax.ShapeDtypeStruct((s, 256), F32)] + [jax.ShapeDtypeStruct((16, 1, 1), F32)] * 3,
        scratch_shapes=[pltpu.VMEM((8, 64, 128), F32)],
        compiler_params=_params("arbitrary", "arbitrary"),
    )(xt, dtr, dtc, xbc, xbc, alog, dbias, dskip, hs, dyt)


def _merge(oa, ob, y, z, gla, glb, glc, x, g1, nw, ea, eb, ec, eo, wba, wbb, wbc, wout):
    gated = y * (z * _sig(z))
    sq = gated * gated
    left = lax.broadcasted_iota(jnp.int32, (1, D), 1) < 512
    ms0 = jnp.sum(jnp.where(left, sq, 0.0), axis=-1, keepdims=True) / 512.0
    ms1 = jnp.sum(jnp.where(left, 0.0, sq), axis=-1, keepdims=True) / 512.0
    oc = gated * jnp.where(left, lax.rsqrt(ms0 + EPS), lax.rsqrt(ms1 + EPS)) * nw
    ya, yb, yc = mm_nc(oa, wba) + ea, mm_nc(ob, wbb) + eb, mm_nc(oc, wbc) + ec
    merged = _sig(gla) * ya + _sig(glb) * yb + _sig(glc) * yc
    x1 = x + g1 * (mm_nc(merged, wout) + eo)
    return x1, (oc, merged)


def _merge_specs(tr):
    row = lambda w: pl.BlockSpec((tr, w), lambda i: (i, 0))
    acts = [row(D), row(512), row(D), pl.BlockSpec((tr, D), lambda i: (i, O_Z // D)),
            pl.BlockSpec((tr, 3 * D), lambda i: (i, 0)), row(D), pl.BlockSpec((8, D), lambda i: (0, 0))]
    cst = lambda r: pl.BlockSpec((r, D), lambda i: (0, 0), **CONST)
    return acts, [cst(D), cst(512), cst(D), cst(D)], row


def merge_fwd(oa, ob, y, proj, x, mvec, wba, wbb, wbc, wout, name):
    s = x.shape[0]
    tr = _pick(s, 256)
    acts, wts, row = _merge_specs(tr)

    def body(oa_ref, ob_ref, y_ref, z_ref, gl_ref, x_ref, mv_ref, wba_ref, wbb_ref, wbc_ref, wout_ref, o_ref):
        zero = jnp.zeros((1, D), F32)
        x1, _ = _merge(oa_ref[...], ob_ref[...], y_ref[...], z_ref[...], gl_ref[:, 0:D], gl_ref[:, D:2 * D],
                       gl_ref[:, 2 * D:3 * D], x_ref[...], mv_ref[0:1, :], mv_ref[1:2, :], zero, zero, zero, zero,
                       wba_ref[...], wbb_ref[...], wbc_ref[...], wout_ref[...])
        o_ref[...] = x1

    return pl.pallas_call(
        body, name=name, grid=(s // tr,), in_specs=acts + wts, out_specs=row(D),
        out_shape=jax.ShapeDtypeStruct((s, D), F32), compiler_params=_params("arbitrary"),
    )(oa, ob, y, proj, proj, x, mvec, wba, wbb, wbc, wout)


def merge_bwd(oa, ob, y, proj, x, mvec, wba, wbb, wbc, wout, dx1, name):
    s = x.shape[0]
    tr = _pick(s, 128)
    acts, wts, row = _merge_specs(tr)

    def body(oa_ref, ob_ref, y_ref, z_ref, gl_ref, x_ref, mv_ref, wba_ref, wbb_ref, wbc_ref, wout_ref, dx1_ref,
             doa_o, dob_o, dy_o, dz_o, dgl_o, dx_o, dmv_o, dya_o, dyb_o, dyc_o, dpre_o, oc_o, mg_o):
        zero = jnp.zeros((tr, D), F32)
        wts_ = (wba_ref[...], wbb_ref[...], wbc_ref[...], wout_ref[...])
        f = lambda *a: _merge(*a, *wts_)
        _, vjp, (oc, merged) = jax.vjp(
            f, oa_ref[...], ob_ref[...], y_ref[...], z_ref[...], gl_ref[:, 0:D], gl_ref[:, D:2 * D],
            gl_ref[:, 2 * D:3 * D], x_ref[...], mv_ref[0:1, :], mv_ref[1:2, :], zero, zero, zero, zero, has_aux=True)
        g = vjp(dx1_ref[...])
        doa_o[...] = g[0]
        dob_o[...] = g[1]
        dy_o[...] = g[2]
        dz_o[...] = g[3].astype(BF16)
        dgl_o[:, 0:D] = g[4].astype(BF16)
        dgl_o[:, D:2 * D] = g[5].astype(BF16)
        dgl_o[:, 2 * D:3 * D] = g[6].astype(BF16)
        dx_o[...] = g[7]

        @pl.when(pl.program_id(0) == 0)
        def _():
            dmv_o[...] = jnp.zeros_like(dmv_o)

        dmv_o[0:1, :] += g[8]
        dmv_o[1:2, :] += g[9]
        dya_o[...] = g[10].astype(BF16)
        dyb_o[...] = g[11].astype(BF16)
        dyc_o[...] = g[12].astype(BF16)
        dpre_o[...] = g[13].astype(BF16)
        oc_o[...] = oc.astype(BF16)
        mg_o[...] = merged.astype(BF16)

    sd = lambda w, dt: jax.ShapeDtypeStruct((s, w), dt)
    return pl.pallas_call(
        body, name=name, grid=(s // tr,), in_specs=acts + wts + [row(D)],
        out_specs=[row(D), row(512), row(D), row(D), row(3 * D), row(D), pl.BlockSpec((8, D), lambda i: (0, 0))] + [row(D)] * 6,
        out_shape=[sd(D, F32), sd(512, F32), sd(D, F32), sd(D, BF16), sd(3 * D, BF16), sd(D, F32),
                   jax.ShapeDtypeStruct((8, D), F32)] + [sd(D, BF16)] * 6,
        compiler_params=_params("arbitrary"),
    )(oa, ob, y, proj, proj, x, mvec, wba, wbb, wbc, wout, dx1)


def _conv3(u_scr, w_ref, first, rows, lanes):
    return sum(w_ref[k:k + 1, :] * u_scr[first + k:first + k + rows, lanes] for k in range(3))


def _ffn_tile_specs(tf, tile):
    def at(rows, off):
        return pl.BlockSpec((rows, tf), lambda *g: (0, off + tile(*g)))
    return [at(D, 0), at(D, FFN_NT), at(3, 0), at(3, FFN_NT), at(1, 0), at(1, FFN_NT)]


def ffn_fwd(x1, fvec, wup, cw, cb, wdn, name):
    s = x1.shape[0]
    tr, tf = _pick(s, 512), FFN_TILE
    lg, lv = slice(0, tf), slice(tf, 2 * tf)

    def body(x_ref, v_ref, wg_ref, wv_ref, cwg_ref, cwv_ref, cbg_ref, cbv_ref, wd_ref, x2_ref, h_ref, pre_ref,
             h_scr, u_scr, acc):
        i, t = pl.program_id(0), pl.program_id(1)

        @pl.when(t == 0)
        def _():
            @pl.when(i == 0)
            def _():
                h_scr[0:16, :] = jnp.zeros((16, D), BF16)

            @pl.when(i > 0)
            def _():
                h_scr[0:16, :] = h_scr[tr:tr + 16, :]

            h = (_rms(x_ref[...], v_ref[0:1, :], D) * (1.0 + v_ref[2:3, :]) + v_ref[1:2, :]).astype(BF16)
            h_scr[16:16 + tr, :] = h
            h_ref[...] = h
            acc[...] = jnp.zeros_like(acc)

        u_scr[:, lg] = jnp.dot(h_scr[...], wg_ref[...], preferred_element_type=F32)
        u_scr[:, lv] = jnp.dot(h_scr[...], wv_ref[...], preferred_element_type=F32)
        cg = _conv3(u_scr, cwg_ref, 14, tr, lg) + cbg_ref[...]
        cval = _conv3(u_scr, cwv_ref, 14, tr, lv) + cbv_ref[...]
        acc[...] += _raw(cg * _sig(cg) * cval, wd_ref[...], _NN)

        @pl.when(t == FFN_NT - 1)
        def _():
            pre_ref[...] = acc[...]
            x2_ref[...] = x_ref[...] + v_ref[3:4, :] * acc[...]

    row = pl.BlockSpec((tr, D), lambda i, t: (i, 0))
    return pl.pallas_call(
        body, name=name, grid=(s // tr, FFN_NT),
        in_specs=[row, pl.BlockSpec((8, D), lambda i, t: (0, 0))] + _ffn_tile_specs(tf, lambda i, t: t)
                 + [pl.BlockSpec((tf, D), lambda i, t: (t, 0))],
        out_specs=[row, row, row],
        out_shape=[jax.ShapeDtypeStruct((s, D), F32), jax.ShapeDtypeStruct((s, D), BF16), jax.ShapeDtypeStruct((s, D), F32)],
        scratch_shapes=[pltpu.VMEM((tr + 16, D), BF16), pltpu.VMEM((tr + 16, 2 * tf), F32), pltpu.VMEM((tr, D), F32)],
        compiler_params=_params("arbitrary", "arbitrary"),
    )(x1, fvec, wup, wup, cw, cw, cb, cb, wdn)


def ffn_bwd(h2, dx2, fvec, wup, cw, cb, wdn, name):
    s = h2.shape[0]
    tr, tf = _pick(s, 512), FFN_TILE
    ni, nb = s // tr, s // 16
    lg, lv = slice(0, tf), slice(tf, 2 * tf)

    def body(hp_ref, hm_ref, hn_ref, dm_ref, dn_ref, v_ref, wg_ref, wv_ref, cwg_ref, cwv_ref, cbg_ref, cbv_ref, wd_ref,
             dup_ref, act_ref, dcw_ref, u_scr, dc_scr):
        i = pl.program_id(1)
        hfull = jnp.concatenate([jnp.where(i > 0, hp_ref[...], jnp.zeros((16, D), BF16)), hm_ref[...],
                                 jnp.where(i < ni - 1, hn_ref[...], jnp.zeros((16, D), BF16))], axis=0)
        u_scr[:, lg] = jnp.dot(hfull, wg_ref[...], preferred_element_type=F32)
        u_scr[:, lv] = jnp.dot(hfull, wv_ref[...], preferred_element_type=F32)
        cg = _conv3(u_scr, cwg_ref, 14, tr + 16, lg) + cbg_ref[...]
        cval = _conv3(u_scr, cwv_ref, 14, tr + 16, lv) + cbv_ref[...]
        g2 = v_ref[3:4, :]
        dpre = jnp.concatenate([dm_ref[...] * g2, jnp.where(i < ni - 1, dn_ref[...], 0.0) * g2], axis=0)
        dact = _raw(dpre, wd_ref[...], _NT)
        sg = _sig(cg)
        sl = cg * sg
        dc_scr[:, lg] = dact * cval * (sg * (1.0 + cg * (1.0 - sg)))
        dc_scr[:, lv] = dact * sl
        act_ref[...] = (sl * cval)[0:tr, :].astype(BF16)

        @pl.when(i == 0)
        def _():
            dcw_ref[...] = jnp.zeros_like(dcw_ref)

        for half, lanes, cw_ref in ((0, lg, cwg_ref), (1, lv, cwv_ref)):
            dup_ref[half] = sum(cw_ref[k:k + 1, :] * dc_scr[2 - k:2 - k + tr, lanes] for k in range(3)).astype(BF16)
            dcm = dc_scr[0:tr, lanes]
            for k in range(3):
                dcw_ref[half, k:k + 1, :] += jnp.sum(dcm * u_scr[14 + k:14 + k + tr, lanes], axis=0, keepdims=True)
            dcw_ref[half, 3:4, :] += jnp.sum(dcm, axis=0, keepdims=True)

    r16 = tr // 16
    prev = lambda t, i: (jnp.maximum(i * r16 - 1, 0), 0)
    nxt = lambda t, i: (jnp.minimum((i + 1) * r16, nb - 1), 0)
    main = lambda t, i: (i, 0)
    return pl.pallas_call(
        body, name=name, grid=(FFN_NT, ni),
        in_specs=[pl.BlockSpec((16, D), prev), pl.BlockSpec((tr, D), main), pl.BlockSpec((16, D), nxt),
                  pl.BlockSpec((tr, D), main), pl.BlockSpec((16, D), nxt), pl.BlockSpec((8, D), lambda t, i: (0, 0))]
                 + _ffn_tile_specs(tf, lambda t, i: t) + [pl.BlockSpec((tf, D), lambda t, i: (t, 0))],
        out_specs=[pl.BlockSpec((2, tr, tf), lambda t, i: (0, i, t)), pl.BlockSpec((tr, tf), lambda t, i: (i, t)),
                   pl.BlockSpec((2, 8, tf), lambda t, i: (0, 0, t))],
        out_shape=[jax.ShapeDtypeStruct((2, s, FFN), BF16), jax.ShapeDtypeStruct((s, FFN), BF16),
                   jax.ShapeDtypeStruct((2, 8, FFN), F32)],
        scratch_shapes=[pltpu.VMEM((tr + 32, 2 * tf), F32), pltpu.VMEM((tr + 16, 2 * tf), F32)],
        compiler_params=_params("arbitrary", "arbitrary"),
    )(h2, h2, h2, dx2, dx2, fvec, wup, wup, cw, cw, cb, cb, wdn)


def loss_head(y, target):
    s = y.shape[0]
    tr = _pick(s, 512)

    def body(y_ref, t_ref, dx_ref, l_ref):
        @pl.when(pl.program_id(0) == 0)
        def _():
            l_ref[...] = jnp.zeros_like(l_ref)
        err = y_ref[...] - t_ref[...]
        dx_ref[...] = err / float(D)
        l_ref[...] += 0.5 * jnp.sum(jnp.sum(err * err, axis=-1, keepdims=True) / float(D), axis=0, keepdims=True)

    row = pl.BlockSpec((tr, D), lambda i: (i, 0))
    return pl.pallas_call(
        body, name="loss_head", grid=(s // tr,), in_specs=[row, row],
        out_specs=[row, pl.BlockSpec((8, 128), lambda i: (0, 0))],
        out_shape=[jax.ShapeDtypeStruct((s, D), F32), jax.ShapeDtypeStruct((8, 128), F32)],
        compiler_params=_params("arbitrary"),
    )(y, target)


def adamw(parts, w, m, v, name):
    nseg = len(parts)
    p, r, c = parts[0].shape
    tr = _pick(r, 256, 8)
    ni = r // tr

    def body(*refs):
        p_refs = refs[:nseg]
        w_ref, m_ref, v_ref, g_out, d_out, m_out, v_out, g_scr = refs[nseg:]
        for q in range(nseg):
            @pl.when(pl.program_id(0) == q)
            def _(q=q):
                g = p_refs[q][0].astype(F32)
                for j in range(1, p):
                    g = g + p_refs[q][j].astype(F32)
                g_scr[...] = g
        g = g_scr[...]
        mn = B1 * m_ref[...] + (1.0 - B1) * g
        vn = B2 * v_ref[...] + (1.0 - B2) * (g * g)
        m_hat = mn / (1.0 - B1 ** STEP)
        v_hat = vn / (1.0 - B2 ** STEP)
        g_out[...] = g
        d_out[...] = -LR * (m_hat / (jnp.sqrt(v_hat) + ADAM_EPS) + WD * w_ref[...])
        m_out[...] = mn
        v_out[...] = vn

    row = pl.BlockSpec((tr, c), lambda l, i: (l * ni + i, 0))
    part = lambda q: pl.BlockSpec((p, tr, c), lambda l, i: (0, jnp.clip((l - q) * ni + i, 0, ni - 1), 0))
    return pl.pallas_call(
        body, name=name, grid=(nseg, ni), in_specs=[part(q) for q in range(nseg)] + [row, row, row],
        out_specs=[row] * 4, out_shape=[jax.ShapeDtypeStruct((nseg * r, c), F32)] * 4,
        scratch_shapes=[pltpu.VMEM((tr, c), F32)],
        compiler_params=_params("arbitrary", "arbitrary"),
    )(*parts, w, m, v)


def _padc(a, n):
    return jnp.pad(a, [(0, 0)] * (a.ndim - 1) + [(0, n - a.shape[-1])])


def _swap16(a):
    return jnp.concatenate([a[..., 16:32], a[..., 0:16]], axis=-1)


def _win_layout(w):
    kr = w[:, 640:672]
    return jnp.concatenate([w[:, 3760:6832], w[:, 2208:3232], w[:, 1184:2208], w[:, 672:1184], w[:, 3232:3744],
                            w[:, 384:640], _padc(kr, 128), _padc(_swap16(kr), 128), _padc(w[:, 3744:3760], 128),
                            jnp.zeros((w.shape[0], 128), w.dtype), w[:, 0:384]], axis=1)


def _win_unlayout(g):
    kr = g[:, O_KR:O_KR + 32] + _swap16(g[:, O_KRS:O_KRS + 32])
    return jnp.concatenate([g[:, O_QL:O_QL + 384], g[:, O_CKV:O_CKV + 256], kr, g[:, O_PU:O_PU + 512], g[:, O_Z:O_Z + D],
                            g[:, O_XS:O_XS + D], g[:, O_BC:O_BC + 512], g[:, O_DT:O_DT + 16], g[:, O_G:O_G + 3 * D]], axis=1)


def _wq_layout(w):
    w = w.reshape(384, HEADS, 96).transpose(1, 0, 2)
    rope = w[:, :, 64:96]
    return jnp.concatenate([_padc(w[:, :, 0:64], 128), _padc(rope, 128), _padc(_swap16(rope), 128)], axis=2)


def _wq_unlayout(g):
    rope = g[:, :, 128:160] + _swap16(g[:, :, 256:288])
    return jnp.concatenate([g[:, :, 0:64], rope], axis=2).transpose(1, 0, 2).reshape(384, HEADS * 96)


def _wkv_layout(w):
    w = w.reshape(256, HEADS, 128).transpose(1, 0, 2)
    return jnp.concatenate([_padc(w[:, :, 0:64], 128), _padc(w[:, :, 64:128], 128)], axis=2)


def _wkv_unlayout(g):
    return jnp.concatenate([g[:, :, 0:64], g[:, :, 128:192]], axis=2).transpose(1, 0, 2).reshape(256, HEADS * 128)


def _wba_layout(w):
    return jnp.pad(w.reshape(HEADS, 64, D), ((0, 0), (0, 64), (0, 0))).reshape(HEADS * 128, D)


def _rows8(rows, width):
    out = jnp.stack([_padc(r.astype(F32), width) for r in rows])
    return jnp.pad(out, ((0, 8 - out.shape[0]), (0, 0)))


def _mla_vec(qa, kva, qn, kn):
    def row(n):
        return jnp.concatenate([_padc(n[0:64], 128), _padc(n[64:96], 128), _padc(_swap16(n[64:96]), 128)])
    return _rows8([qa, kva, row(qn), row(kn)], 512)


def _mla_unvec(g):
    def un(r):
        return jnp.concatenate([r[0:64], r[128:160] + _swap16(r[256:288])])
    return g[0, 0:384], g[1, 0:256], un(g[2]), un(g[3])


SMALL = (("ada_b", (6 * D,)), ("norm1_w", (D,)), ("q_a_norm", (384,)), ("kv_a_norm", (256,)), ("q_norm", (96,)),
         ("k_norm", (96,)), ("pool_w", (4, 128, 128)), ("pool_scale", (512,)), ("ssd_conv_b", (1536,)),
         ("ssd_dt_bias", (16,)), ("ssd_a_log", (16,)), ("ssd_d", (16,)), ("ssd_norm_w", (D,)), ("norm2_w", (D,)),
         ("ffn_conv_b", (2 * FFN,)), ("ssd_conv_w", (4, 1536)), ("ffn_conv_w", (3, 2 * FFN)))
SMALL_REPL = SMALL[:15]
SMALL_ROWS = 208


def _pack(per_layer, names):
    flat = jnp.concatenate([per_layer[l][n].reshape(-1).astype(F32) for n, _ in names for l in range(LAYERS)])
    return jnp.pad(flat, (0, SMALL_ROWS * D - flat.shape[0])).reshape(SMALL_ROWS, D)


def _unpack(packed, names):
    flat, out, off = packed.reshape(-1), {}, 0
    for n, shp in names:
        size = LAYERS * math.prod(shp)
        out[n] = flat[off:off + size].reshape((LAYERS,) + shp)
        off += size
    return out


GROUP_A = ("w_in", "w_q_b", "w_kv_b")
GROUP_B = ("w_branch", "w_out", "ffn_up", "ffn_down")
BIG = GROUP_A + GROUP_B
COL_SHARDED = ("w_in", "w_q_b", "w_kv_b", "ffn_up")


def _behind(arrs, tok):
    return [arrs[0] + tok[0, 0].astype(arrs[0].dtype)] + list(arrs[1:])


def _gathered_full(g, name):
    if name in COL_SHARDED:
        return g.transpose(1, 0, 2).reshape(g.shape[1], NDEV * g.shape[2])
    return g.reshape(NDEV * g.shape[1], g.shape[2])


def _to_shards(full, name):
    if name in COL_SHARDED:
        r, c = full.shape
        return full.reshape(r, NDEV, c // NDEV).transpose(1, 0, 2).astype(BF16)
    r, c = full.shape
    return full.reshape(NDEV, r // NDEV, c).astype(BF16)


def _fwd_a(x, lw, mod, cos2, sin2, l, tok):
    sh1, sc1, g1, sh2, sc2, g2 = [mod[j * D:(j + 1) * D] for j in range(6)]
    vec1 = _rows8([lw["norm1_w"], sh1, sc1], D) + tok[0, 0]
    proj, h1 = norm_proj_fwd(x, vec1, lw["win"], f"inproj_fwd{l}")
    q, k, v = mla_pre_fwd(proj, lw["wq"], lw["wkv"], lw["mla_vec"], cos2, sin2, f"mla_pre_fwd{l}")
    oa = mla_attn_fwd(q, k, v, f"mla_attn_fwd{l}")
    ob = pool_fwd(proj, lw["pool_w"], lw["pool_scale"].reshape(1, 512), f"pool_fwd{l}")
    xbc = conv_fwd(proj, lw["ssd_conv_w"], lw["ssd_conv_b"].reshape(1, 1536), f"conv_fwd{l}")
    s = x.shape[0]
    xt = xbc[:, 0:D].reshape(s, 16, 64).transpose(1, 2, 0)
    dt = proj[:, O_DT:O_DT + 16].T
    dtr, dtc = dt[:, None, :], dt[:, :, None]
    hv = lambda a: a.reshape(16, 1, 1)
    yt, hs = ssd_fwd(xt, dtr, dtc, xbc, hv(lw["ssd_a_log"]), hv(lw["ssd_dt_bias"]), hv(lw["ssd_d"]), f"ssd_fwd{l}")
    y = yt.transpose(2, 0, 1).reshape(s, D)
    return dict(x=x, vec1=vec1, proj=proj, h1=h1, q=q, k=k, v=v, oa=oa, ob=ob, xbc=xbc, xt=xt, dtr=dtr, dtc=dtc,
                hs=hs, y=y, mvec=_rows8([g1, lw["ssd_norm_w"]], D), fvec=_rows8([lw["norm2_w"], sh2, sc2, g2], D))


def _fwd_b(sv, lw, l, tok):
    sv["mvec"] = sv["mvec"] + tok[0, 0]
    x1 = merge_fwd(sv["oa"], sv["ob"], sv["y"], sv["proj"], sv["x"], sv["mvec"], lw["wba"], lw["wbb"], lw["wbc"],
                   lw["wout"], f"merge_fwd{l}")
    x2, h2, pre = ffn_fwd(x1, sv["fvec"], lw["wup"], lw["ffn_conv_w"], lw["ffn_conv_b"].reshape(1, 2 * FFN), lw["wdn"],
                          f"ffn_fwd{l}")
    sv.update(x1=x1, h2=h2, pre=pre)
    return x2


def _bwd_b(dx2, lw, sv, l, tok):
    grads, small = {}, {}
    fvec = sv["fvec"] + tok[0, 0]
    dup, act, dcw = ffn_bwd(sv["h2"], dx2, fvec, lw["wup"], lw["ffn_conv_w"], lw["ffn_conv_b"].reshape(1, 2 * FFN),
                            lw["wdn"], f"ffn_bwd{l}")
    grads["ffn_down"] = tn_matmul(act, dx2, f"dw_down{l}", scale=fvec[3:4])
    grads["ffn_up"] = tn_matmul(sv["h2"], dup, f"dw_up{l}")
    dx1, dfvec = norm_proj_bwd(sv["x1"], fvec, dup, lw["wup"], dx2, sv["pre"], f"ffn_norm_bwd{l}")
    small["ffn_conv_w"] = jnp.concatenate([dcw[0, 0:3], dcw[1, 0:3]], axis=1)
    small["ffn_conv_b"] = jnp.concatenate([dcw[0, 3], dcw[1, 3]])
    small["norm2_w"] = dfvec[0]
    (doa, dob, dy, dz, dgl, dx, dmvec, dya, dyb, dyc, dpre, oc, merged) = merge_bwd(
        sv["oa"], sv["ob"], sv["y"], sv["proj"], sv["x"], sv["mvec"], lw["wba"], lw["wbb"], lw["wbc"], lw["wout"], dx1,
        f"merge_bwd{l}")
    dwba = tn_matmul(sv["oa"], dya, f"dw_ba{l}").reshape(HEADS, 128, D)[:, 0:64].reshape(512, D)
    grads["w_branch"] = jnp.concatenate([dwba, tn_matmul(sv["ob"], dyb, f"dw_bb{l}"), tn_matmul(oc, dyc, f"dw_bc{l}")])
    grads["w_out"] = tn_matmul(merged, dpre, f"dw_out{l}")
    small["ssd_norm_w"] = dmvec[1]
    small["dmod_b"] = (dmvec[0], dfvec[1], dfvec[2], dfvec[3])
    return dx, dict(doa=doa, dob=dob, dy=dy, dz=dz, dgl=dgl), grads, small


def _bwd_a(dx, cot, lw, sv, cos2, sin2, l, tok, small):
    s = dx.shape[0]
    grads = {}
    doa, dob, dz, dgl = cot["doa"], cot["dob"], cot["dz"], cot["dgl"]
    dyt = (cot["dy"] + tok[0, 0]).reshape(s, 16, 64).transpose(1, 2, 0)
    hv = lambda a: a.reshape(16, 1, 1)
    dxt, ddtr, ddtc, dbm, dcm, dal, ddb, ddk = ssd_bwd(
        sv["xt"], sv["dtr"], sv["dtc"], sv["xbc"], hv(lw["ssd_a_log"]), hv(lw["ssd_dt_bias"]), hv(lw["ssd_d"]), sv["hs"],
        dyt, f"ssd_bwd{l}")
    small["ssd_a_log"], small["ssd_dt_bias"], small["ssd_d"] = dal.reshape(16), ddb.reshape(16), ddk.reshape(16)
    dact = jnp.concatenate([dxt.transpose(2, 0, 1).reshape(s, D), dbm, dcm], axis=1)
    dxbc, dscw, dscb = conv_bwd(sv["proj"], lw["ssd_conv_w"], lw["ssd_conv_b"].reshape(1, 1536), dact, f"conv_bwd{l}")
    small["ssd_conv_w"], small["ssd_conv_b"] = dscw, dscb.reshape(1536)
    ddt = (ddtr[:, 0, :] + ddtc[:, :, 0]).T
    du, dpw, dps = pool_bwd(sv["proj"], lw["pool_w"], lw["pool_scale"].reshape(1, 512), dob, f"pool_bwd{l}")
    small["pool_w"], small["pool_scale"] = dpw, dps.reshape(512)
    dq, dk, dv = mla_attn_bwd(sv["q"], sv["k"], sv["v"], doa, f"mla_attn_bwd{l}")
    dql, dckv, dkr, dkrs, dwq, dwkv, dmv = mla_pre_bwd(sv["proj"], lw["wq"], lw["wkv"], lw["mla_vec"], cos2, sin2,
                                                       dq, dk, dv, f"mla_pre_bwd{l}")
    grads["w_q_b"], grads["w_kv_b"] = _wq_unlayout(dwq), _wkv_unlayout(dwkv)
    small["q_a_norm"], small["kv_a_norm"], small["q_norm"], small["k_norm"] = _mla_unvec(dmv)
    dproj = jnp.concatenate([dgl, dxbc[:, 0:D], dz, du, dxbc[:, D:1536], dckv, dkr, dkrs,
                             _padc(ddt, 128).astype(BF16), jnp.zeros((s, 128), BF16), dql], axis=1)
    grads["w_in"] = _win_unlayout(tn_matmul(sv["h1"], dproj, f"dw_in{l}"))
    dx0, dvec1 = norm_proj_bwd(sv["x"], sv["vec1"], dproj, lw["win"], dx, None, f"inproj_bwd{l}")
    small["norm1_w"] = dvec1[0]
    small["ada_b"] = jnp.concatenate([dvec1[1], dvec1[2], *small.pop("dmod_b")])
    return dx0, grads, small


def kernel(x, c, positions, ada_w, ada_b, norm1_w, w_in, q_a_norm, w_q_b, kv_a_norm, w_kv_b, q_norm, k_norm, pool_w, pool_scale, ssd_conv_w, ssd_conv_b, ssd_dt_bias, ssd_a_log, ssd_d, ssd_norm_w, w_branch, w_out, norm2_w, ffn_up, ffn_conv_w, ffn_conv_b, ffn_down, loss_target, m_ada_w, m_ada_b, m_norm1_w, m_w_in, m_q_a_norm, m_w_q_b, m_kv_a_norm, m_w_kv_b, m_q_norm, m_k_norm, m_pool_w, m_pool_scale, m_ssd_conv_w, m_ssd_conv_b, m_ssd_dt_bias, m_ssd_a_log, m_ssd_d, m_ssd_norm_w, m_w_branch, m_w_out, m_norm2_w, m_ffn_up, m_ffn_conv_w, m_ffn_conv_b, m_ffn_down, v_ada_w, v_ada_b, v_norm1_w, v_w_in, v_q_a_norm, v_w_q_b, v_kv_a_norm, v_w_kv_b, v_q_norm, v_k_norm, v_pool_w, v_pool_scale, v_ssd_conv_w, v_ssd_conv_b, v_ssd_dt_bias, v_ssd_a_log, v_ssd_d, v_ssd_norm_w, v_w_branch, v_w_out, v_norm2_w, v_ffn_up, v_ffn_conv_w, v_ffn_conv_b, v_ffn_down):
    p = dict(ada_w=ada_w, ada_b=ada_b, norm1_w=norm1_w, w_in=w_in, q_a_norm=q_a_norm, w_q_b=w_q_b, kv_a_norm=kv_a_norm,
             w_kv_b=w_kv_b, q_norm=q_norm, k_norm=k_norm, pool_w=pool_w, pool_scale=pool_scale, ssd_conv_w=ssd_conv_w,
             ssd_conv_b=ssd_conv_b, ssd_dt_bias=ssd_dt_bias, ssd_a_log=ssd_a_log, ssd_d=ssd_d, ssd_norm_w=ssd_norm_w,
             w_branch=w_branch, w_out=w_out, norm2_w=norm2_w, ffn_up=ffn_up, ffn_conv_w=ffn_conv_w, ffn_conv_b=ffn_conv_b,
             ffn_down=ffn_down)
    mom = dict(ada_w=m_ada_w, ada_b=m_ada_b, norm1_w=m_norm1_w, w_in=m_w_in, q_a_norm=m_q_a_norm, w_q_b=m_w_q_b,
               kv_a_norm=m_kv_a_norm, w_kv_b=m_w_kv_b, q_norm=m_q_norm, k_norm=m_k_norm, pool_w=m_pool_w,
               pool_scale=m_pool_scale, ssd_conv_w=m_ssd_conv_w, ssd_conv_b=m_ssd_conv_b, ssd_dt_bias=m_ssd_dt_bias,
               ssd_a_log=m_ssd_a_log, ssd_d=m_ssd_d, ssd_norm_w=m_ssd_norm_w, w_branch=m_w_branch, w_out=m_w_out,
               norm2_w=m_norm2_w, ffn_up=m_ffn_up, ffn_conv_w=m_ffn_conv_w, ffn_conv_b=m_ffn_conv_b, ffn_down=m_ffn_down)
    var = dict(ada_w=v_ada_w, ada_b=v_ada_b, norm1_w=v_norm1_w, w_in=v_w_in, q_a_norm=v_q_a_norm, w_q_b=v_w_q_b,
               kv_a_norm=v_kv_a_norm, w_kv_b=v_w_kv_b, q_norm=v_q_norm, k_norm=v_k_norm, pool_w=v_pool_w,
               pool_scale=v_pool_scale, ssd_conv_w=v_ssd_conv_w, ssd_conv_b=v_ssd_conv_b, ssd_dt_bias=v_ssd_dt_bias,
               ssd_a_log=v_ssd_a_log, ssd_d=v_ssd_d, ssd_norm_w=v_ssd_norm_w, w_branch=v_w_branch, w_out=v_w_out,
               norm2_w=v_norm2_w, ffn_up=v_ffn_up, ffn_conv_w=v_ffn_conv_w, ffn_conv_b=v_ffn_conv_b, ffn_down=v_ffn_down)
    names = list(p)
    me = 4 * lax.axis_index("x") + 2 * lax.axis_index("y") + lax.axis_index("c")
    xs, tgt = x[0], loss_target[0]
    s = xs.shape[0]

    inv_freq = ROPE_THETA ** (-jnp.arange(0, 32, 2, dtype=F32) / 32.0)
    ang = positions[0].astype(F32)[:, None] * inv_freq
    cos, sin = jnp.cos(ang), jnp.sin(ang)
    cos2 = _padc(jnp.concatenate([cos, cos], axis=1), 128)
    sin2 = _padc(jnp.concatenate([-sin, sin], axis=1), 128)

    (c_all,) = all_to_all([c], [True], "gather_c")
    modp, cact = ada_mod(jnp.pad(c_all.reshape(NDEV, D), ((0, 8), (0, 0))), ada_w)
    (mod_in,) = all_to_all([modp[:, 0:NDEV].transpose(1, 0, 2)], [False], "scatter_mod")

    mod = mod_in.transpose(1, 0, 2).reshape(LAYERS, 6 * D) + ada_b

    conv_shards = jnp.concatenate([ssd_conv_w.reshape(-1), ffn_conv_w.reshape(-1)])
    (conv_all,) = all_to_all([conv_shards], [True], "gather_conv_w")
    n1 = LAYERS * 4 * 192
    scw = conv_all[:, :n1].reshape(NDEV, LAYERS, 4, 192).transpose(1, 2, 0, 3).reshape(LAYERS, 4, 1536)
    fcw = conv_all[:, n1:].reshape(NDEV, LAYERS, 3, 704).transpose(1, 2, 0, 3).reshape(LAYERS, 3, 2 * FFN)

    def weights_a(gathered, l):
        full = {n: _gathered_full(g, n) for n, g in zip(GROUP_A, gathered)}
        lw = {n: p[n][l] for n in names}
        lw.update(win=_win_layout(full["w_in"]), wq=_wq_layout(full["w_q_b"]), wkv=_wkv_layout(full["w_kv_b"]),
                  ssd_conv_w=scw[l], ffn_conv_w=fcw[l],
                  mla_vec=_mla_vec(lw["q_a_norm"], lw["kv_a_norm"], lw["q_norm"], lw["k_norm"]))
        return lw

    def weights_b(gathered):
        full = {n: _gathered_full(g, n) for n, g in zip(GROUP_B, gathered)}
        wb = full["w_branch"]
        return dict(wba=_wba_layout(wb[0:512]), wbb=wb[512:1024], wbc=wb[1024:2048], wout=full["w_out"],
                    wup=full["ffn_up"], wdn=full["ffn_down"])

    shards = lambda group, l: [p[n][l].astype(BF16) for n in group]
    bc = lambda group: [True] * len(group)
    lws, saved = [None] * LAYERS, [None] * LAYERS
    st, tok = exchange_start(shards(GROUP_A, 0), bc(GROUP_A), "gather_a0_start")
    got, tok = exchange_wait(st, tok, "gather_a0_wait")
    h = xs
    for l in range(LAYERS):
        st, tok = exchange_start(_behind(shards(GROUP_B, l), tok), bc(GROUP_B), f"gather_b{l}_start")
        lws[l] = weights_a(got, l)
        saved[l] = _fwd_a(h, lws[l], mod[l], cos2, sin2, l, tok)
        got, tok = exchange_wait(st, saved[l]["y"], f"gather_b{l}_wait")
        lws[l].update(weights_b(got))
        if l + 1 < LAYERS:
            st, tok = exchange_start(_behind(shards(GROUP_A, l + 1), tok), bc(GROUP_A), f"gather_a{l + 1}_start")
        h = _fwd_b(saved[l], lws[l], l, tok)
        if l + 1 < LAYERS:
            got, tok = exchange_wait(st, h, f"gather_a{l + 1}_wait")
    dx, lpart = loss_head(h, tgt)
    loss = lax.psum(lpart[0, 0], ("x", "y", "c"))

    grads, small, parts = [None] * LAYERS, [None] * LAYERS, {}
    to_shards = lambda g, group: [_to_shards(g[n], n) for n in group]
    nb = lambda group: [False] * len(group)
    st = None
    for l in reversed(range(LAYERS)):
        dx, cot, gb, small[l] = _bwd_b(dx, lws[l], saved[l], l, tok)
        if st is not None:
            parts[("a", l + 1)], tok = exchange_wait(st, dx, f"scatter_a{l + 1}_wait")
        st, tok = exchange_start(_behind(to_shards(gb, GROUP_B), tok), nb(GROUP_B), f"scatter_b{l}_start")
        dx, ga, small[l] = _bwd_a(dx, cot, lws[l], saved[l], cos2, sin2, l, tok, small[l])
        parts[("b", l)], tok = exchange_wait(st, dx, f"scatter_b{l}_wait")
        st, tok = exchange_start(_behind(to_shards(ga, GROUP_A), tok), nb(GROUP_A), f"scatter_a{l}_start")
    scatter0 = st

    dmod = jnp.stack([small[l]["ada_b"] for l in range(LAYERS)]) + tok[0, 0]
    (dmod_in,) = all_to_all([dmod.reshape(LAYERS, NDEV, 768).transpose(1, 0, 2)], [False], "scatter_dmod")
    dmod16 = jnp.pad(dmod_in, ((0, 8), (0, 0), (0, 0)))
    g_ada = jnp.stack([tn_matmul(cact, dmod16[:, l], f"dw_ada{l}") for l in range(LAYERS)])
    out = {}
    flat = lambda a: a.reshape(LAYERS * D, 768)
    out["ada_w"] = [r.reshape(ada_w.shape) for r in
                    adamw([flat(g_ada)[None]], flat(ada_w), flat(m_ada_w), flat(v_ada_w), "adamw_ada_w")]

    (small_all,) = all_to_all([_pack(small, SMALL) + tok[0, 0]], [True], "gather_small")
    zeros = jnp.zeros((SMALL_ROWS, D), F32)
    g_small = _unpack(adamw([small_all], zeros, zeros, zeros, "sum_small")[0], SMALL)
    per = lambda d, nm: [{n: d[n][l] for n, _ in nm} for l in range(LAYERS)]
    res = adamw([_pack(per(g_small, SMALL_REPL), SMALL_REPL)[None]], _pack(per(p, SMALL_REPL), SMALL_REPL),
                _pack(per(mom, SMALL_REPL), SMALL_REPL), _pack(per(var, SMALL_REPL), SMALL_REPL), "adamw_small")
    after = res[0][0:8, 0:128] + out["ada_w"][0][0, 0:8, 0:128]
    res = [_unpack(r, SMALL_REPL) for r in res]
    for n, _ in SMALL_REPL:
        out[n] = [r[n] for r in res]
    for n, k, w in (("ssd_conv_w", 4, 192), ("ffn_conv_w", 3, 704)):
        g_mine = lax.dynamic_slice(g_small[n], (0, 0, me * w), (LAYERS, k, w))
        f2 = lambda a: jnp.pad(a.reshape(LAYERS * k, w), ((0, 8 - LAYERS * k), (0, 0)))
        res = adamw([f2(g_mine)[None]], f2(p[n]), f2(mom[n]), f2(var[n]), f"adamw_{n}")
        out[n] = [r[0:LAYERS * k].reshape(LAYERS, k, w) for r in res]

    parts[("a", 0)], _ = exchange_wait(scatter0, after, "scatter_a0_wait")
    for n in BIG:
        grp, idx = ("a", GROUP_A.index(n)) if n in GROUP_A else ("b", GROUP_B.index(n))
        p0, p1 = parts[(grp, 0)][idx], parts[(grp, 1)][idx]
        shp = p[n].shape
        flat = lambda a: a.reshape(shp[0] * shp[1], shp[2])
        res = adamw([p0, p1], flat(p[n]), flat(mom[n]), flat(var[n]), f"adamw_{n}")
        out[n] = [r.reshape(shp) for r in res]

    outs = [loss, dx[None]]
    for q in range(4):
        outs += [out[n][q] for n in names]
    return tuple(outs)
```

```python
import functools
import math

import jax
import jax.numpy as jnp
from jax import lax
from jax.experimental import pallas as pl
from jax.experimental.pallas import tpu as pltpu

F32, BF16 = jnp.float32, jnp.bfloat16
EPS = 1e-6
D = 1024
NDEV = 8
LAYERS = 2
HEADS = 8
FFN = 2816
FFN_TILE = 1408
FFN_NT = FFN // FFN_TILE
ATT_SCALE = 96 ** -0.5
ROPE_THETA = 10000.0
LR, B1, B2, ADAM_EPS, WD, STEP = 0.001, 0.9, 0.999, 1e-08, 0.01, 10

O_G, O_XS, O_Z, O_PU, O_BC, O_CKV, O_KR, O_KRS, O_DT, O_QL = 0, 3072, 4096, 5120, 5632, 6144, 6400, 6528, 6656, 6912
NPROJ = 7296
CONST = dict(pipeline_mode=pl.Buffered(1))


def _pick(n, cap, mult=128):
    if n <= cap:
        return n
    best = None
    for t in range(mult, cap + 1, mult):
        if n % t == 0:
            best = t
    assert best is not None, (n, cap, mult)
    return best


def _sig(x):
    return 1.0 / (1.0 + jnp.exp(-x))


def _rms(x, w, n):
    return x * lax.rsqrt(jnp.sum(x * x, axis=-1, keepdims=True) / n + EPS) * w


def _raw(a, b, dims):
    return lax.dot_general(a.astype(BF16), b.astype(BF16), dims, preferred_element_type=F32)


_NN = (((1,), (0,)), ((), ()))
_NT = (((1,), (1,)), ((), ()))
_TN = (((0,), (0,)), ((), ()))
_BNN = (((2,), (1,)), ((0,), (0,)))
_BNT = (((2,), (2,)), ((0,), (0,)))
_BTN = (((1,), (1,)), ((0,), (0,)))


@jax.custom_vjp
def mm_nn(a, b):
    return _raw(a, b, _NN)


mm_nn.defvjp(lambda a, b: (_raw(a, b, _NN), (a, b)),
             lambda r, g: (_raw(g, r[1], _NT), _raw(r[0], g, _TN)))


@jax.custom_vjp
def mm_nc(a, b):
    return _raw(a, b, _NN)


mm_nc.defvjp(lambda a, b: (_raw(a, b, _NN), b),
             lambda b, g: (_raw(g, b, _NT), jnp.zeros_like(b)))


@jax.custom_vjp
def mm_nt(a, b):
    return _raw(a, b, _NT)


mm_nt.defvjp(lambda a, b: (_raw(a, b, _NT), (a, b)),
             lambda r, g: (_raw(g, r[1], _NN), _raw(g, r[0], _TN)))


@jax.custom_vjp
def bmm_nn(a, b):
    return _raw(a, b, _BNN)


bmm_nn.defvjp(lambda a, b: (_raw(a, b, _BNN), (a, b)),
              lambda r, g: (_raw(g, r[1], _BNT), _raw(r[0], g, _BTN)))


@jax.custom_vjp
def bmm_nt(a, b):
    return _raw(a, b, _BNT)


bmm_nt.defvjp(lambda a, b: (_raw(a, b, _BNT), (a, b)),
              lambda r, g: (_raw(g, r[1], _BNN), _raw(g, r[0], _BTN)))


@jax.custom_vjp
def softplus(x):
    t = jnp.exp(-jnp.abs(x))
    u = 1.0 + t
    one = u == 1.0
    l1p = jnp.where(one, t, jnp.log(u) * (t / jnp.where(one, 1.0, u - 1.0)))
    return jnp.maximum(x, 0.0) + l1p


softplus.defvjp(lambda x: (softplus(x), x), lambda x, g: (g * _sig(x),))


def _params(*sem):
    return pltpu.CompilerParams(dimension_semantics=sem, vmem_limit_bytes=56 * 1024 * 1024)


def all_to_all(arrs, bcast, name):
    n = len(arrs)
    out_shapes = [jax.ShapeDtypeStruct(((NDEV,) + a.shape) if b else a.shape, a.dtype) for a, b in zip(arrs, bcast)]

    def body(*refs):
        ins, outs, token = refs[:n], refs[n:2 * n], refs[2 * n]
        send_sems, recv_sems, local_sems = refs[2 * n + 1:]
        me, remote = _exchange_copies(ins, outs, bcast, send_sems, recv_sems)
        local = [pltpu.make_async_copy(ins[j] if bcast[j] else ins[j].at[me], outs[j].at[me], local_sems.at[j])
                 for j in range(n)]
        for cp in local + remote:
            cp.start()
        for cp in remote + local:
            cp.wait()
        token[...] = jnp.zeros_like(token)

    any_spec = pl.BlockSpec(memory_space=pl.ANY)
    res = pl.pallas_call(
        body, name=name, out_shape=out_shapes + [jax.ShapeDtypeStruct((8, 128), F32)], in_specs=[any_spec] * n,
        out_specs=[any_spec] * n + [pl.BlockSpec(memory_space=pltpu.VMEM)],
        scratch_shapes=[pltpu.SemaphoreType.DMA((7 * n,)), pltpu.SemaphoreType.DMA((7 * n,)),
                        pltpu.SemaphoreType.DMA((n,))],
        compiler_params=pltpu.CompilerParams(has_side_effects=True),
    )(*arrs)
    return res[:n], res[n]


def _peers():
    x, y, c = lax.axis_index("x"), lax.axis_index("y"), lax.axis_index("c")
    out = []
    for k in range(1, NDEV):
        px, py, pc = x ^ ((k >> 2) & 1), y ^ ((k >> 1) & 1), c ^ (k & 1)
        out.append(((px, py, pc), 4 * px + 2 * py + pc))
    return 4 * x + 2 * y + c, out


def _exchange_copies(ins, lands, bcast, send_sems, recv_sems):
    me, peers = _peers()
    n, copies = len(ins), []
    for k, (dev, lin) in enumerate(peers):
        for j in range(n):
            copies.append(pltpu.make_async_remote_copy(
                src_ref=ins[j] if bcast[j] else ins[j].at[lin], dst_ref=lands[j].at[me],
                send_sem=send_sems.at[k * n + j], recv_sem=recv_sems.at[k * n + j],
                device_id=dev, device_id_type=pl.DeviceIdType.MESH))
    return me, copies


_HBM = pl.BlockSpec(memory_space=pltpu.HBM)
_SEM = pl.BlockSpec(memory_space=pltpu.SEMAPHORE)
_EFFECT = pltpu.SideEffectType.DATAFLOW_SIDE_EFFECTING


def exchange_start(arrs, bcast, name):
    n = len(arrs)
    land_shapes = [((NDEV,) + a.shape) if b else a.shape for a, b in zip(arrs, bcast)]

    def body(*refs):
        ins, lands = refs[:n], refs[n:2 * n]
        send_sems, recv_sems = refs[2 * n], refs[2 * n + 1]
        token = refs[-1]
        _, copies = _exchange_copies(ins, lands, bcast, send_sems, recv_sems)
        for cp in copies:
            cp.start()
        token[...] = jnp.zeros_like(token)

    hbm = lambda shp, a: pltpu.HBM(shp, a.dtype)
    res = pl.pallas_call(
        body, name=name,
        out_shape=[pltpu.SemaphoreType.DMA((7 * n,)), pltpu.SemaphoreType.DMA((7 * n,))]
                  + [hbm(a.shape, a) for a in arrs] + [hbm(s_, a) for s_, a in zip(land_shapes, arrs)]
                  + [jax.ShapeDtypeStruct((8, 128), F32)],
        in_specs=[_HBM] * (2 * n), out_specs=[_SEM, _SEM] + [_HBM] * (2 * n) + [pl.BlockSpec(memory_space=pltpu.VMEM)],
        input_output_aliases={i: 2 + i for i in range(2 * n)},
        compiler_params=pltpu.CompilerParams(has_side_effects=_EFFECT),
    )(*[pltpu.with_memory_space_constraint(a, pltpu.HBM) for a in arrs],
      *[pltpu.with_memory_space_constraint(lax.empty(s_, a.dtype), pltpu.HBM) for s_, a in zip(land_shapes, arrs)])
    return (res[0], res[1], res[2:2 + n], res[2 + n:2 + 2 * n], tuple(bcast)), res[-1]


def exchange_wait(state, after, name):
    send_sems, recv_sems, ins, lands, bcast = state
    n = len(ins)

    def body(*refs):
        in_refs, land_refs = refs[:n], refs[n:2 * n]
        s_sems, r_sems = refs[2 * n], refs[2 * n + 1]
        token = refs[-1]
        _, copies = _exchange_copies(in_refs, land_refs, bcast, s_sems, r_sems)
        for cp in copies:
            cp.wait_send()
            cp.wait_recv()
        token[...] = jnp.zeros_like(token)

    res = pl.pallas_call(
        body, name=name,
        out_shape=[pltpu.HBM(a.shape, a.dtype) for a in ins] + [pltpu.HBM(a.shape, a.dtype) for a in lands]
                  + [jax.ShapeDtypeStruct((8, 128), F32)],
        in_specs=[_HBM] * (2 * n) + [_SEM, _SEM, pl.BlockSpec(memory_space=pl.ANY)],
        out_specs=[_HBM] * (2 * n) + [pl.BlockSpec(memory_space=pltpu.VMEM)],
        input_output_aliases={i: i for i in range(2 * n)},
        compiler_params=pltpu.CompilerParams(has_side_effects=_EFFECT),
    )(*ins, *lands, send_sems, recv_sems, after)
    me = 4 * lax.axis_index("x") + 2 * lax.axis_index("y") + lax.axis_index("c")
    got = []
    for j in range(n):
        own = res[j][None] if bcast[j] else lax.dynamic_index_in_dim(res[j], me, 0, keepdims=True)
        got.append(lax.dynamic_update_slice_in_dim(res[n + j], own, me, axis=0))
    return got, res[-1]


def norm_proj_fwd(x, vec, w, name):
    s, n = x.shape[0], w.shape[1]
    tr, tn = _pick(s, 512), _pick(n, 2560)
    jdt, odt = O_DT // tn, O_DT % tn

    def body(x_ref, v_ref, w_ref, o_ref, h_ref, dt_ref, h_scr):
        @pl.when(pl.program_id(1) == 0)
        def _():
            h = _rms(x_ref[...], v_ref[0:1, :], D) * (1.0 + v_ref[2:3, :]) + v_ref[1:2, :]
            h_scr[...] = h.astype(BF16)
            h_ref[...] = h.astype(BF16)
        res = jnp.dot(h_scr[...], w_ref[...], preferred_element_type=F32)
        o_ref[...] = res

        @pl.when(pl.program_id(1) == jdt)
        def _():
            dt_ref[...] = res[:, odt:odt + 128]

    return pl.pallas_call(
        body, name=name, grid=(s // tr, n // tn),
        in_specs=[pl.BlockSpec((tr, D), lambda i, j: (i, 0)), pl.BlockSpec((8, D), lambda i, j: (0, 0)),
                  pl.BlockSpec((D, tn), lambda i, j: (0, j))],
        out_specs=[pl.BlockSpec((tr, tn), lambda i, j: (i, j)), pl.BlockSpec((tr, D), lambda i, j: (i, 0)),
                   pl.BlockSpec((tr, 128), lambda i, j: (i, 0))],
        out_shape=[jax.ShapeDtypeStruct((s, n), F32), jax.ShapeDtypeStruct((s, D), BF16),
                   jax.ShapeDtypeStruct((s, 128), F32)],
        scratch_shapes=[pltpu.VMEM((tr, D), BF16)],
        compiler_params=_params("arbitrary", "arbitrary"),
    )(x, vec, w)


def _col_tiles(arr, cap):
    if arr.ndim == 2:
        n = arr.shape[1]
        t = _pick(n, cap)
        return n, t, lambda rows, ix: pl.BlockSpec((rows, t), lambda *g: ix(*g))
    width = arr.shape[2]
    t = _pick(width, cap)
    per = width // t

    def spec(rows, ix):
        def index(*g):
            r, j = ix(*g)
            return (j // per, r, j % per)
        return pl.BlockSpec((None, rows, t), index)
    return arr.shape[0] * width, t, spec


def norm_proj_bwd(x, vec, dp, w, dx_in, aux, name):
    s = x.shape[0]
    tr = _pick(s, 512)
    n, tk, dp_spec = _col_tiles(dp, 2560)
    nk, has_aux = n // tk, aux is not None

    def body(*refs):
        if has_aux:
            x_ref, v_ref, dp_ref, w_ref, dxin_ref, aux_ref, dx_ref, dv_ref, acc = refs
        else:
            x_ref, v_ref, dp_ref, w_ref, dxin_ref, dx_ref, dv_ref, acc = refs
        i, k = pl.program_id(0), pl.program_id(1)

        @pl.when(k == 0)
        def _():
            acc[...] = jnp.zeros_like(acc)

        acc[...] += _raw(dp_ref[...], w_ref[...], _NT)

        @pl.when(k == nk - 1)
        def _():
            f = lambda xx, nw, sh, sc: _rms(xx, nw, D) * (1.0 + sc) + sh
            _, vjp = jax.vjp(f, x_ref[...], v_ref[0:1, :], v_ref[1:2, :], v_ref[2:3, :])
            dx, dnw, dsh, dsc = vjp(acc[...])
            dx_ref[...] = dxin_ref[...] + dx

            @pl.when(i == 0)
            def _():
                dv_ref[...] = jnp.zeros_like(dv_ref)

            dv_ref[0:1, :] += dnw
            dv_ref[1:2, :] += dsh
            dv_ref[2:3, :] += dsc
            if has_aux:
                dv_ref[3:4, :] += jnp.sum(dxin_ref[...] * aux_ref[...], axis=0, keepdims=True)

    row = pl.BlockSpec((tr, D), lambda i, k: (i, 0))
    in_specs = [row, pl.BlockSpec((8, D), lambda i, k: (0, 0)), dp_spec(tr, lambda i, k: (i, k)),
                pl.BlockSpec((D, tk), lambda i, k: (0, k)), row] + ([row] if has_aux else [])
    args = [x, vec, dp, w, dx_in] + ([aux] if has_aux else [])
    return pl.pallas_call(
        body, name=name, grid=(s // tr, nk), in_specs=in_specs,
        out_specs=[row, pl.BlockSpec((8, D), lambda i, k: (0, 0))],
        out_shape=[jax.ShapeDtypeStruct((s, D), F32), jax.ShapeDtypeStruct((8, D), F32)],
        scratch_shapes=[pltpu.VMEM((tr, D), F32)],
        compiler_params=_params("arbitrary", "arbitrary"),
    )(*args)


def tn_matmul(a, b, name, scale=None):
    s, m = a.shape
    ts, tm = _pick(s, 512, 16), _pick(m, 1408)
    n, tn, b_spec = _col_tiles(b, 2560)
    ns, has_scale = s // ts, scale is not None

    def body(*refs):
        if has_scale:
            a_ref, b_ref, sc_ref, o_ref = refs
        else:
            a_ref, b_ref, o_ref = refs
        k = pl.program_id(2)

        @pl.when(k == 0)
        def _():
            o_ref[...] = jnp.zeros_like(o_ref)

        o_ref[...] += _raw(a_ref[...], b_ref[...], _TN)
        if has_scale:
            @pl.when(k == ns - 1)
            def _():
                o_ref[...] = o_ref[...] * sc_ref[...]

    in_specs = [pl.BlockSpec((ts, tm), lambda i, j, k: (k, i)), b_spec(ts, lambda i, j, k: (k, j))]
    if has_scale:
        in_specs.append(pl.BlockSpec((1, tn), lambda i, j, k: (0, j)))
    return pl.pallas_call(
        body, name=name, grid=(m // tm, n // tn, ns), in_specs=in_specs,
        out_specs=pl.BlockSpec((tm, tn), lambda i, j, k: (i, j)),
        out_shape=jax.ShapeDtypeStruct((m, n), F32),
        compiler_params=_params("arbitrary", "arbitrary", "arbitrary"),
    )(*([a, b] + ([scale] if has_scale else [])))


def ada_mod(c16, w):
    ncol = w.shape[2]

    def body(c_ref, w_ref, o_ref, a_ref):
        cc = c_ref[...]
        act = cc * _sig(cc)
        a_ref[...] = act
        o_ref[...] = _raw(act, w_ref[...], _NN)

    return pl.pallas_call(
        body, name="ada_mod", grid=(LAYERS,),
        in_specs=[pl.BlockSpec((16, D), lambda l: (0, 0)), pl.BlockSpec((None, D, ncol), lambda l: (l, 0, 0))],
        out_specs=[pl.BlockSpec((None, 16, ncol), lambda l: (l, 0, 0)), pl.BlockSpec((16, D), lambda l: (0, 0))],
        out_shape=[jax.ShapeDtypeStruct((LAYERS, 16, ncol), F32), jax.ShapeDtypeStruct((16, D), F32)],
        compiler_params=_params("arbitrary"),
    )(c16, w)


def _mla_head(q_lat, c_kv, kr, krs, wqn, wqr, wqrs, wkn, wv, qa_w, kva_w, qn_w, qr_w, qrs_w, kn_w, kr_w, krs_w,
              cos2, sin2):
    qn = _rms(q_lat, qa_w, 384.0)
    kvn = _rms(c_kv, kva_w, 256.0)
    qnope = _rms(mm_nn(qn, wqn), qn_w, 64.0)
    qr, qrs = mm_nn(qn, wqr), mm_nn(qn, wqrs)
    rq = lax.rsqrt(jnp.sum(qr * qr, axis=-1, keepdims=True) / 32.0 + EPS)
    qrope = rq * (qr * qr_w * cos2 + qrs * qrs_w * sin2)
    knope = _rms(mm_nn(kvn, wkn), kn_w, 64.0)
    v = mm_nn(kvn, wv)
    rk = lax.rsqrt(jnp.sum(kr * kr, axis=-1, keepdims=True) / 32.0 + EPS)
    krope = rk * (kr * kr_w * cos2 + krs * krs_w * sin2)
    return qnope, qrope, knope, krope, v


def _mla_vec_pieces(v_ref):
    return (v_ref[0:1, 0:384], v_ref[1:2, 0:256], v_ref[2:3, 0:128], v_ref[2:3, 128:256], v_ref[2:3, 256:384],
            v_ref[3:4, 0:128], v_ref[3:4, 128:256], v_ref[3:4, 256:384])


def _mla_in_specs(tr):
    return [pl.BlockSpec((tr, 384), lambda i: (i, O_QL // 384)), pl.BlockSpec((tr, 256), lambda i: (i, O_CKV // 256)),
            pl.BlockSpec((tr, 128), lambda i: (i, O_KR // 128)), pl.BlockSpec((tr, 128), lambda i: (i, O_KRS // 128)),
            pl.BlockSpec((HEADS, 384, 384), lambda i: (0, 0, 0), **CONST),
            pl.BlockSpec((HEADS, 256, 256), lambda i: (0, 0, 0), **CONST),
            pl.BlockSpec((8, 512), lambda i: (0, 0)),
            pl.BlockSpec((tr, 128), lambda i: (i, 0)), pl.BlockSpec((tr, 128), lambda i: (i, 0))]


def mla_pre_fwd(proj, wq, wkv, vec, cos2, sin2, name):
    s = proj.shape[0]
    tr = _pick(s, 256)

    def body(ql_ref, ckv_ref, kr_ref, krs_ref, wq_ref, wkv_ref, v_ref, cos_ref, sin_ref, q_out, k_out, v_out):
        acts = (ql_ref[...], ckv_ref[...], kr_ref[...], krs_ref[...])
        vp = _mla_vec_pieces(v_ref)
        for h in range(HEADS):
            ws = (wq_ref[h, :, 0:128], wq_ref[h, :, 128:256], wq_ref[h, :, 256:384],
                  wkv_ref[h, :, 0:128], wkv_ref[h, :, 128:256])
            qn, qr, kn, krp, v = _mla_head(*acts, *ws, *vp, cos_ref[...], sin_ref[...])
            q_out[h, :, 0:128] = qn.astype(BF16)
            q_out[h, :, 128:256] = qr.astype(BF16)
            k_out[h, :, 0:128] = kn.astype(BF16)
            k_out[h, :, 128:256] = krp.astype(BF16)
            v_out[h] = v.astype(BF16)

    return pl.pallas_call(
        body, name=name, grid=(s // tr,), in_specs=_mla_in_specs(tr),
        out_specs=[pl.BlockSpec((HEADS, tr, 256), lambda i: (0, i, 0)), pl.BlockSpec((HEADS, tr, 256), lambda i: (0, i, 0)),
                   pl.BlockSpec((HEADS, tr, 128), lambda i: (0, i, 0))],
        out_shape=[jax.ShapeDtypeStruct((HEADS, s, 256), BF16), jax.ShapeDtypeStruct((HEADS, s, 256), BF16),
                   jax.ShapeDtypeStruct((HEADS, s, 128), BF16)],
        compiler_params=_params("arbitrary"),
    )(proj, proj, proj, proj, wq, wkv, vec, cos2, sin2)


def mla_pre_bwd(proj, wq, wkv, vec, cos2, sin2, dq, dk, dv, name):
    s = proj.shape[0]
    tr = _pick(s, 256)

    def body(ql_ref, ckv_ref, kr_ref, krs_ref, wq_ref, wkv_ref, v_ref, cos_ref, sin_ref, dq_ref, dk_ref, dv_ref,
             dql_out, dckv_out, dkr_out, dkrs_out, dwq_out, dwkv_out, dvec_out):
        @pl.when(pl.program_id(0) == 0)
        def _():
            dwq_out[...] = jnp.zeros_like(dwq_out)
            dwkv_out[...] = jnp.zeros_like(dwkv_out)
            dvec_out[...] = jnp.zeros_like(dvec_out)

        acts = (ql_ref[...], ckv_ref[...], kr_ref[...], krs_ref[...])
        vp = _mla_vec_pieces(v_ref)
        cos2_, sin2_ = cos_ref[...], sin_ref[...]

        def head(h, carry):
            wq_h, wkv_h = wq_ref[h].astype(F32), wkv_ref[h].astype(F32)
            ws = (wq_h[:, 0:128], wq_h[:, 128:256], wq_h[:, 256:384], wkv_h[:, 0:128], wkv_h[:, 128:256])
            f = lambda *a: _mla_head(*a, cos2_, sin2_)
            _, vjp = jax.vjp(f, *acts, *ws, *vp)
            dq_h, dk_h = dq_ref[h], dk_ref[h]
            g = vjp((dq_h[:, 0:128], dq_h[:, 128:256], dk_h[:, 0:128], dk_h[:, 128:256], dv_ref[h]))
            dwq_out[h, :, 0:128] += g[4]
            dwq_out[h, :, 128:256] += g[5]
            dwq_out[h, :, 256:384] += g[6]
            dwkv_out[h, :, 0:128] += g[7]
            dwkv_out[h, :, 128:256] += g[8]
            dvec_out[0:1, 0:384] += g[9]
            dvec_out[1:2, 0:256] += g[10]
            dvec_out[2:3, 0:128] += g[11]
            dvec_out[2:3, 128:256] += g[12]
            dvec_out[2:3, 256:384] += g[13]
            dvec_out[3:4, 0:128] += g[14]
            dvec_out[3:4, 128:256] += g[15]
            dvec_out[3:4, 256:384] += g[16]
            return tuple(c + gg for c, gg in zip(carry, g[:4]))

        tot = lax.fori_loop(0, HEADS, head, tuple(jnp.zeros_like(a) for a in acts))
        dql_out[...] = tot[0].astype(BF16)
        dckv_out[...] = tot[1].astype(BF16)
        dkr_out[...] = tot[2].astype(BF16)
        dkrs_out[...] = tot[3].astype(BF16)

    hb = lambda w: pl.BlockSpec((HEADS, tr, w), lambda i: (0, i, 0))
    return pl.pallas_call(
        body, name=name, grid=(s // tr,), in_specs=_mla_in_specs(tr) + [hb(256), hb(256), hb(128)],
        out_specs=[pl.BlockSpec((tr, 384), lambda i: (i, 0)), pl.BlockSpec((tr, 256), lambda i: (i, 0)),
                   pl.BlockSpec((tr, 128), lambda i: (i, 0)), pl.BlockSpec((tr, 128), lambda i: (i, 0)),
                   pl.BlockSpec((HEADS, 384, 384), lambda i: (0, 0, 0)), pl.BlockSpec((HEADS, 256, 256), lambda i: (0, 0, 0)),
                   pl.BlockSpec((8, 512), lambda i: (0, 0))],
        out_shape=[jax.ShapeDtypeStruct((s, 384), BF16), jax.ShapeDtypeStruct((s, 256), BF16),
                   jax.ShapeDtypeStruct((s, 128), BF16), jax.ShapeDtypeStruct((s, 128), BF16),
                   jax.ShapeDtypeStruct((HEADS, 384, 384), F32), jax.ShapeDtypeStruct((HEADS, 256, 256), F32),
                   jax.ShapeDtypeStruct((8, 512), F32)],
        compiler_params=_params("arbitrary"),
    )(proj, proj, proj, proj, wq, wkv, vec, cos2, sin2, dq, dk, dv)


def _att_probs(q, kk, i, tq):
    sc = _raw(q, kk, _NT) * ATT_SCALE
    rows = lax.broadcasted_iota(jnp.int32, sc.shape, 0) + i * tq
    cols = lax.broadcasted_iota(jnp.int32, sc.shape, 1)
    sc = jnp.where(cols <= rows, sc, -jnp.inf)
    e = jnp.exp(sc - jnp.max(sc, axis=-1, keepdims=True))
    return e / jnp.sum(e, axis=-1, keepdims=True)


def mla_attn_fwd(q, k, v, name):
    s = q.shape[1]
    tq = _pick(s, 256)

    def body(q_ref, k_ref, v_ref, o_ref):
        for i in range(s // tq):
            n = (i + 1) * tq
            p = _att_probs(q_ref[i * tq:n, :], k_ref[0:n, :], i, tq)
            o_ref[i * tq:n, :] = _raw(p, v_ref[0:n, :], _NN)

    hs = lambda w: pl.BlockSpec((None, s, w), lambda h: (h, 0, 0))
    return pl.pallas_call(
        body, name=name, grid=(HEADS,), in_specs=[hs(256), hs(256), hs(128)],
        out_specs=pl.BlockSpec((s, 128), lambda h: (0, h)),
        out_shape=jax.ShapeDtypeStruct((s, HEADS * 128), F32),
        compiler_params=_params("arbitrary"),
    )(q, k, v)


def mla_attn_bwd(q, k, v, do, name):
    s = q.shape[1]
    tq = _pick(s, 256)

    def body(q_ref, k_ref, v_ref, do_ref, dq_ref, dk_ref, dv_ref):
        dk_ref[...] = jnp.zeros_like(dk_ref)
        dv_ref[...] = jnp.zeros_like(dv_ref)
        for i in range(s // tq):
            n = (i + 1) * tq
            qq, kk, vv = q_ref[i * tq:n, :], k_ref[0:n, :], v_ref[0:n, :]
            p = _att_probs(qq, kk, i, tq)
            o = _raw(p, vv, _NN)
            dout = do_ref[i * tq:n, :]
            delta = jnp.sum(dout * o, axis=-1, keepdims=True)
            dp = _raw(dout, vv, _NT)
            ds = p * (dp - delta) * ATT_SCALE
            dq_ref[i * tq:n, :] = _raw(ds, kk, _NN)
            dk_ref[0:n, :] += _raw(ds, qq, _TN)
            dv_ref[0:n, :] += _raw(p, dout, _TN)

    hs = lambda w: pl.BlockSpec((None, s, w), lambda h: (h, 0, 0))
    return pl.pallas_call(
        body, name=name, grid=(HEADS,),
        in_specs=[hs(256), hs(256), hs(128), pl.BlockSpec((s, 128), lambda h: (0, h))],
        out_specs=[hs(256), hs(256), hs(128)],
        out_shape=[jax.ShapeDtypeStruct((HEADS, s, 256), F32), jax.ShapeDtypeStruct((HEADS, s, 256), F32),
                   jax.ShapeDtypeStruct((HEADS, s, 128), F32)],
        compiler_params=_params("arbitrary"),
    )(q, k, v, do)


def _pool_windows(u, pad, s, g):
    pad[0:16, :] = jnp.zeros((16, 128), F32)
    cur, sel = u, None
    for j, k in enumerate((1, 2, 4, 8)):
        pad[16:16 + s, :] = cur
        cur = cur + pad[16 - k:16 - k + s, :]
        sel = cur if sel is None else jnp.where(g == j, cur, sel)
    return sel


def _pool_count(s, g):
    t = lax.broadcasted_iota(jnp.int32, (s, 1), 0)
    return jnp.minimum(t + 1, 2 << g).astype(F32)


def pool_fwd(proj, pw, ps, name):
    s = proj.shape[0]

    def body(u_ref, w_ref, s_ref, o_ref, pad):
        g = pl.program_id(0)
        u = u_ref[...]
        pooled = _pool_windows(u, pad, s, g) / _pool_count(s, g) - u
        o_ref[...] = _raw(pooled, w_ref[...], _NN) * s_ref[...]

    return pl.pallas_call(
        body, name=name, grid=(4,),
        in_specs=[pl.BlockSpec((s, 128), lambda g: (0, O_PU // 128 + g)), pl.BlockSpec((None, 128, 128), lambda g: (g, 0, 0)),
                  pl.BlockSpec((1, 128), lambda g: (0, g))],
        out_specs=pl.BlockSpec((s, 128), lambda g: (0, g)),
        out_shape=jax.ShapeDtypeStruct((s, 512), F32),
        scratch_shapes=[pltpu.VMEM((s + 16, 128), F32)],
        compiler_params=_params("arbitrary"),
    )(proj, pw, ps)


def pool_bwd(proj, pw, ps, do, name):
    s = proj.shape[0]

    def body(u_ref, w_ref, s_ref, do_ref, du_ref, dw_ref, ds_ref, pad):
        g = pl.program_id(0)
        u, w, dout = u_ref[...], w_ref[...], do_ref[...]
        cnt = _pool_count(s, g)
        pooled = _pool_windows(u, pad, s, g) / cnt - u
        mixed = _raw(pooled, w, _NN)
        ds_ref[...] = jnp.sum(dout * mixed, axis=0, keepdims=True)
        dmixed = dout * s_ref[...]
        dw_ref[...] = _raw(pooled, dmixed, _TN)
        dpooled = _raw(dmixed, w, _NT)
        dsel = dpooled / cnt
        pad[s:s + 16, :] = jnp.zeros((16, 128), F32)
        cur = jnp.where(g == 3, dsel, 0.0)
        for j, k in ((2, 8), (1, 4), (0, 2)):
            pad[0:s, :] = cur
            cur = cur + pad[k:k + s, :] + jnp.where(g == j, dsel, 0.0)
        pad[0:s, :] = cur
        cur = cur + pad[1:1 + s, :]
        du_ref[...] = (cur - dpooled).astype(BF16)

    return pl.pallas_call(
        body, name=name, grid=(4,),
        in_specs=[pl.BlockSpec((s, 128), lambda g: (0, O_PU // 128 + g)), pl.BlockSpec((None, 128, 128), lambda g: (g, 0, 0)),
                  pl.BlockSpec((1, 128), lambda g: (0, g)), pl.BlockSpec((s, 128), lambda g: (0, g))],
        out_specs=[pl.BlockSpec((s, 128), lambda g: (0, g)), pl.BlockSpec((None, 128, 128), lambda g: (g, 0, 0)),
                   pl.BlockSpec((1, 128), lambda g: (0, g))],
        out_shape=[jax.ShapeDtypeStruct((s, 512), BF16), jax.ShapeDtypeStruct((4, 128, 128), F32),
                   jax.ShapeDtypeStruct((1, 512), F32)],
        scratch_shapes=[pltpu.VMEM((s + 16, 128), F32)],
        compiler_params=_params("arbitrary"),
    )(proj, pw, ps, do)


def _xbc_col(i):
    return jnp.where(i < 2, O_XS // 512 + i, O_BC // 512)


def conv_fwd(proj, cw, cb, name):
    s = proj.shape[0]

    def body(x_ref, w_ref, b_ref, o_ref, pad):
        pad[0:8, :] = jnp.zeros((8, 512), F32)
        pad[8:8 + s, :] = x_ref[...]
        y = b_ref[...] + sum(w_ref[k:k + 1, :] * pad[5 + k:5 + k + s, :] for k in range(4))
        o_ref[...] = y * _sig(y)

    return pl.pallas_call(
        body, name=name, grid=(3,),
        in_specs=[pl.BlockSpec((s, 512), lambda i: (0, _xbc_col(i))), pl.BlockSpec((4, 512), lambda i: (0, i)),
                  pl.BlockSpec((1, 512), lambda i: (0, i))],
        out_specs=pl.BlockSpec((s, 512), lambda i: (0, i)),
        out_shape=jax.ShapeDtypeStruct((s, 1536), F32),
        scratch_shapes=[pltpu.VMEM((s + 8, 512), F32)],
        compiler_params=_params("arbitrary"),
    )(proj, cw, cb)


def conv_bwd(proj, cw, cb, dact, name):
    s = proj.shape[0]

    def body(x_ref, w_ref, b_ref, da_ref, dx_ref, dw_ref, db_ref, pad, pad2):
        pad[0:8, :] = jnp.zeros((8, 512), F32)
        pad[8:8 + s, :] = x_ref[...]
        y = b_ref[...] + sum(w_ref[k:k + 1, :] * pad[5 + k:5 + k + s, :] for k in range(4))
        sg = _sig(y)
        dy = da_ref[...] * (sg * (1.0 + y * (1.0 - sg)))
        db_ref[...] = jnp.sum(dy, axis=0, keepdims=True)
        for k in range(4):
            dw_ref[k:k + 1, :] = jnp.sum(dy * pad[5 + k:5 + k + s, :], axis=0, keepdims=True)
        pad2[s:s + 8, :] = jnp.zeros((8, 512), F32)
        pad2[0:s, :] = dy
        dx_ref[...] = sum(w_ref[k:k + 1, :] * pad2[3 - k:3 - k + s, :] for k in range(4)).astype(BF16)

    return pl.pallas_call(
        body, name=name, grid=(3,),
        in_specs=[pl.BlockSpec((s, 512), lambda i: (0, _xbc_col(i))), pl.BlockSpec((4, 512), lambda i: (0, i)),
                  pl.BlockSpec((1, 512), lambda i: (0, i)), pl.BlockSpec((s, 512), lambda i: (0, i))],
        out_specs=[pl.BlockSpec((s, 512), lambda i: (0, i)), pl.BlockSpec((4, 512), lambda i: (0, i)),
                   pl.BlockSpec((1, 512), lambda i: (0, i))],
        out_shape=[jax.ShapeDtypeStruct((s, 1536), BF16), jax.ShapeDtypeStruct((4, 1536), F32),
                   jax.ShapeDtypeStruct((1, 1536), F32)],
        scratch_shapes=[pltpu.VMEM((s + 8, 512), F32), pltpu.VMEM((s + 8, 512), F32)],
        compiler_params=_params("arbitrary"),
    )(proj, cw, cb, dact)


def _ssd_chunk(xt, dtr, dtc, bm, cm, hprev, alog, dbias, dskip):
    ln = 128
    a = -jnp.exp(alog)
    dt_r = softplus(dtr + dbias)
    da_r = dt_r * a
    da_c = softplus(dtc + dbias) * a
    li = lax.broadcasted_iota(jnp.int32, (1, ln, ln), 1)
    si = lax.broadcasted_iota(jnp.int32, (1, ln, ln), 2)
    causal = si <= li
    acs_c = jnp.sum(jnp.where(causal, da_r, 0.0), axis=2, keepdims=True)
    acs_r = jnp.sum(jnp.where(li <= si, da_c, 0.0), axis=1, keepdims=True)
    acs_last = jnp.sum(da_r, axis=2, keepdims=True)
    decay = jnp.exp(jnp.where(causal, acs_c - acs_r, -jnp.inf))
    m = mm_nt(cm, bm)[None] * decay
    xdt = xt * dt_r
    y_diag = bmm_nt(xdt, m)
    bb = jnp.broadcast_to(bm[None], (8, ln, ln))
    cc = jnp.broadcast_to(cm[None], (8, ln, ln))
    states = bmm_nn(xdt * jnp.exp(acs_last - acs_r), bb)
    y_off = bmm_nt(hprev, cc) * jnp.exp(acs_r)
    hnew = hprev * jnp.exp(acs_last) + states
    return y_diag + y_off + xt * dskip, hnew


def _ssd_specs(nc, rev):
    cix = (lambda c: nc - 1 - c) if rev else (lambda c: c)
    hv = pl.BlockSpec((8, 1, 1), lambda g, c: (g, 0, 0))
    return [pl.BlockSpec((8, 64, 128), lambda g, c: (g, 0, cix(c))), pl.BlockSpec((8, 1, 128), lambda g, c: (g, 0, cix(c))),
            pl.BlockSpec((8, 128, 1), lambda g, c: (g, cix(c), 0)), pl.BlockSpec((128, 128), lambda g, c: (cix(c), 8 + g)),
            pl.BlockSpec((128, 128), lambda g, c: (cix(c), 10 + g))], hv, cix


def ssd_fwd(xt, dtr, dtc, xbc, alog, dbias, dskip, name):
    s = xt.shape[2]
    nc = s // 128
    specs, hv, _ = _ssd_specs(nc, False)

    def body(x_ref, dr_ref, dc_ref, b_ref, c_ref, al_ref, db_ref, dk_ref, y_ref, hs_ref, h_scr):
        @pl.when(pl.program_id(1) == 0)
        def _():
            h_scr[...] = jnp.zeros_like(h_scr)
        hp = h_scr[...]
        hs_ref[...] = hp
        y, hn = _ssd_chunk(x_ref[...], dr_ref[...], dc_ref[...], b_ref[...], c_ref[...], hp,
                           al_ref[...], db_ref[...], dk_ref[...])
        y_ref[...] = y
        h_scr[...] = hn

    return pl.pallas_call(
        body, name=name, grid=(2, nc), in_specs=specs + [hv, hv, hv],
        out_specs=[pl.BlockSpec((8, 64, 128), lambda g, c: (g, 0, c)),
                   pl.BlockSpec((None, None, 8, 64, 128), lambda g, c: (g, c, 0, 0, 0))],
        out_shape=[jax.ShapeDtypeStruct((16, 64, s), F32), jax.ShapeDtypeStruct((2, nc, 8, 64, 128), F32)],
        scratch_shapes=[pltpu.VMEM((8, 64, 128), F32)],
        compiler_params=_params("arbitrary", "arbitrary"),
    )(xt, dtr, dtc, xbc, xbc, alog, dbias, dskip)


def ssd_bwd(xt, dtr, dtc, xbc, alog, dbias, dskip, hs, dyt, name):
    s = xt.shape[2]
    nc = s // 128
    specs, hv, cix = _ssd_specs(nc, True)

    def body(x_ref, dr_ref, dc_ref, b_ref, c_ref, al_ref, db_ref, dk_ref, hs_ref, dy_ref,
             dx_out, ddr_out, ddc_out, dbm_out, dcm_out, dal_out, ddb_out, ddk_out, dh_scr):
        @pl.when(pl.program_id(1) == 0)
        def _():
            dh_scr[...] = jnp.zeros_like(dh_scr)
            dal_out[...] = jnp.zeros_like(dal_out)
            ddb_out[...] = jnp.zeros_like(ddb_out)
            ddk_out[...] = jnp.zeros_like(ddk_out)
        _, vjp = jax.vjp(_ssd_chunk, x_ref[...], dr_ref[...], dc_ref[...], b_ref[...], c_ref[...], hs_ref[...],
                         al_ref[...], db_ref[...], dk_ref[...])
        g = vjp((dy_ref[...], dh_scr[...]))
        dx_out[...] = g[0]
        ddr_out[...] = g[1]
        ddc_out[...] = g[2]
        dbm_out[...] = g[3]
        dcm_out[...] = g[4]
        dh_scr[...] = g[5]
        dal_out[...] += g[6]
        ddb_out[...] += g[7]
        ddk_out[...] += g[8]

    return pl.pallas_call(
        body, name=name, grid=(2, nc),
        in_specs=specs + [hv, hv, hv, pl.BlockSpec((None, None, 8, 64, 128), lambda g, c: (g, cix(c), 0, 0, 0)),
                          pl.BlockSpec((8, 64, 128), lambda g, c: (g, 0, cix(c)))],
        out_specs=[pl.BlockSpec((8, 64, 128), lambda g, c: (g, 0, cix(c))), pl.BlockSpec((8, 1, 128), lambda g, c: (g, 0, cix(c))),
                   pl.BlockSpec((8, 128, 1), lambda g, c: (g, cix(c), 0)), pl.BlockSpec((128, 128), lambda g, c: (cix(c), g)),
                   pl.BlockSpec((128, 128), lambda g, c: (cix(c), g)), hv, hv, hv],
        out_shape=[jax.ShapeDtypeStruct((16, 64, s), F32), jax.ShapeDtypeStruct((16, 1, s), F32),
                   jax.ShapeDtypeStruct((16, s, 1), F32), jax.ShapeDtypeStruct((s, 256), F32),
                   jax.ShapeDtypeStruct((s, 256), F32)] + [jax.ShapeDtypeStruct((16, 1, 1), F32)] * 3,
        scratch_shapes=[pltpu.VMEM((8, 64, 128), F32)],
        compiler_params=_params("arbitrary", "arbitrary"),
    )(xt, dtr, dtc, xbc, xbc, alog, dbias, dskip, hs, dyt)


def _merge(oa, ob, y, z, gla, glb, glc, x, g1, nw, ea, eb, ec, eo, wba, wbb, wbc, wout):
    gated = y * (z * _sig(z))
    sq = gated * gated
    left = lax.broadcasted_iota(jnp.int32, (1, D), 1) < 512
    ms0 = jnp.sum(jnp.where(left, sq, 0.0), axis=-1, keepdims=True) / 512.0
    ms1 = jnp.sum(jnp.where(left, 0.0, sq), axis=-1, keepdims=True) / 512.0
    oc = gated * jnp.where(left, lax.rsqrt(ms0 + EPS), lax.rsqrt(ms1 + EPS)) * nw
    ya, yb, yc = mm_nc(oa, wba) + ea, mm_nc(ob, wbb) + eb, mm_nc(oc, wbc) + ec
    merged = _sig(gla) * ya + _sig(glb) * yb + _sig(glc) * yc
    x1 = x + g1 * (mm_nc(merged, wout) + eo)
    return x1, (oc, merged)


def _merge_specs(tr):
    row = lambda w: pl.BlockSpec((tr, w), lambda i: (i, 0))
    acts = [row(D), row(512), row(D), pl.BlockSpec((tr, D), lambda i: (i, O_Z // D)),
            pl.BlockSpec((tr, 3 * D), lambda i: (i, 0)), row(D), pl.BlockSpec((8, D), lambda i: (0, 0))]
    cst = lambda r: pl.BlockSpec((r, D), lambda i: (0, 0), **CONST)
    return acts, [cst(D), cst(512), cst(D), cst(D)], row


def merge_fwd(oa, ob, y, proj, x, mvec, wba, wbb, wbc, wout, name):
    s = x.shape[0]
    tr = _pick(s, 256)
    acts, wts, row = _merge_specs(tr)

    def body(oa_ref, ob_ref, y_ref, z_ref, gl_ref, x_ref, mv_ref, wba_ref, wbb_ref, wbc_ref, wout_ref, o_ref):
        zero = jnp.zeros((1, D), F32)
        x1, _ = _merge(oa_ref[...], ob_ref[...], y_ref[...], z_ref[...], gl_ref[:, 0:D], gl_ref[:, D:2 * D],
                       gl_ref[:, 2 * D:3 * D], x_ref[...], mv_ref[0:1, :], mv_ref[1:2, :], zero, zero, zero, zero,
                       wba_ref[...], wbb_ref[...], wbc_ref[...], wout_ref[...])
        o_ref[...] = x1

    return pl.pallas_call(
        body, name=name, grid=(s // tr,), in_specs=acts + wts, out_specs=row(D),
        out_shape=jax.ShapeDtypeStruct((s, D), F32), compiler_params=_params("arbitrary"),
    )(oa, ob, y, proj, proj, x, mvec, wba, wbb, wbc, wout)


def merge_bwd(oa, ob, y, proj, x, mvec, wba, wbb, wbc, wout, dx1, name):
    s = x.shape[0]
    tr = _pick(s, 128)
    acts, wts, row = _merge_specs(tr)

    def body(oa_ref, ob_ref, y_ref, z_ref, gl_ref, x_ref, mv_ref, wba_ref, wbb_ref, wbc_ref, wout_ref, dx1_ref,
             doa_o, dob_o, dy_o, dz_o, dgl_o, dx_o, dmv_o, dya_o, dyb_o, dyc_o, dpre_o, oc_o, mg_o):
        zero = jnp.zeros((tr, D), F32)
        wts_ = (wba_ref[...], wbb_ref[...], wbc_ref[...], wout_ref[...])
        f = lambda *a: _merge(*a, *wts_)
        _, vjp, (oc, merged) = jax.vjp(
            f, oa_ref[...], ob_ref[...], y_ref[...], z_ref[...], gl_ref[:, 0:D], gl_ref[:, D:2 * D],
            gl_ref[:, 2 * D:3 * D], x_ref[...], mv_ref[0:1, :], mv_ref[1:2, :], zero, zero, zero, zero, has_aux=True)
        g = vjp(dx1_ref[...])
        doa_o[...] = g[0]
        dob_o[...] = g[1]
        dy_o[...] = g[2]
        dz_o[...] = g[3].astype(BF16)
        dgl_o[:, 0:D] = g[4].astype(BF16)
        dgl_o[:, D:2 * D] = g[5].astype(BF16)
        dgl_o[:, 2 * D:3 * D] = g[6].astype(BF16)
        dx_o[...] = g[7]

        @pl.when(pl.program_id(0) == 0)
        def _():
            dmv_o[...] = jnp.zeros_like(dmv_o)

        dmv_o[0:1, :] += g[8]
        dmv_o[1:2, :] += g[9]
        dya_o[...] = g[10].astype(BF16)
        dyb_o[...] = g[11].astype(BF16)
        dyc_o[...] = g[12].astype(BF16)
        dpre_o[...] = g[13].astype(BF16)
        oc_o[...] = oc.astype(BF16)
        mg_o[...] = merged.astype(BF16)

    sd = lambda w, dt: jax.ShapeDtypeStruct((s, w), dt)
    return pl.pallas_call(
        body, name=name, grid=(s // tr,), in_specs=acts + wts + [row(D)],
        out_specs=[row(D), row(512), row(D), row(D), row(3 * D), row(D), pl.BlockSpec((8, D), lambda i: (0, 0))] + [row(D)] * 6,
        out_shape=[sd(D, F32), sd(512, F32), sd(D, F32), sd(D, BF16), sd(3 * D, BF16), sd(D, F32),
                   jax.ShapeDtypeStruct((8, D), F32)] + [sd(D, BF16)] * 6,
        compiler_params=_params("arbitrary"),
    )(oa, ob, y, proj, proj, x, mvec, wba, wbb, wbc, wout, dx1)


def _conv3(u_scr, w_ref, first, rows, lanes):
    return sum(w_ref[k:k + 1, :] * u_scr[first + k:first + k + rows, lanes] for k in range(3))


def _ffn_tile_specs(tf, tile):
    def at(rows, off):
        return pl.BlockSpec((rows, tf), lambda *g: (0, off + tile(*g)))
    return [at(D, 0), at(D, FFN_NT), at(3, 0), at(3, FFN_NT), at(1, 0), at(1, FFN_NT)]


def ffn_fwd(x1, fvec, wup, cw, cb, wdn, name):
    s = x1.shape[0]
    tr, tf = _pick(s, 512), FFN_TILE
    lg, lv = slice(0, tf), slice(tf, 2 * tf)

    def body(x_ref, v_ref, wg_ref, wv_ref, cwg_ref, cwv_ref, cbg_ref, cbv_ref, wd_ref, x2_ref, h_ref, pre_ref,
             h_scr, u_scr, acc):
        i, t = pl.program_id(0), pl.program_id(1)

        @pl.when(t == 0)
        def _():
            @pl.when(i == 0)
            def _():
                h_scr[0:16, :] = jnp.zeros((16, D), BF16)

            @pl.when(i > 0)
            def _():
                h_scr[0:16, :] = h_scr[tr:tr + 16, :]

            h = (_rms(x_ref[...], v_ref[0:1, :], D) * (1.0 + v_ref[2:3, :]) + v_ref[1:2, :]).astype(BF16)
            h_scr[16:16 + tr, :] = h
            h_ref[...] = h
            acc[...] = jnp.zeros_like(acc)

        u_scr[:, lg] = jnp.dot(h_scr[...], wg_ref[...], preferred_element_type=F32)
        u_scr[:, lv] = jnp.dot(h_scr[...], wv_ref[...], preferred_element_type=F32)
        cg = _conv3(u_scr, cwg_ref, 14, tr, lg) + cbg_ref[...]
        cval = _conv3(u_scr, cwv_ref, 14, tr, lv) + cbv_ref[...]
        acc[...] += _raw(cg * _sig(cg) * cval, wd_ref[...], _NN)

        @pl.when(t == FFN_NT - 1)
        def _():
            pre_ref[...] = acc[...]
            x2_ref[...] = x_ref[...] + v_ref[3:4, :] * acc[...]

    row = pl.BlockSpec((tr, D), lambda i, t: (i, 0))
    return pl.pallas_call(
        body, name=name, grid=(s // tr, FFN_NT),
        in_specs=[row, pl.BlockSpec((8, D), lambda i, t: (0, 0))] + _ffn_tile_specs(tf, lambda i, t: t)
                 + [pl.BlockSpec((tf, D), lambda i, t: (t, 0))],
        out_specs=[row, row, row],
        out_shape=[jax.ShapeDtypeStruct((s, D), F32), jax.ShapeDtypeStruct((s, D), BF16), jax.ShapeDtypeStruct((s, D), F32)],
        scratch_shapes=[pltpu.VMEM((tr + 16, D), BF16), pltpu.VMEM((tr + 16, 2 * tf), F32), pltpu.VMEM((tr, D), F32)],
        compiler_params=_params("arbitrary", "arbitrary"),
    )(x1, fvec, wup, wup, cw, cw, cb, cb, wdn)


def ffn_bwd(h2, dx2, fvec, wup, cw, cb, wdn, name):
    s = h2.shape[0]
    tr, tf = _pick(s, 512), FFN_TILE
    ni, nb = s // tr, s // 16
    lg, lv = slice(0, tf), slice(tf, 2 * tf)

    def body(hp_ref, hm_ref, hn_ref, dm_ref, dn_ref, v_ref, wg_ref, wv_ref, cwg_ref, cwv_ref, cbg_ref, cbv_ref, wd_ref,
             dup_ref, act_ref, dcw_ref, u_scr, dc_scr):
        i = pl.program_id(1)
        hfull = jnp.concatenate([jnp.where(i > 0, hp_ref[...], jnp.zeros((16, D), BF16)), hm_ref[...],
                                 jnp.where(i < ni - 1, hn_ref[...], jnp.zeros((16, D), BF16))], axis=0)
        u_scr[:, lg] = jnp.dot(hfull, wg_ref[...], preferred_element_type=F32)
        u_scr[:, lv] = jnp.dot(hfull, wv_ref[...], preferred_element_type=F32)
        cg = _conv3(u_scr, cwg_ref, 14, tr + 16, lg) + cbg_ref[...]
        cval = _conv3(u_scr, cwv_ref, 14, tr + 16, lv) + cbv_ref[...]
        g2 = v_ref[3:4, :]
        dpre = jnp.concatenate([dm_ref[...] * g2, jnp.where(i < ni - 1, dn_ref[...], 0.0) * g2], axis=0)
        dact = _raw(dpre, wd_ref[...], _NT)
        sg = _sig(cg)
        sl = cg * sg
        dc_scr[:, lg] = dact * cval * (sg * (1.0 + cg * (1.0 - sg)))
        dc_scr[:, lv] = dact * sl
        act_ref[...] = (sl * cval)[0:tr, :].astype(BF16)

        @pl.when(i == 0)
        def _():
            dcw_ref[...] = jnp.zeros_like(dcw_ref)

        for half, lanes, cw_ref in ((0, lg, cwg_ref), (1, lv, cwv_ref)):
            dup_ref[half] = sum(cw_ref[k:k + 1, :] * dc_scr[2 - k:2 - k + tr, lanes] for k in range(3)).astype(BF16)
            dcm = dc_scr[0:tr, lanes]
            for k in range(3):
                dcw_ref[half, k:k + 1, :] += jnp.sum(dcm * u_scr[14 + k:14 + k + tr, lanes], axis=0, keepdims=True)
            dcw_ref[half, 3:4, :] += jnp.sum(dcm, axis=0, keepdims=True)

    r16 = tr // 16
    prev = lambda t, i: (jnp.maximum(i * r16 - 1, 0), 0)
    nxt = lambda t, i: (jnp.minimum((i + 1) * r16, nb - 1), 0)
    main = lambda t, i: (i, 0)
    return pl.pallas_call(
        body, name=name, grid=(FFN_NT, ni),
        in_specs=[pl.BlockSpec((16, D), prev), pl.BlockSpec((tr, D), main), pl.BlockSpec((16, D), nxt),
                  pl.BlockSpec((tr, D), main), pl.BlockSpec((16, D), nxt), pl.BlockSpec((8, D), lambda t, i: (0, 0))]
                 + _ffn_tile_specs(tf, lambda t, i: t) + [pl.BlockSpec((tf, D), lambda t, i: (t, 0))],
        out_specs=[pl.BlockSpec((2, tr, tf), lambda t, i: (0, i, t)), pl.BlockSpec((tr, tf), lambda t, i: (i, t)),
                   pl.BlockSpec((2, 8, tf), lambda t, i: (0, 0, t))],
        out_shape=[jax.ShapeDtypeStruct((2, s, FFN), BF16), jax.ShapeDtypeStruct((s, FFN), BF16),
                   jax.ShapeDtypeStruct((2, 8, FFN), F32)],
        scratch_shapes=[pltpu.VMEM((tr + 32, 2 * tf), F32), pltpu.VMEM((tr + 16, 2 * tf), F32)],
        compiler_params=_params("arbitrary", "arbitrary"),
    )(h2, h2, h2, dx2, dx2, fvec, wup, wup, cw, cw, cb, cb, wdn)


def loss_head(y, target):
    s = y.shape[0]
    tr = _pick(s, 512)

    def body(y_ref, t_ref, dx_ref, l_ref):
        @pl.when(pl.program_id(0) == 0)
        def _():
            l_ref[...] = jnp.zeros_like(l_ref)
        err = y_ref[...] - t_ref[...]
        dx_ref[...] = err / float(D)
        l_ref[...] += 0.5 * jnp.sum(jnp.sum(err * err, axis=-1, keepdims=True) / float(D), axis=0, keepdims=True)

    row = pl.BlockSpec((tr, D), lambda i: (i, 0))
    return pl.pallas_call(
        body, name="loss_head", grid=(s // tr,), in_specs=[row, row],
        out_specs=[row, pl.BlockSpec((8, 128), lambda i: (0, 0))],
        out_shape=[jax.ShapeDtypeStruct((s, D), F32), jax.ShapeDtypeStruct((8, 128), F32)],
        compiler_params=_params("arbitrary"),
    )(y, target)


def adamw(parts, w, m, v, name, tok=None):
    nseg = len(parts)
    p, r, c = parts[0].shape
    tr = _pick(r, 256, 8)
    ni = r // tr
    tok = jnp.zeros((8, 128), F32) if tok is None else tok

    def body(*refs):
        p_refs = refs[:nseg]
        w_ref, m_ref, v_ref, _, g_out, d_out, m_out, v_out, g_scr = refs[nseg:]
        for q in range(nseg):
            @pl.when(pl.program_id(0) == q)
            def _(q=q):
                g = p_refs[q][0].astype(F32)
                for j in range(1, p):
                    g = g + p_refs[q][j].astype(F32)
                g_scr[...] = g
        g = g_scr[...]
        mn = B1 * m_ref[...] + (1.0 - B1) * g
        vn = B2 * v_ref[...] + (1.0 - B2) * (g * g)
        m_hat = mn / (1.0 - B1 ** STEP)
        v_hat = vn / (1.0 - B2 ** STEP)
        g_out[...] = g
        d_out[...] = -LR * (m_hat / (jnp.sqrt(v_hat) + ADAM_EPS) + WD * w_ref[...])
        m_out[...] = mn
        v_out[...] = vn

    row = pl.BlockSpec((tr, c), lambda l, i: (l * ni + i, 0))
    part = lambda q: pl.BlockSpec((p, tr, c), lambda l, i: (0, jnp.clip((l - q) * ni + i, 0, ni - 1), 0))
    return pl.pallas_call(
        body, name=name, grid=(nseg, ni),
        in_specs=[part(q) for q in range(nseg)] + [row, row, row, pl.BlockSpec((8, 128), lambda l, i: (0, 0))],
        out_specs=[row] * 4, out_shape=[jax.ShapeDtypeStruct((nseg * r, c), F32)] * 4,
        scratch_shapes=[pltpu.VMEM((tr, c), F32)],
        compiler_params=_params("arbitrary", "arbitrary"),
    )(*parts, w, m, v, tok)


def _padc(a, n):
    return jnp.pad(a, [(0, 0)] * (a.ndim - 1) + [(0, n - a.shape[-1])])


def _swap16(a):
    return jnp.concatenate([a[..., 16:32], a[..., 0:16]], axis=-1)


def _win_layout(w):
    kr = w[:, 640:672]
    return jnp.concatenate([w[:, 3760:6832], w[:, 2208:3232], w[:, 1184:2208], w[:, 672:1184], w[:, 3232:3744],
                            w[:, 384:640], _padc(kr, 128), _padc(_swap16(kr), 128), _padc(w[:, 3744:3760], 128),
                            jnp.zeros((w.shape[0], 128), w.dtype), w[:, 0:384]], axis=1)


def _win_unlayout(g):
    kr = g[:, O_KR:O_KR + 32] + _swap16(g[:, O_KRS:O_KRS + 32])
    return jnp.concatenate([g[:, O_QL:O_QL + 384], g[:, O_CKV:O_CKV + 256], kr, g[:, O_PU:O_PU + 512], g[:, O_Z:O_Z + D],
                            g[:, O_XS:O_XS + D], g[:, O_BC:O_BC + 512], g[:, O_DT:O_DT + 16], g[:, O_G:O_G + 3 * D]], axis=1)


def _wq_layout(w):
    w = w.reshape(384, HEADS, 96).transpose(1, 0, 2)
    rope = w[:, :, 64:96]
    return jnp.concatenate([_padc(w[:, :, 0:64], 128), _padc(rope, 128), _padc(_swap16(rope), 128)], axis=2)


def _wq_unlayout(g):
    rope = g[:, :, 128:160] + _swap16(g[:, :, 256:288])
    return jnp.concatenate([g[:, :, 0:64], rope], axis=2).transpose(1, 0, 2).reshape(384, HEADS * 96)


def _wkv_layout(w):
    w = w.reshape(256, HEADS, 128).transpose(1, 0, 2)
    return jnp.concatenate([_padc(w[:, :, 0:64], 128), _padc(w[:, :, 64:128], 128)], axis=2)


def _wkv_unlayout(g):
    return jnp.concatenate([g[:, :, 0:64], g[:, :, 128:192]], axis=2).transpose(1, 0, 2).reshape(256, HEADS * 128)


def _wba_layout(w):
    return jnp.pad(w.reshape(HEADS, 64, D), ((0, 0), (0, 64), (0, 0))).reshape(HEADS * 128, D)


def _rows8(rows, width):
    out = jnp.stack([_padc(r.astype(F32), width) for r in rows])
    return jnp.pad(out, ((0, 8 - out.shape[0]), (0, 0)))


def _mla_vec(qa, kva, qn, kn):
    def row(n):
        return jnp.concatenate([_padc(n[0:64], 128), _padc(n[64:96], 128), _padc(_swap16(n[64:96]), 128)])
    return _rows8([qa, kva, row(qn), row(kn)], 512)


def _mla_unvec(g):
    def un(r):
        return jnp.concatenate([r[0:64], r[128:160] + _swap16(r[256:288])])
    return g[0, 0:384], g[1, 0:256], un(g[2]), un(g[3])


SMALL = (("ada_b", (6 * D,)), ("norm1_w", (D,)), ("q_a_norm", (384,)), ("kv_a_norm", (256,)), ("q_norm", (96,)),
         ("k_norm", (96,)), ("pool_w", (4, 128, 128)), ("pool_scale", (512,)), ("ssd_conv_b", (1536,)),
         ("ssd_dt_bias", (16,)), ("ssd_a_log", (16,)), ("ssd_d", (16,)), ("ssd_norm_w", (D,)), ("norm2_w", (D,)),
         ("ffn_conv_b", (2 * FFN,)), ("ssd_conv_w", (4, 1536)), ("ffn_conv_w", (3, 2 * FFN)))
SMALL_REPL = SMALL[:15]
SMALL_ROWS = 208


def _pack(per_layer, names):
    flat = jnp.concatenate([per_layer[l][n].reshape(-1).astype(F32) for n, _ in names for l in range(LAYERS)])
    return jnp.pad(flat, (0, SMALL_ROWS * D - flat.shape[0])).reshape(SMALL_ROWS, D)


def _unpack(packed, names):
    flat, out, off = packed.reshape(-1), {}, 0
    for n, shp in names:
        size = LAYERS * math.prod(shp)
        out[n] = flat[off:off + size].reshape((LAYERS,) + shp)
        off += size
    return out


GROUP_A = ("w_in", "w_q_b", "w_kv_b")
GROUP_B = ("w_branch", "w_out", "ffn_up", "ffn_down")
BIG = GROUP_A + GROUP_B
COL_SHARDED = ("w_in", "w_q_b", "w_kv_b", "ffn_up")


def _behind(arrs, tok):
    return [arrs[0] + tok[0, 0].astype(arrs[0].dtype)] + list(arrs[1:])


def _gathered_full(g, name):
    if name in COL_SHARDED:
        return g.transpose(1, 0, 2).reshape(g.shape[1], NDEV * g.shape[2])
    return g.reshape(NDEV * g.shape[1], g.shape[2])


def _to_shards(full, name):
    if name in COL_SHARDED:
        r, c = full.shape
        return full.reshape(r, NDEV, c // NDEV).transpose(1, 0, 2).astype(BF16)
    r, c = full.shape
    return full.reshape(NDEV, r // NDEV, c).astype(BF16)


def _fwd_a(x, lw, mod, cos2, sin2, l, tok):
    sh1, sc1, g1, sh2, sc2, g2 = [mod[j * D:(j + 1) * D] for j in range(6)]
    vec1 = _rows8([lw["norm1_w"], sh1, sc1], D) + tok[0, 0]
    proj, h1, dt_cols = norm_proj_fwd(x, vec1, lw["win"], f"inproj_fwd{l}")
    q, k, v = mla_pre_fwd(proj, lw["wq"], lw["wkv"], lw["mla_vec"], cos2, sin2, f"mla_pre_fwd{l}")
    oa = mla_attn_fwd(q, k, v, f"mla_attn_fwd{l}")
    ob = pool_fwd(proj, lw["pool_w"], lw["pool_scale"].reshape(1, 512), f"pool_fwd{l}")
    xbc = conv_fwd(proj, lw["ssd_conv_w"], lw["ssd_conv_b"].reshape(1, 1536), f"conv_fwd{l}")
    s = x.shape[0]
    xt = xbc[:, 0:D].reshape(s, 16, 64).transpose(1, 2, 0)
    dt = dt_cols[:, 0:16].T
    dtr, dtc = dt[:, None, :], dt[:, :, None]
    hv = lambda a: a.reshape(16, 1, 1)
    yt, hs = ssd_fwd(xt, dtr, dtc, xbc, hv(lw["ssd_a_log"]), hv(lw["ssd_dt_bias"]), hv(lw["ssd_d"]), f"ssd_fwd{l}")
    y = yt.transpose(2, 0, 1).reshape(s, D)
    return dict(x=x, vec1=vec1, proj=proj, h1=h1, q=q, k=k, v=v, oa=oa, ob=ob, xbc=xbc, xt=xt, dtr=dtr, dtc=dtc,
                hs=hs, y=y, mvec=_rows8([g1, lw["ssd_norm_w"]], D), fvec=_rows8([lw["norm2_w"], sh2, sc2, g2], D))


def _fwd_b(sv, lw, l, tok):
    sv["mvec"] = sv["mvec"] + tok[0, 0]
    x1 = merge_fwd(sv["oa"], sv["ob"], sv["y"], sv["proj"], sv["x"], sv["mvec"], lw["wba"], lw["wbb"], lw["wbc"],
                   lw["wout"], f"merge_fwd{l}")
    x2, h2, pre = ffn_fwd(x1, sv["fvec"], lw["wup"], lw["ffn_conv_w"], lw["ffn_conv_b"].reshape(1, 2 * FFN), lw["wdn"],
                          f"ffn_fwd{l}")
    sv.update(x1=x1, h2=h2, pre=pre)
    return x2


def _bwd_b(dx2, lw, sv, l, tok):
    grads, small = {}, {}
    fvec = sv["fvec"] + tok[0, 0]
    dup, act, dcw = ffn_bwd(sv["h2"], dx2, fvec, lw["wup"], lw["ffn_conv_w"], lw["ffn_conv_b"].reshape(1, 2 * FFN),
                            lw["wdn"], f"ffn_bwd{l}")
    grads["ffn_down"] = tn_matmul(act, dx2, f"dw_down{l}", scale=fvec[3:4])
    grads["ffn_up"] = tn_matmul(sv["h2"], dup, f"dw_up{l}")
    dx1, dfvec = norm_proj_bwd(sv["x1"], fvec, dup, lw["wup"], dx2, sv["pre"], f"ffn_norm_bwd{l}")
    small["ffn_conv_w"] = jnp.concatenate([dcw[0, 0:3], dcw[1, 0:3]], axis=1)
    small["ffn_conv_b"] = jnp.concatenate([dcw[0, 3], dcw[1, 3]])
    small["norm2_w"] = dfvec[0]
    (doa, dob, dy, dz, dgl, dx, dmvec, dya, dyb, dyc, dpre, oc, merged) = merge_bwd(
        sv["oa"], sv["ob"], sv["y"], sv["proj"], sv["x"], sv["mvec"], lw["wba"], lw["wbb"], lw["wbc"], lw["wout"], dx1,
        f"merge_bwd{l}")
    dwba = tn_matmul(sv["oa"], dya, f"dw_ba{l}").reshape(HEADS, 128, D)[:, 0:64].reshape(512, D)
    grads["w_branch"] = jnp.concatenate([dwba, tn_matmul(sv["ob"], dyb, f"dw_bb{l}"), tn_matmul(oc, dyc, f"dw_bc{l}")])
    grads["w_out"] = tn_matmul(merged, dpre, f"dw_out{l}")
    small["ssd_norm_w"] = dmvec[1]
    small["dmod_b"] = (dmvec[0], dfvec[1], dfvec[2], dfvec[3])
    return dx, dict(doa=doa, dob=dob, dy=dy, dz=dz, dgl=dgl), grads, small


def _bwd_a(dx, cot, lw, sv, cos2, sin2, l, tok, small):
    s = dx.shape[0]
    grads = {}
    doa, dob, dz, dgl = cot["doa"], cot["dob"], cot["dz"], cot["dgl"]
    dyt = (cot["dy"] + tok[0, 0]).reshape(s, 16, 64).transpose(1, 2, 0)
    hv = lambda a: a.reshape(16, 1, 1)
    dxt, ddtr, ddtc, dbm, dcm, dal, ddb, ddk = ssd_bwd(
        sv["xt"], sv["dtr"], sv["dtc"], sv["xbc"], hv(lw["ssd_a_log"]), hv(lw["ssd_dt_bias"]), hv(lw["ssd_d"]), sv["hs"],
        dyt, f"ssd_bwd{l}")
    small["ssd_a_log"], small["ssd_dt_bias"], small["ssd_d"] = dal.reshape(16), ddb.reshape(16), ddk.reshape(16)
    dact = jnp.concatenate([dxt.transpose(2, 0, 1).reshape(s, D), dbm, dcm], axis=1)
    dxbc, dscw, dscb = conv_bwd(sv["proj"], lw["ssd_conv_w"], lw["ssd_conv_b"].reshape(1, 1536), dact, f"conv_bwd{l}")
    small["ssd_conv_w"], small["ssd_conv_b"] = dscw, dscb.reshape(1536)
    ddt = (ddtr[:, 0, :] + ddtc[:, :, 0]).T
    du, dpw, dps = pool_bwd(sv["proj"], lw["pool_w"], lw["pool_scale"].reshape(1, 512), dob, f"pool_bwd{l}")
    small["pool_w"], small["pool_scale"] = dpw, dps.reshape(512)
    dq, dk, dv = mla_attn_bwd(sv["q"], sv["k"], sv["v"], doa, f"mla_attn_bwd{l}")
    dql, dckv, dkr, dkrs, dwq, dwkv, dmv = mla_pre_bwd(sv["proj"], lw["wq"], lw["wkv"], lw["mla_vec"], cos2, sin2,
                                                       dq, dk, dv, f"mla_pre_bwd{l}")
    grads["w_q_b"], grads["w_kv_b"] = _wq_unlayout(dwq), _wkv_unlayout(dwkv)
    small["q_a_norm"], small["kv_a_norm"], small["q_norm"], small["k_norm"] = _mla_unvec(dmv)
    dproj = jnp.concatenate([dgl, dxbc[:, 0:D], dz, du, dxbc[:, D:1536], dckv, dkr, dkrs,
                             _padc(ddt, 128).astype(BF16), jnp.zeros((s, 128), BF16), dql], axis=1)
    grads["w_in"] = _win_unlayout(tn_matmul(sv["h1"], dproj, f"dw_in{l}"))
    dx0, dvec1 = norm_proj_bwd(sv["x"], sv["vec1"], dproj, lw["win"], dx, None, f"inproj_bwd{l}")
    small["norm1_w"] = dvec1[0]
    small["ada_b"] = jnp.concatenate([dvec1[1], dvec1[2], *small.pop("dmod_b")])
    return dx0, grads, small


def kernel(x, c, positions, ada_w, ada_b, norm1_w, w_in, q_a_norm, w_q_b, kv_a_norm, w_kv_b, q_norm, k_norm, pool_w, pool_scale, ssd_conv_w, ssd_conv_b, ssd_dt_bias, ssd_a_log, ssd_d, ssd_norm_w, w_branch, w_out, norm2_w, ffn_up, ffn_conv_w, ffn_conv_b, ffn_down, loss_target, m_ada_w, m_ada_b, m_norm1_w, m_w_in, m_q_a_norm, m_w_q_b, m_kv_a_norm, m_w_kv_b, m_q_norm, m_k_norm, m_pool_w, m_pool_scale, m_ssd_conv_w, m_ssd_conv_b, m_ssd_dt_bias, m_ssd_a_log, m_ssd_d, m_ssd_norm_w, m_w_branch, m_w_out, m_norm2_w, m_ffn_up, m_ffn_conv_w, m_ffn_conv_b, m_ffn_down, v_ada_w, v_ada_b, v_norm1_w, v_w_in, v_q_a_norm, v_w_q_b, v_kv_a_norm, v_w_kv_b, v_q_norm, v_k_norm, v_pool_w, v_pool_scale, v_ssd_conv_w, v_ssd_conv_b, v_ssd_dt_bias, v_ssd_a_log, v_ssd_d, v_ssd_norm_w, v_w_branch, v_w_out, v_norm2_w, v_ffn_up, v_ffn_conv_w, v_ffn_conv_b, v_ffn_down):
    p = dict(ada_w=ada_w, ada_b=ada_b, norm1_w=norm1_w, w_in=w_in, q_a_norm=q_a_norm, w_q_b=w_q_b, kv_a_norm=kv_a_norm,
             w_kv_b=w_kv_b, q_norm=q_norm, k_norm=k_norm, pool_w=pool_w, pool_scale=pool_scale, ssd_conv_w=ssd_conv_w,
             ssd_conv_b=ssd_conv_b, ssd_dt_bias=ssd_dt_bias, ssd_a_log=ssd_a_log, ssd_d=ssd_d, ssd_norm_w=ssd_norm_w,
             w_branch=w_branch, w_out=w_out, norm2_w=norm2_w, ffn_up=ffn_up, ffn_conv_w=ffn_conv_w, ffn_conv_b=ffn_conv_b,
             ffn_down=ffn_down)
    mom = dict(ada_w=m_ada_w, ada_b=m_ada_b, norm1_w=m_norm1_w, w_in=m_w_in, q_a_norm=m_q_a_norm, w_q_b=m_w_q_b,
               kv_a_norm=m_kv_a_norm, w_kv_b=m_w_kv_b, q_norm=m_q_norm, k_norm=m_k_norm, pool_w=m_pool_w,
               pool_scale=m_pool_scale, ssd_conv_w=m_ssd_conv_w, ssd_conv_b=m_ssd_conv_b, ssd_dt_bias=m_ssd_dt_bias,
               ssd_a_log=m_ssd_a_log, ssd_d=m_ssd_d, ssd_norm_w=m_ssd_norm_w, w_branch=m_w_branch, w_out=m_w_out,
               norm2_w=m_norm2_w, ffn_up=m_ffn_up, ffn_conv_w=m_ffn_conv_w, ffn_conv_b=m_ffn_conv_b, ffn_down=m_ffn_down)
    var = dict(ada_w=v_ada_w, ada_b=v_ada_b, norm1_w=v_norm1_w, w_in=v_w_in, q_a_norm=v_q_a_norm, w_q_b=v_w_q_b,
               kv_a_norm=v_kv_a_norm, w_kv_b=v_w_kv_b, q_norm=v_q_norm, k_norm=v_k_norm, pool_w=v_pool_w,
               pool_scale=v_pool_scale, ssd_conv_w=v_ssd_conv_w, ssd_conv_b=v_ssd_conv_b, ssd_dt_bias=v_ssd_dt_bias,
               ssd_a_log=v_ssd_a_log, ssd_d=v_ssd_d, ssd_norm_w=v_ssd_norm_w, w_branch=v_w_branch, w_out=v_w_out,
               norm2_w=v_norm2_w, ffn_up=v_ffn_up, ffn_conv_w=v_ffn_conv_w, ffn_conv_b=v_ffn_conv_b, ffn_down=v_ffn_down)
    names = list(p)
    me = 4 * lax.axis_index("x") + 2 * lax.axis_index("y") + lax.axis_index("c")
    xs, tgt = x[0], loss_target[0]
    s = xs.shape[0]

    inv_freq = ROPE_THETA ** (-jnp.arange(0, 32, 2, dtype=F32) / 32.0)
    ang = positions[0].astype(F32)[:, None] * inv_freq
    cos, sin = jnp.cos(ang), jnp.sin(ang)
    cos2 = _padc(jnp.concatenate([cos, cos], axis=1), 128)
    sin2 = _padc(jnp.concatenate([-sin, sin], axis=1), 128)

    conv_shards = jnp.concatenate([ssd_conv_w.reshape(-1), ffn_conv_w.reshape(-1)])
    (c_all, conv_all), _ = all_to_all([c, conv_shards], [True, True], "gather_c")
    modp, cact = ada_mod(jnp.pad(c_all.reshape(NDEV, D), ((0, 8), (0, 0))), ada_w)
    (mod_in,), tok = all_to_all([modp[:, 0:NDEV].transpose(1, 0, 2)], [False], "scatter_mod")
    mod = mod_in.transpose(1, 0, 2).reshape(LAYERS, 6 * D) + ada_b

    n1 = LAYERS * 4 * 192
    scw = conv_all[:, :n1].reshape(NDEV, LAYERS, 4, 192).transpose(1, 2, 0, 3).reshape(LAYERS, 4, 1536)
    fcw = conv_all[:, n1:].reshape(NDEV, LAYERS, 3, 704).transpose(1, 2, 0, 3).reshape(LAYERS, 3, 2 * FFN)

    def weights_a(gathered, l):
        full = {n: _gathered_full(g, n) for n, g in zip(GROUP_A, gathered)}
        lw = {n: p[n][l] for n in names}
        lw.update(win=_win_layout(full["w_in"]), wq=_wq_layout(full["w_q_b"]), wkv=_wkv_layout(full["w_kv_b"]),
                  ssd_conv_w=scw[l], ffn_conv_w=fcw[l],
                  mla_vec=_mla_vec(lw["q_a_norm"], lw["kv_a_norm"], lw["q_norm"], lw["k_norm"]))
        return lw

    def weights_b(gathered):
        full = {n: _gathered_full(g, n) for n, g in zip(GROUP_B, gathered)}
        wb = full["w_branch"]
        return dict(wba=_wba_layout(wb[0:512]), wbb=wb[512:1024], wbc=wb[1024:2048], wout=full["w_out"],
                    wup=full["ffn_up"], wdn=full["ffn_down"])

    shards = lambda group, l: [p[n][l].astype(BF16) for n in group]
    bc = lambda group: [True] * len(group)
    lws, saved = [None] * LAYERS, [None] * LAYERS
    st, tok = exchange_start(_behind(shards(GROUP_A, 0), tok), bc(GROUP_A), "gather_a0_start")
    got, tok = exchange_wait(st, tok, "gather_a0_wait")
    h = xs
    for l in range(LAYERS):
        st, tok = exchange_start(_behind(shards(GROUP_B, l), tok), bc(GROUP_B), f"gather_b{l}_start")
        lws[l] = weights_a(got, l)
        saved[l] = _fwd_a(h, lws[l], mod[l], cos2, sin2, l, tok)
        got, tok = exchange_wait(st, saved[l]["y"], f"gather_b{l}_wait")
        lws[l].update(weights_b(got))
        if l + 1 < LAYERS:
            st, tok = exchange_start(_behind(shards(GROUP_A, l + 1), tok), bc(GROUP_A), f"gather_a{l + 1}_start")
        h = _fwd_b(saved[l], lws[l], l, tok)
        if l + 1 < LAYERS:
            got, tok = exchange_wait(st, h, f"gather_a{l + 1}_wait")
    dx, lpart = loss_head(h, tgt)
    loss = lax.psum(lpart[0, 0], ("x", "y", "c"))
    tok = tok + loss * 0.0

    grads, small, parts = [None] * LAYERS, [None] * LAYERS, {}
    to_shards = lambda g, group: [_to_shards(g[n], n) for n in group]
    nb = lambda group: [False] * len(group)
    st = None
    for l in reversed(range(LAYERS)):
        dx, cot, gb, small[l] = _bwd_b(dx, lws[l], saved[l], l, tok)
        if st is not None:
            parts[("a", l + 1)], tok = exchange_wait(st, dx, f"scatter_a{l + 1}_wait")
        st, tok = exchange_start(_behind(to_shards(gb, GROUP_B), tok), nb(GROUP_B), f"scatter_b{l}_start")
        dx, ga, small[l] = _bwd_a(dx, cot, lws[l], saved[l], cos2, sin2, l, tok, small[l])
        parts[("b", l)], tok = exchange_wait(st, dx, f"scatter_b{l}_wait")
        arrs, flags = to_shards(ga, GROUP_A), nb(GROUP_A)
        if l == 0:
            dmod = jnp.stack([small[q]["ada_b"] for q in range(LAYERS)])
            arrs += [_pack(small, SMALL), dmod.reshape(LAYERS, NDEV, 768).transpose(1, 0, 2)]
            flags += [True, False]
        st, tok = exchange_start(_behind(arrs, tok), flags, f"scatter_a{l}_start")

    out = {}

    def big_adamw(group, tok):
        res = None
        for n in group:
            grp, idx = ("a", GROUP_A.index(n)) if n in GROUP_A else ("b", GROUP_B.index(n))
            shp = p[n].shape
            flat = lambda a: a.reshape(shp[0] * shp[1], shp[2])
            res = adamw([parts[(grp, 0)][idx], parts[(grp, 1)][idx]], flat(p[n]), flat(mom[n]), flat(var[n]),
                        f"adamw_{n}", tok)
            out[n] = [r.reshape(shp) for r in res]
        return res[0]

    g_last = big_adamw(GROUP_B, tok)
    got, _ = exchange_wait(st, g_last, "scatter_a0_wait")
    parts[("a", 0)], small_all, dmod_in = got[0:3], got[3], got[4]
    big_adamw(GROUP_A, None)

    dmod16 = jnp.pad(dmod_in, ((0, 8), (0, 0), (0, 0)))
    g_ada = jnp.stack([tn_matmul(cact, dmod16[:, l], f"dw_ada{l}") for l in range(LAYERS)])
    flat = lambda a: a.reshape(LAYERS * D, 768)
    out["ada_w"] = [r.reshape(ada_w.shape) for r in
                    adamw([flat(g_ada)[None]], flat(ada_w), flat(m_ada_w), flat(v_ada_w), "adamw_ada_w")]

    zeros = jnp.zeros((SMALL_ROWS, D), F32)
    g_small = _unpack(adamw([small_all], zeros, zeros, zeros, "sum_small")[0], SMALL)
    per = lambda d, nm: [{n: d[n][l] for n, _ in nm} for l in range(LAYERS)]
    res = adamw([_pack(per(g_small, SMALL_REPL), SMALL_REPL)[None]], _pack(per(p, SMALL_REPL), SMALL_REPL),
                _pack(per(mom, SMALL_REPL), SMALL_REPL), _pack(per(var, SMALL_REPL), SMALL_REPL), "adamw_small")
    res = [_unpack(r, SMALL_REPL) for r in res]
    for n, _ in SMALL_REPL:
        out[n] = [r[n] for r in res]
    for n, k, w in (("ssd_conv_w", 4, 192), ("ffn_conv_w", 3, 704)):
        g_mine = lax.dynamic_slice(g_small[n], (0, 0, me * w), (LAYERS, k, w))
        f2 = lambda a: jnp.pad(a.reshape(LAYERS * k, w), ((0, 8 - LAYERS * k), (0, 0)))
        res = adamw([f2(g_mine)[None]], f2(p[n]), f2(mom[n]), f2(var[n]), f"adamw_{n}")
        out[n] = [r[0:LAYERS * k].reshape(LAYERS, k, w) for r in res]

    outs = [loss, dx[None]]
    for q in range(4):
        outs += [out[n][q] for n in names]
    return tuple(outs)
```

```python
import functools
import math

import jax
import jax.numpy as jnp
from jax import lax
from jax.experimental import pallas as pl
from jax.experimental.pallas import tpu as pltpu

F32, BF16 = jnp.float32, jnp.bfloat16
EPS = 1e-6
D = 1024
NDEV = 8
LAYERS = 2
HEADS = 8
FFN = 2816
FFN_TILE = 1408
FFN_NT = FFN // FFN_TILE
ATT_SCALE = 96 ** -0.5
ROPE_THETA = 10000.0
LR, B1, B2, ADAM_EPS, WD, STEP = 0.001, 0.9, 0.999, 1e-08, 0.01, 10

O_G, O_XS, O_Z, O_PU, O_BC, O_CKV, O_KR, O_KRS, O_DT, O_QL = 0, 3072, 4096, 5120, 5632, 6144, 6400, 6528, 6656, 6912
NPROJ = 7296
CONST = dict(pipeline_mode=pl.Buffered(1))


def _pick(n, cap, mult=128):
    if n <= cap:
        return n
    best = None
    for t in range(mult, cap + 1, mult):
        if n % t == 0:
            best = t
    assert best is not None, (n, cap, mult)
    return best


def _sig(x):
    return 1.0 / (1.0 + jnp.exp(-x))


def _rms(x, w, n):
    return x * lax.rsqrt(jnp.sum(x * x, axis=-1, keepdims=True) / n + EPS) * w


def _raw(a, b, dims):
    return lax.dot_general(a.astype(BF16), b.astype(BF16), dims, preferred_element_type=F32)


_NN = (((1,), (0,)), ((), ()))
_NT = (((1,), (1,)), ((), ()))
_TN = (((0,), (0,)), ((), ()))
_BNN = (((2,), (1,)), ((0,), (0,)))
_BNT = (((2,), (2,)), ((0,), (0,)))
_BTN = (((1,), (1,)), ((0,), (0,)))


@jax.custom_vjp
def mm_nn(a, b):
    return _raw(a, b, _NN)


mm_nn.defvjp(lambda a, b: (_raw(a, b, _NN), (a, b)),
             lambda r, g: (_raw(g, r[1], _NT), _raw(r[0], g, _TN)))


@jax.custom_vjp
def mm_nc(a, b):
    return _raw(a, b, _NN)


mm_nc.defvjp(lambda a, b: (_raw(a, b, _NN), b),
             lambda b, g: (_raw(g, b, _NT), jnp.zeros_like(b)))


@jax.custom_vjp
def mm_nt(a, b):
    return _raw(a, b, _NT)


mm_nt.defvjp(lambda a, b: (_raw(a, b, _NT), (a, b)),
             lambda r, g: (_raw(g, r[1], _NN), _raw(g, r[0], _TN)))


@jax.custom_vjp
def bmm_nn(a, b):
    return _raw(a, b, _BNN)


bmm_nn.defvjp(lambda a, b: (_raw(a, b, _BNN), (a, b)),
              lambda r, g: (_raw(g, r[1], _BNT), _raw(r[0], g, _BTN)))


@jax.custom_vjp
def bmm_nt(a, b):
    return _raw(a, b, _BNT)


bmm_nt.defvjp(lambda a, b: (_raw(a, b, _BNT), (a, b)),
              lambda r, g: (_raw(g, r[1], _BNN), _raw(g, r[0], _BTN)))


@jax.custom_vjp
def softplus(x):
    t = jnp.exp(-jnp.abs(x))
    u = 1.0 + t
    one = u == 1.0
    l1p = jnp.where(one, t, jnp.log(u) * (t / jnp.where(one, 1.0, u - 1.0)))
    return jnp.maximum(x, 0.0) + l1p


softplus.defvjp(lambda x: (softplus(x), x), lambda x, g: (g * _sig(x),))


def _params(*sem):
    return pltpu.CompilerParams(dimension_semantics=sem, vmem_limit_bytes=56 * 1024 * 1024)


def all_to_all(arrs, bcast, name):
    n = len(arrs)
    out_shapes = [jax.ShapeDtypeStruct(((NDEV,) + a.shape) if b else a.shape, a.dtype) for a, b in zip(arrs, bcast)]

    def body(*refs):
        ins, outs, token = refs[:n], refs[n:2 * n], refs[2 * n]
        send_sems, recv_sems, local_sems = refs[2 * n + 1:]
        me, remote = _exchange_copies(ins, outs, bcast, send_sems, recv_sems)
        local = [pltpu.make_async_copy(ins[j] if bcast[j] else ins[j].at[me], outs[j].at[me], local_sems.at[j])
                 for j in range(n)]
        for cp in local + remote:
            cp.start()
        for cp in remote + local:
            cp.wait()
        token[...] = jnp.zeros_like(token)

    any_spec = pl.BlockSpec(memory_space=pl.ANY)
    res = pl.pallas_call(
        body, name=name, out_shape=out_shapes + [jax.ShapeDtypeStruct((8, 128), F32)], in_specs=[any_spec] * n,
        out_specs=[any_spec] * n + [pl.BlockSpec(memory_space=pltpu.VMEM)],
        scratch_shapes=[pltpu.SemaphoreType.DMA((7 * n,)), pltpu.SemaphoreType.DMA((7 * n,)),
                        pltpu.SemaphoreType.DMA((n,))],
        compiler_params=pltpu.CompilerParams(has_side_effects=True),
    )(*arrs)
    return res[:n], res[n]


def _peers():
    x, y, c = lax.axis_index("x"), lax.axis_index("y"), lax.axis_index("c")
    out = []
    for k in range(1, NDEV):
        px, py, pc = x ^ ((k >> 2) & 1), y ^ ((k >> 1) & 1), c ^ (k & 1)
        out.append(((px, py, pc), 4 * px + 2 * py + pc))
    return 4 * x + 2 * y + c, out


def _exchange_copies(ins, lands, bcast, send_sems, recv_sems):
    me, peers = _peers()
    n, copies = len(ins), []
    for k, (dev, lin) in enumerate(peers):
        for j in range(n):
            copies.append(pltpu.make_async_remote_copy(
                src_ref=ins[j] if bcast[j] else ins[j].at[lin], dst_ref=lands[j].at[me],
                send_sem=send_sems.at[k * n + j], recv_sem=recv_sems.at[k * n + j],
                device_id=dev, device_id_type=pl.DeviceIdType.MESH))
    return me, copies


_HBM = pl.BlockSpec(memory_space=pltpu.HBM)
_SEM = pl.BlockSpec(memory_space=pltpu.SEMAPHORE)
_EFFECT = pltpu.SideEffectType.DATAFLOW_SIDE_EFFECTING


def exchange_start(arrs, bcast, name):
    n = len(arrs)
    land_shapes = [((NDEV,) + a.shape) if b else a.shape for a, b in zip(arrs, bcast)]

    def body(*refs):
        ins, lands = refs[:n], refs[n:2 * n]
        send_sems, recv_sems = refs[2 * n], refs[2 * n + 1]
        token = refs[-1]
        _, copies = _exchange_copies(ins, lands, bcast, send_sems, recv_sems)
        for cp in copies:
            cp.start()
        token[...] = jnp.zeros_like(token)

    hbm = lambda shp, a: pltpu.HBM(shp, a.dtype)
    res = pl.pallas_call(
        body, name=name,
        out_shape=[pltpu.SemaphoreType.DMA((7 * n,)), pltpu.SemaphoreType.DMA((7 * n,))]
                  + [hbm(a.shape, a) for a in arrs] + [hbm(s_, a) for s_, a in zip(land_shapes, arrs)]
                  + [jax.ShapeDtypeStruct((8, 128), F32)],
        in_specs=[_HBM] * (2 * n), out_specs=[_SEM, _SEM] + [_HBM] * (2 * n) + [pl.BlockSpec(memory_space=pltpu.VMEM)],
        input_output_aliases={i: 2 + i for i in range(2 * n)},
        compiler_params=pltpu.CompilerParams(has_side_effects=_EFFECT),
    )(*[pltpu.with_memory_space_constraint(a, pltpu.HBM) for a in arrs],
      *[pltpu.with_memory_space_constraint(lax.empty(s_, a.dtype), pltpu.HBM) for s_, a in zip(land_shapes, arrs)])
    return (res[0], res[1], res[2:2 + n], res[2 + n:2 + 2 * n], tuple(bcast)), res[-1]


def exchange_wait(state, after, name):
    send_sems, recv_sems, ins, lands, bcast = state
    n = len(ins)

    def body(*refs):
        in_refs, land_refs = refs[:n], refs[n:2 * n]
        s_sems, r_sems = refs[2 * n], refs[2 * n + 1]
        token = refs[-1]
        _, copies = _exchange_copies(in_refs, land_refs, bcast, s_sems, r_sems)
        for cp in copies:
            cp.wait_send()
            cp.wait_recv()
        token[...] = jnp.zeros_like(token)

    res = pl.pallas_call(
        body, name=name,
        out_shape=[pltpu.HBM(a.shape, a.dtype) for a in ins] + [pltpu.HBM(a.shape, a.dtype) for a in lands]
                  + [jax.ShapeDtypeStruct((8, 128), F32)],
        in_specs=[_HBM] * (2 * n) + [_SEM, _SEM, pl.BlockSpec(memory_space=pl.ANY)],
        out_specs=[_HBM] * (2 * n) + [pl.BlockSpec(memory_space=pltpu.VMEM)],
        input_output_aliases={i: i for i in range(2 * n)},
        compiler_params=pltpu.CompilerParams(has_side_effects=_EFFECT),
    )(*ins, *lands, send_sems, recv_sems, after)
    me = 4 * lax.axis_index("x") + 2 * lax.axis_index("y") + lax.axis_index("c")
    got = []
    for j in range(n):
        own = res[j][None] if bcast[j] else lax.dynamic_index_in_dim(res[j], me, 0, keepdims=True)
        got.append(lax.dynamic_update_slice_in_dim(res[n + j], own, me, axis=0))
    return got, res[-1]


def norm_proj_fwd(x, vec, w, name):
    s, n = x.shape[0], w.shape[1]
    tr, tn = _pick(s, 512), _pick(n, 2560)
    jdt, odt = O_DT // tn, O_DT % tn

    def body(x_ref, v_ref, w_ref, o_ref, h_ref, dt_ref, h_scr):
        @pl.when(pl.program_id(1) == 0)
        def _():
            h = _rms(x_ref[...], v_ref[0:1, :], D) * (1.0 + v_ref[2:3, :]) + v_ref[1:2, :]
            h_scr[...] = h.astype(BF16)
            h_ref[...] = h.astype(BF16)
        res = jnp.dot(h_scr[...], w_ref[...], preferred_element_type=F32)
        o_ref[...] = res

        @pl.when(pl.program_id(1) == jdt)
        def _():
            dt_ref[...] = res[:, odt:odt + 128]

    return pl.pallas_call(
        body, name=name, grid=(s // tr, n // tn),
        in_specs=[pl.BlockSpec((tr, D), lambda i, j: (i, 0)), pl.BlockSpec((8, D), lambda i, j: (0, 0)),
                  pl.BlockSpec((D, tn), lambda i, j: (0, j))],
        out_specs=[pl.BlockSpec((tr, tn), lambda i, j: (i, j)), pl.BlockSpec((tr, D), lambda i, j: (i, 0)),
                   pl.BlockSpec((tr, 128), lambda i, j: (i, 0))],
        out_shape=[jax.ShapeDtypeStruct((s, n), F32), jax.ShapeDtypeStruct((s, D), BF16),
                   jax.ShapeDtypeStruct((s, 128), F32)],
        scratch_shapes=[pltpu.VMEM((tr, D), BF16)],
        compiler_params=_params("arbitrary", "arbitrary"),
    )(x, vec, w)


def _col_tiles(arr, cap):
    if arr.ndim == 2:
        n = arr.shape[1]
        t = _pick(n, cap)
        return n, t, lambda rows, ix: pl.BlockSpec((rows, t), lambda *g: ix(*g))
    width = arr.shape[2]
    t = _pick(width, cap)
    per = width // t

    def spec(rows, ix):
        def index(*g):
            r, j = ix(*g)
            return (j // per, r, j % per)
        return pl.BlockSpec((None, rows, t), index)
    return arr.shape[0] * width, t, spec


def norm_proj_bwd(x, vec, dp, w, dx_in, aux, name):
    s = x.shape[0]
    tr = _pick(s, 512)
    n, tk, dp_spec = _col_tiles(dp, 2560)
    nk, has_aux = n // tk, aux is not None

    def body(*refs):
        if has_aux:
            x_ref, v_ref, dp_ref, w_ref, dxin_ref, aux_ref, dx_ref, dv_ref, acc = refs
        else:
            x_ref, v_ref, dp_ref, w_ref, dxin_ref, dx_ref, dv_ref, acc = refs
        i, k = pl.program_id(0), pl.program_id(1)

        @pl.when(k == 0)
        def _():
            acc[...] = jnp.zeros_like(acc)

        acc[...] += _raw(dp_ref[...], w_ref[...], _NT)

        @pl.when(k == nk - 1)
        def _():
            f = lambda xx, nw, sh, sc: _rms(xx, nw, D) * (1.0 + sc) + sh
            _, vjp = jax.vjp(f, x_ref[...], v_ref[0:1, :], v_ref[1:2, :], v_ref[2:3, :])
            dx, dnw, dsh, dsc = vjp(acc[...])
            dx_ref[...] = dxin_ref[...] + dx

            @pl.when(i == 0)
            def _():
                dv_ref[...] = jnp.zeros_like(dv_ref)

            dv_ref[0:1, :] += dnw
            dv_ref[1:2, :] += dsh
            dv_ref[2:3, :] += dsc
            if has_aux:
                dv_ref[3:4, :] += jnp.sum(dxin_ref[...] * aux_ref[...], axis=0, keepdims=True)

    row = pl.BlockSpec((tr, D), lambda i, k: (i, 0))
    in_specs = [row, pl.BlockSpec((8, D), lambda i, k: (0, 0)), dp_spec(tr, lambda i, k: (i, k)),
                pl.BlockSpec((D, tk), lambda i, k: (0, k)), row] + ([row] if has_aux else [])
    args = [x, vec, dp, w, dx_in] + ([aux] if has_aux else [])
    return pl.pallas_call(
        body, name=name, grid=(s // tr, nk), in_specs=in_specs,
        out_specs=[row, pl.BlockSpec((8, D), lambda i, k: (0, 0))],
        out_shape=[jax.ShapeDtypeStruct((s, D), F32), jax.ShapeDtypeStruct((8, D), F32)],
        scratch_shapes=[pltpu.VMEM((tr, D), F32)],
        compiler_params=_params("arbitrary", "arbitrary"),
    )(*args)


def tn_matmul(a, b, name, scale=None, out_dtype=None):
    out_dtype = BF16 if out_dtype is None else out_dtype
    s, m = a.shape
    ts, tm = _pick(s, 512, 16), _pick(m, 1408)
    n, tn, b_spec = _col_tiles(b, 2560)
    ns, has_scale = s // ts, scale is not None

    def body(*refs):
        if has_scale:
            a_ref, b_ref, sc_ref, o_ref, acc = refs
        else:
            a_ref, b_ref, o_ref, acc = refs
        k = pl.program_id(2)

        @pl.when(k == 0)
        def _():
            acc[...] = jnp.zeros_like(acc)

        acc[...] += _raw(a_ref[...], b_ref[...], _TN)

        @pl.when(k == ns - 1)
        def _():
            o_ref[...] = (acc[...] * sc_ref[...] if has_scale else acc[...]).astype(out_dtype)

    in_specs = [pl.BlockSpec((ts, tm), lambda i, j, k: (k, i)), b_spec(ts, lambda i, j, k: (k, j))]
    if has_scale:
        in_specs.append(pl.BlockSpec((1, tn), lambda i, j, k: (0, j)))
    return pl.pallas_call(
        body, name=name, grid=(m // tm, n // tn, ns), in_specs=in_specs,
        out_specs=pl.BlockSpec((tm, tn), lambda i, j, k: (i, j)),
        out_shape=jax.ShapeDtypeStruct((m, n), out_dtype),
        scratch_shapes=[pltpu.VMEM((tm, tn), F32)],
        compiler_params=_params("arbitrary", "arbitrary", "arbitrary"),
    )(*([a, b] + ([scale] if has_scale else [])))


def ada_mod(c16, w):
    ncol = w.shape[2]

    def body(c_ref, w_ref, o_ref, a_ref):
        cc = c_ref[...]
        act = cc * _sig(cc)
        a_ref[...] = act
        o_ref[...] = _raw(act, w_ref[...], _NN)

    return pl.pallas_call(
        body, name="ada_mod", grid=(LAYERS,),
        in_specs=[pl.BlockSpec((16, D), lambda l: (0, 0)), pl.BlockSpec((None, D, ncol), lambda l: (l, 0, 0))],
        out_specs=[pl.BlockSpec((None, 16, ncol), lambda l: (l, 0, 0)), pl.BlockSpec((16, D), lambda l: (0, 0))],
        out_shape=[jax.ShapeDtypeStruct((LAYERS, 16, ncol), F32), jax.ShapeDtypeStruct((16, D), F32)],
        compiler_params=_params("arbitrary"),
    )(c16, w)


def _mla_head(q_lat, c_kv, kr, krs, wqn, wqr, wqrs, wkn, wv, qa_w, kva_w, qn_w, qr_w, qrs_w, kn_w, kr_w, krs_w,
              cos2, sin2):
    qn = _rms(q_lat, qa_w, 384.0)
    kvn = _rms(c_kv, kva_w, 256.0)
    qnope = _rms(mm_nn(qn, wqn), qn_w, 64.0)
    qr, qrs = mm_nn(qn, wqr), mm_nn(qn, wqrs)
    rq = lax.rsqrt(jnp.sum(qr * qr, axis=-1, keepdims=True) / 32.0 + EPS)
    qrope = rq * (qr * qr_w * cos2 + qrs * qrs_w * sin2)
    knope = _rms(mm_nn(kvn, wkn), kn_w, 64.0)
    v = mm_nn(kvn, wv)
    rk = lax.rsqrt(jnp.sum(kr * kr, axis=-1, keepdims=True) / 32.0 + EPS)
    krope = rk * (kr * kr_w * cos2 + krs * krs_w * sin2)
    return qnope, qrope, knope, krope, v


def _mla_vec_pieces(v_ref):
    return (v_ref[0:1, 0:384], v_ref[1:2, 0:256], v_ref[2:3, 0:128], v_ref[2:3, 128:256], v_ref[2:3, 256:384],
            v_ref[3:4, 0:128], v_ref[3:4, 128:256], v_ref[3:4, 256:384])


def _mla_in_specs(tr):
    return [pl.BlockSpec((tr, 384), lambda i: (i, O_QL // 384)), pl.BlockSpec((tr, 256), lambda i: (i, O_CKV // 256)),
            pl.BlockSpec((tr, 128), lambda i: (i, O_KR // 128)), pl.BlockSpec((tr, 128), lambda i: (i, O_KRS // 128)),
            pl.BlockSpec((HEADS, 384, 384), lambda i: (0, 0, 0), **CONST),
            pl.BlockSpec((HEADS, 256, 256), lambda i: (0, 0, 0), **CONST),
            pl.BlockSpec((8, 512), lambda i: (0, 0)),
            pl.BlockSpec((tr, 128), lambda i: (i, 0)), pl.BlockSpec((tr, 128), lambda i: (i, 0))]


def mla_pre_fwd(proj, wq, wkv, vec, cos2, sin2, name):
    s = proj.shape[0]
    tr = _pick(s, 256)

    def body(ql_ref, ckv_ref, kr_ref, krs_ref, wq_ref, wkv_ref, v_ref, cos_ref, sin_ref, q_out, k_out, v_out):
        acts = (ql_ref[...], ckv_ref[...], kr_ref[...], krs_ref[...])
        vp = _mla_vec_pieces(v_ref)
        for h in range(HEADS):
            ws = (wq_ref[h, :, 0:128], wq_ref[h, :, 128:256], wq_ref[h, :, 256:384],
                  wkv_ref[h, :, 0:128], wkv_ref[h, :, 128:256])
            qn, qr, kn, krp, v = _mla_head(*acts, *ws, *vp, cos_ref[...], sin_ref[...])
            q_out[h, :, 0:128] = qn.astype(BF16)
            q_out[h, :, 128:256] = qr.astype(BF16)
            k_out[h, :, 0:128] = kn.astype(BF16)
            k_out[h, :, 128:256] = krp.astype(BF16)
            v_out[h] = v.astype(BF16)

    return pl.pallas_call(
        body, name=name, grid=(s // tr,), in_specs=_mla_in_specs(tr),
        out_specs=[pl.BlockSpec((HEADS, tr, 256), lambda i: (0, i, 0)), pl.BlockSpec((HEADS, tr, 256), lambda i: (0, i, 0)),
                   pl.BlockSpec((HEADS, tr, 128), lambda i: (0, i, 0))],
        out_shape=[jax.ShapeDtypeStruct((HEADS, s, 256), BF16), jax.ShapeDtypeStruct((HEADS, s, 256), BF16),
                   jax.ShapeDtypeStruct((HEADS, s, 128), BF16)],
        compiler_params=_params("arbitrary"),
    )(proj, proj, proj, proj, wq, wkv, vec, cos2, sin2)


def mla_pre_bwd(proj, wq, wkv, vec, cos2, sin2, dq, dk, dv, name):
    s = proj.shape[0]
    tr = _pick(s, 256)

    def body(ql_ref, ckv_ref, kr_ref, krs_ref, wq_ref, wkv_ref, v_ref, cos_ref, sin_ref, dq_ref, dk_ref, dv_ref,
             dql_out, dckv_out, dkr_out, dkrs_out, dwq_out, dwkv_out, dvec_out):
        @pl.when(pl.program_id(0) == 0)
        def _():
            dwq_out[...] = jnp.zeros_like(dwq_out)
            dwkv_out[...] = jnp.zeros_like(dwkv_out)
            dvec_out[...] = jnp.zeros_like(dvec_out)

        acts = (ql_ref[...], ckv_ref[...], kr_ref[...], krs_ref[...])
        vp = _mla_vec_pieces(v_ref)
        cos2_, sin2_ = cos_ref[...], sin_ref[...]

        def head(h, carry):
            wq_h, wkv_h = wq_ref[h].astype(F32), wkv_ref[h].astype(F32)
            ws = (wq_h[:, 0:128], wq_h[:, 128:256], wq_h[:, 256:384], wkv_h[:, 0:128], wkv_h[:, 128:256])
            f = lambda *a: _mla_head(*a, cos2_, sin2_)
            _, vjp = jax.vjp(f, *acts, *ws, *vp)
            dq_h, dk_h = dq_ref[h], dk_ref[h]
            g = vjp((dq_h[:, 0:128], dq_h[:, 128:256], dk_h[:, 0:128], dk_h[:, 128:256], dv_ref[h]))
            dwq_out[h, :, 0:128] += g[4]
            dwq_out[h, :, 128:256] += g[5]
            dwq_out[h, :, 256:384] += g[6]
            dwkv_out[h, :, 0:128] += g[7]
            dwkv_out[h, :, 128:256] += g[8]
            dvec_out[0:1, 0:384] += g[9]
            dvec_out[1:2, 0:256] += g[10]
            dvec_out[2:3, 0:128] += g[11]
            dvec_out[2:3, 128:256] += g[12]
            dvec_out[2:3, 256:384] += g[13]
            dvec_out[3:4, 0:128] += g[14]
            dvec_out[3:4, 128:256] += g[15]
            dvec_out[3:4, 256:384] += g[16]
            return tuple(c + gg for c, gg in zip(carry, g[:4]))

        tot = lax.fori_loop(0, HEADS, head, tuple(jnp.zeros_like(a) for a in acts))
        dql_out[...] = tot[0].astype(BF16)
        dckv_out[...] = tot[1].astype(BF16)
        dkr_out[...] = tot[2].astype(BF16)
        dkrs_out[...] = tot[3].astype(BF16)

    hb = lambda w: pl.BlockSpec((HEADS, tr, w), lambda i: (0, i, 0))
    return pl.pallas_call(
        body, name=name, grid=(s // tr,), in_specs=_mla_in_specs(tr) + [hb(256), hb(256), hb(128)],
        out_specs=[pl.BlockSpec((tr, 384), lambda i: (i, 0)), pl.BlockSpec((tr, 256), lambda i: (i, 0)),
                   pl.BlockSpec((tr, 128), lambda i: (i, 0)), pl.BlockSpec((tr, 128), lambda i: (i, 0)),
                   pl.BlockSpec((HEADS, 384, 384), lambda i: (0, 0, 0)), pl.BlockSpec((HEADS, 256, 256), lambda i: (0, 0, 0)),
                   pl.BlockSpec((8, 512), lambda i: (0, 0))],
        out_shape=[jax.ShapeDtypeStruct((s, 384), BF16), jax.ShapeDtypeStruct((s, 256), BF16),
                   jax.ShapeDtypeStruct((s, 128), BF16), jax.ShapeDtypeStruct((s, 128), BF16),
                   jax.ShapeDtypeStruct((HEADS, 384, 384), F32), jax.ShapeDtypeStruct((HEADS, 256, 256), F32),
                   jax.ShapeDtypeStruct((8, 512), F32)],
        compiler_params=_params("arbitrary"),
    )(proj, proj, proj, proj, wq, wkv, vec, cos2, sin2, dq, dk, dv)


def _att_probs(q, kk, i, tq):
    sc = _raw(q, kk, _NT) * ATT_SCALE
    rows = lax.broadcasted_iota(jnp.int32, sc.shape, 0) + i * tq
    cols = lax.broadcasted_iota(jnp.int32, sc.shape, 1)
    sc = jnp.where(cols <= rows, sc, -jnp.inf)
    e = jnp.exp(sc - jnp.max(sc, axis=-1, keepdims=True))
    return e / jnp.sum(e, axis=-1, keepdims=True)


def mla_attn_fwd(q, k, v, name):
    s = q.shape[1]
    tq = _pick(s, 256)

    def body(q_ref, k_ref, v_ref, o_ref):
        for i in range(s // tq):
            n = (i + 1) * tq
            p = _att_probs(q_ref[i * tq:n, :], k_ref[0:n, :], i, tq)
            o_ref[i * tq:n, :] = _raw(p, v_ref[0:n, :], _NN)

    hs = lambda w: pl.BlockSpec((None, s, w), lambda h: (h, 0, 0))
    return pl.pallas_call(
        body, name=name, grid=(HEADS,), in_specs=[hs(256), hs(256), hs(128)],
        out_specs=pl.BlockSpec((s, 128), lambda h: (0, h)),
        out_shape=jax.ShapeDtypeStruct((s, HEADS * 128), F32),
        compiler_params=_params("arbitrary"),
    )(q, k, v)


def mla_attn_bwd(q, k, v, do, name):
    s = q.shape[1]
    tq = _pick(s, 256)

    def body(q_ref, k_ref, v_ref, do_ref, dq_ref, dk_ref, dv_ref):
        dk_ref[...] = jnp.zeros_like(dk_ref)
        dv_ref[...] = jnp.zeros_like(dv_ref)
        for i in range(s // tq):
            n = (i + 1) * tq
            qq, kk, vv = q_ref[i * tq:n, :], k_ref[0:n, :], v_ref[0:n, :]
            p = _att_probs(qq, kk, i, tq)
            o = _raw(p, vv, _NN)
            dout = do_ref[i * tq:n, :]
            delta = jnp.sum(dout * o, axis=-1, keepdims=True)
            dp = _raw(dout, vv, _NT)
            ds = p * (dp - delta) * ATT_SCALE
            dq_ref[i * tq:n, :] = _raw(ds, kk, _NN)
            dk_ref[0:n, :] += _raw(ds, qq, _TN)
            dv_ref[0:n, :] += _raw(p, dout, _TN)

    hs = lambda w: pl.BlockSpec((None, s, w), lambda h: (h, 0, 0))
    return pl.pallas_call(
        body, name=name, grid=(HEADS,),
        in_specs=[hs(256), hs(256), hs(128), pl.BlockSpec((s, 128), lambda h: (0, h))],
        out_specs=[hs(256), hs(256), hs(128)],
        out_shape=[jax.ShapeDtypeStruct((HEADS, s, 256), F32), jax.ShapeDtypeStruct((HEADS, s, 256), F32),
                   jax.ShapeDtypeStruct((HEADS, s, 128), F32)],
        compiler_params=_params("arbitrary"),
    )(q, k, v, do)


def _pool_windows(u, pad, s, g):
    pad[0:16, :] = jnp.zeros((16, 128), F32)
    cur, sel = u, None
    for j, k in enumerate((1, 2, 4, 8)):
        pad[16:16 + s, :] = cur
        cur = cur + pad[16 - k:16 - k + s, :]
        sel = cur if sel is None else jnp.where(g == j, cur, sel)
    return sel


def _pool_count(s, g):
    t = lax.broadcasted_iota(jnp.int32, (s, 1), 0)
    return jnp.minimum(t + 1, 2 << g).astype(F32)


def pool_fwd(proj, pw, ps, name):
    s = proj.shape[0]

    def body(u_ref, w_ref, s_ref, o_ref, pad):
        g = pl.program_id(0)
        u = u_ref[...]
        pooled = _pool_windows(u, pad, s, g) / _pool_count(s, g) - u
        o_ref[...] = _raw(pooled, w_ref[...], _NN) * s_ref[...]

    return pl.pallas_call(
        body, name=name, grid=(4,),
        in_specs=[pl.BlockSpec((s, 128), lambda g: (0, O_PU // 128 + g)), pl.BlockSpec((None, 128, 128), lambda g: (g, 0, 0)),
                  pl.BlockSpec((1, 128), lambda g: (0, g))],
        out_specs=pl.BlockSpec((s, 128), lambda g: (0, g)),
        out_shape=jax.ShapeDtypeStruct((s, 512), F32),
        scratch_shapes=[pltpu.VMEM((s + 16, 128), F32)],
        compiler_params=_params("arbitrary"),
    )(proj, pw, ps)


def pool_bwd(proj, pw, ps, do, name):
    s = proj.shape[0]

    def body(u_ref, w_ref, s_ref, do_ref, du_ref, dw_ref, ds_ref, pad):
        g = pl.program_id(0)
        u, w, dout = u_ref[...], w_ref[...], do_ref[...]
        cnt = _pool_count(s, g)
        pooled = _pool_windows(u, pad, s, g) / cnt - u
        mixed = _raw(pooled, w, _NN)
        ds_ref[...] = jnp.sum(dout * mixed, axis=0, keepdims=True)
        dmixed = dout * s_ref[...]
        dw_ref[...] = _raw(pooled, dmixed, _TN)
        dpooled = _raw(dmixed, w, _NT)
        dsel = dpooled / cnt
        pad[s:s + 16, :] = jnp.zeros((16, 128), F32)
        cur = jnp.where(g == 3, dsel, 0.0)
        for j, k in ((2, 8), (1, 4), (0, 2)):
            pad[0:s, :] = cur
            cur = cur + pad[k:k + s, :] + jnp.where(g == j, dsel, 0.0)
        pad[0:s, :] = cur
        cur = cur + pad[1:1 + s, :]
        du_ref[...] = (cur - dpooled).astype(BF16)

    return pl.pallas_call(
        body, name=name, grid=(4,),
        in_specs=[pl.BlockSpec((s, 128), lambda g: (0, O_PU // 128 + g)), pl.BlockSpec((None, 128, 128), lambda g: (g, 0, 0)),
                  pl.BlockSpec((1, 128), lambda g: (0, g)), pl.BlockSpec((s, 128), lambda g: (0, g))],
        out_specs=[pl.BlockSpec((s, 128), lambda g: (0, g)), pl.BlockSpec((None, 128, 128), lambda g: (g, 0, 0)),
                   pl.BlockSpec((1, 128), lambda g: (0, g))],
        out_shape=[jax.ShapeDtypeStruct((s, 512), BF16), jax.ShapeDtypeStruct((4, 128, 128), F32),
                   jax.ShapeDtypeStruct((1, 512), F32)],
        scratch_shapes=[pltpu.VMEM((s + 16, 128), F32)],
        compiler_params=_params("arbitrary"),
    )(proj, pw, ps, do)


def _xbc_col(i):
    return jnp.where(i < 2, O_XS // 512 + i, O_BC // 512)


def conv_fwd(proj, cw, cb, name):
    s = proj.shape[0]

    def body(x_ref, w_ref, b_ref, o_ref, t_ref, pad):
        pad[0:8, :] = jnp.zeros((8, 512), F32)
        pad[8:8 + s, :] = x_ref[...]
        y = b_ref[...] + sum(w_ref[k:k + 1, :] * pad[5 + k:5 + k + s, :] for k in range(4))
        act = y * _sig(y)
        o_ref[...] = act

        @pl.when(pl.program_id(0) < 2)
        def _():
            t_ref[...] = act.T

    return pl.pallas_call(
        body, name=name, grid=(3,),
        in_specs=[pl.BlockSpec((s, 512), lambda i: (0, _xbc_col(i))), pl.BlockSpec((4, 512), lambda i: (0, i)),
                  pl.BlockSpec((1, 512), lambda i: (0, i))],
        out_specs=[pl.BlockSpec((s, 512), lambda i: (0, i)), pl.BlockSpec((512, s), lambda i: (jnp.minimum(i, 1), 0))],
        out_shape=[jax.ShapeDtypeStruct((s, 1536), F32), jax.ShapeDtypeStruct((D, s), F32)],
        scratch_shapes=[pltpu.VMEM((s + 8, 512), F32)],
        compiler_params=_params("arbitrary"),
    )(proj, cw, cb)


def conv_bwd(proj, cw, cb, dxt, dbm, dcm, name):
    s = proj.shape[0]

    def body(x_ref, w_ref, b_ref, dxt_ref, dbm_ref, dcm_ref, dx_ref, dw_ref, db_ref, pad, pad2):
        pad[0:8, :] = jnp.zeros((8, 512), F32)
        pad[8:8 + s, :] = x_ref[...]
        y = b_ref[...] + sum(w_ref[k:k + 1, :] * pad[5 + k:5 + k + s, :] for k in range(4))
        sg = _sig(y)

        @pl.when(pl.program_id(0) < 2)
        def _():
            pad2[0:s, :] = dxt_ref[...].T

        @pl.when(pl.program_id(0) == 2)
        def _():
            pad2[0:s, 0:256] = dbm_ref[...]
            pad2[0:s, 256:512] = dcm_ref[...]

        dy = pad2[0:s, :] * (sg * (1.0 + y * (1.0 - sg)))
        db_ref[...] = jnp.sum(dy, axis=0, keepdims=True)
        for k in range(4):
            dw_ref[k:k + 1, :] = jnp.sum(dy * pad[5 + k:5 + k + s, :], axis=0, keepdims=True)
        pad2[s:s + 8, :] = jnp.zeros((8, 512), F32)
        pad2[0:s, :] = dy
        dx_ref[...] = sum(w_ref[k:k + 1, :] * pad2[3 - k:3 - k + s, :] for k in range(4)).astype(BF16)

    return pl.pallas_call(
        body, name=name, grid=(3,),
        in_specs=[pl.BlockSpec((s, 512), lambda i: (0, _xbc_col(i))), pl.BlockSpec((4, 512), lambda i: (0, i)),
                  pl.BlockSpec((1, 512), lambda i: (0, i)), pl.BlockSpec((512, s), lambda i: (jnp.minimum(i, 1), 0)),
                  pl.BlockSpec((s, 256), lambda i: (0, 0)), pl.BlockSpec((s, 256), lambda i: (0, 0))],
        out_specs=[pl.BlockSpec((s, 512), lambda i: (0, i)), pl.BlockSpec((4, 512), lambda i: (0, i)),
                   pl.BlockSpec((1, 512), lambda i: (0, i))],
        out_shape=[jax.ShapeDtypeStruct((s, 1536), BF16), jax.ShapeDtypeStruct((4, 1536), F32),
                   jax.ShapeDtypeStruct((1, 1536), F32)],
        scratch_shapes=[pltpu.VMEM((s + 8, 512), F32), pltpu.VMEM((s + 8, 512), F32)],
        compiler_params=_params("arbitrary"),
    )(proj, cw, cb, dxt, dbm, dcm)


def _ssd_chunk(xt, dtr, dtc, bm, cm, hprev, alog, dbias, dskip):
    ln = 128
    a = -jnp.exp(alog)
    dt_r = softplus(dtr + dbias)
    da_r = dt_r * a
    da_c = softplus(dtc + dbias) * a
    li = lax.broadcasted_iota(jnp.int32, (1, ln, ln), 1)
    si = lax.broadcasted_iota(jnp.int32, (1, ln, ln), 2)
    causal = si <= li
    acs_c = jnp.sum(jnp.where(causal, da_r, 0.0), axis=2, keepdims=True)
    acs_r = jnp.sum(jnp.where(li <= si, da_c, 0.0), axis=1, keepdims=True)
    acs_last = jnp.sum(da_r, axis=2, keepdims=True)
    decay = jnp.exp(jnp.where(causal, acs_c - acs_r, -jnp.inf))
    m = mm_nt(cm, bm)[None] * decay
    xdt = xt * dt_r
    y_diag = bmm_nt(xdt, m)
    bb = jnp.broadcast_to(bm[None], (8, ln, ln))
    cc = jnp.broadcast_to(cm[None], (8, ln, ln))
    states = bmm_nn(xdt * jnp.exp(acs_last - acs_r), bb)
    y_off = bmm_nt(hprev, cc) * jnp.exp(acs_r)
    hnew = hprev * jnp.exp(acs_last) + states
    return y_diag + y_off + xt * dskip, hnew


def _ssd_specs(nc, rev):
    cix = (lambda c: nc - 1 - c) if rev else (lambda c: c)
    hv = pl.BlockSpec((8, 1, 1), lambda g, c: (g, 0, 0))
    return [pl.BlockSpec((8, 64, 128), lambda g, c: (g, 0, cix(c))), pl.BlockSpec((8, 1, 128), lambda g, c: (g, 0, cix(c))),
            pl.BlockSpec((8, 128, 1), lambda g, c: (g, cix(c), 0)), pl.BlockSpec((128, 128), lambda g, c: (cix(c), 8 + g)),
            pl.BlockSpec((128, 128), lambda g, c: (cix(c), 10 + g))], hv, cix


def ssd_fwd(xt, dtr, dtc, xbc, alog, dbias, dskip, name):
    s = xt.shape[2]
    nc = s // 128
    specs, hv, _ = _ssd_specs(nc, False)

    def body(x_ref, dr_ref, dc_ref, b_ref, c_ref, al_ref, db_ref, dk_ref, y_ref, hs_ref, h_scr):
        @pl.when(pl.program_id(1) == 0)
        def _():
            h_scr[...] = jnp.zeros_like(h_scr)
        hp = h_scr[...]
        hs_ref[...] = hp
        y, hn = _ssd_chunk(x_ref[...], dr_ref[...], dc_ref[...], b_ref[...], c_ref[...], hp,
                           al_ref[...], db_ref[...], dk_ref[...])
        y_ref[...] = y
        h_scr[...] = hn

    return pl.pallas_call(
        body, name=name, grid=(2, nc), in_specs=specs + [hv, hv, hv],
        out_specs=[pl.BlockSpec((8, 64, 128), lambda g, c: (g, 0, c)),
                   pl.BlockSpec((None, None, 8, 64, 128), lambda g, c: (g, c, 0, 0, 0))],
        out_shape=[jax.ShapeDtypeStruct((16, 64, s), F32), jax.ShapeDtypeStruct((2, nc, 8, 64, 128), F32)],
        scratch_shapes=[pltpu.VMEM((8, 64, 128), F32)],
        compiler_params=_params("arbitrary", "arbitrary"),
    )(xt, dtr, dtc, xbc, xbc, alog, dbias, dskip)


def ssd_bwd(xt, dtr, dtc, xbc, alog, dbias, dskip, hs, dyt, name):
    s = xt.shape[2]
    nc = s // 128
    specs, hv, cix = _ssd_specs(nc, True)

    def body(x_ref, dr_ref, dc_ref, b_ref, c_ref, al_ref, db_ref, dk_ref, hs_ref, dy_ref,
             dx_out, ddr_out, ddc_out, dbm_out, dcm_out, dal_out, ddb_out, ddk_out, dh_scr):
        @pl.when(pl.program_id(1) == 0)
        def _():
            dh_scr[...] = jnp.zeros_like(dh_scr)
            dal_out[...] = jnp.zeros_like(dal_out)
            ddb_out[...] = jnp.zeros_like(ddb_out)
            ddk_out[...] = jnp.zeros_like(ddk_out)
        _, vjp = jax.vjp(_ssd_chunk, x_ref[...], dr_ref[...], dc_ref[...], b_ref[...], c_ref[...], hs_ref[...],
                         al_ref[...], db_ref[...], dk_ref[...])
        g = vjp((dy_ref[...], dh_scr[...]))
        dx_out[...] = g[0]
        ddr_out[...] = g[1]
        ddc_out[...] = g[2]
        dbm_out[...] = g[3]
        dcm_out[...] = g[4]
        dh_scr[...] = g[5]
        dal_out[...] += g[6]
        ddb_out[...] += g[7]
        ddk_out[...] += g[8]

    return pl.pallas_call(
        body, name=name, grid=(2, nc),
        in_specs=specs + [hv, hv, hv, pl.BlockSpec((None, None, 8, 64, 128), lambda g, c: (g, cix(c), 0, 0, 0)),
                          pl.BlockSpec((8, 64, 128), lambda g, c: (g, 0, cix(c)))],
        out_specs=[pl.BlockSpec((8, 64, 128), lambda g, c: (g, 0, cix(c))), pl.BlockSpec((8, 1, 128), lambda g, c: (g, 0, cix(c))),
                   pl.BlockSpec((8, 128, 1), lambda g, c: (g, cix(c), 0)), pl.BlockSpec((128, 128), lambda g, c: (cix(c), g)),
                   pl.BlockSpec((128, 128), lambda g, c: (cix(c), g)), hv, hv, hv],
        out_shape=[jax.ShapeDtypeStruct((16, 64, s), F32), jax.ShapeDtypeStruct((16, 1, s), F32),
                   jax.ShapeDtypeStruct((16, s, 1), F32), jax.ShapeDtypeStruct((s, 256), F32),
                   jax.ShapeDtypeStruct((s, 256), F32)] + [jax.ShapeDtypeStruct((16, 1, 1), F32)] * 3,
        scratch_shapes=[pltpu.VMEM((8, 64, 128), F32)],
        compiler_params=_params("arbitrary", "arbitrary"),
    )(xt, dtr, dtc, xbc, xbc, alog, dbias, dskip, hs, dyt)


def _merge(oa, ob, y, z, gla, glb, glc, x, g1, nw, ea, eb, ec, eo, wba, wbb, wbc, wout):
    gated = y * (z * _sig(z))
    sq = gated * gated
    left = lax.broadcasted_iota(jnp.int32, (1, D), 1) < 512
    ms0 = jnp.sum(jnp.where(left, sq, 0.0), axis=-1, keepdims=True) / 512.0
    ms1 = jnp.sum(jnp.where(left, 0.0, sq), axis=-1, keepdims=True) / 512.0
    oc = gated * jnp.where(left, lax.rsqrt(ms0 + EPS), lax.rsqrt(ms1 + EPS)) * nw
    ya, yb, yc = mm_nc(oa, wba) + ea, mm_nc(ob, wbb) + eb, mm_nc(oc, wbc) + ec
    merged = _sig(gla) * ya + _sig(glb) * yb + _sig(glc) * yc
    x1 = x + g1 * (mm_nc(merged, wout) + eo)
    return x1, (oc, merged)


def _merge_specs(tr):
    row = lambda w: pl.BlockSpec((tr, w), lambda i: (i, 0))
    acts = [row(D), row(512), pl.BlockSpec((D, tr), lambda i: (0, i)), pl.BlockSpec((tr, D), lambda i: (i, O_Z // D)),
            pl.BlockSpec((tr, 3 * D), lambda i: (i, 0)), row(D), pl.BlockSpec((8, D), lambda i: (0, 0))]
    cst = lambda r: pl.BlockSpec((r, D), lambda i: (0, 0), **CONST)
    return acts, [cst(D), cst(512), cst(D), cst(D)], row


def merge_fwd(oa, ob, y, proj, x, mvec, wba, wbb, wbc, wout, name):
    s = x.shape[0]
    tr = _pick(s, 256)
    acts, wts, row = _merge_specs(tr)

    def body(oa_ref, ob_ref, y_ref, z_ref, gl_ref, x_ref, mv_ref, wba_ref, wbb_ref, wbc_ref, wout_ref, o_ref):
        zero = jnp.zeros((1, D), F32)
        x1, _ = _merge(oa_ref[...], ob_ref[...], y_ref[...].T, z_ref[...], gl_ref[:, 0:D], gl_ref[:, D:2 * D],
                       gl_ref[:, 2 * D:3 * D], x_ref[...], mv_ref[0:1, :], mv_ref[1:2, :], zero, zero, zero, zero,
                       wba_ref[...], wbb_ref[...], wbc_ref[...], wout_ref[...])
        o_ref[...] = x1

    return pl.pallas_call(
        body, name=name, grid=(s // tr,), in_specs=acts + wts, out_specs=row(D),
        out_shape=jax.ShapeDtypeStruct((s, D), F32), compiler_params=_params("arbitrary"),
    )(oa, ob, y, proj, proj, x, mvec, wba, wbb, wbc, wout)


def merge_bwd(oa, ob, y, proj, x, mvec, wba, wbb, wbc, wout, dx1, name):
    s = x.shape[0]
    tr = _pick(s, 128)
    acts, wts, row = _merge_specs(tr)

    def body(oa_ref, ob_ref, y_ref, z_ref, gl_ref, x_ref, mv_ref, wba_ref, wbb_ref, wbc_ref, wout_ref, dx1_ref,
             doa_o, dob_o, dy_o, dz_o, dgl_o, dx_o, dmv_o, dya_o, dyb_o, dyc_o, dpre_o, oc_o, mg_o):
        zero = jnp.zeros((tr, D), F32)
        wts_ = (wba_ref[...], wbb_ref[...], wbc_ref[...], wout_ref[...])
        f = lambda *a: _merge(*a, *wts_)
        _, vjp, (oc, merged) = jax.vjp(
            f, oa_ref[...], ob_ref[...], y_ref[...].T, z_ref[...], gl_ref[:, 0:D], gl_ref[:, D:2 * D],
            gl_ref[:, 2 * D:3 * D], x_ref[...], mv_ref[0:1, :], mv_ref[1:2, :], zero, zero, zero, zero, has_aux=True)
        g = vjp(dx1_ref[...])
        doa_o[...] = g[0]
        dob_o[...] = g[1]
        dy_o[...] = g[2].T
        dz_o[...] = g[3].astype(BF16)
        dgl_o[:, 0:D] = g[4].astype(BF16)
        dgl_o[:, D:2 * D] = g[5].astype(BF16)
        dgl_o[:, 2 * D:3 * D] = g[6].astype(BF16)
        dx_o[...] = g[7]

        @pl.when(pl.program_id(0) == 0)
        def _():
            dmv_o[...] = jnp.zeros_like(dmv_o)

        dmv_o[0:1, :] += g[8]
        dmv_o[1:2, :] += g[9]
        dya_o[...] = g[10].astype(BF16)
        dyb_o[...] = g[11].astype(BF16)
        dyc_o[...] = g[12].astype(BF16)
        dpre_o[...] = g[13].astype(BF16)
        oc_o[...] = oc.astype(BF16)
        mg_o[...] = merged.astype(BF16)

    sd = lambda w, dt: jax.ShapeDtypeStruct((s, w), dt)
    return pl.pallas_call(
        body, name=name, grid=(s // tr,), in_specs=acts + wts + [row(D)],
        out_specs=[row(D), row(512), pl.BlockSpec((D, tr), lambda i: (0, i)), row(D), row(3 * D), row(D),
                   pl.BlockSpec((8, D), lambda i: (0, 0))] + [row(D)] * 6,
        out_shape=[sd(D, F32), sd(512, F32), jax.ShapeDtypeStruct((D, s), F32), sd(D, BF16), sd(3 * D, BF16), sd(D, F32),
                   jax.ShapeDtypeStruct((8, D), F32)] + [sd(D, BF16)] * 6,
        compiler_params=_params("arbitrary"),
    )(oa, ob, y, proj, proj, x, mvec, wba, wbb, wbc, wout, dx1)


def _conv3(u_scr, w_ref, first, rows, lanes):
    return sum(w_ref[k:k + 1, :] * u_scr[first + k:first + k + rows, lanes] for k in range(3))


def _ffn_tile_specs(tf, tile):
    def at(rows, off):
        return pl.BlockSpec((rows, tf), lambda *g: (0, off + tile(*g)))
    return [at(D, 0), at(D, FFN_NT), at(3, 0), at(3, FFN_NT), at(1, 0), at(1, FFN_NT)]


def ffn_fwd(x1, fvec, wup, cw, cb, wdn, name):
    s = x1.shape[0]
    tr, tf = _pick(s, 512), FFN_TILE
    lg, lv = slice(0, tf), slice(tf, 2 * tf)

    def body(x_ref, v_ref, wg_ref, wv_ref, cwg_ref, cwv_ref, cbg_ref, cbv_ref, wd_ref, x2_ref, h_ref, pre_ref,
             h_scr, u_scr, acc):
        i, t = pl.program_id(0), pl.program_id(1)

        @pl.when(t == 0)
        def _():
            @pl.when(i == 0)
            def _():
                h_scr[0:16, :] = jnp.zeros((16, D), BF16)

            @pl.when(i > 0)
            def _():
                h_scr[0:16, :] = h_scr[tr:tr + 16, :]

            h = (_rms(x_ref[...], v_ref[0:1, :], D) * (1.0 + v_ref[2:3, :]) + v_ref[1:2, :]).astype(BF16)
            h_scr[16:16 + tr, :] = h
            h_ref[...] = h
            acc[...] = jnp.zeros_like(acc)

        u_scr[:, lg] = jnp.dot(h_scr[...], wg_ref[...], preferred_element_type=F32)
        u_scr[:, lv] = jnp.dot(h_scr[...], wv_ref[...], preferred_element_type=F32)
        cg = _conv3(u_scr, cwg_ref, 14, tr, lg) + cbg_ref[...]
        cval = _conv3(u_scr, cwv_ref, 14, tr, lv) + cbv_ref[...]
        acc[...] += _raw(cg * _sig(cg) * cval, wd_ref[...], _NN)

        @pl.when(t == FFN_NT - 1)
        def _():
            pre_ref[...] = acc[...]
            x2_ref[...] = x_ref[...] + v_ref[3:4, :] * acc[...]

    row = pl.BlockSpec((tr, D), lambda i, t: (i, 0))
    return pl.pallas_call(
        body, name=name, grid=(s // tr, FFN_NT),
        in_specs=[row, pl.BlockSpec((8, D), lambda i, t: (0, 0))] + _ffn_tile_specs(tf, lambda i, t: t)
                 + [pl.BlockSpec((tf, D), lambda i, t: (t, 0))],
        out_specs=[row, row, row],
        out_shape=[jax.ShapeDtypeStruct((s, D), F32), jax.ShapeDtypeStruct((s, D), BF16), jax.ShapeDtypeStruct((s, D), F32)],
        scratch_shapes=[pltpu.VMEM((tr + 16, D), BF16), pltpu.VMEM((tr + 16, 2 * tf), F32), pltpu.VMEM((tr, D), F32)],
        compiler_params=_params("arbitrary", "arbitrary"),
    )(x1, fvec, wup, wup, cw, cw, cb, cb, wdn)


def ffn_bwd(h2, dx2, fvec, wup, cw, cb, wdn, name):
    s = h2.shape[0]
    tr, tf = _pick(s, 512), FFN_TILE
    ni, nb = s // tr, s // 16
    lg, lv = slice(0, tf), slice(tf, 2 * tf)

    def body(hp_ref, hm_ref, hn_ref, dm_ref, dn_ref, v_ref, wg_ref, wv_ref, cwg_ref, cwv_ref, cbg_ref, cbv_ref, wd_ref,
             dup_ref, act_ref, dcw_ref, u_scr, dc_scr):
        i = pl.program_id(1)
        hfull = jnp.concatenate([jnp.where(i > 0, hp_ref[...], jnp.zeros((16, D), BF16)), hm_ref[...],
                                 jnp.where(i < ni - 1, hn_ref[...], jnp.zeros((16, D), BF16))], axis=0)
        u_scr[:, lg] = jnp.dot(hfull, wg_ref[...], preferred_element_type=F32)
        u_scr[:, lv] = jnp.dot(hfull, wv_ref[...], preferred_element_type=F32)
        cg = _conv3(u_scr, cwg_ref, 14, tr + 16, lg) + cbg_ref[...]
        cval = _conv3(u_scr, cwv_ref, 14, tr + 16, lv) + cbv_ref[...]
        g2 = v_ref[3:4, :]
        dpre = jnp.concatenate([dm_ref[...] * g2, jnp.where(i < ni - 1, dn_ref[...], 0.0) * g2], axis=0)
        dact = _raw(dpre, wd_ref[...], _NT)
        sg = _sig(cg)
        sl = cg * sg
        dc_scr[:, lg] = dact * cval * (sg * (1.0 + cg * (1.0 - sg)))
        dc_scr[:, lv] = dact * sl
        act_ref[...] = (sl * cval)[0:tr, :].astype(BF16)

        @pl.when(i == 0)
        def _():
            dcw_ref[...] = jnp.zeros_like(dcw_ref)

        for half, lanes, cw_ref in ((0, lg, cwg_ref), (1, lv, cwv_ref)):
            dup_ref[half] = sum(cw_ref[k:k + 1, :] * dc_scr[2 - k:2 - k + tr, lanes] for k in range(3)).astype(BF16)
            dcm = dc_scr[0:tr, lanes]
            for k in range(3):
                dcw_ref[half, k:k + 1, :] += jnp.sum(dcm * u_scr[14 + k:14 + k + tr, lanes], axis=0, keepdims=True)
            dcw_ref[half, 3:4, :] += jnp.sum(dcm, axis=0, keepdims=True)

    r16 = tr // 16
    prev = lambda t, i: (jnp.maximum(i * r16 - 1, 0), 0)
    nxt = lambda t, i: (jnp.minimum((i + 1) * r16, nb - 1), 0)
    main = lambda t, i: (i, 0)
    return pl.pallas_call(
        body, name=name, grid=(FFN_NT, ni),
        in_specs=[pl.BlockSpec((16, D), prev), pl.BlockSpec((tr, D), main), pl.BlockSpec((16, D), nxt),
                  pl.BlockSpec((tr, D), main), pl.BlockSpec((16, D), nxt), pl.BlockSpec((8, D), lambda t, i: (0, 0))]
                 + _ffn_tile_specs(tf, lambda t, i: t) + [pl.BlockSpec((tf, D), lambda t, i: (t, 0))],
        out_specs=[pl.BlockSpec((2, tr, tf), lambda t, i: (0, i, t)), pl.BlockSpec((tr, tf), lambda t, i: (i, t)),
                   pl.BlockSpec((2, 8, tf), lambda t, i: (0, 0, t))],
        out_shape=[jax.ShapeDtypeStruct((2, s, FFN), BF16), jax.ShapeDtypeStruct((s, FFN), BF16),
                   jax.ShapeDtypeStruct((2, 8, FFN), F32)],
        scratch_shapes=[pltpu.VMEM((tr + 32, 2 * tf), F32), pltpu.VMEM((tr + 16, 2 * tf), F32)],
        compiler_params=_params("arbitrary", "arbitrary"),
    )(h2, h2, h2, dx2, dx2, fvec, wup, wup, cw, cw, cb, cb, wdn)


def loss_head(y, target):
    s = y.shape[0]
    tr = _pick(s, 512)

    def body(y_ref, t_ref, dx_ref, l_ref):
        @pl.when(pl.program_id(0) == 0)
        def _():
            l_ref[...] = jnp.zeros_like(l_ref)
        err = y_ref[...] - t_ref[...]
        dx_ref[...] = err / float(D)
        l_ref[...] += 0.5 * jnp.sum(jnp.sum(err * err, axis=-1, keepdims=True) / float(D), axis=0, keepdims=True)

    row = pl.BlockSpec((tr, D), lambda i: (i, 0))
    return pl.pallas_call(
        body, name="loss_head", grid=(s // tr,), in_specs=[row, row],
        out_specs=[row, pl.BlockSpec((8, 128), lambda i: (0, 0))],
        out_shape=[jax.ShapeDtypeStruct((s, D), F32), jax.ShapeDtypeStruct((8, 128), F32)],
        compiler_params=_params("arbitrary"),
    )(y, target)


def adamw(parts, w, m, v, name, tok=None):
    nseg = len(parts)
    p, r, c = parts[0].shape
    tr = _pick(r, 256, 8)
    ni = r // tr
    tok = jnp.zeros((8, 128), F32) if tok is None else tok

    def body(*refs):
        p_refs = refs[:nseg]
        w_ref, m_ref, v_ref, _, g_out, d_out, m_out, v_out, g_scr = refs[nseg:]
        for q in range(nseg):
            @pl.when(pl.program_id(0) == q)
            def _(q=q):
                g = p_refs[q][0].astype(F32)
                for j in range(1, p):
                    g = g + p_refs[q][j].astype(F32)
                g_scr[...] = g
        g = g_scr[...]
        mn = B1 * m_ref[...] + (1.0 - B1) * g
        vn = B2 * v_ref[...] + (1.0 - B2) * (g * g)
        m_hat = mn / (1.0 - B1 ** STEP)
        v_hat = vn / (1.0 - B2 ** STEP)
        g_out[...] = g
        d_out[...] = -LR * (m_hat / (jnp.sqrt(v_hat) + ADAM_EPS) + WD * w_ref[...])
        m_out[...] = mn
        v_out[...] = vn

    row = pl.BlockSpec((tr, c), lambda l, i: (l * ni + i, 0))
    part = lambda q: pl.BlockSpec((p, tr, c), lambda l, i: (0, jnp.clip((l - q) * ni + i, 0, ni - 1), 0))
    return pl.pallas_call(
        body, name=name, grid=(nseg, ni),
        in_specs=[part(q) for q in range(nseg)] + [row, row, row, pl.BlockSpec((8, 128), lambda l, i: (0, 0))],
        out_specs=[row] * 4, out_shape=[jax.ShapeDtypeStruct((nseg * r, c), F32)] * 4,
        scratch_shapes=[pltpu.VMEM((tr, c), F32)],
        compiler_params=_params("arbitrary", "arbitrary"),
    )(*parts, w, m, v, tok)


def _padc(a, n):
    return jnp.pad(a, [(0, 0)] * (a.ndim - 1) + [(0, n - a.shape[-1])])


def _swap16(a):
    return jnp.concatenate([a[..., 16:32], a[..., 0:16]], axis=-1)


def _shard_cols(g8, a, b):
    c = g8.shape[2]
    return [g8[j][:, max(a, j * c) - j * c:min(b, (j + 1) * c) - j * c] for j in range(a // c, (b - 1) // c + 1)]


def _win_layout(g8):
    cols = lambda a, b: _shard_cols(g8, a, b)
    kr = jnp.concatenate(cols(640, 672), axis=1)
    dt = jnp.concatenate(cols(3744, 3760), axis=1)
    return jnp.concatenate(cols(3760, 6832) + cols(2208, 3232) + cols(1184, 2208) + cols(672, 1184) + cols(3232, 3744)
                           + cols(384, 640) + [_padc(kr, 128), _padc(_swap16(kr), 128), _padc(dt, 128),
                                               jnp.zeros((g8.shape[1], 128), g8.dtype)] + cols(0, 384), axis=1)


def _win_grad_shards(g):
    kr = (g[:, O_KR:O_KR + 32].astype(F32) + _swap16(g[:, O_KRS:O_KRS + 32].astype(F32))).astype(g.dtype)
    segs = [(g, O_QL, 384), (g, O_CKV, 256), (kr, 0, 32), (g, O_PU, 512), (g, O_Z, D), (g, O_XS, D), (g, O_BC, 512),
            (g, O_DT, 16), (g, O_G, 3 * D)]
    shards, width = [], sum(w for _, _, w in segs) // NDEV
    for j in range(NDEV):
        a, b, off, pieces = width * j, width * (j + 1), 0, []
        for arr, lo, w in segs:
            s0, s1 = max(a, off), min(b, off + w)
            if s0 < s1:
                pieces.append(arr[:, lo + s0 - off:lo + s1 - off])
            off += w
        shards.append(jnp.concatenate(pieces, axis=1))
    return jnp.stack(shards).astype(BF16)


def _wq_layout(w):
    w = w.reshape(384, HEADS, 96).transpose(1, 0, 2)
    rope = w[:, :, 64:96]
    return jnp.concatenate([_padc(w[:, :, 0:64], 128), _padc(rope, 128), _padc(_swap16(rope), 128)], axis=2)


def _wq_unlayout(g):
    rope = g[:, :, 128:160] + _swap16(g[:, :, 256:288])
    return jnp.concatenate([g[:, :, 0:64], rope], axis=2).transpose(1, 0, 2).reshape(384, HEADS * 96)


def _wkv_layout(w):
    w = w.reshape(256, HEADS, 128).transpose(1, 0, 2)
    return jnp.concatenate([_padc(w[:, :, 0:64], 128), _padc(w[:, :, 64:128], 128)], axis=2)


def _wkv_unlayout(g):
    return jnp.concatenate([g[:, :, 0:64], g[:, :, 128:192]], axis=2).transpose(1, 0, 2).reshape(256, HEADS * 128)


def _wba_layout(w):
    return jnp.pad(w.reshape(HEADS, 64, D), ((0, 0), (0, 64), (0, 0))).reshape(HEADS * 128, D)


def _rows8(rows, width):
    out = jnp.stack([_padc(r.astype(F32), width) for r in rows])
    return jnp.pad(out, ((0, 8 - out.shape[0]), (0, 0)))


def _mla_vec(qa, kva, qn, kn):
    def row(n):
        return jnp.concatenate([_padc(n[0:64], 128), _padc(n[64:96], 128), _padc(_swap16(n[64:96]), 128)])
    return _rows8([qa, kva, row(qn), row(kn)], 512)


def _mla_unvec(g):
    def un(r):
        return jnp.concatenate([r[0:64], r[128:160] + _swap16(r[256:288])])
    return g[0, 0:384], g[1, 0:256], un(g[2]), un(g[3])


SMALL = (("ada_b", (6 * D,)), ("norm1_w", (D,)), ("q_a_norm", (384,)), ("kv_a_norm", (256,)), ("q_norm", (96,)),
         ("k_norm", (96,)), ("pool_w", (4, 128, 128)), ("pool_scale", (512,)), ("ssd_conv_b", (1536,)),
         ("ssd_dt_bias", (16,)), ("ssd_a_log", (16,)), ("ssd_d", (16,)), ("ssd_norm_w", (D,)), ("norm2_w", (D,)),
         ("ffn_conv_b", (2 * FFN,)), ("ssd_conv_w", (4, 1536)), ("ffn_conv_w", (3, 2 * FFN)))
SMALL_REPL = SMALL[:15]
SMALL_ROWS = 208


def _pack(per_layer, names):
    flat = jnp.concatenate([per_layer[l][n].reshape(-1).astype(F32) for n, _ in names for l in range(LAYERS)])
    return jnp.pad(flat, (0, SMALL_ROWS * D - flat.shape[0])).reshape(SMALL_ROWS, D)


def _unpack(packed, names):
    flat, out, off = packed.reshape(-1), {}, 0
    for n, shp in names:
        size = LAYERS * math.prod(shp)
        out[n] = flat[off:off + size].reshape((LAYERS,) + shp)
        off += size
    return out


GROUP_A = ("w_in", "w_q_b", "w_kv_b")
GROUP_B = ("w_branch", "w_out", "ffn_up", "ffn_down")
BIG = GROUP_A + GROUP_B
COL_SHARDED = ("w_in", "w_q_b", "w_kv_b", "ffn_up")


def _behind(arrs, tok):
    arrs = list(arrs)
    j = min(range(len(arrs)), key=lambda q: arrs[q].size)
    arrs[j] = arrs[j] + tok[0, 0].astype(arrs[j].dtype)
    return arrs


def _gathered_full(g, name):
    if name in COL_SHARDED:
        return g.transpose(1, 0, 2).reshape(g.shape[1], NDEV * g.shape[2])
    return g.reshape(NDEV * g.shape[1], g.shape[2])


def _to_shards(full, name):
    if name == "w_in":
        return _win_grad_shards(full)
    if name in COL_SHARDED:
        r, c = full.shape
        return full.reshape(r, NDEV, c // NDEV).transpose(1, 0, 2).astype(BF16)
    r, c = full.shape
    return full.reshape(NDEV, r // NDEV, c).astype(BF16)


def _fwd_a(x, lw, mod, cos2, sin2, l, tok):
    sh1, sc1, g1, sh2, sc2, g2 = [mod[j * D:(j + 1) * D] for j in range(6)]
    vec1 = _rows8([lw["norm1_w"], sh1, sc1], D) + tok[0, 0]
    proj, h1, dt_cols = norm_proj_fwd(x, vec1, lw["win"], f"inproj_fwd{l}")
    q, k, v = mla_pre_fwd(proj, lw["wq"], lw["wkv"], lw["mla_vec"], cos2, sin2, f"mla_pre_fwd{l}")
    oa = mla_attn_fwd(q, k, v, f"mla_attn_fwd{l}")
    ob = pool_fwd(proj, lw["pool_w"], lw["pool_scale"].reshape(1, 512), f"pool_fwd{l}")
    xbc, xt = conv_fwd(proj, lw["ssd_conv_w"], lw["ssd_conv_b"].reshape(1, 1536), f"conv_fwd{l}")
    s = x.shape[0]
    xt = xt.reshape(16, 64, s)
    dt = dt_cols[:, 0:16].T
    dtr, dtc = dt[:, None, :], dt[:, :, None]
    hv = lambda a: a.reshape(16, 1, 1)
    yt, hs = ssd_fwd(xt, dtr, dtc, xbc, hv(lw["ssd_a_log"]), hv(lw["ssd_dt_bias"]), hv(lw["ssd_d"]), f"ssd_fwd{l}")
    return dict(x=x, vec1=vec1, proj=proj, h1=h1, q=q, k=k, v=v, oa=oa, ob=ob, xbc=xbc, xt=xt, dtr=dtr, dtc=dtc,
                hs=hs, yt=yt.reshape(D, s), mvec=_rows8([g1, lw["ssd_norm_w"]], D),
                fvec=_rows8([lw["norm2_w"], sh2, sc2, g2], D))


def _fwd_b(sv, lw, l, tok):
    sv["mvec"] = sv["mvec"] + tok[0, 0]
    x1 = merge_fwd(sv["oa"], sv["ob"], sv["yt"], sv["proj"], sv["x"], sv["mvec"], lw["wba"], lw["wbb"], lw["wbc"],
                   lw["wout"], f"merge_fwd{l}")
    x2, h2, pre = ffn_fwd(x1, sv["fvec"], lw["wup"], lw["ffn_conv_w"], lw["ffn_conv_b"].reshape(1, 2 * FFN), lw["wdn"],
                          f"ffn_fwd{l}")
    sv.update(x1=x1, h2=h2, pre=pre)
    return x2


def _bwd_b(dx2, lw, sv, l, tok):
    grads, small = {}, {}
    fvec = sv["fvec"] + tok[0, 0]
    dup, act, dcw = ffn_bwd(sv["h2"], dx2, fvec, lw["wup"], lw["ffn_conv_w"], lw["ffn_conv_b"].reshape(1, 2 * FFN),
                            lw["wdn"], f"ffn_bwd{l}")
    grads["ffn_down"] = tn_matmul(act, dx2, f"dw_down{l}", scale=fvec[3:4])
    grads["ffn_up"] = tn_matmul(sv["h2"], dup, f"dw_up{l}")
    dx1, dfvec = norm_proj_bwd(sv["x1"], fvec, dup, lw["wup"], dx2, sv["pre"], f"ffn_norm_bwd{l}")
    small["ffn_conv_w"] = jnp.concatenate([dcw[0, 0:3], dcw[1, 0:3]], axis=1)
    small["ffn_conv_b"] = jnp.concatenate([dcw[0, 3], dcw[1, 3]])
    small["norm2_w"] = dfvec[0]
    (doa, dob, dyt, dz, dgl, dx, dmvec, dya, dyb, dyc, dpre, oc, merged) = merge_bwd(
        sv["oa"], sv["ob"], sv["yt"], sv["proj"], sv["x"], sv["mvec"], lw["wba"], lw["wbb"], lw["wbc"], lw["wout"], dx1,
        f"merge_bwd{l}")
    dwba = tn_matmul(sv["oa"], dya, f"dw_ba{l}").reshape(HEADS, 128, D)[:, 0:64].reshape(512, D)
    grads["w_branch"] = jnp.concatenate([dwba, tn_matmul(sv["ob"], dyb, f"dw_bb{l}"), tn_matmul(oc, dyc, f"dw_bc{l}")])
    grads["w_out"] = tn_matmul(merged, dpre, f"dw_out{l}")
    small["ssd_norm_w"] = dmvec[1]
    small["dmod_b"] = (dmvec[0], dfvec[1], dfvec[2], dfvec[3])
    return dx, dict(doa=doa, dob=dob, dyt=dyt, dz=dz, dgl=dgl), grads, small


def _bwd_a(dx, cot, lw, sv, cos2, sin2, l, tok, small):
    s = dx.shape[0]
    grads = {}
    doa, dob, dz, dgl = cot["doa"], cot["dob"], cot["dz"], cot["dgl"]
    hv = lambda a: a.reshape(16, 1, 1)
    dxt, ddtr, ddtc, dbm, dcm, dal, ddb, ddk = ssd_bwd(
        sv["xt"], sv["dtr"], sv["dtc"], sv["xbc"], hv(lw["ssd_a_log"]) + tok[0, 0], hv(lw["ssd_dt_bias"]),
        hv(lw["ssd_d"]), sv["hs"], cot["dyt"].reshape(16, 64, s), f"ssd_bwd{l}")
    small["ssd_a_log"], small["ssd_dt_bias"], small["ssd_d"] = dal.reshape(16), ddb.reshape(16), ddk.reshape(16)
    dxbc, dscw, dscb = conv_bwd(sv["proj"], lw["ssd_conv_w"], lw["ssd_conv_b"].reshape(1, 1536), dxt.reshape(D, s),
                                dbm, dcm, f"conv_bwd{l}")
    small["ssd_conv_w"], small["ssd_conv_b"] = dscw, dscb.reshape(1536)
    ddt = (ddtr[:, 0, :] + ddtc[:, :, 0]).T
    du, dpw, dps = pool_bwd(sv["proj"], lw["pool_w"], lw["pool_scale"].reshape(1, 512), dob, f"pool_bwd{l}")
    small["pool_w"], small["pool_scale"] = dpw, dps.reshape(512)
    dq, dk, dv = mla_attn_bwd(sv["q"], sv["k"], sv["v"], doa, f"mla_attn_bwd{l}")
    dql, dckv, dkr, dkrs, dwq, dwkv, dmv = mla_pre_bwd(sv["proj"], lw["wq"], lw["wkv"], lw["mla_vec"], cos2, sin2,
                                                       dq, dk, dv, f"mla_pre_bwd{l}")
    grads["w_q_b"], grads["w_kv_b"] = _wq_unlayout(dwq), _wkv_unlayout(dwkv)
    small["q_a_norm"], small["kv_a_norm"], small["q_norm"], small["k_norm"] = _mla_unvec(dmv)
    dproj = jnp.concatenate([dgl, dxbc[:, 0:D], dz, du, dxbc[:, D:1536], dckv, dkr, dkrs,
                             _padc(ddt, 128).astype(BF16), jnp.zeros((s, 128), BF16), dql], axis=1)
    grads["w_in"] = tn_matmul(sv["h1"], dproj, f"dw_in{l}")
    dx0, dvec1 = norm_proj_bwd(sv["x"], sv["vec1"], dproj, lw["win"], dx, None, f"inproj_bwd{l}")
    small["norm1_w"] = dvec1[0]
    small["ada_b"] = jnp.concatenate([dvec1[1], dvec1[2], *small.pop("dmod_b")])
    return dx0, grads, small


def kernel(x, c, positions, ada_w, ada_b, norm1_w, w_in, q_a_norm, w_q_b, kv_a_norm, w_kv_b, q_norm, k_norm, pool_w, pool_scale, ssd_conv_w, ssd_conv_b, ssd_dt_bias, ssd_a_log, ssd_d, ssd_norm_w, w_branch, w_out, norm2_w, ffn_up, ffn_conv_w, ffn_conv_b, ffn_down, loss_target, m_ada_w, m_ada_b, m_norm1_w, m_w_in, m_q_a_norm, m_w_q_b, m_kv_a_norm, m_w_kv_b, m_q_norm, m_k_norm, m_pool_w, m_pool_scale, m_ssd_conv_w, m_ssd_conv_b, m_ssd_dt_bias, m_ssd_a_log, m_ssd_d, m_ssd_norm_w, m_w_branch, m_w_out, m_norm2_w, m_ffn_up, m_ffn_conv_w, m_ffn_conv_b, m_ffn_down, v_ada_w, v_ada_b, v_norm1_w, v_w_in, v_q_a_norm, v_w_q_b, v_kv_a_norm, v_w_kv_b, v_q_norm, v_k_norm, v_pool_w, v_pool_scale, v_ssd_conv_w, v_ssd_conv_b, v_ssd_dt_bias, v_ssd_a_log, v_ssd_d, v_ssd_norm_w, v_w_branch, v_w_out, v_norm2_w, v_ffn_up, v_ffn_conv_w, v_ffn_conv_b, v_ffn_down):
    p = dict(ada_w=ada_w, ada_b=ada_b, norm1_w=norm1_w, w_in=w_in, q_a_norm=q_a_norm, w_q_b=w_q_b, kv_a_norm=kv_a_norm,
             w_kv_b=w_kv_b, q_norm=q_norm, k_norm=k_norm, pool_w=pool_w, pool_scale=pool_scale, ssd_conv_w=ssd_conv_w,
             ssd_conv_b=ssd_conv_b, ssd_dt_bias=ssd_dt_bias, ssd_a_log=ssd_a_log, ssd_d=ssd_d, ssd_norm_w=ssd_norm_w,
             w_branch=w_branch, w_out=w_out, norm2_w=norm2_w, ffn_up=ffn_up, ffn_conv_w=ffn_conv_w, ffn_conv_b=ffn_conv_b,
             ffn_down=ffn_down)
    mom = dict(ada_w=m_ada_w, ada_b=m_ada_b, norm1_w=m_norm1_w, w_in=m_w_in, q_a_norm=m_q_a_norm, w_q_b=m_w_q_b,
               kv_a_norm=m_kv_a_norm, w_kv_b=m_w_kv_b, q_norm=m_q_norm, k_norm=m_k_norm, pool_w=m_pool_w,
               pool_scale=m_pool_scale, ssd_conv_w=m_ssd_conv_w, ssd_conv_b=m_ssd_conv_b, ssd_dt_bias=m_ssd_dt_bias,
               ssd_a_log=m_ssd_a_log, ssd_d=m_ssd_d, ssd_norm_w=m_ssd_norm_w, w_branch=m_w_branch, w_out=m_w_out,
               norm2_w=m_norm2_w, ffn_up=m_ffn_up, ffn_conv_w=m_ffn_conv_w, ffn_conv_b=m_ffn_conv_b, ffn_down=m_ffn_down)
    var = dict(ada_w=v_ada_w, ada_b=v_ada_b, norm1_w=v_norm1_w, w_in=v_w_in, q_a_norm=v_q_a_norm, w_q_b=v_w_q_b,
               kv_a_norm=v_kv_a_norm, w_kv_b=v_w_kv_b, q_norm=v_q_norm, k_norm=v_k_norm, pool_w=v_pool_w,
               pool_scale=v_pool_scale, ssd_conv_w=v_ssd_conv_w, ssd_conv_b=v_ssd_conv_b, ssd_dt_bias=v_ssd_dt_bias,
               ssd_a_log=v_ssd_a_log, ssd_d=v_ssd_d, ssd_norm_w=v_ssd_norm_w, w_branch=v_w_branch, w_out=v_w_out,
               norm2_w=v_norm2_w, ffn_up=v_ffn_up, ffn_conv_w=v_ffn_conv_w, ffn_conv_b=v_ffn_conv_b, ffn_down=v_ffn_down)
    names = list(p)
    me = 4 * lax.axis_index("x") + 2 * lax.axis_index("y") + lax.axis_index("c")
    xs, tgt = x[0], loss_target[0]
    s = xs.shape[0]

    inv_freq = ROPE_THETA ** (-jnp.arange(0, 32, 2, dtype=F32) / 32.0)
    ang = positions[0].astype(F32)[:, None] * inv_freq
    cos, sin = jnp.cos(ang), jnp.sin(ang)
    cos2 = _padc(jnp.concatenate([cos, cos], axis=1), 128)
    sin2 = _padc(jnp.concatenate([-sin, sin], axis=1), 128)

    conv_shards = jnp.concatenate([ssd_conv_w.reshape(-1), ffn_conv_w.reshape(-1)])
    (c_all, conv_all), _ = all_to_all([c, conv_shards], [True, True], "gather_c")
    modp, cact = ada_mod(jnp.pad(c_all.reshape(NDEV, D), ((0, 8), (0, 0))), ada_w)
    (mod_in,), tok = all_to_all([modp[:, 0:NDEV].transpose(1, 0, 2)], [False], "scatter_mod")
    mod = mod_in.transpose(1, 0, 2).reshape(LAYERS, 6 * D) + ada_b

    n1 = LAYERS * 4 * 192
    scw = conv_all[:, :n1].reshape(NDEV, LAYERS, 4, 192).transpose(1, 2, 0, 3).reshape(LAYERS, 4, 1536)
    fcw = conv_all[:, n1:].reshape(NDEV, LAYERS, 3, 704).transpose(1, 2, 0, 3).reshape(LAYERS, 3, 2 * FFN)

    def weights_a(gathered, l):
        full = {n: _gathered_full(g, n) for n, g in zip(GROUP_A[1:], gathered[1:])}
        lw = {n: p[n][l] for n in names}
        lw.update(win=_win_layout(gathered[0]), wq=_wq_layout(full["w_q_b"]), wkv=_wkv_layout(full["w_kv_b"]),
                  ssd_conv_w=scw[l], ffn_conv_w=fcw[l],
                  mla_vec=_mla_vec(lw["q_a_norm"], lw["kv_a_norm"], lw["q_norm"], lw["k_norm"]))
        return lw

    def weights_b(gathered):
        full = {n: _gathered_full(g, n) for n, g in zip(GROUP_B, gathered)}
        wb = full["w_branch"]
        return dict(wba=_wba_layout(wb[0:512]), wbb=wb[512:1024], wbc=wb[1024:2048], wout=full["w_out"],
                    wup=full["ffn_up"], wdn=full["ffn_down"])

    shards = lambda group, l: [p[n][l].astype(BF16) for n in group]
    bc = lambda group: [True] * len(group)
    lws, saved = [None] * LAYERS, [None] * LAYERS
    st, tok = exchange_start(_behind(shards(GROUP_A, 0), tok), bc(GROUP_A), "gather_a0_start")
    got, tok = exchange_wait(st, tok, "gather_a0_wait")
    h = xs
    for l in range(LAYERS):
        st, tok = exchange_start(_behind(shards(GROUP_B, l), tok), bc(GROUP_B), f"gather_b{l}_start")
        lws[l] = weights_a(got, l)
        saved[l] = _fwd_a(h, lws[l], mod[l], cos2, sin2, l, tok)
        got, tok = exchange_wait(st, saved[l]["yt"], f"gather_b{l}_wait")
        lws[l].update(weights_b(got))
        if l + 1 < LAYERS:
            st, tok = exchange_start(_behind(shards(GROUP_A, l + 1), tok), bc(GROUP_A), f"gather_a{l + 1}_start")
        h = _fwd_b(saved[l], lws[l], l, tok)
        if l + 1 < LAYERS:
            got, tok = exchange_wait(st, h, f"gather_a{l + 1}_wait")
    dx, lpart = loss_head(h, tgt)
    loss = lax.psum(lpart[0, 0], ("x", "y", "c"))
    tok = tok + loss * 0.0

    grads, small, parts = [None] * LAYERS, [None] * LAYERS, {}
    to_shards = lambda g, group: [_to_shards(g[n], n) for n in group]
    nb = lambda group: [False] * len(group)
    st = None
    for l in reversed(range(LAYERS)):
        dx, cot, gb, small[l] = _bwd_b(dx, lws[l], saved[l], l, tok)
        if st is not None:
            parts[("a", l + 1)], tok = exchange_wait(st, dx, f"scatter_a{l + 1}_wait")
        st, tok = exchange_start(_behind(to_shards(gb, GROUP_B), tok), nb(GROUP_B), f"scatter_b{l}_start")
        dx, ga, small[l] = _bwd_a(dx, cot, lws[l], saved[l], cos2, sin2, l, tok, small[l])
        parts[("b", l)], tok = exchange_wait(st, dx, f"scatter_b{l}_wait")
        arrs, flags = to_shards(ga, GROUP_A), nb(GROUP_A)
        if l == 0:
            dmod = jnp.stack([small[q]["ada_b"] for q in range(LAYERS)])
            arrs += [_pack(small, SMALL), dmod.reshape(LAYERS, NDEV, 768).transpose(1, 0, 2)]
            flags += [True, False]
        st, tok = exchange_start(_behind(arrs, tok), flags, f"scatter_a{l}_start")

    out = {}

    def big_adamw(group, tok):
        res = None
        for n in group:
            grp, idx = ("a", GROUP_A.index(n)) if n in GROUP_A else ("b", GROUP_B.index(n))
            shp = p[n].shape
            flat = lambda a: a.reshape(shp[0] * shp[1], shp[2])
            res = adamw([parts[(grp, 0)][idx], parts[(grp, 1)][idx]], flat(p[n]), flat(mom[n]), flat(var[n]),
                        f"adamw_{n}", tok)
            out[n] = [r.reshape(shp) for r in res]
        return res[0]

    g_last = big_adamw(GROUP_B, tok)
    got, _ = exchange_wait(st, g_last, "scatter_a0_wait")
    parts[("a", 0)], small_all, dmod_in = got[0:3], got[3], got[4]
    big_adamw(GROUP_A, None)

    dmod16 = jnp.pad(dmod_in, ((0, 8), (0, 0), (0, 0)))
    g_ada = jnp.stack([tn_matmul(cact, dmod16[:, l], f"dw_ada{l}", out_dtype=F32) for l in range(LAYERS)])
    flat = lambda a: a.reshape(LAYERS * D, 768)
    out["ada_w"] = [r.reshape(ada_w.shape) for r in
                    adamw([flat(g_ada)[None]], flat(ada_w), flat(m_ada_w), flat(v_ada_w), "adamw_ada_w")]

    zeros = jnp.zeros((SMALL_ROWS, D), F32)
    g_small = _unpack(adamw([small_all], zeros, zeros, zeros, "sum_small")[0], SMALL)
    per = lambda d, nm: [{n: d[n][l] for n, _ in nm} for l in range(LAYERS)]
    res = adamw([_pack(per(g_small, SMALL_REPL), SMALL_REPL)[None]], _pack(per(p, SMALL_REPL), SMALL_REPL),
                _pack(per(mom, SMALL_REPL), SMALL_REPL), _pack(per(var, SMALL_REPL), SMALL_REPL), "adamw_small")
    res = [_unpack(r, SMALL_REPL) for r in res]
    for n, _ in SMALL_REPL:
        out[n] = [r[n] for r in res]
    for n, k, w in (("ssd_conv_w", 4, 192), ("ffn_conv_w", 3, 704)):
        g_mine = lax.dynamic_slice(g_small[n], (0, 0, me * w), (LAYERS, k, w))
        f2 = lambda a: jnp.pad(a.reshape(LAYERS * k, w), ((0, 8 - LAYERS * k), (0, 0)))
        res = adamw([f2(g_mine)[None]], f2(p[n]), f2(mom[n]), f2(var[n]), f"adamw_{n}")
        out[n] = [r[0:LAYERS * k].reshape(LAYERS, k, w) for r in res]

    outs = [loss, dx[None]]
    for q in range(4):
        outs += [out[n][q] for n in names]
    return tuple(outs)
```

```python
import functools
import math

import jax
import jax.numpy as jnp
from jax import lax
from jax.experimental import pallas as pl
from jax.experimental.pallas import tpu as pltpu

F32, BF16 = jnp.float32, jnp.bfloat16
EPS = 1e-6
D = 1024
NDEV = 8
LAYERS = 2
HEADS = 8
FFN = 2816
FFN_TILE = 1408
FFN_NT = FFN // FFN_TILE
ATT_SCALE = 96 ** -0.5
ROPE_THETA = 10000.0
LR, B1, B2, ADAM_EPS, WD, STEP = 0.001, 0.9, 0.999, 1e-08, 0.01, 10

O_G, O_XS, O_Z, O_PU, O_BC, O_CKV, O_KR, O_KRS, O_DT, O_QL = 0, 3072, 4096, 5120, 5632, 6144, 6400, 6528, 6656, 6912
NPROJ = 7296
CONST = dict(pipeline_mode=pl.Buffered(1))


def _pick(n, cap, mult=128):
    if n <= cap:
        return n
    best = None
    for t in range(mult, cap + 1, mult):
        if n % t == 0:
            best = t
    assert best is not None, (n, cap, mult)
    return best


def _sig(x):
    return 1.0 / (1.0 + jnp.exp(-x))


def _rms(x, w, n):
    return x * lax.rsqrt(jnp.sum(x * x, axis=-1, keepdims=True) / n + EPS) * w


def _raw(a, b, dims):
    return lax.dot_general(a.astype(BF16), b.astype(BF16), dims, preferred_element_type=F32)


_NN = (((1,), (0,)), ((), ()))
_NT = (((1,), (1,)), ((), ()))
_TN = (((0,), (0,)), ((), ()))
_BNN = (((2,), (1,)), ((0,), (0,)))
_BNT = (((2,), (2,)), ((0,), (0,)))
_BTN = (((1,), (1,)), ((0,), (0,)))


@jax.custom_vjp
def mm_nn(a, b):
    return _raw(a, b, _NN)


mm_nn.defvjp(lambda a, b: (_raw(a, b, _NN), (a, b)),
             lambda r, g: (_raw(g, r[1], _NT), _raw(r[0], g, _TN)))


@jax.custom_vjp
def mm_nc(a, b):
    return _raw(a, b, _NN)


mm_nc.defvjp(lambda a, b: (_raw(a, b, _NN), b),
             lambda b, g: (_raw(g, b, _NT), jnp.zeros_like(b)))


@jax.custom_vjp
def mm_nt(a, b):
    return _raw(a, b, _NT)


mm_nt.defvjp(lambda a, b: (_raw(a, b, _NT), (a, b)),
             lambda r, g: (_raw(g, r[1], _NN), _raw(g, r[0], _TN)))


@jax.custom_vjp
def bmm_nn(a, b):
    return _raw(a, b, _BNN)


bmm_nn.defvjp(lambda a, b: (_raw(a, b, _BNN), (a, b)),
              lambda r, g: (_raw(g, r[1], _BNT), _raw(r[0], g, _BTN)))


@jax.custom_vjp
def bmm_nt(a, b):
    return _raw(a, b, _BNT)


bmm_nt.defvjp(lambda a, b: (_raw(a, b, _BNT), (a, b)),
              lambda r, g: (_raw(g, r[1], _BNN), _raw(g, r[0], _BTN)))


@jax.custom_vjp
def softplus(x):
    t = jnp.exp(-jnp.abs(x))
    u = 1.0 + t
    one = u == 1.0
    l1p = jnp.where(one, t, jnp.log(u) * (t / jnp.where(one, 1.0, u - 1.0)))
    return jnp.maximum(x, 0.0) + l1p


softplus.defvjp(lambda x: (softplus(x), x), lambda x, g: (g * _sig(x),))


def _params(*sem):
    return pltpu.CompilerParams(dimension_semantics=sem, vmem_limit_bytes=56 * 1024 * 1024)


def all_to_all(arrs, bcast, name):
    n = len(arrs)
    out_shapes = [jax.ShapeDtypeStruct(((NDEV,) + a.shape) if b else a.shape, a.dtype) for a, b in zip(arrs, bcast)]

    def body(*refs):
        ins, outs, token = refs[:n], refs[n:2 * n], refs[2 * n]
        send_sems, recv_sems, local_sems = refs[2 * n + 1:]
        me, remote = _exchange_copies(ins, outs, bcast, send_sems, recv_sems)
        local = [pltpu.make_async_copy(ins[j] if bcast[j] else ins[j].at[me], outs[j].at[me], local_sems.at[j])
                 for j in range(n)]
        for cp in local + remote:
            cp.start()
        for cp in remote + local:
            cp.wait()
        token[...] = jnp.zeros_like(token)

    any_spec = pl.BlockSpec(memory_space=pl.ANY)
    res = pl.pallas_call(
        body, name=name, out_shape=out_shapes + [jax.ShapeDtypeStruct((8, 128), F32)], in_specs=[any_spec] * n,
        out_specs=[any_spec] * n + [pl.BlockSpec(memory_space=pltpu.VMEM)],
        scratch_shapes=[pltpu.SemaphoreType.DMA((7 * n,)), pltpu.SemaphoreType.DMA((7 * n,)),
                        pltpu.SemaphoreType.DMA((n,))],
        compiler_params=pltpu.CompilerParams(has_side_effects=True),
    )(*arrs)
    return res[:n], res[n]


def _peers():
    x, y, c = lax.axis_index("x"), lax.axis_index("y"), lax.axis_index("c")
    out = []
    for k in range(1, NDEV):
        px, py, pc = x ^ ((k >> 2) & 1), y ^ ((k >> 1) & 1), c ^ (k & 1)
        out.append(((px, py, pc), 4 * px + 2 * py + pc))
    return 4 * x + 2 * y + c, out


COPIES = {"all": 7, "chips": 3, "pass": 4}


def _exchange_copies(ins, lands, bcast, send_sems, recv_sems, mode="all"):
    x, y, c = lax.axis_index("x"), lax.axis_index("y"), lax.axis_index("c")
    me = 4 * x + 2 * y + c
    n, copies = len(ins), []

    def add(q, j, src, dst, dev):
        copies.append(pltpu.make_async_remote_copy(
            src_ref=src, dst_ref=dst, send_sem=send_sems.at[q * n + j], recv_sem=recv_sems.at[q * n + j],
            device_id=dev, device_id_type=pl.DeviceIdType.MESH))

    if mode == "pass":
        for q in range(4):
            slot = 4 * (x ^ (q >> 1)) + 2 * (y ^ (q & 1)) + c
            for j in range(n):
                add(q, j, ins[j] if q == 0 else lands[j].at[slot], lands[j].at[slot], (x, y, 1 - c))
        return me, copies
    for q, k in enumerate(range(1, NDEV) if mode == "all" else (2, 4, 6)):
        px, py, pc = x ^ ((k >> 2) & 1), y ^ ((k >> 1) & 1), c ^ (k & 1)
        for j in range(n):
            add(q, j, ins[j] if bcast[j] else ins[j].at[4 * px + 2 * py + pc], lands[j].at[me], (px, py, pc))
    return me, copies


_HBM = pl.BlockSpec(memory_space=pltpu.HBM)
_SEM = pl.BlockSpec(memory_space=pltpu.SEMAPHORE)
_EFFECT = pltpu.SideEffectType.DATAFLOW_SIDE_EFFECTING


def exchange_start(arrs, bcast, name, mode="all", lands=None):
    n, ncp = len(arrs), COPIES[mode] * len(arrs)
    land_shapes = [((NDEV,) + a.shape) if b else a.shape for a, b in zip(arrs, bcast)]
    if lands is None:
        lands = [lax.empty(s_, a.dtype) for s_, a in zip(land_shapes, arrs)]

    def body(*refs):
        in_refs, land_refs = refs[:n], refs[n:2 * n]
        send_sems, recv_sems = refs[2 * n], refs[2 * n + 1]
        token = refs[-1]
        _, copies = _exchange_copies(in_refs, land_refs, bcast, send_sems, recv_sems, mode)
        for cp in copies:
            cp.start()
        token[...] = jnp.zeros_like(token)

    hbm = lambda shp, a: pltpu.HBM(shp, a.dtype)
    res = pl.pallas_call(
        body, name=name,
        out_shape=[pltpu.SemaphoreType.DMA((ncp,)), pltpu.SemaphoreType.DMA((ncp,))]
                  + [hbm(a.shape, a) for a in arrs] + [hbm(s_, a) for s_, a in zip(land_shapes, arrs)]
                  + [jax.ShapeDtypeStruct((8, 128), F32)],
        in_specs=[_HBM] * (2 * n), out_specs=[_SEM, _SEM] + [_HBM] * (2 * n) + [pl.BlockSpec(memory_space=pltpu.VMEM)],
        input_output_aliases={i: 2 + i for i in range(2 * n)},
        compiler_params=pltpu.CompilerParams(has_side_effects=_EFFECT),
    )(*[pltpu.with_memory_space_constraint(a, pltpu.HBM) for a in arrs],
      *[pltpu.with_memory_space_constraint(a, pltpu.HBM) for a in lands])
    return (res[0], res[1], res[2:2 + n], res[2 + n:2 + 2 * n], tuple(bcast), mode), res[-1]


def exchange_wait(state, after, name):
    send_sems, recv_sems, ins, lands, bcast, mode = state
    n = len(ins)

    def body(*refs):
        in_refs, land_refs = refs[:n], refs[n:2 * n]
        s_sems, r_sems = refs[2 * n], refs[2 * n + 1]
        token = refs[-1]
        _, copies = _exchange_copies(in_refs, land_refs, bcast, s_sems, r_sems, mode)
        for cp in copies:
            cp.wait_send()
            cp.wait_recv()
        token[...] = jnp.zeros_like(token)

    res = pl.pallas_call(
        body, name=name,
        out_shape=[pltpu.HBM(a.shape, a.dtype) for a in ins] + [pltpu.HBM(a.shape, a.dtype) for a in lands]
                  + [jax.ShapeDtypeStruct((8, 128), F32)],
        in_specs=[_HBM] * (2 * n) + [_SEM, _SEM, pl.BlockSpec(memory_space=pl.ANY)],
        out_specs=[_HBM] * (2 * n) + [pl.BlockSpec(memory_space=pltpu.VMEM)],
        input_output_aliases={i: i for i in range(2 * n)},
        compiler_params=pltpu.CompilerParams(has_side_effects=_EFFECT),
    )(*ins, *lands, send_sems, recv_sems, after)
    if mode == "chips":
        return list(res[n:2 * n]), res[-1], list(res[:n])
    me = 4 * lax.axis_index("x") + 2 * lax.axis_index("y") + lax.axis_index("c")
    got = []
    for j in range(n):
        own = res[j][None] if bcast[j] else lax.dynamic_index_in_dim(res[j], me, 0, keepdims=True)
        got.append(lax.dynamic_update_slice_in_dim(res[n + j], own, me, axis=0))
    return got, res[-1], list(res[:n])


def gather_start(shards, name):
    return exchange_start(shards, [True] * len(shards), name + "_chips_start", mode="chips")


def gather_finish(state, after, name):
    lands, _, sent = exchange_wait(state, after, name + "_chips_wait")
    state, tok = exchange_start(sent, [True] * len(sent), name + "_pass_start", mode="pass", lands=lands)
    got, tok, _ = exchange_wait(state, tok, name + "_pass_wait")
    return got, tok


def norm_proj_fwd(x, vec, w, name):
    s, n = x.shape[0], w.shape[1]
    tr, tn = _pick(s, 512), _pick(n, 2560)
    jdt, odt = O_DT // tn, O_DT % tn

    def body(x_ref, v_ref, w_ref, o_ref, h_ref, dt_ref, h_scr):
        @pl.when(pl.program_id(1) == 0)
        def _():
            h = _rms(x_ref[...], v_ref[0:1, :], D) * (1.0 + v_ref[2:3, :]) + v_ref[1:2, :]
            h_scr[...] = h.astype(BF16)
            h_ref[...] = h.astype(BF16)
        res = jnp.dot(h_scr[...], w_ref[...], preferred_element_type=F32)
        o_ref[...] = res

        @pl.when(pl.program_id(1) == jdt)
        def _():
            dt_ref[...] = res[:, odt:odt + 128]

    return pl.pallas_call(
        body, name=name, grid=(s // tr, n // tn),
        in_specs=[pl.BlockSpec((tr, D), lambda i, j: (i, 0)), pl.BlockSpec((8, D), lambda i, j: (0, 0)),
                  pl.BlockSpec((D, tn), lambda i, j: (0, j))],
        out_specs=[pl.BlockSpec((tr, tn), lambda i, j: (i, j)), pl.BlockSpec((tr, D), lambda i, j: (i, 0)),
                   pl.BlockSpec((tr, 128), lambda i, j: (i, 0))],
        out_shape=[jax.ShapeDtypeStruct((s, n), F32), jax.ShapeDtypeStruct((s, D), BF16),
                   jax.ShapeDtypeStruct((s, 128), F32)],
        scratch_shapes=[pltpu.VMEM((tr, D), BF16)],
        compiler_params=_params("arbitrary", "arbitrary"),
    )(x, vec, w)


def _col_tiles(arr, cap):
    if arr.ndim == 2:
        n = arr.shape[1]
        t = _pick(n, cap)
        return n, t, lambda rows, ix: pl.BlockSpec((rows, t), lambda *g: ix(*g))
    width = arr.shape[2]
    t = _pick(width, cap)
    per = width // t

    def spec(rows, ix):
        def index(*g):
            r, j = ix(*g)
            return (j // per, r, j % per)
        return pl.BlockSpec((None, rows, t), index)
    return arr.shape[0] * width, t, spec


def norm_proj_bwd(x, vec, dp, w, dx_in, aux, name):
    s = x.shape[0]
    tr = _pick(s, 512)
    n, tk, dp_spec = _col_tiles(dp, 2560)
    nk, has_aux = n // tk, aux is not None

    def body(*refs):
        if has_aux:
            x_ref, v_ref, dp_ref, w_ref, dxin_ref, aux_ref, dx_ref, dv_ref, acc = refs
        else:
            x_ref, v_ref, dp_ref, w_ref, dxin_ref, dx_ref, dv_ref, acc = refs
        i, k = pl.program_id(0), pl.program_id(1)

        @pl.when(k == 0)
        def _():
            acc[...] = jnp.zeros_like(acc)

        acc[...] += _raw(dp_ref[...], w_ref[...], _NT)

        @pl.when(k == nk - 1)
        def _():
            f = lambda xx, nw, sh, sc: _rms(xx, nw, D) * (1.0 + sc) + sh
            _, vjp = jax.vjp(f, x_ref[...], v_ref[0:1, :], v_ref[1:2, :], v_ref[2:3, :])
            dx, dnw, dsh, dsc = vjp(acc[...])
            dx_ref[...] = dxin_ref[...] + dx

            @pl.when(i == 0)
            def _():
                dv_ref[...] = jnp.zeros_like(dv_ref)

            dv_ref[0:1, :] += dnw
            dv_ref[1:2, :] += dsh
            dv_ref[2:3, :] += dsc
            if has_aux:
                dv_ref[3:4, :] += jnp.sum(dxin_ref[...] * aux_ref[...], axis=0, keepdims=True)

    row = pl.BlockSpec((tr, D), lambda i, k: (i, 0))
    in_specs = [row, pl.BlockSpec((8, D), lambda i, k: (0, 0)), dp_spec(tr, lambda i, k: (i, k)),
                pl.BlockSpec((D, tk), lambda i, k: (0, k)), row] + ([row] if has_aux else [])
    args = [x, vec, dp, w, dx_in] + ([aux] if has_aux else [])
    return pl.pallas_call(
        body, name=name, grid=(s // tr, nk), in_specs=in_specs,
        out_specs=[row, pl.BlockSpec((8, D), lambda i, k: (0, 0))],
        out_shape=[jax.ShapeDtypeStruct((s, D), F32), jax.ShapeDtypeStruct((8, D), F32)],
        scratch_shapes=[pltpu.VMEM((tr, D), F32)],
        compiler_params=_params("arbitrary", "arbitrary"),
    )(*args)


def tn_matmul(a, b, name, scale=None, out_dtype=None):
    out_dtype = BF16 if out_dtype is None else out_dtype
    s, m = a.shape
    ts, tm = _pick(s, 512, 16), _pick(m, 1408)
    n, tn, b_spec = _col_tiles(b, 2560)
    ns, has_scale = s // ts, scale is not None

    def body(*refs):
        if has_scale:
            a_ref, b_ref, sc_ref, o_ref, acc = refs
        else:
            a_ref, b_ref, o_ref, acc = refs
        k = pl.program_id(2)

        @pl.when(k == 0)
        def _():
            acc[...] = jnp.zeros_like(acc)

        acc[...] += _raw(a_ref[...], b_ref[...], _TN)

        @pl.when(k == ns - 1)
        def _():
            o_ref[...] = (acc[...] * sc_ref[...] if has_scale else acc[...]).astype(out_dtype)

    in_specs = [pl.BlockSpec((ts, tm), lambda i, j, k: (k, i)), b_spec(ts, lambda i, j, k: (k, j))]
    if has_scale:
        in_specs.append(pl.BlockSpec((1, tn), lambda i, j, k: (0, j)))
    return pl.pallas_call(
        body, name=name, grid=(m // tm, n // tn, ns), in_specs=in_specs,
        out_specs=pl.BlockSpec((tm, tn), lambda i, j, k: (i, j)),
        out_shape=jax.ShapeDtypeStruct((m, n), out_dtype),
        scratch_shapes=[pltpu.VMEM((tm, tn), F32)],
        compiler_params=_params("arbitrary", "arbitrary", "arbitrary"),
    )(*([a, b] + ([scale] if has_scale else [])))


def ada_mod(c16, w):
    ncol = w.shape[2]

    def body(c_ref, w_ref, o_ref, a_ref):
        cc = c_ref[...]
        act = cc * _sig(cc)
        a_ref[...] = act
        o_ref[...] = _raw(act, w_ref[...], _NN)

    return pl.pallas_call(
        body, name="ada_mod", grid=(LAYERS,),
        in_specs=[pl.BlockSpec((16, D), lambda l: (0, 0)), pl.BlockSpec((None, D, ncol), lambda l: (l, 0, 0))],
        out_specs=[pl.BlockSpec((None, 16, ncol), lambda l: (l, 0, 0)), pl.BlockSpec((16, D), lambda l: (0, 0))],
        out_shape=[jax.ShapeDtypeStruct((LAYERS, 16, ncol), F32), jax.ShapeDtypeStruct((16, D), F32)],
        compiler_params=_params("arbitrary"),
    )(c16, w)


def _mla_head(q_lat, c_kv, kr, krs, wqn, wqr, wqrs, wkn, wv, qa_w, kva_w, qn_w, qr_w, qrs_w, kn_w, kr_w, krs_w,
              cos2, sin2):
    qn = _rms(q_lat, qa_w, 384.0)
    kvn = _rms(c_kv, kva_w, 256.0)
    qnope = _rms(mm_nn(qn, wqn), qn_w, 64.0)
    qr, qrs = mm_nn(qn, wqr), mm_nn(qn, wqrs)
    rq = lax.rsqrt(jnp.sum(qr * qr, axis=-1, keepdims=True) / 32.0 + EPS)
    qrope = rq * (qr * qr_w * cos2 + qrs * qrs_w * sin2)
    knope = _rms(mm_nn(kvn, wkn), kn_w, 64.0)
    v = mm_nn(kvn, wv)
    rk = lax.rsqrt(jnp.sum(kr * kr, axis=-1, keepdims=True) / 32.0 + EPS)
    krope = rk * (kr * kr_w * cos2 + krs * krs_w * sin2)
    return qnope, qrope, knope, krope, v


def _mla_vec_pieces(v_ref):
    return (v_ref[0:1, 0:384], v_ref[1:2, 0:256], v_ref[2:3, 0:128], v_ref[2:3, 128:256], v_ref[2:3, 256:384],
            v_ref[3:4, 0:128], v_ref[3:4, 128:256], v_ref[3:4, 256:384])


def _mla_in_specs(tr):
    return [pl.BlockSpec((tr, 384), lambda i: (i, O_QL // 384)), pl.BlockSpec((tr, 256), lambda i: (i, O_CKV // 256)),
            pl.BlockSpec((tr, 128), lambda i: (i, O_KR // 128)), pl.BlockSpec((tr, 128), lambda i: (i, O_KRS // 128)),
            pl.BlockSpec((HEADS, 384, 384), lambda i: (0, 0, 0), **CONST),
            pl.BlockSpec((HEADS, 256, 256), lambda i: (0, 0, 0), **CONST),
            pl.BlockSpec((8, 512), lambda i: (0, 0)),
            pl.BlockSpec((tr, 128), lambda i: (i, 0)), pl.BlockSpec((tr, 128), lambda i: (i, 0))]


def mla_pre_fwd(proj, wq, wkv, vec, cos2, sin2, name):
    s = proj.shape[0]
    tr = _pick(s, 256)

    def body(ql_ref, ckv_ref, kr_ref, krs_ref, wq_ref, wkv_ref, v_ref, cos_ref, sin_ref, q_out, k_out, v_out):
        acts = (ql_ref[...], ckv_ref[...], kr_ref[...], krs_ref[...])
        vp = _mla_vec_pieces(v_ref)
        for h in range(HEADS):
            ws = (wq_ref[h, :, 0:128], wq_ref[h, :, 128:256], wq_ref[h, :, 256:384],
                  wkv_ref[h, :, 0:128], wkv_ref[h, :, 128:256])
            qn, qr, kn, krp, v = _mla_head(*acts, *ws, *vp, cos_ref[...], sin_ref[...])
            q_out[h, :, 0:128] = qn.astype(BF16)
            q_out[h, :, 128:256] = qr.astype(BF16)
            k_out[h, :, 0:128] = kn.astype(BF16)
            k_out[h, :, 128:256] = krp.astype(BF16)
            v_out[h] = v.astype(BF16)

    return pl.pallas_call(
        body, name=name, grid=(s // tr,), in_specs=_mla_in_specs(tr),
        out_specs=[pl.BlockSpec((HEADS, tr, 256), lambda i: (0, i, 0)), pl.BlockSpec((HEADS, tr, 256), lambda i: (0, i, 0)),
                   pl.BlockSpec((HEADS, tr, 128), lambda i: (0, i, 0))],
        out_shape=[jax.ShapeDtypeStruct((HEADS, s, 256), BF16), jax.ShapeDtypeStruct((HEADS, s, 256), BF16),
                   jax.ShapeDtypeStruct((HEADS, s, 128), BF16)],
        compiler_params=_params("arbitrary"),
    )(proj, proj, proj, proj, wq, wkv, vec, cos2, sin2)


def mla_pre_bwd(proj, wq, wkv, vec, cos2, sin2, dq, dk, dv, name):
    s = proj.shape[0]
    tr = _pick(s, 256)

    def body(ql_ref, ckv_ref, kr_ref, krs_ref, wq_ref, wkv_ref, v_ref, cos_ref, sin_ref, dq_ref, dk_ref, dv_ref,
             dql_out, dckv_out, dkr_out, dkrs_out, dwq_out, dwkv_out, dvec_out):
        @pl.when(pl.program_id(0) == 0)
        def _():
            dwq_out[...] = jnp.zeros_like(dwq_out)
            dwkv_out[...] = jnp.zeros_like(dwkv_out)
            dvec_out[...] = jnp.zeros_like(dvec_out)

        acts = (ql_ref[...], ckv_ref[...], kr_ref[...], krs_ref[...])
        vp = _mla_vec_pieces(v_ref)
        cos2_, sin2_ = cos_ref[...], sin_ref[...]

        def head(h, carry):
            wq_h, wkv_h = wq_ref[h].astype(F32), wkv_ref[h].astype(F32)
            ws = (wq_h[:, 0:128], wq_h[:, 128:256], wq_h[:, 256:384], wkv_h[:, 0:128], wkv_h[:, 128:256])
            f = lambda *a: _mla_head(*a, cos2_, sin2_)
            _, vjp = jax.vjp(f, *acts, *ws, *vp)
            dq_h, dk_h = dq_ref[h], dk_ref[h]
            g = vjp((dq_h[:, 0:128], dq_h[:, 128:256], dk_h[:, 0:128], dk_h[:, 128:256], dv_ref[h]))
            dwq_out[h, :, 0:128] += g[4]
            dwq_out[h, :, 128:256] += g[5]
            dwq_out[h, :, 256:384] += g[6]
            dwkv_out[h, :, 0:128] += g[7]
            dwkv_out[h, :, 128:256] += g[8]
            dvec_out[0:1, 0:384] += g[9]
            dvec_out[1:2, 0:256] += g[10]
            dvec_out[2:3, 0:128] += g[11]
            dvec_out[2:3, 128:256] += g[12]
            dvec_out[2:3, 256:384] += g[13]
            dvec_out[3:4, 0:128] += g[14]
            dvec_out[3:4, 128:256] += g[15]
            dvec_out[3:4, 256:384] += g[16]
            return tuple(c + gg for c, gg in zip(carry, g[:4]))

        tot = lax.fori_loop(0, HEADS, head, tuple(jnp.zeros_like(a) for a in acts))
        dql_out[...] = tot[0].astype(BF16)
        dckv_out[...] = tot[1].astype(BF16)
        dkr_out[...] = tot[2].astype(BF16)
        dkrs_out[...] = tot[3].astype(BF16)

    hb = lambda w: pl.BlockSpec((HEADS, tr, w), lambda i: (0, i, 0))
    return pl.pallas_call(
        body, name=name, grid=(s // tr,), in_specs=_mla_in_specs(tr) + [hb(256), hb(256), hb(128)],
        out_specs=[pl.BlockSpec((tr, 384), lambda i: (i, 0)), pl.BlockSpec((tr, 256), lambda i: (i, 0)),
                   pl.BlockSpec((tr, 128), lambda i: (i, 0)), pl.BlockSpec((tr, 128), lambda i: (i, 0)),
                   pl.BlockSpec((HEADS, 384, 384), lambda i: (0, 0, 0)), pl.BlockSpec((HEADS, 256, 256), lambda i: (0, 0, 0)),
                   pl.BlockSpec((8, 512), lambda i: (0, 0))],
        out_shape=[jax.ShapeDtypeStruct((s, 384), BF16), jax.ShapeDtypeStruct((s, 256), BF16),
                   jax.ShapeDtypeStruct((s, 128), BF16), jax.ShapeDtypeStruct((s, 128), BF16),
                   jax.ShapeDtypeStruct((HEADS, 384, 384), F32), jax.ShapeDtypeStruct((HEADS, 256, 256), F32),
                   jax.ShapeDtypeStruct((8, 512), F32)],
        compiler_params=_params("arbitrary"),
    )(proj, proj, proj, proj, wq, wkv, vec, cos2, sin2, dq, dk, dv)


def _att_probs(q, kk, i, tq):
    sc = _raw(q, kk, _NT) * ATT_SCALE
    rows = lax.broadcasted_iota(jnp.int32, sc.shape, 0) + i * tq
    cols = lax.broadcasted_iota(jnp.int32, sc.shape, 1)
    sc = jnp.where(cols <= rows, sc, -jnp.inf)
    e = jnp.exp(sc - jnp.max(sc, axis=-1, keepdims=True))
    return e / jnp.sum(e, axis=-1, keepdims=True)


def mla_attn_fwd(q, k, v, name):
    s = q.shape[1]
    tq = _pick(s, 256)

    def body(q_ref, k_ref, v_ref, o_ref):
        for i in range(s // tq):
            n = (i + 1) * tq
            p = _att_probs(q_ref[i * tq:n, :], k_ref[0:n, :], i, tq)
            o_ref[i * tq:n, :] = _raw(p, v_ref[0:n, :], _NN)

    hs = lambda w: pl.BlockSpec((None, s, w), lambda h: (h, 0, 0))
    return pl.pallas_call(
        body, name=name, grid=(HEADS,), in_specs=[hs(256), hs(256), hs(128)],
        out_specs=pl.BlockSpec((s, 128), lambda h: (0, h)),
        out_shape=jax.ShapeDtypeStruct((s, HEADS * 128), F32),
        compiler_params=_params("arbitrary"),
    )(q, k, v)


def mla_attn_bwd(q, k, v, do, name):
    s = q.shape[1]
    tq = _pick(s, 256)

    def body(q_ref, k_ref, v_ref, do_ref, dq_ref, dk_ref, dv_ref):
        dk_ref[...] = jnp.zeros_like(dk_ref)
        dv_ref[...] = jnp.zeros_like(dv_ref)
        for i in range(s // tq):
            n = (i + 1) * tq
            qq, kk, vv = q_ref[i * tq:n, :], k_ref[0:n, :], v_ref[0:n, :]
            p = _att_probs(qq, kk, i, tq)
            o = _raw(p, vv, _NN)
            dout = do_ref[i * tq:n, :]
            delta = jnp.sum(dout * o, axis=-1, keepdims=True)
            dp = _raw(dout, vv, _NT)
            ds = p * (dp - delta) * ATT_SCALE
            dq_ref[i * tq:n, :] = _raw(ds, kk, _NN)
            dk_ref[0:n, :] += _raw(ds, qq, _TN)
            dv_ref[0:n, :] += _raw(p, dout, _TN)

    hs = lambda w: pl.BlockSpec((None, s, w), lambda h: (h, 0, 0))
    return pl.pallas_call(
        body, name=name, grid=(HEADS,),
        in_specs=[hs(256), hs(256), hs(128), pl.BlockSpec((s, 128), lambda h: (0, h))],
        out_specs=[hs(256), hs(256), hs(128)],
        out_shape=[jax.ShapeDtypeStruct((HEADS, s, 256), F32), jax.ShapeDtypeStruct((HEADS, s, 256), F32),
                   jax.ShapeDtypeStruct((HEADS, s, 128), F32)],
        compiler_params=_params("arbitrary"),
    )(q, k, v, do)


def _pool_windows(u, pad, s, g):
    pad[0:16, :] = jnp.zeros((16, 128), F32)
    cur, sel = u, None
    for j, k in enumerate((1, 2, 4, 8)):
        pad[16:16 + s, :] = cur
        cur = cur + pad[16 - k:16 - k + s, :]
        sel = cur if sel is None else jnp.where(g == j, cur, sel)
    return sel


def _pool_count(s, g):
    t = lax.broadcasted_iota(jnp.int32, (s, 1), 0)
    return jnp.minimum(t + 1, 2 << g).astype(F32)


def pool_fwd(proj, pw, ps, name):
    s = proj.shape[0]

    def body(u_ref, w_ref, s_ref, o_ref, pad):
        g = pl.program_id(0)
        u = u_ref[...]
        pooled = _pool_windows(u, pad, s, g) / _pool_count(s, g) - u
        o_ref[...] = _raw(pooled, w_ref[...], _NN) * s_ref[...]

    return pl.pallas_call(
        body, name=name, grid=(4,),
        in_specs=[pl.BlockSpec((s, 128), lambda g: (0, O_PU // 128 + g)), pl.BlockSpec((None, 128, 128), lambda g: (g, 0, 0)),
                  pl.BlockSpec((1, 128), lambda g: (0, g))],
        out_specs=pl.BlockSpec((s, 128), lambda g: (0, g)),
        out_shape=jax.ShapeDtypeStruct((s, 512), F32),
        scratch_shapes=[pltpu.VMEM((s + 16, 128), F32)],
        compiler_params=_params("arbitrary"),
    )(proj, pw, ps)


def pool_bwd(proj, pw, ps, do, name):
    s = proj.shape[0]

    def body(u_ref, w_ref, s_ref, do_ref, du_ref, dw_ref, ds_ref, pad):
        g = pl.program_id(0)
        u, w, dout = u_ref[...], w_ref[...], do_ref[...]
        cnt = _pool_count(s, g)
        pooled = _pool_windows(u, pad, s, g) / cnt - u
        mixed = _raw(pooled, w, _NN)
        ds_ref[...] = jnp.sum(dout * mixed, axis=0, keepdims=True)
        dmixed = dout * s_ref[...]
        dw_ref[...] = _raw(pooled, dmixed, _TN)
        dpooled = _raw(dmixed, w, _NT)
        dsel = dpooled / cnt
        pad[s:s + 16, :] = jnp.zeros((16, 128), F32)
        cur = jnp.where(g == 3, dsel, 0.0)
        for j, k in ((2, 8), (1, 4), (0, 2)):
            pad[0:s, :] = cur
            cur = cur + pad[k:k + s, :] + jnp.where(g == j, dsel, 0.0)
        pad[0:s, :] = cur
        cur = cur + pad[1:1 + s, :]
        du_ref[...] = (cur - dpooled).astype(BF16)

    return pl.pallas_call(
        body, name=name, grid=(4,),
        in_specs=[pl.BlockSpec((s, 128), lambda g: (0, O_PU // 128 + g)), pl.BlockSpec((None, 128, 128), lambda g: (g, 0, 0)),
                  pl.BlockSpec((1, 128), lambda g: (0, g)), pl.BlockSpec((s, 128), lambda g: (0, g))],
        out_specs=[pl.BlockSpec((s, 128), lambda g: (0, g)), pl.BlockSpec((None, 128, 128), lambda g: (g, 0, 0)),
                   pl.BlockSpec((1, 128), lambda g: (0, g))],
        out_shape=[jax.ShapeDtypeStruct((s, 512), BF16), jax.ShapeDtypeStruct((4, 128, 128), F32),
                   jax.ShapeDtypeStruct((1, 512), F32)],
        scratch_shapes=[pltpu.VMEM((s + 16, 128), F32)],
        compiler_params=_params("arbitrary"),
    )(proj, pw, ps, do)


def _xbc_col(i):
    return jnp.where(i < 2, O_XS // 512 + i, O_BC // 512)


def conv_fwd(proj, cw, cb, name):
    s = proj.shape[0]

    def body(x_ref, w_ref, b_ref, o_ref, t_ref, pad):
        pad[0:8, :] = jnp.zeros((8, 512), F32)
        pad[8:8 + s, :] = x_ref[...]
        y = b_ref[...] + sum(w_ref[k:k + 1, :] * pad[5 + k:5 + k + s, :] for k in range(4))
        act = y * _sig(y)
        o_ref[...] = act

        @pl.when(pl.program_id(0) < 2)
        def _():
            t_ref[...] = act.T

    return pl.pallas_call(
        body, name=name, grid=(3,),
        in_specs=[pl.BlockSpec((s, 512), lambda i: (0, _xbc_col(i))), pl.BlockSpec((4, 512), lambda i: (0, i)),
                  pl.BlockSpec((1, 512), lambda i: (0, i))],
        out_specs=[pl.BlockSpec((s, 512), lambda i: (0, i)), pl.BlockSpec((512, s), lambda i: (jnp.minimum(i, 1), 0))],
        out_shape=[jax.ShapeDtypeStruct((s, 1536), F32), jax.ShapeDtypeStruct((D, s), F32)],
        scratch_shapes=[pltpu.VMEM((s + 8, 512), F32)],
        compiler_params=_params("arbitrary"),
    )(proj, cw, cb)


def conv_bwd(proj, cw, cb, dxt, dbm, dcm, name):
    s = proj.shape[0]

    def body(x_ref, w_ref, b_ref, dxt_ref, dbm_ref, dcm_ref, dx_ref, dw_ref, db_ref, pad, pad2):
        pad[0:8, :] = jnp.zeros((8, 512), F32)
        pad[8:8 + s, :] = x_ref[...]
        y = b_ref[...] + sum(w_ref[k:k + 1, :] * pad[5 + k:5 + k + s, :] for k in range(4))
        sg = _sig(y)

        @pl.when(pl.program_id(0) < 2)
        def _():
            pad2[0:s, :] = dxt_ref[...].T

        @pl.when(pl.program_id(0) == 2)
        def _():
            pad2[0:s, 0:256] = dbm_ref[...]
            pad2[0:s, 256:512] = dcm_ref[...]

        dy = pad2[0:s, :] * (sg * (1.0 + y * (1.0 - sg)))
        db_ref[...] = jnp.sum(dy, axis=0, keepdims=True)
        for k in range(4):
            dw_ref[k:k + 1, :] = jnp.sum(dy * pad[5 + k:5 + k + s, :], axis=0, keepdims=True)
        pad2[s:s + 8, :] = jnp.zeros((8, 512), F32)
        pad2[0:s, :] = dy
        dx_ref[...] = sum(w_ref[k:k + 1, :] * pad2[3 - k:3 - k + s, :] for k in range(4)).astype(BF16)

    return pl.pallas_call(
        body, name=name, grid=(3,),
        in_specs=[pl.BlockSpec((s, 512), lambda i: (0, _xbc_col(i))), pl.BlockSpec((4, 512), lambda i: (0, i)),
                  pl.BlockSpec((1, 512), lambda i: (0, i)), pl.BlockSpec((512, s), lambda i: (jnp.minimum(i, 1), 0)),
                  pl.BlockSpec((s, 256), lambda i: (0, 0)), pl.BlockSpec((s, 256), lambda i: (0, 0))],
        out_specs=[pl.BlockSpec((s, 512), lambda i: (0, i)), pl.BlockSpec((4, 512), lambda i: (0, i)),
                   pl.BlockSpec((1, 512), lambda i: (0, i))],
        out_shape=[jax.ShapeDtypeStruct((s, 1536), BF16), jax.ShapeDtypeStruct((4, 1536), F32),
                   jax.ShapeDtypeStruct((1, 1536), F32)],
        scratch_shapes=[pltpu.VMEM((s + 8, 512), F32), pltpu.VMEM((s + 8, 512), F32)],
        compiler_params=_params("arbitrary"),
    )(proj, cw, cb, dxt, dbm, dcm)


def _ssd_chunk(xt, dtr, dtc, bm, cm, hprev, alog, dbias, dskip):
    ln = 128
    a = -jnp.exp(alog)
    dt_r = softplus(dtr + dbias)
    da_r = dt_r * a
    da_c = softplus(dtc + dbias) * a
    li = lax.broadcasted_iota(jnp.int32, (1, ln, ln), 1)
    si = lax.broadcasted_iota(jnp.int32, (1, ln, ln), 2)
    causal = si <= li
    acs_c = jnp.sum(jnp.where(causal, da_r, 0.0), axis=2, keepdims=True)
    acs_r = jnp.sum(jnp.where(li <= si, da_c, 0.0), axis=1, keepdims=True)
    acs_last = jnp.sum(da_r, axis=2, keepdims=True)
    decay = jnp.exp(jnp.where(causal, acs_c - acs_r, -jnp.inf))
    m = mm_nt(cm, bm)[None] * decay
    xdt = xt * dt_r
    y_diag = bmm_nt(xdt, m)
    bb = jnp.broadcast_to(bm[None], (8, ln, ln))
    cc = jnp.broadcast_to(cm[None], (8, ln, ln))
    states = bmm_nn(xdt * jnp.exp(acs_last - acs_r), bb)
    y_off = bmm_nt(hprev, cc) * jnp.exp(acs_r)
    hnew = hprev * jnp.exp(acs_last) + states
    return y_diag + y_off + xt * dskip, hnew


def _ssd_specs(nc, rev):
    cix = (lambda c: nc - 1 - c) if rev else (lambda c: c)
    hv = pl.BlockSpec((8, 1, 1), lambda g, c: (g, 0, 0))
    return [pl.BlockSpec((8, 64, 128), lambda g, c: (g, 0, cix(c))), pl.BlockSpec((8, 1, 128), lambda g, c: (g, 0, cix(c))),
            pl.BlockSpec((8, 128, 1), lambda g, c: (g, cix(c), 0)), pl.BlockSpec((128, 128), lambda g, c: (cix(c), 8 + g)),
            pl.BlockSpec((128, 128), lambda g, c: (cix(c), 10 + g))], hv, cix


def ssd_fwd(xt, dtr, dtc, xbc, alog, dbias, dskip, name):
    s = xt.shape[2]
    nc = s // 128
    specs, hv, _ = _ssd_specs(nc, False)

    def body(x_ref, dr_ref, dc_ref, b_ref, c_ref, al_ref, db_ref, dk_ref, y_ref, hs_ref, h_scr):
        @pl.when(pl.program_id(1) == 0)
        def _():
            h_scr[...] = jnp.zeros_like(h_scr)
        hp = h_scr[...]
        hs_ref[...] = hp
        y, hn = _ssd_chunk(x_ref[...], dr_ref[...], dc_ref[...], b_ref[...], c_ref[...], hp,
                           al_ref[...], db_ref[...], dk_ref[...])
        y_ref[...] = y
        h_scr[...] = hn

    return pl.pallas_call(
        body, name=name, grid=(2, nc), in_specs=specs + [hv, hv, hv],
        out_specs=[pl.BlockSpec((8, 64, 128), lambda g, c: (g, 0, c)),
                   pl.BlockSpec((None, None, 8, 64, 128), lambda g, c: (g, c, 0, 0, 0))],
        out_shape=[jax.ShapeDtypeStruct((16, 64, s), F32), jax.ShapeDtypeStruct((2, nc, 8, 64, 128), F32)],
        scratch_shapes=[pltpu.VMEM((8, 64, 128), F32)],
        compiler_params=_params("arbitrary", "arbitrary"),
    )(xt, dtr, dtc, xbc, xbc, alog, dbias, dskip)


def ssd_bwd(xt, dtr, dtc, xbc, alog, dbias, dskip, hs, dyt, name):
    s = xt.shape[2]
    nc = s // 128
    specs, hv, cix = _ssd_specs(nc, True)

    def body(x_ref, dr_ref, dc_ref, b_ref, c_ref, al_ref, db_ref, dk_ref, hs_ref, dy_ref,
             dx_out, ddr_out, ddc_out, dbm_out, dcm_out, dal_out, ddb_out, ddk_out, dh_scr):
        @pl.when(pl.program_id(1) == 0)
        def _():
            dh_scr[...] = jnp.zeros_like(dh_scr)
            dal_out[...] = jnp.zeros_like(dal_out)
            ddb_out[...] = jnp.zeros_like(ddb_out)
            ddk_out[...] = jnp.zeros_like(ddk_out)
        _, vjp = jax.vjp(_ssd_chunk, x_ref[...], dr_ref[...], dc_ref[...], b_ref[...], c_ref[...], hs_ref[...],
                         al_ref[...], db_ref[...], dk_ref[...])
        g = vjp((dy_ref[...], dh_scr[...]))
        dx_out[...] = g[0]
        ddr_out[...] = g[1]
        ddc_out[...] = g[2]
        dbm_out[...] = g[3]
        dcm_out[...] = g[4]
        dh_scr[...] = g[5]
        dal_out[...] += g[6]
        ddb_out[...] += g[7]
        ddk_out[...] += g[8]

    return pl.pallas_call(
        body, name=name, grid=(2, nc),
        in_specs=specs + [hv, hv, hv, pl.BlockSpec((None, None, 8, 64, 128), lambda g, c: (g, cix(c), 0, 0, 0)),
                          pl.BlockSpec((8, 64, 128), lambda g, c: (g, 0, cix(c)))],
        out_specs=[pl.BlockSpec((8, 64, 128), lambda g, c: (g, 0, cix(c))), pl.BlockSpec((8, 1, 128), lambda g, c: (g, 0, cix(c))),
                   pl.BlockSpec((8, 128, 1), lambda g, c: (g, cix(c), 0)), pl.BlockSpec((128, 128), lambda g, c: (cix(c), g)),
                   pl.BlockSpec((128, 128), lambda g, c: (cix(c), g)), hv, hv, hv],
        out_shape=[jax.ShapeDtypeStruct((16, 64, s), F32), jax.ShapeDtypeStruct((16, 1, s), F32),
                   jax.ShapeDtypeStruct((16, s, 1), F32), jax.ShapeDtypeStruct((s, 256), F32),
                   jax.ShapeDtypeStruct((s, 256), F32)] + [jax.ShapeDtypeStruct((16, 1, 1), F32)] * 3,
        scratch_shapes=[pltpu.VMEM((8, 64, 128), F32)],
        compiler_params=_params("arbitrary", "arbitrary"),
    )(xt, dtr, dtc, xbc, xbc, alog, dbias, dskip, hs, dyt)


def _merge(oa, ob, y, z, gla, glb, glc, x, g1, nw, ea, eb, ec, eo, wba, wbb, wbc, wout):
    gated = y * (z * _sig(z))
    sq = gated * gated
    left = lax.broadcasted_iota(jnp.int32, (1, D), 1) < 512
    ms0 = jnp.sum(jnp.where(left, sq, 0.0), axis=-1, keepdims=True) / 512.0
    ms1 = jnp.sum(jnp.where(left, 0.0, sq), axis=-1, keepdims=True) / 512.0
    oc = gated * jnp.where(left, lax.rsqrt(ms0 + EPS), lax.rsqrt(ms1 + EPS)) * nw
    ya, yb, yc = mm_nc(oa, wba) + ea, mm_nc(ob, wbb) + eb, mm_nc(oc, wbc) + ec
    merged = _sig(gla) * ya + _sig(glb) * yb + _sig(glc) * yc
    x1 = x + g1 * (mm_nc(merged, wout) + eo)
    return x1, (oc, merged)


def _merge_specs(tr):
    row = lambda w: pl.BlockSpec((tr, w), lambda i: (i, 0))
    acts = [row(D), row(512), pl.BlockSpec((D, tr), lambda i: (0, i)), pl.BlockSpec((tr, D), lambda i: (i, O_Z // D)),
            pl.BlockSpec((tr, 3 * D), lambda i: (i, 0)), row(D), pl.BlockSpec((8, D), lambda i: (0, 0))]
    cst = lambda r: pl.BlockSpec((r, D), lambda i: (0, 0), **CONST)
    return acts, [cst(D), cst(512), cst(D), cst(D)], row


def merge_fwd(oa, ob, y, proj, x, mvec, wba, wbb, wbc, wout, name):
    s = x.shape[0]
    tr = _pick(s, 256)
    acts, wts, row = _merge_specs(tr)

    def body(oa_ref, ob_ref, y_ref, z_ref, gl_ref, x_ref, mv_ref, wba_ref, wbb_ref, wbc_ref, wout_ref, o_ref):
        zero = jnp.zeros((1, D), F32)
        x1, _ = _merge(oa_ref[...], ob_ref[...], y_ref[...].T, z_ref[...], gl_ref[:, 0:D], gl_ref[:, D:2 * D],
                       gl_ref[:, 2 * D:3 * D], x_ref[...], mv_ref[0:1, :], mv_ref[1:2, :], zero, zero, zero, zero,
                       wba_ref[...], wbb_ref[...], wbc_ref[...], wout_ref[...])
        o_ref[...] = x1

    return pl.pallas_call(
        body, name=name, grid=(s // tr,), in_specs=acts + wts, out_specs=row(D),
        out_shape=jax.ShapeDtypeStruct((s, D), F32), compiler_params=_params("arbitrary"),
    )(oa, ob, y, proj, proj, x, mvec, wba, wbb, wbc, wout)


def merge_bwd(oa, ob, y, proj, x, mvec, wba, wbb, wbc, wout, dx1, name):
    s = x.shape[0]
    tr = _pick(s, 128)
    acts, wts, row = _merge_specs(tr)

    def body(oa_ref, ob_ref, y_ref, z_ref, gl_ref, x_ref, mv_ref, wba_ref, wbb_ref, wbc_ref, wout_ref, dx1_ref,
             doa_o, dob_o, dy_o, dz_o, dgl_o, dx_o, dmv_o, dya_o, dyb_o, dyc_o, dpre_o, oc_o, mg_o):
        zero = jnp.zeros((tr, D), F32)
        wts_ = (wba_ref[...], wbb_ref[...], wbc_ref[...], wout_ref[...])
        f = lambda *a: _merge(*a, *wts_)
        _, vjp, (oc, merged) = jax.vjp(
            f, oa_ref[...], ob_ref[...], y_ref[...].T, z_ref[...], gl_ref[:, 0:D], gl_ref[:, D:2 * D],
            gl_ref[:, 2 * D:3 * D], x_ref[...], mv_ref[0:1, :], mv_ref[1:2, :], zero, zero, zero, zero, has_aux=True)
        g = vjp(dx1_ref[...])
        doa_o[...] = g[0]
        dob_o[...] = g[1]
        dy_o[...] = g[2].T
        dz_o[...] = g[3].astype(BF16)
        dgl_o[:, 0:D] = g[4].astype(BF16)
        dgl_o[:, D:2 * D] = g[5].astype(BF16)
        dgl_o[:, 2 * D:3 * D] = g[6].astype(BF16)
        dx_o[...] = g[7]

        @pl.when(pl.program_id(0) == 0)
        def _():
            dmv_o[...] = jnp.zeros_like(dmv_o)

        dmv_o[0:1, :] += g[8]
        dmv_o[1:2, :] += g[9]
        dya_o[...] = g[10].astype(BF16)
        dyb_o[...] = g[11].astype(BF16)
        dyc_o[...] = g[12].astype(BF16)
        dpre_o[...] = g[13].astype(BF16)
        oc_o[...] = oc.astype(BF16)
        mg_o[...] = merged.astype(BF16)

    sd = lambda w, dt: jax.ShapeDtypeStruct((s, w), dt)
    return pl.pallas_call(
        body, name=name, grid=(s // tr,), in_specs=acts + wts + [row(D)],
        out_specs=[row(D), row(512), pl.BlockSpec((D, tr), lambda i: (0, i)), row(D), row(3 * D), row(D),
                   pl.BlockSpec((8, D), lambda i: (0, 0))] + [row(D)] * 6,
        out_shape=[sd(D, F32), sd(512, F32), jax.ShapeDtypeStruct((D, s), F32), sd(D, BF16), sd(3 * D, BF16), sd(D, F32),
                   jax.ShapeDtypeStruct((8, D), F32)] + [sd(D, BF16)] * 6,
        compiler_params=_params("arbitrary"),
    )(oa, ob, y, proj, proj, x, mvec, wba, wbb, wbc, wout, dx1)


def _conv3(u_scr, w_ref, first, rows, lanes):
    return sum(w_ref[k:k + 1, :] * u_scr[first + k:first + k + rows, lanes] for k in range(3))


def _ffn_tile_specs(tf, tile):
    def at(rows, off):
        return pl.BlockSpec((rows, tf), lambda *g: (0, off + tile(*g)))
    return [at(D, 0), at(D, FFN_NT), at(3, 0), at(3, FFN_NT), at(1, 0), at(1, FFN_NT)]


def ffn_fwd(x1, fvec, wup, cw, cb, wdn, name):
    s = x1.shape[0]
    tr, tf = _pick(s, 512), FFN_TILE
    lg, lv = slice(0, tf), slice(tf, 2 * tf)

    def body(x_ref, v_ref, wg_ref, wv_ref, cwg_ref, cwv_ref, cbg_ref, cbv_ref, wd_ref, x2_ref, h_ref, pre_ref,
             h_scr, u_scr, acc):
        i, t = pl.program_id(0), pl.program_id(1)

        @pl.when(t == 0)
        def _():
            @pl.when(i == 0)
            def _():
                h_scr[0:16, :] = jnp.zeros((16, D), BF16)

            @pl.when(i > 0)
            def _():
                h_scr[0:16, :] = h_scr[tr:tr + 16, :]

            h = (_rms(x_ref[...], v_ref[0:1, :], D) * (1.0 + v_ref[2:3, :]) + v_ref[1:2, :]).astype(BF16)
            h_scr[16:16 + tr, :] = h
            h_ref[...] = h
            acc[...] = jnp.zeros_like(acc)

        u_scr[:, lg] = jnp.dot(h_scr[...], wg_ref[...], preferred_element_type=F32)
        u_scr[:, lv] = jnp.dot(h_scr[...], wv_ref[...], preferred_element_type=F32)
        cg = _conv3(u_scr, cwg_ref, 14, tr, lg) + cbg_ref[...]
        cval = _conv3(u_scr, cwv_ref, 14, tr, lv) + cbv_ref[...]
        acc[...] += _raw(cg * _sig(cg) * cval, wd_ref[...], _NN)

        @pl.when(t == FFN_NT - 1)
        def _():
            pre_ref[...] = acc[...]
            x2_ref[...] = x_ref[...] + v_ref[3:4, :] * acc[...]

    row = pl.BlockSpec((tr, D), lambda i, t: (i, 0))
    return pl.pallas_call(
        body, name=name, grid=(s // tr, FFN_NT),
        in_specs=[row, pl.BlockSpec((8, D), lambda i, t: (0, 0))] + _ffn_tile_specs(tf, lambda i, t: t)
                 + [pl.BlockSpec((tf, D), lambda i, t: (t, 0))],
        out_specs=[row, row, row],
        out_shape=[jax.ShapeDtypeStruct((s, D), F32), jax.ShapeDtypeStruct((s, D), BF16), jax.ShapeDtypeStruct((s, D), F32)],
        scratch_shapes=[pltpu.VMEM((tr + 16, D), BF16), pltpu.VMEM((tr + 16, 2 * tf), F32), pltpu.VMEM((tr, D), F32)],
        compiler_params=_params("arbitrary", "arbitrary"),
    )(x1, fvec, wup, wup, cw, cw, cb, cb, wdn)


def ffn_bwd(h2, dx2, fvec, wup, cw, cb, wdn, name):
    s = h2.shape[0]
    tr, tf = _pick(s, 512), FFN_TILE
    ni, nb = s // tr, s // 16
    lg, lv = slice(0, tf), slice(tf, 2 * tf)

    def body(hp_ref, hm_ref, hn_ref, dm_ref, dn_ref, v_ref, wg_ref, wv_ref, cwg_ref, cwv_ref, cbg_ref, cbv_ref, wd_ref,
             dup_ref, act_ref, dcw_ref, u_scr, dc_scr):
        i = pl.program_id(1)
        hfull = jnp.concatenate([jnp.where(i > 0, hp_ref[...], jnp.zeros((16, D), BF16)), hm_ref[...],
                                 jnp.where(i < ni - 1, hn_ref[...], jnp.zeros((16, D), BF16))], axis=0)
        u_scr[:, lg] = jnp.dot(hfull, wg_ref[...], preferred_element_type=F32)
        u_scr[:, lv] = jnp.dot(hfull, wv_ref[...], preferred_element_type=F32)
        cg = _conv3(u_scr, cwg_ref, 14, tr + 16, lg) + cbg_ref[...]
        cval = _conv3(u_scr, cwv_ref, 14, tr + 16, lv) + cbv_ref[...]
        g2 = v_ref[3:4, :]
        dpre = jnp.concatenate([dm_ref[...] * g2, jnp.where(i < ni - 1, dn_ref[...], 0.0) * g2], axis=0)
        dact = _raw(dpre, wd_ref[...], _NT)
        sg = _sig(cg)
        sl = cg * sg
        dc_scr[:, lg] = dact * cval * (sg * (1.0 + cg * (1.0 - sg)))
        dc_scr[:, lv] = dact * sl
        act_ref[...] = (sl * cval)[0:tr, :].astype(BF16)

        @pl.when(i == 0)
        def _():
            dcw_ref[...] = jnp.zeros_like(dcw_ref)

        for half, lanes, cw_ref in ((0, lg, cwg_ref), (1, lv, cwv_ref)):
            dup_ref[half] = sum(cw_ref[k:k + 1, :] * dc_scr[2 - k:2 - k + tr, lanes] for k in range(3)).astype(BF16)
            dcm = dc_scr[0:tr, lanes]
            for k in range(3):
                dcw_ref[half, k:k + 1, :] += jnp.sum(dcm * u_scr[14 + k:14 + k + tr, lanes], axis=0, keepdims=True)
            dcw_ref[half, 3:4, :] += jnp.sum(dcm, axis=0, keepdims=True)

    r16 = tr // 16
    prev = lambda t, i: (jnp.maximum(i * r16 - 1, 0), 0)
    nxt = lambda t, i: (jnp.minimum((i + 1) * r16, nb - 1), 0)
    main = lambda t, i: (i, 0)
    return pl.pallas_call(
        body, name=name, grid=(FFN_NT, ni),
        in_specs=[pl.BlockSpec((16, D), prev), pl.BlockSpec((tr, D), main), pl.BlockSpec((16, D), nxt),
                  pl.BlockSpec((tr, D), main), pl.BlockSpec((16, D), nxt), pl.BlockSpec((8, D), lambda t, i: (0, 0))]
                 + _ffn_tile_specs(tf, lambda t, i: t) + [pl.BlockSpec((tf, D), lambda t, i: (t, 0))],
        out_specs=[pl.BlockSpec((2, tr, tf), lambda t, i: (0, i, t)), pl.BlockSpec((tr, tf), lambda t, i: (i, t)),
                   pl.BlockSpec((2, 8, tf), lambda t, i: (0, 0, t))],
        out_shape=[jax.ShapeDtypeStruct((2, s, FFN), BF16), jax.ShapeDtypeStruct((s, FFN), BF16),
                   jax.ShapeDtypeStruct((2, 8, FFN), F32)],
        scratch_shapes=[pltpu.VMEM((tr + 32, 2 * tf), F32), pltpu.VMEM((tr + 16, 2 * tf), F32)],
        compiler_params=_params("arbitrary", "arbitrary"),
    )(h2, h2, h2, dx2, dx2, fvec, wup, wup, cw, cw, cb, cb, wdn)


def loss_head(y, target):
    s = y.shape[0]
    tr = _pick(s, 512)

    def body(y_ref, t_ref, dx_ref, l_ref):
        @pl.when(pl.program_id(0) == 0)
        def _():
            l_ref[...] = jnp.zeros_like(l_ref)
        err = y_ref[...] - t_ref[...]
        dx_ref[...] = err / float(D)
        l_ref[...] += 0.5 * jnp.sum(jnp.sum(err * err, axis=-1, keepdims=True) / float(D), axis=0, keepdims=True)

    row = pl.BlockSpec((tr, D), lambda i: (i, 0))
    return pl.pallas_call(
        body, name="loss_head", grid=(s // tr,), in_specs=[row, row],
        out_specs=[row, pl.BlockSpec((8, 128), lambda i: (0, 0))],
        out_shape=[jax.ShapeDtypeStruct((s, D), F32), jax.ShapeDtypeStruct((8, 128), F32)],
        compiler_params=_params("arbitrary"),
    )(y, target)


def adamw(parts, w, m, v, name, tok=None):
    nseg = len(parts)
    p, r, c = parts[0].shape
    tr = _pick(r, 256, 8)
    ni = r // tr
    tok = jnp.zeros((8, 128), F32) if tok is None else tok

    def body(*refs):
        p_refs = refs[:nseg]
        w_ref, m_ref, v_ref, _, g_out, d_out, m_out, v_out, g_scr = refs[nseg:]
        for q in range(nseg):
            @pl.when(pl.program_id(0) == q)
            def _(q=q):
                g = p_refs[q][0].astype(F32)
                for j in range(1, p):
                    g = g + p_refs[q][j].astype(F32)
                g_scr[...] = g
        g = g_scr[...]
        mn = B1 * m_ref[...] + (1.0 - B1) * g
        vn = B2 * v_ref[...] + (1.0 - B2) * (g * g)
        m_hat = mn / (1.0 - B1 ** STEP)
        v_hat = vn / (1.0 - B2 ** STEP)
        g_out[...] = g
        d_out[...] = -LR * (m_hat / (jnp.sqrt(v_hat) + ADAM_EPS) + WD * w_ref[...])
        m_out[...] = mn
        v_out[...] = vn

    row = pl.BlockSpec((tr, c), lambda l, i: (l * ni + i, 0))
    part = lambda q: pl.BlockSpec((p, tr, c), lambda l, i: (0, jnp.clip((l - q) * ni + i, 0, ni - 1), 0))
    return pl.pallas_call(
        body, name=name, grid=(nseg, ni),
        in_specs=[part(q) for q in range(nseg)] + [row, row, row, pl.BlockSpec((8, 128), lambda l, i: (0, 0))],
        out_specs=[row] * 4, out_shape=[jax.ShapeDtypeStruct((nseg * r, c), F32)] * 4,
        scratch_shapes=[pltpu.VMEM((tr, c), F32)],
        compiler_params=_params("arbitrary", "arbitrary"),
    )(*parts, w, m, v, tok)


def _padc(a, n):
    return jnp.pad(a, [(0, 0)] * (a.ndim - 1) + [(0, n - a.shape[-1])])


def _swap16(a):
    return jnp.concatenate([a[..., 16:32], a[..., 0:16]], axis=-1)


def _shard_cols(g8, a, b):
    c = g8.shape[2]
    return [g8[j][:, max(a, j * c) - j * c:min(b, (j + 1) * c) - j * c] for j in range(a // c, (b - 1) // c + 1)]


def _win_layout(g8):
    cols = lambda a, b: _shard_cols(g8, a, b)
    kr = jnp.concatenate(cols(640, 672), axis=1)
    dt = jnp.concatenate(cols(3744, 3760), axis=1)
    return jnp.concatenate(cols(3760, 6832) + cols(2208, 3232) + cols(1184, 2208) + cols(672, 1184) + cols(3232, 3744)
                           + cols(384, 640) + [_padc(kr, 128), _padc(_swap16(kr), 128), _padc(dt, 128),
                                               jnp.zeros((g8.shape[1], 128), g8.dtype)] + cols(0, 384), axis=1)


def _win_grad_shards(g):
    kr = (g[:, O_KR:O_KR + 32].astype(F32) + _swap16(g[:, O_KRS:O_KRS + 32].astype(F32))).astype(g.dtype)
    segs = [(g, O_QL, 384), (g, O_CKV, 256), (kr, 0, 32), (g, O_PU, 512), (g, O_Z, D), (g, O_XS, D), (g, O_BC, 512),
            (g, O_DT, 16), (g, O_G, 3 * D)]
    shards, width = [], sum(w for _, _, w in segs) // NDEV
    for j in range(NDEV):
        a, b, off, pieces = width * j, width * (j + 1), 0, []
        for arr, lo, w in segs:
            s0, s1 = max(a, off), min(b, off + w)
            if s0 < s1:
                pieces.append(arr[:, lo + s0 - off:lo + s1 - off])
            off += w
        shards.append(jnp.concatenate(pieces, axis=1))
    return jnp.stack(shards).astype(BF16)


def _wq_layout(w):
    w = w.reshape(384, HEADS, 96).transpose(1, 0, 2)
    rope = w[:, :, 64:96]
    return jnp.concatenate([_padc(w[:, :, 0:64], 128), _padc(rope, 128), _padc(_swap16(rope), 128)], axis=2)


def _wq_unlayout(g):
    rope = g[:, :, 128:160] + _swap16(g[:, :, 256:288])
    return jnp.concatenate([g[:, :, 0:64], rope], axis=2).transpose(1, 0, 2).reshape(384, HEADS * 96)


def _wkv_layout(w):
    w = w.reshape(256, HEADS, 128).transpose(1, 0, 2)
    return jnp.concatenate([_padc(w[:, :, 0:64], 128), _padc(w[:, :, 64:128], 128)], axis=2)


def _wkv_unlayout(g):
    return jnp.concatenate([g[:, :, 0:64], g[:, :, 128:192]], axis=2).transpose(1, 0, 2).reshape(256, HEADS * 128)


def _wba_layout(w):
    return jnp.pad(w.reshape(HEADS, 64, D), ((0, 0), (0, 64), (0, 0))).reshape(HEADS * 128, D)


def _rows8(rows, width):
    out = jnp.stack([_padc(r.astype(F32), width) for r in rows])
    return jnp.pad(out, ((0, 8 - out.shape[0]), (0, 0)))


def _mla_vec(qa, kva, qn, kn):
    def row(n):
        return jnp.concatenate([_padc(n[0:64], 128), _padc(n[64:96], 128), _padc(_swap16(n[64:96]), 128)])
    return _rows8([qa, kva, row(qn), row(kn)], 512)


def _mla_unvec(g):
    def un(r):
        return jnp.concatenate([r[0:64], r[128:160] + _swap16(r[256:288])])
    return g[0, 0:384], g[1, 0:256], un(g[2]), un(g[3])


SMALL = (("ada_b", (6 * D,)), ("norm1_w", (D,)), ("q_a_norm", (384,)), ("kv_a_norm", (256,)), ("q_norm", (96,)),
         ("k_norm", (96,)), ("pool_w", (4, 128, 128)), ("pool_scale", (512,)), ("ssd_conv_b", (1536,)),
         ("ssd_dt_bias", (16,)), ("ssd_a_log", (16,)), ("ssd_d", (16,)), ("ssd_norm_w", (D,)), ("norm2_w", (D,)),
         ("ffn_conv_b", (2 * FFN,)), ("ssd_conv_w", (4, 1536)), ("ffn_conv_w", (3, 2 * FFN)))
SMALL_REPL = SMALL[:15]
SMALL_ROWS = 208


def _pack(per_layer, names):
    flat = jnp.concatenate([per_layer[l][n].reshape(-1).astype(F32) for n, _ in names for l in range(LAYERS)])
    return jnp.pad(flat, (0, SMALL_ROWS * D - flat.shape[0])).reshape(SMALL_ROWS, D)


def _unpack(packed, names):
    flat, out, off = packed.reshape(-1), {}, 0
    for n, shp in names:
        size = LAYERS * math.prod(shp)
        out[n] = flat[off:off + size].reshape((LAYERS,) + shp)
        off += size
    return out


GROUP_A = ("w_in", "w_q_b", "w_kv_b")
GROUP_B = ("w_branch", "w_out", "ffn_up", "ffn_down")
BIG = GROUP_A + GROUP_B
COL_SHARDED = ("w_in", "w_q_b", "w_kv_b", "ffn_up")


def _behind(arrs, tok):
    arrs = list(arrs)
    j = min(range(len(arrs)), key=lambda q: arrs[q].size)
    arrs[j] = arrs[j] + tok[0, 0].astype(arrs[j].dtype)
    return arrs


def _gathered_full(g, name):
    if name in COL_SHARDED:
        return g.transpose(1, 0, 2).reshape(g.shape[1], NDEV * g.shape[2])
    return g.reshape(NDEV * g.shape[1], g.shape[2])


def _to_shards(full, name):
    if name == "w_in":
        return _win_grad_shards(full)
    if name in COL_SHARDED:
        r, c = full.shape
        return full.reshape(r, NDEV, c // NDEV).transpose(1, 0, 2).astype(BF16)
    r, c = full.shape
    return full.reshape(NDEV, r // NDEV, c).astype(BF16)


def _fwd_a(x, lw, mod, cos2, sin2, l, tok):
    sh1, sc1, g1, sh2, sc2, g2 = [mod[j * D:(j + 1) * D] for j in range(6)]
    vec1 = _rows8([lw["norm1_w"], sh1, sc1], D) + tok[0, 0]
    proj, h1, dt_cols = norm_proj_fwd(x, vec1, lw["win"], f"inproj_fwd{l}")
    q, k, v = mla_pre_fwd(proj, lw["wq"], lw["wkv"], lw["mla_vec"], cos2, sin2, f"mla_pre_fwd{l}")
    oa = mla_attn_fwd(q, k, v, f"mla_attn_fwd{l}")
    ob = pool_fwd(proj, lw["pool_w"], lw["pool_scale"].reshape(1, 512), f"pool_fwd{l}")
    xbc, xt = conv_fwd(proj, lw["ssd_conv_w"], lw["ssd_conv_b"].reshape(1, 1536), f"conv_fwd{l}")
    s = x.shape[0]
    xt = xt.reshape(16, 64, s)
    dt = dt_cols[:, 0:16].T
    dtr, dtc = dt[:, None, :], dt[:, :, None]
    hv = lambda a: a.reshape(16, 1, 1)
    yt, hs = ssd_fwd(xt, dtr, dtc, xbc, hv(lw["ssd_a_log"]), hv(lw["ssd_dt_bias"]), hv(lw["ssd_d"]), f"ssd_fwd{l}")
    return dict(x=x, vec1=vec1, proj=proj, h1=h1, q=q, k=k, v=v, oa=oa, ob=ob, xbc=xbc, xt=xt, dtr=dtr, dtc=dtc,
                hs=hs, yt=yt.reshape(D, s), mvec=_rows8([g1, lw["ssd_norm_w"]], D),
                fvec=_rows8([lw["norm2_w"], sh2, sc2, g2], D))


def _fwd_b(sv, lw, l, tok):
    sv["mvec"] = sv["mvec"] + tok[0, 0]
    x1 = merge_fwd(sv["oa"], sv["ob"], sv["yt"], sv["proj"], sv["x"], sv["mvec"], lw["wba"], lw["wbb"], lw["wbc"],
                   lw["wout"], f"merge_fwd{l}")
    x2, h2, pre = ffn_fwd(x1, sv["fvec"], lw["wup"], lw["ffn_conv_w"], lw["ffn_conv_b"].reshape(1, 2 * FFN), lw["wdn"],
                          f"ffn_fwd{l}")
    sv.update(x1=x1, h2=h2, pre=pre)
    return x2


def _bwd_b(dx2, lw, sv, l, tok):
    grads, small = {}, {}
    fvec = sv["fvec"] + tok[0, 0]
    dup, act, dcw = ffn_bwd(sv["h2"], dx2, fvec, lw["wup"], lw["ffn_conv_w"], lw["ffn_conv_b"].reshape(1, 2 * FFN),
                            lw["wdn"], f"ffn_bwd{l}")
    grads["ffn_down"] = tn_matmul(act, dx2, f"dw_down{l}", scale=fvec[3:4])
    grads["ffn_up"] = tn_matmul(sv["h2"], dup, f"dw_up{l}")
    dx1, dfvec = norm_proj_bwd(sv["x1"], fvec, dup, lw["wup"], dx2, sv["pre"], f"ffn_norm_bwd{l}")
    small["ffn_conv_w"] = jnp.concatenate([dcw[0, 0:3], dcw[1, 0:3]], axis=1)
    small["ffn_conv_b"] = jnp.concatenate([dcw[0, 3], dcw[1, 3]])
    small["norm2_w"] = dfvec[0]
    (doa, dob, dyt, dz, dgl, dx, dmvec, dya, dyb, dyc, dpre, oc, merged) = merge_bwd(
        sv["oa"], sv["ob"], sv["yt"], sv["proj"], sv["x"], sv["mvec"], lw["wba"], lw["wbb"], lw["wbc"], lw["wout"], dx1,
        f"merge_bwd{l}")
    dwba = tn_matmul(sv["oa"], dya, f"dw_ba{l}").reshape(HEADS, 128, D)[:, 0:64].reshape(512, D)
    grads["w_branch"] = jnp.concatenate([dwba, tn_matmul(sv["ob"], dyb, f"dw_bb{l}"), tn_matmul(oc, dyc, f"dw_bc{l}")])
    grads["w_out"] = tn_matmul(merged, dpre, f"dw_out{l}")
    small["ssd_norm_w"] = dmvec[1]
    small["dmod_b"] = (dmvec[0], dfvec[1], dfvec[2], dfvec[3])
    return dx, dict(doa=doa, dob=dob, dyt=dyt, dz=dz, dgl=dgl), grads, small


def _bwd_a(dx, cot, lw, sv, cos2, sin2, l, tok, small):
    s = dx.shape[0]
    grads = {}
    doa, dob, dz, dgl = cot["doa"], cot["dob"], cot["dz"], cot["dgl"]
    hv = lambda a: a.reshape(16, 1, 1)
    dxt, ddtr, ddtc, dbm, dcm, dal, ddb, ddk = ssd_bwd(
        sv["xt"], sv["dtr"], sv["dtc"], sv["xbc"], hv(lw["ssd_a_log"]) + tok[0, 0], hv(lw["ssd_dt_bias"]),
        hv(lw["ssd_d"]), sv["hs"], cot["dyt"].reshape(16, 64, s), f"ssd_bwd{l}")
    small["ssd_a_log"], small["ssd_dt_bias"], small["ssd_d"] = dal.reshape(16), ddb.reshape(16), ddk.reshape(16)
    dxbc, dscw, dscb = conv_bwd(sv["proj"], lw["ssd_conv_w"], lw["ssd_conv_b"].reshape(1, 1536), dxt.reshape(D, s),
                                dbm, dcm, f"conv_bwd{l}")
    small["ssd_conv_w"], small["ssd_conv_b"] = dscw, dscb.reshape(1536)
    ddt = (ddtr[:, 0, :] + ddtc[:, :, 0]).T
    du, dpw, dps = pool_bwd(sv["proj"], lw["pool_w"], lw["pool_scale"].reshape(1, 512), dob, f"pool_bwd{l}")
    small["pool_w"], small["pool_scale"] = dpw, dps.reshape(512)
    dq, dk, dv = mla_attn_bwd(sv["q"], sv["k"], sv["v"], doa, f"mla_attn_bwd{l}")
    dql, dckv, dkr, dkrs, dwq, dwkv, dmv = mla_pre_bwd(sv["proj"], lw["wq"], lw["wkv"], lw["mla_vec"], cos2, sin2,
                                                       dq, dk, dv, f"mla_pre_bwd{l}")
    grads["w_q_b"], grads["w_kv_b"] = _wq_unlayout(dwq), _wkv_unlayout(dwkv)
    small["q_a_norm"], small["kv_a_norm"], small["q_norm"], small["k_norm"] = _mla_unvec(dmv)
    dproj = jnp.concatenate([dgl, dxbc[:, 0:D], dz, du, dxbc[:, D:1536], dckv, dkr, dkrs,
                             _padc(ddt, 128).astype(BF16), jnp.zeros((s, 128), BF16), dql], axis=1)
    grads["w_in"] = tn_matmul(sv["h1"], dproj, f"dw_in{l}")
    dx0, dvec1 = norm_proj_bwd(sv["x"], sv["vec1"], dproj, lw["win"], dx, None, f"inproj_bwd{l}")
    small["norm1_w"] = dvec1[0]
    small["ada_b"] = jnp.concatenate([dvec1[1], dvec1[2], *small.pop("dmod_b")])
    return dx0, grads, small


def kernel(x, c, positions, ada_w, ada_b, norm1_w, w_in, q_a_norm, w_q_b, kv_a_norm, w_kv_b, q_norm, k_norm, pool_w, pool_scale, ssd_conv_w, ssd_conv_b, ssd_dt_bias, ssd_a_log, ssd_d, ssd_norm_w, w_branch, w_out, norm2_w, ffn_up, ffn_conv_w, ffn_conv_b, ffn_down, loss_target, m_ada_w, m_ada_b, m_norm1_w, m_w_in, m_q_a_norm, m_w_q_b, m_kv_a_norm, m_w_kv_b, m_q_norm, m_k_norm, m_pool_w, m_pool_scale, m_ssd_conv_w, m_ssd_conv_b, m_ssd_dt_bias, m_ssd_a_log, m_ssd_d, m_ssd_norm_w, m_w_branch, m_w_out, m_norm2_w, m_ffn_up, m_ffn_conv_w, m_ffn_conv_b, m_ffn_down, v_ada_w, v_ada_b, v_norm1_w, v_w_in, v_q_a_norm, v_w_q_b, v_kv_a_norm, v_w_kv_b, v_q_norm, v_k_norm, v_pool_w, v_pool_scale, v_ssd_conv_w, v_ssd_conv_b, v_ssd_dt_bias, v_ssd_a_log, v_ssd_d, v_ssd_norm_w, v_w_branch, v_w_out, v_norm2_w, v_ffn_up, v_ffn_conv_w, v_ffn_conv_b, v_ffn_down):
    p = dict(ada_w=ada_w, ada_b=ada_b, norm1_w=norm1_w, w_in=w_in, q_a_norm=q_a_norm, w_q_b=w_q_b, kv_a_norm=kv_a_norm,
             w_kv_b=w_kv_b, q_norm=q_norm, k_norm=k_norm, pool_w=pool_w, pool_scale=pool_scale, ssd_conv_w=ssd_conv_w,
             ssd_conv_b=ssd_conv_b, ssd_dt_bias=ssd_dt_bias, ssd_a_log=ssd_a_log, ssd_d=ssd_d, ssd_norm_w=ssd_norm_w,
             w_branch=w_branch, w_out=w_out, norm2_w=norm2_w, ffn_up=ffn_up, ffn_conv_w=ffn_conv_w, ffn_conv_b=ffn_conv_b,
             ffn_down=ffn_down)
    mom = dict(ada_w=m_ada_w, ada_b=m_ada_b, norm1_w=m_norm1_w, w_in=m_w_in, q_a_norm=m_q_a_norm, w_q_b=m_w_q_b,
               kv_a_norm=m_kv_a_norm, w_kv_b=m_w_kv_b, q_norm=m_q_norm, k_norm=m_k_norm, pool_w=m_pool_w,
               pool_scale=m_pool_scale, ssd_conv_w=m_ssd_conv_w, ssd_conv_b=m_ssd_conv_b, ssd_dt_bias=m_ssd_dt_bias,
               ssd_a_log=m_ssd_a_log, ssd_d=m_ssd_d, ssd_norm_w=m_ssd_norm_w, w_branch=m_w_branch, w_out=m_w_out,
               norm2_w=m_norm2_w, ffn_up=m_ffn_up, ffn_conv_w=m_ffn_conv_w, ffn_conv_b=m_ffn_conv_b, ffn_down=m_ffn_down)
    var = dict(ada_w=v_ada_w, ada_b=v_ada_b, norm1_w=v_norm1_w, w_in=v_w_in, q_a_norm=v_q_a_norm, w_q_b=v_w_q_b,
               kv_a_norm=v_kv_a_norm, w_kv_b=v_w_kv_b, q_norm=v_q_norm, k_norm=v_k_norm, pool_w=v_pool_w,
               pool_scale=v_pool_scale, ssd_conv_w=v_ssd_conv_w, ssd_conv_b=v_ssd_conv_b, ssd_dt_bias=v_ssd_dt_bias,
               ssd_a_log=v_ssd_a_log, ssd_d=v_ssd_d, ssd_norm_w=v_ssd_norm_w, w_branch=v_w_branch, w_out=v_w_out,
               norm2_w=v_norm2_w, ffn_up=v_ffn_up, ffn_conv_w=v_ffn_conv_w, ffn_conv_b=v_ffn_conv_b, ffn_down=v_ffn_down)
    names = list(p)
    me = 4 * lax.axis_index("x") + 2 * lax.axis_index("y") + lax.axis_index("c")
    xs, tgt = x[0], loss_target[0]
    s = xs.shape[0]

    inv_freq = ROPE_THETA ** (-jnp.arange(0, 32, 2, dtype=F32) / 32.0)
    ang = positions[0].astype(F32)[:, None] * inv_freq
    cos, sin = jnp.cos(ang), jnp.sin(ang)
    cos2 = _padc(jnp.concatenate([cos, cos], axis=1), 128)
    sin2 = _padc(jnp.concatenate([-sin, sin], axis=1), 128)

    conv_shards = jnp.concatenate([ssd_conv_w.reshape(-1), ffn_conv_w.reshape(-1)])
    (c_all, conv_all), _ = all_to_all([c, conv_shards], [True, True], "gather_c")
    modp, cact = ada_mod(jnp.pad(c_all.reshape(NDEV, D), ((0, 8), (0, 0))), ada_w)
    (mod_in,), tok = all_to_all([modp[:, 0:NDEV].transpose(1, 0, 2)], [False], "scatter_mod")
    mod = mod_in.transpose(1, 0, 2).reshape(LAYERS, 6 * D) + ada_b

    n1 = LAYERS * 4 * 192
    scw = conv_all[:, :n1].reshape(NDEV, LAYERS, 4, 192).transpose(1, 2, 0, 3).reshape(LAYERS, 4, 1536)
    fcw = conv_all[:, n1:].reshape(NDEV, LAYERS, 3, 704).transpose(1, 2, 0, 3).reshape(LAYERS, 3, 2 * FFN)

    def weights_a(gathered, l):
        full = {n: _gathered_full(g, n) for n, g in zip(GROUP_A[1:], gathered[1:])}
        lw = {n: p[n][l] for n in names}
        lw.update(win=_win_layout(gathered[0]), wq=_wq_layout(full["w_q_b"]), wkv=_wkv_layout(full["w_kv_b"]),
                  ssd_conv_w=scw[l], ffn_conv_w=fcw[l],
                  mla_vec=_mla_vec(lw["q_a_norm"], lw["kv_a_norm"], lw["q_norm"], lw["k_norm"]))
        return lw

    def weights_b(gathered):
        full = {n: _gathered_full(g, n) for n, g in zip(GROUP_B, gathered)}
        wb = full["w_branch"]
        return dict(wba=_wba_layout(wb[0:512]), wbb=wb[512:1024], wbc=wb[1024:2048], wout=full["w_out"],
                    wup=full["ffn_up"], wdn=full["ffn_down"])

    shards = lambda group, l: [p[n][l].astype(BF16) for n in group]
    lws, saved = [None] * LAYERS, [None] * LAYERS
    st, tok = gather_start(_behind(shards(GROUP_A, 0), tok), "gather_a0")
    got, tok = gather_finish(st, tok, "gather_a0")
    h = xs
    for l in range(LAYERS):
        st, tok = gather_start(_behind(shards(GROUP_B, l), tok), f"gather_b{l}")
        lws[l] = weights_a(got, l)
        saved[l] = _fwd_a(h, lws[l], mod[l], cos2, sin2, l, tok)
        got, tok = gather_finish(st, saved[l]["yt"], f"gather_b{l}")
        lws[l].update(weights_b(got))
        if l + 1 < LAYERS:
            st, tok = gather_start(_behind(shards(GROUP_A, l + 1), tok), f"gather_a{l + 1}")
        h = _fwd_b(saved[l], lws[l], l, tok)
        if l + 1 < LAYERS:
            got, tok = gather_finish(st, h, f"gather_a{l + 1}")
    dx, lpart = loss_head(h, tgt)
    loss = lax.psum(lpart[0, 0], ("x", "y", "c"))
    tok = tok + loss * 0.0

    grads, small, parts = [None] * LAYERS, [None] * LAYERS, {}
    to_shards = lambda g, group: [_to_shards(g[n], n) for n in group]
    nb = lambda group: [False] * len(group)
    st = None
    for l in reversed(range(LAYERS)):
        dx, cot, gb, small[l] = _bwd_b(dx, lws[l], saved[l], l, tok)
        if st is not None:
            parts[("a", l + 1)], tok, _ = exchange_wait(st, dx, f"scatter_a{l + 1}_wait")
        st, tok = exchange_start(_behind(to_shards(gb, GROUP_B), tok), nb(GROUP_B), f"scatter_b{l}_start")
        dx, ga, small[l] = _bwd_a(dx, cot, lws[l], saved[l], cos2, sin2, l, tok, small[l])
        parts[("b", l)], tok, _ = exchange_wait(st, dx, f"scatter_b{l}_wait")
        arrs, flags = to_shards(ga, GROUP_A), nb(GROUP_A)
        if l == 0:
            dmod = jnp.stack([small[q]["ada_b"] for q in range(LAYERS)])
            arrs += [_pack(small, SMALL), dmod.reshape(LAYERS, NDEV, 768).transpose(1, 0, 2)]
            flags += [True, False]
        st, tok = exchange_start(_behind(arrs, tok), flags, f"scatter_a{l}_start")

    out = {}

    def big_adamw(group, tok):
        res = None
        for n in group:
            grp, idx = ("a", GROUP_A.index(n)) if n in GROUP_A else ("b", GROUP_B.index(n))
            shp = p[n].shape
            flat = lambda a: a.reshape(shp[0] * shp[1], shp[2])
            res = adamw([parts[(grp, 0)][idx], parts[(grp, 1)][idx]], flat(p[n]), flat(mom[n]), flat(var[n]),
                        f"adamw_{n}", tok)
            out[n] = [r.reshape(shp) for r in res]
        return res[0]

    g_last = big_adamw(GROUP_B, tok)
    got, _, _ = exchange_wait(st, g_last, "scatter_a0_wait")
    parts[("a", 0)], small_all, dmod_in = got[0:3], got[3], got[4]
    big_adamw(GROUP_A, None)

    dmod16 = jnp.pad(dmod_in, ((0, 8), (0, 0), (0, 0)))
    g_ada = jnp.stack([tn_matmul(cact, dmod16[:, l], f"dw_ada{l}", out_dtype=F32) for l in range(LAYERS)])
    flat = lambda a: a.reshape(LAYERS * D, 768)
    out["ada_w"] = [r.reshape(ada_w.shape) for r in
                    adamw([flat(g_ada)[None]], flat(ada_w), flat(m_ada_w), flat(v_ada_w), "adamw_ada_w")]

    zeros = jnp.zeros((SMALL_ROWS, D), F32)
    g_small = _unpack(adamw([small_all], zeros, zeros, zeros, "sum_small")[0], SMALL)
    per = lambda d, nm: [{n: d[n][l] for n, _ in nm} for l in range(LAYERS)]
    res = adamw([_pack(per(g_small, SMALL_REPL), SMALL_REPL)[None]], _pack(per(p, SMALL_REPL), SMALL_REPL),
                _pack(per(mom, SMALL_REPL), SMALL_REPL), _pack(per(var, SMALL_REPL), SMALL_REPL), "adamw_small")
    res = [_unpack(r, SMALL_REPL) for r in res]
    for n, _ in SMALL_REPL:
        out[n] = [r[n] for r in res]
    for n, k, w in (("ssd_conv_w", 4, 192), ("ffn_conv_w", 3, 704)):
        g_mine = lax.dynamic_slice(g_small[n], (0, 0, me * w), (LAYERS, k, w))
        f2 = lambda a: jnp.pad(a.reshape(LAYERS * k, w), ((0, 8 - LAYERS * k), (0, 0)))
        res = adamw([f2(g_mine)[None]], f2(p[n]), f2(mom[n]), f2(var[n]), f"adamw_{n}")
        out[n] = [r[0:LAYERS * k].reshape(LAYERS, k, w) for r in res]

    outs = [loss, dx[None]]
    for q in range(4):
        outs += [out[n][q] for n in names]
    return tuple(outs)
```

```python
import functools
import math

import jax
import jax.numpy as jnp
from jax import lax
from jax.experimental import pallas as pl
from jax.experimental.pallas import tpu as pltpu

F32, BF16 = jnp.float32, jnp.bfloat16
EPS = 1e-6
D = 1024
NDEV = 8
LAYERS = 2
HEADS = 8
FFN = 2816
FFN_TILE = 1408
FFN_NT = FFN // FFN_TILE
ATT_SCALE = 96 ** -0.5
ROPE_THETA = 10000.0
LR, B1, B2, ADAM_EPS, WD, STEP = 0.001, 0.9, 0.999, 1e-08, 0.01, 10

O_G, O_XS, O_Z, O_PU, O_BC, O_CKV, O_KR, O_KRS, O_DT, O_QL = 0, 3072, 4096, 5120, 5632, 6144, 6400, 6528, 6656, 6912
NPROJ = 7296
CONST = dict(pipeline_mode=pl.Buffered(1))


def _pick(n, cap, mult=128):
    if n <= cap:
        return n
    best = None
    for t in range(mult, cap + 1, mult):
        if n % t == 0:
            best = t
    assert best is not None, (n, cap, mult)
    return best


def _sig(x):
    return 1.0 / (1.0 + jnp.exp(-x))


def _rms(x, w, n):
    return x * lax.rsqrt(jnp.sum(x * x, axis=-1, keepdims=True) / n + EPS) * w


def _raw(a, b, dims):
    return lax.dot_general(a.astype(BF16), b.astype(BF16), dims, preferred_element_type=F32)


_NN = (((1,), (0,)), ((), ()))
_NT = (((1,), (1,)), ((), ()))
_TN = (((0,), (0,)), ((), ()))
_BNN = (((2,), (1,)), ((0,), (0,)))
_BNT = (((2,), (2,)), ((0,), (0,)))
_BTN = (((1,), (1,)), ((0,), (0,)))


@jax.custom_vjp
def mm_nn(a, b):
    return _raw(a, b, _NN)


mm_nn.defvjp(lambda a, b: (_raw(a, b, _NN), (a, b)),
             lambda r, g: (_raw(g, r[1], _NT), _raw(r[0], g, _TN)))


@jax.custom_vjp
def mm_nc(a, b):
    return _raw(a, b, _NN)


mm_nc.defvjp(lambda a, b: (_raw(a, b, _NN), b),
             lambda b, g: (_raw(g, b, _NT), jnp.zeros_like(b)))


@jax.custom_vjp
def mm_nt(a, b):
    return _raw(a, b, _NT)


mm_nt.defvjp(lambda a, b: (_raw(a, b, _NT), (a, b)),
             lambda r, g: (_raw(g, r[1], _NN), _raw(g, r[0], _TN)))


@jax.custom_vjp
def bmm_nn(a, b):
    return _raw(a, b, _BNN)


bmm_nn.defvjp(lambda a, b: (_raw(a, b, _BNN), (a, b)),
              lambda r, g: (_raw(g, r[1], _BNT), _raw(r[0], g, _BTN)))


@jax.custom_vjp
def bmm_nt(a, b):
    return _raw(a, b, _BNT)


bmm_nt.defvjp(lambda a, b: (_raw(a, b, _BNT), (a, b)),
              lambda r, g: (_raw(g, r[1], _BNN), _raw(g, r[0], _BTN)))


@jax.custom_vjp
def softplus(x):
    t = jnp.exp(-jnp.abs(x))
    u = 1.0 + t
    one = u == 1.0
    l1p = jnp.where(one, t, jnp.log(u) * (t / jnp.where(one, 1.0, u - 1.0)))
    return jnp.maximum(x, 0.0) + l1p


softplus.defvjp(lambda x: (softplus(x), x), lambda x, g: (g * _sig(x),))


def _params(*sem):
    return pltpu.CompilerParams(dimension_semantics=sem, vmem_limit_bytes=56 * 1024 * 1024)


def all_to_all(arrs, bcast, name):
    n = len(arrs)
    out_shapes = [jax.ShapeDtypeStruct(((NDEV,) + a.shape) if b else a.shape, a.dtype) for a, b in zip(arrs, bcast)]

    def body(*refs):
        ins, outs, token = refs[:n], refs[n:2 * n], refs[2 * n]
        send_sems, recv_sems, local_sems = refs[2 * n + 1:]
        me, remote = _exchange_copies(ins, outs, bcast, send_sems, recv_sems)
        local = [pltpu.make_async_copy(ins[j] if bcast[j] else ins[j].at[me], outs[j].at[me], local_sems.at[j])
                 for j in range(n)]
        for cp in local + remote:
            cp.start()
        for cp in remote + local:
            cp.wait()
        token[...] = jnp.zeros_like(token)

    any_spec = pl.BlockSpec(memory_space=pl.ANY)
    res = pl.pallas_call(
        body, name=name, out_shape=out_shapes + [jax.ShapeDtypeStruct((8, 128), F32)], in_specs=[any_spec] * n,
        out_specs=[any_spec] * n + [pl.BlockSpec(memory_space=pltpu.VMEM)],
        scratch_shapes=[pltpu.SemaphoreType.DMA((7 * n,)), pltpu.SemaphoreType.DMA((7 * n,)),
                        pltpu.SemaphoreType.DMA((n,))],
        compiler_params=pltpu.CompilerParams(has_side_effects=True),
    )(*arrs)
    return res[:n], res[n]


def _peers():
    x, y, c = lax.axis_index("x"), lax.axis_index("y"), lax.axis_index("c")
    out = []
    for k in range(1, NDEV):
        px, py, pc = x ^ ((k >> 2) & 1), y ^ ((k >> 1) & 1), c ^ (k & 1)
        out.append(((px, py, pc), 4 * px + 2 * py + pc))
    return 4 * x + 2 * y + c, out


COPIES = {"all": 7, "chips": 3, "pass": 4}


def _exchange_copies(ins, lands, bcast, send_sems, recv_sems, mode="all"):
    x, y, c = lax.axis_index("x"), lax.axis_index("y"), lax.axis_index("c")
    me = 4 * x + 2 * y + c
    n, copies = len(ins), []

    def add(q, j, src, dst, dev):
        copies.append(pltpu.make_async_remote_copy(
            src_ref=src, dst_ref=dst, send_sem=send_sems.at[q * n + j], recv_sem=recv_sems.at[q * n + j],
            device_id=dev, device_id_type=pl.DeviceIdType.MESH))

    if mode == "pass":
        for q in range(4):
            slot = 4 * (x ^ (q >> 1)) + 2 * (y ^ (q & 1)) + c
            for j in range(n):
                add(q, j, ins[j] if q == 0 else lands[j].at[slot], lands[j].at[slot], (x, y, 1 - c))
        return me, copies
    for q, k in enumerate(range(1, NDEV) if mode == "all" else (2, 4, 6)):
        px, py, pc = x ^ ((k >> 2) & 1), y ^ ((k >> 1) & 1), c ^ (k & 1)
        for j in range(n):
            add(q, j, ins[j] if bcast[j] else ins[j].at[4 * px + 2 * py + pc], lands[j].at[me], (px, py, pc))
    return me, copies


_HBM = pl.BlockSpec(memory_space=pltpu.HBM)
_SEM = pl.BlockSpec(memory_space=pltpu.SEMAPHORE)
_EFFECT = pltpu.SideEffectType.DATAFLOW_SIDE_EFFECTING


def exchange_start(arrs, bcast, name, mode="all", lands=None):
    n, ncp = len(arrs), COPIES[mode] * len(arrs)
    land_shapes = [((NDEV,) + a.shape) if b else a.shape for a, b in zip(arrs, bcast)]
    if lands is None:
        lands = [lax.empty(s_, a.dtype) for s_, a in zip(land_shapes, arrs)]

    def body(*refs):
        in_refs, land_refs = refs[:n], refs[n:2 * n]
        send_sems, recv_sems = refs[2 * n], refs[2 * n + 1]
        token = refs[-1]
        _, copies = _exchange_copies(in_refs, land_refs, bcast, send_sems, recv_sems, mode)
        for cp in copies:
            cp.start()
        token[...] = jnp.zeros_like(token)

    hbm = lambda shp, a: pltpu.HBM(shp, a.dtype)
    res = pl.pallas_call(
        body, name=name,
        out_shape=[pltpu.SemaphoreType.DMA((ncp,)), pltpu.SemaphoreType.DMA((ncp,))]
                  + [hbm(a.shape, a) for a in arrs] + [hbm(s_, a) for s_, a in zip(land_shapes, arrs)]
                  + [jax.ShapeDtypeStruct((8, 128), F32)],
        in_specs=[_HBM] * (2 * n), out_specs=[_SEM, _SEM] + [_HBM] * (2 * n) + [pl.BlockSpec(memory_space=pltpu.VMEM)],
        input_output_aliases={i: 2 + i for i in range(2 * n)},
        compiler_params=pltpu.CompilerParams(has_side_effects=_EFFECT),
    )(*[pltpu.with_memory_space_constraint(a, pltpu.HBM) for a in arrs],
      *[pltpu.with_memory_space_constraint(a, pltpu.HBM) for a in lands])
    return (res[0], res[1], res[2:2 + n], res[2 + n:2 + 2 * n], tuple(bcast), mode), res[-1]


def exchange_wait(state, after, name):
    send_sems, recv_sems, ins, lands, bcast, mode = state
    n = len(ins)

    def body(*refs):
        in_refs, land_refs = refs[:n], refs[n:2 * n]
        s_sems, r_sems = refs[2 * n], refs[2 * n + 1]
        token = refs[-1]
        _, copies = _exchange_copies(in_refs, land_refs, bcast, s_sems, r_sems, mode)
        for cp in copies:
            cp.wait_send()
            cp.wait_recv()
        token[...] = jnp.zeros_like(token)

    res = pl.pallas_call(
        body, name=name,
        out_shape=[pltpu.HBM(a.shape, a.dtype) for a in ins] + [pltpu.HBM(a.shape, a.dtype) for a in lands]
                  + [jax.ShapeDtypeStruct((8, 128), F32)],
        in_specs=[_HBM] * (2 * n) + [_SEM, _SEM, pl.BlockSpec(memory_space=pl.ANY)],
        out_specs=[_HBM] * (2 * n) + [pl.BlockSpec(memory_space=pltpu.VMEM)],
        input_output_aliases={i: i for i in range(2 * n)},
        compiler_params=pltpu.CompilerParams(has_side_effects=_EFFECT),
    )(*ins, *lands, send_sems, recv_sems, after)
    if mode == "chips":
        return list(res[n:2 * n]), res[-1], list(res[:n])
    me = 4 * lax.axis_index("x") + 2 * lax.axis_index("y") + lax.axis_index("c")
    got = []
    for j in range(n):
        own = res[j][None] if bcast[j] else lax.dynamic_index_in_dim(res[j], me, 0, keepdims=True)
        got.append(lax.dynamic_update_slice_in_dim(res[n + j], own, me, axis=0))
    return got, res[-1], list(res[:n])


def gather_start(shards, name):
    return exchange_start(shards, [True] * len(shards), name + "_chips_start", mode="chips")


def gather_finish(state, after, name):
    lands, _, sent = exchange_wait(state, after, name + "_chips_wait")
    state, tok = exchange_start(sent, [True] * len(sent), name + "_pass_start", mode="pass", lands=lands)
    got, tok, _ = exchange_wait(state, tok, name + "_pass_wait")
    return got, tok


def norm_proj_fwd(x, vec, w, name):
    s, n = x.shape[0], w.shape[1]
    tr, tn = _pick(s, 512), _pick(n, 2560)
    ni, jdt, odt = s // tr, O_DT // tn, O_DT % tn

    def body(x_ref, v_ref, w_ref, o_ref, h_ref, dt_ref, h_scr):
        j, i = pl.program_id(0), pl.program_id(1)
        rows = pl.ds(pl.multiple_of(i * tr, tr), tr)

        @pl.when(j == 0)
        def _():
            h = _rms(x_ref[...], v_ref[0:1, :], D) * (1.0 + v_ref[2:3, :]) + v_ref[1:2, :]
            h_scr[rows, :] = h.astype(BF16)
            h_ref[...] = h.astype(BF16)
        res = jnp.dot(h_scr[rows, :], w_ref[...], preferred_element_type=F32)
        o_ref[...] = res

        @pl.when(j == jdt)
        def _():
            dt_ref[...] = res[:, odt:odt + 128]

    first = lambda j, i: (jnp.where(j == 0, i, ni - 1), 0)
    dtix = lambda j, i: (jnp.where(j < jdt, 0, jnp.where(j == jdt, i, ni - 1)), 0)
    return pl.pallas_call(
        body, name=name, grid=(n // tn, ni),
        in_specs=[pl.BlockSpec((tr, D), first), pl.BlockSpec((8, D), lambda j, i: (0, 0)),
                  pl.BlockSpec((D, tn), lambda j, i: (0, j))],
        out_specs=[pl.BlockSpec((tr, tn), lambda j, i: (i, j)), pl.BlockSpec((tr, D), first),
                   pl.BlockSpec((tr, 128), dtix)],
        out_shape=[jax.ShapeDtypeStruct((s, n), F32), jax.ShapeDtypeStruct((s, D), BF16),
                   jax.ShapeDtypeStruct((s, 128), F32)],
        scratch_shapes=[pltpu.VMEM((s, D), BF16)],
        compiler_params=_params("arbitrary", "arbitrary"),
    )(x, vec, w)


def _col_tiles(arr, cap):
    if arr.ndim == 2:
        n = arr.shape[1]
        t = _pick(n, cap)
        return n, t, lambda rows, ix: pl.BlockSpec((rows, t), lambda *g: ix(*g))
    width = arr.shape[2]
    t = _pick(width, cap)
    per = width // t

    def spec(rows, ix):
        def index(*g):
            r, j = ix(*g)
            return (j // per, r, j % per)
        return pl.BlockSpec((None, rows, t), index)
    return arr.shape[0] * width, t, spec


def norm_proj_bwd(x, vec, dp, w, dx_in, aux, name):
    s = x.shape[0]
    tr = _pick(s, 512)
    n, tk, dp_spec = _col_tiles(dp, 2560)
    nk, has_aux = n // tk, aux is not None

    def body(*refs):
        if has_aux:
            x_ref, v_ref, dp_ref, w_ref, dxin_ref, aux_ref, dx_ref, dv_ref, acc = refs
        else:
            x_ref, v_ref, dp_ref, w_ref, dxin_ref, dx_ref, dv_ref, acc = refs
        k, i = pl.program_id(0), pl.program_id(1)
        rows = pl.ds(pl.multiple_of(i * tr, tr), tr)
        part = _raw(dp_ref[...], w_ref[...], _NT)

        @pl.when(k == 0)
        def _():
            acc[rows, :] = part

        @pl.when(k > 0)
        def _():
            acc[rows, :] += part

        @pl.when(k == nk - 1)
        def _():
            f = lambda xx, nw, sh, sc: _rms(xx, nw, D) * (1.0 + sc) + sh
            _, vjp = jax.vjp(f, x_ref[...], v_ref[0:1, :], v_ref[1:2, :], v_ref[2:3, :])
            dx, dnw, dsh, dsc = vjp(acc[rows, :])
            dx_ref[...] = dxin_ref[...] + dx

            @pl.when(i == 0)
            def _():
                dv_ref[...] = jnp.zeros_like(dv_ref)

            dv_ref[0:1, :] += dnw
            dv_ref[1:2, :] += dsh
            dv_ref[2:3, :] += dsc
            if has_aux:
                dv_ref[3:4, :] += jnp.sum(dxin_ref[...] * aux_ref[...], axis=0, keepdims=True)

    row = pl.BlockSpec((tr, D), lambda k, i: (jnp.where(k == nk - 1, i, 0), 0))
    in_specs = [row, pl.BlockSpec((8, D), lambda k, i: (0, 0)), dp_spec(tr, lambda k, i: (i, k)),
                pl.BlockSpec((D, tk), lambda k, i: (0, k)), row] + ([row] if has_aux else [])
    args = [x, vec, dp, w, dx_in] + ([aux] if has_aux else [])
    return pl.pallas_call(
        body, name=name, grid=(nk, s // tr), in_specs=in_specs,
        out_specs=[row, pl.BlockSpec((8, D), lambda k, i: (0, 0))],
        out_shape=[jax.ShapeDtypeStruct((s, D), F32), jax.ShapeDtypeStruct((8, D), F32)],
        scratch_shapes=[pltpu.VMEM((s, D), F32)],
        compiler_params=_params("arbitrary", "arbitrary"),
    )(*args)


def tn_matmul(a, b, name, scale=None, out_dtype=None):
    out_dtype = BF16 if out_dtype is None else out_dtype
    s, m = a.shape
    ts, tm = _pick(s, 512, 16), _pick(m, 1408)
    n, tn, b_spec = _col_tiles(b, 2560)
    ns, has_scale = s // ts, scale is not None

    def body(*refs):
        if has_scale:
            a_ref, b_ref, sc_ref, o_ref, acc = refs
        else:
            a_ref, b_ref, o_ref, acc = refs
        k = pl.program_id(2)

        @pl.when(k == 0)
        def _():
            acc[...] = jnp.zeros_like(acc)

        acc[...] += _raw(a_ref[...], b_ref[...], _TN)

        @pl.when(k == ns - 1)
        def _():
            o_ref[...] = (acc[...] * sc_ref[...] if has_scale else acc[...]).astype(out_dtype)

    in_specs = [pl.BlockSpec((ts, tm), lambda i, j, k: (k, i)), b_spec(ts, lambda i, j, k: (k, j))]
    if has_scale:
        in_specs.append(pl.BlockSpec((1, tn), lambda i, j, k: (0, j)))
    return pl.pallas_call(
        body, name=name, grid=(m // tm, n // tn, ns), in_specs=in_specs,
        out_specs=pl.BlockSpec((tm, tn), lambda i, j, k: (i, j)),
        out_shape=jax.ShapeDtypeStruct((m, n), out_dtype),
        scratch_shapes=[pltpu.VMEM((tm, tn), F32)],
        compiler_params=_params("arbitrary", "arbitrary", "arbitrary"),
    )(*([a, b] + ([scale] if has_scale else [])))


def ada_mod(c16, w):
    ncol = w.shape[2]

    def body(c_ref, w_ref, o_ref, a_ref):
        cc = c_ref[...]
        act = cc * _sig(cc)
        a_ref[...] = act
        o_ref[...] = _raw(act, w_ref[...], _NN)

    return pl.pallas_call(
        body, name="ada_mod", grid=(LAYERS,),
        in_specs=[pl.BlockSpec((16, D), lambda l: (0, 0)), pl.BlockSpec((None, D, ncol), lambda l: (l, 0, 0))],
        out_specs=[pl.BlockSpec((None, 16, ncol), lambda l: (l, 0, 0)), pl.BlockSpec((16, D), lambda l: (0, 0))],
        out_shape=[jax.ShapeDtypeStruct((LAYERS, 16, ncol), F32), jax.ShapeDtypeStruct((16, D), F32)],
        compiler_params=_params("arbitrary"),
    )(c16, w)


def _mla_head(q_lat, c_kv, kr, krs, wqn, wqr, wqrs, wkn, wv, qa_w, kva_w, qn_w, qr_w, qrs_w, kn_w, kr_w, krs_w,
              cos2, sin2):
    qn = _rms(q_lat, qa_w, 384.0)
    kvn = _rms(c_kv, kva_w, 256.0)
    qnope = _rms(mm_nn(qn, wqn), qn_w, 64.0)
    qr, qrs = mm_nn(qn, wqr), mm_nn(qn, wqrs)
    rq = lax.rsqrt(jnp.sum(qr * qr, axis=-1, keepdims=True) / 32.0 + EPS)
    qrope = rq * (qr * qr_w * cos2 + qrs * qrs_w * sin2)
    knope = _rms(mm_nn(kvn, wkn), kn_w, 64.0)
    v = mm_nn(kvn, wv)
    rk = lax.rsqrt(jnp.sum(kr * kr, axis=-1, keepdims=True) / 32.0 + EPS)
    krope = rk * (kr * kr_w * cos2 + krs * krs_w * sin2)
    return qnope, qrope, knope, krope, v


def _mla_vec_pieces(v_ref):
    return (v_ref[0:1, 0:384], v_ref[1:2, 0:256], v_ref[2:3, 0:128], v_ref[2:3, 128:256], v_ref[2:3, 256:384],
            v_ref[3:4, 0:128], v_ref[3:4, 128:256], v_ref[3:4, 256:384])


def _mla_in_specs(tr):
    return [pl.BlockSpec((tr, 384), lambda i: (i, O_QL // 384)), pl.BlockSpec((tr, 256), lambda i: (i, O_CKV // 256)),
            pl.BlockSpec((tr, 128), lambda i: (i, O_KR // 128)), pl.BlockSpec((tr, 128), lambda i: (i, O_KRS // 128)),
            pl.BlockSpec((HEADS, 384, 384), lambda i: (0, 0, 0), **CONST),
            pl.BlockSpec((HEADS, 256, 256), lambda i: (0, 0, 0), **CONST),
            pl.BlockSpec((8, 512), lambda i: (0, 0)),
            pl.BlockSpec((tr, 128), lambda i: (i, 0)), pl.BlockSpec((tr, 128), lambda i: (i, 0))]


def mla_pre_fwd(proj, wq, wkv, vec, cos2, sin2, name):
    s = proj.shape[0]
    tr = _pick(s, 256)

    def body(ql_ref, ckv_ref, kr_ref, krs_ref, wq_ref, wkv_ref, v_ref, cos_ref, sin_ref, q_out, k_out, v_out):
        acts = (ql_ref[...], ckv_ref[...], kr_ref[...], krs_ref[...])
        vp = _mla_vec_pieces(v_ref)
        for h in range(HEADS):
            ws = (wq_ref[h, :, 0:128], wq_ref[h, :, 128:256], wq_ref[h, :, 256:384],
                  wkv_ref[h, :, 0:128], wkv_ref[h, :, 128:256])
            qn, qr, kn, krp, v = _mla_head(*acts, *ws, *vp, cos_ref[...], sin_ref[...])
            q_out[h, :, 0:128] = qn.astype(BF16)
            q_out[h, :, 128:256] = qr.astype(BF16)
            k_out[h, :, 0:128] = kn.astype(BF16)
            k_out[h, :, 128:256] = krp.astype(BF16)
            v_out[h] = v.astype(BF16)

    return pl.pallas_call(
        body, name=name, grid=(s // tr,), in_specs=_mla_in_specs(tr),
        out_specs=[pl.BlockSpec((HEADS, tr, 256), lambda i: (0, i, 0)), pl.BlockSpec((HEADS, tr, 256), lambda i: (0, i, 0)),
                   pl.BlockSpec((HEADS, tr, 128), lambda i: (0, i, 0))],
        out_shape=[jax.ShapeDtypeStruct((HEADS, s, 256), BF16), jax.ShapeDtypeStruct((HEADS, s, 256), BF16),
                   jax.ShapeDtypeStruct((HEADS, s, 128), BF16)],
        compiler_params=_params("arbitrary"),
    )(proj, proj, proj, proj, wq, wkv, vec, cos2, sin2)


def mla_pre_bwd(proj, wq, wkv, vec, cos2, sin2, dq, dk, dv, name):
    s = proj.shape[0]
    tr = _pick(s, 256)

    def body(ql_ref, ckv_ref, kr_ref, krs_ref, wq_ref, wkv_ref, v_ref, cos_ref, sin_ref, dq_ref, dk_ref, dv_ref,
             dql_out, dckv_out, dkr_out, dkrs_out, dwq_out, dwkv_out, dvec_out):
        @pl.when(pl.program_id(0) == 0)
        def _():
            dwq_out[...] = jnp.zeros_like(dwq_out)
            dwkv_out[...] = jnp.zeros_like(dwkv_out)
            dvec_out[...] = jnp.zeros_like(dvec_out)

        acts = (ql_ref[...], ckv_ref[...], kr_ref[...], krs_ref[...])
        vp = _mla_vec_pieces(v_ref)
        cos2_, sin2_ = cos_ref[...], sin_ref[...]

        def head(h, carry):
            wq_h, wkv_h = wq_ref[h].astype(F32), wkv_ref[h].astype(F32)
            ws = (wq_h[:, 0:128], wq_h[:, 128:256], wq_h[:, 256:384], wkv_h[:, 0:128], wkv_h[:, 128:256])
            f = lambda *a: _mla_head(*a, cos2_, sin2_)
            _, vjp = jax.vjp(f, *acts, *ws, *vp)
            dq_h, dk_h = dq_ref[h], dk_ref[h]
            g = vjp((dq_h[:, 0:128], dq_h[:, 128:256], dk_h[:, 0:128], dk_h[:, 128:256], dv_ref[h]))
            dwq_out[h, :, 0:128] += g[4]
            dwq_out[h, :, 128:256] += g[5]
            dwq_out[h, :, 256:384] += g[6]
            dwkv_out[h, :, 0:128] += g[7]
            dwkv_out[h, :, 128:256] += g[8]
            dvec_out[0:1, 0:384] += g[9]
            dvec_out[1:2, 0:256] += g[10]
            dvec_out[2:3, 0:128] += g[11]
            dvec_out[2:3, 128:256] += g[12]
            dvec_out[2:3, 256:384] += g[13]
            dvec_out[3:4, 0:128] += g[14]
            dvec_out[3:4, 128:256] += g[15]
            dvec_out[3:4, 256:384] += g[16]
            return tuple(c + gg for c, gg in zip(carry, g[:4]))

        tot = lax.fori_loop(0, HEADS, head, tuple(jnp.zeros_like(a) for a in acts))
        dql_out[...] = tot[0].astype(BF16)
        dckv_out[...] = tot[1].astype(BF16)
        dkr_out[...] = tot[2].astype(BF16)
        dkrs_out[...] = tot[3].astype(BF16)

    hb = lambda w: pl.BlockSpec((HEADS, tr, w), lambda i: (0, i, 0))
    return pl.pallas_call(
        body, name=name, grid=(s // tr,), in_specs=_mla_in_specs(tr) + [hb(256), hb(256), hb(128)],
        out_specs=[pl.BlockSpec((tr, 384), lambda i: (i, 0)), pl.BlockSpec((tr, 256), lambda i: (i, 0)),
                   pl.BlockSpec((tr, 128), lambda i: (i, 0)), pl.BlockSpec((tr, 128), lambda i: (i, 0)),
                   pl.BlockSpec((HEADS, 384, 384), lambda i: (0, 0, 0)), pl.BlockSpec((HEADS, 256, 256), lambda i: (0, 0, 0)),
                   pl.BlockSpec((8, 512), lambda i: (0, 0))],
        out_shape=[jax.ShapeDtypeStruct((s, 384), BF16), jax.ShapeDtypeStruct((s, 256), BF16),
                   jax.ShapeDtypeStruct((s, 128), BF16), jax.ShapeDtypeStruct((s, 128), BF16),
                   jax.ShapeDtypeStruct((HEADS, 384, 384), F32), jax.ShapeDtypeStruct((HEADS, 256, 256), F32),
                   jax.ShapeDtypeStruct((8, 512), F32)],
        compiler_params=_params("arbitrary"),
    )(proj, proj, proj, proj, wq, wkv, vec, cos2, sin2, dq, dk, dv)


def _att_probs(q, kk, i, tq):
    sc = _raw(q, kk, _NT) * ATT_SCALE
    rows = lax.broadcasted_iota(jnp.int32, sc.shape, 0) + i * tq
    cols = lax.broadcasted_iota(jnp.int32, sc.shape, 1)
    sc = jnp.where(cols <= rows, sc, -jnp.inf)
    e = jnp.exp(sc - jnp.max(sc, axis=-1, keepdims=True))
    return e / jnp.sum(e, axis=-1, keepdims=True)


def mla_attn_fwd(q, k, v, name):
    s = q.shape[1]
    tq = _pick(s, 256)

    def body(q_ref, k_ref, v_ref, o_ref):
        for i in range(s // tq):
            n = (i + 1) * tq
            p = _att_probs(q_ref[i * tq:n, :], k_ref[0:n, :], i, tq)
            o_ref[i * tq:n, :] = _raw(p, v_ref[0:n, :], _NN)

    hs = lambda w: pl.BlockSpec((None, s, w), lambda h: (h, 0, 0))
    return pl.pallas_call(
        body, name=name, grid=(HEADS,), in_specs=[hs(256), hs(256), hs(128)],
        out_specs=pl.BlockSpec((s, 128), lambda h: (0, h)),
        out_shape=jax.ShapeDtypeStruct((s, HEADS * 128), F32),
        compiler_params=_params("arbitrary"),
    )(q, k, v)


def mla_attn_bwd(q, k, v, do, name):
    s = q.shape[1]
    tq = _pick(s, 256)

    def body(q_ref, k_ref, v_ref, do_ref, dq_ref, dk_ref, dv_ref):
        dk_ref[...] = jnp.zeros_like(dk_ref)
        dv_ref[...] = jnp.zeros_like(dv_ref)
        for i in range(s // tq):
            n = (i + 1) * tq
            qq, kk, vv = q_ref[i * tq:n, :], k_ref[0:n, :], v_ref[0:n, :]
            p = _att_probs(qq, kk, i, tq)
            o = _raw(p, vv, _NN)
            dout = do_ref[i * tq:n, :]
            delta = jnp.sum(dout * o, axis=-1, keepdims=True)
            dp = _raw(dout, vv, _NT)
            ds = p * (dp - delta) * ATT_SCALE
            dq_ref[i * tq:n, :] = _raw(ds, kk, _NN)
            dk_ref[0:n, :] += _raw(ds, qq, _TN)
            dv_ref[0:n, :] += _raw(p, dout, _TN)

    hs = lambda w: pl.BlockSpec((None, s, w), lambda h: (h, 0, 0))
    return pl.pallas_call(
        body, name=name, grid=(HEADS,),
        in_specs=[hs(256), hs(256), hs(128), pl.BlockSpec((s, 128), lambda h: (0, h))],
        out_specs=[hs(256), hs(256), hs(128)],
        out_shape=[jax.ShapeDtypeStruct((HEADS, s, 256), F32), jax.ShapeDtypeStruct((HEADS, s, 256), F32),
                   jax.ShapeDtypeStruct((HEADS, s, 128), F32)],
        compiler_params=_params("arbitrary"),
    )(q, k, v, do)


def _pool_windows(u, pad, s, g):
    pad[0:16, :] = jnp.zeros((16, 128), F32)
    cur, sel = u, None
    for j, k in enumerate((1, 2, 4, 8)):
        pad[16:16 + s, :] = cur
        cur = cur + pad[16 - k:16 - k + s, :]
        sel = cur if sel is None else jnp.where(g == j, cur, sel)
    return sel


def _pool_count(s, g):
    t = lax.broadcasted_iota(jnp.int32, (s, 1), 0)
    return jnp.minimum(t + 1, 2 << g).astype(F32)


def pool_fwd(proj, pw, ps, name):
    s = proj.shape[0]

    def body(u_ref, w_ref, s_ref, o_ref, pad):
        g = pl.program_id(0)
        u = u_ref[...]
        pooled = _pool_windows(u, pad, s, g) / _pool_count(s, g) - u
        o_ref[...] = _raw(pooled, w_ref[...], _NN) * s_ref[...]

    return pl.pallas_call(
        body, name=name, grid=(4,),
        in_specs=[pl.BlockSpec((s, 128), lambda g: (0, O_PU // 128 + g)), pl.BlockSpec((None, 128, 128), lambda g: (g, 0, 0)),
                  pl.BlockSpec((1, 128), lambda g: (0, g))],
        out_specs=pl.BlockSpec((s, 128), lambda g: (0, g)),
        out_shape=jax.ShapeDtypeStruct((s, 512), F32),
        scratch_shapes=[pltpu.VMEM((s + 16, 128), F32)],
        compiler_params=_params("arbitrary"),
    )(proj, pw, ps)


def pool_bwd(proj, pw, ps, do, name):
    s = proj.shape[0]

    def body(u_ref, w_ref, s_ref, do_ref, du_ref, dw_ref, ds_ref, pad):
        g = pl.program_id(0)
        u, w, dout = u_ref[...], w_ref[...], do_ref[...]
        cnt = _pool_count(s, g)
        pooled = _pool_windows(u, pad, s, g) / cnt - u
        mixed = _raw(pooled, w, _NN)
        ds_ref[...] = jnp.sum(dout * mixed, axis=0, keepdims=True)
        dmixed = dout * s_ref[...]
        dw_ref[...] = _raw(pooled, dmixed, _TN)
        dpooled = _raw(dmixed, w, _NT)
        dsel = dpooled / cnt
        pad[s:s + 16, :] = jnp.zeros((16, 128), F32)
        cur = jnp.where(g == 3, dsel, 0.0)
        for j, k in ((2, 8), (1, 4), (0, 2)):
            pad[0:s, :] = cur
            cur = cur + pad[k:k + s, :] + jnp.where(g == j, dsel, 0.0)
        pad[0:s, :] = cur
        cur = cur + pad[1:1 + s, :]
        du_ref[...] = (cur - dpooled).astype(BF16)

    return pl.pallas_call(
        body, name=name, grid=(4,),
        in_specs=[pl.BlockSpec((s, 128), lambda g: (0, O_PU // 128 + g)), pl.BlockSpec((None, 128, 128), lambda g: (g, 0, 0)),
                  pl.BlockSpec((1, 128), lambda g: (0, g)), pl.BlockSpec((s, 128), lambda g: (0, g))],
        out_specs=[pl.BlockSpec((s, 128), lambda g: (0, g)), pl.BlockSpec((None, 128, 128), lambda g: (g, 0, 0)),
                   pl.BlockSpec((1, 128), lambda g: (0, g))],
        out_shape=[jax.ShapeDtypeStruct((s, 512), BF16), jax.ShapeDtypeStruct((4, 128, 128), F32),
                   jax.ShapeDtypeStruct((1, 512), F32)],
        scratch_shapes=[pltpu.VMEM((s + 16, 128), F32)],
        compiler_params=_params("arbitrary"),
    )(proj, pw, ps, do)


def _xbc_col(i):
    return jnp.where(i < 2, O_XS // 512 + i, O_BC // 512)


def conv_fwd(proj, cw, cb, name):
    s = proj.shape[0]

    def body(x_ref, w_ref, b_ref, o_ref, t_ref, pad):
        pad[0:8, :] = jnp.zeros((8, 512), F32)
        pad[8:8 + s, :] = x_ref[...]
        y = b_ref[...] + sum(w_ref[k:k + 1, :] * pad[5 + k:5 + k + s, :] for k in range(4))
        act = y * _sig(y)
        o_ref[...] = act

        @pl.when(pl.program_id(0) < 2)
        def _():
            t_ref[...] = act.T

    return pl.pallas_call(
        body, name=name, grid=(3,),
        in_specs=[pl.BlockSpec((s, 512), lambda i: (0, _xbc_col(i))), pl.BlockSpec((4, 512), lambda i: (0, i)),
                  pl.BlockSpec((1, 512), lambda i: (0, i))],
        out_specs=[pl.BlockSpec((s, 512), lambda i: (0, i)), pl.BlockSpec((512, s), lambda i: (jnp.minimum(i, 1), 0))],
        out_shape=[jax.ShapeDtypeStruct((s, 1536), F32), jax.ShapeDtypeStruct((D, s), F32)],
        scratch_shapes=[pltpu.VMEM((s + 8, 512), F32)],
        compiler_params=_params("arbitrary"),
    )(proj, cw, cb)


def conv_bwd(proj, cw, cb, dxt, dbm, dcm, name):
    s = proj.shape[0]

    def body(x_ref, w_ref, b_ref, dxt_ref, dbm_ref, dcm_ref, dx_ref, dw_ref, db_ref, pad, pad2):
        pad[0:8, :] = jnp.zeros((8, 512), F32)
        pad[8:8 + s, :] = x_ref[...]
        y = b_ref[...] + sum(w_ref[k:k + 1, :] * pad[5 + k:5 + k + s, :] for k in range(4))
        sg = _sig(y)

        @pl.when(pl.program_id(0) < 2)
        def _():
            pad2[0:s, :] = dxt_ref[...].T

        @pl.when(pl.program_id(0) == 2)
        def _():
            pad2[0:s, 0:256] = dbm_ref[...]
            pad2[0:s, 256:512] = dcm_ref[...]

        dy = pad2[0:s, :] * (sg * (1.0 + y * (1.0 - sg)))
        db_ref[...] = jnp.sum(dy, axis=0, keepdims=True)
        for k in range(4):
            dw_ref[k:k + 1, :] = jnp.sum(dy * pad[5 + k:5 + k + s, :], axis=0, keepdims=True)
        pad2[s:s + 8, :] = jnp.zeros((8, 512), F32)
        pad2[0:s, :] = dy
        dx_ref[...] = sum(w_ref[k:k + 1, :] * pad2[3 - k:3 - k + s, :] for k in range(4)).astype(BF16)

    return pl.pallas_call(
        body, name=name, grid=(3,),
        in_specs=[pl.BlockSpec((s, 512), lambda i: (0, _xbc_col(i))), pl.BlockSpec((4, 512), lambda i: (0, i)),
                  pl.BlockSpec((1, 512), lambda i: (0, i)), pl.BlockSpec((512, s), lambda i: (jnp.minimum(i, 1), 0)),
                  pl.BlockSpec((s, 256), lambda i: (0, 0)), pl.BlockSpec((s, 256), lambda i: (0, 0))],
        out_specs=[pl.BlockSpec((s, 512), lambda i: (0, i)), pl.BlockSpec((4, 512), lambda i: (0, i)),
                   pl.BlockSpec((1, 512), lambda i: (0, i))],
        out_shape=[jax.ShapeDtypeStruct((s, 1536), BF16), jax.ShapeDtypeStruct((4, 1536), F32),
                   jax.ShapeDtypeStruct((1, 1536), F32)],
        scratch_shapes=[pltpu.VMEM((s + 8, 512), F32), pltpu.VMEM((s + 8, 512), F32)],
        compiler_params=_params("arbitrary"),
    )(proj, cw, cb, dxt, dbm, dcm)


def _ssd_chunk(xt, dtr, dtc, bm, cm, hprev, alog, dbias, dskip):
    ln = 128
    a = -jnp.exp(alog)
    dt_r = softplus(dtr + dbias)
    da_r = dt_r * a
    da_c = softplus(dtc + dbias) * a
    li = lax.broadcasted_iota(jnp.int32, (1, ln, ln), 1)
    si = lax.broadcasted_iota(jnp.int32, (1, ln, ln), 2)
    causal = si <= li
    acs_c = jnp.sum(jnp.where(causal, da_r, 0.0), axis=2, keepdims=True)
    acs_r = jnp.sum(jnp.where(li <= si, da_c, 0.0), axis=1, keepdims=True)
    acs_last = jnp.sum(da_r, axis=2, keepdims=True)
    decay = jnp.exp(jnp.where(causal, acs_c - acs_r, -jnp.inf))
    m = mm_nt(cm, bm)[None] * decay
    xdt = xt * dt_r
    y_diag = bmm_nt(xdt, m)
    bb = jnp.broadcast_to(bm[None], (8, ln, ln))
    cc = jnp.broadcast_to(cm[None], (8, ln, ln))
    states = bmm_nn(xdt * jnp.exp(acs_last - acs_r), bb)
    y_off = bmm_nt(hprev, cc) * jnp.exp(acs_r)
    hnew = hprev * jnp.exp(acs_last) + states
    return y_diag + y_off + xt * dskip, hnew


def _ssd_specs(nc, rev):
    cix = (lambda c: nc - 1 - c) if rev else (lambda c: c)
    hv = pl.BlockSpec((8, 1, 1), lambda g, c: (g, 0, 0))
    return [pl.BlockSpec((8, 64, 128), lambda g, c: (g, 0, cix(c))), pl.BlockSpec((8, 1, 128), lambda g, c: (g, 0, cix(c))),
            pl.BlockSpec((8, 128, 1), lambda g, c: (g, cix(c), 0)), pl.BlockSpec((128, 128), lambda g, c: (cix(c), 8 + g)),
            pl.BlockSpec((128, 128), lambda g, c: (cix(c), 10 + g))], hv, cix


def ssd_fwd(xt, dtr, dtc, xbc, alog, dbias, dskip, name):
    s = xt.shape[2]
    nc = s // 128
    specs, hv, _ = _ssd_specs(nc, False)

    def body(x_ref, dr_ref, dc_ref, b_ref, c_ref, al_ref, db_ref, dk_ref, y_ref, hs_ref, h_scr):
        @pl.when(pl.program_id(1) == 0)
        def _():
            h_scr[...] = jnp.zeros_like(h_scr)
        hp = h_scr[...]
        hs_ref[...] = hp
        y, hn = _ssd_chunk(x_ref[...], dr_ref[...], dc_ref[...], b_ref[...], c_ref[...], hp,
                           al_ref[...], db_ref[...], dk_ref[...])
        y_ref[...] = y
        h_scr[...] = hn

    return pl.pallas_call(
        body, name=name, grid=(2, nc), in_specs=specs + [hv, hv, hv],
        out_specs=[pl.BlockSpec((8, 64, 128), lambda g, c: (g, 0, c)),
                   pl.BlockSpec((None, None, 8, 64, 128), lambda g, c: (g, c, 0, 0, 0))],
        out_shape=[jax.ShapeDtypeStruct((16, 64, s), F32), jax.ShapeDtypeStruct((2, nc, 8, 64, 128), F32)],
        scratch_shapes=[pltpu.VMEM((8, 64, 128), F32)],
        compiler_params=_params("arbitrary", "arbitrary"),
    )(xt, dtr, dtc, xbc, xbc, alog, dbias, dskip)


def ssd_bwd(xt, dtr, dtc, xbc, alog, dbias, dskip, hs, dyt, name):
    s = xt.shape[2]
    nc = s // 128
    specs, hv, cix = _ssd_specs(nc, True)

    def body(x_ref, dr_ref, dc_ref, b_ref, c_ref, al_ref, db_ref, dk_ref, hs_ref, dy_ref,
             dx_out, ddr_out, ddc_out, dbm_out, dcm_out, dal_out, ddb_out, ddk_out, dh_scr):
        @pl.when(pl.program_id(1) == 0)
        def _():
            dh_scr[...] = jnp.zeros_like(dh_scr)
            dal_out[...] = jnp.zeros_like(dal_out)
            ddb_out[...] = jnp.zeros_like(ddb_out)
            ddk_out[...] = jnp.zeros_like(ddk_out)
        _, vjp = jax.vjp(_ssd_chunk, x_ref[...], dr_ref[...], dc_ref[...], b_ref[...], c_ref[...], hs_ref[...],
                         al_ref[...], db_ref[...], dk_ref[...])
        g = vjp((dy_ref[...], dh_scr[...]))
        dx_out[...] = g[0]
        ddr_out[...] = g[1]
        ddc_out[...] = g[2]
        dbm_out[...] = g[3]
        dcm_out[...] = g[4]
        dh_scr[...] = g[5]
        dal_out[...] += g[6]
        ddb_out[...] += g[7]
        ddk_out[...] += g[8]

    return pl.pallas_call(
        body, name=name, grid=(2, nc),
        in_specs=specs + [hv, hv, hv, pl.BlockSpec((None, None, 8, 64, 128), lambda g, c: (g, cix(c), 0, 0, 0)),
                          pl.BlockSpec((8, 64, 128), lambda g, c: (g, 0, cix(c)))],
        out_specs=[pl.BlockSpec((8, 64, 128), lambda g, c: (g, 0, cix(c))), pl.BlockSpec((8, 1, 128), lambda g, c: (g, 0, cix(c))),
                   pl.BlockSpec((8, 128, 1), lambda g, c: (g, cix(c), 0)), pl.BlockSpec((128, 128), lambda g, c: (cix(c), g)),
                   pl.BlockSpec((128, 128), lambda g, c: (cix(c), g)), hv, hv, hv],
        out_shape=[jax.ShapeDtypeStruct((16, 64, s), F32), jax.ShapeDtypeStruct((16, 1, s), F32),
                   jax.ShapeDtypeStruct((16, s, 1), F32), jax.ShapeDtypeStruct((s, 256), F32),
                   jax.ShapeDtypeStruct((s, 256), F32)] + [jax.ShapeDtypeStruct((16, 1, 1), F32)] * 3,
        scratch_shapes=[pltpu.VMEM((8, 64, 128), F32)],
        compiler_params=_params("arbitrary", "arbitrary"),
    )(xt, dtr, dtc, xbc, xbc, alog, dbias, dskip, hs, dyt)


def _merge(oa, ob, y, z, gla, glb, glc, x, g1, nw, ea, eb, ec, eo, wba, wbb, wbc, wout):
    gated = y * (z * _sig(z))
    sq = gated * gated
    left = lax.broadcasted_iota(jnp.int32, (1, D), 1) < 512
    ms0 = jnp.sum(jnp.where(left, sq, 0.0), axis=-1, keepdims=True) / 512.0
    ms1 = jnp.sum(jnp.where(left, 0.0, sq), axis=-1, keepdims=True) / 512.0
    oc = gated * jnp.where(left, lax.rsqrt(ms0 + EPS), lax.rsqrt(ms1 + EPS)) * nw
    ya, yb, yc = mm_nc(oa, wba) + ea, mm_nc(ob, wbb) + eb, mm_nc(oc, wbc) + ec
    merged = _sig(gla) * ya + _sig(glb) * yb + _sig(glc) * yc
    x1 = x + g1 * (mm_nc(merged, wout) + eo)
    return x1, (oc, merged)


def _merge_specs(tr):
    row = lambda w: pl.BlockSpec((tr, w), lambda i: (i, 0))
    acts = [row(D), row(512), pl.BlockSpec((D, tr), lambda i: (0, i)), pl.BlockSpec((tr, D), lambda i: (i, O_Z // D)),
            pl.BlockSpec((tr, 3 * D), lambda i: (i, 0)), row(D), pl.BlockSpec((8, D), lambda i: (0, 0))]
    cst = lambda r: pl.BlockSpec((r, D), lambda i: (0, 0), **CONST)
    return acts, [cst(D), cst(512), cst(D), cst(D)], row


def merge_fwd(oa, ob, y, proj, x, mvec, wba, wbb, wbc, wout, name):
    s = x.shape[0]
    tr = _pick(s, 256)
    acts, wts, row = _merge_specs(tr)

    def body(oa_ref, ob_ref, y_ref, z_ref, gl_ref, x_ref, mv_ref, wba_ref, wbb_ref, wbc_ref, wout_ref, o_ref):
        zero = jnp.zeros((1, D), F32)
        x1, _ = _merge(oa_ref[...], ob_ref[...], y_ref[...].T, z_ref[...], gl_ref[:, 0:D], gl_ref[:, D:2 * D],
                       gl_ref[:, 2 * D:3 * D], x_ref[...], mv_ref[0:1, :], mv_ref[1:2, :], zero, zero, zero, zero,
                       wba_ref[...], wbb_ref[...], wbc_ref[...], wout_ref[...])
        o_ref[...] = x1

    return pl.pallas_call(
        body, name=name, grid=(s // tr,), in_specs=acts + wts, out_specs=row(D),
        out_shape=jax.ShapeDtypeStruct((s, D), F32), compiler_params=_params("arbitrary"),
    )(oa, ob, y, proj, proj, x, mvec, wba, wbb, wbc, wout)


def merge_bwd(oa, ob, y, proj, x, mvec, wba, wbb, wbc, wout, dx1, name):
    s = x.shape[0]
    tr = _pick(s, 128)
    acts, wts, row = _merge_specs(tr)

    def body(oa_ref, ob_ref, y_ref, z_ref, gl_ref, x_ref, mv_ref, wba_ref, wbb_ref, wbc_ref, wout_ref, dx1_ref,
             doa_o, dob_o, dy_o, dz_o, dgl_o, dx_o, dmv_o, dya_o, dyb_o, dyc_o, dpre_o, oc_o, mg_o):
        zero = jnp.zeros((tr, D), F32)
        wts_ = (wba_ref[...], wbb_ref[...], wbc_ref[...], wout_ref[...])
        f = lambda *a: _merge(*a, *wts_)
        _, vjp, (oc, merged) = jax.vjp(
            f, oa_ref[...], ob_ref[...], y_ref[...].T, z_ref[...], gl_ref[:, 0:D], gl_ref[:, D:2 * D],
            gl_ref[:, 2 * D:3 * D], x_ref[...], mv_ref[0:1, :], mv_ref[1:2, :], zero, zero, zero, zero, has_aux=True)
        g = vjp(dx1_ref[...])
        doa_o[...] = g[0]
        dob_o[...] = g[1]
        dy_o[...] = g[2].T
        dz_o[...] = g[3].astype(BF16)
        dgl_o[:, 0:D] = g[4].astype(BF16)
        dgl_o[:, D:2 * D] = g[5].astype(BF16)
        dgl_o[:, 2 * D:3 * D] = g[6].astype(BF16)
        dx_o[...] = g[7]

        @pl.when(pl.program_id(0) == 0)
        def _():
            dmv_o[...] = jnp.zeros_like(dmv_o)

        dmv_o[0:1, :] += g[8]
        dmv_o[1:2, :] += g[9]
        dya_o[...] = g[10].astype(BF16)
        dyb_o[...] = g[11].astype(BF16)
        dyc_o[...] = g[12].astype(BF16)
        dpre_o[...] = g[13].astype(BF16)
        oc_o[...] = oc.astype(BF16)
        mg_o[...] = merged.astype(BF16)

    sd = lambda w, dt: jax.ShapeDtypeStruct((s, w), dt)
    return pl.pallas_call(
        body, name=name, grid=(s // tr,), in_specs=acts + wts + [row(D)],
        out_specs=[row(D), row(512), pl.BlockSpec((D, tr), lambda i: (0, i)), row(D), row(3 * D), row(D),
                   pl.BlockSpec((8, D), lambda i: (0, 0))] + [row(D)] * 6,
        out_shape=[sd(D, F32), sd(512, F32), jax.ShapeDtypeStruct((D, s), F32), sd(D, BF16), sd(3 * D, BF16), sd(D, F32),
                   jax.ShapeDtypeStruct((8, D), F32)] + [sd(D, BF16)] * 6,
        compiler_params=_params("arbitrary"),
    )(oa, ob, y, proj, proj, x, mvec, wba, wbb, wbc, wout, dx1)


def _conv3(u_scr, w_ref, first, rows, lanes):
    return sum(w_ref[k:k + 1, :] * u_scr[first + k:first + k + rows, lanes] for k in range(3))


def _ffn_tile_specs(tf, tile):
    def at(rows, off):
        return pl.BlockSpec((rows, tf), lambda *g: (0, off + tile(*g)))
    return [at(D, 0), at(D, FFN_NT), at(3, 0), at(3, FFN_NT), at(1, 0), at(1, FFN_NT)]


def ffn_fwd(x1, fvec, wup, cw, cb, wdn, name):
    s = x1.shape[0]
    tr, tf = _pick(s, 512), FFN_TILE
    lg, lv = slice(0, tf), slice(tf, 2 * tf)

    def body(x_ref, v_ref, wg_ref, wv_ref, cwg_ref, cwv_ref, cbg_ref, cbv_ref, wd_ref, x2_ref, h_ref, pre_ref,
             h_scr, u_scr, acc):
        i, t = pl.program_id(0), pl.program_id(1)

        @pl.when(t == 0)
        def _():
            @pl.when(i == 0)
            def _():
                h_scr[0:16, :] = jnp.zeros((16, D), BF16)

            @pl.when(i > 0)
            def _():
                h_scr[0:16, :] = h_scr[tr:tr + 16, :]

            h = (_rms(x_ref[...], v_ref[0:1, :], D) * (1.0 + v_ref[2:3, :]) + v_ref[1:2, :]).astype(BF16)
            h_scr[16:16 + tr, :] = h
            h_ref[...] = h
            acc[...] = jnp.zeros_like(acc)

        u_scr[:, lg] = jnp.dot(h_scr[...], wg_ref[...], preferred_element_type=F32)
        u_scr[:, lv] = jnp.dot(h_scr[...], wv_ref[...], preferred_element_type=F32)
        cg = _conv3(u_scr, cwg_ref, 14, tr, lg) + cbg_ref[...]
        cval = _conv3(u_scr, cwv_ref, 14, tr, lv) + cbv_ref[...]
        acc[...] += _raw(cg * _sig(cg) * cval, wd_ref[...], _NN)

        @pl.when(t == FFN_NT - 1)
        def _():
            pre_ref[...] = acc[...]
            x2_ref[...] = x_ref[...] + v_ref[3:4, :] * acc[...]

    row = pl.BlockSpec((tr, D), lambda i, t: (i, 0))
    return pl.pallas_call(
        body, name=name, grid=(s // tr, FFN_NT),
        in_specs=[row, pl.BlockSpec((8, D), lambda i, t: (0, 0))] + _ffn_tile_specs(tf, lambda i, t: t)
                 + [pl.BlockSpec((tf, D), lambda i, t: (t, 0))],
        out_specs=[row, row, row],
        out_shape=[jax.ShapeDtypeStruct((s, D), F32), jax.ShapeDtypeStruct((s, D), BF16), jax.ShapeDtypeStruct((s, D), F32)],
        scratch_shapes=[pltpu.VMEM((tr + 16, D), BF16), pltpu.VMEM((tr + 16, 2 * tf), F32), pltpu.VMEM((tr, D), F32)],
        compiler_params=_params("arbitrary", "arbitrary"),
    )(x1, fvec, wup, wup, cw, cw, cb, cb, wdn)


def ffn_bwd(h2, dx2, fvec, wup, cw, cb, wdn, name):
    s = h2.shape[0]
    tr, tf = _pick(s, 512), FFN_TILE
    ni, nb = s // tr, s // 16
    lg, lv = slice(0, tf), slice(tf, 2 * tf)

    def body(hp_ref, hm_ref, hn_ref, dm_ref, dn_ref, v_ref, wg_ref, wv_ref, cwg_ref, cwv_ref, cbg_ref, cbv_ref, wd_ref,
             dup_ref, act_ref, dcw_ref, u_scr, dc_scr):
        i = pl.program_id(1)
        hfull = jnp.concatenate([jnp.where(i > 0, hp_ref[...], jnp.zeros((16, D), BF16)), hm_ref[...],
                                 jnp.where(i < ni - 1, hn_ref[...], jnp.zeros((16, D), BF16))], axis=0)
        u_scr[:, lg] = jnp.dot(hfull, wg_ref[...], preferred_element_type=F32)
        u_scr[:, lv] = jnp.dot(hfull, wv_ref[...], preferred_element_type=F32)
        cg = _conv3(u_scr, cwg_ref, 14, tr + 16, lg) + cbg_ref[...]
        cval = _conv3(u_scr, cwv_ref, 14, tr + 16, lv) + cbv_ref[...]
        g2 = v_ref[3:4, :]
        dpre = jnp.concatenate([dm_ref[...] * g2, jnp.where(i < ni - 1, dn_ref[...], 0.0) * g2], axis=0)
        dact = _raw(dpre, wd_ref[...], _NT)
        sg = _sig(cg)
        sl = cg * sg
        dc_scr[:, lg] = dact * cval * (sg * (1.0 + cg * (1.0 - sg)))
        dc_scr[:, lv] = dact * sl
        act_ref[...] = (sl * cval)[0:tr, :].astype(BF16)

        @pl.when(i == 0)
        def _():
            dcw_ref[...] = jnp.zeros_like(dcw_ref)

        for half, lanes, cw_ref in ((0, lg, cwg_ref), (1, lv, cwv_ref)):
            dup_ref[half] = sum(cw_ref[k:k + 1, :] * dc_scr[2 - k:2 - k + tr, lanes] for k in range(3)).astype(BF16)
            dcm = dc_scr[0:tr, lanes]
            for k in range(3):
                dcw_ref[half, k:k + 1, :] += jnp.sum(dcm * u_scr[14 + k:14 + k + tr, lanes], axis=0, keepdims=True)
            dcw_ref[half, 3:4, :] += jnp.sum(dcm, axis=0, keepdims=True)

    r16 = tr // 16
    prev = lambda t, i: (jnp.maximum(i * r16 - 1, 0), 0)
    nxt = lambda t, i: (jnp.minimum((i + 1) * r16, nb - 1), 0)
    main = lambda t, i: (i, 0)
    return pl.pallas_call(
        body, name=name, grid=(FFN_NT, ni),
        in_specs=[pl.BlockSpec((16, D), prev), pl.BlockSpec((tr, D), main), pl.BlockSpec((16, D), nxt),
                  pl.BlockSpec((tr, D), main), pl.BlockSpec((16, D), nxt), pl.BlockSpec((8, D), lambda t, i: (0, 0))]
                 + _ffn_tile_specs(tf, lambda t, i: t) + [pl.BlockSpec((tf, D), lambda t, i: (t, 0))],
        out_specs=[pl.BlockSpec((2, tr, tf), lambda t, i: (0, i, t)), pl.BlockSpec((tr, tf), lambda t, i: (i, t)),
                   pl.BlockSpec((2, 8, tf), lambda t, i: (0, 0, t))],
        out_shape=[jax.ShapeDtypeStruct((2, s, FFN), BF16), jax.ShapeDtypeStruct((s, FFN), BF16),
                   jax.ShapeDtypeStruct((2, 8, FFN), F32)],
        scratch_shapes=[pltpu.VMEM((tr + 32, 2 * tf), F32), pltpu.VMEM((tr + 16, 2 * tf), F32)],
        compiler_params=_params("arbitrary", "arbitrary"),
    )(h2, h2, h2, dx2, dx2, fvec, wup, wup, cw, cw, cb, cb, wdn)


def loss_head(y, target):
    s = y.shape[0]
    tr = _pick(s, 512)

    def body(y_ref, t_ref, dx_ref, l_ref):
        @pl.when(pl.program_id(0) == 0)
        def _():
            l_ref[...] = jnp.zeros_like(l_ref)
        err = y_ref[...] - t_ref[...]
        dx_ref[...] = err / float(D)
        l_ref[...] += 0.5 * jnp.sum(jnp.sum(err * err, axis=-1, keepdims=True) / float(D), axis=0, keepdims=True)

    row = pl.BlockSpec((tr, D), lambda i: (i, 0))
    return pl.pallas_call(
        body, name="loss_head", grid=(s // tr,), in_specs=[row, row],
        out_specs=[row, pl.BlockSpec((8, 128), lambda i: (0, 0))],
        out_shape=[jax.ShapeDtypeStruct((s, D), F32), jax.ShapeDtypeStruct((8, 128), F32)],
        compiler_params=_params("arbitrary"),
    )(y, target)


def adamw(parts, w, m, v, name, tok=None):
    nseg = len(parts)
    p, r, c = parts[0].shape
    tr = _pick(r, 256 if c > 128 else 2048, 8)
    ni = r // tr
    tok = jnp.zeros((8, 128), F32) if tok is None else tok

    def body(*refs):
        p_refs = refs[:nseg]
        w_ref, m_ref, v_ref, _, g_out, d_out, m_out, v_out, g_scr = refs[nseg:]
        for q in range(nseg):
            @pl.when(pl.program_id(0) == q)
            def _(q=q):
                g = p_refs[q][0].astype(F32)
                for j in range(1, p):
                    g = g + p_refs[q][j].astype(F32)
                g_scr[...] = g
        g = g_scr[...]
        mn = B1 * m_ref[...] + (1.0 - B1) * g
        vn = B2 * v_ref[...] + (1.0 - B2) * (g * g)
        m_hat = mn / (1.0 - B1 ** STEP)
        v_hat = vn / (1.0 - B2 ** STEP)
        g_out[...] = g
        d_out[...] = -LR * (m_hat / (jnp.sqrt(v_hat) + ADAM_EPS) + WD * w_ref[...])
        m_out[...] = mn
        v_out[...] = vn

    row = pl.BlockSpec((tr, c), lambda l, i: (l * ni + i, 0))
    part = lambda q: pl.BlockSpec((p, tr, c), lambda l, i: (0, jnp.clip((l - q) * ni + i, 0, ni - 1), 0))
    return pl.pallas_call(
        body, name=name, grid=(nseg, ni),
        in_specs=[part(q) for q in range(nseg)] + [row, row, row, pl.BlockSpec((8, 128), lambda l, i: (0, 0))],
        out_specs=[row] * 4, out_shape=[jax.ShapeDtypeStruct((nseg * r, c), F32)] * 4,
        scratch_shapes=[pltpu.VMEM((tr, c), F32)],
        compiler_params=_params("arbitrary", "arbitrary"),
    )(*parts, w, m, v, tok)


def _padc(a, n):
    return jnp.pad(a, [(0, 0)] * (a.ndim - 1) + [(0, n - a.shape[-1])])


def _swap16(a):
    return jnp.concatenate([a[..., 16:32], a[..., 0:16]], axis=-1)


def _shard_cols(g8, a, b):
    c = g8.shape[2]
    return [g8[j][:, max(a, j * c) - j * c:min(b, (j + 1) * c) - j * c] for j in range(a // c, (b - 1) // c + 1)]


def _win_layout(g8):
    cols = lambda a, b: _shard_cols(g8, a, b)
    kr = jnp.concatenate(cols(640, 672), axis=1)
    dt = jnp.concatenate(cols(3744, 3760), axis=1)
    return jnp.concatenate(cols(3760, 6832) + cols(2208, 3232) + cols(1184, 2208) + cols(672, 1184) + cols(3232, 3744)
                           + cols(384, 640) + [_padc(kr, 128), _padc(_swap16(kr), 128), _padc(dt, 128),
                                               jnp.zeros((g8.shape[1], 128), g8.dtype)] + cols(0, 384), axis=1)


def _win_grad_shards(g):
    kr = (g[:, O_KR:O_KR + 32].astype(F32) + _swap16(g[:, O_KRS:O_KRS + 32].astype(F32))).astype(g.dtype)
    segs = [(g, O_QL, 384), (g, O_CKV, 256), (kr, 0, 32), (g, O_PU, 512), (g, O_Z, D), (g, O_XS, D), (g, O_BC, 512),
            (g, O_DT, 16), (g, O_G, 3 * D)]
    shards, width = [], sum(w for _, _, w in segs) // NDEV
    for j in range(NDEV):
        a, b, off, pieces = width * j, width * (j + 1), 0, []
        for arr, lo, w in segs:
            s0, s1 = max(a, off), min(b, off + w)
            if s0 < s1:
                pieces.append(arr[:, lo + s0 - off:lo + s1 - off])
            off += w
        shards.append(jnp.concatenate(pieces, axis=1))
    return jnp.stack(shards).astype(BF16)


def _wq_layout(w):
    w = w.reshape(384, HEADS, 96).transpose(1, 0, 2)
    rope = w[:, :, 64:96]
    return jnp.concatenate([_padc(w[:, :, 0:64], 128), _padc(rope, 128), _padc(_swap16(rope), 128)], axis=2)


def _wq_unlayout(g):
    rope = g[:, :, 128:160] + _swap16(g[:, :, 256:288])
    return jnp.concatenate([g[:, :, 0:64], rope], axis=2).transpose(1, 0, 2).reshape(384, HEADS * 96)


def _wkv_layout(w):
    w = w.reshape(256, HEADS, 128).transpose(1, 0, 2)
    return jnp.concatenate([_padc(w[:, :, 0:64], 128), _padc(w[:, :, 64:128], 128)], axis=2)


def _wkv_unlayout(g):
    return jnp.concatenate([g[:, :, 0:64], g[:, :, 128:192]], axis=2).transpose(1, 0, 2).reshape(256, HEADS * 128)


def _wba_layout(w):
    return jnp.pad(w.reshape(HEADS, 64, D), ((0, 0), (0, 64), (0, 0))).reshape(HEADS * 128, D)


def _rows8(rows, width):
    out = jnp.stack([_padc(r.astype(F32), width) for r in rows])
    return jnp.pad(out, ((0, 8 - out.shape[0]), (0, 0)))


def _mla_vec(qa, kva, qn, kn):
    def row(n):
        return jnp.concatenate([_padc(n[0:64], 128), _padc(n[64:96], 128), _padc(_swap16(n[64:96]), 128)])
    return _rows8([qa, kva, row(qn), row(kn)], 512)


def _mla_unvec(g):
    def un(r):
        return jnp.concatenate([r[0:64], r[128:160] + _swap16(r[256:288])])
    return g[0, 0:384], g[1, 0:256], un(g[2]), un(g[3])


SMALL = (("ada_b", (6 * D,)), ("norm1_w", (D,)), ("q_a_norm", (384,)), ("kv_a_norm", (256,)), ("q_norm", (96,)),
         ("k_norm", (96,)), ("pool_w", (4, 128, 128)), ("pool_scale", (512,)), ("ssd_conv_b", (1536,)),
         ("ssd_dt_bias", (16,)), ("ssd_a_log", (16,)), ("ssd_d", (16,)), ("ssd_norm_w", (D,)), ("norm2_w", (D,)),
         ("ffn_conv_b", (2 * FFN,)), ("ssd_conv_w", (4, 1536)), ("ffn_conv_w", (3, 2 * FFN)))
SMALL_REPL = SMALL[:15]


def _pack_rows(shp):
    return -(-LAYERS * math.prod(shp) // 1024) * 8


def _pack(per_layer, names):
    pieces = []
    for n, shp in names:
        pieces += [per_layer[l][n].reshape(-1).astype(F32) for l in range(LAYERS)]
        fill = _pack_rows(shp) * 128 - LAYERS * math.prod(shp)
        if fill:
            pieces.append(jnp.zeros((fill,), F32))
    return jnp.concatenate(pieces).reshape(-1, 128)


def _unpack(packed, names):
    out, off = {}, 0
    for n, shp in names:
        rows, size = _pack_rows(shp), LAYERS * math.prod(shp)
        out[n] = packed[off:off + rows].reshape(-1)[0:size].reshape((LAYERS,) + shp)
        off += rows
    return out


GROUP_A = ("w_in", "w_q_b", "w_kv_b")
GROUP_B = ("w_branch", "w_out", "ffn_up", "ffn_down")
BIG = GROUP_A + GROUP_B
COL_SHARDED = ("w_in", "w_q_b", "w_kv_b", "ffn_up")


def _behind(arrs, tok):
    arrs = list(arrs)
    j = min(range(len(arrs)), key=lambda q: arrs[q].size)
    arrs[j] = arrs[j] + tok[0, 0].astype(arrs[j].dtype)
    return arrs


def _gathered_full(g, name):
    if name in COL_SHARDED:
        return g.transpose(1, 0, 2).reshape(g.shape[1], NDEV * g.shape[2])
    return g.reshape(NDEV * g.shape[1], g.shape[2])


def _to_shards(full, name):
    if name == "w_in":
        return _win_grad_shards(full)
    if name in COL_SHARDED:
        r, c = full.shape
        return full.reshape(r, NDEV, c // NDEV).transpose(1, 0, 2).astype(BF16)
    r, c = full.shape
    return full.reshape(NDEV, r // NDEV, c).astype(BF16)


def _fwd_a(x, lw, mod, cos2, sin2, l, tok):
    sh1, sc1, g1, sh2, sc2, g2 = [mod[j * D:(j + 1) * D] for j in range(6)]
    vec1 = _rows8([lw["norm1_w"], sh1, sc1], D) + tok[0, 0]
    proj, h1, dt_cols = norm_proj_fwd(x, vec1, lw["win"], f"inproj_fwd{l}")
    q, k, v = mla_pre_fwd(proj, lw["wq"], lw["wkv"], lw["mla_vec"], cos2, sin2, f"mla_pre_fwd{l}")
    oa = mla_attn_fwd(q, k, v, f"mla_attn_fwd{l}")
    ob = pool_fwd(proj, lw["pool_w"], lw["pool_scale"].reshape(1, 512), f"pool_fwd{l}")
    xbc, xt = conv_fwd(proj, lw["ssd_conv_w"], lw["ssd_conv_b"].reshape(1, 1536), f"conv_fwd{l}")
    s = x.shape[0]
    xt = xt.reshape(16, 64, s)
    dt = dt_cols[:, 0:16].T
    dtr, dtc = dt[:, None, :], dt[:, :, None]
    hv = lambda a: a.reshape(16, 1, 1)
    yt, hs = ssd_fwd(xt, dtr, dtc, xbc, hv(lw["ssd_a_log"]), hv(lw["ssd_dt_bias"]), hv(lw["ssd_d"]), f"ssd_fwd{l}")
    return dict(x=x, vec1=vec1, proj=proj, h1=h1, q=q, k=k, v=v, oa=oa, ob=ob, xbc=xbc, xt=xt, dtr=dtr, dtc=dtc,
                hs=hs, yt=yt.reshape(D, s), mvec=_rows8([g1, lw["ssd_norm_w"]], D),
                fvec=_rows8([lw["norm2_w"], sh2, sc2, g2], D))


def _fwd_b(sv, lw, l, tok):
    sv["mvec"] = sv["mvec"] + tok[0, 0]
    x1 = merge_fwd(sv["oa"], sv["ob"], sv["yt"], sv["proj"], sv["x"], sv["mvec"], lw["wba"], lw["wbb"], lw["wbc"],
                   lw["wout"], f"merge_fwd{l}")
    x2, h2, pre = ffn_fwd(x1, sv["fvec"], lw["wup"], lw["ffn_conv_w"], lw["ffn_conv_b"].reshape(1, 2 * FFN), lw["wdn"],
                          f"ffn_fwd{l}")
    sv.update(x1=x1, h2=h2, pre=pre)
    return x2


def _bwd_b(dx2, lw, sv, l, tok):
    grads, small = {}, {}
    fvec = sv["fvec"] + tok[0, 0]
    dup, act, dcw = ffn_bwd(sv["h2"], dx2, fvec, lw["wup"], lw["ffn_conv_w"], lw["ffn_conv_b"].reshape(1, 2 * FFN),
                            lw["wdn"], f"ffn_bwd{l}")
    grads["ffn_down"] = tn_matmul(act, dx2, f"dw_down{l}", scale=fvec[3:4])
    grads["ffn_up"] = tn_matmul(sv["h2"], dup, f"dw_up{l}")
    dx1, dfvec = norm_proj_bwd(sv["x1"], fvec, dup, lw["wup"], dx2, sv["pre"], f"ffn_norm_bwd{l}")
    small["ffn_conv_w"] = jnp.concatenate([dcw[0, 0:3], dcw[1, 0:3]], axis=1)
    small["ffn_conv_b"] = jnp.concatenate([dcw[0, 3], dcw[1, 3]])
    small["norm2_w"] = dfvec[0]
    (doa, dob, dyt, dz, dgl, dx, dmvec, dya, dyb, dyc, dpre, oc, merged) = merge_bwd(
        sv["oa"], sv["ob"], sv["yt"], sv["proj"], sv["x"], sv["mvec"], lw["wba"], lw["wbb"], lw["wbc"], lw["wout"], dx1,
        f"merge_bwd{l}")
    dwba = tn_matmul(sv["oa"], dya, f"dw_ba{l}").reshape(HEADS, 128, D)[:, 0:64].reshape(512, D)
    grads["w_branch"] = jnp.concatenate([dwba, tn_matmul(sv["ob"], dyb, f"dw_bb{l}"), tn_matmul(oc, dyc, f"dw_bc{l}")])
    grads["w_out"] = tn_matmul(merged, dpre, f"dw_out{l}")
    small["ssd_norm_w"] = dmvec[1]
    small["dmod_b"] = (dmvec[0], dfvec[1], dfvec[2], dfvec[3])
    return dx, dict(doa=doa, dob=dob, dyt=dyt, dz=dz, dgl=dgl), grads, small


def _bwd_a(dx, cot, lw, sv, cos2, sin2, l, tok, small):
    s = dx.shape[0]
    grads = {}
    doa, dob, dz, dgl = cot["doa"], cot["dob"], cot["dz"], cot["dgl"]
    hv = lambda a: a.reshape(16, 1, 1)
    dxt, ddtr, ddtc, dbm, dcm, dal, ddb, ddk = ssd_bwd(
        sv["xt"], sv["dtr"], sv["dtc"], sv["xbc"], hv(lw["ssd_a_log"]) + tok[0, 0], hv(lw["ssd_dt_bias"]),
        hv(lw["ssd_d"]), sv["hs"], cot["dyt"].reshape(16, 64, s), f"ssd_bwd{l}")
    small["ssd_a_log"], small["ssd_dt_bias"], small["ssd_d"] = dal.reshape(16), ddb.reshape(16), ddk.reshape(16)
    dxbc, dscw, dscb = conv_bwd(sv["proj"], lw["ssd_conv_w"], lw["ssd_conv_b"].reshape(1, 1536), dxt.reshape(D, s),
                                dbm, dcm, f"conv_bwd{l}")
    small["ssd_conv_w"], small["ssd_conv_b"] = dscw, dscb.reshape(1536)
    ddt = (ddtr[:, 0, :] + ddtc[:, :, 0]).T
    du, dpw, dps = pool_bwd(sv["proj"], lw["pool_w"], lw["pool_scale"].reshape(1, 512), dob, f"pool_bwd{l}")
    small["pool_w"], small["pool_scale"] = dpw, dps.reshape(512)
    dq, dk, dv = mla_attn_bwd(sv["q"], sv["k"], sv["v"], doa, f"mla_attn_bwd{l}")
    dql, dckv, dkr, dkrs, dwq, dwkv, dmv = mla_pre_bwd(sv["proj"], lw["wq"], lw["wkv"], lw["mla_vec"], cos2, sin2,
                                                       dq, dk, dv, f"mla_pre_bwd{l}")
    grads["w_q_b"], grads["w_kv_b"] = _wq_unlayout(dwq), _wkv_unlayout(dwkv)
    small["q_a_norm"], small["kv_a_norm"], small["q_norm"], small["k_norm"] = _mla_unvec(dmv)
    dproj = jnp.concatenate([dgl, dxbc[:, 0:D], dz, du, dxbc[:, D:1536], dckv, dkr, dkrs,
                             _padc(ddt, 128).astype(BF16), jnp.zeros((s, 128), BF16), dql], axis=1)
    grads["w_in"] = tn_matmul(sv["h1"], dproj, f"dw_in{l}")
    dx0, dvec1 = norm_proj_bwd(sv["x"], sv["vec1"], dproj, lw["win"], dx, None, f"inproj_bwd{l}")
    small["norm1_w"] = dvec1[0]
    small["ada_b"] = jnp.concatenate([dvec1[1], dvec1[2], *small.pop("dmod_b")])
    return dx0, grads, small


def kernel(x, c, positions, ada_w, ada_b, norm1_w, w_in, q_a_norm, w_q_b, kv_a_norm, w_kv_b, q_norm, k_norm, pool_w, pool_scale, ssd_conv_w, ssd_conv_b, ssd_dt_bias, ssd_a_log, ssd_d, ssd_norm_w, w_branch, w_out, norm2_w, ffn_up, ffn_conv_w, ffn_conv_b, ffn_down, loss_target, m_ada_w, m_ada_b, m_norm1_w, m_w_in, m_q_a_norm, m_w_q_b, m_kv_a_norm, m_w_kv_b, m_q_norm, m_k_norm, m_pool_w, m_pool_scale, m_ssd_conv_w, m_ssd_conv_b, m_ssd_dt_bias, m_ssd_a_log, m_ssd_d, m_ssd_norm_w, m_w_branch, m_w_out, m_norm2_w, m_ffn_up, m_ffn_conv_w, m_ffn_conv_b, m_ffn_down, v_ada_w, v_ada_b, v_norm1_w, v_w_in, v_q_a_norm, v_w_q_b, v_kv_a_norm, v_w_kv_b, v_q_norm, v_k_norm, v_pool_w, v_pool_scale, v_ssd_conv_w, v_ssd_conv_b, v_ssd_dt_bias, v_ssd_a_log, v_ssd_d, v_ssd_norm_w, v_w_branch, v_w_out, v_norm2_w, v_ffn_up, v_ffn_conv_w, v_ffn_conv_b, v_ffn_down):
    p = dict(ada_w=ada_w, ada_b=ada_b, norm1_w=norm1_w, w_in=w_in, q_a_norm=q_a_norm, w_q_b=w_q_b, kv_a_norm=kv_a_norm,
             w_kv_b=w_kv_b, q_norm=q_norm, k_norm=k_norm, pool_w=pool_w, pool_scale=pool_scale, ssd_conv_w=ssd_conv_w,
             ssd_conv_b=ssd_conv_b, ssd_dt_bias=ssd_dt_bias, ssd_a_log=ssd_a_log, ssd_d=ssd_d, ssd_norm_w=ssd_norm_w,
             w_branch=w_branch, w_out=w_out, norm2_w=norm2_w, ffn_up=ffn_up, ffn_conv_w=ffn_conv_w, ffn_conv_b=ffn_conv_b,
             ffn_down=ffn_down)
    mom = dict(ada_w=m_ada_w, ada_b=m_ada_b, norm1_w=m_norm1_w, w_in=m_w_in, q_a_norm=m_q_a_norm, w_q_b=m_w_q_b,
               kv_a_norm=m_kv_a_norm, w_kv_b=m_w_kv_b, q_norm=m_q_norm, k_norm=m_k_norm, pool_w=m_pool_w,
               pool_scale=m_pool_scale, ssd_conv_w=m_ssd_conv_w, ssd_conv_b=m_ssd_conv_b, ssd_dt_bias=m_ssd_dt_bias,
               ssd_a_log=m_ssd_a_log, ssd_d=m_ssd_d, ssd_norm_w=m_ssd_norm_w, w_branch=m_w_branch, w_out=m_w_out,
               norm2_w=m_norm2_w, ffn_up=m_ffn_up, ffn_conv_w=m_ffn_conv_w, ffn_conv_b=m_ffn_conv_b, ffn_down=m_ffn_down)
    var = dict(ada_w=v_ada_w, ada_b=v_ada_b, norm1_w=v_norm1_w, w_in=v_w_in, q_a_norm=v_q_a_norm, w_q_b=v_w_q_b,
               kv_a_norm=v_kv_a_norm, w_kv_b=v_w_kv_b, q_norm=v_q_norm, k_norm=v_k_norm, pool_w=v_pool_w,
               pool_scale=v_pool_scale, ssd_conv_w=v_ssd_conv_w, ssd_conv_b=v_ssd_conv_b, ssd_dt_bias=v_ssd_dt_bias,
               ssd_a_log=v_ssd_a_log, ssd_d=v_ssd_d, ssd_norm_w=v_ssd_norm_w, w_branch=v_w_branch, w_out=v_w_out,
               norm2_w=v_norm2_w, ffn_up=v_ffn_up, ffn_conv_w=v_ffn_conv_w, ffn_conv_b=v_ffn_conv_b, ffn_down=v_ffn_down)
    names = list(p)
    me = 4 * lax.axis_index("x") + 2 * lax.axis_index("y") + lax.axis_index("c")
    xs, tgt = x[0], loss_target[0]
    s = xs.shape[0]

    inv_freq = ROPE_THETA ** (-jnp.arange(0, 32, 2, dtype=F32) / 32.0)
    ang = positions[0].astype(F32)[:, None] * inv_freq
    cos, sin = jnp.cos(ang), jnp.sin(ang)
    cos2 = _padc(jnp.concatenate([cos, cos], axis=1), 128)
    sin2 = _padc(jnp.concatenate([-sin, sin], axis=1), 128)

    conv_shards = jnp.concatenate([ssd_conv_w.reshape(-1), ffn_conv_w.reshape(-1)])
    (c_all, conv_all), _ = all_to_all([c, conv_shards], [True, True], "gather_c")
    modp, cact = ada_mod(jnp.pad(c_all.reshape(NDEV, D), ((0, 8), (0, 0))), ada_w)
    (mod_in,), tok = all_to_all([modp[:, 0:NDEV].transpose(1, 0, 2)], [False], "scatter_mod")
    mod = mod_in.transpose(1, 0, 2).reshape(LAYERS, 6 * D) + ada_b

    n1 = LAYERS * 4 * 192
    scw = conv_all[:, :n1].reshape(NDEV, LAYERS, 4, 192).transpose(1, 2, 0, 3).reshape(LAYERS, 4, 1536)
    fcw = conv_all[:, n1:].reshape(NDEV, LAYERS, 3, 704).transpose(1, 2, 0, 3).reshape(LAYERS, 3, 2 * FFN)

    def weights_a(gathered, l):
        full = {n: _gathered_full(g, n) for n, g in zip(GROUP_A[1:], gathered[1:])}
        lw = {n: p[n][l] for n in names}
        lw.update(win=_win_layout(gathered[0]), wq=_wq_layout(full["w_q_b"]), wkv=_wkv_layout(full["w_kv_b"]),
                  ssd_conv_w=scw[l], ffn_conv_w=fcw[l],
                  mla_vec=_mla_vec(lw["q_a_norm"], lw["kv_a_norm"], lw["q_norm"], lw["k_norm"]))
        return lw

    def weights_b(gathered):
        full = {n: _gathered_full(g, n) for n, g in zip(GROUP_B, gathered)}
        wb = full["w_branch"]
        return dict(wba=_wba_layout(wb[0:512]), wbb=wb[512:1024], wbc=wb[1024:2048], wout=full["w_out"],
                    wup=full["ffn_up"], wdn=full["ffn_down"])

    shards = lambda group, l: [p[n][l].astype(BF16) for n in group]
    lws, saved = [None] * LAYERS, [None] * LAYERS
    st, tok = gather_start(_behind(shards(GROUP_A, 0), tok), "gather_a0")
    got, tok = gather_finish(st, tok, "gather_a0")
    h = xs
    for l in range(LAYERS):
        st, tok = gather_start(_behind(shards(GROUP_B, l), tok), f"gather_b{l}")
        lws[l] = weights_a(got, l)
        saved[l] = _fwd_a(h, lws[l], mod[l], cos2, sin2, l, tok)
        got, tok = gather_finish(st, saved[l]["yt"], f"gather_b{l}")
        lws[l].update(weights_b(got))
        if l + 1 < LAYERS:
            st, tok = gather_start(_behind(shards(GROUP_A, l + 1), tok), f"gather_a{l + 1}")
        h = _fwd_b(saved[l], lws[l], l, tok)
        if l + 1 < LAYERS:
            got, tok = gather_finish(st, h, f"gather_a{l + 1}")
    dx, lpart = loss_head(h, tgt)
    loss = lax.psum(lpart[0, 0], ("x", "y", "c"))
    tok = tok + loss * 0.0

    grads, small, parts = [None] * LAYERS, [None] * LAYERS, {}
    to_shards = lambda g, group: [_to_shards(g[n], n) for n in group]
    nb = lambda group: [False] * len(group)
    st = None
    for l in reversed(range(LAYERS)):
        dx, cot, gb, small[l] = _bwd_b(dx, lws[l], saved[l], l, tok)
        if st is not None:
            parts[("a", l + 1)], tok, _ = exchange_wait(st, dx, f"scatter_a{l + 1}_wait")
        st, tok = exchange_start(_behind(to_shards(gb, GROUP_B), tok), nb(GROUP_B), f"scatter_b{l}_start")
        dx, ga, small[l] = _bwd_a(dx, cot, lws[l], saved[l], cos2, sin2, l, tok, small[l])
        parts[("b", l)], tok, _ = exchange_wait(st, dx, f"scatter_b{l}_wait")
        arrs, flags = to_shards(ga, GROUP_A), nb(GROUP_A)
        if l == 0:
            dmod = jnp.stack([small[q]["ada_b"] for q in range(LAYERS)])
            arrs += [_pack(small, SMALL), dmod.reshape(LAYERS, NDEV, 768).transpose(1, 0, 2)]
            flags += [True, False]
        st, tok = exchange_start(_behind(arrs, tok), flags, f"scatter_a{l}_start")

    out = {}

    def big_adamw(group, tok):
        res = None
        for n in group:
            grp, idx = ("a", GROUP_A.index(n)) if n in GROUP_A else ("b", GROUP_B.index(n))
            shp = p[n].shape
            flat = lambda a: a.reshape(shp[0] * shp[1], shp[2])
            res = adamw([parts[(grp, 0)][idx], parts[(grp, 1)][idx]], flat(p[n]), flat(mom[n]), flat(var[n]),
                        f"adamw_{n}", tok)
            out[n] = [r.reshape(shp) for r in res]
        return res[0]

    g_last = big_adamw(GROUP_B, tok)
    got, _, _ = exchange_wait(st, g_last, "scatter_a0_wait")
    parts[("a", 0)], small_all, dmod_in = got[0:3], got[3], got[4]
    big_adamw(GROUP_A, None)

    dmod16 = jnp.pad(dmod_in, ((0, 8), (0, 0), (0, 0)))
    g_ada = jnp.stack([tn_matmul(cact, dmod16[:, l], f"dw_ada{l}", out_dtype=F32) for l in range(LAYERS)])
    flat = lambda a: a.reshape(LAYERS * D, 768)
    out["ada_w"] = [r.reshape(ada_w.shape) for r in
                    adamw([flat(g_ada)[None]], flat(ada_w), flat(m_ada_w), flat(v_ada_w), "adamw_ada_w")]

    zeros = jnp.zeros(small_all.shape[1:], F32)
    g_pack = adamw([small_all], zeros, zeros, zeros, "sum_small")[0]
    g_small = _unpack(g_pack, SMALL)
    per = lambda d, nm: [{n: d[n][l] for n, _ in nm} for l in range(LAYERS)]
    repl_rows = sum(_pack_rows(shp) for _, shp in SMALL_REPL)
    res = adamw([g_pack[0:repl_rows][None]], _pack(per(p, SMALL_REPL), SMALL_REPL),
                _pack(per(mom, SMALL_REPL), SMALL_REPL), _pack(per(var, SMALL_REPL), SMALL_REPL), "adamw_small")
    res = [_unpack(r, SMALL_REPL) for r in res]
    for n, _ in SMALL_REPL:
        out[n] = [r[n] for r in res]
    for n, k, w in (("ssd_conv_w", 4, 192), ("ffn_conv_w", 3, 704)):
        g_mine = lax.dynamic_slice(g_small[n], (0, 0, me * w), (LAYERS, k, w))
        f2 = lambda a: jnp.pad(a.reshape(LAYERS * k, w), ((0, 8 - LAYERS * k), (0, 0)))
        res = adamw([f2(g_mine)[None]], f2(p[n]), f2(mom[n]), f2(var[n]), f"adamw_{n}")
        out[n] = [r[0:LAYERS * k].reshape(LAYERS, k, w) for r in res]

    outs = [loss, dx[None]]
    for q in range(4):
        outs += [out[n][q] for n in names]
    return tuple(outs)
```

```python
import functools
import math

import jax
import jax.numpy as jnp
from jax import lax
from jax.experimental import pallas as pl
from jax.experimental.pallas import tpu as pltpu

F32, BF16 = jnp.float32, jnp.bfloat16
EPS = 1e-6
D = 1024
NDEV = 8
LAYERS = 2
HEADS = 8
FFN = 2816
FFN_TILE = 1408
FFN_NT = FFN // FFN_TILE
ATT_SCALE = 96 ** -0.5
ROPE_THETA = 10000.0
LR, B1, B2, ADAM_EPS, WD, STEP = 0.001, 0.9, 0.999, 1e-08, 0.01, 10

O_G, O_XS, O_Z, O_PU, O_BC, O_CKV, O_KR, O_KRS, O_DT, O_QL = 0, 3072, 4096, 5120, 5632, 6144, 6400, 6528, 6656, 6912
NPROJ = 7296
CONST = dict(pipeline_mode=pl.Buffered(1))


def _pick(n, cap, mult=128):
    if n <= cap:
        return n
    best = None
    for t in range(mult, cap + 1, mult):
        if n % t == 0:
            best = t
    assert best is not None, (n, cap, mult)
    return best


def _sig(x):
    return 1.0 / (1.0 + jnp.exp(-x))


def _rms(x, w, n):
    return x * lax.rsqrt(jnp.sum(x * x, axis=-1, keepdims=True) / n + EPS) * w


def _raw(a, b, dims):
    return lax.dot_general(a.astype(BF16), b.astype(BF16), dims, preferred_element_type=F32)


_NN = (((1,), (0,)), ((), ()))
_NT = (((1,), (1,)), ((), ()))
_TN = (((0,), (0,)), ((), ()))
_BNN = (((2,), (1,)), ((0,), (0,)))
_BNT = (((2,), (2,)), ((0,), (0,)))
_BTN = (((1,), (1,)), ((0,), (0,)))


@jax.custom_vjp
def mm_nn(a, b):
    return _raw(a, b, _NN)


mm_nn.defvjp(lambda a, b: (_raw(a, b, _NN), (a, b)),
             lambda r, g: (_raw(g, r[1], _NT), _raw(r[0], g, _TN)))


@jax.custom_vjp
def mm_nc(a, b):
    return _raw(a, b, _NN)


mm_nc.defvjp(lambda a, b: (_raw(a, b, _NN), b),
             lambda b, g: (_raw(g, b, _NT), jnp.zeros_like(b)))


@jax.custom_vjp
def mm_nt(a, b):
    return _raw(a, b, _NT)


mm_nt.defvjp(lambda a, b: (_raw(a, b, _NT), (a, b)),
             lambda r, g: (_raw(g, r[1], _NN), _raw(g, r[0], _TN)))


@jax.custom_vjp
def bmm_nn(a, b):
    return _raw(a, b, _BNN)


bmm_nn.defvjp(lambda a, b: (_raw(a, b, _BNN), (a, b)),
              lambda r, g: (_raw(g, r[1], _BNT), _raw(r[0], g, _BTN)))


@jax.custom_vjp
def bmm_nt(a, b):
    return _raw(a, b, _BNT)


bmm_nt.defvjp(lambda a, b: (_raw(a, b, _BNT), (a, b)),
              lambda r, g: (_raw(g, r[1], _BNN), _raw(g, r[0], _BTN)))


@jax.custom_vjp
def softplus(x):
    t = jnp.exp(-jnp.abs(x))
    u = 1.0 + t
    one = u == 1.0
    l1p = jnp.where(one, t, jnp.log(u) * (t / jnp.where(one, 1.0, u - 1.0)))
    return jnp.maximum(x, 0.0) + l1p


softplus.defvjp(lambda x: (softplus(x), x), lambda x, g: (g * _sig(x),))


def _params(*sem):
    return pltpu.CompilerParams(dimension_semantics=sem, vmem_limit_bytes=56 * 1024 * 1024)


def all_to_all(arrs, bcast, name):
    n = len(arrs)
    out_shapes = [jax.ShapeDtypeStruct(((NDEV,) + a.shape) if b else a.shape, a.dtype) for a, b in zip(arrs, bcast)]

    def body(*refs):
        ins, outs, token = refs[:n], refs[n:2 * n], refs[2 * n]
        send_sems, recv_sems, local_sems = refs[2 * n + 1:]
        me, remote = _exchange_copies(ins, outs, bcast, send_sems, recv_sems)
        local = [pltpu.make_async_copy(ins[j] if bcast[j] else ins[j].at[me], outs[j].at[me], local_sems.at[j])
                 for j in range(n)]
        for cp in local + remote:
            cp.start()
        for cp in remote + local:
            cp.wait()
        token[...] = jnp.zeros_like(token)

    any_spec = pl.BlockSpec(memory_space=pl.ANY)
    res = pl.pallas_call(
        body, name=name, out_shape=out_shapes + [jax.ShapeDtypeStruct((8, 128), F32)], in_specs=[any_spec] * n,
        out_specs=[any_spec] * n + [pl.BlockSpec(memory_space=pltpu.VMEM)],
        scratch_shapes=[pltpu.SemaphoreType.DMA((7 * n,)), pltpu.SemaphoreType.DMA((7 * n,)),
                        pltpu.SemaphoreType.DMA((n,))],
        compiler_params=pltpu.CompilerParams(has_side_effects=True),
    )(*arrs)
    return res[:n], res[n]


def _peers():
    x, y, c = lax.axis_index("x"), lax.axis_index("y"), lax.axis_index("c")
    out = []
    for k in range(1, NDEV):
        px, py, pc = x ^ ((k >> 2) & 1), y ^ ((k >> 1) & 1), c ^ (k & 1)
        out.append(((px, py, pc), 4 * px + 2 * py + pc))
    return 4 * x + 2 * y + c, out


COPIES = {"all": 7, "chips": 3, "pass": 4}


def _exchange_copies(ins, lands, bcast, send_sems, recv_sems, mode="all"):
    x, y, c = lax.axis_index("x"), lax.axis_index("y"), lax.axis_index("c")
    me = 4 * x + 2 * y + c
    n, copies = len(ins), []

    def add(q, j, src, dst, dev):
        copies.append(pltpu.make_async_remote_copy(
            src_ref=src, dst_ref=dst, send_sem=send_sems.at[q * n + j], recv_sem=recv_sems.at[q * n + j],
            device_id=dev, device_id_type=pl.DeviceIdType.MESH))

    if mode == "pass":
        for q in range(4):
            slot = 4 * (x ^ (q >> 1)) + 2 * (y ^ (q & 1)) + c
            for j in range(n):
                add(q, j, ins[j] if q == 0 else lands[j].at[slot], lands[j].at[slot], (x, y, 1 - c))
        return me, copies
    for q, k in enumerate(range(1, NDEV) if mode == "all" else (2, 4, 6)):
        px, py, pc = x ^ ((k >> 2) & 1), y ^ ((k >> 1) & 1), c ^ (k & 1)
        for j in range(n):
            add(q, j, ins[j] if bcast[j] else ins[j].at[4 * px + 2 * py + pc], lands[j].at[me], (px, py, pc))
    return me, copies


_HBM = pl.BlockSpec(memory_space=pltpu.HBM)
_SEM = pl.BlockSpec(memory_space=pltpu.SEMAPHORE)
_EFFECT = pltpu.SideEffectType.DATAFLOW_SIDE_EFFECTING


def exchange_start(arrs, bcast, name, mode="all", lands=None):
    n, ncp = len(arrs), COPIES[mode] * len(arrs)
    land_shapes = [((NDEV,) + a.shape) if b else a.shape for a, b in zip(arrs, bcast)]
    if lands is None:
        lands = [lax.empty(s_, a.dtype) for s_, a in zip(land_shapes, arrs)]

    def body(*refs):
        in_refs, land_refs = refs[:n], refs[n:2 * n]
        send_sems, recv_sems = refs[2 * n], refs[2 * n + 1]
        token = refs[-1]
        _, copies = _exchange_copies(in_refs, land_refs, bcast, send_sems, recv_sems, mode)
        for cp in copies:
            cp.start()
        token[...] = jnp.zeros_like(token)

    hbm = lambda shp, a: pltpu.HBM(shp, a.dtype)
    res = pl.pallas_call(
        body, name=name,
        out_shape=[pltpu.SemaphoreType.DMA((ncp,)), pltpu.SemaphoreType.DMA((ncp,))]
                  + [hbm(a.shape, a) for a in arrs] + [hbm(s_, a) for s_, a in zip(land_shapes, arrs)]
                  + [jax.ShapeDtypeStruct((8, 128), F32)],
        in_specs=[_HBM] * (2 * n), out_specs=[_SEM, _SEM] + [_HBM] * (2 * n) + [pl.BlockSpec(memory_space=pltpu.VMEM)],
        input_output_aliases={i: 2 + i for i in range(2 * n)},
        compiler_params=pltpu.CompilerParams(has_side_effects=_EFFECT),
    )(*[pltpu.with_memory_space_constraint(a, pltpu.HBM) for a in arrs],
      *[pltpu.with_memory_space_constraint(a, pltpu.HBM) for a in lands])
    return (res[0], res[1], res[2:2 + n], res[2 + n:2 + 2 * n], tuple(bcast), mode), res[-1]


def exchange_wait(state, after, name):
    send_sems, recv_sems, ins, lands, bcast, mode = state
    n = len(ins)

    def body(*refs):
        in_refs, land_refs = refs[:n], refs[n:2 * n]
        s_sems, r_sems = refs[2 * n], refs[2 * n + 1]
        token = refs[-1]
        _, copies = _exchange_copies(in_refs, land_refs, bcast, s_sems, r_sems, mode)
        for cp in copies:
            cp.wait_send()
            cp.wait_recv()
        token[...] = jnp.zeros_like(token)

    res = pl.pallas_call(
        body, name=name,
        out_shape=[pltpu.HBM(a.shape, a.dtype) for a in ins] + [pltpu.HBM(a.shape, a.dtype) for a in lands]
                  + [jax.ShapeDtypeStruct((8, 128), F32)],
        in_specs=[_HBM] * (2 * n) + [_SEM, _SEM, pl.BlockSpec(memory_space=pl.ANY)],
        out_specs=[_HBM] * (2 * n) + [pl.BlockSpec(memory_space=pltpu.VMEM)],
        input_output_aliases={i: i for i in range(2 * n)},
        compiler_params=pltpu.CompilerParams(has_side_effects=_EFFECT),
    )(*ins, *lands, send_sems, recv_sems, after)
    if mode == "chips":
        return list(res[n:2 * n]), res[-1], list(res[:n])
    me = 4 * lax.axis_index("x") + 2 * lax.axis_index("y") + lax.axis_index("c")
    got = []
    for j in range(n):
        own = res[j][None] if bcast[j] else lax.dynamic_index_in_dim(res[j], me, 0, keepdims=True)
        got.append(lax.dynamic_update_slice_in_dim(res[n + j], own, me, axis=0))
    return got, res[-1], list(res[:n])


def gather_start(shards, name):
    return exchange_start(shards, [True] * len(shards), name + "_chips_start", mode="chips")


def gather_finish(state, after, name):
    lands, _, sent = exchange_wait(state, after, name + "_chips_wait")
    state, tok = exchange_start(sent, [True] * len(sent), name + "_pass_start", mode="pass", lands=lands)
    got, tok, _ = exchange_wait(state, tok, name + "_pass_wait")
    return got, tok


def norm_proj_fwd(x, vec, w, name):
    s, n = x.shape[0], w.shape[1]
    tr, tn = _pick(s, 512), _pick(n, 2560)
    ni, jdt, odt = s // tr, O_DT // tn, O_DT % tn

    def body(x_ref, v_ref, w_ref, o_ref, h_ref, dt_ref, h_scr):
        j, i = pl.program_id(0), pl.program_id(1)
        rows = pl.ds(pl.multiple_of(i * tr, tr), tr)

        @pl.when(j == 0)
        def _():
            h = _rms(x_ref[...], v_ref[0:1, :], D) * (1.0 + v_ref[2:3, :]) + v_ref[1:2, :]
            h_scr[rows, :] = h.astype(BF16)
            h_ref[...] = h.astype(BF16)
        res = jnp.dot(h_scr[rows, :], w_ref[...], preferred_element_type=F32)
        o_ref[...] = res

        @pl.when(j == jdt)
        def _():
            dt_ref[...] = res[:, odt:odt + 128]

    first = lambda j, i: (jnp.where(j == 0, i, ni - 1), 0)
    dtix = lambda j, i: (jnp.where(j < jdt, 0, jnp.where(j == jdt, i, ni - 1)), 0)
    return pl.pallas_call(
        body, name=name, grid=(n // tn, ni),
        in_specs=[pl.BlockSpec((tr, D), first), pl.BlockSpec((8, D), lambda j, i: (0, 0)),
                  pl.BlockSpec((D, tn), lambda j, i: (0, j))],
        out_specs=[pl.BlockSpec((tr, tn), lambda j, i: (i, j)), pl.BlockSpec((tr, D), first),
                   pl.BlockSpec((tr, 128), dtix)],
        out_shape=[jax.ShapeDtypeStruct((s, n), F32), jax.ShapeDtypeStruct((s, D), BF16),
                   jax.ShapeDtypeStruct((s, 128), F32)],
        scratch_shapes=[pltpu.VMEM((s, D), BF16)],
        compiler_params=_params("arbitrary", "arbitrary"),
    )(x, vec, w)


def _col_tiles(arr, cap):
    if arr.ndim == 2:
        n = arr.shape[1]
        t = _pick(n, cap)
        return n, t, lambda rows, ix: pl.BlockSpec((rows, t), lambda *g: ix(*g))
    width = arr.shape[2]
    t = _pick(width, cap)
    per = width // t

    def spec(rows, ix):
        def index(*g):
            r, j = ix(*g)
            return (j // per, r, j % per)
        return pl.BlockSpec((None, rows, t), index)
    return arr.shape[0] * width, t, spec


def norm_proj_bwd(x, vec, dp, w, dx_in, aux, name):
    s = x.shape[0]
    tr = _pick(s, 512)
    n, tk, dp_spec = _col_tiles(dp, 2560)
    nk, has_aux = n // tk, aux is not None

    def body(*refs):
        if has_aux:
            x_ref, v_ref, dp_ref, w_ref, dxin_ref, aux_ref, dx_ref, dv_ref, acc = refs
        else:
            x_ref, v_ref, dp_ref, w_ref, dxin_ref, dx_ref, dv_ref, acc = refs
        k, i = pl.program_id(0), pl.program_id(1)
        rows = pl.ds(pl.multiple_of(i * tr, tr), tr)
        part = _raw(dp_ref[...], w_ref[...], _NT)

        @pl.when(k == 0)
        def _():
            acc[rows, :] = part

        @pl.when(k > 0)
        def _():
            acc[rows, :] += part

        @pl.when(k == nk - 1)
        def _():
            f = lambda xx, nw, sh, sc: _rms(xx, nw, D) * (1.0 + sc) + sh
            _, vjp = jax.vjp(f, x_ref[...], v_ref[0:1, :], v_ref[1:2, :], v_ref[2:3, :])
            dx, dnw, dsh, dsc = vjp(acc[rows, :])
            dx_ref[...] = dxin_ref[...] + dx

            @pl.when(i == 0)
            def _():
                dv_ref[...] = jnp.zeros_like(dv_ref)

            dv_ref[0:1, :] += dnw
            dv_ref[1:2, :] += dsh
            dv_ref[2:3, :] += dsc
            if has_aux:
                dv_ref[3:4, :] += jnp.sum(dxin_ref[...] * aux_ref[...], axis=0, keepdims=True)

    row = pl.BlockSpec((tr, D), lambda k, i: (jnp.where(k == nk - 1, i, 0), 0))
    in_specs = [row, pl.BlockSpec((8, D), lambda k, i: (0, 0)), dp_spec(tr, lambda k, i: (i, k)),
                pl.BlockSpec((D, tk), lambda k, i: (0, k)), row] + ([row] if has_aux else [])
    args = [x, vec, dp, w, dx_in] + ([aux] if has_aux else [])
    return pl.pallas_call(
        body, name=name, grid=(nk, s // tr), in_specs=in_specs,
        out_specs=[row, pl.BlockSpec((8, D), lambda k, i: (0, 0))],
        out_shape=[jax.ShapeDtypeStruct((s, D), F32), jax.ShapeDtypeStruct((8, D), F32)],
        scratch_shapes=[pltpu.VMEM((s, D), F32)],
        compiler_params=_params("arbitrary", "arbitrary"),
    )(*args)


def tn_matmul(a, b, name, scale=None, out_dtype=None):
    out_dtype = BF16 if out_dtype is None else out_dtype
    s, m = a.shape
    ts, tm = _pick(s, 512, 16), _pick(m, 1408)
    n, tn, b_spec = _col_tiles(b, 2560)
    ns, has_scale = s // ts, scale is not None

    def body(*refs):
        if has_scale:
            a_ref, b_ref, sc_ref, o_ref, acc = refs
        else:
            a_ref, b_ref, o_ref, acc = refs
        k = pl.program_id(2)

        @pl.when(k == 0)
        def _():
            acc[...] = jnp.zeros_like(acc)

        acc[...] += _raw(a_ref[...], b_ref[...], _TN)

        @pl.when(k == ns - 1)
        def _():
            o_ref[...] = (acc[...] * sc_ref[...] if has_scale else acc[...]).astype(out_dtype)

    in_specs = [pl.BlockSpec((ts, tm), lambda i, j, k: (k, i)), b_spec(ts, lambda i, j, k: (k, j))]
    if has_scale:
        in_specs.append(pl.BlockSpec((1, tn), lambda i, j, k: (0, j)))
    return pl.pallas_call(
        body, name=name, grid=(m // tm, n // tn, ns), in_specs=in_specs,
        out_specs=pl.BlockSpec((tm, tn), lambda i, j, k: (i, j)),
        out_shape=jax.ShapeDtypeStruct((m, n), out_dtype),
        scratch_shapes=[pltpu.VMEM((tm, tn), F32)],
        compiler_params=_params("arbitrary", "arbitrary", "arbitrary"),
    )(*([a, b] + ([scale] if has_scale else [])))


def ada_mod(c16, w):
    ncol = w.shape[2]

    def body(c_ref, w_ref, o_ref, a_ref):
        cc = c_ref[...]
        act = cc * _sig(cc)
        a_ref[...] = act
        o_ref[...] = _raw(act, w_ref[...], _NN)

    return pl.pallas_call(
        body, name="ada_mod", grid=(LAYERS,),
        in_specs=[pl.BlockSpec((16, D), lambda l: (0, 0)), pl.BlockSpec((None, D, ncol), lambda l: (l, 0, 0))],
        out_specs=[pl.BlockSpec((None, 16, ncol), lambda l: (l, 0, 0)), pl.BlockSpec((16, D), lambda l: (0, 0))],
        out_shape=[jax.ShapeDtypeStruct((LAYERS, 16, ncol), F32), jax.ShapeDtypeStruct((16, D), F32)],
        compiler_params=_params("arbitrary"),
    )(c16, w)


def _mla_shared(q_lat, c_kv, kr, krs, qa_w, kva_w, kr_w, krs_w, cos2, sin2):
    qn = _rms(q_lat, qa_w, 384.0)
    kvn = _rms(c_kv, kva_w, 256.0)
    rk = lax.rsqrt(jnp.sum(kr * kr, axis=-1, keepdims=True) / 32.0 + EPS)
    krope = rk * (kr * kr_w * cos2 + krs * krs_w * sin2)
    return qn, kvn, krope


def _mla_head(qn, kvn, wqn, wqr, wqrs, wkn, wv, qn_w, qr_w, qrs_w, kn_w, cos2, sin2):
    qnope = _rms(mm_nn(qn, wqn), qn_w, 64.0)
    qr, qrs = mm_nn(qn, wqr), mm_nn(qn, wqrs)
    rq = lax.rsqrt(jnp.sum(qr * qr, axis=-1, keepdims=True) / 32.0 + EPS)
    qrope = rq * (qr * qr_w * cos2 + qrs * qrs_w * sin2)
    knope = _rms(mm_nn(kvn, wkn), kn_w, 64.0)
    return qnope, qrope, knope, mm_nn(kvn, wv)


def _mla_vec_pieces(v_ref):
    return ((v_ref[0:1, 0:384], v_ref[1:2, 0:256], v_ref[3:4, 128:256], v_ref[3:4, 256:384]),
            (v_ref[2:3, 0:128], v_ref[2:3, 128:256], v_ref[2:3, 256:384], v_ref[3:4, 0:128]))


def _mla_in_specs(tr):
    return [pl.BlockSpec((tr, 384), lambda i: (i, O_QL // 384)), pl.BlockSpec((tr, 256), lambda i: (i, O_CKV // 256)),
            pl.BlockSpec((tr, 128), lambda i: (i, O_KR // 128)), pl.BlockSpec((tr, 128), lambda i: (i, O_KRS // 128)),
            pl.BlockSpec((HEADS, 384, 384), lambda i: (0, 0, 0), **CONST),
            pl.BlockSpec((HEADS, 256, 256), lambda i: (0, 0, 0), **CONST),
            pl.BlockSpec((8, 512), lambda i: (0, 0)),
            pl.BlockSpec((tr, 128), lambda i: (i, 0)), pl.BlockSpec((tr, 128), lambda i: (i, 0))]


def mla_pre_fwd(proj, wq, wkv, vec, cos2, sin2, name):
    s = proj.shape[0]
    tr = _pick(s, 256)

    def body(ql_ref, ckv_ref, kr_ref, krs_ref, wq_ref, wkv_ref, v_ref, cos_ref, sin_ref, q_out, k_out, v_out):
        vshared, vhead = _mla_vec_pieces(v_ref)
        cos2_, sin2_ = cos_ref[...], sin_ref[...]
        qlat_n, kv_n, krope = _mla_shared(ql_ref[...], ckv_ref[...], kr_ref[...], krs_ref[...], *vshared, cos2_, sin2_)
        qlat_n, kv_n, krope = qlat_n.astype(BF16), kv_n.astype(BF16), krope.astype(BF16)
        for h in range(HEADS):
            ws = (wq_ref[h, :, 0:128], wq_ref[h, :, 128:256], wq_ref[h, :, 256:384],
                  wkv_ref[h, :, 0:128], wkv_ref[h, :, 128:256])
            qn, qr, kn, v = _mla_head(qlat_n, kv_n, *ws, *vhead, cos2_, sin2_)
            q_out[h, :, 0:128] = qn.astype(BF16)
            q_out[h, :, 128:256] = qr.astype(BF16)
            k_out[h, :, 0:128] = kn.astype(BF16)
            k_out[h, :, 128:256] = krope
            v_out[h] = v.astype(BF16)

    return pl.pallas_call(
        body, name=name, grid=(s // tr,), in_specs=_mla_in_specs(tr),
        out_specs=[pl.BlockSpec((HEADS, tr, 256), lambda i: (0, i, 0)), pl.BlockSpec((HEADS, tr, 256), lambda i: (0, i, 0)),
                   pl.BlockSpec((HEADS, tr, 128), lambda i: (0, i, 0))],
        out_shape=[jax.ShapeDtypeStruct((HEADS, s, 256), BF16), jax.ShapeDtypeStruct((HEADS, s, 256), BF16),
                   jax.ShapeDtypeStruct((HEADS, s, 128), BF16)],
        compiler_params=_params("arbitrary"),
    )(proj, proj, proj, proj, wq, wkv, vec, cos2, sin2)


def mla_pre_bwd(proj, wq, wkv, vec, cos2, sin2, dq, dk, dv, name):
    s = proj.shape[0]
    tr = _pick(s, 256)

    def body(ql_ref, ckv_ref, kr_ref, krs_ref, wq_ref, wkv_ref, v_ref, cos_ref, sin_ref, dq_ref, dk_ref, dv_ref,
             dql_out, dckv_out, dkr_out, dkrs_out, dwq_out, dwkv_out, dvec_out):
        @pl.when(pl.program_id(0) == 0)
        def _():
            dwq_out[...] = jnp.zeros_like(dwq_out)
            dwkv_out[...] = jnp.zeros_like(dwkv_out)
            dvec_out[...] = jnp.zeros_like(dvec_out)

        vshared, vhead = _mla_vec_pieces(v_ref)
        cos2_, sin2_ = cos_ref[...], sin_ref[...]
        fs = lambda *a: _mla_shared(*a, cos2_, sin2_)
        (qlat_n, kv_n, _), vjp_shared = jax.vjp(fs, ql_ref[...], ckv_ref[...], kr_ref[...], krs_ref[...], *vshared)

        def head(h, carry):
            wq_h, wkv_h = wq_ref[h].astype(F32), wkv_ref[h].astype(F32)
            ws = (wq_h[:, 0:128], wq_h[:, 128:256], wq_h[:, 256:384], wkv_h[:, 0:128], wkv_h[:, 128:256])
            f = lambda *a: _mla_head(*a, cos2_, sin2_)
            _, vjp = jax.vjp(f, qlat_n, kv_n, *ws, *vhead)
            dq_h, dk_h = dq_ref[h], dk_ref[h]
            g = vjp((dq_h[:, 0:128], dq_h[:, 128:256], dk_h[:, 0:128], dv_ref[h]))
            dwq_out[h, :, 0:128] += g[2]
            dwq_out[h, :, 128:256] += g[3]
            dwq_out[h, :, 256:384] += g[4]
            dwkv_out[h, :, 0:128] += g[5]
            dwkv_out[h, :, 128:256] += g[6]
            dvec_out[2:3, 0:128] += g[7]
            dvec_out[2:3, 128:256] += g[8]
            dvec_out[2:3, 256:384] += g[9]
            dvec_out[3:4, 0:128] += g[10]
            return carry[0] + g[0], carry[1] + g[1], carry[2] + dk_h[:, 128:256]

        zero = lambda w: jnp.zeros((tr, w), F32)
        dqn, dkvn, dkrope = lax.fori_loop(0, HEADS, head, (zero(384), zero(256), zero(128)))
        g = vjp_shared((dqn, dkvn, dkrope))
        dql_out[...] = g[0].astype(BF16)
        dckv_out[...] = g[1].astype(BF16)
        dkr_out[...] = g[2].astype(BF16)
        dkrs_out[...] = g[3].astype(BF16)
        dvec_out[0:1, 0:384] += g[4]
        dvec_out[1:2, 0:256] += g[5]
        dvec_out[3:4, 128:256] += g[6]
        dvec_out[3:4, 256:384] += g[7]

    hb = lambda w: pl.BlockSpec((HEADS, tr, w), lambda i: (0, i, 0))
    return pl.pallas_call(
        body, name=name, grid=(s // tr,), in_specs=_mla_in_specs(tr) + [hb(256), hb(256), hb(128)],
        out_specs=[pl.BlockSpec((tr, 384), lambda i: (i, 0)), pl.BlockSpec((tr, 256), lambda i: (i, 0)),
                   pl.BlockSpec((tr, 128), lambda i: (i, 0)), pl.BlockSpec((tr, 128), lambda i: (i, 0)),
                   pl.BlockSpec((HEADS, 384, 384), lambda i: (0, 0, 0)), pl.BlockSpec((HEADS, 256, 256), lambda i: (0, 0, 0)),
                   pl.BlockSpec((8, 512), lambda i: (0, 0))],
        out_shape=[jax.ShapeDtypeStruct((s, 384), BF16), jax.ShapeDtypeStruct((s, 256), BF16),
                   jax.ShapeDtypeStruct((s, 128), BF16), jax.ShapeDtypeStruct((s, 128), BF16),
                   jax.ShapeDtypeStruct((HEADS, 384, 384), F32), jax.ShapeDtypeStruct((HEADS, 256, 256), F32),
                   jax.ShapeDtypeStruct((8, 512), F32)],
        compiler_params=_params("arbitrary"),
    )(proj, proj, proj, proj, wq, wkv, vec, cos2, sin2, dq, dk, dv)


def _att_probs(q, kk, i, tq):
    sc = _raw(q, kk, _NT) * ATT_SCALE
    rows = lax.broadcasted_iota(jnp.int32, sc.shape, 0) + i * tq
    cols = lax.broadcasted_iota(jnp.int32, sc.shape, 1)
    sc = jnp.where(cols <= rows, sc, -jnp.inf)
    e = jnp.exp(sc - jnp.max(sc, axis=-1, keepdims=True))
    return e / jnp.sum(e, axis=-1, keepdims=True)


def mla_attn_fwd(q, k, v, name):
    s = q.shape[1]
    tq = _pick(s, 256)

    def body(q_ref, k_ref, v_ref, o_ref):
        for i in range(s // tq):
            n = (i + 1) * tq
            p = _att_probs(q_ref[i * tq:n, :], k_ref[0:n, :], i, tq)
            o_ref[i * tq:n, :] = _raw(p, v_ref[0:n, :], _NN)

    hs = lambda w: pl.BlockSpec((None, s, w), lambda h: (h, 0, 0))
    return pl.pallas_call(
        body, name=name, grid=(HEADS,), in_specs=[hs(256), hs(256), hs(128)],
        out_specs=pl.BlockSpec((s, 128), lambda h: (0, h)),
        out_shape=jax.ShapeDtypeStruct((s, HEADS * 128), F32),
        compiler_params=_params("arbitrary"),
    )(q, k, v)


def mla_attn_bwd(q, k, v, do, name):
    s = q.shape[1]
    tq = _pick(s, 256)

    def body(q_ref, k_ref, v_ref, do_ref, dq_ref, dk_ref, dv_ref):
        dk_ref[...] = jnp.zeros_like(dk_ref)
        dv_ref[...] = jnp.zeros_like(dv_ref)
        for i in range(s // tq):
            n = (i + 1) * tq
            qq, kk, vv = q_ref[i * tq:n, :], k_ref[0:n, :], v_ref[0:n, :]
            p = _att_probs(qq, kk, i, tq)
            o = _raw(p, vv, _NN)
            dout = do_ref[i * tq:n, :]
            delta = jnp.sum(dout * o, axis=-1, keepdims=True)
            dp = _raw(dout, vv, _NT)
            ds = p * (dp - delta) * ATT_SCALE
            dq_ref[i * tq:n, :] = _raw(ds, kk, _NN)
            dk_ref[0:n, :] += _raw(ds, qq, _TN)
            dv_ref[0:n, :] += _raw(p, dout, _TN)

    hs = lambda w: pl.BlockSpec((None, s, w), lambda h: (h, 0, 0))
    return pl.pallas_call(
        body, name=name, grid=(HEADS,),
        in_specs=[hs(256), hs(256), hs(128), pl.BlockSpec((s, 128), lambda h: (0, h))],
        out_specs=[hs(256), hs(256), hs(128)],
        out_shape=[jax.ShapeDtypeStruct((HEADS, s, 256), F32), jax.ShapeDtypeStruct((HEADS, s, 256), F32),
                   jax.ShapeDtypeStruct((HEADS, s, 128), F32)],
        compiler_params=_params("arbitrary"),
    )(q, k, v, do)


def _pool_windows(u, pad, s, g):
    pad[0:16, :] = jnp.zeros((16, 128), F32)
    cur, sel = u, None
    for j, k in enumerate((1, 2, 4, 8)):
        pad[16:16 + s, :] = cur
        cur = cur + pad[16 - k:16 - k + s, :]
        sel = cur if sel is None else jnp.where(g == j, cur, sel)
    return sel


def _pool_count(s, g):
    t = lax.broadcasted_iota(jnp.int32, (s, 1), 0)
    return jnp.minimum(t + 1, 2 << g).astype(F32)


def pool_fwd(proj, pw, ps, name):
    s = proj.shape[0]

    def body(u_ref, w_ref, s_ref, o_ref, pad):
        g = pl.program_id(0)
        u = u_ref[...]
        pooled = _pool_windows(u, pad, s, g) / _pool_count(s, g) - u
        o_ref[...] = _raw(pooled, w_ref[...], _NN) * s_ref[...]

    return pl.pallas_call(
        body, name=name, grid=(4,),
        in_specs=[pl.BlockSpec((s, 128), lambda g: (0, O_PU // 128 + g)), pl.BlockSpec((None, 128, 128), lambda g: (g, 0, 0)),
                  pl.BlockSpec((1, 128), lambda g: (0, g))],
        out_specs=pl.BlockSpec((s, 128), lambda g: (0, g)),
        out_shape=jax.ShapeDtypeStruct((s, 512), F32),
        scratch_shapes=[pltpu.VMEM((s + 16, 128), F32)],
        compiler_params=_params("arbitrary"),
    )(proj, pw, ps)


def pool_bwd(proj, pw, ps, do, name):
    s = proj.shape[0]

    def body(u_ref, w_ref, s_ref, do_ref, du_ref, dw_ref, ds_ref, pad):
        g = pl.program_id(0)
        u, w, dout = u_ref[...], w_ref[...], do_ref[...]
        cnt = _pool_count(s, g)
        pooled = _pool_windows(u, pad, s, g) / cnt - u
        mixed = _raw(pooled, w, _NN)
        ds_ref[...] = jnp.sum(dout * mixed, axis=0, keepdims=True)
        dmixed = dout * s_ref[...]
        dw_ref[...] = _raw(pooled, dmixed, _TN)
        dpooled = _raw(dmixed, w, _NT)
        dsel = dpooled / cnt
        pad[s:s + 16, :] = jnp.zeros((16, 128), F32)
        cur = jnp.where(g == 3, dsel, 0.0)
        for j, k in ((2, 8), (1, 4), (0, 2)):
            pad[0:s, :] = cur
            cur = cur + pad[k:k + s, :] + jnp.where(g == j, dsel, 0.0)
        pad[0:s, :] = cur
        cur = cur + pad[1:1 + s, :]
        du_ref[...] = (cur - dpooled).astype(BF16)

    return pl.pallas_call(
        body, name=name, grid=(4,),
        in_specs=[pl.BlockSpec((s, 128), lambda g: (0, O_PU // 128 + g)), pl.BlockSpec((None, 128, 128), lambda g: (g, 0, 0)),
                  pl.BlockSpec((1, 128), lambda g: (0, g)), pl.BlockSpec((s, 128), lambda g: (0, g))],
        out_specs=[pl.BlockSpec((s, 128), lambda g: (0, g)), pl.BlockSpec((None, 128, 128), lambda g: (g, 0, 0)),
                   pl.BlockSpec((1, 128), lambda g: (0, g))],
        out_shape=[jax.ShapeDtypeStruct((s, 512), BF16), jax.ShapeDtypeStruct((4, 128, 128), F32),
                   jax.ShapeDtypeStruct((1, 512), F32)],
        scratch_shapes=[pltpu.VMEM((s + 16, 128), F32)],
        compiler_params=_params("arbitrary"),
    )(proj, pw, ps, do)


def _xbc_col(i):
    return jnp.where(i < 2, O_XS // 512 + i, O_BC // 512)


def conv_fwd(proj, cw, cb, name):
    s = proj.shape[0]

    def body(x_ref, w_ref, b_ref, o_ref, t_ref, pad):
        pad[0:8, :] = jnp.zeros((8, 512), F32)
        pad[8:8 + s, :] = x_ref[...]
        y = b_ref[...] + sum(w_ref[k:k + 1, :] * pad[5 + k:5 + k + s, :] for k in range(4))
        act = y * _sig(y)
        o_ref[...] = act

        @pl.when(pl.program_id(0) < 2)
        def _():
            t_ref[...] = act.T

    return pl.pallas_call(
        body, name=name, grid=(3,),
        in_specs=[pl.BlockSpec((s, 512), lambda i: (0, _xbc_col(i))), pl.BlockSpec((4, 512), lambda i: (0, i)),
                  pl.BlockSpec((1, 512), lambda i: (0, i))],
        out_specs=[pl.BlockSpec((s, 512), lambda i: (0, i)), pl.BlockSpec((512, s), lambda i: (jnp.minimum(i, 1), 0))],
        out_shape=[jax.ShapeDtypeStruct((s, 1536), F32), jax.ShapeDtypeStruct((D, s), F32)],
        scratch_shapes=[pltpu.VMEM((s + 8, 512), F32)],
        compiler_params=_params("arbitrary"),
    )(proj, cw, cb)


def conv_bwd(proj, cw, cb, dxt, dbm, dcm, name):
    s = proj.shape[0]

    def body(x_ref, w_ref, b_ref, dxt_ref, dbm_ref, dcm_ref, dx_ref, dw_ref, db_ref, pad, pad2):
        pad[0:8, :] = jnp.zeros((8, 512), F32)
        pad[8:8 + s, :] = x_ref[...]
        y = b_ref[...] + sum(w_ref[k:k + 1, :] * pad[5 + k:5 + k + s, :] for k in range(4))
        sg = _sig(y)

        @pl.when(pl.program_id(0) < 2)
        def _():
            pad2[0:s, :] = dxt_ref[...].T

        @pl.when(pl.program_id(0) == 2)
        def _():
            pad2[0:s, 0:256] = dbm_ref[...]
            pad2[0:s, 256:512] = dcm_ref[...]

        dy = pad2[0:s, :] * (sg * (1.0 + y * (1.0 - sg)))
        db_ref[...] = jnp.sum(dy, axis=0, keepdims=True)
        for k in range(4):
            dw_ref[k:k + 1, :] = jnp.sum(dy * pad[5 + k:5 + k + s, :], axis=0, keepdims=True)
        pad2[s:s + 8, :] = jnp.zeros((8, 512), F32)
        pad2[0:s, :] = dy
        dx_ref[...] = sum(w_ref[k:k + 1, :] * pad2[3 - k:3 - k + s, :] for k in range(4)).astype(BF16)

    return pl.pallas_call(
        body, name=name, grid=(3,),
        in_specs=[pl.BlockSpec((s, 512), lambda i: (0, _xbc_col(i))), pl.BlockSpec((4, 512), lambda i: (0, i)),
                  pl.BlockSpec((1, 512), lambda i: (0, i)), pl.BlockSpec((512, s), lambda i: (jnp.minimum(i, 1), 0)),
                  pl.BlockSpec((s, 256), lambda i: (0, 0)), pl.BlockSpec((s, 256), lambda i: (0, 0))],
        out_specs=[pl.BlockSpec((s, 512), lambda i: (0, i)), pl.BlockSpec((4, 512), lambda i: (0, i)),
                   pl.BlockSpec((1, 512), lambda i: (0, i))],
        out_shape=[jax.ShapeDtypeStruct((s, 1536), BF16), jax.ShapeDtypeStruct((4, 1536), F32),
                   jax.ShapeDtypeStruct((1, 1536), F32)],
        scratch_shapes=[pltpu.VMEM((s + 8, 512), F32), pltpu.VMEM((s + 8, 512), F32)],
        compiler_params=_params("arbitrary"),
    )(proj, cw, cb, dxt, dbm, dcm)


def _ssd_chunk(xt, dtr, dtc, bm, cm, hprev, alog, dbias, dskip):
    ln = 128
    a = -jnp.exp(alog)
    dt_r = softplus(dtr + dbias)
    da_r = dt_r * a
    da_c = softplus(dtc + dbias) * a
    li = lax.broadcasted_iota(jnp.int32, (1, ln, ln), 1)
    si = lax.broadcasted_iota(jnp.int32, (1, ln, ln), 2)
    causal = si <= li
    acs_c = jnp.sum(jnp.where(causal, da_r, 0.0), axis=2, keepdims=True)
    acs_r = jnp.sum(jnp.where(li <= si, da_c, 0.0), axis=1, keepdims=True)
    acs_last = jnp.sum(da_r, axis=2, keepdims=True)
    decay = jnp.exp(jnp.where(causal, acs_c - acs_r, -jnp.inf))
    m = mm_nt(cm, bm)[None] * decay
    xdt = xt * dt_r
    y_diag = bmm_nt(xdt, m)
    bb = jnp.broadcast_to(bm[None], (8, ln, ln))
    cc = jnp.broadcast_to(cm[None], (8, ln, ln))
    states = bmm_nn(xdt * jnp.exp(acs_last - acs_r), bb)
    y_off = bmm_nt(hprev, cc) * jnp.exp(acs_r)
    hnew = hprev * jnp.exp(acs_last) + states
    return y_diag + y_off + xt * dskip, hnew


def _ssd_specs(nc, rev):
    cix = (lambda c: nc - 1 - c) if rev else (lambda c: c)
    hv = pl.BlockSpec((8, 1, 1), lambda g, c: (g, 0, 0))
    return [pl.BlockSpec((8, 64, 128), lambda g, c: (g, 0, cix(c))), pl.BlockSpec((8, 1, 128), lambda g, c: (g, 0, cix(c))),
            pl.BlockSpec((8, 128, 1), lambda g, c: (g, cix(c), 0)), pl.BlockSpec((128, 128), lambda g, c: (cix(c), 8 + g)),
            pl.BlockSpec((128, 128), lambda g, c: (cix(c), 10 + g))], hv, cix


def ssd_fwd(xt, dtr, dtc, xbc, alog, dbias, dskip, name):
    s = xt.shape[2]
    nc = s // 128
    specs, hv, _ = _ssd_specs(nc, False)

    def body(x_ref, dr_ref, dc_ref, b_ref, c_ref, al_ref, db_ref, dk_ref, y_ref, hs_ref, h_scr):
        @pl.when(pl.program_id(1) == 0)
        def _():
            h_scr[...] = jnp.zeros_like(h_scr)
        hp = h_scr[...]
        hs_ref[...] = hp
        y, hn = _ssd_chunk(x_ref[...], dr_ref[...], dc_ref[...], b_ref[...], c_ref[...], hp,
                           al_ref[...], db_ref[...], dk_ref[...])
        y_ref[...] = y
        h_scr[...] = hn

    return pl.pallas_call(
        body, name=name, grid=(2, nc), in_specs=specs + [hv, hv, hv],
        out_specs=[pl.BlockSpec((8, 64, 128), lambda g, c: (g, 0, c)),
                   pl.BlockSpec((None, None, 8, 64, 128), lambda g, c: (g, c, 0, 0, 0))],
        out_shape=[jax.ShapeDtypeStruct((16, 64, s), F32), jax.ShapeDtypeStruct((2, nc, 8, 64, 128), F32)],
        scratch_shapes=[pltpu.VMEM((8, 64, 128), F32)],
        compiler_params=_params("arbitrary", "arbitrary"),
    )(xt, dtr, dtc, xbc, xbc, alog, dbias, dskip)


def ssd_bwd(xt, dtr, dtc, xbc, alog, dbias, dskip, hs, dyt, name):
    s = xt.shape[2]
    nc = s // 128
    specs, hv, cix = _ssd_specs(nc, True)

    def body(x_ref, dr_ref, dc_ref, b_ref, c_ref, al_ref, db_ref, dk_ref, hs_ref, dy_ref,
             dx_out, ddr_out, ddc_out, dbm_out, dcm_out, dal_out, ddb_out, ddk_out, dh_scr):
        @pl.when(pl.program_id(1) == 0)
        def _():
            dh_scr[...] = jnp.zeros_like(dh_scr)
            dal_out[...] = jnp.zeros_like(dal_out)
            ddb_out[...] = jnp.zeros_like(ddb_out)
            ddk_out[...] = jnp.zeros_like(ddk_out)
        _, vjp = jax.vjp(_ssd_chunk, x_ref[...], dr_ref[...], dc_ref[...], b_ref[...], c_ref[...], hs_ref[...],
                         al_ref[...], db_ref[...], dk_ref[...])
        g = vjp((dy_ref[...], dh_scr[...]))
        dx_out[...] = g[0]
        ddr_out[...] = g[1]
        ddc_out[...] = g[2]
        dbm_out[...] = g[3]
        dcm_out[...] = g[4]
        dh_scr[...] = g[5]
        dal_out[...] += g[6]
        ddb_out[...] += g[7]
        ddk_out[...] += g[8]

    return pl.pallas_call(
        body, name=name, grid=(2, nc),
        in_specs=specs + [hv, hv, hv, pl.BlockSpec((None, None, 8, 64, 128), lambda g, c: (g, cix(c), 0, 0, 0)),
                          pl.BlockSpec((8, 64, 128), lambda g, c: (g, 0, cix(c)))],
        out_specs=[pl.BlockSpec((8, 64, 128), lambda g, c: (g, 0, cix(c))), pl.BlockSpec((8, 1, 128), lambda g, c: (g, 0, cix(c))),
                   pl.BlockSpec((8, 128, 1), lambda g, c: (g, cix(c), 0)), pl.BlockSpec((128, 128), lambda g, c: (cix(c), g)),
                   pl.BlockSpec((128, 128), lambda g, c: (cix(c), g)), hv, hv, hv],
        out_shape=[jax.ShapeDtypeStruct((16, 64, s), F32), jax.ShapeDtypeStruct((16, 1, s), F32),
                   jax.ShapeDtypeStruct((16, s, 1), F32), jax.ShapeDtypeStruct((s, 256), F32),
                   jax.ShapeDtypeStruct((s, 256), F32)] + [jax.ShapeDtypeStruct((16, 1, 1), F32)] * 3,
        scratch_shapes=[pltpu.VMEM((8, 64, 128), F32)],
        compiler_params=_params("arbitrary", "arbitrary"),
    )(xt, dtr, dtc, xbc, xbc, alog, dbias, dskip, hs, dyt)


def _merge(oa, ob, y, z, gla, glb, glc, x, g1, nw, ea, eb, ec, eo, wba, wbb, wbc, wout):
    gated = y * (z * _sig(z))
    sq = gated * gated
    left = lax.broadcasted_iota(jnp.int32, (1, D), 1) < 512
    ms0 = jnp.sum(jnp.where(left, sq, 0.0), axis=-1, keepdims=True) / 512.0
    ms1 = jnp.sum(jnp.where(left, 0.0, sq), axis=-1, keepdims=True) / 512.0
    oc = gated * jnp.where(left, lax.rsqrt(ms0 + EPS), lax.rsqrt(ms1 + EPS)) * nw
    ya, yb, yc = mm_nc(oa, wba) + ea, mm_nc(ob, wbb) + eb, mm_nc(oc, wbc) + ec
    merged = _sig(gla) * ya + _sig(glb) * yb + _sig(glc) * yc
    x1 = x + g1 * (mm_nc(merged, wout) + eo)
    return x1, (oc, merged)


def _merge_specs(tr):
    row = lambda w: pl.BlockSpec((tr, w), lambda i: (i, 0))
    acts = [row(D), row(512), pl.BlockSpec((D, tr), lambda i: (0, i)), pl.BlockSpec((tr, D), lambda i: (i, O_Z // D)),
            pl.BlockSpec((tr, 3 * D), lambda i: (i, 0)), row(D), pl.BlockSpec((8, D), lambda i: (0, 0))]
    cst = lambda r: pl.BlockSpec((r, D), lambda i: (0, 0), **CONST)
    return acts, [cst(D), cst(512), cst(D), cst(D)], row


def merge_fwd(oa, ob, y, proj, x, mvec, wba, wbb, wbc, wout, name):
    s = x.shape[0]
    tr = _pick(s, 256)
    acts, wts, row = _merge_specs(tr)

    def body(oa_ref, ob_ref, y_ref, z_ref, gl_ref, x_ref, mv_ref, wba_ref, wbb_ref, wbc_ref, wout_ref, o_ref):
        zero = jnp.zeros((1, D), F32)
        x1, _ = _merge(oa_ref[...], ob_ref[...], y_ref[...].T, z_ref[...], gl_ref[:, 0:D], gl_ref[:, D:2 * D],
                       gl_ref[:, 2 * D:3 * D], x_ref[...], mv_ref[0:1, :], mv_ref[1:2, :], zero, zero, zero, zero,
                       wba_ref[...], wbb_ref[...], wbc_ref[...], wout_ref[...])
        o_ref[...] = x1

    return pl.pallas_call(
        body, name=name, grid=(s // tr,), in_specs=acts + wts, out_specs=row(D),
        out_shape=jax.ShapeDtypeStruct((s, D), F32), compiler_params=_params("arbitrary"),
    )(oa, ob, y, proj, proj, x, mvec, wba, wbb, wbc, wout)


def merge_bwd(oa, ob, y, proj, x, mvec, wba, wbb, wbc, wout, dx1, name):
    s = x.shape[0]
    tr = _pick(s, 128)
    acts, wts, row = _merge_specs(tr)

    def body(oa_ref, ob_ref, y_ref, z_ref, gl_ref, x_ref, mv_ref, wba_ref, wbb_ref, wbc_ref, wout_ref, dx1_ref,
             doa_o, dob_o, dy_o, dz_o, dgl_o, dx_o, dmv_o, dya_o, dyb_o, dyc_o, dpre_o, oc_o, mg_o):
        zero = jnp.zeros((tr, D), F32)
        wts_ = (wba_ref[...], wbb_ref[...], wbc_ref[...], wout_ref[...])
        f = lambda *a: _merge(*a, *wts_)
        _, vjp, (oc, merged) = jax.vjp(
            f, oa_ref[...], ob_ref[...], y_ref[...].T, z_ref[...], gl_ref[:, 0:D], gl_ref[:, D:2 * D],
            gl_ref[:, 2 * D:3 * D], x_ref[...], mv_ref[0:1, :], mv_ref[1:2, :], zero, zero, zero, zero, has_aux=True)
        g = vjp(dx1_ref[...])
        doa_o[...] = g[0]
        dob_o[...] = g[1]
        dy_o[...] = g[2].T
        dz_o[...] = g[3].astype(BF16)
        dgl_o[:, 0:D] = g[4].astype(BF16)
        dgl_o[:, D:2 * D] = g[5].astype(BF16)
        dgl_o[:, 2 * D:3 * D] = g[6].astype(BF16)
        dx_o[...] = g[7]

        @pl.when(pl.program_id(0) == 0)
        def _():
            dmv_o[...] = jnp.zeros_like(dmv_o)

        dmv_o[0:1, :] += g[8]
        dmv_o[1:2, :] += g[9]
        dya_o[...] = g[10].astype(BF16)
        dyb_o[...] = g[11].astype(BF16)
        dyc_o[...] = g[12].astype(BF16)
        dpre_o[...] = g[13].astype(BF16)
        oc_o[...] = oc.astype(BF16)
        mg_o[...] = merged.astype(BF16)

    sd = lambda w, dt: jax.ShapeDtypeStruct((s, w), dt)
    return pl.pallas_call(
        body, name=name, grid=(s // tr,), in_specs=acts + wts + [row(D)],
        out_specs=[row(D), row(512), pl.BlockSpec((D, tr), lambda i: (0, i)), row(D), row(3 * D), row(D),
                   pl.BlockSpec((8, D), lambda i: (0, 0))] + [row(D)] * 6,
        out_shape=[sd(D, F32), sd(512, F32), jax.ShapeDtypeStruct((D, s), F32), sd(D, BF16), sd(3 * D, BF16), sd(D, F32),
                   jax.ShapeDtypeStruct((8, D), F32)] + [sd(D, BF16)] * 6,
        compiler_params=_params("arbitrary"),
    )(oa, ob, y, proj, proj, x, mvec, wba, wbb, wbc, wout, dx1)


def _conv3(u_scr, w_ref, first, rows, lanes):
    return sum(w_ref[k:k + 1, :] * u_scr[first + k:first + k + rows, lanes] for k in range(3))


def _ffn_tile_specs(tf, tile):
    def at(rows, off):
        return pl.BlockSpec((rows, tf), lambda *g: (0, off + tile(*g)))
    return [at(D, 0), at(D, FFN_NT), at(3, 0), at(3, FFN_NT), at(1, 0), at(1, FFN_NT)]


def ffn_fwd(x1, fvec, wup, cw, cb, wdn, name):
    s = x1.shape[0]
    tr, tf = _pick(s, 512), FFN_TILE
    lg, lv = slice(0, tf), slice(tf, 2 * tf)

    def body(x_ref, v_ref, wg_ref, wv_ref, cwg_ref, cwv_ref, cbg_ref, cbv_ref, wd_ref, x2_ref, h_ref, pre_ref,
             h_scr, u_scr, acc):
        i, t = pl.program_id(0), pl.program_id(1)

        @pl.when(t == 0)
        def _():
            @pl.when(i == 0)
            def _():
                h_scr[0:16, :] = jnp.zeros((16, D), BF16)

            @pl.when(i > 0)
            def _():
                h_scr[0:16, :] = h_scr[tr:tr + 16, :]

            h = (_rms(x_ref[...], v_ref[0:1, :], D) * (1.0 + v_ref[2:3, :]) + v_ref[1:2, :]).astype(BF16)
            h_scr[16:16 + tr, :] = h
            h_ref[...] = h
            acc[...] = jnp.zeros_like(acc)

        u_scr[:, lg] = jnp.dot(h_scr[...], wg_ref[...], preferred_element_type=F32)
        u_scr[:, lv] = jnp.dot(h_scr[...], wv_ref[...], preferred_element_type=F32)
        cg = _conv3(u_scr, cwg_ref, 14, tr, lg) + cbg_ref[...]
        cval = _conv3(u_scr, cwv_ref, 14, tr, lv) + cbv_ref[...]
        acc[...] += _raw(cg * _sig(cg) * cval, wd_ref[...], _NN)

        @pl.when(t == FFN_NT - 1)
        def _():
            pre_ref[...] = acc[...]
            x2_ref[...] = x_ref[...] + v_ref[3:4, :] * acc[...]

    row = pl.BlockSpec((tr, D), lambda i, t: (i, 0))
    return pl.pallas_call(
        body, name=name, grid=(s // tr, FFN_NT),
        in_specs=[row, pl.BlockSpec((8, D), lambda i, t: (0, 0))] + _ffn_tile_specs(tf, lambda i, t: t)
                 + [pl.BlockSpec((tf, D), lambda i, t: (t, 0))],
        out_specs=[row, row, row],
        out_shape=[jax.ShapeDtypeStruct((s, D), F32), jax.ShapeDtypeStruct((s, D), BF16), jax.ShapeDtypeStruct((s, D), F32)],
        scratch_shapes=[pltpu.VMEM((tr + 16, D), BF16), pltpu.VMEM((tr + 16, 2 * tf), F32), pltpu.VMEM((tr, D), F32)],
        compiler_params=_params("arbitrary", "arbitrary"),
    )(x1, fvec, wup, wup, cw, cw, cb, cb, wdn)


def ffn_bwd(h2, dx2, fvec, wup, cw, cb, wdn, name):
    s = h2.shape[0]
    tr, tf = _pick(s, 512), FFN_TILE
    ni, nb = s // tr, s // 16
    lg, lv = slice(0, tf), slice(tf, 2 * tf)

    def body(hp_ref, hm_ref, hn_ref, dm_ref, dn_ref, v_ref, wg_ref, wv_ref, cwg_ref, cwv_ref, cbg_ref, cbv_ref, wd_ref,
             dup_ref, act_ref, dcw_ref, u_scr, dc_scr):
        i = pl.program_id(1)
        hfull = jnp.concatenate([jnp.where(i > 0, hp_ref[...], jnp.zeros((16, D), BF16)), hm_ref[...],
                                 jnp.where(i < ni - 1, hn_ref[...], jnp.zeros((16, D), BF16))], axis=0)
        u_scr[:, lg] = jnp.dot(hfull, wg_ref[...], preferred_element_type=F32)
        u_scr[:, lv] = jnp.dot(hfull, wv_ref[...], preferred_element_type=F32)
        cg = _conv3(u_scr, cwg_ref, 14, tr + 16, lg) + cbg_ref[...]
        cval = _conv3(u_scr, cwv_ref, 14, tr + 16, lv) + cbv_ref[...]
        g2 = v_ref[3:4, :]
        dpre = jnp.concatenate([dm_ref[...] * g2, jnp.where(i < ni - 1, dn_ref[...], 0.0) * g2], axis=0)
        dact = _raw(dpre, wd_ref[...], _NT)
        sg = _sig(cg)
        sl = cg * sg
        dc_scr[:, lg] = dact * cval * (sg * (1.0 + cg * (1.0 - sg)))
        dc_scr[:, lv] = dact * sl
        act_ref[...] = (sl * cval)[0:tr, :].astype(BF16)

        @pl.when(i == 0)
        def _():
            dcw_ref[...] = jnp.zeros_like(dcw_ref)

        for half, lanes, cw_ref in ((0, lg, cwg_ref), (1, lv, cwv_ref)):
            dup_ref[half] = sum(cw_ref[k:k + 1, :] * dc_scr[2 - k:2 - k + tr, lanes] for k in range(3)).astype(BF16)
            dcm = dc_scr[0:tr, lanes]
            for k in range(3):
                dcw_ref[half, k:k + 1, :] += jnp.sum(dcm * u_scr[14 + k:14 + k + tr, lanes], axis=0, keepdims=True)
            dcw_ref[half, 3:4, :] += jnp.sum(dcm, axis=0, keepdims=True)

    r16 = tr // 16
    prev = lambda t, i: (jnp.maximum(i * r16 - 1, 0), 0)
    nxt = lambda t, i: (jnp.minimum((i + 1) * r16, nb - 1), 0)
    main = lambda t, i: (i, 0)
    return pl.pallas_call(
        body, name=name, grid=(FFN_NT, ni),
        in_specs=[pl.BlockSpec((16, D), prev), pl.BlockSpec((tr, D), main), pl.BlockSpec((16, D), nxt),
                  pl.BlockSpec((tr, D), main), pl.BlockSpec((16, D), nxt), pl.BlockSpec((8, D), lambda t, i: (0, 0))]
                 + _ffn_tile_specs(tf, lambda t, i: t) + [pl.BlockSpec((tf, D), lambda t, i: (t, 0))],
        out_specs=[pl.BlockSpec((2, tr, tf), lambda t, i: (0, i, t)), pl.BlockSpec((tr, tf), lambda t, i: (i, t)),
                   pl.BlockSpec((2, 8, tf), lambda t, i: (0, 0, t))],
        out_shape=[jax.ShapeDtypeStruct((2, s, FFN), BF16), jax.ShapeDtypeStruct((s, FFN), BF16),
                   jax.ShapeDtypeStruct((2, 8, FFN), F32)],
        scratch_shapes=[pltpu.VMEM((tr + 32, 2 * tf), F32), pltpu.VMEM((tr + 16, 2 * tf), F32)],
        compiler_params=_params("arbitrary", "arbitrary"),
    )(h2, h2, h2, dx2, dx2, fvec, wup, wup, cw, cw, cb, cb, wdn)


def loss_head(y, target):
    s = y.shape[0]
    tr = _pick(s, 512)

    def body(y_ref, t_ref, dx_ref, l_ref):
        @pl.when(pl.program_id(0) == 0)
        def _():
            l_ref[...] = jnp.zeros_like(l_ref)
        err = y_ref[...] - t_ref[...]
        dx_ref[...] = err / float(D)
        l_ref[...] += 0.5 * jnp.sum(jnp.sum(err * err, axis=-1, keepdims=True) / float(D), axis=0, keepdims=True)

    row = pl.BlockSpec((tr, D), lambda i: (i, 0))
    return pl.pallas_call(
        body, name="loss_head", grid=(s // tr,), in_specs=[row, row],
        out_specs=[row, pl.BlockSpec((8, 128), lambda i: (0, 0))],
        out_shape=[jax.ShapeDtypeStruct((s, D), F32), jax.ShapeDtypeStruct((8, 128), F32)],
        compiler_params=_params("arbitrary"),
    )(y, target)


def adamw(parts, w, m, v, name, tok=None):
    nseg = len(parts)
    p, r, c = parts[0].shape
    tr = _pick(r, 256 if c > 128 else 2048, 8)
    ni = r // tr
    tok = jnp.zeros((8, 128), F32) if tok is None else tok

    def body(*refs):
        p_refs = refs[:nseg]
        w_ref, m_ref, v_ref, _, g_out, d_out, m_out, v_out, g_scr = refs[nseg:]
        for q in range(nseg):
            @pl.when(pl.program_id(0) == q)
            def _(q=q):
                g = p_refs[q][0].astype(F32)
                for j in range(1, p):
                    g = g + p_refs[q][j].astype(F32)
                g_scr[...] = g
        g = g_scr[...]
        mn = B1 * m_ref[...] + (1.0 - B1) * g
        vn = B2 * v_ref[...] + (1.0 - B2) * (g * g)
        m_hat = mn / (1.0 - B1 ** STEP)
        v_hat = vn / (1.0 - B2 ** STEP)
        g_out[...] = g
        d_out[...] = -LR * (m_hat / (jnp.sqrt(v_hat) + ADAM_EPS) + WD * w_ref[...])
        m_out[...] = mn
        v_out[...] = vn

    row = pl.BlockSpec((tr, c), lambda l, i: (l * ni + i, 0))
    part = lambda q: pl.BlockSpec((p, tr, c), lambda l, i: (0, jnp.clip((l - q) * ni + i, 0, ni - 1), 0))
    return pl.pallas_call(
        body, name=name, grid=(nseg, ni),
        in_specs=[part(q) for q in range(nseg)] + [row, row, row, pl.BlockSpec((8, 128), lambda l, i: (0, 0))],
        out_specs=[row] * 4, out_shape=[jax.ShapeDtypeStruct((nseg * r, c), F32)] * 4,
        scratch_shapes=[pltpu.VMEM((tr, c), F32)],
        compiler_params=_params("arbitrary", "arbitrary"),
    )(*parts, w, m, v, tok)


def _padc(a, n):
    return jnp.pad(a, [(0, 0)] * (a.ndim - 1) + [(0, n - a.shape[-1])])


def _swap16(a):
    return jnp.concatenate([a[..., 16:32], a[..., 0:16]], axis=-1)


def _shard_cols(g8, a, b):
    c = g8.shape[2]
    return [g8[j][:, max(a, j * c) - j * c:min(b, (j + 1) * c) - j * c] for j in range(a // c, (b - 1) // c + 1)]


def _win_layout(g8):
    cols = lambda a, b: _shard_cols(g8, a, b)
    kr = jnp.concatenate(cols(640, 672), axis=1)
    dt = jnp.concatenate(cols(3744, 3760), axis=1)
    return jnp.concatenate(cols(3760, 6832) + cols(2208, 3232) + cols(1184, 2208) + cols(672, 1184) + cols(3232, 3744)
                           + cols(384, 640) + [_padc(kr, 128), _padc(_swap16(kr), 128), _padc(dt, 128),
                                               jnp.zeros((g8.shape[1], 128), g8.dtype)] + cols(0, 384), axis=1)


def _win_grad_shards(g):
    kr = (g[:, O_KR:O_KR + 32].astype(F32) + _swap16(g[:, O_KRS:O_KRS + 32].astype(F32))).astype(g.dtype)
    segs = [(g, O_QL, 384), (g, O_CKV, 256), (kr, 0, 32), (g, O_PU, 512), (g, O_Z, D), (g, O_XS, D), (g, O_BC, 512),
            (g, O_DT, 16), (g, O_G, 3 * D)]
    shards, width = [], sum(w for _, _, w in segs) // NDEV
    for j in range(NDEV):
        a, b, off, pieces = width * j, width * (j + 1), 0, []
        for arr, lo, w in segs:
            s0, s1 = max(a, off), min(b, off + w)
            if s0 < s1:
                pieces.append(arr[:, lo + s0 - off:lo + s1 - off])
            off += w
        shards.append(jnp.concatenate(pieces, axis=1))
    return jnp.stack(shards).astype(BF16)


def _wq_layout(w):
    w = w.reshape(384, HEADS, 96).transpose(1, 0, 2)
    rope = w[:, :, 64:96]
    return jnp.concatenate([_padc(w[:, :, 0:64], 128), _padc(rope, 128), _padc(_swap16(rope), 128)], axis=2)


def _wq_unlayout(g):
    rope = g[:, :, 128:160] + _swap16(g[:, :, 256:288])
    return jnp.concatenate([g[:, :, 0:64], rope], axis=2).transpose(1, 0, 2).reshape(384, HEADS * 96)


def _wkv_layout(w):
    w = w.reshape(256, HEADS, 128).transpose(1, 0, 2)
    return jnp.concatenate([_padc(w[:, :, 0:64], 128), _padc(w[:, :, 64:128], 128)], axis=2)


def _wkv_unlayout(g):
    return jnp.concatenate([g[:, :, 0:64], g[:, :, 128:192]], axis=2).transpose(1, 0, 2).reshape(256, HEADS * 128)


def _wba_layout(w):
    return jnp.pad(w.reshape(HEADS, 64, D), ((0, 0), (0, 64), (0, 0))).reshape(HEADS * 128, D)


def _rows8(rows, width):
    out = jnp.stack([_padc(r.astype(F32), width) for r in rows])
    return jnp.pad(out, ((0, 8 - out.shape[0]), (0, 0)))


def _mla_vec(qa, kva, qn, kn):
    def row(n):
        return jnp.concatenate([_padc(n[0:64], 128), _padc(n[64:96], 128), _padc(_swap16(n[64:96]), 128)])
    return _rows8([qa, kva, row(qn), row(kn)], 512)


def _mla_unvec(g):
    def un(r):
        return jnp.concatenate([r[0:64], r[128:160] + _swap16(r[256:288])])
    return g[0, 0:384], g[1, 0:256], un(g[2]), un(g[3])


SMALL = (("ada_b", (6 * D,)), ("norm1_w", (D,)), ("q_a_norm", (384,)), ("kv_a_norm", (256,)), ("q_norm", (96,)),
         ("k_norm", (96,)), ("pool_w", (4, 128, 128)), ("pool_scale", (512,)), ("ssd_conv_b", (1536,)),
         ("ssd_dt_bias", (16,)), ("ssd_a_log", (16,)), ("ssd_d", (16,)), ("ssd_norm_w", (D,)), ("norm2_w", (D,)),
         ("ffn_conv_b", (2 * FFN,)), ("ssd_conv_w", (4, 1536)), ("ffn_conv_w", (3, 2 * FFN)))
SHARDED_SMALL = {"ssd_conv_w": 192, "ffn_conv_w": 704}


def _pack_rows(shp):
    return -(-math.prod(shp) // 1024) * 8


def _pack(small):
    pieces = []
    for n, shp in SMALL:
        pieces.append(small[n].reshape(-1).astype(F32))
        fill = _pack_rows(shp) * 128 - math.prod(shp)
        if fill:
            pieces.append(jnp.zeros((fill,), F32))
    return jnp.concatenate(pieces).reshape(-1, 128)


def _unpack_parts(packs):
    out, off = {}, 0
    for n, shp in SMALL:
        rows, size = _pack_rows(shp), math.prod(shp)
        r, c = math.prod(shp[:-1]), shp[-1]
        per_layer = [pk[:, off:off + rows].reshape(NDEV, rows * 128)[:, 0:size].reshape(NDEV, r, c) for pk in packs]
        out[n] = jnp.concatenate(per_layer, axis=1)
        off += rows
    return out


GROUP_A = ("w_in", "w_q_b", "w_kv_b")
GROUP_B = ("w_branch", "w_out", "ffn_up", "ffn_down")
BIG = GROUP_A + GROUP_B
COL_SHARDED = ("w_in", "w_q_b", "w_kv_b", "ffn_up")


def _behind(arrs, tok):
    arrs = list(arrs)
    j = min(range(len(arrs)), key=lambda q: arrs[q].size)
    arrs[j] = arrs[j] + tok[0, 0].astype(arrs[j].dtype)
    return arrs


def _gathered_full(g, name):
    if name in COL_SHARDED:
        return g.transpose(1, 0, 2).reshape(g.shape[1], NDEV * g.shape[2])
    return g.reshape(NDEV * g.shape[1], g.shape[2])


def _to_shards(full, name):
    if name == "w_in":
        return _win_grad_shards(full)
    if name in COL_SHARDED:
        r, c = full.shape
        return full.reshape(r, NDEV, c // NDEV).transpose(1, 0, 2).astype(BF16)
    r, c = full.shape
    return full.reshape(NDEV, r // NDEV, c).astype(BF16)


def _fwd_a(x, lw, mod, cos2, sin2, l, tok):
    sh1, sc1, g1, sh2, sc2, g2 = [mod[j * D:(j + 1) * D] for j in range(6)]
    vec1 = _rows8([lw["norm1_w"], sh1, sc1], D) + tok[0, 0]
    proj, h1, dt_cols = norm_proj_fwd(x, vec1, lw["win"], f"inproj_fwd{l}")
    q, k, v = mla_pre_fwd(proj, lw["wq"], lw["wkv"], lw["mla_vec"], cos2, sin2, f"mla_pre_fwd{l}")
    oa = mla_attn_fwd(q, k, v, f"mla_attn_fwd{l}")
    ob = pool_fwd(proj, lw["pool_w"], lw["pool_scale"].reshape(1, 512), f"pool_fwd{l}")
    xbc, xt = conv_fwd(proj, lw["ssd_conv_w"], lw["ssd_conv_b"].reshape(1, 1536), f"conv_fwd{l}")
    s = x.shape[0]
    xt = xt.reshape(16, 64, s)
    dt = dt_cols[:, 0:16].T
    dtr, dtc = dt[:, None, :], dt[:, :, None]
    hv = lambda a: a.reshape(16, 1, 1)
    yt, hs = ssd_fwd(xt, dtr, dtc, xbc, hv(lw["ssd_a_log"]), hv(lw["ssd_dt_bias"]), hv(lw["ssd_d"]), f"ssd_fwd{l}")
    return dict(x=x, vec1=vec1, proj=proj, h1=h1, q=q, k=k, v=v, oa=oa, ob=ob, xbc=xbc, xt=xt, dtr=dtr, dtc=dtc,
                hs=hs, yt=yt.reshape(D, s), mvec=_rows8([g1, lw["ssd_norm_w"]], D),
                fvec=_rows8([lw["norm2_w"], sh2, sc2, g2], D))


def _fwd_b(sv, lw, l, tok):
    sv["mvec"] = sv["mvec"] + tok[0, 0]
    x1 = merge_fwd(sv["oa"], sv["ob"], sv["yt"], sv["proj"], sv["x"], sv["mvec"], lw["wba"], lw["wbb"], lw["wbc"],
                   lw["wout"], f"merge_fwd{l}")
    x2, h2, pre = ffn_fwd(x1, sv["fvec"], lw["wup"], lw["ffn_conv_w"], lw["ffn_conv_b"].reshape(1, 2 * FFN), lw["wdn"],
                          f"ffn_fwd{l}")
    sv.update(x1=x1, h2=h2, pre=pre)
    return x2


def _bwd_b(dx2, lw, sv, l, tok):
    grads, small = {}, {}
    fvec = sv["fvec"] + tok[0, 0]
    dup, act, dcw = ffn_bwd(sv["h2"], dx2, fvec, lw["wup"], lw["ffn_conv_w"], lw["ffn_conv_b"].reshape(1, 2 * FFN),
                            lw["wdn"], f"ffn_bwd{l}")
    grads["ffn_down"] = tn_matmul(act, dx2, f"dw_down{l}", scale=fvec[3:4])
    grads["ffn_up"] = tn_matmul(sv["h2"], dup, f"dw_up{l}")
    dx1, dfvec = norm_proj_bwd(sv["x1"], fvec, dup, lw["wup"], dx2, sv["pre"], f"ffn_norm_bwd{l}")
    small["ffn_conv_w"] = jnp.concatenate([dcw[0, 0:3], dcw[1, 0:3]], axis=1)
    small["ffn_conv_b"] = jnp.concatenate([dcw[0, 3], dcw[1, 3]])
    small["norm2_w"] = dfvec[0]
    (doa, dob, dyt, dz, dgl, dx, dmvec, dya, dyb, dyc, dpre, oc, merged) = merge_bwd(
        sv["oa"], sv["ob"], sv["yt"], sv["proj"], sv["x"], sv["mvec"], lw["wba"], lw["wbb"], lw["wbc"], lw["wout"], dx1,
        f"merge_bwd{l}")
    dwba = tn_matmul(sv["oa"], dya, f"dw_ba{l}").reshape(HEADS, 128, D)[:, 0:64].reshape(512, D)
    grads["w_branch"] = jnp.concatenate([dwba, tn_matmul(sv["ob"], dyb, f"dw_bb{l}"), tn_matmul(oc, dyc, f"dw_bc{l}")])
    grads["w_out"] = tn_matmul(merged, dpre, f"dw_out{l}")
    small["ssd_norm_w"] = dmvec[1]
    small["dmod_b"] = (dmvec[0], dfvec[1], dfvec[2], dfvec[3])
    return dx, dict(doa=doa, dob=dob, dyt=dyt, dz=dz, dgl=dgl), grads, small


def _bwd_a(dx, cot, lw, sv, cos2, sin2, l, tok, small):
    s = dx.shape[0]
    grads = {}
    doa, dob, dz, dgl = cot["doa"], cot["dob"], cot["dz"], cot["dgl"]
    hv = lambda a: a.reshape(16, 1, 1)
    dxt, ddtr, ddtc, dbm, dcm, dal, ddb, ddk = ssd_bwd(
        sv["xt"], sv["dtr"], sv["dtc"], sv["xbc"], hv(lw["ssd_a_log"]) + tok[0, 0], hv(lw["ssd_dt_bias"]),
        hv(lw["ssd_d"]), sv["hs"], cot["dyt"].reshape(16, 64, s), f"ssd_bwd{l}")
    small["ssd_a_log"], small["ssd_dt_bias"], small["ssd_d"] = dal.reshape(16), ddb.reshape(16), ddk.reshape(16)
    dxbc, dscw, dscb = conv_bwd(sv["proj"], lw["ssd_conv_w"], lw["ssd_conv_b"].reshape(1, 1536), dxt.reshape(D, s),
                                dbm, dcm, f"conv_bwd{l}")
    small["ssd_conv_w"], small["ssd_conv_b"] = dscw, dscb.reshape(1536)
    ddt = (ddtr[:, 0, :] + ddtc[:, :, 0]).T
    du, dpw, dps = pool_bwd(sv["proj"], lw["pool_w"], lw["pool_scale"].reshape(1, 512), dob, f"pool_bwd{l}")
    small["pool_w"], small["pool_scale"] = dpw, dps.reshape(512)
    dq, dk, dv = mla_attn_bwd(sv["q"], sv["k"], sv["v"], doa, f"mla_attn_bwd{l}")
    dql, dckv, dkr, dkrs, dwq, dwkv, dmv = mla_pre_bwd(sv["proj"], lw["wq"], lw["wkv"], lw["mla_vec"], cos2, sin2,
                                                       dq, dk, dv, f"mla_pre_bwd{l}")
    grads["w_q_b"], grads["w_kv_b"] = _wq_unlayout(dwq), _wkv_unlayout(dwkv)
    small["q_a_norm"], small["kv_a_norm"], small["q_norm"], small["k_norm"] = _mla_unvec(dmv)
    dproj = jnp.concatenate([dgl, dxbc[:, 0:D], dz, du, dxbc[:, D:1536], dckv, dkr, dkrs,
                             _padc(ddt, 128).astype(BF16), jnp.zeros((s, 128), BF16), dql], axis=1)
    grads["w_in"] = tn_matmul(sv["h1"], dproj, f"dw_in{l}")
    dx0, dvec1 = norm_proj_bwd(sv["x"], sv["vec1"], dproj, lw["win"], dx, None, f"inproj_bwd{l}")
    small["norm1_w"] = dvec1[0]
    small["ada_b"] = jnp.concatenate([dvec1[1], dvec1[2], *small.pop("dmod_b")])
    return dx0, grads, small


def kernel(x, c, positions, ada_w, ada_b, norm1_w, w_in, q_a_norm, w_q_b, kv_a_norm, w_kv_b, q_norm, k_norm, pool_w, pool_scale, ssd_conv_w, ssd_conv_b, ssd_dt_bias, ssd_a_log, ssd_d, ssd_norm_w, w_branch, w_out, norm2_w, ffn_up, ffn_conv_w, ffn_conv_b, ffn_down, loss_target, m_ada_w, m_ada_b, m_norm1_w, m_w_in, m_q_a_norm, m_w_q_b, m_kv_a_norm, m_w_kv_b, m_q_norm, m_k_norm, m_pool_w, m_pool_scale, m_ssd_conv_w, m_ssd_conv_b, m_ssd_dt_bias, m_ssd_a_log, m_ssd_d, m_ssd_norm_w, m_w_branch, m_w_out, m_norm2_w, m_ffn_up, m_ffn_conv_w, m_ffn_conv_b, m_ffn_down, v_ada_w, v_ada_b, v_norm1_w, v_w_in, v_q_a_norm, v_w_q_b, v_kv_a_norm, v_w_kv_b, v_q_norm, v_k_norm, v_pool_w, v_pool_scale, v_ssd_conv_w, v_ssd_conv_b, v_ssd_dt_bias, v_ssd_a_log, v_ssd_d, v_ssd_norm_w, v_w_branch, v_w_out, v_norm2_w, v_ffn_up, v_ffn_conv_w, v_ffn_conv_b, v_ffn_down):
    p = dict(ada_w=ada_w, ada_b=ada_b, norm1_w=norm1_w, w_in=w_in, q_a_norm=q_a_norm, w_q_b=w_q_b, kv_a_norm=kv_a_norm,
             w_kv_b=w_kv_b, q_norm=q_norm, k_norm=k_norm, pool_w=pool_w, pool_scale=pool_scale, ssd_conv_w=ssd_conv_w,
             ssd_conv_b=ssd_conv_b, ssd_dt_bias=ssd_dt_bias, ssd_a_log=ssd_a_log, ssd_d=ssd_d, ssd_norm_w=ssd_norm_w,
             w_branch=w_branch, w_out=w_out, norm2_w=norm2_w, ffn_up=ffn_up, ffn_conv_w=ffn_conv_w, ffn_conv_b=ffn_conv_b,
             ffn_down=ffn_down)
    mom = dict(ada_w=m_ada_w, ada_b=m_ada_b, norm1_w=m_norm1_w, w_in=m_w_in, q_a_norm=m_q_a_norm, w_q_b=m_w_q_b,
               kv_a_norm=m_kv_a_norm, w_kv_b=m_w_kv_b, q_norm=m_q_norm, k_norm=m_k_norm, pool_w=m_pool_w,
               pool_scale=m_pool_scale, ssd_conv_w=m_ssd_conv_w, ssd_conv_b=m_ssd_conv_b, ssd_dt_bias=m_ssd_dt_bias,
               ssd_a_log=m_ssd_a_log, ssd_d=m_ssd_d, ssd_norm_w=m_ssd_norm_w, w_branch=m_w_branch, w_out=m_w_out,
               norm2_w=m_norm2_w, ffn_up=m_ffn_up, ffn_conv_w=m_ffn_conv_w, ffn_conv_b=m_ffn_conv_b, ffn_down=m_ffn_down)
    var = dict(ada_w=v_ada_w, ada_b=v_ada_b, norm1_w=v_norm1_w, w_in=v_w_in, q_a_norm=v_q_a_norm, w_q_b=v_w_q_b,
               kv_a_norm=v_kv_a_norm, w_kv_b=v_w_kv_b, q_norm=v_q_norm, k_norm=v_k_norm, pool_w=v_pool_w,
               pool_scale=v_pool_scale, ssd_conv_w=v_ssd_conv_w, ssd_conv_b=v_ssd_conv_b, ssd_dt_bias=v_ssd_dt_bias,
               ssd_a_log=v_ssd_a_log, ssd_d=v_ssd_d, ssd_norm_w=v_ssd_norm_w, w_branch=v_w_branch, w_out=v_w_out,
               norm2_w=v_norm2_w, ffn_up=v_ffn_up, ffn_conv_w=v_ffn_conv_w, ffn_conv_b=v_ffn_conv_b, ffn_down=v_ffn_down)
    names = list(p)
    me = 4 * lax.axis_index("x") + 2 * lax.axis_index("y") + lax.axis_index("c")
    xs, tgt = x[0], loss_target[0]
    s = xs.shape[0]

    inv_freq = ROPE_THETA ** (-jnp.arange(0, 32, 2, dtype=F32) / 32.0)
    ang = positions[0].astype(F32)[:, None] * inv_freq
    cos, sin = jnp.cos(ang), jnp.sin(ang)
    cos2 = _padc(jnp.concatenate([cos, cos], axis=1), 128)
    sin2 = _padc(jnp.concatenate([-sin, sin], axis=1), 128)

    conv_shards = jnp.concatenate([ssd_conv_w.reshape(-1), ffn_conv_w.reshape(-1)])
    (c_all, conv_all), _ = all_to_all([c, conv_shards], [True, True], "gather_c")
    modp, cact = ada_mod(jnp.pad(c_all.reshape(NDEV, D), ((0, 8), (0, 0))), ada_w)
    (mod_in,), tok = all_to_all([modp[:, 0:NDEV].transpose(1, 0, 2)], [False], "scatter_mod")
    mod = mod_in.transpose(1, 0, 2).reshape(LAYERS, 6 * D) + ada_b

    n1 = LAYERS * 4 * 192
    scw = conv_all[:, :n1].reshape(NDEV, LAYERS, 4, 192).transpose(1, 2, 0, 3).reshape(LAYERS, 4, 1536)
    fcw = conv_all[:, n1:].reshape(NDEV, LAYERS, 3, 704).transpose(1, 2, 0, 3).reshape(LAYERS, 3, 2 * FFN)

    def weights_a(gathered, l):
        full = {n: _gathered_full(g, n) for n, g in zip(GROUP_A[1:], gathered[1:])}
        lw = {n: p[n][l] for n in names}
        lw.update(win=_win_layout(gathered[0]), wq=_wq_layout(full["w_q_b"]), wkv=_wkv_layout(full["w_kv_b"]),
                  ssd_conv_w=scw[l], ffn_conv_w=fcw[l],
                  mla_vec=_mla_vec(lw["q_a_norm"], lw["kv_a_norm"], lw["q_norm"], lw["k_norm"]))
        return lw

    def weights_b(gathered):
        full = {n: _gathered_full(g, n) for n, g in zip(GROUP_B, gathered)}
        wb = full["w_branch"]
        return dict(wba=_wba_layout(wb[0:512]), wbb=wb[512:1024], wbc=wb[1024:2048], wout=full["w_out"],
                    wup=full["ffn_up"], wdn=full["ffn_down"])

    shards = lambda group, l: [p[n][l].astype(BF16) for n in group]
    lws, saved = [None] * LAYERS, [None] * LAYERS
    st, tok = gather_start(_behind(shards(GROUP_A, 0), tok), "gather_a0")
    got, tok = gather_finish(st, tok, "gather_a0")
    h = xs
    for l in range(LAYERS):
        st, tok = gather_start(_behind(shards(GROUP_B, l), tok), f"gather_b{l}")
        lws[l] = weights_a(got, l)
        saved[l] = _fwd_a(h, lws[l], mod[l], cos2, sin2, l, tok)
        got, tok = gather_finish(st, saved[l]["yt"], f"gather_b{l}")
        lws[l].update(weights_b(got))
        if l + 1 < LAYERS:
            st, tok = gather_start(_behind(shards(GROUP_A, l + 1), tok), f"gather_a{l + 1}")
        h = _fwd_b(saved[l], lws[l], l, tok)
        if l + 1 < LAYERS:
            got, tok = gather_finish(st, h, f"gather_a{l + 1}")
    dx, lpart = loss_head(h, tgt)
    loss = lax.psum(lpart[0, 0], ("x", "y", "c"))
    tok = tok + loss * 0.0

    grads, small, parts, packs = [None] * LAYERS, [None] * LAYERS, {}, [None] * LAYERS
    to_shards = lambda g, group: [_to_shards(g[n], n) for n in group]
    nb = lambda group: [False] * len(group)
    st = None
    for l in reversed(range(LAYERS)):
        dx, cot, gb, small[l] = _bwd_b(dx, lws[l], saved[l], l, tok)
        if st is not None:
            parts[("a", l + 1)], tok, _ = exchange_wait(st, dx, f"scatter_a{l + 1}_wait")
        arrs, flags = to_shards(gb, GROUP_B), nb(GROUP_B)
        if l + 1 < LAYERS:
            arrs, flags = arrs + [_pack(small[l + 1])], flags + [True]
        st, tok = exchange_start(_behind(arrs, tok), flags, f"scatter_b{l}_start")
        dx, ga, small[l] = _bwd_a(dx, cot, lws[l], saved[l], cos2, sin2, l, tok, small[l])
        got, tok, _ = exchange_wait(st, dx, f"scatter_b{l}_wait")
        parts[("b", l)] = got[0:len(GROUP_B)]
        if l + 1 < LAYERS:
            packs[l + 1] = got[len(GROUP_B)]
        arrs, flags = to_shards(ga, GROUP_A), nb(GROUP_A)
        if l == 0:
            dmod = jnp.stack([small[q]["ada_b"] for q in range(LAYERS)])
            arrs += [_pack(small[0]), dmod.reshape(LAYERS, NDEV, 768).transpose(1, 0, 2)]
            flags += [True, False]
        st, tok = exchange_start(_behind(arrs, tok), flags, f"scatter_a{l}_start")

    out = {}

    def big_adamw(group, tok):
        res = None
        for n in group:
            grp, idx = ("a", GROUP_A.index(n)) if n in GROUP_A else ("b", GROUP_B.index(n))
            shp = p[n].shape
            flat = lambda a: a.reshape(shp[0] * shp[1], shp[2])
            res = adamw([parts[(grp, 0)][idx], parts[(grp, 1)][idx]], flat(p[n]), flat(mom[n]), flat(var[n]),
                        f"adamw_{n}", tok)
            out[n] = [r.reshape(shp) for r in res]
        return res[0]

    g_last = big_adamw(GROUP_B, tok)
    got, _, _ = exchange_wait(st, g_last, "scatter_a0_wait")
    parts[("a", 0)], packs[0], dmod_in = got[0:3], got[3], got[4]
    big_adamw(GROUP_A, None)

    dmod16 = jnp.pad(dmod_in, ((0, 8), (0, 0), (0, 0)))
    g_ada = jnp.stack([tn_matmul(cact, dmod16[:, l], f"dw_ada{l}", out_dtype=F32) for l in range(LAYERS)])
    flat = lambda a: a.reshape(LAYERS * D, 768)
    out["ada_w"] = [r.reshape(ada_w.shape) for r in
                    adamw([flat(g_ada)[None]], flat(ada_w), flat(m_ada_w), flat(v_ada_w), "adamw_ada_w")]

    for n, pt in _unpack_parts(packs).items():
        if n in SHARDED_SMALL:
            w = SHARDED_SMALL[n]
            pt = lax.dynamic_slice_in_dim(pt, me * w, w, axis=2)
        r, c = pt.shape[1:]
        res = adamw([pt], p[n].reshape(r, c), mom[n].reshape(r, c), var[n].reshape(r, c), f"adamw_{n}")
        out[n] = [a.reshape(p[n].shape) for a in res]

    outs = [loss, dx[None]]
    for q in range(4):
        outs += [out[n][q] for n in names]
    return tuple(outs)
```

```python
import functools
import math

import jax
import jax.numpy as jnp
from jax import lax
from jax.experimental import pallas as pl
from jax.experimental.pallas import tpu as pltpu

F32, BF16 = jnp.float32, jnp.bfloat16
EPS = 1e-6
D = 1024
NDEV = 8
LAYERS = 2
HEADS = 8
FFN = 2816
FFN_TILE = 1408
FFN_NT = FFN // FFN_TILE
ATT_SCALE = 96 ** -0.5
ROPE_THETA = 10000.0
LR, B1, B2, ADAM_EPS, WD, STEP = 0.001, 0.9, 0.999, 1e-08, 0.01, 10

O_G, O_XS, O_Z, O_PU, O_BC, O_CKV, O_KR, O_KRS, O_DT, O_QL = 0, 3072, 4096, 5120, 5632, 6144, 6400, 6528, 6656, 6912
NPROJ = 7296
CONST = dict(pipeline_mode=pl.Buffered(1))


def _pick(n, cap, mult=128):
    if n <= cap:
        return n
    best = None
    for t in range(mult, cap + 1, mult):
        if n % t == 0:
            best = t
    assert best is not None, (n, cap, mult)
    return best


def _sig(x):
    return 1.0 / (1.0 + jnp.exp(-x))


def _rms(x, w, n):
    return x * lax.rsqrt(jnp.sum(x * x, axis=-1, keepdims=True) / n + EPS) * w


def _raw(a, b, dims):
    return lax.dot_general(a.astype(BF16), b.astype(BF16), dims, preferred_element_type=F32)


_NN = (((1,), (0,)), ((), ()))
_NT = (((1,), (1,)), ((), ()))
_TN = (((0,), (0,)), ((), ()))
_BNN = (((2,), (1,)), ((0,), (0,)))
_BNT = (((2,), (2,)), ((0,), (0,)))
_BTN = (((1,), (1,)), ((0,), (0,)))


@jax.custom_vjp
def mm_nn(a, b):
    return _raw(a, b, _NN)


mm_nn.defvjp(lambda a, b: (_raw(a, b, _NN), (a, b)),
             lambda r, g: (_raw(g, r[1], _NT), _raw(r[0], g, _TN)))


@jax.custom_vjp
def mm_nc(a, b):
    return _raw(a, b, _NN)


mm_nc.defvjp(lambda a, b: (_raw(a, b, _NN), b),
             lambda b, g: (_raw(g, b, _NT), jnp.zeros_like(b)))


@jax.custom_vjp
def mm_nt(a, b):
    return _raw(a, b, _NT)


mm_nt.defvjp(lambda a, b: (_raw(a, b, _NT), (a, b)),
             lambda r, g: (_raw(g, r[1], _NN), _raw(g, r[0], _TN)))


@jax.custom_vjp
def bmm_nn(a, b):
    return _raw(a, b, _BNN)


bmm_nn.defvjp(lambda a, b: (_raw(a, b, _BNN), (a, b)),
              lambda r, g: (_raw(g, r[1], _BNT), _raw(r[0], g, _BTN)))


@jax.custom_vjp
def bmm_nt(a, b):
    return _raw(a, b, _BNT)


bmm_nt.defvjp(lambda a, b: (_raw(a, b, _BNT), (a, b)),
              lambda r, g: (_raw(g, r[1], _BNN), _raw(g, r[0], _BTN)))


@jax.custom_vjp
def softplus(x):
    t = jnp.exp(-jnp.abs(x))
    u = 1.0 + t
    one = u == 1.0
    l1p = jnp.where(one, t, jnp.log(u) * (t / jnp.where(one, 1.0, u - 1.0)))
    return jnp.maximum(x, 0.0) + l1p


softplus.defvjp(lambda x: (softplus(x), x), lambda x, g: (g * _sig(x),))


def _params(*sem):
    return pltpu.CompilerParams(dimension_semantics=sem, vmem_limit_bytes=56 * 1024 * 1024)


def all_to_all(arrs, bcast, name):
    n = len(arrs)
    out_shapes = [jax.ShapeDtypeStruct(((NDEV,) + a.shape) if b else a.shape, a.dtype) for a, b in zip(arrs, bcast)]

    def body(*refs):
        ins, outs, token = refs[:n], refs[n:2 * n], refs[2 * n]
        send_sems, recv_sems, local_sems = refs[2 * n + 1:]
        me, remote = _exchange_copies(ins, outs, bcast, send_sems, recv_sems)
        local = [pltpu.make_async_copy(ins[j] if bcast[j] else ins[j].at[me], outs[j].at[me], local_sems.at[j])
                 for j in range(n)]
        for cp in local + remote:
            cp.start()
        for cp in remote + local:
            cp.wait()
        token[...] = jnp.zeros_like(token)

    any_spec = pl.BlockSpec(memory_space=pl.ANY)
    res = pl.pallas_call(
        body, name=name, out_shape=out_shapes + [jax.ShapeDtypeStruct((8, 128), F32)], in_specs=[any_spec] * n,
        out_specs=[any_spec] * n + [pl.BlockSpec(memory_space=pltpu.VMEM)],
        scratch_shapes=[pltpu.SemaphoreType.DMA((7 * n,)), pltpu.SemaphoreType.DMA((7 * n,)),
                        pltpu.SemaphoreType.DMA((n,))],
        compiler_params=pltpu.CompilerParams(has_side_effects=True),
    )(*arrs)
    return res[:n], res[n]


def _peers():
    x, y, c = lax.axis_index("x"), lax.axis_index("y"), lax.axis_index("c")
    out = []
    for k in range(1, NDEV):
        px, py, pc = x ^ ((k >> 2) & 1), y ^ ((k >> 1) & 1), c ^ (k & 1)
        out.append(((px, py, pc), 4 * px + 2 * py + pc))
    return 4 * x + 2 * y + c, out


COPIES = {"all": 7, "chips": 3, "pass": 4}


def _exchange_copies(ins, lands, bcast, send_sems, recv_sems, mode="all"):
    x, y, c = lax.axis_index("x"), lax.axis_index("y"), lax.axis_index("c")
    me = 4 * x + 2 * y + c
    n, copies = len(ins), []

    def add(q, j, src, dst, dev):
        copies.append(pltpu.make_async_remote_copy(
            src_ref=src, dst_ref=dst, send_sem=send_sems.at[q * n + j], recv_sem=recv_sems.at[q * n + j],
            device_id=dev, device_id_type=pl.DeviceIdType.MESH))

    if mode == "pass":
        for q in range(4):
            slot = 4 * (x ^ (q >> 1)) + 2 * (y ^ (q & 1)) + c
            for j in range(n):
                add(q, j, ins[j] if q == 0 else lands[j].at[slot], lands[j].at[slot], (x, y, 1 - c))
        return me, copies
    for q, k in enumerate(range(1, NDEV) if mode == "all" else (2, 4, 6)):
        px, py, pc = x ^ ((k >> 2) & 1), y ^ ((k >> 1) & 1), c ^ (k & 1)
        for j in range(n):
            add(q, j, ins[j] if bcast[j] else ins[j].at[4 * px + 2 * py + pc], lands[j].at[me], (px, py, pc))
    return me, copies


_HBM = pl.BlockSpec(memory_space=pltpu.HBM)
_SEM = pl.BlockSpec(memory_space=pltpu.SEMAPHORE)
_EFFECT = pltpu.SideEffectType.DATAFLOW_SIDE_EFFECTING


def exchange_start(arrs, bcast, name, mode="all", lands=None):
    n, ncp = len(arrs), COPIES[mode] * len(arrs)
    land_shapes = [((NDEV,) + a.shape) if b else a.shape for a, b in zip(arrs, bcast)]
    if lands is None:
        lands = [lax.empty(s_, a.dtype) for s_, a in zip(land_shapes, arrs)]

    def body(*refs):
        in_refs, land_refs = refs[:n], refs[n:2 * n]
        send_sems, recv_sems = refs[2 * n], refs[2 * n + 1]
        token = refs[-1]
        _, copies = _exchange_copies(in_refs, land_refs, bcast, send_sems, recv_sems, mode)
        for cp in copies:
            cp.start()
        token[...] = jnp.zeros_like(token)

    hbm = lambda shp, a: pltpu.HBM(shp, a.dtype)
    res = pl.pallas_call(
        body, name=name,
        out_shape=[pltpu.SemaphoreType.DMA((ncp,)), pltpu.SemaphoreType.DMA((ncp,))]
                  + [hbm(a.shape, a) for a in arrs] + [hbm(s_, a) for s_, a in zip(land_shapes, arrs)]
                  + [jax.ShapeDtypeStruct((8, 128), F32)],
        in_specs=[_HBM] * (2 * n), out_specs=[_SEM, _SEM] + [_HBM] * (2 * n) + [pl.BlockSpec(memory_space=pltpu.VMEM)],
        input_output_aliases={i: 2 + i for i in range(2 * n)},
        compiler_params=pltpu.CompilerParams(has_side_effects=_EFFECT),
    )(*[pltpu.with_memory_space_constraint(a, pltpu.HBM) for a in arrs],
      *[pltpu.with_memory_space_constraint(a, pltpu.HBM) for a in lands])
    return (res[0], res[1], res[2:2 + n], res[2 + n:2 + 2 * n], tuple(bcast), mode), res[-1]


def exchange_wait(state, after, name):
    send_sems, recv_sems, ins, lands, bcast, mode = state
    n = len(ins)

    def body(*refs):
        in_refs, land_refs = refs[:n], refs[n:2 * n]
        s_sems, r_sems = refs[2 * n], refs[2 * n + 1]
        token = refs[-1]
        _, copies = _exchange_copies(in_refs, land_refs, bcast, s_sems, r_sems, mode)
        for cp in copies:
            cp.wait_send()
            cp.wait_recv()
        token[...] = jnp.zeros_like(token)

    res = pl.pallas_call(
        body, name=name,
        out_shape=[pltpu.HBM(a.shape, a.dtype) for a in ins] + [pltpu.HBM(a.shape, a.dtype) for a in lands]
                  + [jax.ShapeDtypeStruct((8, 128), F32)],
        in_specs=[_HBM] * (2 * n) + [_SEM, _SEM, pl.BlockSpec(memory_space=pl.ANY)],
        out_specs=[_HBM] * (2 * n) + [pl.BlockSpec(memory_space=pltpu.VMEM)],
        input_output_aliases={i: i for i in range(2 * n)},
        compiler_params=pltpu.CompilerParams(has_side_effects=_EFFECT),
    )(*ins, *lands, send_sems, recv_sems, after)
    if mode == "chips":
        return list(res[n:2 * n]), res[-1], list(res[:n])
    me = 4 * lax.axis_index("x") + 2 * lax.axis_index("y") + lax.axis_index("c")
    got = []
    for j in range(n):
        own = res[j][None] if bcast[j] else lax.dynamic_index_in_dim(res[j], me, 0, keepdims=True)
        got.append(lax.dynamic_update_slice_in_dim(res[n + j], own, me, axis=0))
    return got, res[-1], list(res[:n])


def gather_start(shards, name):
    return exchange_start(shards, [True] * len(shards), name + "_chips_start", mode="chips")


def gather_finish(state, after, name):
    lands, _, sent = exchange_wait(state, after, name + "_chips_wait")
    state, tok = exchange_start(sent, [True] * len(sent), name + "_pass_start", mode="pass", lands=lands)
    got, tok, _ = exchange_wait(state, tok, name + "_pass_wait")
    return got, tok


def norm_proj_fwd(x, vec, w, name):
    s, n = x.shape[0], w.shape[1]
    tr, tn = _pick(s, 512), _pick(n, 2560)
    ni, jdt, odt = s // tr, O_DT // tn, O_DT % tn

    def body(x_ref, v_ref, w_ref, o_ref, h_ref, dt_ref, h_scr):
        j, i = pl.program_id(0), pl.program_id(1)
        rows = pl.ds(pl.multiple_of(i * tr, tr), tr)

        @pl.when(j == 0)
        def _():
            h = _rms(x_ref[...], v_ref[0:1, :], D) * (1.0 + v_ref[2:3, :]) + v_ref[1:2, :]
            h_scr[rows, :] = h.astype(BF16)
            h_ref[...] = h.astype(BF16)
        res = jnp.dot(h_scr[rows, :], w_ref[...], preferred_element_type=F32)
        o_ref[...] = res

        @pl.when(j == jdt)
        def _():
            dt_ref[...] = res[:, odt:odt + 128]

    first = lambda j, i: (jnp.where(j == 0, i, ni - 1), 0)
    dtix = lambda j, i: (jnp.where(j < jdt, 0, jnp.where(j == jdt, i, ni - 1)), 0)
    return pl.pallas_call(
        body, name=name, grid=(n // tn, ni),
        in_specs=[pl.BlockSpec((tr, D), first), pl.BlockSpec((8, D), lambda j, i: (0, 0)),
                  pl.BlockSpec((D, tn), lambda j, i: (0, j))],
        out_specs=[pl.BlockSpec((tr, tn), lambda j, i: (i, j)), pl.BlockSpec((tr, D), first),
                   pl.BlockSpec((tr, 128), dtix)],
        out_shape=[jax.ShapeDtypeStruct((s, n), F32), jax.ShapeDtypeStruct((s, D), BF16),
                   jax.ShapeDtypeStruct((s, 128), F32)],
        scratch_shapes=[pltpu.VMEM((s, D), BF16)],
        compiler_params=_params("arbitrary", "arbitrary"),
    )(x, vec, w)


def _col_tiles(arr, cap):
    if arr.ndim == 2:
        n = arr.shape[1]
        t = _pick(n, cap)
        return n, t, lambda rows, ix: pl.BlockSpec((rows, t), lambda *g: ix(*g))
    width = arr.shape[2]
    t = _pick(width, cap)
    per = width // t

    def spec(rows, ix):
        def index(*g):
            r, j = ix(*g)
            return (j // per, r, j % per)
        return pl.BlockSpec((None, rows, t), index)
    return arr.shape[0] * width, t, spec


def norm_proj_bwd(x, vec, dp, w, dx_in, aux, name):
    s = x.shape[0]
    tr = _pick(s, 512)
    n, tk, dp_spec = _col_tiles(dp, 2560)
    nk, has_aux = n // tk, aux is not None

    def body(*refs):
        if has_aux:
            x_ref, v_ref, dp_ref, w_ref, dxin_ref, aux_ref, dx_ref, dv_ref, acc = refs
        else:
            x_ref, v_ref, dp_ref, w_ref, dxin_ref, dx_ref, dv_ref, acc = refs
        k, i = pl.program_id(0), pl.program_id(1)
        rows = pl.ds(pl.multiple_of(i * tr, tr), tr)
        part = _raw(dp_ref[...], w_ref[...], _NT)

        @pl.when(k == 0)
        def _():
            acc[rows, :] = part

        @pl.when(k > 0)
        def _():
            acc[rows, :] += part

        @pl.when(k == nk - 1)
        def _():
            f = lambda xx, nw, sh, sc: _rms(xx, nw, D) * (1.0 + sc) + sh
            _, vjp = jax.vjp(f, x_ref[...], v_ref[0:1, :], v_ref[1:2, :], v_ref[2:3, :])
            dx, dnw, dsh, dsc = vjp(acc[rows, :])
            dx_ref[...] = dxin_ref[...] + dx

            @pl.when(i == 0)
            def _():
                dv_ref[...] = jnp.zeros_like(dv_ref)

            dv_ref[0:1, :] += dnw
            dv_ref[1:2, :] += dsh
            dv_ref[2:3, :] += dsc
            if has_aux:
                dv_ref[3:4, :] += jnp.sum(dxin_ref[...] * aux_ref[...], axis=0, keepdims=True)

    row = pl.BlockSpec((tr, D), lambda k, i: (jnp.where(k == nk - 1, i, 0), 0))
    in_specs = [row, pl.BlockSpec((8, D), lambda k, i: (0, 0)), dp_spec(tr, lambda k, i: (i, k)),
                pl.BlockSpec((D, tk), lambda k, i: (0, k)), row] + ([row] if has_aux else [])
    args = [x, vec, dp, w, dx_in] + ([aux] if has_aux else [])
    return pl.pallas_call(
        body, name=name, grid=(nk, s // tr), in_specs=in_specs,
        out_specs=[row, pl.BlockSpec((8, D), lambda k, i: (0, 0))],
        out_shape=[jax.ShapeDtypeStruct((s, D), F32), jax.ShapeDtypeStruct((8, D), F32)],
        scratch_shapes=[pltpu.VMEM((s, D), F32)],
        compiler_params=_params("arbitrary", "arbitrary"),
    )(*args)


def tn_matmul(a, b, name, scale=None, out_dtype=None):
    out_dtype = BF16 if out_dtype is None else out_dtype
    s, m = a.shape
    ts, tm = _pick(s, 512, 16), _pick(m, 1408)
    n, tn, b_spec = _col_tiles(b, 2560)
    ns, has_scale = s // ts, scale is not None

    def body(*refs):
        if has_scale:
            a_ref, b_ref, sc_ref, o_ref, acc = refs
        else:
            a_ref, b_ref, o_ref, acc = refs
        k = pl.program_id(2)

        @pl.when(k == 0)
        def _():
            acc[...] = jnp.zeros_like(acc)

        acc[...] += _raw(a_ref[...], b_ref[...], _TN)

        @pl.when(k == ns - 1)
        def _():
            o_ref[...] = (acc[...] * sc_ref[...] if has_scale else acc[...]).astype(out_dtype)

    in_specs = [pl.BlockSpec((ts, tm), lambda i, j, k: (k, i)), b_spec(ts, lambda i, j, k: (k, j))]
    if has_scale:
        in_specs.append(pl.BlockSpec((1, tn), lambda i, j, k: (0, j)))
    return pl.pallas_call(
        body, name=name, grid=(m // tm, n // tn, ns), in_specs=in_specs,
        out_specs=pl.BlockSpec((tm, tn), lambda i, j, k: (i, j)),
        out_shape=jax.ShapeDtypeStruct((m, n), out_dtype),
        scratch_shapes=[pltpu.VMEM((tm, tn), F32)],
        compiler_params=_params("arbitrary", "arbitrary", "arbitrary"),
    )(*([a, b] + ([scale] if has_scale else [])))


def ada_mod(c16, w):
    ncol = w.shape[2]

    def body(c_ref, w_ref, o_ref, a_ref):
        cc = c_ref[...]
        act = cc * _sig(cc)
        a_ref[...] = act
        o_ref[...] = _raw(act, w_ref[...], _NN)

    return pl.pallas_call(
        body, name="ada_mod", grid=(LAYERS,),
        in_specs=[pl.BlockSpec((16, D), lambda l: (0, 0)), pl.BlockSpec((None, D, ncol), lambda l: (l, 0, 0))],
        out_specs=[pl.BlockSpec((None, 16, ncol), lambda l: (l, 0, 0)), pl.BlockSpec((16, D), lambda l: (0, 0))],
        out_shape=[jax.ShapeDtypeStruct((LAYERS, 16, ncol), F32), jax.ShapeDtypeStruct((16, D), F32)],
        compiler_params=_params("arbitrary"),
    )(c16, w)


def _mla_shared(q_lat, c_kv, kr, krs, qa_w, kva_w, kr_w, krs_w, cos2, sin2):
    qn = _rms(q_lat, qa_w, 384.0)
    kvn = _rms(c_kv, kva_w, 256.0)
    rk = lax.rsqrt(jnp.sum(kr * kr, axis=-1, keepdims=True) / 32.0 + EPS)
    krope = rk * (kr * kr_w * cos2 + krs * krs_w * sin2)
    return qn, kvn, krope


def _mla_head(qn, kvn, wqn, wqr, wqrs, wkn, wv, qn_w, qr_w, qrs_w, kn_w, cos2, sin2):
    qnope = _rms(mm_nn(qn, wqn), qn_w, 64.0)
    qr, qrs = mm_nn(qn, wqr), mm_nn(qn, wqrs)
    rq = lax.rsqrt(jnp.sum(qr * qr, axis=-1, keepdims=True) / 32.0 + EPS)
    qrope = rq * (qr * qr_w * cos2 + qrs * qrs_w * sin2)
    knope = _rms(mm_nn(kvn, wkn), kn_w, 64.0)
    return qnope, qrope, knope, mm_nn(kvn, wv)


def _mla_vec_pieces(v_ref):
    return ((v_ref[0:1, 0:384], v_ref[1:2, 0:256], v_ref[3:4, 128:256], v_ref[3:4, 256:384]),
            (v_ref[2:3, 0:128], v_ref[2:3, 128:256], v_ref[2:3, 256:384], v_ref[3:4, 0:128]))


def _mla_in_specs(tr):
    return [pl.BlockSpec((tr, 384), lambda i: (i, O_QL // 384)), pl.BlockSpec((tr, 256), lambda i: (i, O_CKV // 256)),
            pl.BlockSpec((tr, 128), lambda i: (i, O_KR // 128)), pl.BlockSpec((tr, 128), lambda i: (i, O_KRS // 128)),
            pl.BlockSpec((HEADS, 384, 384), lambda i: (0, 0, 0), **CONST),
            pl.BlockSpec((HEADS, 256, 256), lambda i: (0, 0, 0), **CONST),
            pl.BlockSpec((8, 512), lambda i: (0, 0)),
            pl.BlockSpec((tr, 128), lambda i: (i, 0)), pl.BlockSpec((tr, 128), lambda i: (i, 0))]


def mla_pre_fwd(proj, wq, wkv, vec, cos2, sin2, name):
    s = proj.shape[0]
    tr = _pick(s, 256)

    def body(ql_ref, ckv_ref, kr_ref, krs_ref, wq_ref, wkv_ref, v_ref, cos_ref, sin_ref, q_out, k_out, v_out):
        vshared, vhead = _mla_vec_pieces(v_ref)
        cos2_, sin2_ = cos_ref[...], sin_ref[...]
        qlat_n, kv_n, krope = _mla_shared(ql_ref[...], ckv_ref[...], kr_ref[...], krs_ref[...], *vshared, cos2_, sin2_)
        qlat_n, kv_n, krope = qlat_n.astype(BF16), kv_n.astype(BF16), krope.astype(BF16)
        for h in range(HEADS):
            ws = (wq_ref[h, :, 0:128], wq_ref[h, :, 128:256], wq_ref[h, :, 256:384],
                  wkv_ref[h, :, 0:128], wkv_ref[h, :, 128:256])
            qn, qr, kn, v = _mla_head(qlat_n, kv_n, *ws, *vhead, cos2_, sin2_)
            q_out[h, :, 0:128] = qn.astype(BF16)
            q_out[h, :, 128:256] = qr.astype(BF16)
            k_out[h, :, 0:128] = kn.astype(BF16)
            k_out[h, :, 128:256] = krope
            v_out[h] = v.astype(BF16)

    return pl.pallas_call(
        body, name=name, grid=(s // tr,), in_specs=_mla_in_specs(tr),
        out_specs=[pl.BlockSpec((HEADS, tr, 256), lambda i: (0, i, 0)), pl.BlockSpec((HEADS, tr, 256), lambda i: (0, i, 0)),
                   pl.BlockSpec((HEADS, tr, 128), lambda i: (0, i, 0))],
        out_shape=[jax.ShapeDtypeStruct((HEADS, s, 256), BF16), jax.ShapeDtypeStruct((HEADS, s, 256), BF16),
                   jax.ShapeDtypeStruct((HEADS, s, 128), BF16)],
        compiler_params=_params("arbitrary"),
    )(proj, proj, proj, proj, wq, wkv, vec, cos2, sin2)


def mla_pre_bwd(proj, wq, wkv, vec, cos2, sin2, dq, dk, dv, name):
    s = proj.shape[0]
    tr = _pick(s, 256)

    def body(ql_ref, ckv_ref, kr_ref, krs_ref, wq_ref, wkv_ref, v_ref, cos_ref, sin_ref, dq_ref, dk_ref, dv_ref,
             dql_out, dckv_out, dkr_out, dkrs_out, dwq_out, dwkv_out, dvec_out):
        @pl.when(pl.program_id(0) == 0)
        def _():
            dwq_out[...] = jnp.zeros_like(dwq_out)
            dwkv_out[...] = jnp.zeros_like(dwkv_out)
            dvec_out[...] = jnp.zeros_like(dvec_out)

        vshared, vhead = _mla_vec_pieces(v_ref)
        cos2_, sin2_ = cos_ref[...], sin_ref[...]
        fs = lambda *a: _mla_shared(*a, cos2_, sin2_)
        (qlat_n, kv_n, _), vjp_shared = jax.vjp(fs, ql_ref[...], ckv_ref[...], kr_ref[...], krs_ref[...], *vshared)

        def head(h, carry):
            wq_h, wkv_h = wq_ref[h].astype(F32), wkv_ref[h].astype(F32)
            ws = (wq_h[:, 0:128], wq_h[:, 128:256], wq_h[:, 256:384], wkv_h[:, 0:128], wkv_h[:, 128:256])
            f = lambda *a: _mla_head(*a, cos2_, sin2_)
            _, vjp = jax.vjp(f, qlat_n, kv_n, *ws, *vhead)
            dq_h, dk_h = dq_ref[h], dk_ref[h]
            g = vjp((dq_h[:, 0:128], dq_h[:, 128:256], dk_h[:, 0:128], dv_ref[h]))
            dwq_out[h, :, 0:128] += g[2]
            dwq_out[h, :, 128:256] += g[3]
            dwq_out[h, :, 256:384] += g[4]
            dwkv_out[h, :, 0:128] += g[5]
            dwkv_out[h, :, 128:256] += g[6]
            dvec_out[2:3, 0:128] += g[7]
            dvec_out[2:3, 128:256] += g[8]
            dvec_out[2:3, 256:384] += g[9]
            dvec_out[3:4, 0:128] += g[10]
            return carry[0] + g[0], carry[1] + g[1], carry[2] + dk_h[:, 128:256]

        zero = lambda w: jnp.zeros((tr, w), F32)
        dqn, dkvn, dkrope = lax.fori_loop(0, HEADS, head, (zero(384), zero(256), zero(128)))
        g = vjp_shared((dqn, dkvn, dkrope))
        dql_out[...] = g[0].astype(BF16)
        dckv_out[...] = g[1].astype(BF16)
        dkr_out[...] = g[2].astype(BF16)
        dkrs_out[...] = g[3].astype(BF16)
        dvec_out[0:1, 0:384] += g[4]
        dvec_out[1:2, 0:256] += g[5]
        dvec_out[3:4, 128:256] += g[6]
        dvec_out[3:4, 256:384] += g[7]

    hb = lambda w: pl.BlockSpec((HEADS, tr, w), lambda i: (0, i, 0))
    return pl.pallas_call(
        body, name=name, grid=(s // tr,), in_specs=_mla_in_specs(tr) + [hb(256), hb(256), hb(128)],
        out_specs=[pl.BlockSpec((tr, 384), lambda i: (i, 0)), pl.BlockSpec((tr, 256), lambda i: (i, 0)),
                   pl.BlockSpec((tr, 128), lambda i: (i, 0)), pl.BlockSpec((tr, 128), lambda i: (i, 0)),
                   pl.BlockSpec((HEADS, 384, 384), lambda i: (0, 0, 0)), pl.BlockSpec((HEADS, 256, 256), lambda i: (0, 0, 0)),
                   pl.BlockSpec((8, 512), lambda i: (0, 0))],
        out_shape=[jax.ShapeDtypeStruct((s, 384), BF16), jax.ShapeDtypeStruct((s, 256), BF16),
                   jax.ShapeDtypeStruct((s, 128), BF16), jax.ShapeDtypeStruct((s, 128), BF16),
                   jax.ShapeDtypeStruct((HEADS, 384, 384), F32), jax.ShapeDtypeStruct((HEADS, 256, 256), F32),
                   jax.ShapeDtypeStruct((8, 512), F32)],
        compiler_params=_params("arbitrary"),
    )(proj, proj, proj, proj, wq, wkv, vec, cos2, sin2, dq, dk, dv)


def _att_probs(q, kk, i, tq):
    sc = _raw(q, kk, _NT) * ATT_SCALE
    rows = lax.broadcasted_iota(jnp.int32, sc.shape, 0) + i * tq
    cols = lax.broadcasted_iota(jnp.int32, sc.shape, 1)
    sc = jnp.where(cols <= rows, sc, -jnp.inf)
    e = jnp.exp(sc - jnp.max(sc, axis=-1, keepdims=True))
    return e / jnp.sum(e, axis=-1, keepdims=True)


def mla_attn_fwd(q, k, v, name):
    s = q.shape[1]
    tq = _pick(s, 256)

    def body(q_ref, k_ref, v_ref, o_ref):
        for i in range(s // tq):
            n = (i + 1) * tq
            p = _att_probs(q_ref[i * tq:n, :], k_ref[0:n, :], i, tq)
            o_ref[i * tq:n, :] = _raw(p, v_ref[0:n, :], _NN)

    hs = lambda w: pl.BlockSpec((None, s, w), lambda h: (h, 0, 0))
    return pl.pallas_call(
        body, name=name, grid=(HEADS,), in_specs=[hs(256), hs(256), hs(128)],
        out_specs=pl.BlockSpec((s, 128), lambda h: (0, h)),
        out_shape=jax.ShapeDtypeStruct((s, HEADS * 128), F32),
        compiler_params=_params("arbitrary"),
    )(q, k, v)


def mla_attn_bwd(q, k, v, do, name):
    s = q.shape[1]
    tq = _pick(s, 256)

    def body(q_ref, k_ref, v_ref, do_ref, dq_ref, dk_ref, dv_ref):
        dk_ref[...] = jnp.zeros_like(dk_ref)
        dv_ref[...] = jnp.zeros_like(dv_ref)
        for i in range(s // tq):
            n = (i + 1) * tq
            qq, kk, vv = q_ref[i * tq:n, :], k_ref[0:n, :], v_ref[0:n, :]
            p = _att_probs(qq, kk, i, tq)
            o = _raw(p, vv, _NN)
            dout = do_ref[i * tq:n, :]
            delta = jnp.sum(dout * o, axis=-1, keepdims=True)
            dp = _raw(dout, vv, _NT)
            ds = p * (dp - delta) * ATT_SCALE
            dq_ref[i * tq:n, :] = _raw(ds, kk, _NN)
            dk_ref[0:n, :] += _raw(ds, qq, _TN)
            dv_ref[0:n, :] += _raw(p, dout, _TN)

    hs = lambda w: pl.BlockSpec((None, s, w), lambda h: (h, 0, 0))
    return pl.pallas_call(
        body, name=name, grid=(HEADS,),
        in_specs=[hs(256), hs(256), hs(128), pl.BlockSpec((s, 128), lambda h: (0, h))],
        out_specs=[hs(256), hs(256), hs(128)],
        out_shape=[jax.ShapeDtypeStruct((HEADS, s, 256), F32), jax.ShapeDtypeStruct((HEADS, s, 256), F32),
                   jax.ShapeDtypeStruct((HEADS, s, 128), F32)],
        compiler_params=_params("arbitrary"),
    )(q, k, v, do)


def _pool_windows(u, pad, s, g):
    pad[0:16, :] = jnp.zeros((16, 128), F32)
    cur, sel = u, None
    for j, k in enumerate((1, 2, 4, 8)):
        pad[16:16 + s, :] = cur
        cur = cur + pad[16 - k:16 - k + s, :]
        sel = cur if sel is None else jnp.where(g == j, cur, sel)
    return sel


def _pool_count(s, g):
    t = lax.broadcasted_iota(jnp.int32, (s, 1), 0)
    return jnp.minimum(t + 1, 2 << g).astype(F32)


def pool_fwd(proj, pw, ps, name):
    s = proj.shape[0]

    def body(u_ref, w_ref, s_ref, o_ref, pad):
        g = pl.program_id(0)
        u = u_ref[...]
        pooled = _pool_windows(u, pad, s, g) / _pool_count(s, g) - u
        o_ref[...] = _raw(pooled, w_ref[...], _NN) * s_ref[...]

    return pl.pallas_call(
        body, name=name, grid=(4,),
        in_specs=[pl.BlockSpec((s, 128), lambda g: (0, O_PU // 128 + g)), pl.BlockSpec((None, 128, 128), lambda g: (g, 0, 0)),
                  pl.BlockSpec((1, 128), lambda g: (0, g))],
        out_specs=pl.BlockSpec((s, 128), lambda g: (0, g)),
        out_shape=jax.ShapeDtypeStruct((s, 512), F32),
        scratch_shapes=[pltpu.VMEM((s + 16, 128), F32)],
        compiler_params=_params("arbitrary"),
    )(proj, pw, ps)


def pool_bwd(proj, pw, ps, do, name):
    s = proj.shape[0]

    def body(u_ref, w_ref, s_ref, do_ref, du_ref, dw_ref, ds_ref, pad):
        g = pl.program_id(0)
        u, w, dout = u_ref[...], w_ref[...], do_ref[...]
        cnt = _pool_count(s, g)
        pooled = _pool_windows(u, pad, s, g) / cnt - u
        mixed = _raw(pooled, w, _NN)
        ds_ref[...] = jnp.sum(dout * mixed, axis=0, keepdims=True)
        dmixed = dout * s_ref[...]
        dw_ref[...] = _raw(pooled, dmixed, _TN)
        dpooled = _raw(dmixed, w, _NT)
        dsel = dpooled / cnt
        pad[s:s + 16, :] = jnp.zeros((16, 128), F32)
        cur = jnp.where(g == 3, dsel, 0.0)
        for j, k in ((2, 8), (1, 4), (0, 2)):
            pad[0:s, :] = cur
            cur = cur + pad[k:k + s, :] + jnp.where(g == j, dsel, 0.0)
        pad[0:s, :] = cur
        cur = cur + pad[1:1 + s, :]
        du_ref[...] = (cur - dpooled).astype(BF16)

    return pl.pallas_call(
        body, name=name, grid=(4,),
        in_specs=[pl.BlockSpec((s, 128), lambda g: (0, O_PU // 128 + g)), pl.BlockSpec((None, 128, 128), lambda g: (g, 0, 0)),
                  pl.BlockSpec((1, 128), lambda g: (0, g)), pl.BlockSpec((s, 128), lambda g: (0, g))],
        out_specs=[pl.BlockSpec((s, 128), lambda g: (0, g)), pl.BlockSpec((None, 128, 128), lambda g: (g, 0, 0)),
                   pl.BlockSpec((1, 128), lambda g: (0, g))],
        out_shape=[jax.ShapeDtypeStruct((s, 512), BF16), jax.ShapeDtypeStruct((4, 128, 128), F32),
                   jax.ShapeDtypeStruct((1, 512), F32)],
        scratch_shapes=[pltpu.VMEM((s + 16, 128), F32)],
        compiler_params=_params("arbitrary"),
    )(proj, pw, ps, do)


def _xbc_col(i):
    return jnp.where(i < 2, O_XS // 512 + i, O_BC // 512)


def conv_fwd(proj, cw, cb, name):
    s = proj.shape[0]

    def body(x_ref, w_ref, b_ref, o_ref, t_ref, pad):
        pad[0:8, :] = jnp.zeros((8, 512), F32)
        pad[8:8 + s, :] = x_ref[...]
        y = b_ref[...] + sum(w_ref[k:k + 1, :] * pad[5 + k:5 + k + s, :] for k in range(4))
        act = y * _sig(y)
        o_ref[...] = act

        @pl.when(pl.program_id(0) < 2)
        def _():
            t_ref[...] = act.T

    return pl.pallas_call(
        body, name=name, grid=(3,),
        in_specs=[pl.BlockSpec((s, 512), lambda i: (0, _xbc_col(i))), pl.BlockSpec((4, 512), lambda i: (0, i)),
                  pl.BlockSpec((1, 512), lambda i: (0, i))],
        out_specs=[pl.BlockSpec((s, 512), lambda i: (0, i)), pl.BlockSpec((512, s), lambda i: (jnp.minimum(i, 1), 0))],
        out_shape=[jax.ShapeDtypeStruct((s, 1536), F32), jax.ShapeDtypeStruct((D, s), F32)],
        scratch_shapes=[pltpu.VMEM((s + 8, 512), F32)],
        compiler_params=_params("arbitrary"),
    )(proj, cw, cb)


def conv_bwd(proj, cw, cb, dxt, dbm, dcm, name):
    s = proj.shape[0]

    def body(x_ref, w_ref, b_ref, dxt_ref, dbm_ref, dcm_ref, dx_ref, dw_ref, db_ref, pad, pad2):
        pad[0:8, :] = jnp.zeros((8, 512), F32)
        pad[8:8 + s, :] = x_ref[...]
        y = b_ref[...] + sum(w_ref[k:k + 1, :] * pad[5 + k:5 + k + s, :] for k in range(4))
        sg = _sig(y)

        @pl.when(pl.program_id(0) < 2)
        def _():
            pad2[0:s, :] = dxt_ref[...].T

        @pl.when(pl.program_id(0) == 2)
        def _():
            pad2[0:s, 0:256] = dbm_ref[...]
            pad2[0:s, 256:512] = dcm_ref[...]

        dy = pad2[0:s, :] * (sg * (1.0 + y * (1.0 - sg)))
        db_ref[...] = jnp.sum(dy, axis=0, keepdims=True)
        for k in range(4):
            dw_ref[k:k + 1, :] = jnp.sum(dy * pad[5 + k:5 + k + s, :], axis=0, keepdims=True)
        pad2[s:s + 8, :] = jnp.zeros((8, 512), F32)
        pad2[0:s, :] = dy
        dx_ref[...] = sum(w_ref[k:k + 1, :] * pad2[3 - k:3 - k + s, :] for k in range(4)).astype(BF16)

    return pl.pallas_call(
        body, name=name, grid=(3,),
        in_specs=[pl.BlockSpec((s, 512), lambda i: (0, _xbc_col(i))), pl.BlockSpec((4, 512), lambda i: (0, i)),
                  pl.BlockSpec((1, 512), lambda i: (0, i)), pl.BlockSpec((512, s), lambda i: (jnp.minimum(i, 1), 0)),
                  pl.BlockSpec((s, 256), lambda i: (0, 0)), pl.BlockSpec((s, 256), lambda i: (0, 0))],
        out_specs=[pl.BlockSpec((s, 512), lambda i: (0, i)), pl.BlockSpec((4, 512), lambda i: (0, i)),
                   pl.BlockSpec((1, 512), lambda i: (0, i))],
        out_shape=[jax.ShapeDtypeStruct((s, 1536), BF16), jax.ShapeDtypeStruct((4, 1536), F32),
                   jax.ShapeDtypeStruct((1, 1536), F32)],
        scratch_shapes=[pltpu.VMEM((s + 8, 512), F32), pltpu.VMEM((s + 8, 512), F32)],
        compiler_params=_params("arbitrary"),
    )(proj, cw, cb, dxt, dbm, dcm)


def _ssd_chunk(xt, dtr, dtc, bm, cm, hprev, alog, dbias, dskip):
    ln = 128
    a = -jnp.exp(alog)
    dt_r = softplus(dtr + dbias)
    da_r = dt_r * a
    da_c = softplus(dtc + dbias) * a
    li = lax.broadcasted_iota(jnp.int32, (1, ln, ln), 1)
    si = lax.broadcasted_iota(jnp.int32, (1, ln, ln), 2)
    causal = si <= li
    acs_c = jnp.sum(jnp.where(causal, da_r, 0.0), axis=2, keepdims=True)
    acs_r = jnp.sum(jnp.where(li <= si, da_c, 0.0), axis=1, keepdims=True)
    acs_last = jnp.sum(da_r, axis=2, keepdims=True)
    decay = jnp.exp(jnp.where(causal, acs_c - acs_r, -jnp.inf))
    m = mm_nt(cm, bm)[None] * decay
    xdt = xt * dt_r
    y_diag = bmm_nt(xdt, m)
    bb = jnp.broadcast_to(bm[None], (8, ln, ln))
    cc = jnp.broadcast_to(cm[None], (8, ln, ln))
    states = bmm_nn(xdt * jnp.exp(acs_last - acs_r), bb)
    y_off = bmm_nt(hprev, cc) * jnp.exp(acs_r)
    hnew = hprev * jnp.exp(acs_last) + states
    return y_diag + y_off + xt * dskip, hnew


def _ssd_specs(nc, rev):
    cix = (lambda c: nc - 1 - c) if rev else (lambda c: c)
    hv = pl.BlockSpec((8, 1, 1), lambda g, c: (g, 0, 0))
    return [pl.BlockSpec((8, 64, 128), lambda g, c: (g, 0, cix(c))), pl.BlockSpec((8, 1, 128), lambda g, c: (g, 0, cix(c))),
            pl.BlockSpec((8, 128, 1), lambda g, c: (g, cix(c), 0)), pl.BlockSpec((128, 128), lambda g, c: (cix(c), 8 + g)),
            pl.BlockSpec((128, 128), lambda g, c: (cix(c), 10 + g))], hv, cix


def ssd_fwd(xt, dtr, dtc, xbc, alog, dbias, dskip, name):
    s = xt.shape[2]
    nc = s // 128
    specs, hv, _ = _ssd_specs(nc, False)

    def body(x_ref, dr_ref, dc_ref, b_ref, c_ref, al_ref, db_ref, dk_ref, y_ref, hs_ref, h_scr):
        @pl.when(pl.program_id(1) == 0)
        def _():
            h_scr[...] = jnp.zeros_like(h_scr)
        hp = h_scr[...]
        hs_ref[...] = hp
        y, hn = _ssd_chunk(x_ref[...], dr_ref[...], dc_ref[...], b_ref[...], c_ref[...], hp,
                           al_ref[...], db_ref[...], dk_ref[...])
        y_ref[...] = y
        h_scr[...] = hn

    return pl.pallas_call(
        body, name=name, grid=(2, nc), in_specs=specs + [hv, hv, hv],
        out_specs=[pl.BlockSpec((8, 64, 128), lambda g, c: (g, 0, c)),
                   pl.BlockSpec((None, None, 8, 64, 128), lambda g, c: (g, c, 0, 0, 0))],
        out_shape=[jax.ShapeDtypeStruct((16, 64, s), F32), jax.ShapeDtypeStruct((2, nc, 8, 64, 128), F32)],
        scratch_shapes=[pltpu.VMEM((8, 64, 128), F32)],
        compiler_params=_params("arbitrary", "arbitrary"),
    )(xt, dtr, dtc, xbc, xbc, alog, dbias, dskip)


def ssd_bwd(xt, dtr, dtc, xbc, alog, dbias, dskip, hs, dyt, name):
    s = xt.shape[2]
    nc = s // 128
    specs, hv, cix = _ssd_specs(nc, True)

    def body(x_ref, dr_ref, dc_ref, b_ref, c_ref, al_ref, db_ref, dk_ref, hs_ref, dy_ref,
             dx_out, ddr_out, ddc_out, dbm_out, dcm_out, dal_out, ddb_out, ddk_out, dh_scr):
        @pl.when(pl.program_id(1) == 0)
        def _():
            dh_scr[...] = jnp.zeros_like(dh_scr)
            dal_out[...] = jnp.zeros_like(dal_out)
            ddb_out[...] = jnp.zeros_like(ddb_out)
            ddk_out[...] = jnp.zeros_like(ddk_out)
        _, vjp = jax.vjp(_ssd_chunk, x_ref[...], dr_ref[...], dc_ref[...], b_ref[...], c_ref[...], hs_ref[...],
                         al_ref[...], db_ref[...], dk_ref[...])
        g = vjp((dy_ref[...], dh_scr[...]))
        dx_out[...] = g[0]
        ddr_out[...] = g[1]
        ddc_out[...] = g[2]
        dbm_out[...] = g[3]
        dcm_out[...] = g[4]
        dh_scr[...] = g[5]
        dal_out[...] += g[6]
        ddb_out[...] += g[7]
        ddk_out[...] += g[8]

    return pl.pallas_call(
        body, name=name, grid=(2, nc),
        in_specs=specs + [hv, hv, hv, pl.BlockSpec((None, None, 8, 64, 128), lambda g, c: (g, cix(c), 0, 0, 0)),
                          pl.BlockSpec((8, 64, 128), lambda g, c: (g, 0, cix(c)))],
        out_specs=[pl.BlockSpec((8, 64, 128), lambda g, c: (g, 0, cix(c))), pl.BlockSpec((8, 1, 128), lambda g, c: (g, 0, cix(c))),
                   pl.BlockSpec((8, 128, 1), lambda g, c: (g, cix(c), 0)), pl.BlockSpec((128, 128), lambda g, c: (cix(c), g)),
                   pl.BlockSpec((128, 128), lambda g, c: (cix(c), g)), hv, hv, hv],
        out_shape=[jax.ShapeDtypeStruct((16, 64, s), F32), jax.ShapeDtypeStruct((16, 1, s), F32),
                   jax.ShapeDtypeStruct((16, s, 1), F32), jax.ShapeDtypeStruct((s, 256), F32),
                   jax.ShapeDtypeStruct((s, 256), F32)] + [jax.ShapeDtypeStruct((16, 1, 1), F32)] * 3,
        scratch_shapes=[pltpu.VMEM((8, 64, 128), F32)],
        compiler_params=_params("arbitrary", "arbitrary"),
    )(xt, dtr, dtc, xbc, xbc, alog, dbias, dskip, hs, dyt)


def _merge(oa, ob, y, z, gla, glb, glc, x, g1, nw, ea, eb, ec, eo, wba, wbb, wbc, wout):
    gated = y * (z * _sig(z))
    sq = gated * gated
    left = lax.broadcasted_iota(jnp.int32, (1, D), 1) < 512
    ms0 = jnp.sum(jnp.where(left, sq, 0.0), axis=-1, keepdims=True) / 512.0
    ms1 = jnp.sum(jnp.where(left, 0.0, sq), axis=-1, keepdims=True) / 512.0
    oc = gated * jnp.where(left, lax.rsqrt(ms0 + EPS), lax.rsqrt(ms1 + EPS)) * nw
    ya, yb, yc = mm_nc(oa, wba) + ea, mm_nc(ob, wbb) + eb, mm_nc(oc, wbc) + ec
    merged = _sig(gla) * ya + _sig(glb) * yb + _sig(glc) * yc
    x1 = x + g1 * (mm_nc(merged, wout) + eo)
    return x1, (oc, merged)


def _merge_specs(tr):
    row = lambda w: pl.BlockSpec((tr, w), lambda i: (i, 0))
    acts = [row(D), row(512), pl.BlockSpec((D, tr), lambda i: (0, i)), pl.BlockSpec((tr, D), lambda i: (i, O_Z // D)),
            pl.BlockSpec((tr, 3 * D), lambda i: (i, 0)), row(D), pl.BlockSpec((8, D), lambda i: (0, 0))]
    cst = lambda r: pl.BlockSpec((r, D), lambda i: (0, 0), **CONST)
    return acts, [cst(D), cst(512), cst(D), cst(D)], row


def merge_fwd(oa, ob, y, proj, x, mvec, wba, wbb, wbc, wout, name):
    s = x.shape[0]
    tr = _pick(s, 256)
    acts, wts, row = _merge_specs(tr)

    def body(oa_ref, ob_ref, y_ref, z_ref, gl_ref, x_ref, mv_ref, wba_ref, wbb_ref, wbc_ref, wout_ref, o_ref):
        zero = jnp.zeros((1, D), F32)
        x1, _ = _merge(oa_ref[...], ob_ref[...], y_ref[...].T, z_ref[...], gl_ref[:, 0:D], gl_ref[:, D:2 * D],
                       gl_ref[:, 2 * D:3 * D], x_ref[...], mv_ref[0:1, :], mv_ref[1:2, :], zero, zero, zero, zero,
                       wba_ref[...], wbb_ref[...], wbc_ref[...], wout_ref[...])
        o_ref[...] = x1

    return pl.pallas_call(
        body, name=name, grid=(s // tr,), in_specs=acts + wts, out_specs=row(D),
        out_shape=jax.ShapeDtypeStruct((s, D), F32), compiler_params=_params("arbitrary"),
    )(oa, ob, y, proj, proj, x, mvec, wba, wbb, wbc, wout)


def merge_bwd(oa, ob, y, proj, x, mvec, wba, wbb, wbc, wout, dx1, name):
    s = x.shape[0]
    tr = _pick(s, 128)
    acts, wts, row = _merge_specs(tr)

    def body(oa_ref, ob_ref, y_ref, z_ref, gl_ref, x_ref, mv_ref, wba_ref, wbb_ref, wbc_ref, wout_ref, dx1_ref,
             doa_o, dob_o, dy_o, dz_o, dgl_o, dx_o, dmv_o, dya_o, dyb_o, dyc_o, dpre_o, oc_o, mg_o):
        zero = jnp.zeros((tr, D), F32)
        wts_ = (wba_ref[...], wbb_ref[...], wbc_ref[...], wout_ref[...])
        f = lambda *a: _merge(*a, *wts_)
        _, vjp, (oc, merged) = jax.vjp(
            f, oa_ref[...], ob_ref[...], y_ref[...].T, z_ref[...], gl_ref[:, 0:D], gl_ref[:, D:2 * D],
            gl_ref[:, 2 * D:3 * D], x_ref[...], mv_ref[0:1, :], mv_ref[1:2, :], zero, zero, zero, zero, has_aux=True)
        g = vjp(dx1_ref[...])
        doa_o[...] = g[0]
        dob_o[...] = g[1]
        dy_o[...] = g[2].T
        dz_o[...] = g[3].astype(BF16)
        dgl_o[:, 0:D] = g[4].astype(BF16)
        dgl_o[:, D:2 * D] = g[5].astype(BF16)
        dgl_o[:, 2 * D:3 * D] = g[6].astype(BF16)
        dx_o[...] = g[7]

        @pl.when(pl.program_id(0) == 0)
        def _():
            dmv_o[...] = jnp.zeros_like(dmv_o)

        dmv_o[0:1, :] += g[8]
        dmv_o[1:2, :] += g[9]
        dya_o[...] = g[10].astype(BF16)
        dyb_o[...] = g[11].astype(BF16)
        dyc_o[...] = g[12].astype(BF16)
        dpre_o[...] = g[13].astype(BF16)
        oc_o[...] = oc.astype(BF16)
        mg_o[...] = merged.astype(BF16)

    sd = lambda w, dt: jax.ShapeDtypeStruct((s, w), dt)
    return pl.pallas_call(
        body, name=name, grid=(s // tr,), in_specs=acts + wts + [row(D)],
        out_specs=[row(D), row(512), pl.BlockSpec((D, tr), lambda i: (0, i)), row(D), row(3 * D), row(D),
                   pl.BlockSpec((8, D), lambda i: (0, 0))] + [row(D)] * 6,
        out_shape=[sd(D, F32), sd(512, F32), jax.ShapeDtypeStruct((D, s), F32), sd(D, BF16), sd(3 * D, BF16), sd(D, F32),
                   jax.ShapeDtypeStruct((8, D), F32)] + [sd(D, BF16)] * 6,
        compiler_params=_params("arbitrary"),
    )(oa, ob, y, proj, proj, x, mvec, wba, wbb, wbc, wout, dx1)


def _conv3(u_scr, w_ref, first, rows, lanes):
    return sum(w_ref[k:k + 1, :] * u_scr[first + k:first + k + rows, lanes] for k in range(3))


def _ffn_tile_specs(tf, tile):
    def at(rows, off):
        return pl.BlockSpec((rows, tf), lambda *g: (0, off + tile(*g)))
    return [at(D, 0), at(D, FFN_NT), at(3, 0), at(3, FFN_NT), at(1, 0), at(1, FFN_NT)]


def ffn_fwd(x1, fvec, wup, cw, cb, wdn, name):
    s = x1.shape[0]
    tr, tf = _pick(s, 512), FFN_TILE
    lg, lv = slice(0, tf), slice(tf, 2 * tf)

    def body(x_ref, v_ref, wg_ref, wv_ref, cwg_ref, cwv_ref, cbg_ref, cbv_ref, wd_ref, x2_ref, h_ref, pre_ref,
             h_scr, u_scr, acc):
        i, t = pl.program_id(0), pl.program_id(1)

        @pl.when(t == 0)
        def _():
            @pl.when(i == 0)
            def _():
                h_scr[0:16, :] = jnp.zeros((16, D), BF16)

            @pl.when(i > 0)
            def _():
                h_scr[0:16, :] = h_scr[tr:tr + 16, :]

            h = (_rms(x_ref[...], v_ref[0:1, :], D) * (1.0 + v_ref[2:3, :]) + v_ref[1:2, :]).astype(BF16)
            h_scr[16:16 + tr, :] = h
            h_ref[...] = h
            acc[...] = jnp.zeros_like(acc)

        u_scr[:, lg] = jnp.dot(h_scr[...], wg_ref[...], preferred_element_type=F32)
        u_scr[:, lv] = jnp.dot(h_scr[...], wv_ref[...], preferred_element_type=F32)
        cg = _conv3(u_scr, cwg_ref, 14, tr, lg) + cbg_ref[...]
        cval = _conv3(u_scr, cwv_ref, 14, tr, lv) + cbv_ref[...]
        acc[...] += _raw(cg * _sig(cg) * cval, wd_ref[...], _NN)

        @pl.when(t == FFN_NT - 1)
        def _():
            pre_ref[...] = acc[...]
            x2_ref[...] = x_ref[...] + v_ref[3:4, :] * acc[...]

    row = pl.BlockSpec((tr, D), lambda i, t: (i, 0))
    return pl.pallas_call(
        body, name=name, grid=(s // tr, FFN_NT),
        in_specs=[row, pl.BlockSpec((8, D), lambda i, t: (0, 0))] + _ffn_tile_specs(tf, lambda i, t: t)
                 + [pl.BlockSpec((tf, D), lambda i, t: (t, 0))],
        out_specs=[row, row, row],
        out_shape=[jax.ShapeDtypeStruct((s, D), F32), jax.ShapeDtypeStruct((s, D), BF16), jax.ShapeDtypeStruct((s, D), F32)],
        scratch_shapes=[pltpu.VMEM((tr + 16, D), BF16), pltpu.VMEM((tr + 16, 2 * tf), F32), pltpu.VMEM((tr, D), F32)],
        compiler_params=_params("arbitrary", "arbitrary"),
    )(x1, fvec, wup, wup, cw, cw, cb, cb, wdn)


def ffn_bwd(h2, dx2, fvec, wup, cw, cb, wdn, name):
    s = h2.shape[0]
    tr, tf = _pick(s, 512), FFN_TILE
    ni, nb = s // tr, s // 16
    lg, lv = slice(0, tf), slice(tf, 2 * tf)

    def body(hp_ref, hm_ref, hn_ref, dm_ref, dn_ref, v_ref, wg_ref, wv_ref, cwg_ref, cwv_ref, cbg_ref, cbv_ref, wd_ref,
             dup_ref, act_ref, dcw_ref, u_scr, dc_scr):
        i = pl.program_id(1)
        hfull = jnp.concatenate([jnp.where(i > 0, hp_ref[...], jnp.zeros((16, D), BF16)), hm_ref[...],
                                 jnp.where(i < ni - 1, hn_ref[...], jnp.zeros((16, D), BF16))], axis=0)
        u_scr[:, lg] = jnp.dot(hfull, wg_ref[...], preferred_element_type=F32)
        u_scr[:, lv] = jnp.dot(hfull, wv_ref[...], preferred_element_type=F32)
        cg = _conv3(u_scr, cwg_ref, 14, tr + 16, lg) + cbg_ref[...]
        cval = _conv3(u_scr, cwv_ref, 14, tr + 16, lv) + cbv_ref[...]
        g2 = v_ref[3:4, :]
        dpre = jnp.concatenate([dm_ref[...] * g2, jnp.where(i < ni - 1, dn_ref[...], 0.0) * g2], axis=0)
        dact = _raw(dpre, wd_ref[...], _NT)
        sg = _sig(cg)
        sl = cg * sg
        dc_scr[:, lg] = dact * cval * (sg * (1.0 + cg * (1.0 - sg)))
        dc_scr[:, lv] = dact * sl
        act_ref[...] = (sl * cval)[0:tr, :].astype(BF16)

        @pl.when(i == 0)
        def _():
            dcw_ref[...] = jnp.zeros_like(dcw_ref)

        for half, lanes, cw_ref in ((0, lg, cwg_ref), (1, lv, cwv_ref)):
            dup_ref[half] = sum(cw_ref[k:k + 1, :] * dc_scr[2 - k:2 - k + tr, lanes] for k in range(3)).astype(BF16)
            dcm = dc_scr[0:tr, lanes]
            for k in range(3):
                dcw_ref[half, k:k + 1, :] += jnp.sum(dcm * u_scr[14 + k:14 + k + tr, lanes], axis=0, keepdims=True)
            dcw_ref[half, 3:4, :] += jnp.sum(dcm, axis=0, keepdims=True)

    r16 = tr // 16
    prev = lambda t, i: (jnp.maximum(i * r16 - 1, 0), 0)
    nxt = lambda t, i: (jnp.minimum((i + 1) * r16, nb - 1), 0)
    main = lambda t, i: (i, 0)
    return pl.pallas_call(
        body, name=name, grid=(FFN_NT, ni),
        in_specs=[pl.BlockSpec((16, D), prev), pl.BlockSpec((tr, D), main), pl.BlockSpec((16, D), nxt),
                  pl.BlockSpec((tr, D), main), pl.BlockSpec((16, D), nxt), pl.BlockSpec((8, D), lambda t, i: (0, 0))]
                 + _ffn_tile_specs(tf, lambda t, i: t) + [pl.BlockSpec((tf, D), lambda t, i: (t, 0))],
        out_specs=[pl.BlockSpec((2, tr, tf), lambda t, i: (0, i, t)), pl.BlockSpec((tr, tf), lambda t, i: (i, t)),
                   pl.BlockSpec((2, 8, tf), lambda t, i: (0, 0, t))],
        out_shape=[jax.ShapeDtypeStruct((2, s, FFN), BF16), jax.ShapeDtypeStruct((s, FFN), BF16),
                   jax.ShapeDtypeStruct((2, 8, FFN), F32)],
        scratch_shapes=[pltpu.VMEM((tr + 32, 2 * tf), F32), pltpu.VMEM((tr + 16, 2 * tf), F32)],
        compiler_params=_params("arbitrary", "arbitrary"),
    )(h2, h2, h2, dx2, dx2, fvec, wup, wup, cw, cw, cb, cb, wdn)


def loss_head(y, target):
    s = y.shape[0]
    tr = _pick(s, 512)

    def body(y_ref, t_ref, dx_ref, l_ref):
        @pl.when(pl.program_id(0) == 0)
        def _():
            l_ref[...] = jnp.zeros_like(l_ref)
        err = y_ref[...] - t_ref[...]
        dx_ref[...] = err / float(D)
        l_ref[...] += 0.5 * jnp.sum(jnp.sum(err * err, axis=-1, keepdims=True) / float(D), axis=0, keepdims=True)

    row = pl.BlockSpec((tr, D), lambda i: (i, 0))
    return pl.pallas_call(
        body, name="loss_head", grid=(s // tr,), in_specs=[row, row],
        out_specs=[row, pl.BlockSpec((8, 128), lambda i: (0, 0))],
        out_shape=[jax.ShapeDtypeStruct((s, D), F32), jax.ShapeDtypeStruct((8, 128), F32)],
        compiler_params=_params("arbitrary"),
    )(y, target)


def adamw(parts, w, m, v, name, tok=None):
    nseg = len(parts)
    p, r, c = parts[0].shape
    tr = _pick(r, 256 if c > 128 else 2048, 8)
    ni = r // tr
    tok = jnp.zeros((8, 128), F32) if tok is None else tok

    def body(*refs):
        p_refs = refs[:nseg]
        w_ref, m_ref, v_ref, _, g_out, d_out, m_out, v_out, g_scr = refs[nseg:]
        for q in range(nseg):
            @pl.when(pl.program_id(0) == q)
            def _(q=q):
                g = p_refs[q][0].astype(F32)
                for j in range(1, p):
                    g = g + p_refs[q][j].astype(F32)
                g_scr[...] = g
        g = g_scr[...]
        mn = B1 * m_ref[...] + (1.0 - B1) * g
        vn = B2 * v_ref[...] + (1.0 - B2) * (g * g)
        m_hat = mn / (1.0 - B1 ** STEP)
        v_hat = vn / (1.0 - B2 ** STEP)
        g_out[...] = g
        d_out[...] = -LR * (m_hat / (jnp.sqrt(v_hat) + ADAM_EPS) + WD * w_ref[...])
        m_out[...] = mn
        v_out[...] = vn

    row = pl.BlockSpec((tr, c), lambda l, i: (l * ni + i, 0))
    part = lambda q: pl.BlockSpec((p, tr, c), lambda l, i: (0, jnp.clip((l - q) * ni + i, 0, ni - 1), 0))
    return pl.pallas_call(
        body, name=name, grid=(nseg, ni),
        in_specs=[part(q) for q in range(nseg)] + [row, row, row, pl.BlockSpec((8, 128), lambda l, i: (0, 0))],
        out_specs=[row] * 4, out_shape=[jax.ShapeDtypeStruct((nseg * r, c), F32)] * 4,
        scratch_shapes=[pltpu.VMEM((tr, c), F32)],
        compiler_params=_params("arbitrary", "arbitrary"),
    )(*parts, w, m, v, tok)


def _padc(a, n):
    return jnp.pad(a, [(0, 0)] * (a.ndim - 1) + [(0, n - a.shape[-1])])


def _swap16(a):
    return jnp.concatenate([a[..., 16:32], a[..., 0:16]], axis=-1)


def _shard_cols(g8, a, b):
    c = g8.shape[2]
    return [g8[j][:, max(a, j * c) - j * c:min(b, (j + 1) * c) - j * c] for j in range(a // c, (b - 1) // c + 1)]


def _win_layout(g8):
    cols = lambda a, b: _shard_cols(g8, a, b)
    kr = jnp.concatenate(cols(640, 672), axis=1)
    dt = jnp.concatenate(cols(3744, 3760), axis=1)
    return jnp.concatenate(cols(3760, 6832) + cols(2208, 3232) + cols(1184, 2208) + cols(672, 1184) + cols(3232, 3744)
                           + cols(384, 640) + [_padc(kr, 128), _padc(_swap16(kr), 128), _padc(dt, 128),
                                               jnp.zeros((g8.shape[1], 128), g8.dtype)] + cols(0, 384), axis=1)


def _win_grad_shards(g):
    kr = (g[:, O_KR:O_KR + 32].astype(F32) + _swap16(g[:, O_KRS:O_KRS + 32].astype(F32))).astype(g.dtype)
    segs = [(g, O_QL, 384), (g, O_CKV, 256), (kr, 0, 32), (g, O_PU, 512), (g, O_Z, D), (g, O_XS, D), (g, O_BC, 512),
            (g, O_DT, 16), (g, O_G, 3 * D)]
    shards, width = [], sum(w for _, _, w in segs) // NDEV
    for j in range(NDEV):
        a, b, off, pieces = width * j, width * (j + 1), 0, []
        for arr, lo, w in segs:
            s0, s1 = max(a, off), min(b, off + w)
            if s0 < s1:
                pieces.append(arr[:, lo + s0 - off:lo + s1 - off])
            off += w
        shards.append(jnp.concatenate(pieces, axis=1))
    return jnp.stack(shards).astype(BF16)


def _wq_layout(w):
    w = w.reshape(384, HEADS, 96).transpose(1, 0, 2)
    rope = w[:, :, 64:96]
    return jnp.concatenate([_padc(w[:, :, 0:64], 128), _padc(rope, 128), _padc(_swap16(rope), 128)], axis=2)


def _wq_unlayout(g):
    rope = g[:, :, 128:160] + _swap16(g[:, :, 256:288])
    return jnp.concatenate([g[:, :, 0:64], rope], axis=2).transpose(1, 0, 2).reshape(384, HEADS * 96)


def _wkv_layout(w):
    w = w.reshape(256, HEADS, 128).transpose(1, 0, 2)
    return jnp.concatenate([_padc(w[:, :, 0:64], 128), _padc(w[:, :, 64:128], 128)], axis=2)


def _wkv_unlayout(g):
    return jnp.concatenate([g[:, :, 0:64], g[:, :, 128:192]], axis=2).transpose(1, 0, 2).reshape(256, HEADS * 128)


def _wba_layout(w):
    return jnp.pad(w.reshape(HEADS, 64, D), ((0, 0), (0, 64), (0, 0))).reshape(HEADS * 128, D)


def _rows8(rows, width):
    out = jnp.stack([_padc(r.astype(F32), width) for r in rows])
    return jnp.pad(out, ((0, 8 - out.shape[0]), (0, 0)))


def _mla_vec(qa, kva, qn, kn):
    def row(n):
        return jnp.concatenate([_padc(n[0:64], 128), _padc(n[64:96], 128), _padc(_swap16(n[64:96]), 128)])
    return _rows8([qa, kva, row(qn), row(kn)], 512)


def _mla_unvec(g):
    def un(r):
        return jnp.concatenate([r[0:64], r[128:160] + _swap16(r[256:288])])
    return g[0, 0:384], g[1, 0:256], un(g[2]), un(g[3])


SMALL = (("ada_b", (6 * D,)), ("norm1_w", (D,)), ("q_a_norm", (384,)), ("kv_a_norm", (256,)), ("q_norm", (96,)),
         ("k_norm", (96,)), ("pool_w", (4, 128, 128)), ("pool_scale", (512,)), ("ssd_conv_b", (1536,)),
         ("ssd_dt_bias", (16,)), ("ssd_a_log", (16,)), ("ssd_d", (16,)), ("ssd_norm_w", (D,)), ("norm2_w", (D,)),
         ("ffn_conv_b", (2 * FFN,)), ("ssd_conv_w", (4, 1536)), ("ffn_conv_w", (3, 2 * FFN)))
SHARDED_SMALL = {"ssd_conv_w": 192, "ffn_conv_w": 704}


def _pack_rows(shp):
    return -(-math.prod(shp) // 1024) * 8


def _pack(small):
    pieces = []
    for n, shp in SMALL:
        pieces.append(small[n].reshape(-1).astype(F32))
        fill = _pack_rows(shp) * 128 - math.prod(shp)
        if fill:
            pieces.append(jnp.zeros((fill,), F32))
    return jnp.concatenate(pieces).reshape(-1, 128)


def _unpack_parts(packs):
    out, off = {}, 0
    for n, shp in SMALL:
        rows, size = _pack_rows(shp), math.prod(shp)
        r, c = math.prod(shp[:-1]), shp[-1]
        per_layer = [pk[:, off:off + rows].reshape(NDEV, rows * 128)[:, 0:size].reshape(NDEV, r, c) for pk in packs]
        out[n] = jnp.concatenate(per_layer, axis=1)
        off += rows
    return out


GROUP_A = ("w_in", "w_q_b", "w_kv_b")
GROUP_B = ("w_branch", "w_out", "ffn_up", "ffn_down")
BIG = GROUP_A + GROUP_B
COL_SHARDED = ("w_in", "w_q_b", "w_kv_b", "ffn_up")


def _behind(arrs, tok):
    arrs = list(arrs)
    j = min(range(len(arrs)), key=lambda q: arrs[q].size)
    arrs[j] = arrs[j] + tok[0, 0].astype(arrs[j].dtype)
    return arrs


def _gathered_full(g, name):
    if name in COL_SHARDED:
        return g.transpose(1, 0, 2).reshape(g.shape[1], NDEV * g.shape[2])
    return g.reshape(NDEV * g.shape[1], g.shape[2])


def _to_shards(full, name):
    if name == "w_in":
        return _win_grad_shards(full)
    if name in COL_SHARDED:
        r, c = full.shape
        return full.reshape(r, NDEV, c // NDEV).transpose(1, 0, 2).astype(BF16)
    r, c = full.shape
    return full.reshape(NDEV, r // NDEV, c).astype(BF16)


def _fwd_a(x, lw, mod, cos2, sin2, l, tok):
    sh1, sc1, g1, sh2, sc2, g2 = [mod[j * D:(j + 1) * D] for j in range(6)]
    vec1 = _rows8([lw["norm1_w"], sh1, sc1], D) + tok[0, 0]
    proj, h1, dt_cols = norm_proj_fwd(x, vec1, lw["win"], f"inproj_fwd{l}")
    q, k, v = mla_pre_fwd(proj, lw["wq"], lw["wkv"], lw["mla_vec"], cos2, sin2, f"mla_pre_fwd{l}")
    oa = mla_attn_fwd(q, k, v, f"mla_attn_fwd{l}")
    ob = pool_fwd(proj, lw["pool_w"], lw["pool_scale"].reshape(1, 512), f"pool_fwd{l}")
    xbc, xt = conv_fwd(proj, lw["ssd_conv_w"], lw["ssd_conv_b"].reshape(1, 1536), f"conv_fwd{l}")
    s = x.shape[0]
    xt = xt.reshape(16, 64, s)
    dt = dt_cols[:, 0:16].T
    dtr, dtc = dt[:, None, :], dt[:, :, None]
    hv = lambda a: a.reshape(16, 1, 1)
    yt, hs = ssd_fwd(xt, dtr, dtc, xbc, hv(lw["ssd_a_log"]), hv(lw["ssd_dt_bias"]), hv(lw["ssd_d"]), f"ssd_fwd{l}")
    return dict(x=x, vec1=vec1, proj=proj, h1=h1, q=q, k=k, v=v, oa=oa, ob=ob, xbc=xbc, xt=xt, dtr=dtr, dtc=dtc,
                hs=hs, yt=yt.reshape(D, s), mvec=_rows8([g1, lw["ssd_norm_w"]], D),
                fvec=_rows8([lw["norm2_w"], sh2, sc2, g2], D))


def _fwd_b(sv, lw, l, tok):
    sv["mvec"] = sv["mvec"] + tok[0, 0]
    x1 = merge_fwd(sv["oa"], sv["ob"], sv["yt"], sv["proj"], sv["x"], sv["mvec"], lw["wba"], lw["wbb"], lw["wbc"],
                   lw["wout"], f"merge_fwd{l}")
    x2, h2, pre = ffn_fwd(x1, sv["fvec"], lw["wup"], lw["ffn_conv_w"], lw["ffn_conv_b"].reshape(1, 2 * FFN), lw["wdn"],
                          f"ffn_fwd{l}")
    sv.update(x1=x1, h2=h2, pre=pre)
    return x2


def _bwd_b(dx2, lw, sv, l, tok):
    grads, small = {}, {}
    fvec = sv["fvec"] + tok[0, 0]
    dup, act, dcw = ffn_bwd(sv["h2"], dx2, fvec, lw["wup"], lw["ffn_conv_w"], lw["ffn_conv_b"].reshape(1, 2 * FFN),
                            lw["wdn"], f"ffn_bwd{l}")
    grads["ffn_down"] = tn_matmul(act, dx2, f"dw_down{l}", scale=fvec[3:4])
    grads["ffn_up"] = tn_matmul(sv["h2"], dup, f"dw_up{l}")
    dx1, dfvec = norm_proj_bwd(sv["x1"], fvec, dup, lw["wup"], dx2, sv["pre"], f"ffn_norm_bwd{l}")
    small["ffn_conv_w"] = jnp.concatenate([dcw[0, 0:3], dcw[1, 0:3]], axis=1)
    small["ffn_conv_b"] = jnp.concatenate([dcw[0, 3], dcw[1, 3]])
    small["norm2_w"] = dfvec[0]
    (doa, dob, dyt, dz, dgl, dx, dmvec, dya, dyb, dyc, dpre, oc, merged) = merge_bwd(
        sv["oa"], sv["ob"], sv["yt"], sv["proj"], sv["x"], sv["mvec"], lw["wba"], lw["wbb"], lw["wbc"], lw["wout"], dx1,
        f"merge_bwd{l}")
    dwba = tn_matmul(sv["oa"], dya, f"dw_ba{l}").reshape(HEADS, 128, D)[:, 0:64].reshape(512, D)
    grads["w_branch"] = jnp.concatenate([dwba, tn_matmul(sv["ob"], dyb, f"dw_bb{l}"), tn_matmul(oc, dyc, f"dw_bc{l}")])
    grads["w_out"] = tn_matmul(merged, dpre, f"dw_out{l}")
    small["ssd_norm_w"] = dmvec[1]
    small["dmod_b"] = (dmvec[0], dfvec[1], dfvec[2], dfvec[3])
    return dx, dict(doa=doa, dob=dob, dyt=dyt, dz=dz, dgl=dgl), grads, small


def _bwd_a(dx, cot, lw, sv, cos2, sin2, l, tok, small):
    s = dx.shape[0]
    grads = {}
    doa, dob, dz, dgl = cot["doa"], cot["dob"], cot["dz"], cot["dgl"]
    hv = lambda a: a.reshape(16, 1, 1)
    dxt, ddtr, ddtc, dbm, dcm, dal, ddb, ddk = ssd_bwd(
        sv["xt"], sv["dtr"], sv["dtc"], sv["xbc"], hv(lw["ssd_a_log"]) + tok[0, 0], hv(lw["ssd_dt_bias"]),
        hv(lw["ssd_d"]), sv["hs"], cot["dyt"].reshape(16, 64, s), f"ssd_bwd{l}")
    small["ssd_a_log"], small["ssd_dt_bias"], small["ssd_d"] = dal.reshape(16), ddb.reshape(16), ddk.reshape(16)
    dxbc, dscw, dscb = conv_bwd(sv["proj"], lw["ssd_conv_w"], lw["ssd_conv_b"].reshape(1, 1536), dxt.reshape(D, s),
                                dbm, dcm, f"conv_bwd{l}")
    small["ssd_conv_w"], small["ssd_conv_b"] = dscw, dscb.reshape(1536)
    ddt = (ddtr[:, 0, :] + ddtc[:, :, 0]).T
    du, dpw, dps = pool_bwd(sv["proj"], lw["pool_w"], lw["pool_scale"].reshape(1, 512), dob, f"pool_bwd{l}")
    small["pool_w"], small["pool_scale"] = dpw, dps.reshape(512)
    dq, dk, dv = mla_attn_bwd(sv["q"], sv["k"], sv["v"], doa, f"mla_attn_bwd{l}")
    dql, dckv, dkr, dkrs, dwq, dwkv, dmv = mla_pre_bwd(sv["proj"], lw["wq"], lw["wkv"], lw["mla_vec"], cos2, sin2,
                                                       dq, dk, dv, f"mla_pre_bwd{l}")
    grads["w_q_b"], grads["w_kv_b"] = _wq_unlayout(dwq), _wkv_unlayout(dwkv)
    small["q_a_norm"], small["kv_a_norm"], small["q_norm"], small["k_norm"] = _mla_unvec(dmv)
    dproj = jnp.concatenate([dgl, dxbc[:, 0:D], dz, du, dxbc[:, D:1536], dckv, dkr, dkrs,
                             _padc(ddt, 128).astype(BF16), jnp.zeros((s, 128), BF16), dql], axis=1)
    grads["w_in"] = tn_matmul(sv["h1"], dproj, f"dw_in{l}")
    return dproj, grads, small


def _bwd_in(dx, dproj, lw, sv, l, tok, small):
    dx0, dvec1 = norm_proj_bwd(sv["x"], sv["vec1"] + tok[0, 0], dproj, lw["win"], dx, None, f"inproj_bwd{l}")
    small["norm1_w"] = dvec1[0]
    small["ada_b"] = jnp.concatenate([dvec1[1], dvec1[2], *small.pop("dmod_b")])
    return dx0, small


def kernel(x, c, positions, ada_w, ada_b, norm1_w, w_in, q_a_norm, w_q_b, kv_a_norm, w_kv_b, q_norm, k_norm, pool_w, pool_scale, ssd_conv_w, ssd_conv_b, ssd_dt_bias, ssd_a_log, ssd_d, ssd_norm_w, w_branch, w_out, norm2_w, ffn_up, ffn_conv_w, ffn_conv_b, ffn_down, loss_target, m_ada_w, m_ada_b, m_norm1_w, m_w_in, m_q_a_norm, m_w_q_b, m_kv_a_norm, m_w_kv_b, m_q_norm, m_k_norm, m_pool_w, m_pool_scale, m_ssd_conv_w, m_ssd_conv_b, m_ssd_dt_bias, m_ssd_a_log, m_ssd_d, m_ssd_norm_w, m_w_branch, m_w_out, m_norm2_w, m_ffn_up, m_ffn_conv_w, m_ffn_conv_b, m_ffn_down, v_ada_w, v_ada_b, v_norm1_w, v_w_in, v_q_a_norm, v_w_q_b, v_kv_a_norm, v_w_kv_b, v_q_norm, v_k_norm, v_pool_w, v_pool_scale, v_ssd_conv_w, v_ssd_conv_b, v_ssd_dt_bias, v_ssd_a_log, v_ssd_d, v_ssd_norm_w, v_w_branch, v_w_out, v_norm2_w, v_ffn_up, v_ffn_conv_w, v_ffn_conv_b, v_ffn_down):
    p = dict(ada_w=ada_w, ada_b=ada_b, norm1_w=norm1_w, w_in=w_in, q_a_norm=q_a_norm, w_q_b=w_q_b, kv_a_norm=kv_a_norm,
             w_kv_b=w_kv_b, q_norm=q_norm, k_norm=k_norm, pool_w=pool_w, pool_scale=pool_scale, ssd_conv_w=ssd_conv_w,
             ssd_conv_b=ssd_conv_b, ssd_dt_bias=ssd_dt_bias, ssd_a_log=ssd_a_log, ssd_d=ssd_d, ssd_norm_w=ssd_norm_w,
             w_branch=w_branch, w_out=w_out, norm2_w=norm2_w, ffn_up=ffn_up, ffn_conv_w=ffn_conv_w, ffn_conv_b=ffn_conv_b,
             ffn_down=ffn_down)
    mom = dict(ada_w=m_ada_w, ada_b=m_ada_b, norm1_w=m_norm1_w, w_in=m_w_in, q_a_norm=m_q_a_norm, w_q_b=m_w_q_b,
               kv_a_norm=m_kv_a_norm, w_kv_b=m_w_kv_b, q_norm=m_q_norm, k_norm=m_k_norm, pool_w=m_pool_w,
               pool_scale=m_pool_scale, ssd_conv_w=m_ssd_conv_w, ssd_conv_b=m_ssd_conv_b, ssd_dt_bias=m_ssd_dt_bias,
               ssd_a_log=m_ssd_a_log, ssd_d=m_ssd_d, ssd_norm_w=m_ssd_norm_w, w_branch=m_w_branch, w_out=m_w_out,
               norm2_w=m_norm2_w, ffn_up=m_ffn_up, ffn_conv_w=m_ffn_conv_w, ffn_conv_b=m_ffn_conv_b, ffn_down=m_ffn_down)
    var = dict(ada_w=v_ada_w, ada_b=v_ada_b, norm1_w=v_norm1_w, w_in=v_w_in, q_a_norm=v_q_a_norm, w_q_b=v_w_q_b,
               kv_a_norm=v_kv_a_norm, w_kv_b=v_w_kv_b, q_norm=v_q_norm, k_norm=v_k_norm, pool_w=v_pool_w,
               pool_scale=v_pool_scale, ssd_conv_w=v_ssd_conv_w, ssd_conv_b=v_ssd_conv_b, ssd_dt_bias=v_ssd_dt_bias,
               ssd_a_log=v_ssd_a_log, ssd_d=v_ssd_d, ssd_norm_w=v_ssd_norm_w, w_branch=v_w_branch, w_out=v_w_out,
               norm2_w=v_norm2_w, ffn_up=v_ffn_up, ffn_conv_w=v_ffn_conv_w, ffn_conv_b=v_ffn_conv_b, ffn_down=v_ffn_down)
    names = list(p)
    me = 4 * lax.axis_index("x") + 2 * lax.axis_index("y") + lax.axis_index("c")
    xs, tgt = x[0], loss_target[0]
    s = xs.shape[0]

    inv_freq = ROPE_THETA ** (-jnp.arange(0, 32, 2, dtype=F32) / 32.0)
    ang = positions[0].astype(F32)[:, None] * inv_freq
    cos, sin = jnp.cos(ang), jnp.sin(ang)
    cos2 = _padc(jnp.concatenate([cos, cos], axis=1), 128)
    sin2 = _padc(jnp.concatenate([-sin, sin], axis=1), 128)

    conv_shards = jnp.concatenate([ssd_conv_w.reshape(-1), ffn_conv_w.reshape(-1)])
    (c_all, conv_all), _ = all_to_all([c, conv_shards], [True, True], "gather_c")
    modp, cact = ada_mod(jnp.pad(c_all.reshape(NDEV, D), ((0, 8), (0, 0))), ada_w)
    (mod_in,), tok = all_to_all([modp[:, 0:NDEV].transpose(1, 0, 2)], [False], "scatter_mod")
    mod = mod_in.transpose(1, 0, 2).reshape(LAYERS, 6 * D) + ada_b

    n1 = LAYERS * 4 * 192
    scw = conv_all[:, :n1].reshape(NDEV, LAYERS, 4, 192).transpose(1, 2, 0, 3).reshape(LAYERS, 4, 1536)
    fcw = conv_all[:, n1:].reshape(NDEV, LAYERS, 3, 704).transpose(1, 2, 0, 3).reshape(LAYERS, 3, 2 * FFN)

    def weights_a(gathered, l):
        full = {n: _gathered_full(g, n) for n, g in zip(GROUP_A[1:], gathered[1:])}
        lw = {n: p[n][l] for n in names}
        lw.update(win=_win_layout(gathered[0]), wq=_wq_layout(full["w_q_b"]), wkv=_wkv_layout(full["w_kv_b"]),
                  ssd_conv_w=scw[l], ffn_conv_w=fcw[l],
                  mla_vec=_mla_vec(lw["q_a_norm"], lw["kv_a_norm"], lw["q_norm"], lw["k_norm"]))
        return lw

    def weights_b(gathered):
        full = {n: _gathered_full(g, n) for n, g in zip(GROUP_B, gathered)}
        wb = full["w_branch"]
        return dict(wba=_wba_layout(wb[0:512]), wbb=wb[512:1024], wbc=wb[1024:2048], wout=full["w_out"],
                    wup=full["ffn_up"], wdn=full["ffn_down"])

    shards = lambda group, l: [p[n][l].astype(BF16) for n in group]
    lws, saved = [None] * LAYERS, [None] * LAYERS
    st, tok = gather_start(_behind(shards(GROUP_A, 0), tok), "gather_a0")
    got, tok = gather_finish(st, tok, "gather_a0")
    h = xs
    for l in range(LAYERS):
        st, tok = gather_start(_behind(shards(GROUP_B, l), tok), f"gather_b{l}")
        lws[l] = weights_a(got, l)
        saved[l] = _fwd_a(h, lws[l], mod[l], cos2, sin2, l, tok)
        got, tok = gather_finish(st, saved[l]["yt"], f"gather_b{l}")
        lws[l].update(weights_b(got))
        if l + 1 < LAYERS:
            st, tok = gather_start(_behind(shards(GROUP_A, l + 1), tok), f"gather_a{l + 1}")
        h = _fwd_b(saved[l], lws[l], l, tok)
        if l + 1 < LAYERS:
            got, tok = gather_finish(st, h, f"gather_a{l + 1}")
    dx, lpart = loss_head(h, tgt)
    loss = lax.psum(lpart[0, 0], ("x", "y", "c"))
    tok = tok + loss * 0.0

    grads, small, parts, packs = [None] * LAYERS, [None] * LAYERS, {}, [None] * LAYERS
    to_shards = lambda g, group: [_to_shards(g[n], n) for n in group]
    nb = lambda group: [False] * len(group)
    st = None
    for l in reversed(range(LAYERS)):
        dx, cot, gb, small[l] = _bwd_b(dx, lws[l], saved[l], l, tok)
        if st is not None:
            parts[("a", l + 1)], tok, _ = exchange_wait(st, dx, f"scatter_a{l + 1}_wait")
        arrs, flags = to_shards(gb, GROUP_B), nb(GROUP_B)
        if l + 1 < LAYERS:
            arrs, flags = arrs + [_pack(small[l + 1])], flags + [True]
        st, tok = exchange_start(_behind(arrs, tok), flags, f"scatter_b{l}_start")
        dproj, ga, small[l] = _bwd_a(dx, cot, lws[l], saved[l], cos2, sin2, l, tok, small[l])
        got, tok, _ = exchange_wait(st, dproj, f"scatter_b{l}_wait")
        parts[("b", l)] = got[0:len(GROUP_B)]
        if l + 1 < LAYERS:
            packs[l + 1] = got[len(GROUP_B)]
        st, tok = exchange_start(_behind(to_shards(ga, GROUP_A), tok), nb(GROUP_A), f"scatter_a{l}_start")
        dx, small[l] = _bwd_in(dx, dproj, lws[l], saved[l], l, tok, small[l])

    dmod = jnp.stack([small[q]["ada_b"] for q in range(LAYERS)])
    st_small, tok = exchange_start(_behind([_pack(small[0]), dmod.reshape(LAYERS, NDEV, 768).transpose(1, 0, 2)], tok),
                                   [True, False], "scatter_s0_start")
    out = {}

    def big_adamw(group, tok):
        res = None
        for n in group:
            grp, idx = ("a", GROUP_A.index(n)) if n in GROUP_A else ("b", GROUP_B.index(n))
            shp = p[n].shape
            flat = lambda a: a.reshape(shp[0] * shp[1], shp[2])
            res = adamw([parts[(grp, 0)][idx], parts[(grp, 1)][idx]], flat(p[n]), flat(mom[n]), flat(var[n]),
                        f"adamw_{n}", tok)
            out[n] = [r.reshape(shp) for r in res]
        return res[0]

    g_last = big_adamw(GROUP_B, tok)
    parts[("a", 0)], tok, _ = exchange_wait(st, g_last, "scatter_a0_wait")
    (packs[0], dmod_in), _, _ = exchange_wait(st_small, tok, "scatter_s0_wait")
    big_adamw(GROUP_A, None)

    dmod16 = jnp.pad(dmod_in, ((0, 8), (0, 0), (0, 0)))
    g_ada = jnp.stack([tn_matmul(cact, dmod16[:, l], f"dw_ada{l}", out_dtype=F32) for l in range(LAYERS)])
    flat = lambda a: a.reshape(LAYERS * D, 768)
    out["ada_w"] = [r.reshape(ada_w.shape) for r in
                    adamw([flat(g_ada)[None]], flat(ada_w), flat(m_ada_w), flat(v_ada_w), "adamw_ada_w")]

    for n, pt in _unpack_parts(packs).items():
        if n in SHARDED_SMALL:
            w = SHARDED_SMALL[n]
            pt = lax.dynamic_slice_in_dim(pt, me * w, w, axis=2)
        r, c = pt.shape[1:]
        res = adamw([pt], p[n].reshape(r, c), mom[n].reshape(r, c), var[n].reshape(r, c), f"adamw_{n}")
        out[n] = [a.reshape(p[n].shape) for a in res]

    outs = [loss, dx[None]]
    for q in range(4):
        outs += [out[n][q] for n in names]
    return tuple(outs)
```

```python
import functools
import math

import jax
import jax.numpy as jnp
from jax import lax
from jax.experimental import pallas as pl
from jax.experimental.pallas import tpu as pltpu

F32, BF16 = jnp.float32, jnp.bfloat16
EPS = 1e-6
D = 1024
NDEV = 8
LAYERS = 2
HEADS = 8
FFN = 2816
FFN_TILE = 1408
FFN_NT = FFN // FFN_TILE
ATT_SCALE = 96 ** -0.5
ROPE_THETA = 10000.0
LR, B1, B2, ADAM_EPS, WD, STEP = 0.001, 0.9, 0.999, 1e-08, 0.01, 10

O_G, O_XS, O_Z, O_PU, O_BC, O_CKV, O_KR, O_KRS, O_DT, O_QL = 0, 3072, 4096, 5120, 5632, 6144, 6400, 6528, 6656, 6912
NPROJ = 7296
CONST = dict(pipeline_mode=pl.Buffered(1))


def _pick(n, cap, mult=128):
    if n <= cap:
        return n
    best = None
    for t in range(mult, cap + 1, mult):
        if n % t == 0:
            best = t
    assert best is not None, (n, cap, mult)
    return best


def _sig(x):
    return 1.0 / (1.0 + jnp.exp(-x))


def _rms(x, w, n):
    return x * lax.rsqrt(jnp.sum(x * x, axis=-1, keepdims=True) / n + EPS) * w


def _raw(a, b, dims):
    return lax.dot_general(a.astype(BF16), b.astype(BF16), dims, preferred_element_type=F32)


_NN = (((1,), (0,)), ((), ()))
_NT = (((1,), (1,)), ((), ()))
_TN = (((0,), (0,)), ((), ()))
_BNN = (((2,), (1,)), ((0,), (0,)))
_BNT = (((2,), (2,)), ((0,), (0,)))
_BTN = (((1,), (1,)), ((0,), (0,)))


@jax.custom_vjp
def mm_nn(a, b):
    return _raw(a, b, _NN)


mm_nn.defvjp(lambda a, b: (_raw(a, b, _NN), (a, b)),
             lambda r, g: (_raw(g, r[1], _NT), _raw(r[0], g, _TN)))


@jax.custom_vjp
def mm_nc(a, b):
    return _raw(a, b, _NN)


mm_nc.defvjp(lambda a, b: (_raw(a, b, _NN), b),
             lambda b, g: (_raw(g, b, _NT), jnp.zeros_like(b)))


@jax.custom_vjp
def mm_nt(a, b):
    return _raw(a, b, _NT)


mm_nt.defvjp(lambda a, b: (_raw(a, b, _NT), (a, b)),
             lambda r, g: (_raw(g, r[1], _NN), _raw(g, r[0], _TN)))


@jax.custom_vjp
def bmm_nn(a, b):
    return _raw(a, b, _BNN)


bmm_nn.defvjp(lambda a, b: (_raw(a, b, _BNN), (a, b)),
              lambda r, g: (_raw(g, r[1], _BNT), _raw(r[0], g, _BTN)))


@jax.custom_vjp
def bmm_nt(a, b):
    return _raw(a, b, _BNT)


bmm_nt.defvjp(lambda a, b: (_raw(a, b, _BNT), (a, b)),
              lambda r, g: (_raw(g, r[1], _BNN), _raw(g, r[0], _BTN)))


@jax.custom_vjp
def softplus(x):
    t = jnp.exp(-jnp.abs(x))
    u = 1.0 + t
    one = u == 1.0
    l1p = jnp.where(one, t, jnp.log(u) * (t / jnp.where(one, 1.0, u - 1.0)))
    return jnp.maximum(x, 0.0) + l1p


softplus.defvjp(lambda x: (softplus(x), x), lambda x, g: (g * _sig(x),))


def _params(*sem):
    return pltpu.CompilerParams(dimension_semantics=sem, vmem_limit_bytes=56 * 1024 * 1024)


def all_to_all(arrs, bcast, name):
    n = len(arrs)
    out_shapes = [jax.ShapeDtypeStruct(((NDEV,) + a.shape) if b else a.shape, a.dtype) for a, b in zip(arrs, bcast)]

    def body(*refs):
        ins, outs, token = refs[:n], refs[n:2 * n], refs[2 * n]
        send_sems, recv_sems, local_sems = refs[2 * n + 1:]
        me, remote = _exchange_copies(ins, outs, bcast, send_sems, recv_sems)
        local = [pltpu.make_async_copy(ins[j] if bcast[j] else ins[j].at[me], outs[j].at[me], local_sems.at[j])
                 for j in range(n)]
        for cp in local + remote:
            cp.start()
        for cp in remote + local:
            cp.wait()
        token[...] = jnp.zeros_like(token)

    any_spec = pl.BlockSpec(memory_space=pl.ANY)
    res = pl.pallas_call(
        body, name=name, out_shape=out_shapes + [jax.ShapeDtypeStruct((8, 128), F32)], in_specs=[any_spec] * n,
        out_specs=[any_spec] * n + [pl.BlockSpec(memory_space=pltpu.VMEM)],
        scratch_shapes=[pltpu.SemaphoreType.DMA((7 * n,)), pltpu.SemaphoreType.DMA((7 * n,)),
                        pltpu.SemaphoreType.DMA((n,))],
        compiler_params=pltpu.CompilerParams(has_side_effects=True),
    )(*arrs)
    return res[:n], res[n]


def _peers():
    x, y, c = lax.axis_index("x"), lax.axis_index("y"), lax.axis_index("c")
    out = []
    for k in range(1, NDEV):
        px, py, pc = x ^ ((k >> 2) & 1), y ^ ((k >> 1) & 1), c ^ (k & 1)
        out.append(((px, py, pc), 4 * px + 2 * py + pc))
    return 4 * x + 2 * y + c, out


COPIES = {"all": 7, "chips": 3, "pass": 4}


def _exchange_copies(ins, lands, bcast, send_sems, recv_sems, mode="all"):
    x, y, c = lax.axis_index("x"), lax.axis_index("y"), lax.axis_index("c")
    me = 4 * x + 2 * y + c
    n, copies = len(ins), []

    def add(q, j, src, dst, dev):
        copies.append(pltpu.make_async_remote_copy(
            src_ref=src, dst_ref=dst, send_sem=send_sems.at[q * n + j], recv_sem=recv_sems.at[q * n + j],
            device_id=dev, device_id_type=pl.DeviceIdType.MESH))

    if mode == "pass":
        for q in range(4):
            slot = 4 * (x ^ (q >> 1)) + 2 * (y ^ (q & 1)) + c
            for j in range(n):
                add(q, j, ins[j] if q == 0 else lands[j].at[slot], lands[j].at[slot], (x, y, 1 - c))
        return me, copies
    for q, k in enumerate(range(1, NDEV) if mode == "all" else (2, 4, 6)):
        px, py, pc = x ^ ((k >> 2) & 1), y ^ ((k >> 1) & 1), c ^ (k & 1)
        for j in range(n):
            add(q, j, ins[j] if bcast[j] else ins[j].at[4 * px + 2 * py + pc], lands[j].at[me], (px, py, pc))
    return me, copies


_HBM = pl.BlockSpec(memory_space=pltpu.HBM)
_SEM = pl.BlockSpec(memory_space=pltpu.SEMAPHORE)
_EFFECT = pltpu.SideEffectType.DATAFLOW_SIDE_EFFECTING


def exchange_start(arrs, bcast, name, mode="all", lands=None):
    n, ncp = len(arrs), COPIES[mode] * len(arrs)
    land_shapes = [((NDEV,) + a.shape) if b else a.shape for a, b in zip(arrs, bcast)]
    if lands is None:
        lands = [lax.empty(s_, a.dtype) for s_, a in zip(land_shapes, arrs)]

    def body(*refs):
        in_refs, land_refs = refs[:n], refs[n:2 * n]
        send_sems, recv_sems = refs[2 * n], refs[2 * n + 1]
        token = refs[-1]
        _, copies = _exchange_copies(in_refs, land_refs, bcast, send_sems, recv_sems, mode)
        for cp in copies:
            cp.start()
        token[...] = jnp.zeros_like(token)

    hbm = lambda shp, a: pltpu.HBM(shp, a.dtype)
    res = pl.pallas_call(
        body, name=name,
        out_shape=[pltpu.SemaphoreType.DMA((ncp,)), pltpu.SemaphoreType.DMA((ncp,))]
                  + [hbm(a.shape, a) for a in arrs] + [hbm(s_, a) for s_, a in zip(land_shapes, arrs)]
                  + [jax.ShapeDtypeStruct((8, 128), F32)],
        in_specs=[_HBM] * (2 * n), out_specs=[_SEM, _SEM] + [_HBM] * (2 * n) + [pl.BlockSpec(memory_space=pltpu.VMEM)],
        input_output_aliases={i: 2 + i for i in range(2 * n)},
        compiler_params=pltpu.CompilerParams(has_side_effects=_EFFECT),
    )(*[pltpu.with_memory_space_constraint(a, pltpu.HBM) for a in arrs],
      *[pltpu.with_memory_space_constraint(a, pltpu.HBM) for a in lands])
    return (res[0], res[1], res[2:2 + n], res[2 + n:2 + 2 * n], tuple(bcast), mode), res[-1]


def exchange_wait(state, after, name):
    send_sems, recv_sems, ins, lands, bcast, mode = state
    n = len(ins)

    def body(*refs):
        in_refs, land_refs = refs[:n], refs[n:2 * n]
        s_sems, r_sems = refs[2 * n], refs[2 * n + 1]
        token = refs[-1]
        _, copies = _exchange_copies(in_refs, land_refs, bcast, s_sems, r_sems, mode)
        for cp in copies:
            cp.wait_send()
            cp.wait_recv()
        token[...] = jnp.zeros_like(token)

    res = pl.pallas_call(
        body, name=name,
        out_shape=[pltpu.HBM(a.shape, a.dtype) for a in ins] + [pltpu.HBM(a.shape, a.dtype) for a in lands]
                  + [jax.ShapeDtypeStruct((8, 128), F32)],
        in_specs=[_HBM] * (2 * n) + [_SEM, _SEM, pl.BlockSpec(memory_space=pl.ANY)],
        out_specs=[_HBM] * (2 * n) + [pl.BlockSpec(memory_space=pltpu.VMEM)],
        input_output_aliases={i: i for i in range(2 * n)},
        compiler_params=pltpu.CompilerParams(has_side_effects=_EFFECT),
    )(*ins, *lands, send_sems, recv_sems, after)
    if mode == "chips":
        return list(res[n:2 * n]), res[-1], list(res[:n])
    me = 4 * lax.axis_index("x") + 2 * lax.axis_index("y") + lax.axis_index("c")
    got = []
    for j in range(n):
        own = res[j][None] if bcast[j] else lax.dynamic_index_in_dim(res[j], me, 0, keepdims=True)
        got.append(lax.dynamic_update_slice_in_dim(res[n + j], own, me, axis=0))
    return got, res[-1], list(res[:n])


def gather_start(shards, name):
    return exchange_start(shards, [True] * len(shards), name + "_chips_start", mode="chips")


def gather_finish(state, after, name):
    lands, _, sent = exchange_wait(state, after, name + "_chips_wait")
    state, tok = exchange_start(sent, [True] * len(sent), name + "_pass_start", mode="pass", lands=lands)
    got, tok, _ = exchange_wait(state, tok, name + "_pass_wait")
    return got, tok


def norm_proj_fwd(x, vec, w, name):
    s, n = x.shape[0], w.shape[1]
    tr, tn = _pick(s, 512), _pick(n, 2560)
    ni, jdt, odt = s // tr, O_DT // tn, O_DT % tn

    def body(x_ref, v_ref, w_ref, o_ref, h_ref, dt_ref, h_scr):
        j, i = pl.program_id(0), pl.program_id(1)
        rows = pl.ds(pl.multiple_of(i * tr, tr), tr)

        @pl.when(j == 0)
        def _():
            h = _rms(x_ref[...], v_ref[0:1, :], D) * (1.0 + v_ref[2:3, :]) + v_ref[1:2, :]
            h_scr[rows, :] = h.astype(BF16)
            h_ref[...] = h.astype(BF16)
        res = jnp.dot(h_scr[rows, :], w_ref[...], preferred_element_type=F32)
        o_ref[...] = res

        @pl.when(j == jdt)
        def _():
            dt_ref[...] = res[:, odt:odt + 128]

    first = lambda j, i: (jnp.where(j == 0, i, ni - 1), 0)
    dtix = lambda j, i: (jnp.where(j < jdt, 0, jnp.where(j == jdt, i, ni - 1)), 0)
    return pl.pallas_call(
        body, name=name, grid=(n // tn, ni),
        in_specs=[pl.BlockSpec((tr, D), first), pl.BlockSpec((8, D), lambda j, i: (0, 0)),
                  pl.BlockSpec((D, tn), lambda j, i: (0, j))],
        out_specs=[pl.BlockSpec((tr, tn), lambda j, i: (i, j)), pl.BlockSpec((tr, D), first),
                   pl.BlockSpec((tr, 128), dtix)],
        out_shape=[jax.ShapeDtypeStruct((s, n), F32), jax.ShapeDtypeStruct((s, D), BF16),
                   jax.ShapeDtypeStruct((s, 128), F32)],
        scratch_shapes=[pltpu.VMEM((s, D), BF16)],
        compiler_params=_params("arbitrary", "arbitrary"),
    )(x, vec, w)


def _col_tiles(arr, cap):
    if arr.ndim == 2:
        n = arr.shape[1]
        t = _pick(n, cap)
        return n, t, lambda rows, ix: pl.BlockSpec((rows, t), lambda *g: ix(*g))
    width = arr.shape[2]
    t = _pick(width, cap)
    per = width // t

    def spec(rows, ix):
        def index(*g):
            r, j = ix(*g)
            return (j // per, r, j % per)
        return pl.BlockSpec((None, rows, t), index)
    return arr.shape[0] * width, t, spec


def norm_proj_bwd(x, vec, dp, w, dx_in, aux, name):
    s = x.shape[0]
    tr = _pick(s, 512)
    n, tk, dp_spec = _col_tiles(dp, 2560)
    nk, has_aux = n // tk, aux is not None

    def body(*refs):
        if has_aux:
            x_ref, v_ref, dp_ref, w_ref, dxin_ref, aux_ref, dx_ref, dv_ref, acc = refs
        else:
            x_ref, v_ref, dp_ref, w_ref, dxin_ref, dx_ref, dv_ref, acc = refs
        k, i = pl.program_id(0), pl.program_id(1)
        rows = pl.ds(pl.multiple_of(i * tr, tr), tr)
        part = _raw(dp_ref[...], w_ref[...], _NT)

        @pl.when(k == 0)
        def _():
            acc[rows, :] = part

        @pl.when(k > 0)
        def _():
            acc[rows, :] += part

        @pl.when(k == nk - 1)
        def _():
            f = lambda xx, nw, sh, sc: _rms(xx, nw, D) * (1.0 + sc) + sh
            _, vjp = jax.vjp(f, x_ref[...], v_ref[0:1, :], v_ref[1:2, :], v_ref[2:3, :])
            dx, dnw, dsh, dsc = vjp(acc[rows, :])
            dx_ref[...] = dxin_ref[...] + dx

            @pl.when(i == 0)
            def _():
                dv_ref[...] = jnp.zeros_like(dv_ref)

            dv_ref[0:1, :] += dnw
            dv_ref[1:2, :] += dsh
            dv_ref[2:3, :] += dsc
            if has_aux:
                dv_ref[3:4, :] += jnp.sum(dxin_ref[...] * aux_ref[...], axis=0, keepdims=True)

    row = pl.BlockSpec((tr, D), lambda k, i: (jnp.where(k == nk - 1, i, 0), 0))
    in_specs = [row, pl.BlockSpec((8, D), lambda k, i: (0, 0)), dp_spec(tr, lambda k, i: (i, k)),
                pl.BlockSpec((D, tk), lambda k, i: (0, k)), row] + ([row] if has_aux else [])
    args = [x, vec, dp, w, dx_in] + ([aux] if has_aux else [])
    return pl.pallas_call(
        body, name=name, grid=(nk, s // tr), in_specs=in_specs,
        out_specs=[row, pl.BlockSpec((8, D), lambda k, i: (0, 0))],
        out_shape=[jax.ShapeDtypeStruct((s, D), F32), jax.ShapeDtypeStruct((8, D), F32)],
        scratch_shapes=[pltpu.VMEM((s, D), F32)],
        compiler_params=_params("arbitrary", "arbitrary"),
    )(*args)


def tn_matmul(a, b, name, scale=None, out_dtype=None):
    out_dtype = BF16 if out_dtype is None else out_dtype
    s, m = a.shape
    ts, tm = _pick(s, 512, 16), _pick(m, 1408)
    n, tn, b_spec = _col_tiles(b, 2560)
    ns, has_scale = s // ts, scale is not None

    def body(*refs):
        if has_scale:
            a_ref, b_ref, sc_ref, o_ref, acc = refs
        else:
            a_ref, b_ref, o_ref, acc = refs
        k = pl.program_id(2)

        @pl.when(k == 0)
        def _():
            acc[...] = jnp.zeros_like(acc)

        acc[...] += _raw(a_ref[...], b_ref[...], _TN)

        @pl.when(k == ns - 1)
        def _():
            o_ref[...] = (acc[...] * sc_ref[...] if has_scale else acc[...]).astype(out_dtype)

    in_specs = [pl.BlockSpec((ts, tm), lambda i, j, k: (k, i)), b_spec(ts, lambda i, j, k: (k, j))]
    if has_scale:
        in_specs.append(pl.BlockSpec((1, tn), lambda i, j, k: (0, j)))
    return pl.pallas_call(
        body, name=name, grid=(m // tm, n // tn, ns), in_specs=in_specs,
        out_specs=pl.BlockSpec((tm, tn), lambda i, j, k: (i, j)),
        out_shape=jax.ShapeDtypeStruct((m, n), out_dtype),
        scratch_shapes=[pltpu.VMEM((tm, tn), F32)],
        compiler_params=_params("arbitrary", "arbitrary", "arbitrary"),
    )(*([a, b] + ([scale] if has_scale else [])))


def ada_mod(c16, w):
    ncol = w.shape[2]

    def body(c_ref, w_ref, o_ref, a_ref):
        cc = c_ref[...]
        act = cc * _sig(cc)
        a_ref[...] = act
        o_ref[...] = _raw(act, w_ref[...], _NN)

    return pl.pallas_call(
        body, name="ada_mod", grid=(LAYERS,),
        in_specs=[pl.BlockSpec((16, D), lambda l: (0, 0)), pl.BlockSpec((None, D, ncol), lambda l: (l, 0, 0))],
        out_specs=[pl.BlockSpec((None, 16, ncol), lambda l: (l, 0, 0)), pl.BlockSpec((16, D), lambda l: (0, 0))],
        out_shape=[jax.ShapeDtypeStruct((LAYERS, 16, ncol), F32), jax.ShapeDtypeStruct((16, D), F32)],
        compiler_params=_params("arbitrary"),
    )(c16, w)


def _mla_shared(q_lat, c_kv, kr, krs, qa_w, kva_w, kr_w, krs_w, cos2, sin2):
    qn = _rms(q_lat, qa_w, 384.0)
    kvn = _rms(c_kv, kva_w, 256.0)
    rk = lax.rsqrt(jnp.sum(kr * kr, axis=-1, keepdims=True) / 32.0 + EPS)
    krope = rk * (kr * kr_w * cos2 + krs * krs_w * sin2)
    return qn, kvn, krope


def _mla_head(qn, kvn, wqn, wqr, wqrs, wkn, wv, qn_w, qr_w, qrs_w, kn_w, cos2, sin2):
    qnope = _rms(mm_nn(qn, wqn), qn_w, 64.0)
    qr, qrs = mm_nn(qn, wqr), mm_nn(qn, wqrs)
    rq = lax.rsqrt(jnp.sum(qr * qr, axis=-1, keepdims=True) / 32.0 + EPS)
    qrope = rq * (qr * qr_w * cos2 + qrs * qrs_w * sin2)
    knope = _rms(mm_nn(kvn, wkn), kn_w, 64.0)
    return qnope, qrope, knope, mm_nn(kvn, wv)


def _mla_vec_pieces(v_ref):
    return ((v_ref[0:1, 0:384], v_ref[1:2, 0:256], v_ref[3:4, 128:256], v_ref[3:4, 256:384]),
            (v_ref[2:3, 0:128], v_ref[2:3, 128:256], v_ref[2:3, 256:384], v_ref[3:4, 0:128]))


def _mla_in_specs(tr):
    return [pl.BlockSpec((tr, 384), lambda i: (i, O_QL // 384)), pl.BlockSpec((tr, 256), lambda i: (i, O_CKV // 256)),
            pl.BlockSpec((tr, 128), lambda i: (i, O_KR // 128)), pl.BlockSpec((tr, 128), lambda i: (i, O_KRS // 128)),
            pl.BlockSpec((HEADS, 384, 384), lambda i: (0, 0, 0), **CONST),
            pl.BlockSpec((HEADS, 256, 256), lambda i: (0, 0, 0), **CONST),
            pl.BlockSpec((8, 512), lambda i: (0, 0)),
            pl.BlockSpec((tr, 128), lambda i: (i, 0)), pl.BlockSpec((tr, 128), lambda i: (i, 0))]


def mla_pre_fwd(proj, wq, wkv, vec, cos2, sin2, name):
    s = proj.shape[0]
    tr = _pick(s, 256)

    def body(ql_ref, ckv_ref, kr_ref, krs_ref, wq_ref, wkv_ref, v_ref, cos_ref, sin_ref, q_out, k_out, v_out):
        vshared, vhead = _mla_vec_pieces(v_ref)
        cos2_, sin2_ = cos_ref[...], sin_ref[...]
        qlat_n, kv_n, krope = _mla_shared(ql_ref[...], ckv_ref[...], kr_ref[...], krs_ref[...], *vshared, cos2_, sin2_)
        qlat_n, kv_n, krope = qlat_n.astype(BF16), kv_n.astype(BF16), krope.astype(BF16)
        for h in range(HEADS):
            ws = (wq_ref[h, :, 0:128], wq_ref[h, :, 128:256], wq_ref[h, :, 256:384],
                  wkv_ref[h, :, 0:128], wkv_ref[h, :, 128:256])
            qn, qr, kn, v = _mla_head(qlat_n, kv_n, *ws, *vhead, cos2_, sin2_)
            q_out[h, :, 0:128] = qn.astype(BF16)
            q_out[h, :, 128:256] = qr.astype(BF16)
            k_out[h, :, 0:128] = kn.astype(BF16)
            k_out[h, :, 128:256] = krope
            v_out[h] = v.astype(BF16)

    return pl.pallas_call(
        body, name=name, grid=(s // tr,), in_specs=_mla_in_specs(tr),
        out_specs=[pl.BlockSpec((HEADS, tr, 256), lambda i: (0, i, 0)), pl.BlockSpec((HEADS, tr, 256), lambda i: (0, i, 0)),
                   pl.BlockSpec((HEADS, tr, 128), lambda i: (0, i, 0))],
        out_shape=[jax.ShapeDtypeStruct((HEADS, s, 256), BF16), jax.ShapeDtypeStruct((HEADS, s, 256), BF16),
                   jax.ShapeDtypeStruct((HEADS, s, 128), BF16)],
        compiler_params=_params("arbitrary"),
    )(proj, proj, proj, proj, wq, wkv, vec, cos2, sin2)


def mla_pre_bwd(proj, wq, wkv, vec, cos2, sin2, dq, dk, dv, name):
    s = proj.shape[0]
    tr = _pick(s, 256)

    def body(ql_ref, ckv_ref, kr_ref, krs_ref, wq_ref, wkv_ref, v_ref, cos_ref, sin_ref, dq_ref, dk_ref, dv_ref,
             dql_out, dckv_out, dkr_out, dkrs_out, dwq_out, dwkv_out, dvec_out):
        @pl.when(pl.program_id(0) == 0)
        def _():
            dwq_out[...] = jnp.zeros_like(dwq_out)
            dwkv_out[...] = jnp.zeros_like(dwkv_out)
            dvec_out[...] = jnp.zeros_like(dvec_out)

        vshared, vhead = _mla_vec_pieces(v_ref)
        cos2_, sin2_ = cos_ref[...], sin_ref[...]
        fs = lambda *a: _mla_shared(*a, cos2_, sin2_)
        (qlat_n, kv_n, _), vjp_shared = jax.vjp(fs, ql_ref[...], ckv_ref[...], kr_ref[...], krs_ref[...], *vshared)

        def head(h, carry):
            wq_h, wkv_h = wq_ref[h].astype(F32), wkv_ref[h].astype(F32)
            ws = (wq_h[:, 0:128], wq_h[:, 128:256], wq_h[:, 256:384], wkv_h[:, 0:128], wkv_h[:, 128:256])
            f = lambda *a: _mla_head(*a, cos2_, sin2_)
            _, vjp = jax.vjp(f, qlat_n, kv_n, *ws, *vhead)
            dq_h, dk_h = dq_ref[h], dk_ref[h]
            g = vjp((dq_h[:, 0:128], dq_h[:, 128:256], dk_h[:, 0:128], dv_ref[h]))
            dwq_out[h, :, 0:128] += g[2]
            dwq_out[h, :, 128:256] += g[3]
            dwq_out[h, :, 256:384] += g[4]
            dwkv_out[h, :, 0:128] += g[5]
            dwkv_out[h, :, 128:256] += g[6]
            dvec_out[2:3, 0:128] += g[7]
            dvec_out[2:3, 128:256] += g[8]
            dvec_out[2:3, 256:384] += g[9]
            dvec_out[3:4, 0:128] += g[10]
            return carry[0] + g[0], carry[1] + g[1], carry[2] + dk_h[:, 128:256]

        zero = lambda w: jnp.zeros((tr, w), F32)
        dqn, dkvn, dkrope = lax.fori_loop(0, HEADS, head, (zero(384), zero(256), zero(128)))
        g = vjp_shared((dqn, dkvn, dkrope))
        dql_out[...] = g[0].astype(BF16)
        dckv_out[...] = g[1].astype(BF16)
        dkr_out[...] = g[2].astype(BF16)
        dkrs_out[...] = g[3].astype(BF16)
        dvec_out[0:1, 0:384] += g[4]
        dvec_out[1:2, 0:256] += g[5]
        dvec_out[3:4, 128:256] += g[6]
        dvec_out[3:4, 256:384] += g[7]

    hb = lambda w: pl.BlockSpec((HEADS, tr, w), lambda i: (0, i, 0))
    return pl.pallas_call(
        body, name=name, grid=(s // tr,), in_specs=_mla_in_specs(tr) + [hb(256), hb(256), hb(128)],
        out_specs=[pl.BlockSpec((tr, 384), lambda i: (i, 0)), pl.BlockSpec((tr, 256), lambda i: (i, 0)),
                   pl.BlockSpec((tr, 128), lambda i: (i, 0)), pl.BlockSpec((tr, 128), lambda i: (i, 0)),
                   pl.BlockSpec((HEADS, 384, 384), lambda i: (0, 0, 0)), pl.BlockSpec((HEADS, 256, 256), lambda i: (0, 0, 0)),
                   pl.BlockSpec((8, 512), lambda i: (0, 0))],
        out_shape=[jax.ShapeDtypeStruct((s, 384), BF16), jax.ShapeDtypeStruct((s, 256), BF16),
                   jax.ShapeDtypeStruct((s, 128), BF16), jax.ShapeDtypeStruct((s, 128), BF16),
                   jax.ShapeDtypeStruct((HEADS, 384, 384), F32), jax.ShapeDtypeStruct((HEADS, 256, 256), F32),
                   jax.ShapeDtypeStruct((8, 512), F32)],
        compiler_params=_params("arbitrary"),
    )(proj, proj, proj, proj, wq, wkv, vec, cos2, sin2, dq, dk, dv)


def _att_probs(q, kk, i, tq):
    sc = _raw(q, kk, _NT) * ATT_SCALE
    rows = lax.broadcasted_iota(jnp.int32, sc.shape, 0) + i * tq
    cols = lax.broadcasted_iota(jnp.int32, sc.shape, 1)
    sc = jnp.where(cols <= rows, sc, -jnp.inf)
    e = jnp.exp(sc - jnp.max(sc, axis=-1, keepdims=True))
    return e / jnp.sum(e, axis=-1, keepdims=True)


def mla_attn_fwd(q, k, v, name):
    s = q.shape[1]
    tq = _pick(s, 256)

    def body(q_ref, k_ref, v_ref, o_ref):
        for i in range(s // tq):
            n = (i + 1) * tq
            p = _att_probs(q_ref[i * tq:n, :], k_ref[0:n, :], i, tq)
            o_ref[i * tq:n, :] = _raw(p, v_ref[0:n, :], _NN)

    hs = lambda w: pl.BlockSpec((None, s, w), lambda h: (h, 0, 0))
    return pl.pallas_call(
        body, name=name, grid=(HEADS,), in_specs=[hs(256), hs(256), hs(128)],
        out_specs=pl.BlockSpec((s, 128), lambda h: (0, h)),
        out_shape=jax.ShapeDtypeStruct((s, HEADS * 128), F32),
        compiler_params=_params("arbitrary"),
    )(q, k, v)


def mla_attn_bwd(q, k, v, do, name):
    s = q.shape[1]
    tq = _pick(s, 256)

    def body(q_ref, k_ref, v_ref, do_ref, dq_ref, dk_ref, dv_ref):
        dk_ref[...] = jnp.zeros_like(dk_ref)
        dv_ref[...] = jnp.zeros_like(dv_ref)
        for i in range(s // tq):
            n = (i + 1) * tq
            qq, kk, vv = q_ref[i * tq:n, :], k_ref[0:n, :], v_ref[0:n, :]
            p = _att_probs(qq, kk, i, tq)
            o = _raw(p, vv, _NN)
            dout = do_ref[i * tq:n, :]
            delta = jnp.sum(dout * o, axis=-1, keepdims=True)
            dp = _raw(dout, vv, _NT)
            ds = p * (dp - delta) * ATT_SCALE
            dq_ref[i * tq:n, :] = _raw(ds, kk, _NN)
            dk_ref[0:n, :] += _raw(ds, qq, _TN)
            dv_ref[0:n, :] += _raw(p, dout, _TN)

    hs = lambda w: pl.BlockSpec((None, s, w), lambda h: (h, 0, 0))
    return pl.pallas_call(
        body, name=name, grid=(HEADS,),
        in_specs=[hs(256), hs(256), hs(128), pl.BlockSpec((s, 128), lambda h: (0, h))],
        out_specs=[hs(256), hs(256), hs(128)],
        out_shape=[jax.ShapeDtypeStruct((HEADS, s, 256), F32), jax.ShapeDtypeStruct((HEADS, s, 256), F32),
                   jax.ShapeDtypeStruct((HEADS, s, 128), F32)],
        compiler_params=_params("arbitrary"),
    )(q, k, v, do)


def _pool_windows(u, pad, s, g):
    pad[0:16, :] = jnp.zeros((16, 128), F32)
    cur, sel = u, None
    for j, k in enumerate((1, 2, 4, 8)):
        pad[16:16 + s, :] = cur
        cur = cur + pad[16 - k:16 - k + s, :]
        sel = cur if sel is None else jnp.where(g == j, cur, sel)
    return sel


def _pool_count(s, g):
    t = lax.broadcasted_iota(jnp.int32, (s, 1), 0)
    return jnp.minimum(t + 1, 2 << g).astype(F32)


def pool_fwd(proj, pw, ps, name):
    s = proj.shape[0]

    def body(u_ref, w_ref, s_ref, o_ref, pad):
        g = pl.program_id(0)
        u = u_ref[...]
        pooled = _pool_windows(u, pad, s, g) / _pool_count(s, g) - u
        o_ref[...] = _raw(pooled, w_ref[...], _NN) * s_ref[...]

    return pl.pallas_call(
        body, name=name, grid=(4,),
        in_specs=[pl.BlockSpec((s, 128), lambda g: (0, O_PU // 128 + g)), pl.BlockSpec((None, 128, 128), lambda g: (g, 0, 0)),
                  pl.BlockSpec((1, 128), lambda g: (0, g))],
        out_specs=pl.BlockSpec((s, 128), lambda g: (0, g)),
        out_shape=jax.ShapeDtypeStruct((s, 512), F32),
        scratch_shapes=[pltpu.VMEM((s + 16, 128), F32)],
        compiler_params=_params("arbitrary"),
    )(proj, pw, ps)


def pool_bwd(proj, pw, ps, do, name):
    s = proj.shape[0]

    def body(u_ref, w_ref, s_ref, do_ref, du_ref, dw_ref, ds_ref, pad):
        g = pl.program_id(0)
        u, w, dout = u_ref[...], w_ref[...], do_ref[...]
        cnt = _pool_count(s, g)
        pooled = _pool_windows(u, pad, s, g) / cnt - u
        mixed = _raw(pooled, w, _NN)
        ds_ref[...] = jnp.sum(dout * mixed, axis=0, keepdims=True)
        dmixed = dout * s_ref[...]
        dw_ref[...] = _raw(pooled, dmixed, _TN)
        dpooled = _raw(dmixed, w, _NT)
        dsel = dpooled / cnt
        pad[s:s + 16, :] = jnp.zeros((16, 128), F32)
        cur = jnp.where(g == 3, dsel, 0.0)
        for j, k in ((2, 8), (1, 4), (0, 2)):
            pad[0:s, :] = cur
            cur = cur + pad[k:k + s, :] + jnp.where(g == j, dsel, 0.0)
        pad[0:s, :] = cur
        cur = cur + pad[1:1 + s, :]
        du_ref[...] = (cur - dpooled).astype(BF16)

    return pl.pallas_call(
        body, name=name, grid=(4,),
        in_specs=[pl.BlockSpec((s, 128), lambda g: (0, O_PU // 128 + g)), pl.BlockSpec((None, 128, 128), lambda g: (g, 0, 0)),
                  pl.BlockSpec((1, 128), lambda g: (0, g)), pl.BlockSpec((s, 128), lambda g: (0, g))],
        out_specs=[pl.BlockSpec((s, 128), lambda g: (0, g)), pl.BlockSpec((None, 128, 128), lambda g: (g, 0, 0)),
                   pl.BlockSpec((1, 128), lambda g: (0, g))],
        out_shape=[jax.ShapeDtypeStruct((s, 512), BF16), jax.ShapeDtypeStruct((4, 128, 128), F32),
                   jax.ShapeDtypeStruct((1, 512), F32)],
        scratch_shapes=[pltpu.VMEM((s + 16, 128), F32)],
        compiler_params=_params("arbitrary"),
    )(proj, pw, ps, do)


def _xbc_col(i):
    return jnp.where(i < 2, O_XS // 512 + i, O_BC // 512)


def conv_fwd(proj, cw, cb, name):
    s = proj.shape[0]

    def body(x_ref, w_ref, b_ref, o_ref, t_ref, pad):
        pad[0:8, :] = jnp.zeros((8, 512), F32)
        pad[8:8 + s, :] = x_ref[...]
        y = b_ref[...] + sum(w_ref[k:k + 1, :] * pad[5 + k:5 + k + s, :] for k in range(4))
        act = y * _sig(y)
        o_ref[...] = act

        @pl.when(pl.program_id(0) < 2)
        def _():
            t_ref[...] = act.T

    return pl.pallas_call(
        body, name=name, grid=(3,),
        in_specs=[pl.BlockSpec((s, 512), lambda i: (0, _xbc_col(i))), pl.BlockSpec((4, 512), lambda i: (0, i)),
                  pl.BlockSpec((1, 512), lambda i: (0, i))],
        out_specs=[pl.BlockSpec((s, 512), lambda i: (0, i)), pl.BlockSpec((512, s), lambda i: (jnp.minimum(i, 1), 0))],
        out_shape=[jax.ShapeDtypeStruct((s, 1536), F32), jax.ShapeDtypeStruct((D, s), F32)],
        scratch_shapes=[pltpu.VMEM((s + 8, 512), F32)],
        compiler_params=_params("arbitrary"),
    )(proj, cw, cb)


def conv_bwd(proj, cw, cb, dxt, dbm, dcm, name):
    s = proj.shape[0]

    def body(x_ref, w_ref, b_ref, dxt_ref, dbm_ref, dcm_ref, dx_ref, dw_ref, db_ref, pad, pad2):
        pad[0:8, :] = jnp.zeros((8, 512), F32)
        pad[8:8 + s, :] = x_ref[...]
        y = b_ref[...] + sum(w_ref[k:k + 1, :] * pad[5 + k:5 + k + s, :] for k in range(4))
        sg = _sig(y)

        @pl.when(pl.program_id(0) < 2)
        def _():
            pad2[0:s, :] = dxt_ref[...].T

        @pl.when(pl.program_id(0) == 2)
        def _():
            pad2[0:s, 0:256] = dbm_ref[...]
            pad2[0:s, 256:512] = dcm_ref[...]

        dy = pad2[0:s, :] * (sg * (1.0 + y * (1.0 - sg)))
        db_ref[...] = jnp.sum(dy, axis=0, keepdims=True)
        for k in range(4):
            dw_ref[k:k + 1, :] = jnp.sum(dy * pad[5 + k:5 + k + s, :], axis=0, keepdims=True)
        pad2[s:s + 8, :] = jnp.zeros((8, 512), F32)
        pad2[0:s, :] = dy
        dx_ref[...] = sum(w_ref[k:k + 1, :] * pad2[3 - k:3 - k + s, :] for k in range(4)).astype(BF16)

    return pl.pallas_call(
        body, name=name, grid=(3,),
        in_specs=[pl.BlockSpec((s, 512), lambda i: (0, _xbc_col(i))), pl.BlockSpec((4, 512), lambda i: (0, i)),
                  pl.BlockSpec((1, 512), lambda i: (0, i)), pl.BlockSpec((512, s), lambda i: (jnp.minimum(i, 1), 0)),
                  pl.BlockSpec((s, 256), lambda i: (0, 0)), pl.BlockSpec((s, 256), lambda i: (0, 0))],
        out_specs=[pl.BlockSpec((s, 512), lambda i: (0, i)), pl.BlockSpec((4, 512), lambda i: (0, i)),
                   pl.BlockSpec((1, 512), lambda i: (0, i))],
        out_shape=[jax.ShapeDtypeStruct((s, 1536), BF16), jax.ShapeDtypeStruct((4, 1536), F32),
                   jax.ShapeDtypeStruct((1, 1536), F32)],
        scratch_shapes=[pltpu.VMEM((s + 8, 512), F32), pltpu.VMEM((s + 8, 512), F32)],
        compiler_params=_params("arbitrary"),
    )(proj, cw, cb, dxt, dbm, dcm)


def _ssd_chunk(xt, dtr, dtc, bm, cm, hprev, alog, dbias, dskip):
    ln = 128
    a = -jnp.exp(alog)
    dt_r = softplus(dtr + dbias)
    da_r = dt_r * a
    da_c = softplus(dtc + dbias) * a
    li = lax.broadcasted_iota(jnp.int32, (1, ln, ln), 1)
    si = lax.broadcasted_iota(jnp.int32, (1, ln, ln), 2)
    causal = si <= li
    acs_c = jnp.sum(jnp.where(causal, da_r, 0.0), axis=2, keepdims=True)
    acs_r = jnp.sum(jnp.where(li <= si, da_c, 0.0), axis=1, keepdims=True)
    acs_last = jnp.sum(da_r, axis=2, keepdims=True)
    decay = jnp.exp(jnp.where(causal, acs_c - acs_r, -jnp.inf))
    m = mm_nt(cm, bm)[None] * decay
    xdt = xt * dt_r
    y_diag = bmm_nt(xdt, m)
    bb = jnp.broadcast_to(bm[None], (8, ln, ln))
    cc = jnp.broadcast_to(cm[None], (8, ln, ln))
    states = bmm_nn(xdt * jnp.exp(acs_last - acs_r), bb)
    y_off = bmm_nt(hprev, cc) * jnp.exp(acs_r)
    hnew = hprev * jnp.exp(acs_last) + states
    return y_diag + y_off + xt * dskip, hnew


def _ssd_specs(nc, rev):
    cix = (lambda c: nc - 1 - c) if rev else (lambda c: c)
    hv = pl.BlockSpec((8, 1, 1), lambda g, c: (g, 0, 0))
    return [pl.BlockSpec((8, 64, 128), lambda g, c: (g, 0, cix(c))), pl.BlockSpec((8, 1, 128), lambda g, c: (g, 0, cix(c))),
            pl.BlockSpec((8, 128, 1), lambda g, c: (g, cix(c), 0)), pl.BlockSpec((128, 128), lambda g, c: (cix(c), 8 + g)),
            pl.BlockSpec((128, 128), lambda g, c: (cix(c), 10 + g))], hv, cix


def ssd_fwd(xt, dtr, dtc, xbc, alog, dbias, dskip, name):
    s = xt.shape[2]
    nc = s // 128
    specs, hv, _ = _ssd_specs(nc, False)

    def body(x_ref, dr_ref, dc_ref, b_ref, c_ref, al_ref, db_ref, dk_ref, y_ref, hs_ref, h_scr):
        @pl.when(pl.program_id(1) == 0)
        def _():
            h_scr[...] = jnp.zeros_like(h_scr)
        hp = h_scr[...]
        hs_ref[...] = hp
        y, hn = _ssd_chunk(x_ref[...], dr_ref[...], dc_ref[...], b_ref[...], c_ref[...], hp,
                           al_ref[...], db_ref[...], dk_ref[...])
        y_ref[...] = y
        h_scr[...] = hn

    return pl.pallas_call(
        body, name=name, grid=(2, nc), in_specs=specs + [hv, hv, hv],
        out_specs=[pl.BlockSpec((8, 64, 128), lambda g, c: (g, 0, c)),
                   pl.BlockSpec((None, None, 8, 64, 128), lambda g, c: (g, c, 0, 0, 0))],
        out_shape=[jax.ShapeDtypeStruct((16, 64, s), F32), jax.ShapeDtypeStruct((2, nc, 8, 64, 128), F32)],
        scratch_shapes=[pltpu.VMEM((8, 64, 128), F32)],
        compiler_params=_params("arbitrary", "arbitrary"),
    )(xt, dtr, dtc, xbc, xbc, alog, dbias, dskip)


def ssd_bwd(xt, dtr, dtc, xbc, alog, dbias, dskip, hs, dyt, name):
    s = xt.shape[2]
    nc = s // 128
    specs, hv, cix = _ssd_specs(nc, True)

    def body(x_ref, dr_ref, dc_ref, b_ref, c_ref, al_ref, db_ref, dk_ref, hs_ref, dy_ref,
             dx_out, ddr_out, ddc_out, dbm_out, dcm_out, dal_out, ddb_out, ddk_out, dh_scr):
        @pl.when(pl.program_id(1) == 0)
        def _():
            dh_scr[...] = jnp.zeros_like(dh_scr)
            dal_out[...] = jnp.zeros_like(dal_out)
            ddb_out[...] = jnp.zeros_like(ddb_out)
            ddk_out[...] = jnp.zeros_like(ddk_out)
        _, vjp = jax.vjp(_ssd_chunk, x_ref[...], dr_ref[...], dc_ref[...], b_ref[...], c_ref[...], hs_ref[...],
                         al_ref[...], db_ref[...], dk_ref[...])
        g = vjp((dy_ref[...], dh_scr[...]))
        dx_out[...] = g[0]
        ddr_out[...] = g[1]
        ddc_out[...] = g[2]
        dbm_out[...] = g[3]
        dcm_out[...] = g[4]
        dh_scr[...] = g[5]
        dal_out[...] += g[6]
        ddb_out[...] += g[7]
        ddk_out[...] += g[8]

    return pl.pallas_call(
        body, name=name, grid=(2, nc),
        in_specs=specs + [hv, hv, hv, pl.BlockSpec((None, None, 8, 64, 128), lambda g, c: (g, cix(c), 0, 0, 0)),
                          pl.BlockSpec((8, 64, 128), lambda g, c: (g, 0, cix(c)))],
        out_specs=[pl.BlockSpec((8, 64, 128), lambda g, c: (g, 0, cix(c))), pl.BlockSpec((8, 1, 128), lambda g, c: (g, 0, cix(c))),
                   pl.BlockSpec((8, 128, 1), lambda g, c: (g, cix(c), 0)), pl.BlockSpec((128, 128), lambda g, c: (cix(c), g)),
                   pl.BlockSpec((128, 128), lambda g, c: (cix(c), g)), hv, hv, hv],
        out_shape=[jax.ShapeDtypeStruct((16, 64, s), F32), jax.ShapeDtypeStruct((16, 1, s), F32),
                   jax.ShapeDtypeStruct((16, s, 1), F32), jax.ShapeDtypeStruct((s, 256), F32),
                   jax.ShapeDtypeStruct((s, 256), F32)] + [jax.ShapeDtypeStruct((16, 1, 1), F32)] * 3,
        scratch_shapes=[pltpu.VMEM((8, 64, 128), F32)],
        compiler_params=_params("arbitrary", "arbitrary"),
    )(xt, dtr, dtc, xbc, xbc, alog, dbias, dskip, hs, dyt)


def _merge(oa, ob, y, z, gla, glb, glc, x, g1, nw, ea, eb, ec, eo, wba, wbb, wbc, wout):
    gated = y * (z * _sig(z))
    sq = gated * gated
    left = lax.broadcasted_iota(jnp.int32, (1, D), 1) < 512
    ms0 = jnp.sum(jnp.where(left, sq, 0.0), axis=-1, keepdims=True) / 512.0
    ms1 = jnp.sum(jnp.where(left, 0.0, sq), axis=-1, keepdims=True) / 512.0
    oc = gated * jnp.where(left, lax.rsqrt(ms0 + EPS), lax.rsqrt(ms1 + EPS)) * nw
    ya, yb, yc = mm_nc(oa, wba) + ea, mm_nc(ob, wbb) + eb, mm_nc(oc, wbc) + ec
    merged = _sig(gla) * ya + _sig(glb) * yb + _sig(glc) * yc
    x1 = x + g1 * (mm_nc(merged, wout) + eo)
    return x1, (oc, merged)


def _merge_specs(tr):
    row = lambda w: pl.BlockSpec((tr, w), lambda i: (i, 0))
    acts = [row(D), row(512), pl.BlockSpec((D, tr), lambda i: (0, i)), pl.BlockSpec((tr, D), lambda i: (i, O_Z // D)),
            pl.BlockSpec((tr, 3 * D), lambda i: (i, 0)), row(D), pl.BlockSpec((8, D), lambda i: (0, 0))]
    cst = lambda r: pl.BlockSpec((r, D), lambda i: (0, 0), **CONST)
    return acts, [cst(D), cst(512), cst(D), cst(D)], row


def merge_fwd(oa, ob, y, proj, x, mvec, wba, wbb, wbc, wout, name):
    s = x.shape[0]
    tr = _pick(s, 256)
    acts, wts, row = _merge_specs(tr)

    def body(oa_ref, ob_ref, y_ref, z_ref, gl_ref, x_ref, mv_ref, wba_ref, wbb_ref, wbc_ref, wout_ref, o_ref):
        zero = jnp.zeros((1, D), F32)
        x1, _ = _merge(oa_ref[...], ob_ref[...], y_ref[...].T, z_ref[...], gl_ref[:, 0:D], gl_ref[:, D:2 * D],
                       gl_ref[:, 2 * D:3 * D], x_ref[...], mv_ref[0:1, :], mv_ref[1:2, :], zero, zero, zero, zero,
                       wba_ref[...], wbb_ref[...], wbc_ref[...], wout_ref[...])
        o_ref[...] = x1

    return pl.pallas_call(
        body, name=name, grid=(s // tr,), in_specs=acts + wts, out_specs=row(D),
        out_shape=jax.ShapeDtypeStruct((s, D), F32), compiler_params=_params("arbitrary"),
    )(oa, ob, y, proj, proj, x, mvec, wba, wbb, wbc, wout)


def merge_bwd(oa, ob, y, proj, x, mvec, wba, wbb, wbc, wout, dx1, name):
    s = x.shape[0]
    tr = _pick(s, 128)
    acts, wts, row = _merge_specs(tr)

    def body(oa_ref, ob_ref, y_ref, z_ref, gl_ref, x_ref, mv_ref, wba_ref, wbb_ref, wbc_ref, wout_ref, dx1_ref,
             doa_o, dob_o, dy_o, dz_o, dgl_o, dx_o, dmv_o, dya_o, dyb_o, dyc_o, dpre_o, oc_o, mg_o):
        zero = jnp.zeros((tr, D), F32)
        wts_ = (wba_ref[...], wbb_ref[...], wbc_ref[...], wout_ref[...])
        f = lambda *a: _merge(*a, *wts_)
        _, vjp, (oc, merged) = jax.vjp(
            f, oa_ref[...], ob_ref[...], y_ref[...].T, z_ref[...], gl_ref[:, 0:D], gl_ref[:, D:2 * D],
            gl_ref[:, 2 * D:3 * D], x_ref[...], mv_ref[0:1, :], mv_ref[1:2, :], zero, zero, zero, zero, has_aux=True)
        g = vjp(dx1_ref[...])
        doa_o[...] = g[0]
        dob_o[...] = g[1]
        dy_o[...] = g[2].T
        dz_o[...] = g[3].astype(BF16)
        dgl_o[:, 0:D] = g[4].astype(BF16)
        dgl_o[:, D:2 * D] = g[5].astype(BF16)
        dgl_o[:, 2 * D:3 * D] = g[6].astype(BF16)
        dx_o[...] = g[7]

        @pl.when(pl.program_id(0) == 0)
        def _():
            dmv_o[...] = jnp.zeros_like(dmv_o)

        dmv_o[0:1, :] += g[8]
        dmv_o[1:2, :] += g[9]
        dya_o[...] = g[10].astype(BF16)
        dyb_o[...] = g[11].astype(BF16)
        dyc_o[...] = g[12].astype(BF16)
        dpre_o[...] = g[13].astype(BF16)
        oc_o[...] = oc.astype(BF16)
        mg_o[...] = merged.astype(BF16)

    sd = lambda w, dt: jax.ShapeDtypeStruct((s, w), dt)
    return pl.pallas_call(
        body, name=name, grid=(s // tr,), in_specs=acts + wts + [row(D)],
        out_specs=[row(D), row(512), pl.BlockSpec((D, tr), lambda i: (0, i)), row(D), row(3 * D), row(D),
                   pl.BlockSpec((8, D), lambda i: (0, 0))] + [row(D)] * 6,
        out_shape=[sd(D, F32), sd(512, F32), jax.ShapeDtypeStruct((D, s), F32), sd(D, BF16), sd(3 * D, BF16), sd(D, F32),
                   jax.ShapeDtypeStruct((8, D), F32)] + [sd(D, BF16)] * 6,
        compiler_params=_params("arbitrary"),
    )(oa, ob, y, proj, proj, x, mvec, wba, wbb, wbc, wout, dx1)


def _conv3(u_scr, w_ref, first, rows, lanes):
    return sum(w_ref[k:k + 1, :] * u_scr[first + k:first + k + rows, lanes] for k in range(3))


def _ffn_tile_specs(tf, tile):
    def at(rows, off):
        return pl.BlockSpec((rows, tf), lambda *g: (0, off + tile(*g)))
    return [at(D, 0), at(D, FFN_NT), at(3, 0), at(3, FFN_NT), at(1, 0), at(1, FFN_NT)]


def ffn_fwd(x1, fvec, wup, cw, cb, wdn, name):
    s = x1.shape[0]
    tr, tf = _pick(s, 512), FFN_TILE
    lg, lv = slice(0, tf), slice(tf, 2 * tf)

    def body(x_ref, v_ref, wg_ref, wv_ref, cwg_ref, cwv_ref, cbg_ref, cbv_ref, wd_ref, x2_ref, h_ref, pre_ref,
             h_scr, u_scr, acc):
        i, t = pl.program_id(0), pl.program_id(1)

        @pl.when(t == 0)
        def _():
            @pl.when(i == 0)
            def _():
                h_scr[0:16, :] = jnp.zeros((16, D), BF16)

            @pl.when(i > 0)
            def _():
                h_scr[0:16, :] = h_scr[tr:tr + 16, :]

            h = (_rms(x_ref[...], v_ref[0:1, :], D) * (1.0 + v_ref[2:3, :]) + v_ref[1:2, :]).astype(BF16)
            h_scr[16:16 + tr, :] = h
            h_ref[...] = h
            acc[...] = jnp.zeros_like(acc)

        u_scr[:, lg] = jnp.dot(h_scr[...], wg_ref[...], preferred_element_type=F32)
        u_scr[:, lv] = jnp.dot(h_scr[...], wv_ref[...], preferred_element_type=F32)
        cg = _conv3(u_scr, cwg_ref, 14, tr, lg) + cbg_ref[...]
        cval = _conv3(u_scr, cwv_ref, 14, tr, lv) + cbv_ref[...]
        acc[...] += _raw(cg * _sig(cg) * cval, wd_ref[...], _NN)

        @pl.when(t == FFN_NT - 1)
        def _():
            pre_ref[...] = acc[...]
            x2_ref[...] = x_ref[...] + v_ref[3:4, :] * acc[...]

    row = pl.BlockSpec((tr, D), lambda i, t: (i, 0))
    return pl.pallas_call(
        body, name=name, grid=(s // tr, FFN_NT),
        in_specs=[row, pl.BlockSpec((8, D), lambda i, t: (0, 0))] + _ffn_tile_specs(tf, lambda i, t: t)
                 + [pl.BlockSpec((tf, D), lambda i, t: (t, 0))],
        out_specs=[row, row, row],
        out_shape=[jax.ShapeDtypeStruct((s, D), F32), jax.ShapeDtypeStruct((s, D), BF16), jax.ShapeDtypeStruct((s, D), F32)],
        scratch_shapes=[pltpu.VMEM((tr + 16, D), BF16), pltpu.VMEM((tr + 16, 2 * tf), F32), pltpu.VMEM((tr, D), F32)],
        compiler_params=_params("arbitrary", "arbitrary"),
    )(x1, fvec, wup, wup, cw, cw, cb, cb, wdn)


def ffn_bwd(h2, dx2, fvec, wup, cw, cb, wdn, name):
    s = h2.shape[0]
    tr, tf = _pick(s, 512), FFN_TILE
    ni, nb = s // tr, s // 16
    lg, lv = slice(0, tf), slice(tf, 2 * tf)

    def body(hp_ref, hm_ref, hn_ref, dm_ref, dn_ref, v_ref, wg_ref, wv_ref, cwg_ref, cwv_ref, cbg_ref, cbv_ref, wd_ref,
             dup_ref, act_ref, dcw_ref, u_scr, dc_scr):
        i = pl.program_id(1)
        hfull = jnp.concatenate([jnp.where(i > 0, hp_ref[...], jnp.zeros((16, D), BF16)), hm_ref[...],
                                 jnp.where(i < ni - 1, hn_ref[...], jnp.zeros((16, D), BF16))], axis=0)
        u_scr[:, lg] = jnp.dot(hfull, wg_ref[...], preferred_element_type=F32)
        u_scr[:, lv] = jnp.dot(hfull, wv_ref[...], preferred_element_type=F32)
        cg = _conv3(u_scr, cwg_ref, 14, tr + 16, lg) + cbg_ref[...]
        cval = _conv3(u_scr, cwv_ref, 14, tr + 16, lv) + cbv_ref[...]
        g2 = v_ref[3:4, :]
        dpre = jnp.concatenate([dm_ref[...] * g2, jnp.where(i < ni - 1, dn_ref[...], 0.0) * g2], axis=0)
        dact = _raw(dpre, wd_ref[...], _NT)
        sg = _sig(cg)
        sl = cg * sg
        dc_scr[:, lg] = dact * cval * (sg * (1.0 + cg * (1.0 - sg)))
        dc_scr[:, lv] = dact * sl
        act_ref[...] = (sl * cval)[0:tr, :].astype(BF16)

        @pl.when(i == 0)
        def _():
            dcw_ref[...] = jnp.zeros_like(dcw_ref)

        for half, lanes, cw_ref in ((0, lg, cwg_ref), (1, lv, cwv_ref)):
            dup_ref[half] = sum(cw_ref[k:k + 1, :] * dc_scr[2 - k:2 - k + tr, lanes] for k in range(3)).astype(BF16)
            dcm = dc_scr[0:tr, lanes]
            for k in range(3):
                dcw_ref[half, k:k + 1, :] += jnp.sum(dcm * u_scr[14 + k:14 + k + tr, lanes], axis=0, keepdims=True)
            dcw_ref[half, 3:4, :] += jnp.sum(dcm, axis=0, keepdims=True)

    r16 = tr // 16
    prev = lambda t, i: (jnp.maximum(i * r16 - 1, 0), 0)
    nxt = lambda t, i: (jnp.minimum((i + 1) * r16, nb - 1), 0)
    main = lambda t, i: (i, 0)
    return pl.pallas_call(
        body, name=name, grid=(FFN_NT, ni),
        in_specs=[pl.BlockSpec((16, D), prev), pl.BlockSpec((tr, D), main), pl.BlockSpec((16, D), nxt),
                  pl.BlockSpec((tr, D), main), pl.BlockSpec((16, D), nxt), pl.BlockSpec((8, D), lambda t, i: (0, 0))]
                 + _ffn_tile_specs(tf, lambda t, i: t) + [pl.BlockSpec((tf, D), lambda t, i: (t, 0))],
        out_specs=[pl.BlockSpec((2, tr, tf), lambda t, i: (0, i, t)), pl.BlockSpec((tr, tf), lambda t, i: (i, t)),
                   pl.BlockSpec((2, 8, tf), lambda t, i: (0, 0, t))],
        out_shape=[jax.ShapeDtypeStruct((2, s, FFN), BF16), jax.ShapeDtypeStruct((s, FFN), BF16),
                   jax.ShapeDtypeStruct((2, 8, FFN), F32)],
        scratch_shapes=[pltpu.VMEM((tr + 32, 2 * tf), F32), pltpu.VMEM((tr + 16, 2 * tf), F32)],
        compiler_params=_params("arbitrary", "arbitrary"),
    )(h2, h2, h2, dx2, dx2, fvec, wup, wup, cw, cw, cb, cb, wdn)


def loss_head(y, target):
    s = y.shape[0]
    tr = _pick(s, 512)

    def body(y_ref, t_ref, dx_ref, l_ref):
        @pl.when(pl.program_id(0) == 0)
        def _():
            l_ref[...] = jnp.zeros_like(l_ref)
        err = y_ref[...] - t_ref[...]
        dx_ref[...] = err / float(D)
        l_ref[...] += 0.5 * jnp.sum(jnp.sum(err * err, axis=-1, keepdims=True) / float(D), axis=0, keepdims=True)

    row = pl.BlockSpec((tr, D), lambda i: (i, 0))
    return pl.pallas_call(
        body, name="loss_head", grid=(s // tr,), in_specs=[row, row],
        out_specs=[row, pl.BlockSpec((8, 128), lambda i: (0, 0))],
        out_shape=[jax.ShapeDtypeStruct((s, D), F32), jax.ShapeDtypeStruct((8, 128), F32)],
        compiler_params=_params("arbitrary"),
    )(y, target)


def adamw(parts, w, m, v, name, tok=None):
    nseg = len(parts)
    p, r, c = parts[0].shape
    tr = _pick(r, 256 if c > 128 else 2048, 8)
    ni = r // tr
    tok = jnp.zeros((8, 128), F32) if tok is None else tok

    def body(*refs):
        p_refs = refs[:nseg]
        w_ref, m_ref, v_ref, _, g_out, d_out, m_out, v_out, g_scr = refs[nseg:]
        for q in range(nseg):
            @pl.when(pl.program_id(0) == q)
            def _(q=q):
                g = p_refs[q][0].astype(F32)
                for j in range(1, p):
                    g = g + p_refs[q][j].astype(F32)
                g_scr[...] = g
        g = g_scr[...]
        mn = B1 * m_ref[...] + (1.0 - B1) * g
        vn = B2 * v_ref[...] + (1.0 - B2) * (g * g)
        m_hat = mn / (1.0 - B1 ** STEP)
        v_hat = vn / (1.0 - B2 ** STEP)
        g_out[...] = g
        d_out[...] = -LR * (m_hat / (jnp.sqrt(v_hat) + ADAM_EPS) + WD * w_ref[...])
        m_out[...] = mn
        v_out[...] = vn

    row = pl.BlockSpec((tr, c), lambda l, i: (l * ni + i, 0))
    part = lambda q: pl.BlockSpec((p, tr, c), lambda l, i: (0, jnp.clip((l - q) * ni + i, 0, ni - 1), 0))
    return pl.pallas_call(
        body, name=name, grid=(nseg, ni),
        in_specs=[part(q) for q in range(nseg)] + [row, row, row, pl.BlockSpec((8, 128), lambda l, i: (0, 0))],
        out_specs=[row] * 4, out_shape=[jax.ShapeDtypeStruct((nseg * r, c), F32)] * 4,
        scratch_shapes=[pltpu.VMEM((tr, c), F32)],
        compiler_params=_params("arbitrary", "arbitrary"),
    )(*parts, w, m, v, tok)


def _padc(a, n):
    return jnp.pad(a, [(0, 0)] * (a.ndim - 1) + [(0, n - a.shape[-1])])


def _swap16(a):
    return jnp.concatenate([a[..., 16:32], a[..., 0:16]], axis=-1)


def _shard_cols(g8, a, b):
    c = g8.shape[2]
    return [g8[j][:, max(a, j * c) - j * c:min(b, (j + 1) * c) - j * c] for j in range(a // c, (b - 1) // c + 1)]


def _win_layout(g8):
    cols = lambda a, b: _shard_cols(g8, a, b)
    kr = jnp.concatenate(cols(640, 672), axis=1)
    dt = jnp.concatenate(cols(3744, 3760), axis=1)
    return jnp.concatenate(cols(3760, 6832) + cols(2208, 3232) + cols(1184, 2208) + cols(672, 1184) + cols(3232, 3744)
                           + cols(384, 640) + [_padc(kr, 128), _padc(_swap16(kr), 128), _padc(dt, 128),
                                               jnp.zeros((g8.shape[1], 128), g8.dtype)] + cols(0, 384), axis=1)


def _win_grad_shards(g):
    kr = (g[:, O_KR:O_KR + 32].astype(F32) + _swap16(g[:, O_KRS:O_KRS + 32].astype(F32))).astype(g.dtype)
    segs = [(g, O_QL, 384), (g, O_CKV, 256), (kr, 0, 32), (g, O_PU, 512), (g, O_Z, D), (g, O_XS, D), (g, O_BC, 512),
            (g, O_DT, 16), (g, O_G, 3 * D)]
    shards, width = [], sum(w for _, _, w in segs) // NDEV
    for j in range(NDEV):
        a, b, off, pieces = width * j, width * (j + 1), 0, []
        for arr, lo, w in segs:
            s0, s1 = max(a, off), min(b, off + w)
            if s0 < s1:
                pieces.append(arr[:, lo + s0 - off:lo + s1 - off])
            off += w
        shards.append(jnp.concatenate(pieces, axis=1))
    return jnp.stack(shards).astype(BF16)


def _wq_layout(w):
    w = w.reshape(384, HEADS, 96).transpose(1, 0, 2)
    rope = w[:, :, 64:96]
    return jnp.concatenate([_padc(w[:, :, 0:64], 128), _padc(rope, 128), _padc(_swap16(rope), 128)], axis=2)


def _wq_unlayout(g):
    rope = g[:, :, 128:160] + _swap16(g[:, :, 256:288])
    return jnp.concatenate([g[:, :, 0:64], rope], axis=2).transpose(1, 0, 2).reshape(384, HEADS * 96)


def _wkv_layout(w):
    w = w.reshape(256, HEADS, 128).transpose(1, 0, 2)
    return jnp.concatenate([_padc(w[:, :, 0:64], 128), _padc(w[:, :, 64:128], 128)], axis=2)


def _wkv_unlayout(g):
    return jnp.concatenate([g[:, :, 0:64], g[:, :, 128:192]], axis=2).transpose(1, 0, 2).reshape(256, HEADS * 128)


def _wba_layout(w):
    return jnp.pad(w.reshape(HEADS, 64, D), ((0, 0), (0, 64), (0, 0))).reshape(HEADS * 128, D)


def _rows8(rows, width):
    out = jnp.stack([_padc(r.astype(F32), width) for r in rows])
    return jnp.pad(out, ((0, 8 - out.shape[0]), (0, 0)))


def _mla_vec(qa, kva, qn, kn):
    def row(n):
        return jnp.concatenate([_padc(n[0:64], 128), _padc(n[64:96], 128), _padc(_swap16(n[64:96]), 128)])
    return _rows8([qa, kva, row(qn), row(kn)], 512)


def _mla_unvec(g):
    def un(r):
        return jnp.concatenate([r[0:64], r[128:160] + _swap16(r[256:288])])
    return g[0, 0:384], g[1, 0:256], un(g[2]), un(g[3])


SMALL = (("ada_b", (6 * D,)), ("norm1_w", (D,)), ("q_a_norm", (384,)), ("kv_a_norm", (256,)), ("q_norm", (96,)),
         ("k_norm", (96,)), ("pool_w", (4, 128, 128)), ("pool_scale", (512,)), ("ssd_conv_b", (1536,)),
         ("ssd_dt_bias", (16,)), ("ssd_a_log", (16,)), ("ssd_d", (16,)), ("ssd_norm_w", (D,)), ("norm2_w", (D,)),
         ("ffn_conv_b", (2 * FFN,)), ("ssd_conv_w", (4, 1536)), ("ffn_conv_w", (3, 2 * FFN)))
SHARDED_SMALL = {"ssd_conv_w": 192, "ffn_conv_w": 704}


def _pack_rows(shp):
    return -(-math.prod(shp) // 1024) * 8


def _pack(small):
    pieces = []
    for n, shp in SMALL:
        pieces.append(small[n].reshape(-1).astype(F32))
        fill = _pack_rows(shp) * 128 - math.prod(shp)
        if fill:
            pieces.append(jnp.zeros((fill,), F32))
    return jnp.concatenate(pieces).reshape(-1, 128)


def _unpack_parts(packs):
    out, off = {}, 0
    for n, shp in SMALL:
        rows, size = _pack_rows(shp), math.prod(shp)
        r, c = math.prod(shp[:-1]), shp[-1]
        per_layer = [pk[:, off:off + rows].reshape(NDEV, rows * 128)[:, 0:size].reshape(NDEV, r, c) for pk in packs]
        out[n] = jnp.concatenate(per_layer, axis=1)
        off += rows
    return out


GROUP_A = ("w_in", "w_q_b", "w_kv_b")
GROUP_B = ("w_branch", "w_out", "ffn_up", "ffn_down")
BIG = GROUP_A + GROUP_B
SCATTER_FFN, SCATTER_MERGE = ("ffn_up", "ffn_down"), ("w_branch", "w_out")
COL_SHARDED = ("w_in", "w_q_b", "w_kv_b", "ffn_up")


def _behind(arrs, tok):
    arrs = list(arrs)
    j = min(range(len(arrs)), key=lambda q: arrs[q].size)
    arrs[j] = arrs[j] + tok[0, 0].astype(arrs[j].dtype)
    return arrs


def _gathered_full(g, name):
    if name in COL_SHARDED:
        return g.transpose(1, 0, 2).reshape(g.shape[1], NDEV * g.shape[2])
    return g.reshape(NDEV * g.shape[1], g.shape[2])


def _to_shards(full, name):
    if name == "w_in":
        return _win_grad_shards(full)
    if name in COL_SHARDED:
        r, c = full.shape
        return full.reshape(r, NDEV, c // NDEV).transpose(1, 0, 2).astype(BF16)
    r, c = full.shape
    return full.reshape(NDEV, r // NDEV, c).astype(BF16)


def _fwd_a(x, lw, mod, cos2, sin2, l, tok):
    sh1, sc1, g1, sh2, sc2, g2 = [mod[j * D:(j + 1) * D] for j in range(6)]
    vec1 = _rows8([lw["norm1_w"], sh1, sc1], D) + tok[0, 0]
    proj, h1, dt_cols = norm_proj_fwd(x, vec1, lw["win"], f"inproj_fwd{l}")
    q, k, v = mla_pre_fwd(proj, lw["wq"], lw["wkv"], lw["mla_vec"], cos2, sin2, f"mla_pre_fwd{l}")
    oa = mla_attn_fwd(q, k, v, f"mla_attn_fwd{l}")
    ob = pool_fwd(proj, lw["pool_w"], lw["pool_scale"].reshape(1, 512), f"pool_fwd{l}")
    xbc, xt = conv_fwd(proj, lw["ssd_conv_w"], lw["ssd_conv_b"].reshape(1, 1536), f"conv_fwd{l}")
    s = x.shape[0]
    xt = xt.reshape(16, 64, s)
    dt = dt_cols[:, 0:16].T
    dtr, dtc = dt[:, None, :], dt[:, :, None]
    hv = lambda a: a.reshape(16, 1, 1)
    yt, hs = ssd_fwd(xt, dtr, dtc, xbc, hv(lw["ssd_a_log"]), hv(lw["ssd_dt_bias"]), hv(lw["ssd_d"]), f"ssd_fwd{l}")
    return dict(x=x, vec1=vec1, proj=proj, h1=h1, q=q, k=k, v=v, oa=oa, ob=ob, xbc=xbc, xt=xt, dtr=dtr, dtc=dtc,
                hs=hs, yt=yt.reshape(D, s), mvec=_rows8([g1, lw["ssd_norm_w"]], D),
                fvec=_rows8([lw["norm2_w"], sh2, sc2, g2], D))


def _fwd_b(sv, lw, l, tok):
    sv["mvec"] = sv["mvec"] + tok[0, 0]
    x1 = merge_fwd(sv["oa"], sv["ob"], sv["yt"], sv["proj"], sv["x"], sv["mvec"], lw["wba"], lw["wbb"], lw["wbc"],
                   lw["wout"], f"merge_fwd{l}")
    x2, h2, pre = ffn_fwd(x1, sv["fvec"], lw["wup"], lw["ffn_conv_w"], lw["ffn_conv_b"].reshape(1, 2 * FFN), lw["wdn"],
                          f"ffn_fwd{l}")
    sv.update(x1=x1, h2=h2, pre=pre)
    return x2


def _bwd_ffn(dx2, lw, sv, l, tok):
    fvec = sv["fvec"] + tok[0, 0]
    dup, act, dcw = ffn_bwd(sv["h2"], dx2, fvec, lw["wup"], lw["ffn_conv_w"], lw["ffn_conv_b"].reshape(1, 2 * FFN),
                            lw["wdn"], f"ffn_bwd{l}")
    grads = dict(ffn_down=tn_matmul(act, dx2, f"dw_down{l}", scale=fvec[3:4]),
                 ffn_up=tn_matmul(sv["h2"], dup, f"dw_up{l}"))
    small = dict(ffn_conv_w=jnp.concatenate([dcw[0, 0:3], dcw[1, 0:3]], axis=1),
                 ffn_conv_b=jnp.concatenate([dcw[0, 3], dcw[1, 3]]))
    return dup, grads, small


def _bwd_merge(dx2, dup, lw, sv, l, tok, small):
    grads = {}
    fvec = sv["fvec"] + tok[0, 0]
    dx1, dfvec = norm_proj_bwd(sv["x1"], fvec, dup, lw["wup"], dx2, sv["pre"], f"ffn_norm_bwd{l}")
    small["norm2_w"] = dfvec[0]
    (doa, dob, dyt, dz, dgl, dx, dmvec, dya, dyb, dyc, dpre, oc, merged) = merge_bwd(
        sv["oa"], sv["ob"], sv["yt"], sv["proj"], sv["x"], sv["mvec"], lw["wba"], lw["wbb"], lw["wbc"], lw["wout"], dx1,
        f"merge_bwd{l}")
    dwba = tn_matmul(sv["oa"], dya, f"dw_ba{l}").reshape(HEADS, 128, D)[:, 0:64].reshape(512, D)
    grads["w_branch"] = jnp.concatenate([dwba, tn_matmul(sv["ob"], dyb, f"dw_bb{l}"), tn_matmul(oc, dyc, f"dw_bc{l}")])
    grads["w_out"] = tn_matmul(merged, dpre, f"dw_out{l}")
    small["ssd_norm_w"] = dmvec[1]
    small["dmod_b"] = (dmvec[0], dfvec[1], dfvec[2], dfvec[3])
    return dx, dict(doa=doa, dob=dob, dyt=dyt, dz=dz, dgl=dgl), grads, small


def _bwd_a(dx, cot, lw, sv, cos2, sin2, l, tok, small):
    s = dx.shape[0]
    grads = {}
    doa, dob, dz, dgl = cot["doa"], cot["dob"], cot["dz"], cot["dgl"]
    hv = lambda a: a.reshape(16, 1, 1)
    dxt, ddtr, ddtc, dbm, dcm, dal, ddb, ddk = ssd_bwd(
        sv["xt"], sv["dtr"], sv["dtc"], sv["xbc"], hv(lw["ssd_a_log"]) + tok[0, 0], hv(lw["ssd_dt_bias"]),
        hv(lw["ssd_d"]), sv["hs"], cot["dyt"].reshape(16, 64, s), f"ssd_bwd{l}")
    small["ssd_a_log"], small["ssd_dt_bias"], small["ssd_d"] = dal.reshape(16), ddb.reshape(16), ddk.reshape(16)
    dxbc, dscw, dscb = conv_bwd(sv["proj"], lw["ssd_conv_w"], lw["ssd_conv_b"].reshape(1, 1536), dxt.reshape(D, s),
                                dbm, dcm, f"conv_bwd{l}")
    small["ssd_conv_w"], small["ssd_conv_b"] = dscw, dscb.reshape(1536)
    ddt = (ddtr[:, 0, :] + ddtc[:, :, 0]).T
    du, dpw, dps = pool_bwd(sv["proj"], lw["pool_w"], lw["pool_scale"].reshape(1, 512), dob, f"pool_bwd{l}")
    small["pool_w"], small["pool_scale"] = dpw, dps.reshape(512)
    dq, dk, dv = mla_attn_bwd(sv["q"], sv["k"], sv["v"], doa, f"mla_attn_bwd{l}")
    dql, dckv, dkr, dkrs, dwq, dwkv, dmv = mla_pre_bwd(sv["proj"], lw["wq"], lw["wkv"], lw["mla_vec"], cos2, sin2,
                                                       dq, dk, dv, f"mla_pre_bwd{l}")
    grads["w_q_b"], grads["w_kv_b"] = _wq_unlayout(dwq), _wkv_unlayout(dwkv)
    small["q_a_norm"], small["kv_a_norm"], small["q_norm"], small["k_norm"] = _mla_unvec(dmv)
    dproj = jnp.concatenate([dgl, dxbc[:, 0:D], dz, du, dxbc[:, D:1536], dckv, dkr, dkrs,
                             _padc(ddt, 128).astype(BF16), jnp.zeros((s, 128), BF16), dql], axis=1)
    grads["w_in"] = tn_matmul(sv["h1"], dproj, f"dw_in{l}")
    return dproj, grads, small


def _bwd_in(dx, dproj, lw, sv, l, tok, small):
    dx0, dvec1 = norm_proj_bwd(sv["x"], sv["vec1"] + tok[0, 0], dproj, lw["win"], dx, None, f"inproj_bwd{l}")
    small["norm1_w"] = dvec1[0]
    small["ada_b"] = jnp.concatenate([dvec1[1], dvec1[2], *small.pop("dmod_b")])
    return dx0, small


def kernel(x, c, positions, ada_w, ada_b, norm1_w, w_in, q_a_norm, w_q_b, kv_a_norm, w_kv_b, q_norm, k_norm, pool_w, pool_scale, ssd_conv_w, ssd_conv_b, ssd_dt_bias, ssd_a_log, ssd_d, ssd_norm_w, w_branch, w_out, norm2_w, ffn_up, ffn_conv_w, ffn_conv_b, ffn_down, loss_target, m_ada_w, m_ada_b, m_norm1_w, m_w_in, m_q_a_norm, m_w_q_b, m_kv_a_norm, m_w_kv_b, m_q_norm, m_k_norm, m_pool_w, m_pool_scale, m_ssd_conv_w, m_ssd_conv_b, m_ssd_dt_bias, m_ssd_a_log, m_ssd_d, m_ssd_norm_w, m_w_branch, m_w_out, m_norm2_w, m_ffn_up, m_ffn_conv_w, m_ffn_conv_b, m_ffn_down, v_ada_w, v_ada_b, v_norm1_w, v_w_in, v_q_a_norm, v_w_q_b, v_kv_a_norm, v_w_kv_b, v_q_norm, v_k_norm, v_pool_w, v_pool_scale, v_ssd_conv_w, v_ssd_conv_b, v_ssd_dt_bias, v_ssd_a_log, v_ssd_d, v_ssd_norm_w, v_w_branch, v_w_out, v_norm2_w, v_ffn_up, v_ffn_conv_w, v_ffn_conv_b, v_ffn_down):
    p = dict(ada_w=ada_w, ada_b=ada_b, norm1_w=norm1_w, w_in=w_in, q_a_norm=q_a_norm, w_q_b=w_q_b, kv_a_norm=kv_a_norm,
             w_kv_b=w_kv_b, q_norm=q_norm, k_norm=k_norm, pool_w=pool_w, pool_scale=pool_scale, ssd_conv_w=ssd_conv_w,
             ssd_conv_b=ssd_conv_b, ssd_dt_bias=ssd_dt_bias, ssd_a_log=ssd_a_log, ssd_d=ssd_d, ssd_norm_w=ssd_norm_w,
             w_branch=w_branch, w_out=w_out, norm2_w=norm2_w, ffn_up=ffn_up, ffn_conv_w=ffn_conv_w, ffn_conv_b=ffn_conv_b,
             ffn_down=ffn_down)
    mom = dict(ada_w=m_ada_w, ada_b=m_ada_b, norm1_w=m_norm1_w, w_in=m_w_in, q_a_norm=m_q_a_norm, w_q_b=m_w_q_b,
               kv_a_norm=m_kv_a_norm, w_kv_b=m_w_kv_b, q_norm=m_q_norm, k_norm=m_k_norm, pool_w=m_pool_w,
               pool_scale=m_pool_scale, ssd_conv_w=m_ssd_conv_w, ssd_conv_b=m_ssd_conv_b, ssd_dt_bias=m_ssd_dt_bias,
               ssd_a_log=m_ssd_a_log, ssd_d=m_ssd_d, ssd_norm_w=m_ssd_norm_w, w_branch=m_w_branch, w_out=m_w_out,
               norm2_w=m_norm2_w, ffn_up=m_ffn_up, ffn_conv_w=m_ffn_conv_w, ffn_conv_b=m_ffn_conv_b, ffn_down=m_ffn_down)
    var = dict(ada_w=v_ada_w, ada_b=v_ada_b, norm1_w=v_norm1_w, w_in=v_w_in, q_a_norm=v_q_a_norm, w_q_b=v_w_q_b,
               kv_a_norm=v_kv_a_norm, w_kv_b=v_w_kv_b, q_norm=v_q_norm, k_norm=v_k_norm, pool_w=v_pool_w,
               pool_scale=v_pool_scale, ssd_conv_w=v_ssd_conv_w, ssd_conv_b=v_ssd_conv_b, ssd_dt_bias=v_ssd_dt_bias,
               ssd_a_log=v_ssd_a_log, ssd_d=v_ssd_d, ssd_norm_w=v_ssd_norm_w, w_branch=v_w_branch, w_out=v_w_out,
               norm2_w=v_norm2_w, ffn_up=v_ffn_up, ffn_conv_w=v_ffn_conv_w, ffn_conv_b=v_ffn_conv_b, ffn_down=v_ffn_down)
    names = list(p)
    me = 4 * lax.axis_index("x") + 2 * lax.axis_index("y") + lax.axis_index("c")
    xs, tgt = x[0], loss_target[0]
    s = xs.shape[0]

    inv_freq = ROPE_THETA ** (-jnp.arange(0, 32, 2, dtype=F32) / 32.0)
    ang = positions[0].astype(F32)[:, None] * inv_freq
    cos, sin = jnp.cos(ang), jnp.sin(ang)
    cos2 = _padc(jnp.concatenate([cos, cos], axis=1), 128)
    sin2 = _padc(jnp.concatenate([-sin, sin], axis=1), 128)

    conv_shards = jnp.concatenate([ssd_conv_w.reshape(-1), ffn_conv_w.reshape(-1)])
    (c_all, conv_all), _ = all_to_all([c, conv_shards], [True, True], "gather_c")
    modp, cact = ada_mod(jnp.pad(c_all.reshape(NDEV, D), ((0, 8), (0, 0))), ada_w)
    (mod_in,), tok = all_to_all([modp[:, 0:NDEV].transpose(1, 0, 2)], [False], "scatter_mod")
    mod = mod_in.transpose(1, 0, 2).reshape(LAYERS, 6 * D) + ada_b

    n1 = LAYERS * 4 * 192
    scw = conv_all[:, :n1].reshape(NDEV, LAYERS, 4, 192).transpose(1, 2, 0, 3).reshape(LAYERS, 4, 1536)
    fcw = conv_all[:, n1:].reshape(NDEV, LAYERS, 3, 704).transpose(1, 2, 0, 3).reshape(LAYERS, 3, 2 * FFN)

    def weights_a(gathered, l):
        full = {n: _gathered_full(g, n) for n, g in zip(GROUP_A[1:], gathered[1:])}
        lw = {n: p[n][l] for n in names}
        lw.update(win=_win_layout(gathered[0]), wq=_wq_layout(full["w_q_b"]), wkv=_wkv_layout(full["w_kv_b"]),
                  ssd_conv_w=scw[l], ffn_conv_w=fcw[l],
                  mla_vec=_mla_vec(lw["q_a_norm"], lw["kv_a_norm"], lw["q_norm"], lw["k_norm"]))
        return lw

    def weights_b(gathered):
        full = {n: _gathered_full(g, n) for n, g in zip(GROUP_B, gathered)}
        wb = full["w_branch"]
        return dict(wba=_wba_layout(wb[0:512]), wbb=wb[512:1024], wbc=wb[1024:2048], wout=full["w_out"],
                    wup=full["ffn_up"], wdn=full["ffn_down"])

    shards = lambda group, l: [p[n][l].astype(BF16) for n in group]
    lws, saved = [None] * LAYERS, [None] * LAYERS
    st, tok = gather_start(_behind(shards(GROUP_A, 0), tok), "gather_a0")
    got, tok = gather_finish(st, tok, "gather_a0")
    h = xs
    for l in range(LAYERS):
        st, tok = gather_start(_behind(shards(GROUP_B, l), tok), f"gather_b{l}")
        lws[l] = weights_a(got, l)
        saved[l] = _fwd_a(h, lws[l], mod[l], cos2, sin2, l, tok)
        got, tok = gather_finish(st, saved[l]["yt"], f"gather_b{l}")
        lws[l].update(weights_b(got))
        if l + 1 < LAYERS:
            st, tok = gather_start(_behind(shards(GROUP_A, l + 1), tok), f"gather_a{l + 1}")
        h = _fwd_b(saved[l], lws[l], l, tok)
        if l + 1 < LAYERS:
            got, tok = gather_finish(st, h, f"gather_a{l + 1}")
    dx, lpart = loss_head(h, tgt)
    loss = lax.psum(lpart[0, 0], ("x", "y", "c"))
    tok = tok + loss * 0.0

    small, parts, packs = [None] * LAYERS, {n: [None] * LAYERS for n in BIG}, [None] * LAYERS
    st = None

    def scatter(grads, group, l, tok, extra=None):
        arrs, flags = [_to_shards(grads[n], n) for n in group], [False] * len(group)
        if extra is not None:
            arrs, flags = arrs + [extra], flags + [True]
        return exchange_start(_behind(arrs, tok), flags, f"scatter_{group[0]}{l}_start")

    def landed(state, group, l, after):
        got, tok, _ = exchange_wait(state, after, f"scatter_{group[0]}{l}_wait")
        for n, g in zip(group, got):
            parts[n][l] = g
        return got, tok

    for l in reversed(range(LAYERS)):
        dup, g_ffn, small[l] = _bwd_ffn(dx, lws[l], saved[l], l, tok)
        if st is not None:
            _, tok = landed(st, GROUP_A, l + 1, dup)
        st, tok = scatter(g_ffn, SCATTER_FFN, l, tok)
        dx, cot, g_merge, small[l] = _bwd_merge(dx, dup, lws[l], saved[l], l, tok, small[l])
        _, tok = landed(st, SCATTER_FFN, l, dx)
        st, tok = scatter(g_merge, SCATTER_MERGE, l, tok, _pack(small[l + 1]) if l + 1 < LAYERS else None)
        dproj, g_in, small[l] = _bwd_a(dx, cot, lws[l], saved[l], cos2, sin2, l, tok, small[l])
        got, tok = landed(st, SCATTER_MERGE, l, dproj)
        if l + 1 < LAYERS:
            packs[l + 1] = got[-1]
        st, tok = scatter(g_in, GROUP_A, l, tok)
        dx, small[l] = _bwd_in(dx, dproj, lws[l], saved[l], l, tok, small[l])

    dmod = jnp.stack([small[q]["ada_b"] for q in range(LAYERS)])
    st_small, tok = exchange_start(_behind([_pack(small[0]), dmod.reshape(LAYERS, NDEV, 768).transpose(1, 0, 2)], tok),
                                   [True, False], "scatter_s0_start")
    out = {}

    def big_adamw(group, tok):
        res = None
        for n in group:
            shp = p[n].shape
            flat = lambda a: a.reshape(shp[0] * shp[1], shp[2])
            res = adamw(parts[n], flat(p[n]), flat(mom[n]), flat(var[n]), f"adamw_{n}", tok)
            out[n] = [r.reshape(shp) for r in res]
        return res[0]

    g_last = big_adamw(GROUP_B, tok)
    _, tok = landed(st, GROUP_A, 0, g_last)
    (packs[0], dmod_in), _, _ = exchange_wait(st_small, tok, "scatter_s0_wait")
    big_adamw(GROUP_A, None)

    dmod16 = jnp.pad(dmod_in, ((0, 8), (0, 0), (0, 0)))
    g_ada = jnp.stack([tn_matmul(cact, dmod16[:, l], f"dw_ada{l}", out_dtype=F32) for l in range(LAYERS)])
    flat = lambda a: a.reshape(LAYERS * D, 768)
    out["ada_w"] = [r.reshape(ada_w.shape) for r in
                    adamw([flat(g_ada)[None]], flat(ada_w), flat(m_ada_w), flat(v_ada_w), "adamw_ada_w")]

    for n, pt in _unpack_parts(packs).items():
        if n in SHARDED_SMALL:
            w = SHARDED_SMALL[n]
            pt = lax.dynamic_slice_in_dim(pt, me * w, w, axis=2)
        r, c = pt.shape[1:]
        res = adamw([pt], p[n].reshape(r, c), mom[n].reshape(r, c), var[n].reshape(r, c), f"adamw_{n}")
        out[n] = [a.reshape(p[n].shape) for a in res]

    outs = [loss, dx[None]]
    for q in range(4):
        outs += [out[n][q] for n in names]
    return tuple(outs)
```

```python
import functools
import math

import jax
import jax.numpy as jnp
from jax import lax
from jax.experimental import pallas as pl
from jax.experimental.pallas import tpu as pltpu

F32, BF16 = jnp.float32, jnp.bfloat16
EPS = 1e-6
D = 1024
NDEV = 8
LAYERS = 2
HEADS = 8
FFN = 2816
FFN_TILE = 1408
FFN_NT = FFN // FFN_TILE
ATT_SCALE = 96 ** -0.5
ROPE_THETA = 10000.0
LR, B1, B2, ADAM_EPS, WD, STEP = 0.001, 0.9, 0.999, 1e-08, 0.01, 10

O_G, O_XS, O_Z, O_PU, O_BC, O_CKV, O_KR, O_KRS, O_DT, O_QL = 0, 3072, 4096, 5120, 5632, 6144, 6400, 6528, 6656, 6912
NPROJ = 7296
CONST = dict(pipeline_mode=pl.Buffered(1))


def _pick(n, cap, mult=128):
    if n <= cap:
        return n
    best = None
    for t in range(mult, cap + 1, mult):
        if n % t == 0:
            best = t
    assert best is not None, (n, cap, mult)
    return best


def _sig(x):
    return 1.0 / (1.0 + jnp.exp(-x))


def _rms(x, w, n):
    return x * lax.rsqrt(jnp.sum(x * x, axis=-1, keepdims=True) / n + EPS) * w


def _raw(a, b, dims):
    return lax.dot_general(a.astype(BF16), b.astype(BF16), dims, preferred_element_type=F32)


_NN = (((1,), (0,)), ((), ()))
_NT = (((1,), (1,)), ((), ()))
_TN = (((0,), (0,)), ((), ()))
_BNN = (((2,), (1,)), ((0,), (0,)))
_BNT = (((2,), (2,)), ((0,), (0,)))
_BTN = (((1,), (1,)), ((0,), (0,)))


@jax.custom_vjp
def mm_nn(a, b):
    return _raw(a, b, _NN)


mm_nn.defvjp(lambda a, b: (_raw(a, b, _NN), (a, b)),
             lambda r, g: (_raw(g, r[1], _NT), _raw(r[0], g, _TN)))


@jax.custom_vjp
def mm_nc(a, b):
    return _raw(a, b, _NN)


mm_nc.defvjp(lambda a, b: (_raw(a, b, _NN), b),
             lambda b, g: (_raw(g, b, _NT), jnp.zeros_like(b)))


@jax.custom_vjp
def mm_nt(a, b):
    return _raw(a, b, _NT)


mm_nt.defvjp(lambda a, b: (_raw(a, b, _NT), (a, b)),
             lambda r, g: (_raw(g, r[1], _NN), _raw(g, r[0], _TN)))


@jax.custom_vjp
def bmm_nn(a, b):
    return _raw(a, b, _BNN)


bmm_nn.defvjp(lambda a, b: (_raw(a, b, _BNN), (a, b)),
              lambda r, g: (_raw(g, r[1], _BNT), _raw(r[0], g, _BTN)))


@jax.custom_vjp
def bmm_nt(a, b):
    return _raw(a, b, _BNT)


bmm_nt.defvjp(lambda a, b: (_raw(a, b, _BNT), (a, b)),
              lambda r, g: (_raw(g, r[1], _BNN), _raw(g, r[0], _BTN)))


@jax.custom_vjp
def softplus(x):
    t = jnp.exp(-jnp.abs(x))
    u = 1.0 + t
    one = u == 1.0
    l1p = jnp.where(one, t, jnp.log(u) * (t / jnp.where(one, 1.0, u - 1.0)))
    return jnp.maximum(x, 0.0) + l1p


softplus.defvjp(lambda x: (softplus(x), x), lambda x, g: (g * _sig(x),))


def _params(*sem):
    return pltpu.CompilerParams(dimension_semantics=sem, vmem_limit_bytes=56 * 1024 * 1024)


def all_to_all(arrs, bcast, name):
    n = len(arrs)
    out_shapes = [jax.ShapeDtypeStruct(((NDEV,) + a.shape) if b else a.shape, a.dtype) for a, b in zip(arrs, bcast)]

    def body(*refs):
        ins, outs, token = refs[:n], refs[n:2 * n], refs[2 * n]
        send_sems, recv_sems, local_sems = refs[2 * n + 1:]
        me, remote = _exchange_copies(ins, outs, bcast, send_sems, recv_sems)
        local = [pltpu.make_async_copy(ins[j] if bcast[j] else ins[j].at[me], outs[j].at[me], local_sems.at[j])
                 for j in range(n)]
        for cp in local + remote:
            cp.start()
        for cp in remote + local:
            cp.wait()
        token[...] = jnp.zeros_like(token)

    any_spec = pl.BlockSpec(memory_space=pl.ANY)
    res = pl.pallas_call(
        body, name=name, out_shape=out_shapes + [jax.ShapeDtypeStruct((8, 128), F32)], in_specs=[any_spec] * n,
        out_specs=[any_spec] * n + [pl.BlockSpec(memory_space=pltpu.VMEM)],
        scratch_shapes=[pltpu.SemaphoreType.DMA((7 * n,)), pltpu.SemaphoreType.DMA((7 * n,)),
                        pltpu.SemaphoreType.DMA((n,))],
        compiler_params=pltpu.CompilerParams(has_side_effects=True),
    )(*arrs)
    return res[:n], res[n]


def _peers():
    x, y, c = lax.axis_index("x"), lax.axis_index("y"), lax.axis_index("c")
    out = []
    for k in range(1, NDEV):
        px, py, pc = x ^ ((k >> 2) & 1), y ^ ((k >> 1) & 1), c ^ (k & 1)
        out.append(((px, py, pc), 4 * px + 2 * py + pc))
    return 4 * x + 2 * y + c, out


COPIES = {"all": 7, "chips": 3, "pass": 4}


def _exchange_copies(ins, lands, bcast, send_sems, recv_sems, mode="all"):
    x, y, c = lax.axis_index("x"), lax.axis_index("y"), lax.axis_index("c")
    me = 4 * x + 2 * y + c
    n, copies = len(ins), []

    def add(q, j, src, dst, dev):
        copies.append(pltpu.make_async_remote_copy(
            src_ref=src, dst_ref=dst, send_sem=send_sems.at[q * n + j], recv_sem=recv_sems.at[q * n + j],
            device_id=dev, device_id_type=pl.DeviceIdType.MESH))

    if mode == "pass":
        for q in range(4):
            slot = 4 * (x ^ (q >> 1)) + 2 * (y ^ (q & 1)) + c
            for j in range(n):
                add(q, j, ins[j] if q == 0 else lands[j].at[slot], lands[j].at[slot], (x, y, 1 - c))
        return me, copies
    for q, k in enumerate(range(1, NDEV) if mode == "all" else (2, 4, 6)):
        px, py, pc = x ^ ((k >> 2) & 1), y ^ ((k >> 1) & 1), c ^ (k & 1)
        for j in range(n):
            add(q, j, ins[j] if bcast[j] else ins[j].at[4 * px + 2 * py + pc], lands[j].at[me], (px, py, pc))
    return me, copies


_HBM = pl.BlockSpec(memory_space=pltpu.HBM)
_SEM = pl.BlockSpec(memory_space=pltpu.SEMAPHORE)
_EFFECT = pltpu.SideEffectType.DATAFLOW_SIDE_EFFECTING


def exchange_start(arrs, bcast, name, mode="all", lands=None):
    n, ncp = len(arrs), COPIES[mode] * len(arrs)
    land_shapes = [((NDEV,) + a.shape) if b else a.shape for a, b in zip(arrs, bcast)]
    if lands is None:
        lands = [lax.empty(s_, a.dtype) for s_, a in zip(land_shapes, arrs)]

    def body(*refs):
        in_refs, land_refs = refs[:n], refs[n:2 * n]
        send_sems, recv_sems = refs[2 * n], refs[2 * n + 1]
        token = refs[-1]
        _, copies = _exchange_copies(in_refs, land_refs, bcast, send_sems, recv_sems, mode)
        for cp in copies:
            cp.start()
        token[...] = jnp.zeros_like(token)

    hbm = lambda shp, a: pltpu.HBM(shp, a.dtype)
    res = pl.pallas_call(
        body, name=name,
        out_shape=[pltpu.SemaphoreType.DMA((ncp,)), pltpu.SemaphoreType.DMA((ncp,))]
                  + [hbm(a.shape, a) for a in arrs] + [hbm(s_, a) for s_, a in zip(land_shapes, arrs)]
                  + [jax.ShapeDtypeStruct((8, 128), F32)],
        in_specs=[_HBM] * (2 * n), out_specs=[_SEM, _SEM] + [_HBM] * (2 * n) + [pl.BlockSpec(memory_space=pltpu.VMEM)],
        input_output_aliases={i: 2 + i for i in range(2 * n)},
        compiler_params=pltpu.CompilerParams(has_side_effects=_EFFECT),
    )(*[pltpu.with_memory_space_constraint(a, pltpu.HBM) for a in arrs],
      *[pltpu.with_memory_space_constraint(a, pltpu.HBM) for a in lands])
    return (res[0], res[1], res[2:2 + n], res[2 + n:2 + 2 * n], tuple(bcast), mode), res[-1]


def exchange_wait(state, after, name):
    send_sems, recv_sems, ins, lands, bcast, mode = state
    n = len(ins)

    def body(*refs):
        in_refs, land_refs = refs[:n], refs[n:2 * n]
        s_sems, r_sems = refs[2 * n], refs[2 * n + 1]
        token = refs[-1]
        _, copies = _exchange_copies(in_refs, land_refs, bcast, s_sems, r_sems, mode)
        for cp in copies:
            cp.wait_send()
            cp.wait_recv()
        token[...] = jnp.zeros_like(token)

    res = pl.pallas_call(
        body, name=name,
        out_shape=[pltpu.HBM(a.shape, a.dtype) for a in ins] + [pltpu.HBM(a.shape, a.dtype) for a in lands]
                  + [jax.ShapeDtypeStruct((8, 128), F32)],
        in_specs=[_HBM] * (2 * n) + [_SEM, _SEM, pl.BlockSpec(memory_space=pl.ANY)],
        out_specs=[_HBM] * (2 * n) + [pl.BlockSpec(memory_space=pltpu.VMEM)],
        input_output_aliases={i: i for i in range(2 * n)},
        compiler_params=pltpu.CompilerParams(has_side_effects=_EFFECT),
    )(*ins, *lands, send_sems, recv_sems, after)
    if mode == "chips":
        return list(res[n:2 * n]), res[-1], list(res[:n])
    me = 4 * lax.axis_index("x") + 2 * lax.axis_index("y") + lax.axis_index("c")
    got = []
    for j in range(n):
        own = res[j][None] if bcast[j] else lax.dynamic_index_in_dim(res[j], me, 0, keepdims=True)
        got.append(lax.dynamic_update_slice_in_dim(res[n + j], own, me, axis=0))
    return got, res[-1], list(res[:n])


def gather_start(shards, name):
    return exchange_start(shards, [True] * len(shards), name + "_chips_start", mode="chips")


def gather_finish(state, after, name):
    lands, _, sent = exchange_wait(state, after, name + "_chips_wait")
    state, tok = exchange_start(sent, [True] * len(sent), name + "_pass_start", mode="pass", lands=lands)
    got, tok, _ = exchange_wait(state, tok, name + "_pass_wait")
    return got, tok


def norm_proj_fwd(x, vec, w, name):
    s, n = x.shape[0], w.shape[1]
    tr, tn = _pick(s, 512), _pick(n, 2560)
    ni, jdt, odt = s // tr, O_DT // tn, O_DT % tn

    def body(x_ref, v_ref, w_ref, o_ref, h_ref, dt_ref, h_scr):
        j, i = pl.program_id(0), pl.program_id(1)
        rows = pl.ds(pl.multiple_of(i * tr, tr), tr)

        @pl.when(j == 0)
        def _():
            h = _rms(x_ref[...], v_ref[0:1, :], D) * (1.0 + v_ref[2:3, :]) + v_ref[1:2, :]
            h_scr[rows, :] = h.astype(BF16)
            h_ref[...] = h.astype(BF16)
        res = jnp.dot(h_scr[rows, :], w_ref[...], preferred_element_type=F32)
        o_ref[...] = res

        @pl.when(j == jdt)
        def _():
            dt_ref[...] = res[:, odt:odt + 128]

    first = lambda j, i: (jnp.where(j == 0, i, ni - 1), 0)
    dtix = lambda j, i: (jnp.where(j < jdt, 0, jnp.where(j == jdt, i, ni - 1)), 0)
    return pl.pallas_call(
        body, name=name, grid=(n // tn, ni),
        in_specs=[pl.BlockSpec((tr, D), first), pl.BlockSpec((8, D), lambda j, i: (0, 0)),
                  pl.BlockSpec((D, tn), lambda j, i: (0, j))],
        out_specs=[pl.BlockSpec((tr, tn), lambda j, i: (i, j)), pl.BlockSpec((tr, D), first),
                   pl.BlockSpec((tr, 128), dtix)],
        out_shape=[jax.ShapeDtypeStruct((s, n), F32), jax.ShapeDtypeStruct((s, D), BF16),
                   jax.ShapeDtypeStruct((s, 128), F32)],
        scratch_shapes=[pltpu.VMEM((s, D), BF16)],
        compiler_params=_params("arbitrary", "arbitrary"),
    )(x, vec, w)


def _col_tiles(arr, cap):
    if arr.ndim == 2:
        n = arr.shape[1]
        t = _pick(n, cap)
        return n, t, lambda rows, ix: pl.BlockSpec((rows, t), lambda *g: ix(*g))
    width = arr.shape[2]
    t = _pick(width, cap)
    per = width // t

    def spec(rows, ix):
        def index(*g):
            r, j = ix(*g)
            return (j // per, r, j % per)
        return pl.BlockSpec((None, rows, t), index)
    return arr.shape[0] * width, t, spec


def norm_proj_bwd(x, vec, dp, w, dx_in, aux, name):
    s = x.shape[0]
    tr = _pick(s, 512)
    n, tk, dp_spec = _col_tiles(dp, 2560)
    nk, has_aux = n // tk, aux is not None

    def body(*refs):
        if has_aux:
            x_ref, v_ref, dp_ref, w_ref, dxin_ref, aux_ref, dx_ref, dv_ref, acc = refs
        else:
            x_ref, v_ref, dp_ref, w_ref, dxin_ref, dx_ref, dv_ref, acc = refs
        k, i = pl.program_id(0), pl.program_id(1)
        rows = pl.ds(pl.multiple_of(i * tr, tr), tr)
        part = _raw(dp_ref[...], w_ref[...], _NT)

        @pl.when(k == 0)
        def _():
            acc[rows, :] = part

        @pl.when(k > 0)
        def _():
            acc[rows, :] += part

        @pl.when(k == nk - 1)
        def _():
            f = lambda xx, nw, sh, sc: _rms(xx, nw, D) * (1.0 + sc) + sh
            _, vjp = jax.vjp(f, x_ref[...], v_ref[0:1, :], v_ref[1:2, :], v_ref[2:3, :])
            dx, dnw, dsh, dsc = vjp(acc[rows, :])
            dx_ref[...] = dxin_ref[...] + dx

            @pl.when(i == 0)
            def _():
                dv_ref[...] = jnp.zeros_like(dv_ref)

            dv_ref[0:1, :] += dnw
            dv_ref[1:2, :] += dsh
            dv_ref[2:3, :] += dsc
            if has_aux:
                dv_ref[3:4, :] += jnp.sum(dxin_ref[...] * aux_ref[...], axis=0, keepdims=True)

    row = pl.BlockSpec((tr, D), lambda k, i: (jnp.where(k == nk - 1, i, 0), 0))
    in_specs = [row, pl.BlockSpec((8, D), lambda k, i: (0, 0)), dp_spec(tr, lambda k, i: (i, k)),
                pl.BlockSpec((D, tk), lambda k, i: (0, k)), row] + ([row] if has_aux else [])
    args = [x, vec, dp, w, dx_in] + ([aux] if has_aux else [])
    return pl.pallas_call(
        body, name=name, grid=(nk, s // tr), in_specs=in_specs,
        out_specs=[row, pl.BlockSpec((8, D), lambda k, i: (0, 0))],
        out_shape=[jax.ShapeDtypeStruct((s, D), F32), jax.ShapeDtypeStruct((8, D), F32)],
        scratch_shapes=[pltpu.VMEM((s, D), F32)],
        compiler_params=_params("arbitrary", "arbitrary"),
    )(*args)


def tn_matmul(a, b, name, scale=None, out_dtype=None):
    out_dtype = BF16 if out_dtype is None else out_dtype
    s, m = a.shape
    ts, tm = _pick(s, 512, 16), _pick(m, 1408)
    n, tn, b_spec = _col_tiles(b, 2560)
    ns, has_scale = s // ts, scale is not None

    def body(*refs):
        if has_scale:
            a_ref, b_ref, sc_ref, o_ref, acc = refs
        else:
            a_ref, b_ref, o_ref, acc = refs
        k = pl.program_id(2)

        @pl.when(k == 0)
        def _():
            acc[...] = jnp.zeros_like(acc)

        acc[...] += _raw(a_ref[...], b_ref[...], _TN)

        @pl.when(k == ns - 1)
        def _():
            o_ref[...] = (acc[...] * sc_ref[...] if has_scale else acc[...]).astype(out_dtype)

    in_specs = [pl.BlockSpec((ts, tm), lambda i, j, k: (k, i)), b_spec(ts, lambda i, j, k: (k, j))]
    if has_scale:
        in_specs.append(pl.BlockSpec((1, tn), lambda i, j, k: (0, j)))
    return pl.pallas_call(
        body, name=name, grid=(m // tm, n // tn, ns), in_specs=in_specs,
        out_specs=pl.BlockSpec((tm, tn), lambda i, j, k: (i, j)),
        out_shape=jax.ShapeDtypeStruct((m, n), out_dtype),
        scratch_shapes=[pltpu.VMEM((tm, tn), F32)],
        compiler_params=_params("arbitrary", "arbitrary", "arbitrary"),
    )(*([a, b] + ([scale] if has_scale else [])))


def ada_mod(c16, w):
    ncol = w.shape[2]

    def body(c_ref, w_ref, o_ref, a_ref):
        cc = c_ref[...]
        act = cc * _sig(cc)
        a_ref[...] = act
        o_ref[...] = _raw(act, w_ref[...], _NN)

    return pl.pallas_call(
        body, name="ada_mod", grid=(LAYERS,),
        in_specs=[pl.BlockSpec((16, D), lambda l: (0, 0)), pl.BlockSpec((None, D, ncol), lambda l: (l, 0, 0))],
        out_specs=[pl.BlockSpec((None, 16, ncol), lambda l: (l, 0, 0)), pl.BlockSpec((16, D), lambda l: (0, 0))],
        out_shape=[jax.ShapeDtypeStruct((LAYERS, 16, ncol), F32), jax.ShapeDtypeStruct((16, D), F32)],
        compiler_params=_params("arbitrary"),
    )(c16, w)


def _mla_shared(q_lat, c_kv, kr, krs, qa_w, kva_w, kr_w, krs_w, cos2, sin2):
    qn = _rms(q_lat, qa_w, 384.0)
    kvn = _rms(c_kv, kva_w, 256.0)
    rk = lax.rsqrt(jnp.sum(kr * kr, axis=-1, keepdims=True) / 32.0 + EPS)
    krope = rk * (kr * kr_w * cos2 + krs * krs_w * sin2)
    return qn, kvn, krope


def _mla_head(qn, kvn, wqn, wqr, wqrs, wkn, wv, qn_w, qr_w, qrs_w, kn_w, cos2, sin2):
    qnope = _rms(mm_nn(qn, wqn), qn_w, 64.0)
    qr, qrs = mm_nn(qn, wqr), mm_nn(qn, wqrs)
    rq = lax.rsqrt(jnp.sum(qr * qr, axis=-1, keepdims=True) / 32.0 + EPS)
    qrope = rq * (qr * qr_w * cos2 + qrs * qrs_w * sin2)
    knope = _rms(mm_nn(kvn, wkn), kn_w, 64.0)
    return qnope, qrope, knope, mm_nn(kvn, wv)


def _mla_vec_pieces(v_ref):
    return ((v_ref[0:1, 0:384], v_ref[1:2, 0:256], v_ref[3:4, 128:256], v_ref[3:4, 256:384]),
            (v_ref[2:3, 0:128], v_ref[2:3, 128:256], v_ref[2:3, 256:384], v_ref[3:4, 0:128]))


def _mla_in_specs(tr):
    return [pl.BlockSpec((tr, 384), lambda i: (i, O_QL // 384)), pl.BlockSpec((tr, 256), lambda i: (i, O_CKV // 256)),
            pl.BlockSpec((tr, 128), lambda i: (i, O_KR // 128)), pl.BlockSpec((tr, 128), lambda i: (i, O_KRS // 128)),
            pl.BlockSpec((HEADS, 384, 384), lambda i: (0, 0, 0), **CONST),
            pl.BlockSpec((HEADS, 256, 256), lambda i: (0, 0, 0), **CONST),
            pl.BlockSpec((8, 512), lambda i: (0, 0)),
            pl.BlockSpec((tr, 128), lambda i: (i, 0)), pl.BlockSpec((tr, 128), lambda i: (i, 0))]


def mla_pre_fwd(proj, wq, wkv, vec, cos2, sin2, name):
    s = proj.shape[0]
    tr = _pick(s, 256)

    def body(ql_ref, ckv_ref, kr_ref, krs_ref, wq_ref, wkv_ref, v_ref, cos_ref, sin_ref, q_out, k_out, v_out):
        vshared, vhead = _mla_vec_pieces(v_ref)
        cos2_, sin2_ = cos_ref[...], sin_ref[...]
        qlat_n, kv_n, krope = _mla_shared(ql_ref[...], ckv_ref[...], kr_ref[...], krs_ref[...], *vshared, cos2_, sin2_)
        qlat_n, kv_n, krope = qlat_n.astype(BF16), kv_n.astype(BF16), krope.astype(BF16)
        for h in range(HEADS):
            ws = (wq_ref[h, :, 0:128], wq_ref[h, :, 128:256], wq_ref[h, :, 256:384],
                  wkv_ref[h, :, 0:128], wkv_ref[h, :, 128:256])
            qn, qr, kn, v = _mla_head(qlat_n, kv_n, *ws, *vhead, cos2_, sin2_)
            q_out[h, :, 0:128] = qn.astype(BF16)
            q_out[h, :, 128:256] = qr.astype(BF16)
            k_out[h, :, 0:128] = kn.astype(BF16)
            k_out[h, :, 128:256] = krope
            v_out[h] = v.astype(BF16)

    return pl.pallas_call(
        body, name=name, grid=(s // tr,), in_specs=_mla_in_specs(tr),
        out_specs=[pl.BlockSpec((HEADS, tr, 256), lambda i: (0, i, 0)), pl.BlockSpec((HEADS, tr, 256), lambda i: (0, i, 0)),
                   pl.BlockSpec((HEADS, tr, 128), lambda i: (0, i, 0))],
        out_shape=[jax.ShapeDtypeStruct((HEADS, s, 256), BF16), jax.ShapeDtypeStruct((HEADS, s, 256), BF16),
                   jax.ShapeDtypeStruct((HEADS, s, 128), BF16)],
        compiler_params=_params("arbitrary"),
    )(proj, proj, proj, proj, wq, wkv, vec, cos2, sin2)


def mla_pre_bwd(proj, wq, wkv, vec, cos2, sin2, dq, dk, dv, name):
    s = proj.shape[0]
    tr = _pick(s, 256)

    def body(ql_ref, ckv_ref, kr_ref, krs_ref, wq_ref, wkv_ref, v_ref, cos_ref, sin_ref, dq_ref, dk_ref, dv_ref,
             dql_out, dckv_out, dkr_out, dkrs_out, dwq_out, dwkv_out, dvec_out):
        @pl.when(pl.program_id(0) == 0)
        def _():
            dwq_out[...] = jnp.zeros_like(dwq_out)
            dwkv_out[...] = jnp.zeros_like(dwkv_out)
            dvec_out[...] = jnp.zeros_like(dvec_out)

        vshared, vhead = _mla_vec_pieces(v_ref)
        cos2_, sin2_ = cos_ref[...], sin_ref[...]
        fs = lambda *a: _mla_shared(*a, cos2_, sin2_)
        (qlat_n, kv_n, _), vjp_shared = jax.vjp(fs, ql_ref[...], ckv_ref[...], kr_ref[...], krs_ref[...], *vshared)

        def head(h, carry):
            wq_h, wkv_h = wq_ref[h].astype(F32), wkv_ref[h].astype(F32)
            ws = (wq_h[:, 0:128], wq_h[:, 128:256], wq_h[:, 256:384], wkv_h[:, 0:128], wkv_h[:, 128:256])
            f = lambda *a: _mla_head(*a, cos2_, sin2_)
            _, vjp = jax.vjp(f, qlat_n, kv_n, *ws, *vhead)
            dq_h, dk_h = dq_ref[h], dk_ref[h]
            g = vjp((dq_h[:, 0:128], dq_h[:, 128:256], dk_h[:, 0:128], dv_ref[h]))
            dwq_out[h, :, 0:128] += g[2]
            dwq_out[h, :, 128:256] += g[3]
            dwq_out[h, :, 256:384] += g[4]
            dwkv_out[h, :, 0:128] += g[5]
            dwkv_out[h, :, 128:256] += g[6]
            dvec_out[2:3, 0:128] += g[7]
            dvec_out[2:3, 128:256] += g[8]
            dvec_out[2:3, 256:384] += g[9]
            dvec_out[3:4, 0:128] += g[10]
            return carry[0] + g[0], carry[1] + g[1], carry[2] + dk_h[:, 128:256]

        zero = lambda w: jnp.zeros((tr, w), F32)
        dqn, dkvn, dkrope = lax.fori_loop(0, HEADS, head, (zero(384), zero(256), zero(128)))
        g = vjp_shared((dqn, dkvn, dkrope))
        dql_out[...] = g[0].astype(BF16)
        dckv_out[...] = g[1].astype(BF16)
        dkr_out[...] = g[2].astype(BF16)
        dkrs_out[...] = g[3].astype(BF16)
        dvec_out[0:1, 0:384] += g[4]
        dvec_out[1:2, 0:256] += g[5]
        dvec_out[3:4, 128:256] += g[6]
        dvec_out[3:4, 256:384] += g[7]

    hb = lambda w: pl.BlockSpec((HEADS, tr, w), lambda i: (0, i, 0))
    return pl.pallas_call(
        body, name=name, grid=(s // tr,), in_specs=_mla_in_specs(tr) + [hb(256), hb(256), hb(128)],
        out_specs=[pl.BlockSpec((tr, 384), lambda i: (i, 0)), pl.BlockSpec((tr, 256), lambda i: (i, 0)),
                   pl.BlockSpec((tr, 128), lambda i: (i, 0)), pl.BlockSpec((tr, 128), lambda i: (i, 0)),
                   pl.BlockSpec((HEADS, 384, 384), lambda i: (0, 0, 0)), pl.BlockSpec((HEADS, 256, 256), lambda i: (0, 0, 0)),
                   pl.BlockSpec((8, 512), lambda i: (0, 0))],
        out_shape=[jax.ShapeDtypeStruct((s, 384), BF16), jax.ShapeDtypeStruct((s, 256), BF16),
                   jax.ShapeDtypeStruct((s, 128), BF16), jax.ShapeDtypeStruct((s, 128), BF16),
                   jax.ShapeDtypeStruct((HEADS, 384, 384), F32), jax.ShapeDtypeStruct((HEADS, 256, 256), F32),
                   jax.ShapeDtypeStruct((8, 512), F32)],
        compiler_params=_params("arbitrary"),
    )(proj, proj, proj, proj, wq, wkv, vec, cos2, sin2, dq, dk, dv)


def _att_probs(q, kk, i, tq):
    sc = _raw(q, kk, _NT) * ATT_SCALE
    rows = lax.broadcasted_iota(jnp.int32, sc.shape, 0) + i * tq
    cols = lax.broadcasted_iota(jnp.int32, sc.shape, 1)
    sc = jnp.where(cols <= rows, sc, -jnp.inf)
    e = jnp.exp(sc - jnp.max(sc, axis=-1, keepdims=True))
    return e / jnp.sum(e, axis=-1, keepdims=True)


def mla_attn_fwd(q, k, v, name):
    s = q.shape[1]
    tq = _pick(s, 256)

    def body(q_ref, k_ref, v_ref, o_ref):
        for i in range(s // tq):
            n = (i + 1) * tq
            p = _att_probs(q_ref[i * tq:n, :], k_ref[0:n, :], i, tq)
            o_ref[i * tq:n, :] = _raw(p, v_ref[0:n, :], _NN)

    hs = lambda w: pl.BlockSpec((None, s, w), lambda h: (h, 0, 0))
    return pl.pallas_call(
        body, name=name, grid=(HEADS,), in_specs=[hs(256), hs(256), hs(128)],
        out_specs=pl.BlockSpec((s, 128), lambda h: (0, h)),
        out_shape=jax.ShapeDtypeStruct((s, HEADS * 128), F32),
        compiler_params=_params("arbitrary"),
    )(q, k, v)


def mla_attn_bwd(q, k, v, do, name):
    s = q.shape[1]
    tq = _pick(s, 256)

    def body(q_ref, k_ref, v_ref, do_ref, dq_ref, dk_ref, dv_ref):
        dk_ref[...] = jnp.zeros_like(dk_ref)
        dv_ref[...] = jnp.zeros_like(dv_ref)
        for i in range(s // tq):
            n = (i + 1) * tq
            qq, kk, vv = q_ref[i * tq:n, :], k_ref[0:n, :], v_ref[0:n, :]
            p = _att_probs(qq, kk, i, tq)
            o = _raw(p, vv, _NN)
            dout = do_ref[i * tq:n, :]
            delta = jnp.sum(dout * o, axis=-1, keepdims=True)
            dp = _raw(dout, vv, _NT)
            ds = p * (dp - delta) * ATT_SCALE
            dq_ref[i * tq:n, :] = _raw(ds, kk, _NN)
            dk_ref[0:n, :] += _raw(ds, qq, _TN)
            dv_ref[0:n, :] += _raw(p, dout, _TN)

    hs = lambda w: pl.BlockSpec((None, s, w), lambda h: (h, 0, 0))
    return pl.pallas_call(
        body, name=name, grid=(HEADS,),
        in_specs=[hs(256), hs(256), hs(128), pl.BlockSpec((s, 128), lambda h: (0, h))],
        out_specs=[hs(256), hs(256), hs(128)],
        out_shape=[jax.ShapeDtypeStruct((HEADS, s, 256), F32), jax.ShapeDtypeStruct((HEADS, s, 256), F32),
                   jax.ShapeDtypeStruct((HEADS, s, 128), F32)],
        compiler_params=_params("arbitrary"),
    )(q, k, v, do)


def _pool_windows(u, pad, s, g):
    pad[0:16, :] = jnp.zeros((16, 128), F32)
    cur, sel = u, None
    for j, k in enumerate((1, 2, 4, 8)):
        pad[16:16 + s, :] = cur
        cur = cur + pad[16 - k:16 - k + s, :]
        sel = cur if sel is None else jnp.where(g == j, cur, sel)
    return sel


def _pool_count(s, g):
    t = lax.broadcasted_iota(jnp.int32, (s, 1), 0)
    return jnp.minimum(t + 1, 2 << g).astype(F32)


def pool_fwd(proj, pw, ps, name):
    s = proj.shape[0]

    def body(u_ref, w_ref, s_ref, o_ref, pad):
        g = pl.program_id(0)
        u = u_ref[...]
        pooled = _pool_windows(u, pad, s, g) / _pool_count(s, g) - u
        o_ref[...] = _raw(pooled, w_ref[...], _NN) * s_ref[...]

    return pl.pallas_call(
        body, name=name, grid=(4,),
        in_specs=[pl.BlockSpec((s, 128), lambda g: (0, O_PU // 128 + g)), pl.BlockSpec((None, 128, 128), lambda g: (g, 0, 0)),
                  pl.BlockSpec((1, 128), lambda g: (0, g))],
        out_specs=pl.BlockSpec((s, 128), lambda g: (0, g)),
        out_shape=jax.ShapeDtypeStruct((s, 512), F32),
        scratch_shapes=[pltpu.VMEM((s + 16, 128), F32)],
        compiler_params=_params("arbitrary"),
    )(proj, pw, ps)


def pool_bwd(proj, pw, ps, do, name):
    s = proj.shape[0]

    def body(u_ref, w_ref, s_ref, do_ref, du_ref, dw_ref, ds_ref, pad):
        g = pl.program_id(0)
        u, w, dout = u_ref[...], w_ref[...], do_ref[...]
        cnt = _pool_count(s, g)
        pooled = _pool_windows(u, pad, s, g) / cnt - u
        mixed = _raw(pooled, w, _NN)
        ds_ref[...] = jnp.sum(dout * mixed, axis=0, keepdims=True)
        dmixed = dout * s_ref[...]
        dw_ref[...] = _raw(pooled, dmixed, _TN)
        dpooled = _raw(dmixed, w, _NT)
        dsel = dpooled / cnt
        pad[s:s + 16, :] = jnp.zeros((16, 128), F32)
        cur = jnp.where(g == 3, dsel, 0.0)
        for j, k in ((2, 8), (1, 4), (0, 2)):
            pad[0:s, :] = cur
            cur = cur + pad[k:k + s, :] + jnp.where(g == j, dsel, 0.0)
        pad[0:s, :] = cur
        cur = cur + pad[1:1 + s, :]
        du_ref[...] = (cur - dpooled).astype(BF16)

    return pl.pallas_call(
        body, name=name, grid=(4,),
        in_specs=[pl.BlockSpec((s, 128), lambda g: (0, O_PU // 128 + g)), pl.BlockSpec((None, 128, 128), lambda g: (g, 0, 0)),
                  pl.BlockSpec((1, 128), lambda g: (0, g)), pl.BlockSpec((s, 128), lambda g: (0, g))],
        out_specs=[pl.BlockSpec((s, 128), lambda g: (0, g)), pl.BlockSpec((None, 128, 128), lambda g: (g, 0, 0)),
                   pl.BlockSpec((1, 128), lambda g: (0, g))],
        out_shape=[jax.ShapeDtypeStruct((s, 512), BF16), jax.ShapeDtypeStruct((4, 128, 128), F32),
                   jax.ShapeDtypeStruct((1, 512), F32)],
        scratch_shapes=[pltpu.VMEM((s + 16, 128), F32)],
        compiler_params=_params("arbitrary"),
    )(proj, pw, ps, do)


def _xbc_col(i):
    return jnp.where(i < 2, O_XS // 512 + i, O_BC // 512)


def conv_fwd(proj, cw, cb, name):
    s = proj.shape[0]

    def body(x_ref, w_ref, b_ref, o_ref, t_ref, pad):
        pad[0:8, :] = jnp.zeros((8, 512), F32)
        pad[8:8 + s, :] = x_ref[...]
        y = b_ref[...] + sum(w_ref[k:k + 1, :] * pad[5 + k:5 + k + s, :] for k in range(4))
        act = y * _sig(y)
        o_ref[...] = act

        @pl.when(pl.program_id(0) < 2)
        def _():
            t_ref[...] = act.T

    return pl.pallas_call(
        body, name=name, grid=(3,),
        in_specs=[pl.BlockSpec((s, 512), lambda i: (0, _xbc_col(i))), pl.BlockSpec((4, 512), lambda i: (0, i)),
                  pl.BlockSpec((1, 512), lambda i: (0, i))],
        out_specs=[pl.BlockSpec((s, 512), lambda i: (0, i)), pl.BlockSpec((512, s), lambda i: (jnp.minimum(i, 1), 0))],
        out_shape=[jax.ShapeDtypeStruct((s, 1536), F32), jax.ShapeDtypeStruct((D, s), F32)],
        scratch_shapes=[pltpu.VMEM((s + 8, 512), F32)],
        compiler_params=_params("arbitrary"),
    )(proj, cw, cb)


def conv_bwd(proj, cw, cb, dxt, dbm, dcm, name):
    s = proj.shape[0]

    def body(x_ref, w_ref, b_ref, dxt_ref, dbm_ref, dcm_ref, dx_ref, dw_ref, db_ref, pad, pad2):
        pad[0:8, :] = jnp.zeros((8, 512), F32)
        pad[8:8 + s, :] = x_ref[...]
        y = b_ref[...] + sum(w_ref[k:k + 1, :] * pad[5 + k:5 + k + s, :] for k in range(4))
        sg = _sig(y)

        @pl.when(pl.program_id(0) < 2)
        def _():
            pad2[0:s, :] = dxt_ref[...].T

        @pl.when(pl.program_id(0) == 2)
        def _():
            pad2[0:s, 0:256] = dbm_ref[...]
            pad2[0:s, 256:512] = dcm_ref[...]

        dy = pad2[0:s, :] * (sg * (1.0 + y * (1.0 - sg)))
        db_ref[...] = jnp.sum(dy, axis=0, keepdims=True)
        for k in range(4):
            dw_ref[k:k + 1, :] = jnp.sum(dy * pad[5 + k:5 + k + s, :], axis=0, keepdims=True)
        pad2[s:s + 8, :] = jnp.zeros((8, 512), F32)
        pad2[0:s, :] = dy
        dx_ref[...] = sum(w_ref[k:k + 1, :] * pad2[3 - k:3 - k + s, :] for k in range(4)).astype(BF16)

    return pl.pallas_call(
        body, name=name, grid=(3,),
        in_specs=[pl.BlockSpec((s, 512), lambda i: (0, _xbc_col(i))), pl.BlockSpec((4, 512), lambda i: (0, i)),
                  pl.BlockSpec((1, 512), lambda i: (0, i)), pl.BlockSpec((512, s), lambda i: (jnp.minimum(i, 1), 0)),
                  pl.BlockSpec((s, 256), lambda i: (0, 0)), pl.BlockSpec((s, 256), lambda i: (0, 0))],
        out_specs=[pl.BlockSpec((s, 512), lambda i: (0, i)), pl.BlockSpec((4, 512), lambda i: (0, i)),
                   pl.BlockSpec((1, 512), lambda i: (0, i))],
        out_shape=[jax.ShapeDtypeStruct((s, 1536), BF16), jax.ShapeDtypeStruct((4, 1536), F32),
                   jax.ShapeDtypeStruct((1, 1536), F32)],
        scratch_shapes=[pltpu.VMEM((s + 8, 512), F32), pltpu.VMEM((s + 8, 512), F32)],
        compiler_params=_params("arbitrary"),
    )(proj, cw, cb, dxt, dbm, dcm)


def _ssd_chunk(xt, dtr, bm, cm, hprev, alog, dbias, dskip):
    ln = 128
    a = -jnp.exp(alog)
    dt_r = softplus(dtr + dbias)
    da_r = dt_r * a
    li = lax.broadcasted_iota(jnp.int32, (1, ln, ln), 1)
    si = lax.broadcasted_iota(jnp.int32, (1, ln, ln), 2)
    causal = si <= li
    acs_c = jnp.sum(jnp.where(causal, da_r, 0.0), axis=2, keepdims=True)
    acs_r = jnp.sum(jnp.where(li == si, acs_c, 0.0), axis=1, keepdims=True)
    acs_last = jnp.sum(da_r, axis=2, keepdims=True)
    decay = jnp.exp(jnp.where(causal, acs_c - acs_r, -jnp.inf))
    m = mm_nt(cm, bm)[None] * decay
    xdt = xt * dt_r
    y_diag = bmm_nt(xdt, m)
    bb = jnp.broadcast_to(bm[None], (8, ln, ln))
    cc = jnp.broadcast_to(cm[None], (8, ln, ln))
    states = bmm_nn(xdt * jnp.exp(acs_last - acs_r), bb)
    y_off = bmm_nt(hprev, cc) * jnp.exp(acs_r)
    hnew = hprev * jnp.exp(acs_last) + states
    return y_diag + y_off + xt * dskip, hnew


def _ssd_specs(nc, rev):
    cix = (lambda c: nc - 1 - c) if rev else (lambda c: c)
    hv = pl.BlockSpec((8, 1, 1), lambda g, c: (g, 0, 0))
    return [pl.BlockSpec((8, 64, 128), lambda g, c: (g, 0, cix(c))), pl.BlockSpec((8, 1, 128), lambda g, c: (g, 0, cix(c))),
            pl.BlockSpec((128, 128), lambda g, c: (cix(c), 8 + g)),
            pl.BlockSpec((128, 128), lambda g, c: (cix(c), 10 + g))], hv, cix


def ssd_fwd(xt, dtr, xbc, alog, dbias, dskip, name):
    s = xt.shape[2]
    nc = s // 128
    specs, hv, _ = _ssd_specs(nc, False)

    def body(x_ref, dr_ref, b_ref, c_ref, al_ref, db_ref, dk_ref, y_ref, hs_ref, h_scr):
        @pl.when(pl.program_id(1) == 0)
        def _():
            h_scr[...] = jnp.zeros_like(h_scr)
        hp = h_scr[...]
        hs_ref[...] = hp
        y, hn = _ssd_chunk(x_ref[...], dr_ref[...], b_ref[...], c_ref[...], hp, al_ref[...], db_ref[...], dk_ref[...])
        y_ref[...] = y
        h_scr[...] = hn

    return pl.pallas_call(
        body, name=name, grid=(2, nc), in_specs=specs + [hv, hv, hv],
        out_specs=[pl.BlockSpec((8, 64, 128), lambda g, c: (g, 0, c)),
                   pl.BlockSpec((None, None, 8, 64, 128), lambda g, c: (g, c, 0, 0, 0))],
        out_shape=[jax.ShapeDtypeStruct((16, 64, s), F32), jax.ShapeDtypeStruct((2, nc, 8, 64, 128), F32)],
        scratch_shapes=[pltpu.VMEM((8, 64, 128), F32)],
        compiler_params=_params("arbitrary", "arbitrary"),
    )(xt, dtr, xbc, xbc, alog, dbias, dskip)


def ssd_bwd(xt, dtr, xbc, alog, dbias, dskip, hs, dyt, name):
    s = xt.shape[2]
    nc = s // 128
    specs, hv, cix = _ssd_specs(nc, True)

    def body(x_ref, dr_ref, b_ref, c_ref, al_ref, db_ref, dk_ref, hs_ref, dy_ref,
             dx_out, ddr_out, dbm_out, dcm_out, dal_out, ddb_out, ddk_out, dh_scr):
        @pl.when(pl.program_id(1) == 0)
        def _():
            dh_scr[...] = jnp.zeros_like(dh_scr)
            dal_out[...] = jnp.zeros_like(dal_out)
            ddb_out[...] = jnp.zeros_like(ddb_out)
            ddk_out[...] = jnp.zeros_like(ddk_out)
        _, vjp = jax.vjp(_ssd_chunk, x_ref[...], dr_ref[...], b_ref[...], c_ref[...], hs_ref[...],
                         al_ref[...], db_ref[...], dk_ref[...])
        g = vjp((dy_ref[...], dh_scr[...]))
        dx_out[...] = g[0]
        ddr_out[...] = g[1]
        dbm_out[...] = g[2]
        dcm_out[...] = g[3]
        dh_scr[...] = g[4]
        dal_out[...] += g[5]
        ddb_out[...] += g[6]
        ddk_out[...] += g[7]

    return pl.pallas_call(
        body, name=name, grid=(2, nc),
        in_specs=specs + [hv, hv, hv, pl.BlockSpec((None, None, 8, 64, 128), lambda g, c: (g, cix(c), 0, 0, 0)),
                          pl.BlockSpec((8, 64, 128), lambda g, c: (g, 0, cix(c)))],
        out_specs=[pl.BlockSpec((8, 64, 128), lambda g, c: (g, 0, cix(c))), pl.BlockSpec((8, 1, 128), lambda g, c: (g, 0, cix(c))),
                   pl.BlockSpec((128, 128), lambda g, c: (cix(c), g)),
                   pl.BlockSpec((128, 128), lambda g, c: (cix(c), g)), hv, hv, hv],
        out_shape=[jax.ShapeDtypeStruct((16, 64, s), F32), jax.ShapeDtypeStruct((16, 1, s), F32),
                   jax.ShapeDtypeStruct((s, 256), F32),
                   jax.ShapeDtypeStruct((s, 256), F32)] + [jax.ShapeDtypeStruct((16, 1, 1), F32)] * 3,
        scratch_shapes=[pltpu.VMEM((8, 64, 128), F32)],
        compiler_params=_params("arbitrary", "arbitrary"),
    )(xt, dtr, xbc, xbc, alog, dbias, dskip, hs, dyt)


def _merge(oa, ob, y, z, gla, glb, glc, x, g1, nw, ea, eb, ec, eo, wba, wbb, wbc, wout):
    gated = y * (z * _sig(z))
    sq = gated * gated
    left = lax.broadcasted_iota(jnp.int32, (1, D), 1) < 512
    ms0 = jnp.sum(jnp.where(left, sq, 0.0), axis=-1, keepdims=True) / 512.0
    ms1 = jnp.sum(jnp.where(left, 0.0, sq), axis=-1, keepdims=True) / 512.0
    oc = gated * jnp.where(left, lax.rsqrt(ms0 + EPS), lax.rsqrt(ms1 + EPS)) * nw
    ya, yb, yc = mm_nc(oa, wba) + ea, mm_nc(ob, wbb) + eb, mm_nc(oc, wbc) + ec
    merged = _sig(gla) * ya + _sig(glb) * yb + _sig(glc) * yc
    x1 = x + g1 * (mm_nc(merged, wout) + eo)
    return x1, (oc, merged)


def _merge_specs(tr):
    row = lambda w: pl.BlockSpec((tr, w), lambda i: (i, 0))
    acts = [row(D), row(512), pl.BlockSpec((D, tr), lambda i: (0, i)), pl.BlockSpec((tr, D), lambda i: (i, O_Z // D)),
            pl.BlockSpec((tr, 3 * D), lambda i: (i, 0)), row(D), pl.BlockSpec((8, D), lambda i: (0, 0))]
    cst = lambda r: pl.BlockSpec((r, D), lambda i: (0, 0), **CONST)
    return acts, [cst(D), cst(512), cst(D), cst(D)], row


def merge_fwd(oa, ob, y, proj, x, mvec, wba, wbb, wbc, wout, name):
    s = x.shape[0]
    tr = _pick(s, 256)
    acts, wts, row = _merge_specs(tr)

    def body(oa_ref, ob_ref, y_ref, z_ref, gl_ref, x_ref, mv_ref, wba_ref, wbb_ref, wbc_ref, wout_ref, o_ref):
        zero = jnp.zeros((1, D), F32)
        x1, _ = _merge(oa_ref[...], ob_ref[...], y_ref[...].T, z_ref[...], gl_ref[:, 0:D], gl_ref[:, D:2 * D],
                       gl_ref[:, 2 * D:3 * D], x_ref[...], mv_ref[0:1, :], mv_ref[1:2, :], zero, zero, zero, zero,
                       wba_ref[...], wbb_ref[...], wbc_ref[...], wout_ref[...])
        o_ref[...] = x1

    return pl.pallas_call(
        body, name=name, grid=(s // tr,), in_specs=acts + wts, out_specs=row(D),
        out_shape=jax.ShapeDtypeStruct((s, D), F32), compiler_params=_params("arbitrary"),
    )(oa, ob, y, proj, proj, x, mvec, wba, wbb, wbc, wout)


def merge_bwd(oa, ob, y, proj, x, mvec, wba, wbb, wbc, wout, dx1, name):
    s = x.shape[0]
    tr = _pick(s, 128)
    acts, wts, row = _merge_specs(tr)

    def body(oa_ref, ob_ref, y_ref, z_ref, gl_ref, x_ref, mv_ref, wba_ref, wbb_ref, wbc_ref, wout_ref, dx1_ref,
             doa_o, dob_o, dy_o, dz_o, dgl_o, dx_o, dmv_o, dya_o, dyb_o, dyc_o, dpre_o, oc_o, mg_o):
        zero = jnp.zeros((tr, D), F32)
        wts_ = (wba_ref[...], wbb_ref[...], wbc_ref[...], wout_ref[...])
        f = lambda *a: _merge(*a, *wts_)
        _, vjp, (oc, merged) = jax.vjp(
            f, oa_ref[...], ob_ref[...], y_ref[...].T, z_ref[...], gl_ref[:, 0:D], gl_ref[:, D:2 * D],
            gl_ref[:, 2 * D:3 * D], x_ref[...], mv_ref[0:1, :], mv_ref[1:2, :], zero, zero, zero, zero, has_aux=True)
        g = vjp(dx1_ref[...])
        doa_o[...] = g[0]
        dob_o[...] = g[1]
        dy_o[...] = g[2].T
        dz_o[...] = g[3].astype(BF16)
        dgl_o[:, 0:D] = g[4].astype(BF16)
        dgl_o[:, D:2 * D] = g[5].astype(BF16)
        dgl_o[:, 2 * D:3 * D] = g[6].astype(BF16)
        dx_o[...] = g[7]

        @pl.when(pl.program_id(0) == 0)
        def _():
            dmv_o[...] = jnp.zeros_like(dmv_o)

        dmv_o[0:1, :] += g[8]
        dmv_o[1:2, :] += g[9]
        dya_o[...] = g[10].astype(BF16)
        dyb_o[...] = g[11].astype(BF16)
        dyc_o[...] = g[12].astype(BF16)
        dpre_o[...] = g[13].astype(BF16)
        oc_o[...] = oc.astype(BF16)
        mg_o[...] = merged.astype(BF16)

    sd = lambda w, dt: jax.ShapeDtypeStruct((s, w), dt)
    return pl.pallas_call(
        body, name=name, grid=(s // tr,), in_specs=acts + wts + [row(D)],
        out_specs=[row(D), row(512), pl.BlockSpec((D, tr), lambda i: (0, i)), row(D), row(3 * D), row(D),
                   pl.BlockSpec((8, D), lambda i: (0, 0))] + [row(D)] * 6,
        out_shape=[sd(D, F32), sd(512, F32), jax.ShapeDtypeStruct((D, s), F32), sd(D, BF16), sd(3 * D, BF16), sd(D, F32),
                   jax.ShapeDtypeStruct((8, D), F32)] + [sd(D, BF16)] * 6,
        compiler_params=_params("arbitrary"),
    )(oa, ob, y, proj, proj, x, mvec, wba, wbb, wbc, wout, dx1)


def _conv3(u_scr, w_ref, first, rows, lanes):
    return sum(w_ref[k:k + 1, :] * u_scr[first + k:first + k + rows, lanes] for k in range(3))


def _ffn_tile_specs(tf, tile):
    def at(rows, off):
        return pl.BlockSpec((rows, tf), lambda *g: (0, off + tile(*g)))
    return [at(D, 0), at(D, FFN_NT), at(3, 0), at(3, FFN_NT), at(1, 0), at(1, FFN_NT)]


def ffn_fwd(x1, fvec, wup, cw, cb, wdn, name):
    s = x1.shape[0]
    tr, tf = _pick(s, 512), FFN_TILE
    lg, lv = slice(0, tf), slice(tf, 2 * tf)

    def body(x_ref, v_ref, wg_ref, wv_ref, cwg_ref, cwv_ref, cbg_ref, cbv_ref, wd_ref, x2_ref, h_ref, pre_ref,
             h_scr, u_scr, acc):
        i, t = pl.program_id(0), pl.program_id(1)

        @pl.when(t == 0)
        def _():
            @pl.when(i == 0)
            def _():
                h_scr[0:16, :] = jnp.zeros((16, D), BF16)

            @pl.when(i > 0)
            def _():
                h_scr[0:16, :] = h_scr[tr:tr + 16, :]

            h = (_rms(x_ref[...], v_ref[0:1, :], D) * (1.0 + v_ref[2:3, :]) + v_ref[1:2, :]).astype(BF16)
            h_scr[16:16 + tr, :] = h
            h_ref[...] = h
            acc[...] = jnp.zeros_like(acc)

        u_scr[:, lg] = jnp.dot(h_scr[...], wg_ref[...], preferred_element_type=F32)
        u_scr[:, lv] = jnp.dot(h_scr[...], wv_ref[...], preferred_element_type=F32)
        cg = _conv3(u_scr, cwg_ref, 14, tr, lg) + cbg_ref[...]
        cval = _conv3(u_scr, cwv_ref, 14, tr, lv) + cbv_ref[...]
        acc[...] += _raw(cg * _sig(cg) * cval, wd_ref[...], _NN)

        @pl.when(t == FFN_NT - 1)
        def _():
            pre_ref[...] = acc[...]
            x2_ref[...] = x_ref[...] + v_ref[3:4, :] * acc[...]

    row = pl.BlockSpec((tr, D), lambda i, t: (i, 0))
    return pl.pallas_call(
        body, name=name, grid=(s // tr, FFN_NT),
        in_specs=[row, pl.BlockSpec((8, D), lambda i, t: (0, 0))] + _ffn_tile_specs(tf, lambda i, t: t)
                 + [pl.BlockSpec((tf, D), lambda i, t: (t, 0))],
        out_specs=[row, row, row],
        out_shape=[jax.ShapeDtypeStruct((s, D), F32), jax.ShapeDtypeStruct((s, D), BF16), jax.ShapeDtypeStruct((s, D), F32)],
        scratch_shapes=[pltpu.VMEM((tr + 16, D), BF16), pltpu.VMEM((tr + 16, 2 * tf), F32), pltpu.VMEM((tr, D), F32)],
        compiler_params=_params("arbitrary", "arbitrary"),
    )(x1, fvec, wup, wup, cw, cw, cb, cb, wdn)


def ffn_bwd(h2, dx2, fvec, wup, cw, cb, wdn, name):
    s = h2.shape[0]
    tr, tf = _pick(s, 512), FFN_TILE
    ni, nb = s // tr, s // 16
    lg, lv = slice(0, tf), slice(tf, 2 * tf)

    def body(hp_ref, hm_ref, hn_ref, dm_ref, dn_ref, v_ref, wg_ref, wv_ref, cwg_ref, cwv_ref, cbg_ref, cbv_ref, wd_ref,
             dup_ref, act_ref, dcw_ref, u_scr, dc_scr):
        i = pl.program_id(1)
        hfull = jnp.concatenate([jnp.where(i > 0, hp_ref[...], jnp.zeros((16, D), BF16)), hm_ref[...],
                                 jnp.where(i < ni - 1, hn_ref[...], jnp.zeros((16, D), BF16))], axis=0)
        u_scr[:, lg] = jnp.dot(hfull, wg_ref[...], preferred_element_type=F32)
        u_scr[:, lv] = jnp.dot(hfull, wv_ref[...], preferred_element_type=F32)
        cg = _conv3(u_scr, cwg_ref, 14, tr + 16, lg) + cbg_ref[...]
        cval = _conv3(u_scr, cwv_ref, 14, tr + 16, lv) + cbv_ref[...]
        g2 = v_ref[3:4, :]
        dpre = jnp.concatenate([dm_ref[...] * g2, jnp.where(i < ni - 1, dn_ref[...], 0.0) * g2], axis=0)
        dact = _raw(dpre, wd_ref[...], _NT)
        sg = _sig(cg)
        sl = cg * sg
        dc_scr[:, lg] = dact * cval * (sg * (1.0 + cg * (1.0 - sg)))
        dc_scr[:, lv] = dact * sl
        act_ref[...] = (sl * cval)[0:tr, :].astype(BF16)

        @pl.when(i == 0)
        def _():
            dcw_ref[...] = jnp.zeros_like(dcw_ref)

        for half, lanes, cw_ref in ((0, lg, cwg_ref), (1, lv, cwv_ref)):
            dup_ref[half] = sum(cw_ref[k:k + 1, :] * dc_scr[2 - k:2 - k + tr, lanes] for k in range(3)).astype(BF16)
            dcm = dc_scr[0:tr, lanes]
            for k in range(3):
                dcw_ref[half, k:k + 1, :] += jnp.sum(dcm * u_scr[14 + k:14 + k + tr, lanes], axis=0, keepdims=True)
            dcw_ref[half, 3:4, :] += jnp.sum(dcm, axis=0, keepdims=True)

    r16 = tr // 16
    prev = lambda t, i: (jnp.maximum(i * r16 - 1, 0), 0)
    nxt = lambda t, i: (jnp.minimum((i + 1) * r16, nb - 1), 0)
    main = lambda t, i: (i, 0)
    return pl.pallas_call(
        body, name=name, grid=(FFN_NT, ni),
        in_specs=[pl.BlockSpec((16, D), prev), pl.BlockSpec((tr, D), main), pl.BlockSpec((16, D), nxt),
                  pl.BlockSpec((tr, D), main), pl.BlockSpec((16, D), nxt), pl.BlockSpec((8, D), lambda t, i: (0, 0))]
                 + _ffn_tile_specs(tf, lambda t, i: t) + [pl.BlockSpec((tf, D), lambda t, i: (t, 0))],
        out_specs=[pl.BlockSpec((2, tr, tf), lambda t, i: (0, i, t)), pl.BlockSpec((tr, tf), lambda t, i: (i, t)),
                   pl.BlockSpec((2, 8, tf), lambda t, i: (0, 0, t))],
        out_shape=[jax.ShapeDtypeStruct((2, s, FFN), BF16), jax.ShapeDtypeStruct((s, FFN), BF16),
                   jax.ShapeDtypeStruct((2, 8, FFN), F32)],
        scratch_shapes=[pltpu.VMEM((tr + 32, 2 * tf), F32), pltpu.VMEM((tr + 16, 2 * tf), F32)],
        compiler_params=_params("arbitrary", "arbitrary"),
    )(h2, h2, h2, dx2, dx2, fvec, wup, wup, cw, cw, cb, cb, wdn)


def loss_head(y, target):
    s = y.shape[0]
    tr = _pick(s, 512)

    def body(y_ref, t_ref, dx_ref, l_ref):
        @pl.when(pl.program_id(0) == 0)
        def _():
            l_ref[...] = jnp.zeros_like(l_ref)
        err = y_ref[...] - t_ref[...]
        dx_ref[...] = err / float(D)
        l_ref[...] += 0.5 * jnp.sum(jnp.sum(err * err, axis=-1, keepdims=True) / float(D), axis=0, keepdims=True)

    row = pl.BlockSpec((tr, D), lambda i: (i, 0))
    return pl.pallas_call(
        body, name="loss_head", grid=(s // tr,), in_specs=[row, row],
        out_specs=[row, pl.BlockSpec((8, 128), lambda i: (0, 0))],
        out_shape=[jax.ShapeDtypeStruct((s, D), F32), jax.ShapeDtypeStruct((8, 128), F32)],
        compiler_params=_params("arbitrary"),
    )(y, target)


def adamw(parts, w, m, v, name, tok=None):
    nseg = len(parts)
    p, r, c = parts[0].shape
    tr = _pick(r, 256 if c > 128 else 2048, 8)
    ni = r // tr
    tok = jnp.zeros((8, 128), F32) if tok is None else tok

    def body(*refs):
        p_refs = refs[:nseg]
        w_ref, m_ref, v_ref, _, g_out, d_out, m_out, v_out, g_scr = refs[nseg:]
        for q in range(nseg):
            @pl.when(pl.program_id(0) == q)
            def _(q=q):
                g = p_refs[q][0].astype(F32)
                for j in range(1, p):
                    g = g + p_refs[q][j].astype(F32)
                g_scr[...] = g
        g = g_scr[...]
        mn = B1 * m_ref[...] + (1.0 - B1) * g
        vn = B2 * v_ref[...] + (1.0 - B2) * (g * g)
        m_hat = mn / (1.0 - B1 ** STEP)
        v_hat = vn / (1.0 - B2 ** STEP)
        g_out[...] = g
        d_out[...] = -LR * (m_hat / (jnp.sqrt(v_hat) + ADAM_EPS) + WD * w_ref[...])
        m_out[...] = mn
        v_out[...] = vn

    row = pl.BlockSpec((None, tr, c), lambda l, i: (l, i, 0))
    part = lambda q: pl.BlockSpec((p, tr, c), lambda l, i: (0, jnp.clip((l - q) * ni + i, 0, ni - 1), 0))
    return pl.pallas_call(
        body, name=name, grid=(nseg, ni),
        in_specs=[part(q) for q in range(nseg)] + [row, row, row, pl.BlockSpec((8, 128), lambda l, i: (0, 0))],
        out_specs=[row] * 4, out_shape=[jax.ShapeDtypeStruct((nseg, r, c), F32)] * 4,
        scratch_shapes=[pltpu.VMEM((tr, c), F32)],
        compiler_params=_params("arbitrary", "arbitrary"),
    )(*parts, w, m, v, tok)


def _padc(a, n):
    return jnp.pad(a, [(0, 0)] * (a.ndim - 1) + [(0, n - a.shape[-1])])


def _swap16(a):
    return jnp.concatenate([a[..., 16:32], a[..., 0:16]], axis=-1)


def _shard_cols(g8, a, b):
    c = g8.shape[2]
    return [g8[j][:, max(a, j * c) - j * c:min(b, (j + 1) * c) - j * c] for j in range(a // c, (b - 1) // c + 1)]


def _win_layout(g8):
    cols = lambda a, b: _shard_cols(g8, a, b)
    kr = jnp.concatenate(cols(640, 672), axis=1)
    dt = jnp.concatenate(cols(3744, 3760), axis=1)
    return jnp.concatenate(cols(3760, 6832) + cols(2208, 3232) + cols(1184, 2208) + cols(672, 1184) + cols(3232, 3744)
                           + cols(384, 640) + [_padc(kr, 128), _padc(_swap16(kr), 128), _padc(dt, 128),
                                               jnp.zeros((g8.shape[1], 128), g8.dtype)] + cols(0, 384), axis=1)


def _win_grad_shards(g):
    kr = (g[:, O_KR:O_KR + 32].astype(F32) + _swap16(g[:, O_KRS:O_KRS + 32].astype(F32))).astype(g.dtype)
    segs = [(g, O_QL, 384), (g, O_CKV, 256), (kr, 0, 32), (g, O_PU, 512), (g, O_Z, D), (g, O_XS, D), (g, O_BC, 512),
            (g, O_DT, 16), (g, O_G, 3 * D)]
    shards, width = [], sum(w for _, _, w in segs) // NDEV
    for j in range(NDEV):
        a, b, off, pieces = width * j, width * (j + 1), 0, []
        for arr, lo, w in segs:
            s0, s1 = max(a, off), min(b, off + w)
            if s0 < s1:
                pieces.append(arr[:, lo + s0 - off:lo + s1 - off])
            off += w
        shards.append(jnp.concatenate(pieces, axis=1))
    return jnp.stack(shards).astype(BF16)


def _wq_layout(w):
    w = w.reshape(384, HEADS, 96).transpose(1, 0, 2)
    rope = w[:, :, 64:96]
    return jnp.concatenate([_padc(w[:, :, 0:64], 128), _padc(rope, 128), _padc(_swap16(rope), 128)], axis=2)


def _wq_unlayout(g):
    rope = g[:, :, 128:160] + _swap16(g[:, :, 256:288])
    return jnp.concatenate([g[:, :, 0:64], rope], axis=2).transpose(1, 0, 2).reshape(384, HEADS * 96)


def _wkv_layout(w):
    w = w.reshape(256, HEADS, 128).transpose(1, 0, 2)
    return jnp.concatenate([_padc(w[:, :, 0:64], 128), _padc(w[:, :, 64:128], 128)], axis=2)


def _wkv_unlayout(g):
    return jnp.concatenate([g[:, :, 0:64], g[:, :, 128:192]], axis=2).transpose(1, 0, 2).reshape(256, HEADS * 128)


def _wba_layout(w):
    return jnp.pad(w.reshape(HEADS, 64, D), ((0, 0), (0, 64), (0, 0))).reshape(HEADS * 128, D)


def _rows8(rows, width):
    out = jnp.stack([_padc(r.astype(F32), width) for r in rows])
    return jnp.pad(out, ((0, 8 - out.shape[0]), (0, 0)))


def _mla_vec(qa, kva, qn, kn):
    def row(n):
        return jnp.concatenate([_padc(n[0:64], 128), _padc(n[64:96], 128), _padc(_swap16(n[64:96]), 128)])
    return _rows8([qa, kva, row(qn), row(kn)], 512)


def _mla_unvec(g):
    def un(r):
        return jnp.concatenate([r[0:64], r[128:160] + _swap16(r[256:288])])
    return g[0, 0:384], g[1, 0:256], un(g[2]), un(g[3])


SMALL = (("ada_b", (6 * D,)), ("norm1_w", (D,)), ("q_a_norm", (384,)), ("kv_a_norm", (256,)), ("q_norm", (96,)),
         ("k_norm", (96,)), ("pool_w", (4, 128, 128)), ("pool_scale", (512,)), ("ssd_conv_b", (1536,)),
         ("ssd_dt_bias", (16,)), ("ssd_a_log", (16,)), ("ssd_d", (16,)), ("ssd_norm_w", (D,)), ("norm2_w", (D,)),
         ("ffn_conv_b", (2 * FFN,)), ("ssd_conv_w", (4, 1536)), ("ffn_conv_w", (3, 2 * FFN)))
SHARDED_SMALL = {"ssd_conv_w": 192, "ffn_conv_w": 704}


def _pack_rows(shp):
    return -(-math.prod(shp) // 1024) * 8


def _pack(small):
    pieces = []
    for n, shp in SMALL:
        pieces.append(small[n].reshape(-1).astype(F32))
        fill = _pack_rows(shp) * 128 - math.prod(shp)
        if fill:
            pieces.append(jnp.zeros((fill,), F32))
    return jnp.concatenate(pieces).reshape(-1, 128)


def _unpack_parts(packs):
    out, off = {}, 0
    for n, shp in SMALL:
        rows, size = _pack_rows(shp), math.prod(shp)
        r, c = math.prod(shp[:-1]), shp[-1]
        per_layer = [pk[:, off:off + rows].reshape(NDEV, rows * 128)[:, 0:size].reshape(NDEV, r, c) for pk in packs]
        out[n] = jnp.concatenate(per_layer, axis=1)
        off += rows
    return out


GROUP_A = ("w_in", "w_q_b", "w_kv_b")
GROUP_B = ("w_branch", "w_out", "ffn_up", "ffn_down")
BIG = GROUP_A + GROUP_B
SCATTER_FFN, SCATTER_MERGE = ("ffn_up", "ffn_down"), ("w_branch", "w_out")
COL_SHARDED = ("w_in", "w_q_b", "w_kv_b", "ffn_up")


def _behind(arrs, tok):
    arrs = list(arrs)
    j = min(range(len(arrs)), key=lambda q: arrs[q].size)
    arrs[j] = arrs[j] + tok[0, 0].astype(arrs[j].dtype)
    return arrs


def _gathered_full(g, name):
    if name in COL_SHARDED:
        return g.transpose(1, 0, 2).reshape(g.shape[1], NDEV * g.shape[2])
    return g.reshape(NDEV * g.shape[1], g.shape[2])


def _to_shards(full, name):
    if name == "w_in":
        return _win_grad_shards(full)
    if name in COL_SHARDED:
        r, c = full.shape
        return full.reshape(r, NDEV, c // NDEV).transpose(1, 0, 2).astype(BF16)
    r, c = full.shape
    return full.reshape(NDEV, r // NDEV, c).astype(BF16)


def _fwd_a(x, lw, mod, cos2, sin2, l, tok):
    sh1, sc1, g1, sh2, sc2, g2 = [mod[j * D:(j + 1) * D] for j in range(6)]
    vec1 = _rows8([lw["norm1_w"], sh1, sc1], D) + tok[0, 0]
    proj, h1, dt_cols = norm_proj_fwd(x, vec1, lw["win"], f"inproj_fwd{l}")
    q, k, v = mla_pre_fwd(proj, lw["wq"], lw["wkv"], lw["mla_vec"], cos2, sin2, f"mla_pre_fwd{l}")
    oa = mla_attn_fwd(q, k, v, f"mla_attn_fwd{l}")
    ob = pool_fwd(proj, lw["pool_w"], lw["pool_scale"].reshape(1, 512), f"pool_fwd{l}")
    xbc, xt = conv_fwd(proj, lw["ssd_conv_w"], lw["ssd_conv_b"].reshape(1, 1536), f"conv_fwd{l}")
    s = x.shape[0]
    xt = xt.reshape(16, 64, s)
    dt = dt_cols[:, 0:16].T
    dtr = dt[:, None, :]
    hv = lambda a: a.reshape(16, 1, 1)
    yt, hs = ssd_fwd(xt, dtr, xbc, hv(lw["ssd_a_log"]), hv(lw["ssd_dt_bias"]), hv(lw["ssd_d"]), f"ssd_fwd{l}")
    return dict(x=x, vec1=vec1, proj=proj, h1=h1, q=q, k=k, v=v, oa=oa, ob=ob, xbc=xbc, xt=xt, dtr=dtr,
                hs=hs, yt=yt.reshape(D, s), mvec=_rows8([g1, lw["ssd_norm_w"]], D),
                fvec=_rows8([lw["norm2_w"], sh2, sc2, g2], D))


def _fwd_b(sv, lw, l, tok):
    sv["mvec"] = sv["mvec"] + tok[0, 0]
    x1 = merge_fwd(sv["oa"], sv["ob"], sv["yt"], sv["proj"], sv["x"], sv["mvec"], lw["wba"], lw["wbb"], lw["wbc"],
                   lw["wout"], f"merge_fwd{l}")
    x2, h2, pre = ffn_fwd(x1, sv["fvec"], lw["wup"], lw["ffn_conv_w"], lw["ffn_conv_b"].reshape(1, 2 * FFN), lw["wdn"],
                          f"ffn_fwd{l}")
    sv.update(x1=x1, h2=h2, pre=pre)
    return x2


def _bwd_ffn(dx2, lw, sv, l, tok):
    fvec = sv["fvec"] + tok[0, 0]
    dup, act, dcw = ffn_bwd(sv["h2"], dx2, fvec, lw["wup"], lw["ffn_conv_w"], lw["ffn_conv_b"].reshape(1, 2 * FFN),
                            lw["wdn"], f"ffn_bwd{l}")
    grads = dict(ffn_down=tn_matmul(act, dx2, f"dw_down{l}", scale=fvec[3:4]),
                 ffn_up=tn_matmul(sv["h2"], dup, f"dw_up{l}"))
    small = dict(ffn_conv_w=jnp.concatenate([dcw[0, 0:3], dcw[1, 0:3]], axis=1),
                 ffn_conv_b=jnp.concatenate([dcw[0, 3], dcw[1, 3]]))
    return dup, grads, small


def _bwd_merge(dx2, dup, lw, sv, l, tok, small):
    grads = {}
    fvec = sv["fvec"] + tok[0, 0]
    dx1, dfvec = norm_proj_bwd(sv["x1"], fvec, dup, lw["wup"], dx2, sv["pre"], f"ffn_norm_bwd{l}")
    small["norm2_w"] = dfvec[0]
    (doa, dob, dyt, dz, dgl, dx, dmvec, dya, dyb, dyc, dpre, oc, merged) = merge_bwd(
        sv["oa"], sv["ob"], sv["yt"], sv["proj"], sv["x"], sv["mvec"], lw["wba"], lw["wbb"], lw["wbc"], lw["wout"], dx1,
        f"merge_bwd{l}")
    dwba = tn_matmul(sv["oa"], dya, f"dw_ba{l}").reshape(HEADS, 128, D)[:, 0:64].reshape(512, D)
    grads["w_branch"] = jnp.concatenate([dwba, tn_matmul(sv["ob"], dyb, f"dw_bb{l}"), tn_matmul(oc, dyc, f"dw_bc{l}")])
    grads["w_out"] = tn_matmul(merged, dpre, f"dw_out{l}")
    small["ssd_norm_w"] = dmvec[1]
    small["dmod_b"] = (dmvec[0], dfvec[1], dfvec[2], dfvec[3])
    return dx, dict(doa=doa, dob=dob, dyt=dyt, dz=dz, dgl=dgl), grads, small


def _bwd_a(dx, cot, lw, sv, cos2, sin2, l, tok, small):
    s = dx.shape[0]
    grads = {}
    doa, dob, dz, dgl = cot["doa"], cot["dob"], cot["dz"], cot["dgl"]
    hv = lambda a: a.reshape(16, 1, 1)
    dxt, ddtr, dbm, dcm, dal, ddb, ddk = ssd_bwd(
        sv["xt"], sv["dtr"], sv["xbc"], hv(lw["ssd_a_log"]) + tok[0, 0], hv(lw["ssd_dt_bias"]),
        hv(lw["ssd_d"]), sv["hs"], cot["dyt"].reshape(16, 64, s), f"ssd_bwd{l}")
    small["ssd_a_log"], small["ssd_dt_bias"], small["ssd_d"] = dal.reshape(16), ddb.reshape(16), ddk.reshape(16)
    dxbc, dscw, dscb = conv_bwd(sv["proj"], lw["ssd_conv_w"], lw["ssd_conv_b"].reshape(1, 1536), dxt.reshape(D, s),
                                dbm, dcm, f"conv_bwd{l}")
    small["ssd_conv_w"], small["ssd_conv_b"] = dscw, dscb.reshape(1536)
    ddt = ddtr[:, 0, :].T
    du, dpw, dps = pool_bwd(sv["proj"], lw["pool_w"], lw["pool_scale"].reshape(1, 512), dob, f"pool_bwd{l}")
    small["pool_w"], small["pool_scale"] = dpw, dps.reshape(512)
    dq, dk, dv = mla_attn_bwd(sv["q"], sv["k"], sv["v"], doa, f"mla_attn_bwd{l}")
    dql, dckv, dkr, dkrs, dwq, dwkv, dmv = mla_pre_bwd(sv["proj"], lw["wq"], lw["wkv"], lw["mla_vec"], cos2, sin2,
                                                       dq, dk, dv, f"mla_pre_bwd{l}")
    grads["w_q_b"], grads["w_kv_b"] = _wq_unlayout(dwq), _wkv_unlayout(dwkv)
    small["q_a_norm"], small["kv_a_norm"], small["q_norm"], small["k_norm"] = _mla_unvec(dmv)
    dproj = jnp.concatenate([dgl, dxbc[:, 0:D], dz, du, dxbc[:, D:1536], dckv, dkr, dkrs,
                             _padc(ddt, 128).astype(BF16), jnp.zeros((s, 128), BF16), dql], axis=1)
    grads["w_in"] = tn_matmul(sv["h1"], dproj, f"dw_in{l}")
    return dproj, grads, small


def _bwd_in(dx, dproj, lw, sv, l, tok, small):
    dx0, dvec1 = norm_proj_bwd(sv["x"], sv["vec1"] + tok[0, 0], dproj, lw["win"], dx, None, f"inproj_bwd{l}")
    small["norm1_w"] = dvec1[0]
    small["ada_b"] = jnp.concatenate([dvec1[1], dvec1[2], *small.pop("dmod_b")])
    return dx0, small


def kernel(x, c, positions, ada_w, ada_b, norm1_w, w_in, q_a_norm, w_q_b, kv_a_norm, w_kv_b, q_norm, k_norm, pool_w, pool_scale, ssd_conv_w, ssd_conv_b, ssd_dt_bias, ssd_a_log, ssd_d, ssd_norm_w, w_branch, w_out, norm2_w, ffn_up, ffn_conv_w, ffn_conv_b, ffn_down, loss_target, m_ada_w, m_ada_b, m_norm1_w, m_w_in, m_q_a_norm, m_w_q_b, m_kv_a_norm, m_w_kv_b, m_q_norm, m_k_norm, m_pool_w, m_pool_scale, m_ssd_conv_w, m_ssd_conv_b, m_ssd_dt_bias, m_ssd_a_log, m_ssd_d, m_ssd_norm_w, m_w_branch, m_w_out, m_norm2_w, m_ffn_up, m_ffn_conv_w, m_ffn_conv_b, m_ffn_down, v_ada_w, v_ada_b, v_norm1_w, v_w_in, v_q_a_norm, v_w_q_b, v_kv_a_norm, v_w_kv_b, v_q_norm, v_k_norm, v_pool_w, v_pool_scale, v_ssd_conv_w, v_ssd_conv_b, v_ssd_dt_bias, v_ssd_a_log, v_ssd_d, v_ssd_norm_w, v_w_branch, v_w_out, v_norm2_w, v_ffn_up, v_ffn_conv_w, v_ffn_conv_b, v_ffn_down):
    p = dict(ada_w=ada_w, ada_b=ada_b, norm1_w=norm1_w, w_in=w_in, q_a_norm=q_a_norm, w_q_b=w_q_b, kv_a_norm=kv_a_norm,
             w_kv_b=w_kv_b, q_norm=q_norm, k_norm=k_norm, pool_w=pool_w, pool_scale=pool_scale, ssd_conv_w=ssd_conv_w,
             ssd_conv_b=ssd_conv_b, ssd_dt_bias=ssd_dt_bias, ssd_a_log=ssd_a_log, ssd_d=ssd_d, ssd_norm_w=ssd_norm_w,
             w_branch=w_branch, w_out=w_out, norm2_w=norm2_w, ffn_up=ffn_up, ffn_conv_w=ffn_conv_w, ffn_conv_b=ffn_conv_b,
             ffn_down=ffn_down)
    mom = dict(ada_w=m_ada_w, ada_b=m_ada_b, norm1_w=m_norm1_w, w_in=m_w_in, q_a_norm=m_q_a_norm, w_q_b=m_w_q_b,
               kv_a_norm=m_kv_a_norm, w_kv_b=m_w_kv_b, q_norm=m_q_norm, k_norm=m_k_norm, pool_w=m_pool_w,
               pool_scale=m_pool_scale, ssd_conv_w=m_ssd_conv_w, ssd_conv_b=m_ssd_conv_b, ssd_dt_bias=m_ssd_dt_bias,
               ssd_a_log=m_ssd_a_log, ssd_d=m_ssd_d, ssd_norm_w=m_ssd_norm_w, w_branch=m_w_branch, w_out=m_w_out,
               norm2_w=m_norm2_w, ffn_up=m_ffn_up, ffn_conv_w=m_ffn_conv_w, ffn_conv_b=m_ffn_conv_b, ffn_down=m_ffn_down)
    var = dict(ada_w=v_ada_w, ada_b=v_ada_b, norm1_w=v_norm1_w, w_in=v_w_in, q_a_norm=v_q_a_norm, w_q_b=v_w_q_b,
               kv_a_norm=v_kv_a_norm, w_kv_b=v_w_kv_b, q_norm=v_q_norm, k_norm=v_k_norm, pool_w=v_pool_w,
               pool_scale=v_pool_scale, ssd_conv_w=v_ssd_conv_w, ssd_conv_b=v_ssd_conv_b, ssd_dt_bias=v_ssd_dt_bias,
               ssd_a_log=v_ssd_a_log, ssd_d=v_ssd_d, ssd_norm_w=v_ssd_norm_w, w_branch=v_w_branch, w_out=v_w_out,
               norm2_w=v_norm2_w, ffn_up=v_ffn_up, ffn_conv_w=v_ffn_conv_w, ffn_conv_b=v_ffn_conv_b, ffn_down=v_ffn_down)
    names = list(p)
    me = 4 * lax.axis_index("x") + 2 * lax.axis_index("y") + lax.axis_index("c")
    xs, tgt = x[0], loss_target[0]
    s = xs.shape[0]

    inv_freq = ROPE_THETA ** (-jnp.arange(0, 32, 2, dtype=F32) / 32.0)
    ang = positions[0].astype(F32)[:, None] * inv_freq
    cos, sin = jnp.cos(ang), jnp.sin(ang)
    cos2 = _padc(jnp.concatenate([cos, cos], axis=1), 128)
    sin2 = _padc(jnp.concatenate([-sin, sin], axis=1), 128)

    conv_shards = jnp.concatenate([ssd_conv_w.reshape(-1), ffn_conv_w.reshape(-1)])
    (c_all, conv_all), _ = all_to_all([c, conv_shards], [True, True], "gather_c")
    modp, cact = ada_mod(jnp.pad(c_all.reshape(NDEV, D), ((0, 8), (0, 0))), ada_w)
    (mod_in,), tok = all_to_all([modp[:, 0:NDEV].transpose(1, 0, 2)], [False], "scatter_mod")
    mod = mod_in.transpose(1, 0, 2).reshape(LAYERS, 6 * D) + ada_b

    n1 = LAYERS * 4 * 192
    scw = conv_all[:, :n1].reshape(NDEV, LAYERS, 4, 192).transpose(1, 2, 0, 3).reshape(LAYERS, 4, 1536)
    fcw = conv_all[:, n1:].reshape(NDEV, LAYERS, 3, 704).transpose(1, 2, 0, 3).reshape(LAYERS, 3, 2 * FFN)

    def weights_a(gathered, l):
        full = {n: _gathered_full(g, n) for n, g in zip(GROUP_A[1:], gathered[1:])}
        lw = {n: p[n][l] for n in names}
        lw.update(win=_win_layout(gathered[0]), wq=_wq_layout(full["w_q_b"]), wkv=_wkv_layout(full["w_kv_b"]),
                  ssd_conv_w=scw[l], ffn_conv_w=fcw[l],
                  mla_vec=_mla_vec(lw["q_a_norm"], lw["kv_a_norm"], lw["q_norm"], lw["k_norm"]))
        return lw

    def weights_b(gathered):
        full = {n: _gathered_full(g, n) for n, g in zip(GROUP_B, gathered)}
        wb = full["w_branch"]
        return dict(wba=_wba_layout(wb[0:512]), wbb=wb[512:1024], wbc=wb[1024:2048], wout=full["w_out"],
                    wup=full["ffn_up"], wdn=full["ffn_down"])

    shards = lambda group, l: [p[n][l].astype(BF16) for n in group]
    lws, saved = [None] * LAYERS, [None] * LAYERS
    st, tok = gather_start(_behind(shards(GROUP_A, 0), tok), "gather_a0")
    got, tok = gather_finish(st, tok, "gather_a0")
    h = xs
    for l in range(LAYERS):
        st, tok = gather_start(_behind(shards(GROUP_B, l), tok), f"gather_b{l}")
        lws[l] = weights_a(got, l)
        saved[l] = _fwd_a(h, lws[l], mod[l], cos2, sin2, l, tok)
        got, tok = gather_finish(st, saved[l]["yt"], f"gather_b{l}")
        lws[l].update(weights_b(got))
        if l + 1 < LAYERS:
            st, tok = gather_start(_behind(shards(GROUP_A, l + 1), tok), f"gather_a{l + 1}")
        h = _fwd_b(saved[l], lws[l], l, tok)
        if l + 1 < LAYERS:
            got, tok = gather_finish(st, h, f"gather_a{l + 1}")
    dx, lpart = loss_head(h, tgt)
    loss = lax.psum(lpart[0, 0], ("x", "y", "c"))
    tok = tok + loss * 0.0

    small, parts, packs = [None] * LAYERS, {n: [None] * LAYERS for n in BIG}, [None] * LAYERS
    st = None

    def scatter(grads, group, l, tok, extra=None):
        arrs, flags = [_to_shards(grads[n], n) for n in group], [False] * len(group)
        if extra is not None:
            arrs, flags = arrs + [extra], flags + [True]
        return exchange_start(_behind(arrs, tok), flags, f"scatter_{group[0]}{l}_start")

    def landed(state, group, l, after):
        got, tok, _ = exchange_wait(state, after, f"scatter_{group[0]}{l}_wait")
        for n, g in zip(group, got):
            parts[n][l] = g
        return got, tok

    for l in reversed(range(LAYERS)):
        dup, g_ffn, small[l] = _bwd_ffn(dx, lws[l], saved[l], l, tok)
        if st is not None:
            _, tok = landed(st, GROUP_A, l + 1, dup)
        st, tok = scatter(g_ffn, SCATTER_FFN, l, tok)
        dx, cot, g_merge, small[l] = _bwd_merge(dx, dup, lws[l], saved[l], l, tok, small[l])
        _, tok = landed(st, SCATTER_FFN, l, dx)
        st, tok = scatter(g_merge, SCATTER_MERGE, l, tok, _pack(small[l + 1]) if l + 1 < LAYERS else None)
        dproj, g_in, small[l] = _bwd_a(dx, cot, lws[l], saved[l], cos2, sin2, l, tok, small[l])
        got, tok = landed(st, SCATTER_MERGE, l, dproj)
        if l + 1 < LAYERS:
            packs[l + 1] = got[-1]
        st, tok = scatter(g_in, GROUP_A, l, tok)
        dx, small[l] = _bwd_in(dx, dproj, lws[l], saved[l], l, tok, small[l])

    dmod = jnp.stack([small[q]["ada_b"] for q in range(LAYERS)])
    st_small, tok = exchange_start(_behind([_pack(small[0]), dmod.reshape(LAYERS, NDEV, 768).transpose(1, 0, 2)], tok),
                                   [True, False], "scatter_s0_start")
    out = {}

    def big_adamw(group, tok):
        res = None
        for n in group:
            res = out[n] = adamw(parts[n], p[n], mom[n], var[n], f"adamw_{n}", tok)
        return res[0]

    g_last = big_adamw(GROUP_B, tok)
    _, tok = landed(st, GROUP_A, 0, g_last)
    (packs[0], dmod_in), _, _ = exchange_wait(st_small, tok, "scatter_s0_wait")
    big_adamw(GROUP_A, None)

    dmod16 = jnp.pad(dmod_in, ((0, 8), (0, 0), (0, 0)))
    g_ada = [tn_matmul(cact, dmod16[:, l], f"dw_ada{l}", out_dtype=F32)[None] for l in range(LAYERS)]
    out["ada_w"] = adamw(g_ada, ada_w, m_ada_w, v_ada_w, "adamw_ada_w")

    for n, pt in _unpack_parts(packs).items():
        if n in SHARDED_SMALL:
            w = SHARDED_SMALL[n]
            pt = lax.dynamic_slice_in_dim(pt, me * w, w, axis=2)
        r, c = pt.shape[1:]
        res = adamw([pt], p[n].reshape(1, r, c), mom[n].reshape(1, r, c), var[n].reshape(1, r, c), f"adamw_{n}")
        out[n] = [a.reshape(p[n].shape) for a in res]

    outs = [loss, dx[None]]
    for q in range(4):
        outs += [out[n][q] for n in names]
    return tuple(outs)
```

```python
import functools
import math

import jax
import jax.numpy as jnp
from jax import lax
from jax.experimental import pallas as pl
from jax.experimental.pallas import tpu as pltpu

F32, BF16 = jnp.float32, jnp.bfloat16
EPS = 1e-6
D = 1024
NDEV = 8
LAYERS = 2
HEADS = 8
FFN = 2816
FFN_TILE = 1408
FFN_NT = FFN // FFN_TILE
ATT_SCALE = 96 ** -0.5
ROPE_THETA = 10000.0
LR, B1, B2, ADAM_EPS, WD, STEP = 0.001, 0.9, 0.999, 1e-08, 0.01, 10

O_G, O_XS, O_Z, O_PU, O_BC, O_CKV, O_KR, O_KRS, O_DT, O_QL = 0, 3072, 4096, 5120, 5632, 6144, 6400, 6528, 6656, 6912
NPROJ = 7296
CONST = dict(pipeline_mode=pl.Buffered(1))


def _pick(n, cap, mult=128):
    if n <= cap:
        return n
    best = None
    for t in range(mult, cap + 1, mult):
        if n % t == 0:
            best = t
    assert best is not None, (n, cap, mult)
    return best


def _sig(x):
    return 1.0 / (1.0 + jnp.exp(-x))


def _rms(x, w, n):
    return x * lax.rsqrt(jnp.sum(x * x, axis=-1, keepdims=True) / n + EPS) * w


def _raw(a, b, dims):
    return lax.dot_general(a.astype(BF16), b.astype(BF16), dims, preferred_element_type=F32)


_NN = (((1,), (0,)), ((), ()))
_NT = (((1,), (1,)), ((), ()))
_TN = (((0,), (0,)), ((), ()))
_BNN = (((2,), (1,)), ((0,), (0,)))
_BNT = (((2,), (2,)), ((0,), (0,)))
_BTN = (((1,), (1,)), ((0,), (0,)))


@jax.custom_vjp
def mm_nn(a, b):
    return _raw(a, b, _NN)


mm_nn.defvjp(lambda a, b: (_raw(a, b, _NN), (a, b)),
             lambda r, g: (_raw(g, r[1], _NT), _raw(r[0], g, _TN)))


@jax.custom_vjp
def mm_nc(a, b):
    return _raw(a, b, _NN)


mm_nc.defvjp(lambda a, b: (_raw(a, b, _NN), b),
             lambda b, g: (_raw(g, b, _NT), jnp.zeros_like(b)))


@jax.custom_vjp
def mm_nt(a, b):
    return _raw(a, b, _NT)


mm_nt.defvjp(lambda a, b: (_raw(a, b, _NT), (a, b)),
             lambda r, g: (_raw(g, r[1], _NN), _raw(g, r[0], _TN)))


@jax.custom_vjp
def bmm_nn(a, b):
    return _raw(a, b, _BNN)


bmm_nn.defvjp(lambda a, b: (_raw(a, b, _BNN), (a, b)),
              lambda r, g: (_raw(g, r[1], _BNT), _raw(r[0], g, _BTN)))


@jax.custom_vjp
def bmm_nt(a, b):
    return _raw(a, b, _BNT)


bmm_nt.defvjp(lambda a, b: (_raw(a, b, _BNT), (a, b)),
              lambda r, g: (_raw(g, r[1], _BNN), _raw(g, r[0], _BTN)))


@jax.custom_vjp
def softplus(x):
    t = jnp.exp(-jnp.abs(x))
    u = 1.0 + t
    one = u == 1.0
    l1p = jnp.where(one, t, jnp.log(u) * (t / jnp.where(one, 1.0, u - 1.0)))
    return jnp.maximum(x, 0.0) + l1p


softplus.defvjp(lambda x: (softplus(x), x), lambda x, g: (g * _sig(x),))


def _params(*sem):
    return pltpu.CompilerParams(dimension_semantics=sem, vmem_limit_bytes=56 * 1024 * 1024)


def all_to_all(arrs, bcast, name):
    n = len(arrs)
    out_shapes = [jax.ShapeDtypeStruct(((NDEV,) + a.shape) if b else a.shape, a.dtype) for a, b in zip(arrs, bcast)]

    def body(*refs):
        ins, outs, token = refs[:n], refs[n:2 * n], refs[2 * n]
        send_sems, recv_sems, local_sems = refs[2 * n + 1:]
        me, remote = _exchange_copies(ins, outs, bcast, send_sems, recv_sems)
        local = [pltpu.make_async_copy(ins[j] if bcast[j] else ins[j].at[me], outs[j].at[me], local_sems.at[j])
                 for j in range(n)]
        for cp in local + remote:
            cp.start()
        for cp in remote + local:
            cp.wait()
        token[...] = jnp.zeros_like(token)

    any_spec = pl.BlockSpec(memory_space=pl.ANY)
    res = pl.pallas_call(
        body, name=name, out_shape=out_shapes + [jax.ShapeDtypeStruct((8, 128), F32)], in_specs=[any_spec] * n,
        out_specs=[any_spec] * n + [pl.BlockSpec(memory_space=pltpu.VMEM)],
        scratch_shapes=[pltpu.SemaphoreType.DMA((7 * n,)), pltpu.SemaphoreType.DMA((7 * n,)),
                        pltpu.SemaphoreType.DMA((n,))],
        compiler_params=pltpu.CompilerParams(has_side_effects=True),
    )(*arrs)
    return res[:n], res[n]


def _peers():
    x, y, c = lax.axis_index("x"), lax.axis_index("y"), lax.axis_index("c")
    out = []
    for k in range(1, NDEV):
        px, py, pc = x ^ ((k >> 2) & 1), y ^ ((k >> 1) & 1), c ^ (k & 1)
        out.append(((px, py, pc), 4 * px + 2 * py + pc))
    return 4 * x + 2 * y + c, out


COPIES = {"all": 7, "chips": 3, "pass": 4}


def _exchange_copies(ins, lands, bcast, send_sems, recv_sems, mode="all"):
    x, y, c = lax.axis_index("x"), lax.axis_index("y"), lax.axis_index("c")
    me = 4 * x + 2 * y + c
    n, copies = len(ins), []

    def add(q, j, src, dst, dev):
        copies.append(pltpu.make_async_remote_copy(
            src_ref=src, dst_ref=dst, send_sem=send_sems.at[q * n + j], recv_sem=recv_sems.at[q * n + j],
            device_id=dev, device_id_type=pl.DeviceIdType.MESH))

    if mode == "pass":
        for q in range(4):
            slot = 4 * (x ^ (q >> 1)) + 2 * (y ^ (q & 1)) + c
            for j in range(n):
                add(q, j, ins[j] if q == 0 else lands[j].at[slot], lands[j].at[slot], (x, y, 1 - c))
        return me, copies
    for q, k in enumerate(range(1, NDEV) if mode == "all" else (2, 4, 6)):
        px, py, pc = x ^ ((k >> 2) & 1), y ^ ((k >> 1) & 1), c ^ (k & 1)
        for j in range(n):
            add(q, j, ins[j] if bcast[j] else ins[j].at[4 * px + 2 * py + pc], lands[j].at[me], (px, py, pc))
    return me, copies


_HBM = pl.BlockSpec(memory_space=pltpu.HBM)
_SEM = pl.BlockSpec(memory_space=pltpu.SEMAPHORE)
_EFFECT = pltpu.SideEffectType.DATAFLOW_SIDE_EFFECTING


def exchange_start(arrs, bcast, name, mode="all", lands=None):
    n, ncp = len(arrs), COPIES[mode] * len(arrs)
    land_shapes = [((NDEV,) + a.shape) if b else a.shape for a, b in zip(arrs, bcast)]
    if lands is None:
        lands = [lax.empty(s_, a.dtype) for s_, a in zip(land_shapes, arrs)]

    def body(*refs):
        in_refs, land_refs = refs[:n], refs[n:2 * n]
        send_sems, recv_sems = refs[2 * n], refs[2 * n + 1]
        token = refs[-1]
        _, copies = _exchange_copies(in_refs, land_refs, bcast, send_sems, recv_sems, mode)
        for cp in copies:
            cp.start()
        token[...] = jnp.zeros_like(token)

    hbm = lambda shp, a: pltpu.HBM(shp, a.dtype)
    res = pl.pallas_call(
        body, name=name,
        out_shape=[pltpu.SemaphoreType.DMA((ncp,)), pltpu.SemaphoreType.DMA((ncp,))]
                  + [hbm(a.shape, a) for a in arrs] + [hbm(s_, a) for s_, a in zip(land_shapes, arrs)]
                  + [jax.ShapeDtypeStruct((8, 128), F32)],
        in_specs=[_HBM] * (2 * n), out_specs=[_SEM, _SEM] + [_HBM] * (2 * n) + [pl.BlockSpec(memory_space=pltpu.VMEM)],
        input_output_aliases={i: 2 + i for i in range(2 * n)},
        compiler_params=pltpu.CompilerParams(has_side_effects=_EFFECT),
    )(*[pltpu.with_memory_space_constraint(a, pltpu.HBM) for a in arrs],
      *[pltpu.with_memory_space_constraint(a, pltpu.HBM) for a in lands])
    return (res[0], res[1], res[2:2 + n], res[2 + n:2 + 2 * n], tuple(bcast), mode), res[-1]


def exchange_wait(state, after, name):
    send_sems, recv_sems, ins, lands, bcast, mode = state
    n = len(ins)

    def body(*refs):
        in_refs, land_refs = refs[:n], refs[n:2 * n]
        s_sems, r_sems = refs[2 * n], refs[2 * n + 1]
        token = refs[-1]
        _, copies = _exchange_copies(in_refs, land_refs, bcast, s_sems, r_sems, mode)
        for cp in copies:
            cp.wait_send()
            cp.wait_recv()
        token[...] = jnp.zeros_like(token)

    res = pl.pallas_call(
        body, name=name,
        out_shape=[pltpu.HBM(a.shape, a.dtype) for a in ins] + [pltpu.HBM(a.shape, a.dtype) for a in lands]
                  + [jax.ShapeDtypeStruct((8, 128), F32)],
        in_specs=[_HBM] * (2 * n) + [_SEM, _SEM, pl.BlockSpec(memory_space=pl.ANY)],
        out_specs=[_HBM] * (2 * n) + [pl.BlockSpec(memory_space=pltpu.VMEM)],
        input_output_aliases={i: i for i in range(2 * n)},
        compiler_params=pltpu.CompilerParams(has_side_effects=_EFFECT),
    )(*ins, *lands, send_sems, recv_sems, after)
    if mode == "chips":
        return list(res[n:2 * n]), res[-1], list(res[:n])
    me = 4 * lax.axis_index("x") + 2 * lax.axis_index("y") + lax.axis_index("c")
    got = []
    for j in range(n):
        own = res[j][None] if bcast[j] else lax.dynamic_index_in_dim(res[j], me, 0, keepdims=True)
        got.append(lax.dynamic_update_slice_in_dim(res[n + j], own, me, axis=0))
    return got, res[-1], list(res[:n])


def gather_start(shards, name):
    return exchange_start(shards, [True] * len(shards), name + "_chips_start", mode="chips")


def gather_finish(state, after, name):
    lands, _, sent = exchange_wait(state, after, name + "_chips_wait")
    state, tok = exchange_start(sent, [True] * len(sent), name + "_pass_start", mode="pass", lands=lands)
    got, tok, _ = exchange_wait(state, tok, name + "_pass_wait")
    return got, tok


def norm_proj_fwd(x, vec, w, name):
    s, n = x.shape[0], w.shape[0]
    tr, tn = _pick(s, 512), _pick(n, 2560)
    ni, jdt, odt = s // tr, O_DT // tn, O_DT % tn

    def body(x_ref, v_ref, w_ref, o_ref, h_ref, dt_ref, h_scr):
        j, i = pl.program_id(0), pl.program_id(1)
        rows = pl.ds(pl.multiple_of(i * tr, tr), tr)

        @pl.when(j == 0)
        def _():
            h = _rms(x_ref[...], v_ref[0:1, :], D) * (1.0 + v_ref[2:3, :]) + v_ref[1:2, :]
            h_scr[rows, :] = h.astype(BF16)
            h_ref[...] = h.astype(BF16)
        res = _raw(h_scr[rows, :], w_ref[...], _NT)
        o_ref[...] = res

        @pl.when(j == jdt)
        def _():
            dt_ref[...] = res[:, odt:odt + 128]

    first = lambda j, i: (jnp.where(j == 0, i, ni - 1), 0)
    dtix = lambda j, i: (jnp.where(j < jdt, 0, jnp.where(j == jdt, i, ni - 1)), 0)
    return pl.pallas_call(
        body, name=name, grid=(n // tn, ni),
        in_specs=[pl.BlockSpec((tr, D), first), pl.BlockSpec((8, D), lambda j, i: (0, 0)),
                  pl.BlockSpec((tn, D), lambda j, i: (j, 0))],
        out_specs=[pl.BlockSpec((tr, tn), lambda j, i: (i, j)), pl.BlockSpec((tr, D), first),
                   pl.BlockSpec((tr, 128), dtix)],
        out_shape=[jax.ShapeDtypeStruct((s, n), F32), jax.ShapeDtypeStruct((s, D), BF16),
                   jax.ShapeDtypeStruct((s, 128), F32)],
        scratch_shapes=[pltpu.VMEM((s, D), BF16)],
        compiler_params=_params("arbitrary", "arbitrary"),
    )(x, vec, w)


def _col_tiles(arr, cap):
    if arr.ndim == 2:
        n = arr.shape[1]
        t = _pick(n, cap)
        return n, t, lambda rows, ix: pl.BlockSpec((rows, t), lambda *g: ix(*g))
    width = arr.shape[2]
    t = _pick(width, cap)
    per = width // t

    def spec(rows, ix):
        def index(*g):
            r, j = ix(*g)
            return (j // per, r, j % per)
        return pl.BlockSpec((None, rows, t), index)
    return arr.shape[0] * width, t, spec


def norm_proj_bwd(x, vec, dp, w, dx_in, aux, name):
    s = x.shape[0]
    tr = _pick(s, 512)
    n, tk, dp_spec = _col_tiles(dp, 2560)
    nk, has_aux = n // tk, aux is not None

    def body(*refs):
        if has_aux:
            x_ref, v_ref, dp_ref, w_ref, dxin_ref, aux_ref, dx_ref, dv_ref, acc = refs
        else:
            x_ref, v_ref, dp_ref, w_ref, dxin_ref, dx_ref, dv_ref, acc = refs
        k, i = pl.program_id(0), pl.program_id(1)
        rows = pl.ds(pl.multiple_of(i * tr, tr), tr)
        part = _raw(dp_ref[...], w_ref[...], _NN)

        @pl.when(k == 0)
        def _():
            acc[rows, :] = part

        @pl.when(k > 0)
        def _():
            acc[rows, :] += part

        @pl.when(k == nk - 1)
        def _():
            f = lambda xx, nw, sh, sc: _rms(xx, nw, D) * (1.0 + sc) + sh
            _, vjp = jax.vjp(f, x_ref[...], v_ref[0:1, :], v_ref[1:2, :], v_ref[2:3, :])
            dx, dnw, dsh, dsc = vjp(acc[rows, :])
            dx_ref[...] = dxin_ref[...] + dx

            @pl.when(i == 0)
            def _():
                dv_ref[...] = jnp.zeros_like(dv_ref)

            dv_ref[0:1, :] += dnw
            dv_ref[1:2, :] += dsh
            dv_ref[2:3, :] += dsc
            if has_aux:
                dv_ref[3:4, :] += jnp.sum(dxin_ref[...] * aux_ref[...], axis=0, keepdims=True)

    row = pl.BlockSpec((tr, D), lambda k, i: (jnp.where(k == nk - 1, i, 0), 0))
    in_specs = [row, pl.BlockSpec((8, D), lambda k, i: (0, 0)), dp_spec(tr, lambda k, i: (i, k)),
                pl.BlockSpec((tk, D), lambda k, i: (k, 0)), row] + ([row] if has_aux else [])
    args = [x, vec, dp, w, dx_in] + ([aux] if has_aux else [])
    return pl.pallas_call(
        body, name=name, grid=(nk, s // tr), in_specs=in_specs,
        out_specs=[row, pl.BlockSpec((8, D), lambda k, i: (0, 0))],
        out_shape=[jax.ShapeDtypeStruct((s, D), F32), jax.ShapeDtypeStruct((8, D), F32)],
        scratch_shapes=[pltpu.VMEM((s, D), F32)],
        compiler_params=_params("arbitrary", "arbitrary"),
    )(*args)


def tn_matmul(a, b, name, scale=None, out_dtype=None):
    out_dtype = BF16 if out_dtype is None else out_dtype
    s = b.shape[-2]
    ts = _pick(s, 512, 16)
    m, tm, a_spec = _col_tiles(a, 2560 if b.shape[-1] <= D else 1408)
    n, tn, b_spec = _col_tiles(b, 2560)
    ns, has_scale = s // ts, scale is not None

    def body(*refs):
        if has_scale:
            a_ref, b_ref, sc_ref, o_ref, acc = refs
        else:
            a_ref, b_ref, o_ref, acc = refs
        k = pl.program_id(2)

        @pl.when(k == 0)
        def _():
            acc[...] = jnp.zeros_like(acc)

        acc[...] += _raw(a_ref[...], b_ref[...], _TN)

        @pl.when(k == ns - 1)
        def _():
            o_ref[...] = (acc[...] * sc_ref[...] if has_scale else acc[...]).astype(out_dtype)

    in_specs = [a_spec(ts, lambda i, j, k: (k, i)), b_spec(ts, lambda i, j, k: (k, j))]
    if has_scale:
        in_specs.append(pl.BlockSpec((1, tn), lambda i, j, k: (0, j)))
    return pl.pallas_call(
        body, name=name, grid=(m // tm, n // tn, ns), in_specs=in_specs,
        out_specs=pl.BlockSpec((tm, tn), lambda i, j, k: (i, j)),
        out_shape=jax.ShapeDtypeStruct((m, n), out_dtype),
        scratch_shapes=[pltpu.VMEM((tm, tn), F32)],
        compiler_params=_params("arbitrary", "arbitrary", "arbitrary"),
    )(*([a, b] + ([scale] if has_scale else [])))


def ada_mod(c16, w):
    ncol = w.shape[2]

    def body(c_ref, w_ref, o_ref, a_ref):
        cc = c_ref[...]
        act = cc * _sig(cc)
        a_ref[...] = act
        o_ref[...] = _raw(act, w_ref[...], _NN)

    return pl.pallas_call(
        body, name="ada_mod", grid=(LAYERS,),
        in_specs=[pl.BlockSpec((16, D), lambda l: (0, 0)), pl.BlockSpec((None, D, ncol), lambda l: (l, 0, 0))],
        out_specs=[pl.BlockSpec((None, 16, ncol), lambda l: (l, 0, 0)), pl.BlockSpec((16, D), lambda l: (0, 0))],
        out_shape=[jax.ShapeDtypeStruct((LAYERS, 16, ncol), F32), jax.ShapeDtypeStruct((16, D), F32)],
        compiler_params=_params("arbitrary"),
    )(c16, w)


def _mla_shared(q_lat, c_kv, kr, krs, qa_w, kva_w, kr_w, krs_w, cos2, sin2):
    qn = _rms(q_lat, qa_w, 384.0)
    kvn = _rms(c_kv, kva_w, 256.0)
    rk = lax.rsqrt(jnp.sum(kr * kr, axis=-1, keepdims=True) / 32.0 + EPS)
    krope = rk * (kr * kr_w * cos2 + krs * krs_w * sin2)
    return qn, kvn, krope


def _mla_head(qn, kvn, wqn, wqr, wqrs, wkn, wv, qn_w, qr_w, qrs_w, kn_w, cos2, sin2):
    qnope = _rms(mm_nn(qn, wqn), qn_w, 64.0)
    qr, qrs = mm_nn(qn, wqr), mm_nn(qn, wqrs)
    rq = lax.rsqrt(jnp.sum(qr * qr, axis=-1, keepdims=True) / 32.0 + EPS)
    qrope = rq * (qr * qr_w * cos2 + qrs * qrs_w * sin2)
    knope = _rms(mm_nn(kvn, wkn), kn_w, 64.0)
    return qnope, qrope, knope, mm_nn(kvn, wv)


def _mla_vec_pieces(v_ref):
    return ((v_ref[0:1, 0:384], v_ref[1:2, 0:256], v_ref[3:4, 128:256], v_ref[3:4, 256:384]),
            (v_ref[2:3, 0:128], v_ref[2:3, 128:256], v_ref[2:3, 256:384], v_ref[3:4, 0:128]))


def _mla_in_specs(tr):
    return [pl.BlockSpec((tr, 384), lambda i: (i, O_QL // 384)), pl.BlockSpec((tr, 256), lambda i: (i, O_CKV // 256)),
            pl.BlockSpec((tr, 128), lambda i: (i, O_KR // 128)), pl.BlockSpec((tr, 128), lambda i: (i, O_KRS // 128)),
            pl.BlockSpec((HEADS, 384, 384), lambda i: (0, 0, 0), **CONST),
            pl.BlockSpec((HEADS, 256, 256), lambda i: (0, 0, 0), **CONST),
            pl.BlockSpec((8, 512), lambda i: (0, 0)),
            pl.BlockSpec((tr, 128), lambda i: (i, 0)), pl.BlockSpec((tr, 128), lambda i: (i, 0))]


def mla_pre_fwd(proj, wq, wkv, vec, cos2, sin2, name):
    s = proj.shape[0]
    tr = _pick(s, 256)

    def body(ql_ref, ckv_ref, kr_ref, krs_ref, wq_ref, wkv_ref, v_ref, cos_ref, sin_ref, q_out, k_out, v_out):
        vshared, vhead = _mla_vec_pieces(v_ref)
        cos2_, sin2_ = cos_ref[...], sin_ref[...]
        qlat_n, kv_n, krope = _mla_shared(ql_ref[...], ckv_ref[...], kr_ref[...], krs_ref[...], *vshared, cos2_, sin2_)
        qlat_n, kv_n, krope = qlat_n.astype(BF16), kv_n.astype(BF16), krope.astype(BF16)
        for h in range(HEADS):
            ws = (wq_ref[h, :, 0:128], wq_ref[h, :, 128:256], wq_ref[h, :, 256:384],
                  wkv_ref[h, :, 0:128], wkv_ref[h, :, 128:256])
            qn, qr, kn, v = _mla_head(qlat_n, kv_n, *ws, *vhead, cos2_, sin2_)
            q_out[h, :, 0:128] = qn.astype(BF16)
            q_out[h, :, 128:256] = qr.astype(BF16)
            k_out[h, :, 0:128] = kn.astype(BF16)
            k_out[h, :, 128:256] = krope
            v_out[h] = v.astype(BF16)

    return pl.pallas_call(
        body, name=name, grid=(s // tr,), in_specs=_mla_in_specs(tr),
        out_specs=[pl.BlockSpec((HEADS, tr, 256), lambda i: (0, i, 0)), pl.BlockSpec((HEADS, tr, 256), lambda i: (0, i, 0)),
                   pl.BlockSpec((HEADS, tr, 128), lambda i: (0, i, 0))],
        out_shape=[jax.ShapeDtypeStruct((HEADS, s, 256), BF16), jax.ShapeDtypeStruct((HEADS, s, 256), BF16),
                   jax.ShapeDtypeStruct((HEADS, s, 128), BF16)],
        compiler_params=_params("arbitrary"),
    )(proj, proj, proj, proj, wq, wkv, vec, cos2, sin2)


def mla_pre_bwd(proj, wq, wkv, vec, cos2, sin2, dq, dk, dv, name):
    s = proj.shape[0]
    tr = _pick(s, 256)

    def body(ql_ref, ckv_ref, kr_ref, krs_ref, wq_ref, wkv_ref, v_ref, cos_ref, sin_ref, dq_ref, dk_ref, dv_ref,
             dql_out, dckv_out, dkr_out, dkrs_out, dwq_out, dwkv_out, dvec_out):
        @pl.when(pl.program_id(0) == 0)
        def _():
            dwq_out[...] = jnp.zeros_like(dwq_out)
            dwkv_out[...] = jnp.zeros_like(dwkv_out)
            dvec_out[...] = jnp.zeros_like(dvec_out)

        vshared, vhead = _mla_vec_pieces(v_ref)
        cos2_, sin2_ = cos_ref[...], sin_ref[...]
        fs = lambda *a: _mla_shared(*a, cos2_, sin2_)
        (qlat_n, kv_n, _), vjp_shared = jax.vjp(fs, ql_ref[...], ckv_ref[...], kr_ref[...], krs_ref[...], *vshared)

        def head(h, carry):
            wq_h, wkv_h = wq_ref[h].astype(F32), wkv_ref[h].astype(F32)
            ws = (wq_h[:, 0:128], wq_h[:, 128:256], wq_h[:, 256:384], wkv_h[:, 0:128], wkv_h[:, 128:256])
            f = lambda *a: _mla_head(*a, cos2_, sin2_)
            _, vjp = jax.vjp(f, qlat_n, kv_n, *ws, *vhead)
            dq_h, dk_h = dq_ref[h], dk_ref[h]
            g = vjp((dq_h[:, 0:128], dq_h[:, 128:256], dk_h[:, 0:128], dv_ref[h]))
            dwq_out[h, :, 0:128] += g[2]
            dwq_out[h, :, 128:256] += g[3]
            dwq_out[h, :, 256:384] += g[4]
            dwkv_out[h, :, 0:128] += g[5]
            dwkv_out[h, :, 128:256] += g[6]
            dvec_out[2:3, 0:128] += g[7]
            dvec_out[2:3, 128:256] += g[8]
            dvec_out[2:3, 256:384] += g[9]
            dvec_out[3:4, 0:128] += g[10]
            return carry[0] + g[0], carry[1] + g[1], carry[2] + dk_h[:, 128:256]

        zero = lambda w: jnp.zeros((tr, w), F32)
        dqn, dkvn, dkrope = lax.fori_loop(0, HEADS, head, (zero(384), zero(256), zero(128)))
        g = vjp_shared((dqn, dkvn, dkrope))
        dql_out[...] = g[0].astype(BF16)
        dckv_out[...] = g[1].astype(BF16)
        dkr_out[...] = g[2].astype(BF16)
        dkrs_out[...] = g[3].astype(BF16)
        dvec_out[0:1, 0:384] += g[4]
        dvec_out[1:2, 0:256] += g[5]
        dvec_out[3:4, 128:256] += g[6]
        dvec_out[3:4, 256:384] += g[7]

    hb = lambda w: pl.BlockSpec((HEADS, tr, w), lambda i: (0, i, 0))
    return pl.pallas_call(
        body, name=name, grid=(s // tr,), in_specs=_mla_in_specs(tr) + [hb(256), hb(256), hb(128)],
        out_specs=[pl.BlockSpec((tr, 384), lambda i: (i, 0)), pl.BlockSpec((tr, 256), lambda i: (i, 0)),
                   pl.BlockSpec((tr, 128), lambda i: (i, 0)), pl.BlockSpec((tr, 128), lambda i: (i, 0)),
                   pl.BlockSpec((HEADS, 384, 384), lambda i: (0, 0, 0)), pl.BlockSpec((HEADS, 256, 256), lambda i: (0, 0, 0)),
                   pl.BlockSpec((8, 512), lambda i: (0, 0))],
        out_shape=[jax.ShapeDtypeStruct((s, 384), BF16), jax.ShapeDtypeStruct((s, 256), BF16),
                   jax.ShapeDtypeStruct((s, 128), BF16), jax.ShapeDtypeStruct((s, 128), BF16),
                   jax.ShapeDtypeStruct((HEADS, 384, 384), F32), jax.ShapeDtypeStruct((HEADS, 256, 256), F32),
                   jax.ShapeDtypeStruct((8, 512), F32)],
        compiler_params=_params("arbitrary"),
    )(proj, proj, proj, proj, wq, wkv, vec, cos2, sin2, dq, dk, dv)


def _att_probs(q, kk, i, tq):
    sc = _raw(q, kk, _NT) * ATT_SCALE
    rows = lax.broadcasted_iota(jnp.int32, sc.shape, 0) + i * tq
    cols = lax.broadcasted_iota(jnp.int32, sc.shape, 1)
    sc = jnp.where(cols <= rows, sc, -jnp.inf)
    e = jnp.exp(sc - jnp.max(sc, axis=-1, keepdims=True))
    return e / jnp.sum(e, axis=-1, keepdims=True)


def mla_attn_fwd(q, k, v, name):
    s = q.shape[1]
    tq = _pick(s, 256)

    def body(q_ref, k_ref, v_ref, o_ref):
        for i in range(s // tq):
            n = (i + 1) * tq
            p = _att_probs(q_ref[i * tq:n, :], k_ref[0:n, :], i, tq)
            o_ref[i * tq:n, :] = _raw(p, v_ref[0:n, :], _NN)

    hs = lambda w: pl.BlockSpec((None, s, w), lambda h: (h, 0, 0))
    return pl.pallas_call(
        body, name=name, grid=(HEADS,), in_specs=[hs(256), hs(256), hs(128)],
        out_specs=pl.BlockSpec((s, 128), lambda h: (0, h)),
        out_shape=jax.ShapeDtypeStruct((s, HEADS * 128), F32),
        compiler_params=_params("arbitrary"),
    )(q, k, v)


def mla_attn_bwd(q, k, v, do, name):
    s = q.shape[1]
    tq = _pick(s, 256)

    def body(q_ref, k_ref, v_ref, do_ref, dq_ref, dk_ref, dv_ref):
        dk_ref[...] = jnp.zeros_like(dk_ref)
        dv_ref[...] = jnp.zeros_like(dv_ref)
        for i in range(s // tq):
            n = (i + 1) * tq
            qq, kk, vv = q_ref[i * tq:n, :], k_ref[0:n, :], v_ref[0:n, :]
            p = _att_probs(qq, kk, i, tq)
            o = _raw(p, vv, _NN)
            dout = do_ref[i * tq:n, :]
            delta = jnp.sum(dout * o, axis=-1, keepdims=True)
            dp = _raw(dout, vv, _NT)
            ds = p * (dp - delta) * ATT_SCALE
            dq_ref[i * tq:n, :] = _raw(ds, kk, _NN)
            dk_ref[0:n, :] += _raw(ds, qq, _TN)
            dv_ref[0:n, :] += _raw(p, dout, _TN)

    hs = lambda w: pl.BlockSpec((None, s, w), lambda h: (h, 0, 0))
    return pl.pallas_call(
        body, name=name, grid=(HEADS,),
        in_specs=[hs(256), hs(256), hs(128), pl.BlockSpec((s, 128), lambda h: (0, h))],
        out_specs=[hs(256), hs(256), hs(128)],
        out_shape=[jax.ShapeDtypeStruct((HEADS, s, 256), F32), jax.ShapeDtypeStruct((HEADS, s, 256), F32),
                   jax.ShapeDtypeStruct((HEADS, s, 128), F32)],
        compiler_params=_params("arbitrary"),
    )(q, k, v, do)


def _pool_windows(u, pad, s, g):
    pad[0:16, :] = jnp.zeros((16, 128), F32)
    cur, sel = u, None
    for j, k in enumerate((1, 2, 4, 8)):
        pad[16:16 + s, :] = cur
        cur = cur + pad[16 - k:16 - k + s, :]
        sel = cur if sel is None else jnp.where(g == j, cur, sel)
    return sel


def _pool_count(s, g):
    t = lax.broadcasted_iota(jnp.int32, (s, 1), 0)
    return jnp.minimum(t + 1, 2 << g).astype(F32)


def pool_fwd(proj, pw, ps, name):
    s = proj.shape[0]

    def body(u_ref, w_ref, s_ref, o_ref, pad):
        g = pl.program_id(0)
        u = u_ref[...]
        pooled = _pool_windows(u, pad, s, g) / _pool_count(s, g) - u
        o_ref[...] = _raw(pooled, w_ref[...], _NN) * s_ref[...]

    return pl.pallas_call(
        body, name=name, grid=(4,),
        in_specs=[pl.BlockSpec((s, 128), lambda g: (0, O_PU // 128 + g)), pl.BlockSpec((None, 128, 128), lambda g: (g, 0, 0)),
                  pl.BlockSpec((1, 128), lambda g: (0, g))],
        out_specs=pl.BlockSpec((s, 128), lambda g: (0, g)),
        out_shape=jax.ShapeDtypeStruct((s, 512), F32),
        scratch_shapes=[pltpu.VMEM((s + 16, 128), F32)],
        compiler_params=_params("arbitrary"),
    )(proj, pw, ps)


def pool_bwd(proj, pw, ps, do, name):
    s = proj.shape[0]

    def body(u_ref, w_ref, s_ref, do_ref, du_ref, dw_ref, ds_ref, pad):
        g = pl.program_id(0)
        u, w, dout = u_ref[...], w_ref[...], do_ref[...]
        cnt = _pool_count(s, g)
        pooled = _pool_windows(u, pad, s, g) / cnt - u
        mixed = _raw(pooled, w, _NN)
        ds_ref[...] = jnp.sum(dout * mixed, axis=0, keepdims=True)
        dmixed = dout * s_ref[...]
        dw_ref[...] = _raw(pooled, dmixed, _TN)
        dpooled = _raw(dmixed, w, _NT)
        dsel = dpooled / cnt
        pad[s:s + 16, :] = jnp.zeros((16, 128), F32)
        cur = jnp.where(g == 3, dsel, 0.0)
        for j, k in ((2, 8), (1, 4), (0, 2)):
            pad[0:s, :] = cur
            cur = cur + pad[k:k + s, :] + jnp.where(g == j, dsel, 0.0)
        pad[0:s, :] = cur
        cur = cur + pad[1:1 + s, :]
        du_ref[...] = (cur - dpooled).astype(BF16)

    return pl.pallas_call(
        body, name=name, grid=(4,),
        in_specs=[pl.BlockSpec((s, 128), lambda g: (0, O_PU // 128 + g)), pl.BlockSpec((None, 128, 128), lambda g: (g, 0, 0)),
                  pl.BlockSpec((1, 128), lambda g: (0, g)), pl.BlockSpec((s, 128), lambda g: (0, g))],
        out_specs=[pl.BlockSpec((s, 128), lambda g: (0, g)), pl.BlockSpec((None, 128, 128), lambda g: (g, 0, 0)),
                   pl.BlockSpec((1, 128), lambda g: (0, g))],
        out_shape=[jax.ShapeDtypeStruct((s, 512), BF16), jax.ShapeDtypeStruct((4, 128, 128), F32),
                   jax.ShapeDtypeStruct((1, 512), F32)],
        scratch_shapes=[pltpu.VMEM((s + 16, 128), F32)],
        compiler_params=_params("arbitrary"),
    )(proj, pw, ps, do)


def _xbc_col(i):
    return jnp.where(i < 2, O_XS // 512 + i, O_BC // 512)


def conv_fwd(proj, cw, cb, name):
    s = proj.shape[0]

    def body(x_ref, w_ref, b_ref, o_ref, t_ref, pad):
        pad[0:8, :] = jnp.zeros((8, 512), F32)
        pad[8:8 + s, :] = x_ref[...]
        y = b_ref[...] + sum(w_ref[k:k + 1, :] * pad[5 + k:5 + k + s, :] for k in range(4))
        act = y * _sig(y)
        o_ref[...] = act

        @pl.when(pl.program_id(0) < 2)
        def _():
            t_ref[...] = act.T

    return pl.pallas_call(
        body, name=name, grid=(3,),
        in_specs=[pl.BlockSpec((s, 512), lambda i: (0, _xbc_col(i))), pl.BlockSpec((4, 512), lambda i: (0, i)),
                  pl.BlockSpec((1, 512), lambda i: (0, i))],
        out_specs=[pl.BlockSpec((s, 512), lambda i: (0, i)), pl.BlockSpec((512, s), lambda i: (jnp.minimum(i, 1), 0))],
        out_shape=[jax.ShapeDtypeStruct((s, 1536), F32), jax.ShapeDtypeStruct((D, s), F32)],
        scratch_shapes=[pltpu.VMEM((s + 8, 512), F32)],
        compiler_params=_params("arbitrary"),
    )(proj, cw, cb)


def conv_bwd(proj, cw, cb, dxt, dbm, dcm, name):
    s = proj.shape[0]

    def body(x_ref, w_ref, b_ref, dxt_ref, dbm_ref, dcm_ref, dx_ref, dw_ref, db_ref, pad, pad2):
        pad[0:8, :] = jnp.zeros((8, 512), F32)
        pad[8:8 + s, :] = x_ref[...]
        y = b_ref[...] + sum(w_ref[k:k + 1, :] * pad[5 + k:5 + k + s, :] for k in range(4))
        sg = _sig(y)

        @pl.when(pl.program_id(0) < 2)
        def _():
            pad2[0:s, :] = dxt_ref[...].T

        @pl.when(pl.program_id(0) == 2)
        def _():
            pad2[0:s, 0:256] = dbm_ref[...]
            pad2[0:s, 256:512] = dcm_ref[...]

        dy = pad2[0:s, :] * (sg * (1.0 + y * (1.0 - sg)))
        db_ref[...] = jnp.sum(dy, axis=0, keepdims=True)
        for k in range(4):
            dw_ref[k:k + 1, :] = jnp.sum(dy * pad[5 + k:5 + k + s, :], axis=0, keepdims=True)
        pad2[s:s + 8, :] = jnp.zeros((8, 512), F32)
        pad2[0:s, :] = dy
        dx_ref[...] = sum(w_ref[k:k + 1, :] * pad2[3 - k:3 - k + s, :] for k in range(4)).astype(BF16)

    return pl.pallas_call(
        body, name=name, grid=(3,),
        in_specs=[pl.BlockSpec((s, 512), lambda i: (0, _xbc_col(i))), pl.BlockSpec((4, 512), lambda i: (0, i)),
                  pl.BlockSpec((1, 512), lambda i: (0, i)), pl.BlockSpec((512, s), lambda i: (jnp.minimum(i, 1), 0)),
                  pl.BlockSpec((s, 256), lambda i: (0, 0)), pl.BlockSpec((s, 256), lambda i: (0, 0))],
        out_specs=[pl.BlockSpec((s, 512), lambda i: (0, i)), pl.BlockSpec((4, 512), lambda i: (0, i)),
                   pl.BlockSpec((1, 512), lambda i: (0, i))],
        out_shape=[jax.ShapeDtypeStruct((s, 1536), BF16), jax.ShapeDtypeStruct((4, 1536), F32),
                   jax.ShapeDtypeStruct((1, 1536), F32)],
        scratch_shapes=[pltpu.VMEM((s + 8, 512), F32), pltpu.VMEM((s + 8, 512), F32)],
        compiler_params=_params("arbitrary"),
    )(proj, cw, cb, dxt, dbm, dcm)


def _ssd_chunk(xt, dtr, bm, cm, hprev, alog, dbias, dskip):
    ln = 128
    a = -jnp.exp(alog)
    dt_r = softplus(dtr + dbias)
    da_r = dt_r * a
    li = lax.broadcasted_iota(jnp.int32, (1, ln, ln), 1)
    si = lax.broadcasted_iota(jnp.int32, (1, ln, ln), 2)
    causal = si <= li
    acs_c = jnp.sum(jnp.where(causal, da_r, 0.0), axis=2, keepdims=True)
    acs_r = jnp.sum(jnp.where(li == si, acs_c, 0.0), axis=1, keepdims=True)
    acs_last = jnp.sum(da_r, axis=2, keepdims=True)
    decay = jnp.exp(jnp.where(causal, acs_c - acs_r, -jnp.inf))
    m = mm_nt(cm, bm)[None] * decay
    xdt = xt * dt_r
    y_diag = bmm_nt(xdt, m)
    bb = jnp.broadcast_to(bm[None], (8, ln, ln))
    cc = jnp.broadcast_to(cm[None], (8, ln, ln))
    states = bmm_nn(xdt * jnp.exp(acs_last - acs_r), bb)
    y_off = bmm_nt(hprev, cc) * jnp.exp(acs_r)
    hnew = hprev * jnp.exp(acs_last) + states
    return y_diag + y_off + xt * dskip, hnew


def _ssd_specs(nc, rev):
    cix = (lambda c: nc - 1 - c) if rev else (lambda c: c)
    hv = pl.BlockSpec((8, 1, 1), lambda g, c: (g, 0, 0))
    return [pl.BlockSpec((8, 64, 128), lambda g, c: (g, 0, cix(c))), pl.BlockSpec((8, 1, 128), lambda g, c: (g, 0, cix(c))),
            pl.BlockSpec((128, 128), lambda g, c: (cix(c), 8 + g)),
            pl.BlockSpec((128, 128), lambda g, c: (cix(c), 10 + g))], hv, cix


def ssd_fwd(xt, dtr, xbc, alog, dbias, dskip, name):
    s = xt.shape[2]
    nc = s // 128
    specs, hv, _ = _ssd_specs(nc, False)

    def body(x_ref, dr_ref, b_ref, c_ref, al_ref, db_ref, dk_ref, y_ref, hs_ref, h_scr):
        @pl.when(pl.program_id(1) == 0)
        def _():
            h_scr[...] = jnp.zeros_like(h_scr)
        hp = h_scr[...]
        hs_ref[...] = hp
        y, hn = _ssd_chunk(x_ref[...], dr_ref[...], b_ref[...], c_ref[...], hp, al_ref[...], db_ref[...], dk_ref[...])
        y_ref[...] = y
        h_scr[...] = hn

    return pl.pallas_call(
        body, name=name, grid=(2, nc), in_specs=specs + [hv, hv, hv],
        out_specs=[pl.BlockSpec((8, 64, 128), lambda g, c: (g, 0, c)),
                   pl.BlockSpec((None, None, 8, 64, 128), lambda g, c: (g, c, 0, 0, 0))],
        out_shape=[jax.ShapeDtypeStruct((16, 64, s), F32), jax.ShapeDtypeStruct((2, nc, 8, 64, 128), F32)],
        scratch_shapes=[pltpu.VMEM((8, 64, 128), F32)],
        compiler_params=_params("arbitrary", "arbitrary"),
    )(xt, dtr, xbc, xbc, alog, dbias, dskip)


def ssd_bwd(xt, dtr, xbc, alog, dbias, dskip, hs, dyt, name):
    s = xt.shape[2]
    nc = s // 128
    specs, hv, cix = _ssd_specs(nc, True)

    def body(x_ref, dr_ref, b_ref, c_ref, al_ref, db_ref, dk_ref, hs_ref, dy_ref,
             dx_out, ddr_out, dbm_out, dcm_out, dal_out, ddb_out, ddk_out, dh_scr):
        @pl.when(pl.program_id(1) == 0)
        def _():
            dh_scr[...] = jnp.zeros_like(dh_scr)
            dal_out[...] = jnp.zeros_like(dal_out)
            ddb_out[...] = jnp.zeros_like(ddb_out)
            ddk_out[...] = jnp.zeros_like(ddk_out)
        _, vjp = jax.vjp(_ssd_chunk, x_ref[...], dr_ref[...], b_ref[...], c_ref[...], hs_ref[...],
                         al_ref[...], db_ref[...], dk_ref[...])
        g = vjp((dy_ref[...], dh_scr[...]))
        dx_out[...] = g[0]
        ddr_out[...] = g[1]
        dbm_out[...] = g[2]
        dcm_out[...] = g[3]
        dh_scr[...] = g[4]
        dal_out[...] += g[5]
        ddb_out[...] += g[6]
        ddk_out[...] += g[7]

    return pl.pallas_call(
        body, name=name, grid=(2, nc),
        in_specs=specs + [hv, hv, hv, pl.BlockSpec((None, None, 8, 64, 128), lambda g, c: (g, cix(c), 0, 0, 0)),
                          pl.BlockSpec((8, 64, 128), lambda g, c: (g, 0, cix(c)))],
        out_specs=[pl.BlockSpec((8, 64, 128), lambda g, c: (g, 0, cix(c))), pl.BlockSpec((8, 1, 128), lambda g, c: (g, 0, cix(c))),
                   pl.BlockSpec((128, 128), lambda g, c: (cix(c), g)),
                   pl.BlockSpec((128, 128), lambda g, c: (cix(c), g)), hv, hv, hv],
        out_shape=[jax.ShapeDtypeStruct((16, 64, s), F32), jax.ShapeDtypeStruct((16, 1, s), F32),
                   jax.ShapeDtypeStruct((s, 256), F32),
                   jax.ShapeDtypeStruct((s, 256), F32)] + [jax.ShapeDtypeStruct((16, 1, 1), F32)] * 3,
        scratch_shapes=[pltpu.VMEM((8, 64, 128), F32)],
        compiler_params=_params("arbitrary", "arbitrary"),
    )(xt, dtr, xbc, xbc, alog, dbias, dskip, hs, dyt)


def _merge(oa, ob, y, z, gla, glb, glc, x, g1, nw, ea, eb, ec, eo, wba, wbb, wbc, wout):
    gated = y * (z * _sig(z))
    sq = gated * gated
    left = lax.broadcasted_iota(jnp.int32, (1, D), 1) < 512
    ms0 = jnp.sum(jnp.where(left, sq, 0.0), axis=-1, keepdims=True) / 512.0
    ms1 = jnp.sum(jnp.where(left, 0.0, sq), axis=-1, keepdims=True) / 512.0
    oc = gated * jnp.where(left, lax.rsqrt(ms0 + EPS), lax.rsqrt(ms1 + EPS)) * nw
    ya, yb, yc = mm_nc(oa, wba) + ea, mm_nc(ob, wbb) + eb, mm_nc(oc, wbc) + ec
    merged = _sig(gla) * ya + _sig(glb) * yb + _sig(glc) * yc
    x1 = x + g1 * (mm_nc(merged, wout) + eo)
    return x1, (oc, merged)


def _merge_specs(tr):
    row = lambda w: pl.BlockSpec((tr, w), lambda i: (i, 0))
    acts = [row(D), row(512), pl.BlockSpec((D, tr), lambda i: (0, i)), pl.BlockSpec((tr, D), lambda i: (i, O_Z // D)),
            pl.BlockSpec((tr, 3 * D), lambda i: (i, 0)), row(D), pl.BlockSpec((8, D), lambda i: (0, 0))]
    cst = lambda r: pl.BlockSpec((r, D), lambda i: (0, 0), **CONST)
    return acts, [cst(D), cst(512), cst(D), cst(D)], row


def merge_fwd(oa, ob, y, proj, x, mvec, wba, wbb, wbc, wout, name):
    s = x.shape[0]
    tr = _pick(s, 256)
    acts, wts, row = _merge_specs(tr)

    def body(oa_ref, ob_ref, y_ref, z_ref, gl_ref, x_ref, mv_ref, wba_ref, wbb_ref, wbc_ref, wout_ref, o_ref):
        zero = jnp.zeros((1, D), F32)
        x1, _ = _merge(oa_ref[...], ob_ref[...], y_ref[...].T, z_ref[...], gl_ref[:, 0:D], gl_ref[:, D:2 * D],
                       gl_ref[:, 2 * D:3 * D], x_ref[...], mv_ref[0:1, :], mv_ref[1:2, :], zero, zero, zero, zero,
                       wba_ref[...], wbb_ref[...], wbc_ref[...], wout_ref[...])
        o_ref[...] = x1

    return pl.pallas_call(
        body, name=name, grid=(s // tr,), in_specs=acts + wts, out_specs=row(D),
        out_shape=jax.ShapeDtypeStruct((s, D), F32), compiler_params=_params("arbitrary"),
    )(oa, ob, y, proj, proj, x, mvec, wba, wbb, wbc, wout)


def merge_bwd(oa, ob, y, proj, x, mvec, wba, wbb, wbc, wout, dx1, name):
    s = x.shape[0]
    tr = _pick(s, 128)
    acts, wts, row = _merge_specs(tr)

    def body(oa_ref, ob_ref, y_ref, z_ref, gl_ref, x_ref, mv_ref, wba_ref, wbb_ref, wbc_ref, wout_ref, dx1_ref,
             doa_o, dob_o, dy_o, dz_o, dgl_o, dx_o, dmv_o, dya_o, dyb_o, dyc_o, dpre_o, oc_o, mg_o):
        zero = jnp.zeros((tr, D), F32)
        wts_ = (wba_ref[...], wbb_ref[...], wbc_ref[...], wout_ref[...])
        f = lambda *a: _merge(*a, *wts_)
        _, vjp, (oc, merged) = jax.vjp(
            f, oa_ref[...], ob_ref[...], y_ref[...].T, z_ref[...], gl_ref[:, 0:D], gl_ref[:, D:2 * D],
            gl_ref[:, 2 * D:3 * D], x_ref[...], mv_ref[0:1, :], mv_ref[1:2, :], zero, zero, zero, zero, has_aux=True)
        g = vjp(dx1_ref[...])
        doa_o[...] = g[0]
        dob_o[...] = g[1]
        dy_o[...] = g[2].T
        dz_o[...] = g[3].astype(BF16)
        dgl_o[:, 0:D] = g[4].astype(BF16)
        dgl_o[:, D:2 * D] = g[5].astype(BF16)
        dgl_o[:, 2 * D:3 * D] = g[6].astype(BF16)
        dx_o[...] = g[7]

        @pl.when(pl.program_id(0) == 0)
        def _():
            dmv_o[...] = jnp.zeros_like(dmv_o)

        dmv_o[0:1, :] += g[8]
        dmv_o[1:2, :] += g[9]
        dya_o[...] = g[10].astype(BF16)
        dyb_o[...] = g[11].astype(BF16)
        dyc_o[...] = g[12].astype(BF16)
        dpre_o[...] = g[13].astype(BF16)
        oc_o[...] = oc.astype(BF16)
        mg_o[...] = merged.astype(BF16)

    sd = lambda w, dt: jax.ShapeDtypeStruct((s, w), dt)
    return pl.pallas_call(
        body, name=name, grid=(s // tr,), in_specs=acts + wts + [row(D)],
        out_specs=[row(D), row(512), pl.BlockSpec((D, tr), lambda i: (0, i)), row(D), row(3 * D), row(D),
                   pl.BlockSpec((8, D), lambda i: (0, 0))] + [row(D)] * 6,
        out_shape=[sd(D, F32), sd(512, F32), jax.ShapeDtypeStruct((D, s), F32), sd(D, BF16), sd(3 * D, BF16), sd(D, F32),
                   jax.ShapeDtypeStruct((8, D), F32)] + [sd(D, BF16)] * 6,
        compiler_params=_params("arbitrary"),
    )(oa, ob, y, proj, proj, x, mvec, wba, wbb, wbc, wout, dx1)


def _conv3(u_scr, w_ref, first, rows, lanes):
    return sum(w_ref[k:k + 1, :] * u_scr[first + k:first + k + rows, lanes] for k in range(3))


def _ffn_tile_specs(tf, tile):
    def at(rows, off):
        return pl.BlockSpec((rows, tf), lambda *g: (0, off + tile(*g)))

    def wt(off):
        return pl.BlockSpec((tf, D), lambda *g: (off + tile(*g), 0))
    return [wt(0), wt(FFN_NT), at(3, 0), at(3, FFN_NT), at(1, 0), at(1, FFN_NT)]


def ffn_fwd(x1, fvec, wup, cw, cb, wdn, name):
    s = x1.shape[0]
    tr, tf = _pick(s, 512), FFN_TILE
    lg, lv = slice(0, tf), slice(tf, 2 * tf)

    def body(x_ref, v_ref, wg_ref, wv_ref, cwg_ref, cwv_ref, cbg_ref, cbv_ref, wd_ref, x2_ref, h_ref, pre_ref,
             h_scr, u_scr, acc):
        i, t = pl.program_id(0), pl.program_id(1)

        @pl.when(t == 0)
        def _():
            @pl.when(i == 0)
            def _():
                h_scr[0:16, :] = jnp.zeros((16, D), BF16)

            @pl.when(i > 0)
            def _():
                h_scr[0:16, :] = h_scr[tr:tr + 16, :]

            h = (_rms(x_ref[...], v_ref[0:1, :], D) * (1.0 + v_ref[2:3, :]) + v_ref[1:2, :]).astype(BF16)
            h_scr[16:16 + tr, :] = h
            h_ref[...] = h
            acc[...] = jnp.zeros_like(acc)

        u_scr[:, lg] = _raw(h_scr[...], wg_ref[...], _NT)
        u_scr[:, lv] = _raw(h_scr[...], wv_ref[...], _NT)
        cg = _conv3(u_scr, cwg_ref, 14, tr, lg) + cbg_ref[...]
        cval = _conv3(u_scr, cwv_ref, 14, tr, lv) + cbv_ref[...]
        acc[...] += _raw(cg * _sig(cg) * cval, wd_ref[...], _NN)

        @pl.when(t == FFN_NT - 1)
        def _():
            pre_ref[...] = acc[...]
            x2_ref[...] = x_ref[...] + v_ref[3:4, :] * acc[...]

    row = pl.BlockSpec((tr, D), lambda i, t: (i, 0))
    return pl.pallas_call(
        body, name=name, grid=(s // tr, FFN_NT),
        in_specs=[row, pl.BlockSpec((8, D), lambda i, t: (0, 0))] + _ffn_tile_specs(tf, lambda i, t: t)
                 + [pl.BlockSpec((tf, D), lambda i, t: (t, 0))],
        out_specs=[row, row, row],
        out_shape=[jax.ShapeDtypeStruct((s, D), F32), jax.ShapeDtypeStruct((s, D), BF16), jax.ShapeDtypeStruct((s, D), F32)],
        scratch_shapes=[pltpu.VMEM((tr + 16, D), BF16), pltpu.VMEM((tr + 16, 2 * tf), F32), pltpu.VMEM((tr, D), F32)],
        compiler_params=_params("arbitrary", "arbitrary"),
    )(x1, fvec, wup, wup, cw, cw, cb, cb, wdn)


def ffn_bwd(h2, dx2, fvec, wup, cw, cb, wdn, name):
    s = h2.shape[0]
    tr, tf = _pick(s, 512), FFN_TILE
    ni, nb = s // tr, s // 16
    lg, lv = slice(0, tf), slice(tf, 2 * tf)

    def body(hp_ref, hm_ref, hn_ref, dm_ref, dn_ref, v_ref, wg_ref, wv_ref, cwg_ref, cwv_ref, cbg_ref, cbv_ref, wd_ref,
             dup_ref, act_ref, dcw_ref, u_scr, dc_scr):
        i = pl.program_id(1)
        hfull = jnp.concatenate([jnp.where(i > 0, hp_ref[...], jnp.zeros((16, D), BF16)), hm_ref[...],
                                 jnp.where(i < ni - 1, hn_ref[...], jnp.zeros((16, D), BF16))], axis=0)
        u_scr[:, lg] = _raw(hfull, wg_ref[...], _NT)
        u_scr[:, lv] = _raw(hfull, wv_ref[...], _NT)
        cg = _conv3(u_scr, cwg_ref, 14, tr + 16, lg) + cbg_ref[...]
        cval = _conv3(u_scr, cwv_ref, 14, tr + 16, lv) + cbv_ref[...]
        g2 = v_ref[3:4, :]
        dpre = jnp.concatenate([dm_ref[...] * g2, jnp.where(i < ni - 1, dn_ref[...], 0.0) * g2], axis=0)
        dact = _raw(dpre, wd_ref[...], _NT)
        sg = _sig(cg)
        sl = cg * sg
        dc_scr[:, lg] = dact * cval * (sg * (1.0 + cg * (1.0 - sg)))
        dc_scr[:, lv] = dact * sl
        act_ref[...] = (sl * cval)[0:tr, :].astype(BF16)

        @pl.when(i == 0)
        def _():
            dcw_ref[...] = jnp.zeros_like(dcw_ref)

        for half, lanes, cw_ref in ((0, lg, cwg_ref), (1, lv, cwv_ref)):
            dup_ref[half] = sum(cw_ref[k:k + 1, :] * dc_scr[2 - k:2 - k + tr, lanes] for k in range(3)).astype(BF16)
            dcm = dc_scr[0:tr, lanes]
            for k in range(3):
                dcw_ref[half, k:k + 1, :] += jnp.sum(dcm * u_scr[14 + k:14 + k + tr, lanes], axis=0, keepdims=True)
            dcw_ref[half, 3:4, :] += jnp.sum(dcm, axis=0, keepdims=True)

    r16 = tr // 16
    prev = lambda t, i: (jnp.maximum(i * r16 - 1, 0), 0)
    nxt = lambda t, i: (jnp.minimum((i + 1) * r16, nb - 1), 0)
    main = lambda t, i: (i, 0)
    return pl.pallas_call(
        body, name=name, grid=(FFN_NT, ni),
        in_specs=[pl.BlockSpec((16, D), prev), pl.BlockSpec((tr, D), main), pl.BlockSpec((16, D), nxt),
                  pl.BlockSpec((tr, D), main), pl.BlockSpec((16, D), nxt), pl.BlockSpec((8, D), lambda t, i: (0, 0))]
                 + _ffn_tile_specs(tf, lambda t, i: t) + [pl.BlockSpec((tf, D), lambda t, i: (t, 0))],
        out_specs=[pl.BlockSpec((2, tr, tf), lambda t, i: (0, i, t)), pl.BlockSpec((tr, tf), lambda t, i: (i, t)),
                   pl.BlockSpec((2, 8, tf), lambda t, i: (0, 0, t))],
        out_shape=[jax.ShapeDtypeStruct((2, s, FFN), BF16), jax.ShapeDtypeStruct((s, FFN), BF16),
                   jax.ShapeDtypeStruct((2, 8, FFN), F32)],
        scratch_shapes=[pltpu.VMEM((tr + 32, 2 * tf), F32), pltpu.VMEM((tr + 16, 2 * tf), F32)],
        compiler_params=_params("arbitrary", "arbitrary"),
    )(h2, h2, h2, dx2, dx2, fvec, wup, wup, cw, cw, cb, cb, wdn)


def loss_head(y, target):
    s = y.shape[0]
    tr = _pick(s, 512)

    def body(y_ref, t_ref, dx_ref, l_ref):
        @pl.when(pl.program_id(0) == 0)
        def _():
            l_ref[...] = jnp.zeros_like(l_ref)
        err = y_ref[...] - t_ref[...]
        dx_ref[...] = err / float(D)
        l_ref[...] += 0.5 * jnp.sum(jnp.sum(err * err, axis=-1, keepdims=True) / float(D), axis=0, keepdims=True)

    row = pl.BlockSpec((tr, D), lambda i: (i, 0))
    return pl.pallas_call(
        body, name="loss_head", grid=(s // tr,), in_specs=[row, row],
        out_specs=[row, pl.BlockSpec((8, 128), lambda i: (0, 0))],
        out_shape=[jax.ShapeDtypeStruct((s, D), F32), jax.ShapeDtypeStruct((8, 128), F32)],
        compiler_params=_params("arbitrary"),
    )(y, target)


def adamw(parts, w, m, v, name, tok=None):
    nseg = len(parts)
    p, r, c = parts[0].shape
    tok = jnp.zeros((8, 128), F32) if tok is None else tok
    cap = 256 if c > 128 else 2048
    step = lambda q, l, i, ni: jnp.clip((l - q) * ni + i, 0, ni - 1)
    if r <= cap or any(r % t == 0 for t in range(8, cap + 1, 8)):
        tr, tc = _pick(r, cap, 8), c
        ni = r // tr
        row = pl.BlockSpec((None, tr, tc), lambda l, i: (l, i, 0))
        part = lambda q: pl.BlockSpec((p, tr, tc), lambda l, i: (0, step(q, l, i, ni), 0))
    else:
        tr, tc = r, _pick(c, 256)
        ni = c // tc
        row = pl.BlockSpec((None, tr, tc), lambda l, i: (l, 0, i))
        part = lambda q: pl.BlockSpec((p, tr, tc), lambda l, i: (0, 0, step(q, l, i, ni)))

    def body(*refs):
        p_refs = refs[:nseg]
        w_ref, m_ref, v_ref, _, g_out, d_out, m_out, v_out, g_scr = refs[nseg:]
        for q in range(nseg):
            @pl.when(pl.program_id(0) == q)
            def _(q=q):
                g = p_refs[q][0].astype(F32)
                for j in range(1, p):
                    g = g + p_refs[q][j].astype(F32)
                g_scr[...] = g
        g = g_scr[...]
        mn = B1 * m_ref[...] + (1.0 - B1) * g
        vn = B2 * v_ref[...] + (1.0 - B2) * (g * g)
        m_hat = mn / (1.0 - B1 ** STEP)
        v_hat = vn / (1.0 - B2 ** STEP)
        g_out[...] = g
        d_out[...] = -LR * (m_hat / (jnp.sqrt(v_hat) + ADAM_EPS) + WD * w_ref[...])
        m_out[...] = mn
        v_out[...] = vn

    return pl.pallas_call(
        body, name=name, grid=(nseg, ni),
        in_specs=[part(q) for q in range(nseg)] + [row, row, row, pl.BlockSpec((8, 128), lambda l, i: (0, 0))],
        out_specs=[row] * 4, out_shape=[jax.ShapeDtypeStruct((nseg, r, c), F32)] * 4,
        scratch_shapes=[pltpu.VMEM((tr, tc), F32)],
        compiler_params=_params("arbitrary", "arbitrary"),
    )(*parts, w, m, v, tok)


def _padc(a, n):
    return jnp.pad(a, [(0, 0)] * (a.ndim - 1) + [(0, n - a.shape[-1])])


def _swap16(a):
    return jnp.concatenate([a[..., 16:32], a[..., 0:16]], axis=-1)


def _shard_cols(g8, a, b):
    c = g8.shape[2]
    return [g8[j][:, max(a, j * c) - j * c:min(b, (j + 1) * c) - j * c] for j in range(a // c, (b - 1) // c + 1)]


def _padr(a, n):
    return jnp.pad(a, ((0, n - a.shape[0]), (0, 0)))


def _swap16r(a):
    return jnp.concatenate([a[16:32], a[0:16]], axis=0)


def _win_layout(g8):
    w = g8.reshape(NDEV * g8.shape[1], g8.shape[2])
    kr = w[640:672]
    return jnp.concatenate([w[3760:6832], w[2208:3232], w[1184:2208], w[672:1184], w[3232:3744], w[384:640],
                            _padr(kr, 128), _padr(_swap16r(kr), 128), _padr(w[3744:3760], 128),
                            jnp.zeros((128, w.shape[1]), w.dtype), w[0:384]], axis=0)


def _win_grad_shards(g):
    kr = (g[O_KR:O_KR + 32].astype(F32) + _swap16r(g[O_KRS:O_KRS + 32].astype(F32))).astype(g.dtype)
    full = jnp.concatenate([g[O_QL:O_QL + 384], g[O_CKV:O_CKV + 256], kr, g[O_PU:O_PU + 512], g[O_Z:O_Z + D],
                            g[O_XS:O_XS + D], g[O_BC:O_BC + 512], g[O_DT:O_DT + 16], g[O_G:O_G + 3 * D]], axis=0)
    return full.reshape(NDEV, full.shape[0] // NDEV, full.shape[1]).astype(BF16)


def _wq_layout(w):
    w = w.reshape(384, HEADS, 96).transpose(1, 0, 2)
    rope = w[:, :, 64:96]
    return jnp.concatenate([_padc(w[:, :, 0:64], 128), _padc(rope, 128), _padc(_swap16(rope), 128)], axis=2)


def _wq_unlayout(g):
    rope = g[:, :, 128:160] + _swap16(g[:, :, 256:288])
    return jnp.concatenate([g[:, :, 0:64], rope], axis=2).transpose(1, 0, 2).reshape(384, HEADS * 96)


def _wkv_layout(w):
    w = w.reshape(256, HEADS, 128).transpose(1, 0, 2)
    return jnp.concatenate([_padc(w[:, :, 0:64], 128), _padc(w[:, :, 64:128], 128)], axis=2)


def _wkv_unlayout(g):
    return jnp.concatenate([g[:, :, 0:64], g[:, :, 128:192]], axis=2).transpose(1, 0, 2).reshape(256, HEADS * 128)


def _wba_layout(w):
    return jnp.pad(w.reshape(HEADS, 64, D), ((0, 0), (0, 64), (0, 0))).reshape(HEADS * 128, D)


def _rows8(rows, width):
    out = jnp.stack([_padc(r.astype(F32), width) for r in rows])
    return jnp.pad(out, ((0, 8 - out.shape[0]), (0, 0)))


def _mla_vec(qa, kva, qn, kn):
    def row(n):
        return jnp.concatenate([_padc(n[0:64], 128), _padc(n[64:96], 128), _padc(_swap16(n[64:96]), 128)])
    return _rows8([qa, kva, row(qn), row(kn)], 512)


def _mla_unvec(g):
    def un(r):
        return jnp.concatenate([r[0:64], r[128:160] + _swap16(r[256:288])])
    return g[0, 0:384], g[1, 0:256], un(g[2]), un(g[3])


SMALL = (("ada_b", (6 * D,)), ("norm1_w", (D,)), ("q_a_norm", (384,)), ("kv_a_norm", (256,)), ("q_norm", (96,)),
         ("k_norm", (96,)), ("pool_w", (4, 128, 128)), ("pool_scale", (512,)), ("ssd_conv_b", (1536,)),
         ("ssd_dt_bias", (16,)), ("ssd_a_log", (16,)), ("ssd_d", (16,)), ("ssd_norm_w", (D,)), ("norm2_w", (D,)),
         ("ffn_conv_b", (2 * FFN,)), ("ssd_conv_w", (4, 1536)), ("ffn_conv_w", (3, 2 * FFN)))
SHARDED_SMALL = {"ssd_conv_w": 192, "ffn_conv_w": 704}


def _pack_rows(shp):
    return -(-math.prod(shp) // 1024) * 8


def _pack(small):
    pieces = []
    for n, shp in SMALL:
        pieces.append(small[n].reshape(-1).astype(F32))
        fill = _pack_rows(shp) * 128 - math.prod(shp)
        if fill:
            pieces.append(jnp.zeros((fill,), F32))
    return jnp.concatenate(pieces).reshape(-1, 128)


def _unpack_parts(packs):
    out, off = {}, 0
    for n, shp in SMALL:
        rows, size = _pack_rows(shp), math.prod(shp)
        r, c = math.prod(shp[:-1]), shp[-1]
        per_layer = [pk[:, off:off + rows].reshape(NDEV, rows * 128)[:, 0:size].reshape(NDEV, r, c) for pk in packs]
        out[n] = jnp.concatenate(per_layer, axis=1)
        off += rows
    return out


GROUP_A = ("w_in", "w_q_b", "w_kv_b")
GROUP_B = ("w_branch", "w_out", "ffn_up", "ffn_down")
BIG = GROUP_A + GROUP_B
SCATTER_FFN, SCATTER_MERGE = ("ffn_up", "ffn_down"), ("w_branch", "w_out")
COL_SHARDED = ("w_q_b", "w_kv_b")
TRANSPOSED = ("w_in", "ffn_up")


def _behind(arrs, tok):
    arrs = list(arrs)
    j = min(range(len(arrs)), key=lambda q: arrs[q].size)
    arrs[j] = arrs[j] + tok[0, 0].astype(arrs[j].dtype)
    return arrs


def _gathered_full(g, name):
    if name in COL_SHARDED:
        return g.transpose(1, 0, 2).reshape(g.shape[1], NDEV * g.shape[2])
    return g.reshape(NDEV * g.shape[1], g.shape[2])


def _to_shards(full, name):
    if name == "w_in":
        return _win_grad_shards(full)
    if name in COL_SHARDED:
        r, c = full.shape
        return full.reshape(r, NDEV, c // NDEV).transpose(1, 0, 2).astype(BF16)
    r, c = full.shape
    return full.reshape(NDEV, r // NDEV, c).astype(BF16)


def _fwd_a(x, lw, mod, cos2, sin2, l, tok):
    sh1, sc1, g1, sh2, sc2, g2 = [mod[j * D:(j + 1) * D] for j in range(6)]
    vec1 = _rows8([lw["norm1_w"], sh1, sc1], D) + tok[0, 0]
    proj, h1, dt_cols = norm_proj_fwd(x, vec1, lw["win"], f"inproj_fwd{l}")
    q, k, v = mla_pre_fwd(proj, lw["wq"], lw["wkv"], lw["mla_vec"], cos2, sin2, f"mla_pre_fwd{l}")
    oa = mla_attn_fwd(q, k, v, f"mla_attn_fwd{l}")
    ob = pool_fwd(proj, lw["pool_w"], lw["pool_scale"].reshape(1, 512), f"pool_fwd{l}")
    xbc, xt = conv_fwd(proj, lw["ssd_conv_w"], lw["ssd_conv_b"].reshape(1, 1536), f"conv_fwd{l}")
    s = x.shape[0]
    xt = xt.reshape(16, 64, s)
    dt = dt_cols[:, 0:16].T
    dtr = dt[:, None, :]
    hv = lambda a: a.reshape(16, 1, 1)
    yt, hs = ssd_fwd(xt, dtr, xbc, hv(lw["ssd_a_log"]), hv(lw["ssd_dt_bias"]), hv(lw["ssd_d"]), f"ssd_fwd{l}")
    return dict(x=x, vec1=vec1, proj=proj, h1=h1, q=q, k=k, v=v, oa=oa, ob=ob, xbc=xbc, xt=xt, dtr=dtr,
                hs=hs, yt=yt.reshape(D, s), mvec=_rows8([g1, lw["ssd_norm_w"]], D),
                fvec=_rows8([lw["norm2_w"], sh2, sc2, g2], D))


def _fwd_b(sv, lw, l, tok):
    sv["mvec"] = sv["mvec"] + tok[0, 0]
    x1 = merge_fwd(sv["oa"], sv["ob"], sv["yt"], sv["proj"], sv["x"], sv["mvec"], lw["wba"], lw["wbb"], lw["wbc"],
                   lw["wout"], f"merge_fwd{l}")
    x2, h2, pre = ffn_fwd(x1, sv["fvec"], lw["wup"], lw["ffn_conv_w"], lw["ffn_conv_b"].reshape(1, 2 * FFN), lw["wdn"],
                          f"ffn_fwd{l}")
    sv.update(x1=x1, h2=h2, pre=pre)
    return x2


def _bwd_ffn(dx2, lw, sv, l, tok):
    fvec = sv["fvec"] + tok[0, 0]
    dup, act, dcw = ffn_bwd(sv["h2"], dx2, fvec, lw["wup"], lw["ffn_conv_w"], lw["ffn_conv_b"].reshape(1, 2 * FFN),
                            lw["wdn"], f"ffn_bwd{l}")
    grads = dict(ffn_down=tn_matmul(act, dx2, f"dw_down{l}", scale=fvec[3:4]),
                 ffn_up=tn_matmul(dup, sv["h2"], f"dw_up{l}"))
    small = dict(ffn_conv_w=jnp.concatenate([dcw[0, 0:3], dcw[1, 0:3]], axis=1),
                 ffn_conv_b=jnp.concatenate([dcw[0, 3], dcw[1, 3]]))
    return dup, grads, small


def _bwd_merge(dx2, dup, lw, sv, l, tok, small):
    grads = {}
    fvec = sv["fvec"] + tok[0, 0]
    dx1, dfvec = norm_proj_bwd(sv["x1"], fvec, dup, lw["wup"], dx2, sv["pre"], f"ffn_norm_bwd{l}")
    small["norm2_w"] = dfvec[0]
    (doa, dob, dyt, dz, dgl, dx, dmvec, dya, dyb, dyc, dpre, oc, merged) = merge_bwd(
        sv["oa"], sv["ob"], sv["yt"], sv["proj"], sv["x"], sv["mvec"], lw["wba"], lw["wbb"], lw["wbc"], lw["wout"], dx1,
        f"merge_bwd{l}")
    dwba = tn_matmul(sv["oa"], dya, f"dw_ba{l}").reshape(HEADS, 128, D)[:, 0:64].reshape(512, D)
    grads["w_branch"] = jnp.concatenate([dwba, tn_matmul(sv["ob"], dyb, f"dw_bb{l}"), tn_matmul(oc, dyc, f"dw_bc{l}")])
    grads["w_out"] = tn_matmul(merged, dpre, f"dw_out{l}")
    small["ssd_norm_w"] = dmvec[1]
    small["dmod_b"] = (dmvec[0], dfvec[1], dfvec[2], dfvec[3])
    return dx, dict(doa=doa, dob=dob, dyt=dyt, dz=dz, dgl=dgl), grads, small


def _bwd_a(dx, cot, lw, sv, cos2, sin2, l, tok, small):
    s = dx.shape[0]
    grads = {}
    doa, dob, dz, dgl = cot["doa"], cot["dob"], cot["dz"], cot["dgl"]
    hv = lambda a: a.reshape(16, 1, 1)
    dxt, ddtr, dbm, dcm, dal, ddb, ddk = ssd_bwd(
        sv["xt"], sv["dtr"], sv["xbc"], hv(lw["ssd_a_log"]) + tok[0, 0], hv(lw["ssd_dt_bias"]),
        hv(lw["ssd_d"]), sv["hs"], cot["dyt"].reshape(16, 64, s), f"ssd_bwd{l}")
    small["ssd_a_log"], small["ssd_dt_bias"], small["ssd_d"] = dal.reshape(16), ddb.reshape(16), ddk.reshape(16)
    dxbc, dscw, dscb = conv_bwd(sv["proj"], lw["ssd_conv_w"], lw["ssd_conv_b"].reshape(1, 1536), dxt.reshape(D, s),
                                dbm, dcm, f"conv_bwd{l}")
    small["ssd_conv_w"], small["ssd_conv_b"] = dscw, dscb.reshape(1536)
    ddt = ddtr[:, 0, :].T
    du, dpw, dps = pool_bwd(sv["proj"], lw["pool_w"], lw["pool_scale"].reshape(1, 512), dob, f"pool_bwd{l}")
    small["pool_w"], small["pool_scale"] = dpw, dps.reshape(512)
    dq, dk, dv = mla_attn_bwd(sv["q"], sv["k"], sv["v"], doa, f"mla_attn_bwd{l}")
    dql, dckv, dkr, dkrs, dwq, dwkv, dmv = mla_pre_bwd(sv["proj"], lw["wq"], lw["wkv"], lw["mla_vec"], cos2, sin2,
                                                       dq, dk, dv, f"mla_pre_bwd{l}")
    grads["w_q_b"], grads["w_kv_b"] = _wq_unlayout(dwq), _wkv_unlayout(dwkv)
    small["q_a_norm"], small["kv_a_norm"], small["q_norm"], small["k_norm"] = _mla_unvec(dmv)
    dproj = jnp.concatenate([dgl, dxbc[:, 0:D], dz, du, dxbc[:, D:1536], dckv, dkr, dkrs,
                             _padc(ddt, 128).astype(BF16), jnp.zeros((s, 128), BF16), dql], axis=1)
    grads["w_in"] = tn_matmul(dproj, sv["h1"], f"dw_in{l}")
    return dproj, grads, small


def _bwd_in(dx, dproj, lw, sv, l, tok, small):
    dx0, dvec1 = norm_proj_bwd(sv["x"], sv["vec1"] + tok[0, 0], dproj, lw["win"], dx, None, f"inproj_bwd{l}")
    small["norm1_w"] = dvec1[0]
    small["ada_b"] = jnp.concatenate([dvec1[1], dvec1[2], *small.pop("dmod_b")])
    return dx0, small


def kernel(x, c, positions, ada_w, ada_b, norm1_w, w_in, q_a_norm, w_q_b, kv_a_norm, w_kv_b, q_norm, k_norm, pool_w, pool_scale, ssd_conv_w, ssd_conv_b, ssd_dt_bias, ssd_a_log, ssd_d, ssd_norm_w, w_branch, w_out, norm2_w, ffn_up, ffn_conv_w, ffn_conv_b, ffn_down, loss_target, m_ada_w, m_ada_b, m_norm1_w, m_w_in, m_q_a_norm, m_w_q_b, m_kv_a_norm, m_w_kv_b, m_q_norm, m_k_norm, m_pool_w, m_pool_scale, m_ssd_conv_w, m_ssd_conv_b, m_ssd_dt_bias, m_ssd_a_log, m_ssd_d, m_ssd_norm_w, m_w_branch, m_w_out, m_norm2_w, m_ffn_up, m_ffn_conv_w, m_ffn_conv_b, m_ffn_down, v_ada_w, v_ada_b, v_norm1_w, v_w_in, v_q_a_norm, v_w_q_b, v_kv_a_norm, v_w_kv_b, v_q_norm, v_k_norm, v_pool_w, v_pool_scale, v_ssd_conv_w, v_ssd_conv_b, v_ssd_dt_bias, v_ssd_a_log, v_ssd_d, v_ssd_norm_w, v_w_branch, v_w_out, v_norm2_w, v_ffn_up, v_ffn_conv_w, v_ffn_conv_b, v_ffn_down):
    p = dict(ada_w=ada_w, ada_b=ada_b, norm1_w=norm1_w, w_in=w_in, q_a_norm=q_a_norm, w_q_b=w_q_b, kv_a_norm=kv_a_norm,
             w_kv_b=w_kv_b, q_norm=q_norm, k_norm=k_norm, pool_w=pool_w, pool_scale=pool_scale, ssd_conv_w=ssd_conv_w,
             ssd_conv_b=ssd_conv_b, ssd_dt_bias=ssd_dt_bias, ssd_a_log=ssd_a_log, ssd_d=ssd_d, ssd_norm_w=ssd_norm_w,
             w_branch=w_branch, w_out=w_out, norm2_w=norm2_w, ffn_up=ffn_up, ffn_conv_w=ffn_conv_w, ffn_conv_b=ffn_conv_b,
             ffn_down=ffn_down)
    mom = dict(ada_w=m_ada_w, ada_b=m_ada_b, norm1_w=m_norm1_w, w_in=m_w_in, q_a_norm=m_q_a_norm, w_q_b=m_w_q_b,
               kv_a_norm=m_kv_a_norm, w_kv_b=m_w_kv_b, q_norm=m_q_norm, k_norm=m_k_norm, pool_w=m_pool_w,
               pool_scale=m_pool_scale, ssd_conv_w=m_ssd_conv_w, ssd_conv_b=m_ssd_conv_b, ssd_dt_bias=m_ssd_dt_bias,
               ssd_a_log=m_ssd_a_log, ssd_d=m_ssd_d, ssd_norm_w=m_ssd_norm_w, w_branch=m_w_branch, w_out=m_w_out,
               norm2_w=m_norm2_w, ffn_up=m_ffn_up, ffn_conv_w=m_ffn_conv_w, ffn_conv_b=m_ffn_conv_b, ffn_down=m_ffn_down)
    var = dict(ada_w=v_ada_w, ada_b=v_ada_b, norm1_w=v_norm1_w, w_in=v_w_in, q_a_norm=v_q_a_norm, w_q_b=v_w_q_b,
               kv_a_norm=v_kv_a_norm, w_kv_b=v_w_kv_b, q_norm=v_q_norm, k_norm=v_k_norm, pool_w=v_pool_w,
               pool_scale=v_pool_scale, ssd_conv_w=v_ssd_conv_w, ssd_conv_b=v_ssd_conv_b, ssd_dt_bias=v_ssd_dt_bias,
               ssd_a_log=v_ssd_a_log, ssd_d=v_ssd_d, ssd_norm_w=v_ssd_norm_w, w_branch=v_w_branch, w_out=v_w_out,
               norm2_w=v_norm2_w, ffn_up=v_ffn_up, ffn_conv_w=v_ffn_conv_w, ffn_conv_b=v_ffn_conv_b, ffn_down=v_ffn_down)
    names = list(p)
    me = 4 * lax.axis_index("x") + 2 * lax.axis_index("y") + lax.axis_index("c")
    xs, tgt = x[0], loss_target[0]
    s = xs.shape[0]

    inv_freq = ROPE_THETA ** (-jnp.arange(0, 32, 2, dtype=F32) / 32.0)
    ang = positions[0].astype(F32)[:, None] * inv_freq
    cos, sin = jnp.cos(ang), jnp.sin(ang)
    cos2 = _padc(jnp.concatenate([cos, cos], axis=1), 128)
    sin2 = _padc(jnp.concatenate([-sin, sin], axis=1), 128)

    conv_shards = jnp.concatenate([ssd_conv_w.reshape(-1), ffn_conv_w.reshape(-1)])
    (c_all, conv_all), _ = all_to_all([c, conv_shards], [True, True], "gather_c")
    modp, cact = ada_mod(jnp.pad(c_all.reshape(NDEV, D), ((0, 8), (0, 0))), ada_w)
    (mod_in,), tok = all_to_all([modp[:, 0:NDEV].transpose(1, 0, 2)], [False], "scatter_mod")
    mod = mod_in.transpose(1, 0, 2).reshape(LAYERS, 6 * D) + ada_b

    n1 = LAYERS * 4 * 192
    scw = conv_all[:, :n1].reshape(NDEV, LAYERS, 4, 192).transpose(1, 2, 0, 3).reshape(LAYERS, 4, 1536)
    fcw = conv_all[:, n1:].reshape(NDEV, LAYERS, 3, 704).transpose(1, 2, 0, 3).reshape(LAYERS, 3, 2 * FFN)

    def weights_a(gathered, l):
        full = {n: _gathered_full(g, n) for n, g in zip(GROUP_A[1:], gathered[1:])}
        lw = {n: p[n][l] for n in names}
        lw.update(win=_win_layout(gathered[0]), wq=_wq_layout(full["w_q_b"]), wkv=_wkv_layout(full["w_kv_b"]),
                  ssd_conv_w=scw[l], ffn_conv_w=fcw[l],
                  mla_vec=_mla_vec(lw["q_a_norm"], lw["kv_a_norm"], lw["q_norm"], lw["k_norm"]))
        return lw

    def weights_b(gathered):
        full = {n: _gathered_full(g, n) for n, g in zip(GROUP_B, gathered)}
        wb = full["w_branch"]
        return dict(wba=_wba_layout(wb[0:512]), wbb=wb[512:1024], wbc=wb[1024:2048], wout=full["w_out"],
                    wup=full["ffn_up"], wdn=full["ffn_down"])

    shards = lambda group, l: [(p[n][l].T if n in TRANSPOSED else p[n][l]).astype(BF16) for n in group]
    lws, saved = [None] * LAYERS, [None] * LAYERS
    st, tok = gather_start(_behind(shards(GROUP_A, 0), tok), "gather_a0")
    got, tok = gather_finish(st, tok, "gather_a0")
    h = xs
    for l in range(LAYERS):
        st, tok = gather_start(_behind(shards(GROUP_B, l), tok), f"gather_b{l}")
        lws[l] = weights_a(got, l)
        saved[l] = _fwd_a(h, lws[l], mod[l], cos2, sin2, l, tok)
        got, tok = gather_finish(st, saved[l]["yt"], f"gather_b{l}")
        lws[l].update(weights_b(got))
        if l + 1 < LAYERS:
            st, tok = gather_start(_behind(shards(GROUP_A, l + 1), tok), f"gather_a{l + 1}")
        h = _fwd_b(saved[l], lws[l], l, tok)
        if l + 1 < LAYERS:
            got, tok = gather_finish(st, h, f"gather_a{l + 1}")
    dx, lpart = loss_head(h, tgt)
    loss = lax.psum(lpart[0, 0], ("x", "y", "c"))
    tok = tok + loss * 0.0

    small, parts, packs = [None] * LAYERS, {n: [None] * LAYERS for n in BIG}, [None] * LAYERS
    st = None

    def scatter(grads, group, l, tok, extra=None):
        arrs, flags = [_to_shards(grads[n], n) for n in group], [False] * len(group)
        if extra is not None:
            arrs, flags = arrs + [extra], flags + [True]
        return exchange_start(_behind(arrs, tok), flags, f"scatter_{group[0]}{l}_start")

    def landed(state, group, l, after):
        got, tok, _ = exchange_wait(state, after, f"scatter_{group[0]}{l}_wait")
        for n, g in zip(group, got):
            parts[n][l] = g
        return got, tok

    for l in reversed(range(LAYERS)):
        dup, g_ffn, small[l] = _bwd_ffn(dx, lws[l], saved[l], l, tok)
        if st is not None:
            _, tok = landed(st, GROUP_A, l + 1, dup)
        st, tok = scatter(g_ffn, SCATTER_FFN, l, tok)
        dx, cot, g_merge, small[l] = _bwd_merge(dx, dup, lws[l], saved[l], l, tok, small[l])
        _, tok = landed(st, SCATTER_FFN, l, dx)
        st, tok = scatter(g_merge, SCATTER_MERGE, l, tok, _pack(small[l + 1]) if l + 1 < LAYERS else None)
        dproj, g_in, small[l] = _bwd_a(dx, cot, lws[l], saved[l], cos2, sin2, l, tok, small[l])
        got, tok = landed(st, SCATTER_MERGE, l, dproj)
        if l + 1 < LAYERS:
            packs[l + 1] = got[-1]
        st, tok = scatter(g_in, GROUP_A, l, tok)
        dx, small[l] = _bwd_in(dx, dproj, lws[l], saved[l], l, tok, small[l])

    dmod = jnp.stack([small[q]["ada_b"] for q in range(LAYERS)])
    st_small, tok = exchange_start(_behind([_pack(small[0]), dmod.reshape(LAYERS, NDEV, 768).transpose(1, 0, 2)], tok),
                                   [True, False], "scatter_s0_start")
    out = {}

    def big_adamw(group, tok):
        res = None
        for n in group:
            t = (lambda a: a.transpose(0, 2, 1)) if n in TRANSPOSED else (lambda a: a)
            res = adamw(parts[n], t(p[n]), t(mom[n]), t(var[n]), f"adamw_{n}", tok)
            out[n] = [t(a) for a in res]
        return res[0]

    g_last = big_adamw(GROUP_B, tok)
    _, tok = landed(st, GROUP_A, 0, g_last)
    (packs[0], dmod_in), _, _ = exchange_wait(st_small, tok, "scatter_s0_wait")
    big_adamw(GROUP_A, None)

    dmod16 = jnp.pad(dmod_in, ((0, 8), (0, 0), (0, 0)))
    g_ada = [tn_matmul(cact, dmod16[:, l], f"dw_ada{l}", out_dtype=F32)[None] for l in range(LAYERS)]
    out["ada_w"] = adamw(g_ada, ada_w, m_ada_w, v_ada_w, "adamw_ada_w")

    for n, pt in _unpack_parts(packs).items():
        if n in SHARDED_SMALL:
            w = SHARDED_SMALL[n]
            pt = lax.dynamic_slice_in_dim(pt, me * w, w, axis=2)
        r, c = pt.shape[1:]
        res = adamw([pt], p[n].reshape(1, r, c), mom[n].reshape(1, r, c), var[n].reshape(1, r, c), f"adamw_{n}")
        out[n] = [a.reshape(p[n].shape) for a in res]

    outs = [loss, dx[None]]
    for q in range(4):
        outs += [out[n][q] for n in names]
    return tuple(outs)
```

```python
import functools
import math

import jax
import jax.numpy as jnp
from jax import lax
from jax.experimental import pallas as pl
from jax.experimental.pallas import tpu as pltpu

F32, BF16 = jnp.float32, jnp.bfloat16
EPS = 1e-6
D = 1024
NDEV = 8
LAYERS = 2
HEADS = 8
FFN = 2816
FFN_TILE = 1408
FFN_NT = FFN // FFN_TILE
ATT_SCALE = 96 ** -0.5
ROPE_THETA = 10000.0
LR, B1, B2, ADAM_EPS, WD, STEP = 0.001, 0.9, 0.999, 1e-08, 0.01, 10

O_G, O_XS, O_Z, O_PU, O_BC, O_CKV, O_KR, O_KRS, O_DT, O_QL = 0, 3072, 4096, 5120, 5632, 6144, 6400, 6528, 6656, 6912
NPROJ = 7296
CONST = dict(pipeline_mode=pl.Buffered(1))


def _pick(n, cap, mult=128):
    if n <= cap:
        return n
    best = None
    for t in range(mult, cap + 1, mult):
        if n % t == 0:
            best = t
    assert best is not None, (n, cap, mult)
    return best


def _sig(x):
    return 1.0 / (1.0 + jnp.exp(-x))


def _rms(x, w, n):
    return x * lax.rsqrt(jnp.sum(x * x, axis=-1, keepdims=True) / n + EPS) * w


def _raw(a, b, dims):
    return lax.dot_general(a.astype(BF16), b.astype(BF16), dims, preferred_element_type=F32)


_NN = (((1,), (0,)), ((), ()))
_NT = (((1,), (1,)), ((), ()))
_TN = (((0,), (0,)), ((), ()))
_BNN = (((2,), (1,)), ((0,), (0,)))
_BNT = (((2,), (2,)), ((0,), (0,)))
_BTN = (((1,), (1,)), ((0,), (0,)))


@jax.custom_vjp
def mm_nn(a, b):
    return _raw(a, b, _NN)


mm_nn.defvjp(lambda a, b: (_raw(a, b, _NN), (a, b)),
             lambda r, g: (_raw(g, r[1], _NT), _raw(r[0], g, _TN)))


@jax.custom_vjp
def mm_nc(a, b):
    return _raw(a, b, _NN)


mm_nc.defvjp(lambda a, b: (_raw(a, b, _NN), b),
             lambda b, g: (_raw(g, b, _NT), jnp.zeros_like(b)))


@jax.custom_vjp
def mm_nt(a, b):
    return _raw(a, b, _NT)


mm_nt.defvjp(lambda a, b: (_raw(a, b, _NT), (a, b)),
             lambda r, g: (_raw(g, r[1], _NN), _raw(g, r[0], _TN)))


@jax.custom_vjp
def bmm_nn(a, b):
    return _raw(a, b, _BNN)


bmm_nn.defvjp(lambda a, b: (_raw(a, b, _BNN), (a, b)),
              lambda r, g: (_raw(g, r[1], _BNT), _raw(r[0], g, _BTN)))


@jax.custom_vjp
def bmm_nt(a, b):
    return _raw(a, b, _BNT)


bmm_nt.defvjp(lambda a, b: (_raw(a, b, _BNT), (a, b)),
              lambda r, g: (_raw(g, r[1], _BNN), _raw(g, r[0], _BTN)))


@jax.custom_vjp
def softplus(x):
    t = jnp.exp(-jnp.abs(x))
    u = 1.0 + t
    one = u == 1.0
    l1p = jnp.where(one, t, jnp.log(u) * (t / jnp.where(one, 1.0, u - 1.0)))
    return jnp.maximum(x, 0.0) + l1p


softplus.defvjp(lambda x: (softplus(x), x), lambda x, g: (g * _sig(x),))


def _params(*sem):
    return pltpu.CompilerParams(dimension_semantics=sem, vmem_limit_bytes=56 * 1024 * 1024)


def all_to_all(arrs, bcast, name):
    n = len(arrs)
    out_shapes = [jax.ShapeDtypeStruct(((NDEV,) + a.shape) if b else a.shape, a.dtype) for a, b in zip(arrs, bcast)]

    def body(*refs):
        ins, outs, token = refs[:n], refs[n:2 * n], refs[2 * n]
        send_sems, recv_sems, local_sems = refs[2 * n + 1:]
        me, remote = _exchange_copies(ins, outs, bcast, send_sems, recv_sems)
        local = [pltpu.make_async_copy(ins[j] if bcast[j] else ins[j].at[me], outs[j].at[me], local_sems.at[j])
                 for j in range(n)]
        for cp in local + remote:
            cp.start()
        for cp in remote + local:
            cp.wait()
        token[...] = jnp.zeros_like(token)

    any_spec = pl.BlockSpec(memory_space=pl.ANY)
    res = pl.pallas_call(
        body, name=name, out_shape=out_shapes + [jax.ShapeDtypeStruct((8, 128), F32)], in_specs=[any_spec] * n,
        out_specs=[any_spec] * n + [pl.BlockSpec(memory_space=pltpu.VMEM)],
        scratch_shapes=[pltpu.SemaphoreType.DMA((7 * n,)), pltpu.SemaphoreType.DMA((7 * n,)),
                        pltpu.SemaphoreType.DMA((n,))],
        compiler_params=pltpu.CompilerParams(has_side_effects=True),
    )(*arrs)
    return res[:n], res[n]


def _peers():
    x, y, c = lax.axis_index("x"), lax.axis_index("y"), lax.axis_index("c")
    out = []
    for k in range(1, NDEV):
        px, py, pc = x ^ ((k >> 2) & 1), y ^ ((k >> 1) & 1), c ^ (k & 1)
        out.append(((px, py, pc), 4 * px + 2 * py + pc))
    return 4 * x + 2 * y + c, out


COPIES = {"all": 7, "chips": 3, "pass": 4}


def _exchange_copies(ins, lands, bcast, send_sems, recv_sems, mode="all"):
    x, y, c = lax.axis_index("x"), lax.axis_index("y"), lax.axis_index("c")
    me = 4 * x + 2 * y + c
    n, copies = len(ins), []

    def add(q, j, src, dst, dev):
        copies.append(pltpu.make_async_remote_copy(
            src_ref=src, dst_ref=dst, send_sem=send_sems.at[q * n + j], recv_sem=recv_sems.at[q * n + j],
            device_id=dev, device_id_type=pl.DeviceIdType.MESH))

    if mode == "pass":
        for q in range(4):
            slot = 4 * (x ^ (q >> 1)) + 2 * (y ^ (q & 1)) + c
            for j in range(n):
                add(q, j, ins[j] if q == 0 else lands[j].at[slot], lands[j].at[slot], (x, y, 1 - c))
        return me, copies
    for q, k in enumerate(range(1, NDEV) if mode == "all" else (2, 4, 6)):
        px, py, pc = x ^ ((k >> 2) & 1), y ^ ((k >> 1) & 1), c ^ (k & 1)
        for j in range(n):
            add(q, j, ins[j] if bcast[j] else ins[j].at[4 * px + 2 * py + pc], lands[j].at[me], (px, py, pc))
    return me, copies


_HBM = pl.BlockSpec(memory_space=pltpu.HBM)
_SEM = pl.BlockSpec(memory_space=pltpu.SEMAPHORE)
_EFFECT = pltpu.SideEffectType.DATAFLOW_SIDE_EFFECTING


def exchange_start(arrs, bcast, name, mode="all", lands=None):
    n, ncp = len(arrs), COPIES[mode] * len(arrs)
    land_shapes = [((NDEV,) + a.shape) if b else a.shape for a, b in zip(arrs, bcast)]
    if lands is None:
        lands = [lax.empty(s_, a.dtype) for s_, a in zip(land_shapes, arrs)]

    def body(*refs):
        in_refs, land_refs = refs[:n], refs[n:2 * n]
        send_sems, recv_sems = refs[2 * n], refs[2 * n + 1]
        token = refs[-1]
        _, copies = _exchange_copies(in_refs, land_refs, bcast, send_sems, recv_sems, mode)
        for cp in copies:
            cp.start()
        token[...] = jnp.zeros_like(token)

    hbm = lambda shp, a: pltpu.HBM(shp, a.dtype)
    res = pl.pallas_call(
        body, name=name,
        out_shape=[pltpu.SemaphoreType.DMA((ncp,)), pltpu.SemaphoreType.DMA((ncp,))]
                  + [hbm(a.shape, a) for a in arrs] + [hbm(s_, a) for s_, a in zip(land_shapes, arrs)]
                  + [jax.ShapeDtypeStruct((8, 128), F32)],
        in_specs=[_HBM] * (2 * n), out_specs=[_SEM, _SEM] + [_HBM] * (2 * n) + [pl.BlockSpec(memory_space=pltpu.VMEM)],
        input_output_aliases={i: 2 + i for i in range(2 * n)},
        compiler_params=pltpu.CompilerParams(has_side_effects=_EFFECT),
    )(*[pltpu.with_memory_space_constraint(a, pltpu.HBM) for a in arrs],
      *[pltpu.with_memory_space_constraint(a, pltpu.HBM) for a in lands])
    return (res[0], res[1], res[2:2 + n], res[2 + n:2 + 2 * n], tuple(bcast), mode), res[-1]


def exchange_wait(state, after, name):
    send_sems, recv_sems, ins, lands, bcast, mode = state
    n = len(ins)

    def body(*refs):
        in_refs, land_refs = refs[:n], refs[n:2 * n]
        s_sems, r_sems = refs[2 * n], refs[2 * n + 1]
        token = refs[-1]
        _, copies = _exchange_copies(in_refs, land_refs, bcast, s_sems, r_sems, mode)
        for cp in copies:
            cp.wait_send()
            cp.wait_recv()
        token[...] = jnp.zeros_like(token)

    res = pl.pallas_call(
        body, name=name,
        out_shape=[pltpu.HBM(a.shape, a.dtype) for a in ins] + [pltpu.HBM(a.shape, a.dtype) for a in lands]
                  + [jax.ShapeDtypeStruct((8, 128), F32)],
        in_specs=[_HBM] * (2 * n) + [_SEM, _SEM, pl.BlockSpec(memory_space=pl.ANY)],
        out_specs=[_HBM] * (2 * n) + [pl.BlockSpec(memory_space=pltpu.VMEM)],
        input_output_aliases={i: i for i in range(2 * n)},
        compiler_params=pltpu.CompilerParams(has_side_effects=_EFFECT),
    )(*ins, *lands, send_sems, recv_sems, after)
    if mode == "chips":
        return list(res[n:2 * n]), res[-1], list(res[:n])
    me = 4 * lax.axis_index("x") + 2 * lax.axis_index("y") + lax.axis_index("c")
    got = []
    for j in range(n):
        own = res[j][None] if bcast[j] else lax.dynamic_index_in_dim(res[j], me, 0, keepdims=True)
        got.append(lax.dynamic_update_slice_in_dim(res[n + j], own, me, axis=0))
    return got, res[-1], list(res[:n])


def gather_start(shards, name):
    return exchange_start(shards, [True] * len(shards), name + "_chips_start", mode="chips")


def gather_finish(state, after, name):
    lands, _, sent = exchange_wait(state, after, name + "_chips_wait")
    state, tok = exchange_start(sent, [True] * len(sent), name + "_pass_start", mode="pass", lands=lands)
    got, tok, _ = exchange_wait(state, tok, name + "_pass_wait")
    return got, tok


def norm_proj_fwd(x, vec, w, name):
    s, n = x.shape[0], w.shape[0]
    tr, tn = _pick(s, 512), _pick(n, 2560)
    ni, jdt, odt = s // tr, O_DT // tn, O_DT % tn

    def body(x_ref, v_ref, w_ref, o_ref, h_ref, dt_ref, h_scr):
        j, i = pl.program_id(0), pl.program_id(1)
        rows = pl.ds(pl.multiple_of(i * tr, tr), tr)

        @pl.when(j == 0)
        def _():
            h = _rms(x_ref[...], v_ref[0:1, :], D) * (1.0 + v_ref[2:3, :]) + v_ref[1:2, :]
            h_scr[rows, :] = h.astype(BF16)
            h_ref[...] = h.astype(BF16)
        res = _raw(h_scr[rows, :], w_ref[...], _NT)
        o_ref[...] = res

        @pl.when(j == jdt)
        def _():
            dt_ref[...] = res[:, odt:odt + 128]

    first = lambda j, i: (jnp.where(j == 0, i, ni - 1), 0)
    dtix = lambda j, i: (jnp.where(j < jdt, 0, jnp.where(j == jdt, i, ni - 1)), 0)
    return pl.pallas_call(
        body, name=name, grid=(n // tn, ni),
        in_specs=[pl.BlockSpec((tr, D), first), pl.BlockSpec((8, D), lambda j, i: (0, 0)),
                  pl.BlockSpec((tn, D), lambda j, i: (j, 0))],
        out_specs=[pl.BlockSpec((tr, tn), lambda j, i: (i, j)), pl.BlockSpec((tr, D), first),
                   pl.BlockSpec((tr, 128), dtix)],
        out_shape=[jax.ShapeDtypeStruct((s, n), F32), jax.ShapeDtypeStruct((s, D), BF16),
                   jax.ShapeDtypeStruct((s, 128), F32)],
        scratch_shapes=[pltpu.VMEM((s, D), BF16)],
        compiler_params=_params("arbitrary", "arbitrary"),
    )(x, vec, w)


def _col_tiles(arr, cap):
    if arr.ndim == 2:
        n = arr.shape[1]
        t = _pick(n, cap)
        return n, t, lambda rows, ix: pl.BlockSpec((rows, t), lambda *g: ix(*g))
    width = arr.shape[2]
    t = _pick(width, cap)
    per = width // t

    def spec(rows, ix):
        def index(*g):
            r, j = ix(*g)
            return (j // per, r, j % per)
        return pl.BlockSpec((None, rows, t), index)
    return arr.shape[0] * width, t, spec


def norm_proj_bwd(x, vec, dp, w, dx_in, aux, name):
    s = x.shape[0]
    tr = _pick(s, 512)
    n, tk, dp_spec = _col_tiles(dp, 2560)
    nk, has_aux = n // tk, aux is not None

    def body(*refs):
        if has_aux:
            x_ref, v_ref, dp_ref, w_ref, dxin_ref, aux_ref, dx_ref, dv_ref, acc = refs
        else:
            x_ref, v_ref, dp_ref, w_ref, dxin_ref, dx_ref, dv_ref, acc = refs
        k, i = pl.program_id(0), pl.program_id(1)
        rows = pl.ds(pl.multiple_of(i * tr, tr), tr)
        part = _raw(dp_ref[...], w_ref[...], _NN)

        @pl.when(k == 0)
        def _():
            acc[rows, :] = part

        @pl.when(k > 0)
        def _():
            acc[rows, :] += part

        @pl.when(k == nk - 1)
        def _():
            f = lambda xx, nw, sh, sc: _rms(xx, nw, D) * (1.0 + sc) + sh
            _, vjp = jax.vjp(f, x_ref[...], v_ref[0:1, :], v_ref[1:2, :], v_ref[2:3, :])
            dx, dnw, dsh, dsc = vjp(acc[rows, :])
            dx_ref[...] = dxin_ref[...] + dx

            @pl.when(i == 0)
            def _():
                dv_ref[...] = jnp.zeros_like(dv_ref)

            dv_ref[0:1, :] += dnw
            dv_ref[1:2, :] += dsh
            dv_ref[2:3, :] += dsc
            if has_aux:
                dv_ref[3:4, :] += jnp.sum(dxin_ref[...] * aux_ref[...], axis=0, keepdims=True)

    row = pl.BlockSpec((tr, D), lambda k, i: (jnp.where(k == nk - 1, i, 0), 0))
    in_specs = [row, pl.BlockSpec((8, D), lambda k, i: (0, 0)), dp_spec(tr, lambda k, i: (i, k)),
                pl.BlockSpec((tk, D), lambda k, i: (k, 0)), row] + ([row] if has_aux else [])
    args = [x, vec, dp, w, dx_in] + ([aux] if has_aux else [])
    return pl.pallas_call(
        body, name=name, grid=(nk, s // tr), in_specs=in_specs,
        out_specs=[row, pl.BlockSpec((8, D), lambda k, i: (0, 0))],
        out_shape=[jax.ShapeDtypeStruct((s, D), F32), jax.ShapeDtypeStruct((8, D), F32)],
        scratch_shapes=[pltpu.VMEM((s, D), F32)],
        compiler_params=_params("arbitrary", "arbitrary"),
    )(*args)


def tn_matmul(a, b, name, scale=None, out_dtype=None):
    out_dtype = BF16 if out_dtype is None else out_dtype
    s = b.shape[-2]
    ts = _pick(s, 512, 16)
    m, tm, a_spec = _col_tiles(a, 2560 if b.shape[-1] <= D else 1408)
    n, tn, b_spec = _col_tiles(b, 2560)
    ns, has_scale = s // ts, scale is not None

    def body(*refs):
        if has_scale:
            a_ref, b_ref, sc_ref, o_ref, acc = refs
        else:
            a_ref, b_ref, o_ref, acc = refs
        k = pl.program_id(2)

        @pl.when(k == 0)
        def _():
            acc[...] = jnp.zeros_like(acc)

        acc[...] += _raw(a_ref[...], b_ref[...], _TN)

        @pl.when(k == ns - 1)
        def _():
            o_ref[...] = (acc[...] * sc_ref[...] if has_scale else acc[...]).astype(out_dtype)

    in_specs = [a_spec(ts, lambda i, j, k: (k, i)), b_spec(ts, lambda i, j, k: (k, j))]
    if has_scale:
        in_specs.append(pl.BlockSpec((1, tn), lambda i, j, k: (0, j)))
    return pl.pallas_call(
        body, name=name, grid=(m // tm, n // tn, ns), in_specs=in_specs,
        out_specs=pl.BlockSpec((tm, tn), lambda i, j, k: (i, j)),
        out_shape=jax.ShapeDtypeStruct((m, n), out_dtype),
        scratch_shapes=[pltpu.VMEM((tm, tn), F32)],
        compiler_params=_params("arbitrary", "arbitrary", "arbitrary"),
    )(*([a, b] + ([scale] if has_scale else [])))


def ada_mod(c16, w):
    ncol = w.shape[2]

    def body(c_ref, w_ref, o_ref, a_ref):
        cc = c_ref[...]
        act = cc * _sig(cc)
        a_ref[...] = act
        o_ref[...] = _raw(act, w_ref[...], _NN)

    return pl.pallas_call(
        body, name="ada_mod", grid=(LAYERS,),
        in_specs=[pl.BlockSpec((16, D), lambda l: (0, 0)), pl.BlockSpec((None, D, ncol), lambda l: (l, 0, 0))],
        out_specs=[pl.BlockSpec((None, 16, ncol), lambda l: (l, 0, 0)), pl.BlockSpec((16, D), lambda l: (0, 0))],
        out_shape=[jax.ShapeDtypeStruct((LAYERS, 16, ncol), F32), jax.ShapeDtypeStruct((16, D), F32)],
        compiler_params=_params("arbitrary"),
    )(c16, w)


def _mla_shared(q_lat, c_kv, kr, krs, qa_w, kva_w, kr_w, krs_w, cos2, sin2):
    qn = _rms(q_lat, qa_w, 384.0)
    kvn = _rms(c_kv, kva_w, 256.0)
    rk = lax.rsqrt(jnp.sum(kr * kr, axis=-1, keepdims=True) / 32.0 + EPS)
    krope = rk * (kr * kr_w * cos2 + krs * krs_w * sin2)
    return qn, kvn, krope


def _mla_head(qn, kvn, wqn, wqr, wqrs, wkn, wv, qn_w, qr_w, qrs_w, kn_w, cos2, sin2):
    qnope = _rms(mm_nn(qn, wqn), qn_w, 64.0)
    qr, qrs = mm_nn(qn, wqr), mm_nn(qn, wqrs)
    rq = lax.rsqrt(jnp.sum(qr * qr, axis=-1, keepdims=True) / 32.0 + EPS)
    qrope = rq * (qr * qr_w * cos2 + qrs * qrs_w * sin2)
    knope = _rms(mm_nn(kvn, wkn), kn_w, 64.0)
    return qnope, qrope, knope, mm_nn(kvn, wv)


def _mla_vec_pieces(v_ref):
    return ((v_ref[0:1, 0:384], v_ref[1:2, 0:256], v_ref[3:4, 128:256], v_ref[3:4, 256:384]),
            (v_ref[2:3, 0:128], v_ref[2:3, 128:256], v_ref[2:3, 256:384], v_ref[3:4, 0:128]))


def _mla_in_specs(tr):
    return [pl.BlockSpec((tr, 384), lambda i: (i, O_QL // 384)), pl.BlockSpec((tr, 256), lambda i: (i, O_CKV // 256)),
            pl.BlockSpec((tr, 128), lambda i: (i, O_KR // 128)), pl.BlockSpec((tr, 128), lambda i: (i, O_KRS // 128)),
            pl.BlockSpec((HEADS, 384, 384), lambda i: (0, 0, 0), **CONST),
            pl.BlockSpec((HEADS, 256, 256), lambda i: (0, 0, 0), **CONST),
            pl.BlockSpec((8, 512), lambda i: (0, 0)),
            pl.BlockSpec((tr, 128), lambda i: (i, 0)), pl.BlockSpec((tr, 128), lambda i: (i, 0))]


def mla_pre_fwd(proj, wq, wkv, vec, cos2, sin2, name):
    s = proj.shape[0]
    tr = _pick(s, 256)

    def body(ql_ref, ckv_ref, kr_ref, krs_ref, wq_ref, wkv_ref, v_ref, cos_ref, sin_ref, q_out, k_out, v_out):
        vshared, vhead = _mla_vec_pieces(v_ref)
        cos2_, sin2_ = cos_ref[...], sin_ref[...]
        qlat_n, kv_n, krope = _mla_shared(ql_ref[...], ckv_ref[...], kr_ref[...], krs_ref[...], *vshared, cos2_, sin2_)
        qlat_n, kv_n, krope = qlat_n.astype(BF16), kv_n.astype(BF16), krope.astype(BF16)
        for h in range(HEADS):
            ws = (wq_ref[h, :, 0:128], wq_ref[h, :, 128:256], wq_ref[h, :, 256:384],
                  wkv_ref[h, :, 0:128], wkv_ref[h, :, 128:256])
            qn, qr, kn, v = _mla_head(qlat_n, kv_n, *ws, *vhead, cos2_, sin2_)
            q_out[h, :, 0:128] = qn.astype(BF16)
            q_out[h, :, 128:256] = qr.astype(BF16)
            k_out[h, :, 0:128] = kn.astype(BF16)
            k_out[h, :, 128:256] = krope
            v_out[h] = v.astype(BF16)

    return pl.pallas_call(
        body, name=name, grid=(s // tr,), in_specs=_mla_in_specs(tr),
        out_specs=[pl.BlockSpec((HEADS, tr, 256), lambda i: (0, i, 0)), pl.BlockSpec((HEADS, tr, 256), lambda i: (0, i, 0)),
                   pl.BlockSpec((HEADS, tr, 128), lambda i: (0, i, 0))],
        out_shape=[jax.ShapeDtypeStruct((HEADS, s, 256), BF16), jax.ShapeDtypeStruct((HEADS, s, 256), BF16),
                   jax.ShapeDtypeStruct((HEADS, s, 128), BF16)],
        compiler_params=_params("arbitrary"),
    )(proj, proj, proj, proj, wq, wkv, vec, cos2, sin2)


def mla_pre_bwd(proj, wq, wkv, vec, cos2, sin2, dq, dk, dv, name):
    s = proj.shape[0]
    tr = _pick(s, 256)

    def body(ql_ref, ckv_ref, kr_ref, krs_ref, wq_ref, wkv_ref, v_ref, cos_ref, sin_ref, dq_ref, dk_ref, dv_ref,
             dql_out, dckv_out, dkr_out, dkrs_out, dwq_out, dwkv_out, dvec_out):
        @pl.when(pl.program_id(0) == 0)
        def _():
            dwq_out[...] = jnp.zeros_like(dwq_out)
            dwkv_out[...] = jnp.zeros_like(dwkv_out)
            dvec_out[...] = jnp.zeros_like(dvec_out)

        vshared, vhead = _mla_vec_pieces(v_ref)
        cos2_, sin2_ = cos_ref[...], sin_ref[...]
        fs = lambda *a: _mla_shared(*a, cos2_, sin2_)
        (qlat_n, kv_n, _), vjp_shared = jax.vjp(fs, ql_ref[...], ckv_ref[...], kr_ref[...], krs_ref[...], *vshared)

        def head(h, carry):
            wq_h, wkv_h = wq_ref[h].astype(F32), wkv_ref[h].astype(F32)
            ws = (wq_h[:, 0:128], wq_h[:, 128:256], wq_h[:, 256:384], wkv_h[:, 0:128], wkv_h[:, 128:256])
            f = lambda *a: _mla_head(*a, cos2_, sin2_)
            _, vjp = jax.vjp(f, qlat_n, kv_n, *ws, *vhead)
            dq_h, dk_h = dq_ref[h], dk_ref[h]
            g = vjp((dq_h[:, 0:128], dq_h[:, 128:256], dk_h[:, 0:128], dv_ref[h]))
            dwq_out[h, :, 0:128] += g[2]
            dwq_out[h, :, 128:256] += g[3]
            dwq_out[h, :, 256:384] += g[4]
            dwkv_out[h, :, 0:128] += g[5]
            dwkv_out[h, :, 128:256] += g[6]
            dvec_out[2:3, 0:128] += g[7]
            dvec_out[2:3, 128:256] += g[8]
            dvec_out[2:3, 256:384] += g[9]
            dvec_out[3:4, 0:128] += g[10]
            return carry[0] + g[0], carry[1] + g[1], carry[2] + dk_h[:, 128:256]

        zero = lambda w: jnp.zeros((tr, w), F32)
        dqn, dkvn, dkrope = lax.fori_loop(0, HEADS, head, (zero(384), zero(256), zero(128)))
        g = vjp_shared((dqn, dkvn, dkrope))
        dql_out[...] = g[0].astype(BF16)
        dckv_out[...] = g[1].astype(BF16)
        dkr_out[...] = g[2].astype(BF16)
        dkrs_out[...] = g[3].astype(BF16)
        dvec_out[0:1, 0:384] += g[4]
        dvec_out[1:2, 0:256] += g[5]
        dvec_out[3:4, 128:256] += g[6]
        dvec_out[3:4, 256:384] += g[7]

    hb = lambda w: pl.BlockSpec((HEADS, tr, w), lambda i: (0, i, 0))
    return pl.pallas_call(
        body, name=name, grid=(s // tr,), in_specs=_mla_in_specs(tr) + [hb(256), hb(256), hb(128)],
        out_specs=[pl.BlockSpec((tr, 384), lambda i: (i, 0)), pl.BlockSpec((tr, 256), lambda i: (i, 0)),
                   pl.BlockSpec((tr, 128), lambda i: (i, 0)), pl.BlockSpec((tr, 128), lambda i: (i, 0)),
                   pl.BlockSpec((HEADS, 384, 384), lambda i: (0, 0, 0)), pl.BlockSpec((HEADS, 256, 256), lambda i: (0, 0, 0)),
                   pl.BlockSpec((8, 512), lambda i: (0, 0))],
        out_shape=[jax.ShapeDtypeStruct((s, 384), BF16), jax.ShapeDtypeStruct((s, 256), BF16),
                   jax.ShapeDtypeStruct((s, 128), BF16), jax.ShapeDtypeStruct((s, 128), BF16),
                   jax.ShapeDtypeStruct((HEADS, 384, 384), F32), jax.ShapeDtypeStruct((HEADS, 256, 256), F32),
                   jax.ShapeDtypeStruct((8, 512), F32)],
        compiler_params=_params("arbitrary"),
    )(proj, proj, proj, proj, wq, wkv, vec, cos2, sin2, dq, dk, dv)


def _att_probs(q, kk, i, tq):
    sc = _raw(q, kk, _NT) * ATT_SCALE
    rows = lax.broadcasted_iota(jnp.int32, sc.shape, 0) + i * tq
    cols = lax.broadcasted_iota(jnp.int32, sc.shape, 1)
    sc = jnp.where(cols <= rows, sc, -jnp.inf)
    e = jnp.exp(sc - jnp.max(sc, axis=-1, keepdims=True))
    return e / jnp.sum(e, axis=-1, keepdims=True)


def mla_attn_fwd(q, k, v, name):
    s = q.shape[1]
    tq = _pick(s, 256)

    def body(q_ref, k_ref, v_ref, o_ref):
        for i in range(s // tq):
            n = (i + 1) * tq
            p = _att_probs(q_ref[i * tq:n, :], k_ref[0:n, :], i, tq)
            o_ref[i * tq:n, :] = _raw(p, v_ref[0:n, :], _NN)

    hs = lambda w: pl.BlockSpec((None, s, w), lambda h: (h, 0, 0))
    return pl.pallas_call(
        body, name=name, grid=(HEADS,), in_specs=[hs(256), hs(256), hs(128)],
        out_specs=pl.BlockSpec((s, 128), lambda h: (0, h)),
        out_shape=jax.ShapeDtypeStruct((s, HEADS * 128), F32),
        compiler_params=_params("arbitrary"),
    )(q, k, v)


def mla_attn_bwd(q, k, v, do, name):
    s = q.shape[1]
    tq = _pick(s, 256)

    def body(q_ref, k_ref, v_ref, do_ref, dq_ref, dk_ref, dv_ref):
        dk_ref[...] = jnp.zeros_like(dk_ref)
        dv_ref[...] = jnp.zeros_like(dv_ref)
        for i in range(s // tq):
            n = (i + 1) * tq
            qq, kk, vv = q_ref[i * tq:n, :], k_ref[0:n, :], v_ref[0:n, :]
            p = _att_probs(qq, kk, i, tq)
            o = _raw(p, vv, _NN)
            dout = do_ref[i * tq:n, :]
            delta = jnp.sum(dout * o, axis=-1, keepdims=True)
            dp = _raw(dout, vv, _NT)
            ds = p * (dp - delta) * ATT_SCALE
            dq_ref[i * tq:n, :] = _raw(ds, kk, _NN)
            dk_ref[0:n, :] += _raw(ds, qq, _TN)
            dv_ref[0:n, :] += _raw(p, dout, _TN)

    hs = lambda w: pl.BlockSpec((None, s, w), lambda h: (h, 0, 0))
    return pl.pallas_call(
        body, name=name, grid=(HEADS,),
        in_specs=[hs(256), hs(256), hs(128), pl.BlockSpec((s, 128), lambda h: (0, h))],
        out_specs=[hs(256), hs(256), hs(128)],
        out_shape=[jax.ShapeDtypeStruct((HEADS, s, 256), F32), jax.ShapeDtypeStruct((HEADS, s, 256), F32),
                   jax.ShapeDtypeStruct((HEADS, s, 128), F32)],
        compiler_params=_params("arbitrary"),
    )(q, k, v, do)


def _pool_windows(u, pad, s, g):
    pad[0:16, :] = jnp.zeros((16, 128), F32)
    cur, sel = u, None
    for j, k in enumerate((1, 2, 4, 8)):
        pad[16:16 + s, :] = cur
        cur = cur + pad[16 - k:16 - k + s, :]
        sel = cur if sel is None else jnp.where(g == j, cur, sel)
    return sel


def _pool_count(s, g):
    t = lax.broadcasted_iota(jnp.int32, (s, 1), 0)
    return jnp.minimum(t + 1, 2 << g).astype(F32)


def pool_fwd(proj, pw, ps, name):
    s = proj.shape[0]

    def body(u_ref, w_ref, s_ref, o_ref, pad):
        g = pl.program_id(0)
        u = u_ref[...]
        pooled = _pool_windows(u, pad, s, g) / _pool_count(s, g) - u
        o_ref[...] = _raw(pooled, w_ref[...], _NN) * s_ref[...]

    return pl.pallas_call(
        body, name=name, grid=(4,),
        in_specs=[pl.BlockSpec((s, 128), lambda g: (0, O_PU // 128 + g)), pl.BlockSpec((None, 128, 128), lambda g: (g, 0, 0)),
                  pl.BlockSpec((1, 128), lambda g: (0, g))],
        out_specs=pl.BlockSpec((s, 128), lambda g: (0, g)),
        out_shape=jax.ShapeDtypeStruct((s, 512), F32),
        scratch_shapes=[pltpu.VMEM((s + 16, 128), F32)],
        compiler_params=_params("arbitrary"),
    )(proj, pw, ps)


def pool_bwd(proj, pw, ps, do, name):
    s = proj.shape[0]

    def body(u_ref, w_ref, s_ref, do_ref, du_ref, dw_ref, ds_ref, pad):
        g = pl.program_id(0)
        u, w, dout = u_ref[...], w_ref[...], do_ref[...]
        cnt = _pool_count(s, g)
        pooled = _pool_windows(u, pad, s, g) / cnt - u
        mixed = _raw(pooled, w, _NN)
        ds_ref[...] = jnp.sum(dout * mixed, axis=0, keepdims=True)
        dmixed = dout * s_ref[...]
        dw_ref[...] = _raw(pooled, dmixed, _TN)
        dpooled = _raw(dmixed, w, _NT)
        dsel = dpooled / cnt
        pad[s:s + 16, :] = jnp.zeros((16, 128), F32)
        cur = jnp.where(g == 3, dsel, 0.0)
        for j, k in ((2, 8), (1, 4), (0, 2)):
            pad[0:s, :] = cur
            cur = cur + pad[k:k + s, :] + jnp.where(g == j, dsel, 0.0)
        pad[0:s, :] = cur
        cur = cur + pad[1:1 + s, :]
        du_ref[...] = (cur - dpooled).astype(BF16)

    return pl.pallas_call(
        body, name=name, grid=(4,),
        in_specs=[pl.BlockSpec((s, 128), lambda g: (0, O_PU // 128 + g)), pl.BlockSpec((None, 128, 128), lambda g: (g, 0, 0)),
                  pl.BlockSpec((1, 128), lambda g: (0, g)), pl.BlockSpec((s, 128), lambda g: (0, g))],
        out_specs=[pl.BlockSpec((s, 128), lambda g: (0, g)), pl.BlockSpec((None, 128, 128), lambda g: (g, 0, 0)),
                   pl.BlockSpec((1, 128), lambda g: (0, g))],
        out_shape=[jax.ShapeDtypeStruct((s, 512), BF16), jax.ShapeDtypeStruct((4, 128, 128), F32),
                   jax.ShapeDtypeStruct((1, 512), F32)],
        scratch_shapes=[pltpu.VMEM((s + 16, 128), F32)],
        compiler_params=_params("arbitrary"),
    )(proj, pw, ps, do)


def _xbc_col(i):
    return jnp.where(i < 2, O_XS // 512 + i, O_BC // 512)


def conv_fwd(proj, cw, cb, name):
    s = proj.shape[0]

    def body(x_ref, w_ref, b_ref, o_ref, t_ref, pad):
        pad[0:8, :] = jnp.zeros((8, 512), F32)
        pad[8:8 + s, :] = x_ref[...]
        y = b_ref[...] + sum(w_ref[k:k + 1, :] * pad[5 + k:5 + k + s, :] for k in range(4))
        act = y * _sig(y)
        o_ref[...] = act

        @pl.when(pl.program_id(0) < 2)
        def _():
            t_ref[...] = act.T

    return pl.pallas_call(
        body, name=name, grid=(3,),
        in_specs=[pl.BlockSpec((s, 512), lambda i: (0, _xbc_col(i))), pl.BlockSpec((4, 512), lambda i: (0, i)),
                  pl.BlockSpec((1, 512), lambda i: (0, i))],
        out_specs=[pl.BlockSpec((s, 512), lambda i: (0, i)), pl.BlockSpec((512, s), lambda i: (jnp.minimum(i, 1), 0))],
        out_shape=[jax.ShapeDtypeStruct((s, 1536), F32), jax.ShapeDtypeStruct((D, s), F32)],
        scratch_shapes=[pltpu.VMEM((s + 8, 512), F32)],
        compiler_params=_params("arbitrary"),
    )(proj, cw, cb)


def conv_bwd(proj, cw, cb, dxt, dbm, dcm, name):
    s = proj.shape[0]

    def body(x_ref, w_ref, b_ref, dxt_ref, dbm_ref, dcm_ref, dx_ref, dw_ref, db_ref, pad, pad2):
        pad[0:8, :] = jnp.zeros((8, 512), F32)
        pad[8:8 + s, :] = x_ref[...]
        y = b_ref[...] + sum(w_ref[k:k + 1, :] * pad[5 + k:5 + k + s, :] for k in range(4))
        sg = _sig(y)

        @pl.when(pl.program_id(0) < 2)
        def _():
            pad2[0:s, :] = dxt_ref[...].T

        @pl.when(pl.program_id(0) == 2)
        def _():
            pad2[0:s, 0:256] = dbm_ref[...]
            pad2[0:s, 256:512] = dcm_ref[...]

        dy = pad2[0:s, :] * (sg * (1.0 + y * (1.0 - sg)))
        db_ref[...] = jnp.sum(dy, axis=0, keepdims=True)
        for k in range(4):
            dw_ref[k:k + 1, :] = jnp.sum(dy * pad[5 + k:5 + k + s, :], axis=0, keepdims=True)
        pad2[s:s + 8, :] = jnp.zeros((8, 512), F32)
        pad2[0:s, :] = dy
        dx_ref[...] = sum(w_ref[k:k + 1, :] * pad2[3 - k:3 - k + s, :] for k in range(4)).astype(BF16)

    return pl.pallas_call(
        body, name=name, grid=(3,),
        in_specs=[pl.BlockSpec((s, 512), lambda i: (0, _xbc_col(i))), pl.BlockSpec((4, 512), lambda i: (0, i)),
                  pl.BlockSpec((1, 512), lambda i: (0, i)), pl.BlockSpec((512, s), lambda i: (jnp.minimum(i, 1), 0)),
                  pl.BlockSpec((s, 256), lambda i: (0, 0)), pl.BlockSpec((s, 256), lambda i: (0, 0))],
        out_specs=[pl.BlockSpec((s, 512), lambda i: (0, i)), pl.BlockSpec((4, 512), lambda i: (0, i)),
                   pl.BlockSpec((1, 512), lambda i: (0, i))],
        out_shape=[jax.ShapeDtypeStruct((s, 1536), BF16), jax.ShapeDtypeStruct((4, 1536), F32),
                   jax.ShapeDtypeStruct((1, 1536), F32)],
        scratch_shapes=[pltpu.VMEM((s + 8, 512), F32), pltpu.VMEM((s + 8, 512), F32)],
        compiler_params=_params("arbitrary"),
    )(proj, cw, cb, dxt, dbm, dcm)


def _ssd_chunk(xt, dtr, bm, cm, hprev, alog, dbias, dskip):
    ln = 128
    a = -jnp.exp(alog)
    dt_r = softplus(dtr + dbias)
    da_r = dt_r * a
    li = lax.broadcasted_iota(jnp.int32, (1, ln, ln), 1)
    si = lax.broadcasted_iota(jnp.int32, (1, ln, ln), 2)
    causal = si <= li
    acs_c = jnp.sum(jnp.where(causal, da_r, 0.0), axis=2, keepdims=True)
    acs_r = jnp.sum(jnp.where(li == si, acs_c, 0.0), axis=1, keepdims=True)
    acs_last = jnp.sum(da_r, axis=2, keepdims=True)
    decay = jnp.exp(jnp.where(causal, acs_c - acs_r, -jnp.inf))
    m = mm_nt(cm, bm)[None] * decay
    xdt = xt * dt_r
    y_diag = bmm_nt(xdt, m)
    bb = jnp.broadcast_to(bm[None], (8, ln, ln))
    cc = jnp.broadcast_to(cm[None], (8, ln, ln))
    states = bmm_nn(xdt * jnp.exp(acs_last - acs_r), bb)
    y_off = bmm_nt(hprev, cc) * jnp.exp(acs_r)
    hnew = hprev * jnp.exp(acs_last) + states
    return y_diag + y_off + xt * dskip, hnew


def _ssd_specs(nc, rev):
    cix = (lambda c: nc - 1 - c) if rev else (lambda c: c)
    hv = pl.BlockSpec((8, 1, 1), lambda g, c: (g, 0, 0))
    return [pl.BlockSpec((8, 64, 128), lambda g, c: (g, 0, cix(c))), pl.BlockSpec((8, 1, 128), lambda g, c: (g, 0, cix(c))),
            pl.BlockSpec((128, 128), lambda g, c: (cix(c), 8 + g)),
            pl.BlockSpec((128, 128), lambda g, c: (cix(c), 10 + g))], hv, cix


def ssd_fwd(xt, dtr, xbc, alog, dbias, dskip, name):
    s = xt.shape[2]
    nc = s // 128
    specs, hv, _ = _ssd_specs(nc, False)

    def body(x_ref, dr_ref, b_ref, c_ref, al_ref, db_ref, dk_ref, y_ref, hs_ref, h_scr):
        @pl.when(pl.program_id(1) == 0)
        def _():
            h_scr[...] = jnp.zeros_like(h_scr)
        hp = h_scr[...]
        hs_ref[...] = hp
        y, hn = _ssd_chunk(x_ref[...], dr_ref[...], b_ref[...], c_ref[...], hp, al_ref[...], db_ref[...], dk_ref[...])
        y_ref[...] = y
        h_scr[...] = hn

    return pl.pallas_call(
        body, name=name, grid=(2, nc), in_specs=specs + [hv, hv, hv],
        out_specs=[pl.BlockSpec((8, 64, 128), lambda g, c: (g, 0, c)),
                   pl.BlockSpec((None, None, 8, 64, 128), lambda g, c: (g, c, 0, 0, 0))],
        out_shape=[jax.ShapeDtypeStruct((16, 64, s), F32), jax.ShapeDtypeStruct((2, nc, 8, 64, 128), F32)],
        scratch_shapes=[pltpu.VMEM((8, 64, 128), F32)],
        compiler_params=_params("arbitrary", "arbitrary"),
    )(xt, dtr, xbc, xbc, alog, dbias, dskip)


def ssd_bwd(xt, dtr, xbc, alog, dbias, dskip, hs, dyt, name):
    s = xt.shape[2]
    nc = s // 128
    specs, hv, cix = _ssd_specs(nc, True)

    def body(x_ref, dr_ref, b_ref, c_ref, al_ref, db_ref, dk_ref, hs_ref, dy_ref,
             dx_out, ddr_out, dbm_out, dcm_out, dal_out, ddb_out, ddk_out, dh_scr):
        @pl.when(pl.program_id(1) == 0)
        def _():
            dh_scr[...] = jnp.zeros_like(dh_scr)
            dal_out[...] = jnp.zeros_like(dal_out)
            ddb_out[...] = jnp.zeros_like(ddb_out)
            ddk_out[...] = jnp.zeros_like(ddk_out)
        _, vjp = jax.vjp(_ssd_chunk, x_ref[...], dr_ref[...], b_ref[...], c_ref[...], hs_ref[...],
                         al_ref[...], db_ref[...], dk_ref[...])
        g = vjp((dy_ref[...], dh_scr[...]))
        dx_out[...] = g[0]
        ddr_out[...] = g[1]
        dbm_out[...] = g[2]
        dcm_out[...] = g[3]
        dh_scr[...] = g[4]
        dal_out[...] += g[5]
        ddb_out[...] += g[6]
        ddk_out[...] += g[7]

    return pl.pallas_call(
        body, name=name, grid=(2, nc),
        in_specs=specs + [hv, hv, hv, pl.BlockSpec((None, None, 8, 64, 128), lambda g, c: (g, cix(c), 0, 0, 0)),
                          pl.BlockSpec((8, 64, 128), lambda g, c: (g, 0, cix(c)))],
        out_specs=[pl.BlockSpec((8, 64, 128), lambda g, c: (g, 0, cix(c))), pl.BlockSpec((8, 1, 128), lambda g, c: (g, 0, cix(c))),
                   pl.BlockSpec((128, 128), lambda g, c: (cix(c), g)),
                   pl.BlockSpec((128, 128), lambda g, c: (cix(c), g)), hv, hv, hv],
        out_shape=[jax.ShapeDtypeStruct((16, 64, s), F32), jax.ShapeDtypeStruct((16, 1, s), F32),
                   jax.ShapeDtypeStruct((s, 256), F32),
                   jax.ShapeDtypeStruct((s, 256), F32)] + [jax.ShapeDtypeStruct((16, 1, 1), F32)] * 3,
        scratch_shapes=[pltpu.VMEM((8, 64, 128), F32)],
        compiler_params=_params("arbitrary", "arbitrary"),
    )(xt, dtr, xbc, xbc, alog, dbias, dskip, hs, dyt)


def _merge(oa, ob, y, z, gla, glb, glc, x, g1, nw, ea, eb, ec, eo, wba, wbb, wbc, wout):
    gated = y * (z * _sig(z))
    sq = gated * gated
    left = lax.broadcasted_iota(jnp.int32, (1, D), 1) < 512
    ms0 = jnp.sum(jnp.where(left, sq, 0.0), axis=-1, keepdims=True) / 512.0
    ms1 = jnp.sum(jnp.where(left, 0.0, sq), axis=-1, keepdims=True) / 512.0
    oc = gated * jnp.where(left, lax.rsqrt(ms0 + EPS), lax.rsqrt(ms1 + EPS)) * nw
    ya, yb, yc = mm_nc(oa, wba) + ea, mm_nc(ob, wbb) + eb, mm_nc(oc, wbc) + ec
    merged = _sig(gla) * ya + _sig(glb) * yb + _sig(glc) * yc
    x1 = x + g1 * (mm_nc(merged, wout) + eo)
    return x1, (oc, merged)


def _merge_specs(tr):
    row = lambda w: pl.BlockSpec((tr, w), lambda i: (i, 0))
    acts = [row(D), row(512), pl.BlockSpec((D, tr), lambda i: (0, i)), pl.BlockSpec((tr, D), lambda i: (i, O_Z // D)),
            pl.BlockSpec((tr, 3 * D), lambda i: (i, 0)), row(D), pl.BlockSpec((8, D), lambda i: (0, 0))]
    cst = lambda r: pl.BlockSpec((r, D), lambda i: (0, 0), **CONST)
    return acts, [cst(D), cst(512), cst(D), cst(D)], row


def merge_fwd(oa, ob, y, proj, x, mvec, wba, wbb, wbc, wout, name):
    s = x.shape[0]
    tr = _pick(s, 256)
    acts, wts, row = _merge_specs(tr)

    def body(oa_ref, ob_ref, y_ref, z_ref, gl_ref, x_ref, mv_ref, wba_ref, wbb_ref, wbc_ref, wout_ref, o_ref):
        zero = jnp.zeros((1, D), F32)
        x1, _ = _merge(oa_ref[...], ob_ref[...], y_ref[...].T, z_ref[...], gl_ref[:, 0:D], gl_ref[:, D:2 * D],
                       gl_ref[:, 2 * D:3 * D], x_ref[...], mv_ref[0:1, :], mv_ref[1:2, :], zero, zero, zero, zero,
                       wba_ref[...], wbb_ref[...], wbc_ref[...], wout_ref[...])
        o_ref[...] = x1

    return pl.pallas_call(
        body, name=name, grid=(s // tr,), in_specs=acts + wts, out_specs=row(D),
        out_shape=jax.ShapeDtypeStruct((s, D), F32), compiler_params=_params("arbitrary"),
    )(oa, ob, y, proj, proj, x, mvec, wba, wbb, wbc, wout)


def merge_bwd(oa, ob, y, proj, x, mvec, wba, wbb, wbc, wout, dx1, name):
    s = x.shape[0]
    tr = _pick(s, 128)
    acts, wts, row = _merge_specs(tr)

    def body(oa_ref, ob_ref, y_ref, z_ref, gl_ref, x_ref, mv_ref, wba_ref, wbb_ref, wbc_ref, wout_ref, dx1_ref,
             doa_o, dob_o, dy_o, dz_o, dgl_o, dx_o, dmv_o, dya_o, dyb_o, dyc_o, dpre_o, oc_o, mg_o):
        zero = jnp.zeros((tr, D), F32)
        wts_ = (wba_ref[...], wbb_ref[...], wbc_ref[...], wout_ref[...])
        f = lambda *a: _merge(*a, *wts_)
        _, vjp, (oc, merged) = jax.vjp(
            f, oa_ref[...], ob_ref[...], y_ref[...].T, z_ref[...], gl_ref[:, 0:D], gl_ref[:, D:2 * D],
            gl_ref[:, 2 * D:3 * D], x_ref[...], mv_ref[0:1, :], mv_ref[1:2, :], zero, zero, zero, zero, has_aux=True)
        g = vjp(dx1_ref[...])
        doa_o[...] = g[0]
        dob_o[...] = g[1]
        dy_o[...] = g[2].T
        dz_o[...] = g[3].astype(BF16)
        dgl_o[:, 0:D] = g[4].astype(BF16)
        dgl_o[:, D:2 * D] = g[5].astype(BF16)
        dgl_o[:, 2 * D:3 * D] = g[6].astype(BF16)
        dx_o[...] = g[7]

        @pl.when(pl.program_id(0) == 0)
        def _():
            dmv_o[...] = jnp.zeros_like(dmv_o)

        dmv_o[0:1, :] += g[8]
        dmv_o[1:2, :] += g[9]
        dya_o[...] = g[10].astype(BF16)
        dyb_o[...] = g[11].astype(BF16)
        dyc_o[...] = g[12].astype(BF16)
        dpre_o[...] = g[13].astype(BF16)
        oc_o[...] = oc.astype(BF16)
        mg_o[...] = merged.astype(BF16)

    sd = lambda w, dt: jax.ShapeDtypeStruct((s, w), dt)
    return pl.pallas_call(
        body, name=name, grid=(s // tr,), in_specs=acts + wts + [row(D)],
        out_specs=[row(D), row(512), pl.BlockSpec((D, tr), lambda i: (0, i)), row(D), row(3 * D), row(D),
                   pl.BlockSpec((8, D), lambda i: (0, 0))] + [row(D)] * 6,
        out_shape=[sd(D, F32), sd(512, F32), jax.ShapeDtypeStruct((D, s), F32), sd(D, BF16), sd(3 * D, BF16), sd(D, F32),
                   jax.ShapeDtypeStruct((8, D), F32)] + [sd(D, BF16)] * 6,
        compiler_params=_params("arbitrary"),
    )(oa, ob, y, proj, proj, x, mvec, wba, wbb, wbc, wout, dx1)


def _conv3(u_scr, w_ref, first, rows, lanes):
    return sum(w_ref[k:k + 1, :] * u_scr[first + k:first + k + rows, lanes] for k in range(3))


def _ffn_tile_specs(tf, tile):
    def at(rows, off):
        return pl.BlockSpec((rows, tf), lambda *g: (0, off + tile(*g)))

    def wt(off):
        return pl.BlockSpec((tf, D), lambda *g: (off + tile(*g), 0))
    return [wt(0), wt(FFN_NT), at(3, 0), at(3, FFN_NT), at(1, 0), at(1, FFN_NT)]


def ffn_fwd(x1, fvec, wup, cw, cb, wdn, name):
    s = x1.shape[0]
    tr, tf = _pick(s, 512), FFN_TILE
    lg, lv = slice(0, tf), slice(tf, 2 * tf)

    def body(x_ref, v_ref, wg_ref, wv_ref, cwg_ref, cwv_ref, cbg_ref, cbv_ref, wd_ref, x2_ref, h_ref, pre_ref,
             h_scr, u_scr, acc):
        i, t = pl.program_id(0), pl.program_id(1)

        @pl.when(t == 0)
        def _():
            @pl.when(i == 0)
            def _():
                h_scr[0:16, :] = jnp.zeros((16, D), BF16)

            @pl.when(i > 0)
            def _():
                h_scr[0:16, :] = h_scr[tr:tr + 16, :]

            h = (_rms(x_ref[...], v_ref[0:1, :], D) * (1.0 + v_ref[2:3, :]) + v_ref[1:2, :]).astype(BF16)
            h_scr[16:16 + tr, :] = h
            h_ref[...] = h
            acc[...] = jnp.zeros_like(acc)

        u_scr[:, lg] = _raw(h_scr[...], wg_ref[...], _NT)
        u_scr[:, lv] = _raw(h_scr[...], wv_ref[...], _NT)
        cg = _conv3(u_scr, cwg_ref, 14, tr, lg) + cbg_ref[...]
        cval = _conv3(u_scr, cwv_ref, 14, tr, lv) + cbv_ref[...]
        acc[...] += _raw(cg * _sig(cg) * cval, wd_ref[...], _NN)

        @pl.when(t == FFN_NT - 1)
        def _():
            pre_ref[...] = acc[...]
            x2_ref[...] = x_ref[...] + v_ref[3:4, :] * acc[...]

    row = pl.BlockSpec((tr, D), lambda i, t: (i, 0))
    return pl.pallas_call(
        body, name=name, grid=(s // tr, FFN_NT),
        in_specs=[row, pl.BlockSpec((8, D), lambda i, t: (0, 0))] + _ffn_tile_specs(tf, lambda i, t: t)
                 + [pl.BlockSpec((tf, D), lambda i, t: (t, 0))],
        out_specs=[row, row, row],
        out_shape=[jax.ShapeDtypeStruct((s, D), F32), jax.ShapeDtypeStruct((s, D), BF16), jax.ShapeDtypeStruct((s, D), F32)],
        scratch_shapes=[pltpu.VMEM((tr + 16, D), BF16), pltpu.VMEM((tr + 16, 2 * tf), F32), pltpu.VMEM((tr, D), F32)],
        compiler_params=_params("arbitrary", "arbitrary"),
    )(x1, fvec, wup, wup, cw, cw, cb, cb, wdn)


def ffn_bwd(h2, dx2, fvec, wup, cw, cb, wdn, name):
    s = h2.shape[0]
    tr, tf = _pick(s, 512), FFN_TILE
    ni, nb = s // tr, s // 16
    lg, lv = slice(0, tf), slice(tf, 2 * tf)

    def body(hp_ref, hm_ref, hn_ref, dm_ref, dn_ref, v_ref, wg_ref, wv_ref, cwg_ref, cwv_ref, cbg_ref, cbv_ref, wd_ref,
             dup_ref, act_ref, dcw_ref, u_scr, dc_scr):
        i = pl.program_id(1)
        hfull = jnp.concatenate([jnp.where(i > 0, hp_ref[...], jnp.zeros((16, D), BF16)), hm_ref[...],
                                 jnp.where(i < ni - 1, hn_ref[...], jnp.zeros((16, D), BF16))], axis=0)
        u_scr[:, lg] = _raw(hfull, wg_ref[...], _NT)
        u_scr[:, lv] = _raw(hfull, wv_ref[...], _NT)
        cg = _conv3(u_scr, cwg_ref, 14, tr + 16, lg) + cbg_ref[...]
        cval = _conv3(u_scr, cwv_ref, 14, tr + 16, lv) + cbv_ref[...]
        g2 = v_ref[3:4, :]
        dpre = jnp.concatenate([dm_ref[...] * g2, jnp.where(i < ni - 1, dn_ref[...], 0.0) * g2], axis=0)
        dact = _raw(dpre, wd_ref[...], _NT)
        sg = _sig(cg)
        sl = cg * sg
        dc_scr[:, lg] = dact * cval * (sg * (1.0 + cg * (1.0 - sg)))
        dc_scr[:, lv] = dact * sl
        act_ref[...] = (sl * cval)[0:tr, :].astype(BF16)

        @pl.when(i == 0)
        def _():
            dcw_ref[...] = jnp.zeros_like(dcw_ref)

        for half, lanes, cw_ref in ((0, lg, cwg_ref), (1, lv, cwv_ref)):
            u_main, dup = u_scr[16:16 + tr, lanes], None
            for k in range(3):
                dsh = dc_scr[2 - k:2 - k + tr, lanes]
                term = cw_ref[k:k + 1, :] * dsh
                dup = term if dup is None else dup + term
                dcw_ref[half, k:k + 1, :] += jnp.sum(dsh * u_main, axis=0, keepdims=True)
            dup_ref[half] = dup.astype(BF16)
            dcw_ref[half, 3:4, :] += jnp.sum(dc_scr[0:tr, lanes], axis=0, keepdims=True)

    r16 = tr // 16
    prev = lambda t, i: (jnp.maximum(i * r16 - 1, 0), 0)
    nxt = lambda t, i: (jnp.minimum((i + 1) * r16, nb - 1), 0)
    main = lambda t, i: (i, 0)
    return pl.pallas_call(
        body, name=name, grid=(FFN_NT, ni),
        in_specs=[pl.BlockSpec((16, D), prev), pl.BlockSpec((tr, D), main), pl.BlockSpec((16, D), nxt),
                  pl.BlockSpec((tr, D), main), pl.BlockSpec((16, D), nxt), pl.BlockSpec((8, D), lambda t, i: (0, 0))]
                 + _ffn_tile_specs(tf, lambda t, i: t) + [pl.BlockSpec((tf, D), lambda t, i: (t, 0))],
        out_specs=[pl.BlockSpec((2, tr, tf), lambda t, i: (0, i, t)), pl.BlockSpec((tr, tf), lambda t, i: (i, t)),
                   pl.BlockSpec((2, 8, tf), lambda t, i: (0, 0, t))],
        out_shape=[jax.ShapeDtypeStruct((2, s, FFN), BF16), jax.ShapeDtypeStruct((s, FFN), BF16),
                   jax.ShapeDtypeStruct((2, 8, FFN), F32)],
        scratch_shapes=[pltpu.VMEM((tr + 32, 2 * tf), F32), pltpu.VMEM((tr + 16, 2 * tf), F32)],
        compiler_params=_params("arbitrary", "arbitrary"),
    )(h2, h2, h2, dx2, dx2, fvec, wup, wup, cw, cw, cb, cb, wdn)


def loss_head(y, target):
    s = y.shape[0]
    tr = _pick(s, 512)

    def body(y_ref, t_ref, dx_ref, l_ref):
        @pl.when(pl.program_id(0) == 0)
        def _():
            l_ref[...] = jnp.zeros_like(l_ref)
        err = y_ref[...] - t_ref[...]
        dx_ref[...] = err / float(D)
        l_ref[...] += 0.5 * jnp.sum(jnp.sum(err * err, axis=-1, keepdims=True) / float(D), axis=0, keepdims=True)

    row = pl.BlockSpec((tr, D), lambda i: (i, 0))
    return pl.pallas_call(
        body, name="loss_head", grid=(s // tr,), in_specs=[row, row],
        out_specs=[row, pl.BlockSpec((8, 128), lambda i: (0, 0))],
        out_shape=[jax.ShapeDtypeStruct((s, D), F32), jax.ShapeDtypeStruct((8, 128), F32)],
        compiler_params=_params("arbitrary"),
    )(y, target)


def adamw(parts, w, m, v, name, tok=None):
    nseg = len(parts)
    p, r, c = parts[0].shape
    tok = jnp.zeros((8, 128), F32) if tok is None else tok
    cap = 256 if c > 128 else 2048
    step = lambda q, l, i, ni: jnp.clip((l - q) * ni + i, 0, ni - 1)
    if r <= cap or any(r % t == 0 for t in range(8, cap + 1, 8)):
        tr, tc = _pick(r, cap, 8), c
        ni = r // tr
        row = pl.BlockSpec((None, tr, tc), lambda l, i: (l, i, 0))
        part = lambda q: pl.BlockSpec((p, tr, tc), lambda l, i: (0, step(q, l, i, ni), 0))
    else:
        tr, tc = r, _pick(c, 256)
        ni = c // tc
        row = pl.BlockSpec((None, tr, tc), lambda l, i: (l, 0, i))
        part = lambda q: pl.BlockSpec((p, tr, tc), lambda l, i: (0, 0, step(q, l, i, ni)))

    def body(*refs):
        p_refs = refs[:nseg]
        w_ref, m_ref, v_ref, _, g_out, d_out, m_out, v_out, g_scr = refs[nseg:]
        for q in range(nseg):
            @pl.when(pl.program_id(0) == q)
            def _(q=q):
                g = p_refs[q][0].astype(F32)
                for j in range(1, p):
                    g = g + p_refs[q][j].astype(F32)
                g_scr[...] = g
        g = g_scr[...]
        mn = B1 * m_ref[...] + (1.0 - B1) * g
        vn = B2 * v_ref[...] + (1.0 - B2) * (g * g)
        m_hat = mn / (1.0 - B1 ** STEP)
        v_hat = vn / (1.0 - B2 ** STEP)
        g_out[...] = g
        d_out[...] = -LR * (m_hat / (jnp.sqrt(v_hat) + ADAM_EPS) + WD * w_ref[...])
        m_out[...] = mn
        v_out[...] = vn

    return pl.pallas_call(
        body, name=name, grid=(nseg, ni),
        in_specs=[part(q) for q in range(nseg)] + [row, row, row, pl.BlockSpec((8, 128), lambda l, i: (0, 0))],
        out_specs=[row] * 4, out_shape=[jax.ShapeDtypeStruct((nseg, r, c), F32)] * 4,
        scratch_shapes=[pltpu.VMEM((tr, tc), F32)],
        compiler_params=_params("arbitrary", "arbitrary"),
    )(*parts, w, m, v, tok)


def _padc(a, n):
    return jnp.pad(a, [(0, 0)] * (a.ndim - 1) + [(0, n - a.shape[-1])])


def _swap16(a):
    return jnp.concatenate([a[..., 16:32], a[..., 0:16]], axis=-1)


def _shard_cols(g8, a, b):
    c = g8.shape[2]
    return [g8[j][:, max(a, j * c) - j * c:min(b, (j + 1) * c) - j * c] for j in range(a // c, (b - 1) // c + 1)]


def _padr(a, n):
    return jnp.pad(a, ((0, n - a.shape[0]), (0, 0)))


def _swap16r(a):
    return jnp.concatenate([a[16:32], a[0:16]], axis=0)


def _win_layout(g8):
    w = g8.reshape(NDEV * g8.shape[1], g8.shape[2])
    kr = w[640:672]
    return jnp.concatenate([w[3760:6832], w[2208:3232], w[1184:2208], w[672:1184], w[3232:3744], w[384:640],
                            _padr(kr, 128), _padr(_swap16r(kr), 128), _padr(w[3744:3760], 128),
                            jnp.zeros((128, w.shape[1]), w.dtype), w[0:384]], axis=0)


def _win_grad_shards(g):
    kr = (g[O_KR:O_KR + 32].astype(F32) + _swap16r(g[O_KRS:O_KRS + 32].astype(F32))).astype(g.dtype)
    segs = [(g, O_QL, 384), (g, O_CKV, 256), (kr, 0, 32), (g, O_PU, 512), (g, O_Z, D), (g, O_XS, D), (g, O_BC, 512),
            (g, O_DT, 16), (g, O_G, 3 * D)]
    shards, height = [], sum(w for _, _, w in segs) // NDEV
    for j in range(NDEV):
        a, b, off, pieces = height * j, height * (j + 1), 0, []
        for arr, lo, w in segs:
            s0, s1 = max(a, off), min(b, off + w)
            if s0 < s1:
                pieces.append(arr[lo + s0 - off:lo + s1 - off])
            off += w
        shards.append(jnp.concatenate(pieces, axis=0))
    return jnp.stack(shards).astype(BF16)


def _wq_layout(w):
    w = w.reshape(384, HEADS, 96).transpose(1, 0, 2)
    rope = w[:, :, 64:96]
    return jnp.concatenate([_padc(w[:, :, 0:64], 128), _padc(rope, 128), _padc(_swap16(rope), 128)], axis=2)


def _wq_unlayout(g):
    rope = g[:, :, 128:160] + _swap16(g[:, :, 256:288])
    return jnp.concatenate([g[:, :, 0:64], rope], axis=2).transpose(1, 0, 2).reshape(384, HEADS * 96)


def _wkv_layout(w):
    w = w.reshape(256, HEADS, 128).transpose(1, 0, 2)
    return jnp.concatenate([_padc(w[:, :, 0:64], 128), _padc(w[:, :, 64:128], 128)], axis=2)


def _wkv_unlayout(g):
    return jnp.concatenate([g[:, :, 0:64], g[:, :, 128:192]], axis=2).transpose(1, 0, 2).reshape(256, HEADS * 128)


def _wba_layout(w):
    return jnp.pad(w.reshape(HEADS, 64, D), ((0, 0), (0, 64), (0, 0))).reshape(HEADS * 128, D)


def _rows8(rows, width):
    out = jnp.stack([_padc(r.astype(F32), width) for r in rows])
    return jnp.pad(out, ((0, 8 - out.shape[0]), (0, 0)))


def _mla_vec(qa, kva, qn, kn):
    def row(n):
        return jnp.concatenate([_padc(n[0:64], 128), _padc(n[64:96], 128), _padc(_swap16(n[64:96]), 128)])
    return _rows8([qa, kva, row(qn), row(kn)], 512)


def _mla_unvec(g):
    def un(r):
        return jnp.concatenate([r[0:64], r[128:160] + _swap16(r[256:288])])
    return g[0, 0:384], g[1, 0:256], un(g[2]), un(g[3])


SMALL = (("ada_b", (6 * D,)), ("norm1_w", (D,)), ("q_a_norm", (384,)), ("kv_a_norm", (256,)), ("q_norm", (96,)),
         ("k_norm", (96,)), ("pool_w", (4, 128, 128)), ("pool_scale", (512,)), ("ssd_conv_b", (1536,)),
         ("ssd_dt_bias", (16,)), ("ssd_a_log", (16,)), ("ssd_d", (16,)), ("ssd_norm_w", (D,)), ("norm2_w", (D,)),
         ("ffn_conv_b", (2 * FFN,)), ("ssd_conv_w", (4, 1536)), ("ffn_conv_w", (3, 2 * FFN)))
SHARDED_SMALL = {"ssd_conv_w": 192, "ffn_conv_w": 704}


def _pack_rows(shp):
    return -(-math.prod(shp) // 1024) * 8


def _pack(small):
    pieces = []
    for n, shp in SMALL:
        pieces.append(small[n].reshape(-1).astype(F32))
        fill = _pack_rows(shp) * 128 - math.prod(shp)
        if fill:
            pieces.append(jnp.zeros((fill,), F32))
    return jnp.concatenate(pieces).reshape(-1, 128)


def _unpack_parts(packs):
    out, off = {}, 0
    for n, shp in SMALL:
        rows, size = _pack_rows(shp), math.prod(shp)
        r, c = math.prod(shp[:-1]), shp[-1]
        per_layer = [pk[:, off:off + rows].reshape(NDEV, rows * 128)[:, 0:size].reshape(NDEV, r, c) for pk in packs]
        out[n] = jnp.concatenate(per_layer, axis=1)
        off += rows
    return out


GROUP_A = ("w_in", "w_q_b", "w_kv_b")
GROUP_B = ("w_branch", "w_out", "ffn_up", "ffn_down")
BIG = GROUP_A + GROUP_B
SCATTER_FFN, SCATTER_MERGE = ("ffn_up", "ffn_down"), ("w_branch", "w_out")
COL_SHARDED = ("w_q_b", "w_kv_b")
TRANSPOSED = ("w_in", "ffn_up")


def _behind(arrs, tok):
    arrs = list(arrs)
    j = min(range(len(arrs)), key=lambda q: arrs[q].size)
    arrs[j] = arrs[j] + tok[0, 0].astype(arrs[j].dtype)
    return arrs


def _gathered_full(g, name):
    if name in COL_SHARDED:
        return g.transpose(1, 0, 2).reshape(g.shape[1], NDEV * g.shape[2])
    return g.reshape(NDEV * g.shape[1], g.shape[2])


def _to_shards(full, name):
    if name == "w_in":
        return _win_grad_shards(full)
    if name in COL_SHARDED:
        r, c = full.shape
        return full.reshape(r, NDEV, c // NDEV).transpose(1, 0, 2).astype(BF16)
    r, c = full.shape
    return full.reshape(NDEV, r // NDEV, c).astype(BF16)


def _fwd_a(x, lw, mod, cos2, sin2, l, tok):
    sh1, sc1, g1, sh2, sc2, g2 = [mod[j * D:(j + 1) * D] for j in range(6)]
    vec1 = _rows8([lw["norm1_w"], sh1, sc1], D) + tok[0, 0]
    proj, h1, dt_cols = norm_proj_fwd(x, vec1, lw["win"], f"inproj_fwd{l}")
    q, k, v = mla_pre_fwd(proj, lw["wq"], lw["wkv"], lw["mla_vec"], cos2, sin2, f"mla_pre_fwd{l}")
    oa = mla_attn_fwd(q, k, v, f"mla_attn_fwd{l}")
    ob = pool_fwd(proj, lw["pool_w"], lw["pool_scale"].reshape(1, 512), f"pool_fwd{l}")
    xbc, xt = conv_fwd(proj, lw["ssd_conv_w"], lw["ssd_conv_b"].reshape(1, 1536), f"conv_fwd{l}")
    s = x.shape[0]
    xt = xt.reshape(16, 64, s)
    dt = dt_cols[:, 0:16].T
    dtr = dt[:, None, :]
    hv = lambda a: a.reshape(16, 1, 1)
    yt, hs = ssd_fwd(xt, dtr, xbc, hv(lw["ssd_a_log"]), hv(lw["ssd_dt_bias"]), hv(lw["ssd_d"]), f"ssd_fwd{l}")
    return dict(x=x, vec1=vec1, proj=proj, h1=h1, q=q, k=k, v=v, oa=oa, ob=ob, xbc=xbc, xt=xt, dtr=dtr,
                hs=hs, yt=yt.reshape(D, s), mvec=_rows8([g1, lw["ssd_norm_w"]], D),
                fvec=_rows8([lw["norm2_w"], sh2, sc2, g2], D))


def _fwd_b(sv, lw, l, tok):
    sv["mvec"] = sv["mvec"] + tok[0, 0]
    x1 = merge_fwd(sv["oa"], sv["ob"], sv["yt"], sv["proj"], sv["x"], sv["mvec"], lw["wba"], lw["wbb"], lw["wbc"],
                   lw["wout"], f"merge_fwd{l}")
    x2, h2, pre = ffn_fwd(x1, sv["fvec"], lw["wup"], lw["ffn_conv_w"], lw["ffn_conv_b"].reshape(1, 2 * FFN), lw["wdn"],
                          f"ffn_fwd{l}")
    sv.update(x1=x1, h2=h2, pre=pre)
    return x2


def _bwd_ffn(dx2, lw, sv, l, tok):
    fvec = sv["fvec"] + tok[0, 0]
    dup, act, dcw = ffn_bwd(sv["h2"], dx2, fvec, lw["wup"], lw["ffn_conv_w"], lw["ffn_conv_b"].reshape(1, 2 * FFN),
                            lw["wdn"], f"ffn_bwd{l}")
    grads = dict(ffn_down=tn_matmul(act, dx2, f"dw_down{l}", scale=fvec[3:4]),
                 ffn_up=tn_matmul(dup, sv["h2"], f"dw_up{l}"))
    small = dict(ffn_conv_w=jnp.concatenate([dcw[0, 0:3], dcw[1, 0:3]], axis=1),
                 ffn_conv_b=jnp.concatenate([dcw[0, 3], dcw[1, 3]]))
    return dup, grads, small


def _bwd_merge(dx2, dup, lw, sv, l, tok, small):
    grads = {}
    fvec = sv["fvec"] + tok[0, 0]
    dx1, dfvec = norm_proj_bwd(sv["x1"], fvec, dup, lw["wup"], dx2, sv["pre"], f"ffn_norm_bwd{l}")
    small["norm2_w"] = dfvec[0]
    (doa, dob, dyt, dz, dgl, dx, dmvec, dya, dyb, dyc, dpre, oc, merged) = merge_bwd(
        sv["oa"], sv["ob"], sv["yt"], sv["proj"], sv["x"], sv["mvec"], lw["wba"], lw["wbb"], lw["wbc"], lw["wout"], dx1,
        f"merge_bwd{l}")
    dwba = tn_matmul(sv["oa"], dya, f"dw_ba{l}").reshape(HEADS, 128, D)[:, 0:64].reshape(512, D)
    grads["w_branch"] = jnp.concatenate([dwba, tn_matmul(sv["ob"], dyb, f"dw_bb{l}"), tn_matmul(oc, dyc, f"dw_bc{l}")])
    grads["w_out"] = tn_matmul(merged, dpre, f"dw_out{l}")
    small["ssd_norm_w"] = dmvec[1]
    small["dmod_b"] = (dmvec[0], dfvec[1], dfvec[2], dfvec[3])
    return dx, dict(doa=doa, dob=dob, dyt=dyt, dz=dz, dgl=dgl), grads, small


def _bwd_a(dx, cot, lw, sv, cos2, sin2, l, tok, small):
    s = dx.shape[0]
    grads = {}
    doa, dob, dz, dgl = cot["doa"], cot["dob"], cot["dz"], cot["dgl"]
    hv = lambda a: a.reshape(16, 1, 1)
    dxt, ddtr, dbm, dcm, dal, ddb, ddk = ssd_bwd(
        sv["xt"], sv["dtr"], sv["xbc"], hv(lw["ssd_a_log"]) + tok[0, 0], hv(lw["ssd_dt_bias"]),
        hv(lw["ssd_d"]), sv["hs"], cot["dyt"].reshape(16, 64, s), f"ssd_bwd{l}")
    small["ssd_a_log"], small["ssd_dt_bias"], small["ssd_d"] = dal.reshape(16), ddb.reshape(16), ddk.reshape(16)
    dxbc, dscw, dscb = conv_bwd(sv["proj"], lw["ssd_conv_w"], lw["ssd_conv_b"].reshape(1, 1536), dxt.reshape(D, s),
                                dbm, dcm, f"conv_bwd{l}")
    small["ssd_conv_w"], small["ssd_conv_b"] = dscw, dscb.reshape(1536)
    ddt = ddtr[:, 0, :].T
    du, dpw, dps = pool_bwd(sv["proj"], lw["pool_w"], lw["pool_scale"].reshape(1, 512), dob, f"pool_bwd{l}")
    small["pool_w"], small["pool_scale"] = dpw, dps.reshape(512)
    dq, dk, dv = mla_attn_bwd(sv["q"], sv["k"], sv["v"], doa, f"mla_attn_bwd{l}")
    dql, dckv, dkr, dkrs, dwq, dwkv, dmv = mla_pre_bwd(sv["proj"], lw["wq"], lw["wkv"], lw["mla_vec"], cos2, sin2,
                                                       dq, dk, dv, f"mla_pre_bwd{l}")
    grads["w_q_b"], grads["w_kv_b"] = _wq_unlayout(dwq), _wkv_unlayout(dwkv)
    small["q_a_norm"], small["kv_a_norm"], small["q_norm"], small["k_norm"] = _mla_unvec(dmv)
    dproj = jnp.concatenate([dgl, dxbc[:, 0:D], dz, du, dxbc[:, D:1536], dckv, dkr, dkrs,
                             _padc(ddt, 128).astype(BF16), jnp.zeros((s, 128), BF16), dql], axis=1)
    grads["w_in"] = tn_matmul(dproj, sv["h1"], f"dw_in{l}")
    return dproj, grads, small


def _bwd_in(dx, dproj, lw, sv, l, tok, small):
    dx0, dvec1 = norm_proj_bwd(sv["x"], sv["vec1"] + tok[0, 0], dproj, lw["win"], dx, None, f"inproj_bwd{l}")
    small["norm1_w"] = dvec1[0]
    small["ada_b"] = jnp.concatenate([dvec1[1], dvec1[2], *small.pop("dmod_b")])
    return dx0, small


def kernel(x, c, positions, ada_w, ada_b, norm1_w, w_in, q_a_norm, w_q_b, kv_a_norm, w_kv_b, q_norm, k_norm, pool_w, pool_scale, ssd_conv_w, ssd_conv_b, ssd_dt_bias, ssd_a_log, ssd_d, ssd_norm_w, w_branch, w_out, norm2_w, ffn_up, ffn_conv_w, ffn_conv_b, ffn_down, loss_target, m_ada_w, m_ada_b, m_norm1_w, m_w_in, m_q_a_norm, m_w_q_b, m_kv_a_norm, m_w_kv_b, m_q_norm, m_k_norm, m_pool_w, m_pool_scale, m_ssd_conv_w, m_ssd_conv_b, m_ssd_dt_bias, m_ssd_a_log, m_ssd_d, m_ssd_norm_w, m_w_branch, m_w_out, m_norm2_w, m_ffn_up, m_ffn_conv_w, m_ffn_conv_b, m_ffn_down, v_ada_w, v_ada_b, v_norm1_w, v_w_in, v_q_a_norm, v_w_q_b, v_kv_a_norm, v_w_kv_b, v_q_norm, v_k_norm, v_pool_w, v_pool_scale, v_ssd_conv_w, v_ssd_conv_b, v_ssd_dt_bias, v_ssd_a_log, v_ssd_d, v_ssd_norm_w, v_w_branch, v_w_out, v_norm2_w, v_ffn_up, v_ffn_conv_w, v_ffn_conv_b, v_ffn_down):
    p = dict(ada_w=ada_w, ada_b=ada_b, norm1_w=norm1_w, w_in=w_in, q_a_norm=q_a_norm, w_q_b=w_q_b, kv_a_norm=kv_a_norm,
             w_kv_b=w_kv_b, q_norm=q_norm, k_norm=k_norm, pool_w=pool_w, pool_scale=pool_scale, ssd_conv_w=ssd_conv_w,
             ssd_conv_b=ssd_conv_b, ssd_dt_bias=ssd_dt_bias, ssd_a_log=ssd_a_log, ssd_d=ssd_d, ssd_norm_w=ssd_norm_w,
             w_branch=w_branch, w_out=w_out, norm2_w=norm2_w, ffn_up=ffn_up, ffn_conv_w=ffn_conv_w, ffn_conv_b=ffn_conv_b,
             ffn_down=ffn_down)
    mom = dict(ada_w=m_ada_w, ada_b=m_ada_b, norm1_w=m_norm1_w, w_in=m_w_in, q_a_norm=m_q_a_norm, w_q_b=m_w_q_b,
               kv_a_norm=m_kv_a_norm, w_kv_b=m_w_kv_b, q_norm=m_q_norm, k_norm=m_k_norm, pool_w=m_pool_w,
               pool_scale=m_pool_scale, ssd_conv_w=m_ssd_conv_w, ssd_conv_b=m_ssd_conv_b, ssd_dt_bias=m_ssd_dt_bias,
               ssd_a_log=m_ssd_a_log, ssd_d=m_ssd_d, ssd_norm_w=m_ssd_norm_w, w_branch=m_w_branch, w_out=m_w_out,
               norm2_w=m_norm2_w, ffn_up=m_ffn_up, ffn_conv_w=m_ffn_conv_w, ffn_conv_b=m_ffn_conv_b, ffn_down=m_ffn_down)
    var = dict(ada_w=v_ada_w, ada_b=v_ada_b, norm1_w=v_norm1_w, w_in=v_w_in, q_a_norm=v_q_a_norm, w_q_b=v_w_q_b,
               kv_a_norm=v_kv_a_norm, w_kv_b=v_w_kv_b, q_norm=v_q_norm, k_norm=v_k_norm, pool_w=v_pool_w,
               pool_scale=v_pool_scale, ssd_conv_w=v_ssd_conv_w, ssd_conv_b=v_ssd_conv_b, ssd_dt_bias=v_ssd_dt_bias,
               ssd_a_log=v_ssd_a_log, ssd_d=v_ssd_d, ssd_norm_w=v_ssd_norm_w, w_branch=v_w_branch, w_out=v_w_out,
               norm2_w=v_norm2_w, ffn_up=v_ffn_up, ffn_conv_w=v_ffn_conv_w, ffn_conv_b=v_ffn_conv_b, ffn_down=v_ffn_down)
    names = list(p)
    me = 4 * lax.axis_index("x") + 2 * lax.axis_index("y") + lax.axis_index("c")
    xs, tgt = x[0], loss_target[0]
    s = xs.shape[0]

    inv_freq = ROPE_THETA ** (-jnp.arange(0, 32, 2, dtype=F32) / 32.0)
    ang = positions[0].astype(F32)[:, None] * inv_freq
    cos, sin = jnp.cos(ang), jnp.sin(ang)
    cos2 = _padc(jnp.concatenate([cos, cos], axis=1), 128)
    sin2 = _padc(jnp.concatenate([-sin, sin], axis=1), 128)

    conv_shards = jnp.concatenate([ssd_conv_w.reshape(-1), ffn_conv_w.reshape(-1)])
    (c_all, conv_all), _ = all_to_all([c, conv_shards], [True, True], "gather_c")
    modp, cact = ada_mod(jnp.pad(c_all.reshape(NDEV, D), ((0, 8), (0, 0))), ada_w)
    (mod_in,), tok = all_to_all([modp[:, 0:NDEV].transpose(1, 0, 2)], [False], "scatter_mod")
    mod = mod_in.transpose(1, 0, 2).reshape(LAYERS, 6 * D) + ada_b

    n1 = LAYERS * 4 * 192
    scw = conv_all[:, :n1].reshape(NDEV, LAYERS, 4, 192).transpose(1, 2, 0, 3).reshape(LAYERS, 4, 1536)
    fcw = conv_all[:, n1:].reshape(NDEV, LAYERS, 3, 704).transpose(1, 2, 0, 3).reshape(LAYERS, 3, 2 * FFN)

    def weights_a(gathered, l):
        full = {n: _gathered_full(g, n) for n, g in zip(GROUP_A[1:], gathered[1:])}
        lw = {n: p[n][l] for n in names}
        lw.update(win=_win_layout(gathered[0]), wq=_wq_layout(full["w_q_b"]), wkv=_wkv_layout(full["w_kv_b"]),
                  ssd_conv_w=scw[l], ffn_conv_w=fcw[l],
                  mla_vec=_mla_vec(lw["q_a_norm"], lw["kv_a_norm"], lw["q_norm"], lw["k_norm"]))
        return lw

    def weights_b(gathered):
        full = {n: _gathered_full(g, n) for n, g in zip(GROUP_B, gathered)}
        wb = full["w_branch"]
        return dict(wba=_wba_layout(wb[0:512]), wbb=wb[512:1024], wbc=wb[1024:2048], wout=full["w_out"],
                    wup=full["ffn_up"], wdn=full["ffn_down"])

    shards = lambda group, l: [(p[n][l].T if n in TRANSPOSED else p[n][l]).astype(BF16) for n in group]
    lws, saved = [None] * LAYERS, [None] * LAYERS
    st, tok = gather_start(_behind(shards(GROUP_A, 0), tok), "gather_a0")
    got, tok = gather_finish(st, tok, "gather_a0")
    h = xs
    for l in range(LAYERS):
        st, tok = gather_start(_behind(shards(GROUP_B, l), tok), f"gather_b{l}")
        lws[l] = weights_a(got, l)
        saved[l] = _fwd_a(h, lws[l], mod[l], cos2, sin2, l, tok)
        got, tok = gather_finish(st, saved[l]["yt"], f"gather_b{l}")
        lws[l].update(weights_b(got))
        if l + 1 < LAYERS:
            st, tok = gather_start(_behind(shards(GROUP_A, l + 1), tok), f"gather_a{l + 1}")
        h = _fwd_b(saved[l], lws[l], l, tok)
        if l + 1 < LAYERS:
            got, tok = gather_finish(st, h, f"gather_a{l + 1}")
    dx, lpart = loss_head(h, tgt)
    loss = lax.psum(lpart[0, 0], ("x", "y", "c"))
    tok = tok + loss * 0.0

    small, parts, packs = [None] * LAYERS, {n: [None] * LAYERS for n in BIG}, [None] * LAYERS
    st = None

    def scatter(grads, group, l, tok, extra=None):
        arrs, flags = [_to_shards(grads[n], n) for n in group], [False] * len(group)
        if extra is not None:
            arrs, flags = arrs + [extra], flags + [True]
        return exchange_start(_behind(arrs, tok), flags, f"scatter_{group[0]}{l}_start")

    def landed(state, group, l, after):
        got, tok, _ = exchange_wait(state, after, f"scatter_{group[0]}{l}_wait")
        for n, g in zip(group, got):
            parts[n][l] = g
        return got, tok

    for l in reversed(range(LAYERS)):
        dup, g_ffn, small[l] = _bwd_ffn(dx, lws[l], saved[l], l, tok)
        if st is not None:
            _, tok = landed(st, GROUP_A, l + 1, dup)
        st, tok = scatter(g_ffn, SCATTER_FFN, l, tok)
        dx, cot, g_merge, small[l] = _bwd_merge(dx, dup, lws[l], saved[l], l, tok, small[l])
        _, tok = landed(st, SCATTER_FFN, l, dx)
        st, tok = scatter(g_merge, SCATTER_MERGE, l, tok, _pack(small[l + 1]) if l + 1 < LAYERS else None)
        dproj, g_in, small[l] = _bwd_a(dx, cot, lws[l], saved[l], cos2, sin2, l, tok, small[l])
        got, tok = landed(st, SCATTER_MERGE, l, dproj)
        if l + 1 < LAYERS:
            packs[l + 1] = got[-1]
        st, tok = scatter(g_in, GROUP_A, l, tok)
        dx, small[l] = _bwd_in(dx, dproj, lws[l], saved[l], l, tok, small[l])

    dmod = jnp.stack([small[q]["ada_b"] for q in range(LAYERS)])
    st_small, tok = exchange_start(_behind([_pack(small[0]), dmod.reshape(LAYERS, NDEV, 768).transpose(1, 0, 2)], tok),
                                   [True, False], "scatter_s0_start")
    out = {}

    def big_adamw(group, tok):
        res = None
        for n in group:
            t = (lambda a: a.transpose(0, 2, 1)) if n in TRANSPOSED else (lambda a: a)
            res = adamw(parts[n], t(p[n]), t(mom[n]), t(var[n]), f"adamw_{n}", tok)
            out[n] = [t(a) for a in res]
        return res[0]

    g_last = big_adamw(GROUP_B, tok)
    _, tok = landed(st, GROUP_A, 0, g_last)
    (packs[0], dmod_in), _, _ = exchange_wait(st_small, tok, "scatter_s0_wait")
    big_adamw(GROUP_A, None)

    dmod16 = jnp.pad(dmod_in, ((0, 8), (0, 0), (0, 0)))
    g_ada = [tn_matmul(cact, dmod16[:, l], f"dw_ada{l}", out_dtype=F32)[None] for l in range(LAYERS)]
    out["ada_w"] = adamw(g_ada, ada_w, m_ada_w, v_ada_w, "adamw_ada_w")

    for n, pt in _unpack_parts(packs).items():
        if n in SHARDED_SMALL:
            w = SHARDED_SMALL[n]
            pt = lax.dynamic_slice_in_dim(pt, me * w, w, axis=2)
        r, c = pt.shape[1:]
        res = adamw([pt], p[n].reshape(1, r, c), mom[n].reshape(1, r, c), var[n].reshape(1, r, c), f"adamw_{n}")
        out[n] = [a.reshape(p[n].shape) for a in res]

    outs = [loss, dx[None]]
    for q in range(4):
        outs += [out[n][q] for n in names]
    return tuple(outs)
```

```python
import functools
import math

import jax
import jax.numpy as jnp
from jax import lax
from jax.experimental import pallas as pl
from jax.experimental.pallas import tpu as pltpu

F32, BF16 = jnp.float32, jnp.bfloat16
EPS = 1e-6
D = 1024
NDEV = 8
LAYERS = 2
HEADS = 8
FFN = 2816
FFN_TILE = 1408
FFN_NT = FFN // FFN_TILE
ATT_SCALE = 96 ** -0.5
ROPE_THETA = 10000.0
LR, B1, B2, ADAM_EPS, WD, STEP = 0.001, 0.9, 0.999, 1e-08, 0.01, 10

O_G, O_XS, O_Z, O_PU, O_BC, O_CKV, O_KR, O_KRS, O_DT, O_QL = 0, 3072, 4096, 5120, 5632, 6144, 6400, 6528, 6656, 6912
NPROJ = 7296
CONST = dict(pipeline_mode=pl.Buffered(1))


def _pick(n, cap, mult=128):
    if n <= cap:
        return n
    best = None
    for t in range(mult, cap + 1, mult):
        if n % t == 0:
            best = t
    assert best is not None, (n, cap, mult)
    return best


def _sig(x):
    return 1.0 / (1.0 + jnp.exp(-x))


def _rms(x, w, n):
    return x * lax.rsqrt(jnp.sum(x * x, axis=-1, keepdims=True) / n + EPS) * w


def _raw(a, b, dims):
    return lax.dot_general(a.astype(BF16), b.astype(BF16), dims, preferred_element_type=F32)


_NN = (((1,), (0,)), ((), ()))
_NT = (((1,), (1,)), ((), ()))
_TN = (((0,), (0,)), ((), ()))
_BNN = (((2,), (1,)), ((0,), (0,)))
_BNT = (((2,), (2,)), ((0,), (0,)))
_BTN = (((1,), (1,)), ((0,), (0,)))


@jax.custom_vjp
def mm_nn(a, b):
    return _raw(a, b, _NN)


mm_nn.defvjp(lambda a, b: (_raw(a, b, _NN), (a, b)),
             lambda r, g: (_raw(g, r[1], _NT), _raw(r[0], g, _TN)))


@jax.custom_vjp
def mm_nc(a, b):
    return _raw(a, b, _NN)


mm_nc.defvjp(lambda a, b: (_raw(a, b, _NN), b),
             lambda b, g: (_raw(g, b, _NT), jnp.zeros_like(b)))


@jax.custom_vjp
def mm_nt(a, b):
    return _raw(a, b, _NT)


mm_nt.defvjp(lambda a, b: (_raw(a, b, _NT), (a, b)),
             lambda r, g: (_raw(g, r[1], _NN), _raw(g, r[0], _TN)))


@jax.custom_vjp
def bmm_nn(a, b):
    return _raw(a, b, _BNN)


bmm_nn.defvjp(lambda a, b: (_raw(a, b, _BNN), (a, b)),
              lambda r, g: (_raw(g, r[1], _BNT), _raw(r[0], g, _BTN)))


@jax.custom_vjp
def bmm_nt(a, b):
    return _raw(a, b, _BNT)


bmm_nt.defvjp(lambda a, b: (_raw(a, b, _BNT), (a, b)),
              lambda r, g: (_raw(g, r[1], _BNN), _raw(g, r[0], _BTN)))


@jax.custom_vjp
def softplus(x):
    t = jnp.exp(-jnp.abs(x))
    u = 1.0 + t
    one = u == 1.0
    l1p = jnp.where(one, t, jnp.log(u) * (t / jnp.where(one, 1.0, u - 1.0)))
    return jnp.maximum(x, 0.0) + l1p


softplus.defvjp(lambda x: (softplus(x), x), lambda x, g: (g * _sig(x),))


def _params(*sem):
    return pltpu.CompilerParams(dimension_semantics=sem, vmem_limit_bytes=56 * 1024 * 1024)


def all_to_all(arrs, bcast, name):
    n = len(arrs)
    out_shapes = [jax.ShapeDtypeStruct(((NDEV,) + a.shape) if b else a.shape, a.dtype) for a, b in zip(arrs, bcast)]

    def body(*refs):
        ins, outs, token = refs[:n], refs[n:2 * n], refs[2 * n]
        send_sems, recv_sems, local_sems = refs[2 * n + 1:]
        me, remote = _exchange_copies(ins, outs, bcast, send_sems, recv_sems)
        local = [pltpu.make_async_copy(ins[j] if bcast[j] else ins[j].at[me], outs[j].at[me], local_sems.at[j])
                 for j in range(n)]
        for cp in local + remote:
            cp.start()
        for cp in remote + local:
            cp.wait()
        token[...] = jnp.zeros_like(token)

    any_spec = pl.BlockSpec(memory_space=pl.ANY)
    res = pl.pallas_call(
        body, name=name, out_shape=out_shapes + [jax.ShapeDtypeStruct((8, 128), F32)], in_specs=[any_spec] * n,
        out_specs=[any_spec] * n + [pl.BlockSpec(memory_space=pltpu.VMEM)],
        scratch_shapes=[pltpu.SemaphoreType.DMA((7 * n,)), pltpu.SemaphoreType.DMA((7 * n,)),
                        pltpu.SemaphoreType.DMA((n,))],
        compiler_params=pltpu.CompilerParams(has_side_effects=True),
    )(*arrs)
    return res[:n], res[n]


def _peers():
    x, y, c = lax.axis_index("x"), lax.axis_index("y"), lax.axis_index("c")
    out = []
    for k in range(1, NDEV):
        px, py, pc = x ^ ((k >> 2) & 1), y ^ ((k >> 1) & 1), c ^ (k & 1)
        out.append(((px, py, pc), 4 * px + 2 * py + pc))
    return 4 * x + 2 * y + c, out


COPIES = {"all": 7, "chips": 3, "pass": 4}


def _exchange_copies(ins, lands, bcast, send_sems, recv_sems, mode="all"):
    x, y, c = lax.axis_index("x"), lax.axis_index("y"), lax.axis_index("c")
    me = 4 * x + 2 * y + c
    n, copies = len(ins), []

    def add(q, j, src, dst, dev):
        copies.append(pltpu.make_async_remote_copy(
            src_ref=src, dst_ref=dst, send_sem=send_sems.at[q * n + j], recv_sem=recv_sems.at[q * n + j],
            device_id=dev, device_id_type=pl.DeviceIdType.MESH))

    if mode == "pass":
        for q in range(4):
            slot = 4 * (x ^ (q >> 1)) + 2 * (y ^ (q & 1)) + c
            for j in range(n):
                add(q, j, ins[j] if q == 0 else lands[j].at[slot], lands[j].at[slot], (x, y, 1 - c))
        return me, copies
    for q, k in enumerate(range(1, NDEV) if mode == "all" else (2, 4, 6)):
        px, py, pc = x ^ ((k >> 2) & 1), y ^ ((k >> 1) & 1), c ^ (k & 1)
        for j in range(n):
            add(q, j, ins[j] if bcast[j] else ins[j].at[4 * px + 2 * py + pc], lands[j].at[me], (px, py, pc))
    return me, copies


_HBM = pl.BlockSpec(memory_space=pltpu.HBM)
_SEM = pl.BlockSpec(memory_space=pltpu.SEMAPHORE)
_EFFECT = pltpu.SideEffectType.DATAFLOW_SIDE_EFFECTING


def exchange_start(arrs, bcast, name, mode="all", lands=None):
    n, ncp = len(arrs), COPIES[mode] * len(arrs)
    land_shapes = [((NDEV,) + a.shape) if b else a.shape for a, b in zip(arrs, bcast)]
    if lands is None:
        lands = [lax.empty(s_, a.dtype) for s_, a in zip(land_shapes, arrs)]

    def body(*refs):
        in_refs, land_refs = refs[:n], refs[n:2 * n]
        send_sems, recv_sems = refs[2 * n], refs[2 * n + 1]
        token = refs[-1]
        _, copies = _exchange_copies(in_refs, land_refs, bcast, send_sems, recv_sems, mode)
        for cp in copies:
            cp.start()
        token[...] = jnp.zeros_like(token)

    hbm = lambda shp, a: pltpu.HBM(shp, a.dtype)
    res = pl.pallas_call(
        body, name=name,
        out_shape=[pltpu.SemaphoreType.DMA((ncp,)), pltpu.SemaphoreType.DMA((ncp,))]
                  + [hbm(a.shape, a) for a in arrs] + [hbm(s_, a) for s_, a in zip(land_shapes, arrs)]
                  + [jax.ShapeDtypeStruct((8, 128), F32)],
        in_specs=[_HBM] * (2 * n), out_specs=[_SEM, _SEM] + [_HBM] * (2 * n) + [pl.BlockSpec(memory_space=pltpu.VMEM)],
        input_output_aliases={i: 2 + i for i in range(2 * n)},
        compiler_params=pltpu.CompilerParams(has_side_effects=_EFFECT),
    )(*[pltpu.with_memory_space_constraint(a, pltpu.HBM) for a in arrs],
      *[pltpu.with_memory_space_constraint(a, pltpu.HBM) for a in lands])
    return (res[0], res[1], res[2:2 + n], res[2 + n:2 + 2 * n], tuple(bcast), mode), res[-1]


def exchange_wait(state, after, name):
    send_sems, recv_sems, ins, lands, bcast, mode = state
    n = len(ins)

    def body(*refs):
        in_refs, land_refs = refs[:n], refs[n:2 * n]
        s_sems, r_sems = refs[2 * n], refs[2 * n + 1]
        token = refs[-1]
        _, copies = _exchange_copies(in_refs, land_refs, bcast, s_sems, r_sems, mode)
        for cp in copies:
            cp.wait_send()
            cp.wait_recv()
        token[...] = jnp.zeros_like(token)

    res = pl.pallas_call(
        body, name=name,
        out_shape=[pltpu.HBM(a.shape, a.dtype) for a in ins] + [pltpu.HBM(a.shape, a.dtype) for a in lands]
                  + [jax.ShapeDtypeStruct((8, 128), F32)],
        in_specs=[_HBM] * (2 * n) + [_SEM, _SEM, pl.BlockSpec(memory_space=pl.ANY)],
        out_specs=[_HBM] * (2 * n) + [pl.BlockSpec(memory_space=pltpu.VMEM)],
        input_output_aliases={i: i for i in range(2 * n)},
        compiler_params=pltpu.CompilerParams(has_side_effects=_EFFECT),
    )(*ins, *lands, send_sems, recv_sems, after)
    if mode == "chips":
        return list(res[n:2 * n]), res[-1], list(res[:n])
    me = 4 * lax.axis_index("x") + 2 * lax.axis_index("y") + lax.axis_index("c")
    got = []
    for j in range(n):
        own = res[j][None] if bcast[j] else lax.dynamic_index_in_dim(res[j], me, 0, keepdims=True)
        got.append(lax.dynamic_update_slice_in_dim(res[n + j], own, me, axis=0))
    return got, res[-1], list(res[:n])


def gather_start(shards, name):
    return exchange_start(shards, [True] * len(shards), name + "_chips_start", mode="chips")


def gather_pass(state, after, name):
    lands, _, sent = exchange_wait(state, after, name + "_chips_wait")
    return exchange_start(sent, [True] * len(sent), name + "_pass_start", mode="pass", lands=lands)


def gather_finish(state, after, name, passed=False):
    if not passed:
        state, after = gather_pass(state, after, name)
    got, tok, _ = exchange_wait(state, after, name + "_pass_wait")
    return got, tok


def norm_proj_fwd(x, vec, w, name):
    s, n = x.shape[0], w.shape[0]
    tr, tn = _pick(s, 512), _pick(n, 2560)
    ni, jdt, odt = s // tr, O_DT // tn, O_DT % tn

    def body(x_ref, v_ref, w_ref, o_ref, h_ref, dt_ref, h_scr):
        j, i = pl.program_id(0), pl.program_id(1)
        rows = pl.ds(pl.multiple_of(i * tr, tr), tr)

        @pl.when(j == 0)
        def _():
            h = _rms(x_ref[...], v_ref[0:1, :], D) * (1.0 + v_ref[2:3, :]) + v_ref[1:2, :]
            h_scr[rows, :] = h.astype(BF16)
            h_ref[...] = h.astype(BF16)
        res = _raw(h_scr[rows, :], w_ref[...], _NT)
        o_ref[...] = res

        @pl.when(j == jdt)
        def _():
            dt_ref[...] = res[:, odt:odt + 128]

    first = lambda j, i: (jnp.where(j == 0, i, ni - 1), 0)
    dtix = lambda j, i: (jnp.where(j < jdt, 0, jnp.where(j == jdt, i, ni - 1)), 0)
    return pl.pallas_call(
        body, name=name, grid=(n // tn, ni),
        in_specs=[pl.BlockSpec((tr, D), first), pl.BlockSpec((8, D), lambda j, i: (0, 0)),
                  pl.BlockSpec((tn, D), lambda j, i: (j, 0))],
        out_specs=[pl.BlockSpec((tr, tn), lambda j, i: (i, j)), pl.BlockSpec((tr, D), first),
                   pl.BlockSpec((tr, 128), dtix)],
        out_shape=[jax.ShapeDtypeStruct((s, n), F32), jax.ShapeDtypeStruct((s, D), BF16),
                   jax.ShapeDtypeStruct((s, 128), F32)],
        scratch_shapes=[pltpu.VMEM((s, D), BF16)],
        compiler_params=_params("arbitrary", "arbitrary"),
    )(x, vec, w)


def _col_tiles(arr, cap):
    if arr.ndim == 2:
        n = arr.shape[1]
        t = _pick(n, cap)
        return n, t, lambda rows, ix: pl.BlockSpec((rows, t), lambda *g: ix(*g))
    width = arr.shape[2]
    t = _pick(width, cap)
    per = width // t

    def spec(rows, ix):
        def index(*g):
            r, j = ix(*g)
            return (j // per, r, j % per)
        return pl.BlockSpec((None, rows, t), index)
    return arr.shape[0] * width, t, spec


def norm_proj_bwd(x, vec, dp, w, dx_in, aux, name):
    s = x.shape[0]
    tr = _pick(s, 512)
    n, tk, dp_spec = _col_tiles(dp, 2560)
    nk, has_aux = n // tk, aux is not None

    def body(*refs):
        if has_aux:
            x_ref, v_ref, dp_ref, w_ref, dxin_ref, aux_ref, dx_ref, dv_ref, acc = refs
        else:
            x_ref, v_ref, dp_ref, w_ref, dxin_ref, dx_ref, dv_ref, acc = refs
        k, i = pl.program_id(0), pl.program_id(1)
        rows = pl.ds(pl.multiple_of(i * tr, tr), tr)
        part = _raw(dp_ref[...], w_ref[...], _NN)

        @pl.when(k == 0)
        def _():
            acc[rows, :] = part

        @pl.when(k > 0)
        def _():
            acc[rows, :] += part

        @pl.when(k == nk - 1)
        def _():
            f = lambda xx, nw, sh, sc: _rms(xx, nw, D) * (1.0 + sc) + sh
            _, vjp = jax.vjp(f, x_ref[...], v_ref[0:1, :], v_ref[1:2, :], v_ref[2:3, :])
            dx, dnw, dsh, dsc = vjp(acc[rows, :])
            dx_ref[...] = dxin_ref[...] + dx

            @pl.when(i == 0)
            def _():
                dv_ref[...] = jnp.zeros_like(dv_ref)

            dv_ref[0:1, :] += dnw
            dv_ref[1:2, :] += dsh
            dv_ref[2:3, :] += dsc
            if has_aux:
                dv_ref[3:4, :] += jnp.sum(dxin_ref[...] * aux_ref[...], axis=0, keepdims=True)

    row = pl.BlockSpec((tr, D), lambda k, i: (jnp.where(k == nk - 1, i, 0), 0))
    in_specs = [row, pl.BlockSpec((8, D), lambda k, i: (0, 0)), dp_spec(tr, lambda k, i: (i, k)),
                pl.BlockSpec((tk, D), lambda k, i: (k, 0)), row] + ([row] if has_aux else [])
    args = [x, vec, dp, w, dx_in] + ([aux] if has_aux else [])
    return pl.pallas_call(
        body, name=name, grid=(nk, s // tr), in_specs=in_specs,
        out_specs=[row, pl.BlockSpec((8, D), lambda k, i: (0, 0))],
        out_shape=[jax.ShapeDtypeStruct((s, D), F32), jax.ShapeDtypeStruct((8, D), F32)],
        scratch_shapes=[pltpu.VMEM((s, D), F32)],
        compiler_params=_params("arbitrary", "arbitrary"),
    )(*args)


def tn_matmul(a, b, name, scale=None, out_dtype=None):
    out_dtype = BF16 if out_dtype is None else out_dtype
    s = b.shape[-2]
    ts = _pick(s, 512, 16)
    m, tm, a_spec = _col_tiles(a, 2560 if b.shape[-1] <= D else 1408)
    n, tn, b_spec = _col_tiles(b, 2560)
    ns, has_scale = s // ts, scale is not None

    def body(*refs):
        if has_scale:
            a_ref, b_ref, sc_ref, o_ref, acc = refs
        else:
            a_ref, b_ref, o_ref, acc = refs
        k = pl.program_id(2)

        @pl.when(k == 0)
        def _():
            acc[...] = jnp.zeros_like(acc)

        acc[...] += _raw(a_ref[...], b_ref[...], _TN)

        @pl.when(k == ns - 1)
        def _():
            o_ref[...] = (acc[...] * sc_ref[...] if has_scale else acc[...]).astype(out_dtype)

    in_specs = [a_spec(ts, lambda i, j, k: (k, i)), b_spec(ts, lambda i, j, k: (k, j))]
    if has_scale:
        in_specs.append(pl.BlockSpec((1, tn), lambda i, j, k: (0, j)))
    return pl.pallas_call(
        body, name=name, grid=(m // tm, n // tn, ns), in_specs=in_specs,
        out_specs=pl.BlockSpec((tm, tn), lambda i, j, k: (i, j)),
        out_shape=jax.ShapeDtypeStruct((m, n), out_dtype),
        scratch_shapes=[pltpu.VMEM((tm, tn), F32)],
        compiler_params=_params("arbitrary", "arbitrary", "arbitrary"),
    )(*([a, b] + ([scale] if has_scale else [])))


def ada_mod(c16, w):
    ncol = w.shape[2]

    def body(c_ref, w_ref, o_ref, a_ref):
        cc = c_ref[...]
        act = cc * _sig(cc)
        a_ref[...] = act
        o_ref[...] = _raw(act, w_ref[...], _NN)

    return pl.pallas_call(
        body, name="ada_mod", grid=(LAYERS,),
        in_specs=[pl.BlockSpec((16, D), lambda l: (0, 0)), pl.BlockSpec((None, D, ncol), lambda l: (l, 0, 0))],
        out_specs=[pl.BlockSpec((None, 16, ncol), lambda l: (l, 0, 0)), pl.BlockSpec((16, D), lambda l: (0, 0))],
        out_shape=[jax.ShapeDtypeStruct((LAYERS, 16, ncol), F32), jax.ShapeDtypeStruct((16, D), F32)],
        compiler_params=_params("arbitrary"),
    )(c16, w)


def _mla_shared(q_lat, c_kv, kr, krs, qa_w, kva_w, kr_w, krs_w, cos2, sin2):
    qn = _rms(q_lat, qa_w, 384.0)
    kvn = _rms(c_kv, kva_w, 256.0)
    rk = lax.rsqrt(jnp.sum(kr * kr, axis=-1, keepdims=True) / 32.0 + EPS)
    krope = rk * (kr * kr_w * cos2 + krs * krs_w * sin2)
    return qn, kvn, krope


def _mla_head(qn, kvn, wqn, wqr, wqrs, wkn, wv, qn_w, qr_w, qrs_w, kn_w, cos2, sin2):
    qnope = _rms(mm_nn(qn, wqn), qn_w, 64.0)
    qr, qrs = mm_nn(qn, wqr), mm_nn(qn, wqrs)
    rq = lax.rsqrt(jnp.sum(qr * qr, axis=-1, keepdims=True) / 32.0 + EPS)
    qrope = rq * (qr * qr_w * cos2 + qrs * qrs_w * sin2)
    knope = _rms(mm_nn(kvn, wkn), kn_w, 64.0)
    return qnope, qrope, knope, mm_nn(kvn, wv)


def _mla_vec_pieces(v_ref):
    return ((v_ref[0:1, 0:384], v_ref[1:2, 0:256], v_ref[3:4, 128:256], v_ref[3:4, 256:384]),
            (v_ref[2:3, 0:128], v_ref[2:3, 128:256], v_ref[2:3, 256:384], v_ref[3:4, 0:128]))


def _mla_in_specs(tr):
    return [pl.BlockSpec((tr, 384), lambda i: (i, O_QL // 384)), pl.BlockSpec((tr, 256), lambda i: (i, O_CKV // 256)),
            pl.BlockSpec((tr, 128), lambda i: (i, O_KR // 128)), pl.BlockSpec((tr, 128), lambda i: (i, O_KRS // 128)),
            pl.BlockSpec((HEADS, 384, 384), lambda i: (0, 0, 0), **CONST),
            pl.BlockSpec((HEADS, 256, 256), lambda i: (0, 0, 0), **CONST),
            pl.BlockSpec((8, 512), lambda i: (0, 0)),
            pl.BlockSpec((tr, 128), lambda i: (i, 0)), pl.BlockSpec((tr, 128), lambda i: (i, 0))]


def mla_pre_fwd(proj, wq, wkv, vec, cos2, sin2, name):
    s = proj.shape[0]
    tr = _pick(s, 256)

    def body(ql_ref, ckv_ref, kr_ref, krs_ref, wq_ref, wkv_ref, v_ref, cos_ref, sin_ref, q_out, k_out, v_out):
        vshared, vhead = _mla_vec_pieces(v_ref)
        cos2_, sin2_ = cos_ref[...], sin_ref[...]
        qlat_n, kv_n, krope = _mla_shared(ql_ref[...], ckv_ref[...], kr_ref[...], krs_ref[...], *vshared, cos2_, sin2_)
        qlat_n, kv_n, krope = qlat_n.astype(BF16), kv_n.astype(BF16), krope.astype(BF16)
        for h in range(HEADS):
            ws = (wq_ref[h, :, 0:128], wq_ref[h, :, 128:256], wq_ref[h, :, 256:384],
                  wkv_ref[h, :, 0:128], wkv_ref[h, :, 128:256])
            qn, qr, kn, v = _mla_head(qlat_n, kv_n, *ws, *vhead, cos2_, sin2_)
            q_out[h, :, 0:128] = qn.astype(BF16)
            q_out[h, :, 128:256] = qr.astype(BF16)
            k_out[h, :, 0:128] = kn.astype(BF16)
            k_out[h, :, 128:256] = krope
            v_out[h] = v.astype(BF16)

    return pl.pallas_call(
        body, name=name, grid=(s // tr,), in_specs=_mla_in_specs(tr),
        out_specs=[pl.BlockSpec((HEADS, tr, 256), lambda i: (0, i, 0)), pl.BlockSpec((HEADS, tr, 256), lambda i: (0, i, 0)),
                   pl.BlockSpec((HEADS, tr, 128), lambda i: (0, i, 0))],
        out_shape=[jax.ShapeDtypeStruct((HEADS, s, 256), BF16), jax.ShapeDtypeStruct((HEADS, s, 256), BF16),
                   jax.ShapeDtypeStruct((HEADS, s, 128), BF16)],
        compiler_params=_params("arbitrary"),
    )(proj, proj, proj, proj, wq, wkv, vec, cos2, sin2)


def mla_pre_bwd(proj, wq, wkv, vec, cos2, sin2, dq, dk, dv, name):
    s = proj.shape[0]
    tr = _pick(s, 256)

    def body(ql_ref, ckv_ref, kr_ref, krs_ref, wq_ref, wkv_ref, v_ref, cos_ref, sin_ref, dq_ref, dk_ref, dv_ref,
             dql_out, dckv_out, dkr_out, dkrs_out, dwq_out, dwkv_out, dvec_out):
        @pl.when(pl.program_id(0) == 0)
        def _():
            dwq_out[...] = jnp.zeros_like(dwq_out)
            dwkv_out[...] = jnp.zeros_like(dwkv_out)
            dvec_out[...] = jnp.zeros_like(dvec_out)

        vshared, vhead = _mla_vec_pieces(v_ref)
        cos2_, sin2_ = cos_ref[...], sin_ref[...]
        fs = lambda *a: _mla_shared(*a, cos2_, sin2_)
        (qlat_n, kv_n, _), vjp_shared = jax.vjp(fs, ql_ref[...], ckv_ref[...], kr_ref[...], krs_ref[...], *vshared)

        def head(h, carry):
            wq_h, wkv_h = wq_ref[h].astype(F32), wkv_ref[h].astype(F32)
            ws = (wq_h[:, 0:128], wq_h[:, 128:256], wq_h[:, 256:384], wkv_h[:, 0:128], wkv_h[:, 128:256])
            f = lambda *a: _mla_head(*a, cos2_, sin2_)
            _, vjp = jax.vjp(f, qlat_n, kv_n, *ws, *vhead)
            dq_h, dk_h = dq_ref[h], dk_ref[h]
            g = vjp((dq_h[:, 0:128], dq_h[:, 128:256], dk_h[:, 0:128], dv_ref[h]))
            dwq_out[h, :, 0:128] += g[2]
            dwq_out[h, :, 128:256] += g[3]
            dwq_out[h, :, 256:384] += g[4]
            dwkv_out[h, :, 0:128] += g[5]
            dwkv_out[h, :, 128:256] += g[6]
            dvec_out[2:3, 0:128] += g[7]
            dvec_out[2:3, 128:256] += g[8]
            dvec_out[2:3, 256:384] += g[9]
            dvec_out[3:4, 0:128] += g[10]
            return carry[0] + g[0], carry[1] + g[1], carry[2] + dk_h[:, 128:256]

        zero = lambda w: jnp.zeros((tr, w), F32)
        dqn, dkvn, dkrope = lax.fori_loop(0, HEADS, head, (zero(384), zero(256), zero(128)))
        g = vjp_shared((dqn, dkvn, dkrope))
        dql_out[...] = g[0].astype(BF16)
        dckv_out[...] = g[1].astype(BF16)
        dkr_out[...] = g[2].astype(BF16)
        dkrs_out[...] = g[3].astype(BF16)
        dvec_out[0:1, 0:384] += g[4]
        dvec_out[1:2, 0:256] += g[5]
        dvec_out[3:4, 128:256] += g[6]
        dvec_out[3:4, 256:384] += g[7]

    hb = lambda w: pl.BlockSpec((HEADS, tr, w), lambda i: (0, i, 0))
    return pl.pallas_call(
        body, name=name, grid=(s // tr,), in_specs=_mla_in_specs(tr) + [hb(256), hb(256), hb(128)],
        out_specs=[pl.BlockSpec((tr, 384), lambda i: (i, 0)), pl.BlockSpec((tr, 256), lambda i: (i, 0)),
                   pl.BlockSpec((tr, 128), lambda i: (i, 0)), pl.BlockSpec((tr, 128), lambda i: (i, 0)),
                   pl.BlockSpec((HEADS, 384, 384), lambda i: (0, 0, 0)), pl.BlockSpec((HEADS, 256, 256), lambda i: (0, 0, 0)),
                   pl.BlockSpec((8, 512), lambda i: (0, 0))],
        out_shape=[jax.ShapeDtypeStruct((s, 384), BF16), jax.ShapeDtypeStruct((s, 256), BF16),
                   jax.ShapeDtypeStruct((s, 128), BF16), jax.ShapeDtypeStruct((s, 128), BF16),
                   jax.ShapeDtypeStruct((HEADS, 384, 384), F32), jax.ShapeDtypeStruct((HEADS, 256, 256), F32),
                   jax.ShapeDtypeStruct((8, 512), F32)],
        compiler_params=_params("arbitrary"),
    )(proj, proj, proj, proj, wq, wkv, vec, cos2, sin2, dq, dk, dv)


def _att_probs(q, kk, i, tq):
    sc = _raw(q, kk, _NT) * ATT_SCALE
    rows = lax.broadcasted_iota(jnp.int32, sc.shape, 0) + i * tq
    cols = lax.broadcasted_iota(jnp.int32, sc.shape, 1)
    sc = jnp.where(cols <= rows, sc, -jnp.inf)
    e = jnp.exp(sc - jnp.max(sc, axis=-1, keepdims=True))
    return e / jnp.sum(e, axis=-1, keepdims=True)


def mla_attn_fwd(q, k, v, name):
    s = q.shape[1]
    tq = _pick(s, 256)

    def body(q_ref, k_ref, v_ref, o_ref):
        for i in range(s // tq):
            n = (i + 1) * tq
            p = _att_probs(q_ref[i * tq:n, :], k_ref[0:n, :], i, tq)
            o_ref[i * tq:n, :] = _raw(p, v_ref[0:n, :], _NN)

    hs = lambda w: pl.BlockSpec((None, s, w), lambda h: (h, 0, 0))
    return pl.pallas_call(
        body, name=name, grid=(HEADS,), in_specs=[hs(256), hs(256), hs(128)],
        out_specs=pl.BlockSpec((s, 128), lambda h: (0, h)),
        out_shape=jax.ShapeDtypeStruct((s, HEADS * 128), F32),
        compiler_params=_params("arbitrary"),
    )(q, k, v)


def mla_attn_bwd(q, k, v, do, name):
    s = q.shape[1]
    tq = _pick(s, 256)

    def body(q_ref, k_ref, v_ref, do_ref, dq_ref, dk_ref, dv_ref):
        dk_ref[...] = jnp.zeros_like(dk_ref)
        dv_ref[...] = jnp.zeros_like(dv_ref)
        for i in range(s // tq):
            n = (i + 1) * tq
            qq, kk, vv = q_ref[i * tq:n, :], k_ref[0:n, :], v_ref[0:n, :]
            p = _att_probs(qq, kk, i, tq)
            o = _raw(p, vv, _NN)
            dout = do_ref[i * tq:n, :]
            delta = jnp.sum(dout * o, axis=-1, keepdims=True)
            dp = _raw(dout, vv, _NT)
            ds = p * (dp - delta) * ATT_SCALE
            dq_ref[i * tq:n, :] = _raw(ds, kk, _NN)
            dk_ref[0:n, :] += _raw(ds, qq, _TN)
            dv_ref[0:n, :] += _raw(p, dout, _TN)

    hs = lambda w: pl.BlockSpec((None, s, w), lambda h: (h, 0, 0))
    return pl.pallas_call(
        body, name=name, grid=(HEADS,),
        in_specs=[hs(256), hs(256), hs(128), pl.BlockSpec((s, 128), lambda h: (0, h))],
        out_specs=[hs(256), hs(256), hs(128)],
        out_shape=[jax.ShapeDtypeStruct((HEADS, s, 256), F32), jax.ShapeDtypeStruct((HEADS, s, 256), F32),
                   jax.ShapeDtypeStruct((HEADS, s, 128), F32)],
        compiler_params=_params("arbitrary"),
    )(q, k, v, do)


def _pool_windows(u, pad, s, g):
    pad[0:16, :] = jnp.zeros((16, 128), F32)
    cur, sel = u, None
    for j, k in enumerate((1, 2, 4, 8)):
        pad[16:16 + s, :] = cur
        cur = cur + pad[16 - k:16 - k + s, :]
        sel = cur if sel is None else jnp.where(g == j, cur, sel)
    return sel


def _pool_count(s, g):
    t = lax.broadcasted_iota(jnp.int32, (s, 1), 0)
    return jnp.minimum(t + 1, 2 << g).astype(F32)


def pool_fwd(proj, pw, ps, name):
    s = proj.shape[0]

    def body(u_ref, w_ref, s_ref, o_ref, pad):
        g = pl.program_id(0)
        u = u_ref[...]
        pooled = _pool_windows(u, pad, s, g) / _pool_count(s, g) - u
        o_ref[...] = _raw(pooled, w_ref[...], _NN) * s_ref[...]

    return pl.pallas_call(
        body, name=name, grid=(4,),
        in_specs=[pl.BlockSpec((s, 128), lambda g: (0, O_PU // 128 + g)), pl.BlockSpec((None, 128, 128), lambda g: (g, 0, 0)),
                  pl.BlockSpec((1, 128), lambda g: (0, g))],
        out_specs=pl.BlockSpec((s, 128), lambda g: (0, g)),
        out_shape=jax.ShapeDtypeStruct((s, 512), F32),
        scratch_shapes=[pltpu.VMEM((s + 16, 128), F32)],
        compiler_params=_params("arbitrary"),
    )(proj, pw, ps)


def pool_bwd(proj, pw, ps, do, name):
    s = proj.shape[0]

    def body(u_ref, w_ref, s_ref, do_ref, du_ref, dw_ref, ds_ref, pad):
        g = pl.program_id(0)
        u, w, dout = u_ref[...], w_ref[...], do_ref[...]
        cnt = _pool_count(s, g)
        pooled = _pool_windows(u, pad, s, g) / cnt - u
        mixed = _raw(pooled, w, _NN)
        ds_ref[...] = jnp.sum(dout * mixed, axis=0, keepdims=True)
        dmixed = dout * s_ref[...]
        dw_ref[...] = _raw(pooled, dmixed, _TN)
        dpooled = _raw(dmixed, w, _NT)
        dsel = dpooled / cnt
        pad[s:s + 16, :] = jnp.zeros((16, 128), F32)
        cur = jnp.where(g == 3, dsel, 0.0)
        for j, k in ((2, 8), (1, 4), (0, 2)):
            pad[0:s, :] = cur
            cur = cur + pad[k:k + s, :] + jnp.where(g == j, dsel, 0.0)
        pad[0:s, :] = cur
        cur = cur + pad[1:1 + s, :]
        du_ref[...] = (cur - dpooled).astype(BF16)

    return pl.pallas_call(
        body, name=name, grid=(4,),
        in_specs=[pl.BlockSpec((s, 128), lambda g: (0, O_PU // 128 + g)), pl.BlockSpec((None, 128, 128), lambda g: (g, 0, 0)),
                  pl.BlockSpec((1, 128), lambda g: (0, g)), pl.BlockSpec((s, 128), lambda g: (0, g))],
        out_specs=[pl.BlockSpec((s, 128), lambda g: (0, g)), pl.BlockSpec((None, 128, 128), lambda g: (g, 0, 0)),
                   pl.BlockSpec((1, 128), lambda g: (0, g))],
        out_shape=[jax.ShapeDtypeStruct((s, 512), BF16), jax.ShapeDtypeStruct((4, 128, 128), F32),
                   jax.ShapeDtypeStruct((1, 512), F32)],
        scratch_shapes=[pltpu.VMEM((s + 16, 128), F32)],
        compiler_params=_params("arbitrary"),
    )(proj, pw, ps, do)


def _xbc_col(i):
    return jnp.where(i < 2, O_XS // 512 + i, O_BC // 512)


def conv_fwd(proj, cw, cb, name):
    s = proj.shape[0]

    def body(x_ref, w_ref, b_ref, o_ref, t_ref, pad):
        pad[0:8, :] = jnp.zeros((8, 512), F32)
        pad[8:8 + s, :] = x_ref[...]
        y = b_ref[...] + sum(w_ref[k:k + 1, :] * pad[5 + k:5 + k + s, :] for k in range(4))
        act = y * _sig(y)
        o_ref[...] = act

        @pl.when(pl.program_id(0) < 2)
        def _():
            t_ref[...] = act.T

    return pl.pallas_call(
        body, name=name, grid=(3,),
        in_specs=[pl.BlockSpec((s, 512), lambda i: (0, _xbc_col(i))), pl.BlockSpec((4, 512), lambda i: (0, i)),
                  pl.BlockSpec((1, 512), lambda i: (0, i))],
        out_specs=[pl.BlockSpec((s, 512), lambda i: (0, i)), pl.BlockSpec((512, s), lambda i: (jnp.minimum(i, 1), 0))],
        out_shape=[jax.ShapeDtypeStruct((s, 1536), F32), jax.ShapeDtypeStruct((D, s), F32)],
        scratch_shapes=[pltpu.VMEM((s + 8, 512), F32)],
        compiler_params=_params("arbitrary"),
    )(proj, cw, cb)


def conv_bwd(proj, cw, cb, dxt, dbm, dcm, name):
    s = proj.shape[0]

    def body(x_ref, w_ref, b_ref, dxt_ref, dbm_ref, dcm_ref, dx_ref, dw_ref, db_ref, pad, pad2):
        pad[0:8, :] = jnp.zeros((8, 512), F32)
        pad[8:8 + s, :] = x_ref[...]
        y = b_ref[...] + sum(w_ref[k:k + 1, :] * pad[5 + k:5 + k + s, :] for k in range(4))
        sg = _sig(y)

        @pl.when(pl.program_id(0) < 2)
        def _():
            pad2[0:s, :] = dxt_ref[...].T

        @pl.when(pl.program_id(0) == 2)
        def _():
            pad2[0:s, 0:256] = dbm_ref[...]
            pad2[0:s, 256:512] = dcm_ref[...]

        dy = pad2[0:s, :] * (sg * (1.0 + y * (1.0 - sg)))
        db_ref[...] = jnp.sum(dy, axis=0, keepdims=True)
        for k in range(4):
            dw_ref[k:k + 1, :] = jnp.sum(dy * pad[5 + k:5 + k + s, :], axis=0, keepdims=True)
        pad2[s:s + 8, :] = jnp.zeros((8, 512), F32)
        pad2[0:s, :] = dy
        dx_ref[...] = sum(w_ref[k:k + 1, :] * pad2[3 - k:3 - k + s, :] for k in range(4)).astype(BF16)

    return pl.pallas_call(
        body, name=name, grid=(3,),
        in_specs=[pl.BlockSpec((s, 512), lambda i: (0, _xbc_col(i))), pl.BlockSpec((4, 512), lambda i: (0, i)),
                  pl.BlockSpec((1, 512), lambda i: (0, i)), pl.BlockSpec((512, s), lambda i: (jnp.minimum(i, 1), 0)),
                  pl.BlockSpec((s, 256), lambda i: (0, 0)), pl.BlockSpec((s, 256), lambda i: (0, 0))],
        out_specs=[pl.BlockSpec((s, 512), lambda i: (0, i)), pl.BlockSpec((4, 512), lambda i: (0, i)),
                   pl.BlockSpec((1, 512), lambda i: (0, i))],
        out_shape=[jax.ShapeDtypeStruct((s, 1536), BF16), jax.ShapeDtypeStruct((4, 1536), F32),
                   jax.ShapeDtypeStruct((1, 1536), F32)],
        scratch_shapes=[pltpu.VMEM((s + 8, 512), F32), pltpu.VMEM((s + 8, 512), F32)],
        compiler_params=_params("arbitrary"),
    )(proj, cw, cb, dxt, dbm, dcm)


def _ssd_chunk(xt, dtr, bm, cm, hprev, alog, dbias, dskip):
    ln = 128
    a = -jnp.exp(alog)
    dt_r = softplus(dtr + dbias)
    da_r = dt_r * a
    li = lax.broadcasted_iota(jnp.int32, (1, ln, ln), 1)
    si = lax.broadcasted_iota(jnp.int32, (1, ln, ln), 2)
    causal = si <= li
    acs_c = jnp.sum(jnp.where(causal, da_r, 0.0), axis=2, keepdims=True)
    acs_r = jnp.sum(jnp.where(li == si, acs_c, 0.0), axis=1, keepdims=True)
    acs_last = jnp.sum(da_r, axis=2, keepdims=True)
    decay = jnp.exp(jnp.where(causal, acs_c - acs_r, -jnp.inf))
    m = mm_nt(cm, bm)[None] * decay
    xdt = xt * dt_r
    y_diag = bmm_nt(xdt, m)
    bb = jnp.broadcast_to(bm[None], (8, ln, ln))
    cc = jnp.broadcast_to(cm[None], (8, ln, ln))
    states = bmm_nn(xdt * jnp.exp(acs_last - acs_r), bb)
    y_off = bmm_nt(hprev, cc) * jnp.exp(acs_r)
    hnew = hprev * jnp.exp(acs_last) + states
    return y_diag + y_off + xt * dskip, hnew


def _ssd_specs(nc, rev):
    cix = (lambda c: nc - 1 - c) if rev else (lambda c: c)
    hv = pl.BlockSpec((8, 1, 1), lambda g, c: (g, 0, 0))
    return [pl.BlockSpec((8, 64, 128), lambda g, c: (g, 0, cix(c))), pl.BlockSpec((8, 1, 128), lambda g, c: (g, 0, cix(c))),
            pl.BlockSpec((128, 128), lambda g, c: (cix(c), 8 + g)),
            pl.BlockSpec((128, 128), lambda g, c: (cix(c), 10 + g))], hv, cix


def ssd_fwd(xt, dtr, xbc, alog, dbias, dskip, name):
    s = xt.shape[2]
    nc = s // 128
    specs, hv, _ = _ssd_specs(nc, False)

    def body(x_ref, dr_ref, b_ref, c_ref, al_ref, db_ref, dk_ref, y_ref, hs_ref, h_scr):
        @pl.when(pl.program_id(1) == 0)
        def _():
            h_scr[...] = jnp.zeros_like(h_scr)
        hp = h_scr[...]
        hs_ref[...] = hp
        y, hn = _ssd_chunk(x_ref[...], dr_ref[...], b_ref[...], c_ref[...], hp, al_ref[...], db_ref[...], dk_ref[...])
        y_ref[...] = y
        h_scr[...] = hn

    return pl.pallas_call(
        body, name=name, grid=(2, nc), in_specs=specs + [hv, hv, hv],
        out_specs=[pl.BlockSpec((8, 64, 128), lambda g, c: (g, 0, c)),
                   pl.BlockSpec((None, None, 8, 64, 128), lambda g, c: (g, c, 0, 0, 0))],
        out_shape=[jax.ShapeDtypeStruct((16, 64, s), F32), jax.ShapeDtypeStruct((2, nc, 8, 64, 128), F32)],
        scratch_shapes=[pltpu.VMEM((8, 64, 128), F32)],
        compiler_params=_params("arbitrary", "arbitrary"),
    )(xt, dtr, xbc, xbc, alog, dbias, dskip)


def ssd_bwd(xt, dtr, xbc, alog, dbias, dskip, hs, dyt, name):
    s = xt.shape[2]
    nc = s // 128
    specs, hv, cix = _ssd_specs(nc, True)

    def body(x_ref, dr_ref, b_ref, c_ref, al_ref, db_ref, dk_ref, hs_ref, dy_ref,
             dx_out, ddr_out, dbm_out, dcm_out, dal_out, ddb_out, ddk_out, dh_scr):
        @pl.when(pl.program_id(1) == 0)
        def _():
            dh_scr[...] = jnp.zeros_like(dh_scr)
            dal_out[...] = jnp.zeros_like(dal_out)
            ddb_out[...] = jnp.zeros_like(ddb_out)
            ddk_out[...] = jnp.zeros_like(ddk_out)
        _, vjp = jax.vjp(_ssd_chunk, x_ref[...], dr_ref[...], b_ref[...], c_ref[...], hs_ref[...],
                         al_ref[...], db_ref[...], dk_ref[...])
        g = vjp((dy_ref[...], dh_scr[...]))
        dx_out[...] = g[0]
        ddr_out[...] = g[1]
        dbm_out[...] = g[2]
        dcm_out[...] = g[3]
        dh_scr[...] = g[4]
        dal_out[...] += g[5]
        ddb_out[...] += g[6]
        ddk_out[...] += g[7]

    return pl.pallas_call(
        body, name=name, grid=(2, nc),
        in_specs=specs + [hv, hv, hv, pl.BlockSpec((None, None, 8, 64, 128), lambda g, c: (g, cix(c), 0, 0, 0)),
                          pl.BlockSpec((8, 64, 128), lambda g, c: (g, 0, cix(c)))],
        out_specs=[pl.BlockSpec((8, 64, 128), lambda g, c: (g, 0, cix(c))), pl.BlockSpec((8, 1, 128), lambda g, c: (g, 0, cix(c))),
                   pl.BlockSpec((128, 128), lambda g, c: (cix(c), g)),
                   pl.BlockSpec((128, 128), lambda g, c: (cix(c), g)), hv, hv, hv],
        out_shape=[jax.ShapeDtypeStruct((16, 64, s), F32), jax.ShapeDtypeStruct((16, 1, s), F32),
                   jax.ShapeDtypeStruct((s, 256), F32),
                   jax.ShapeDtypeStruct((s, 256), F32)] + [jax.ShapeDtypeStruct((16, 1, 1), F32)] * 3,
        scratch_shapes=[pltpu.VMEM((8, 64, 128), F32)],
        compiler_params=_params("arbitrary", "arbitrary"),
    )(xt, dtr, xbc, xbc, alog, dbias, dskip, hs, dyt)


def _merge(oa, ob, y, z, gla, glb, glc, x, g1, nw, ea, eb, ec, eo, wba, wbb, wbc, wout):
    gated = y * (z * _sig(z))
    sq = gated * gated
    left = lax.broadcasted_iota(jnp.int32, (1, D), 1) < 512
    ms0 = jnp.sum(jnp.where(left, sq, 0.0), axis=-1, keepdims=True) / 512.0
    ms1 = jnp.sum(jnp.where(left, 0.0, sq), axis=-1, keepdims=True) / 512.0
    oc = gated * jnp.where(left, lax.rsqrt(ms0 + EPS), lax.rsqrt(ms1 + EPS)) * nw
    ya, yb, yc = mm_nc(oa, wba) + ea, mm_nc(ob, wbb) + eb, mm_nc(oc, wbc) + ec
    merged = _sig(gla) * ya + _sig(glb) * yb + _sig(glc) * yc
    x1 = x + g1 * (mm_nc(merged, wout) + eo)
    return x1, (oc, merged)


def _merge_specs(tr):
    row = lambda w: pl.BlockSpec((tr, w), lambda i: (i, 0))
    acts = [row(D), row(512), pl.BlockSpec((D, tr), lambda i: (0, i)), pl.BlockSpec((tr, D), lambda i: (i, O_Z // D)),
            pl.BlockSpec((tr, 3 * D), lambda i: (i, 0)), row(D), pl.BlockSpec((8, D), lambda i: (0, 0))]
    cst = lambda r: pl.BlockSpec((r, D), lambda i: (0, 0), **CONST)
    return acts, [cst(D), cst(512), cst(D), cst(D)], row


def merge_fwd(oa, ob, y, proj, x, mvec, wba, wbb, wbc, wout, name):
    s = x.shape[0]
    tr = _pick(s, 256)
    acts, wts, row = _merge_specs(tr)

    def body(oa_ref, ob_ref, y_ref, z_ref, gl_ref, x_ref, mv_ref, wba_ref, wbb_ref, wbc_ref, wout_ref, o_ref):
        zero = jnp.zeros((1, D), F32)
        x1, _ = _merge(oa_ref[...], ob_ref[...], y_ref[...].T, z_ref[...], gl_ref[:, 0:D], gl_ref[:, D:2 * D],
                       gl_ref[:, 2 * D:3 * D], x_ref[...], mv_ref[0:1, :], mv_ref[1:2, :], zero, zero, zero, zero,
                       wba_ref[...], wbb_ref[...], wbc_ref[...], wout_ref[...])
        o_ref[...] = x1

    return pl.pallas_call(
        body, name=name, grid=(s // tr,), in_specs=acts + wts, out_specs=row(D),
        out_shape=jax.ShapeDtypeStruct((s, D), F32), compiler_params=_params("arbitrary"),
    )(oa, ob, y, proj, proj, x, mvec, wba, wbb, wbc, wout)


def merge_bwd(oa, ob, y, proj, x, mvec, wba, wbb, wbc, wout, dx1, name):
    s = x.shape[0]
    tr = _pick(s, 128)
    acts, wts, row = _merge_specs(tr)

    def body(oa_ref, ob_ref, y_ref, z_ref, gl_ref, x_ref, mv_ref, wba_ref, wbb_ref, wbc_ref, wout_ref, dx1_ref,
             doa_o, dob_o, dy_o, dz_o, dgl_o, dx_o, dmv_o, dya_o, dyb_o, dyc_o, dpre_o, oc_o, mg_o):
        zero = jnp.zeros((tr, D), F32)
        wts_ = (wba_ref[...], wbb_ref[...], wbc_ref[...], wout_ref[...])
        f = lambda *a: _merge(*a, *wts_)
        _, vjp, (oc, merged) = jax.vjp(
            f, oa_ref[...], ob_ref[...], y_ref[...].T, z_ref[...], gl_ref[:, 0:D], gl_ref[:, D:2 * D],
            gl_ref[:, 2 * D:3 * D], x_ref[...], mv_ref[0:1, :], mv_ref[1:2, :], zero, zero, zero, zero, has_aux=True)
        g = vjp(dx1_ref[...])
        doa_o[...] = g[0]
        dob_o[...] = g[1]
        dy_o[...] = g[2].T
        dz_o[...] = g[3].astype(BF16)
        dgl_o[:, 0:D] = g[4].astype(BF16)
        dgl_o[:, D:2 * D] = g[5].astype(BF16)
        dgl_o[:, 2 * D:3 * D] = g[6].astype(BF16)
        dx_o[...] = g[7]

        @pl.when(pl.program_id(0) == 0)
        def _():
            dmv_o[...] = jnp.zeros_like(dmv_o)

        dmv_o[0:1, :] += g[8]
        dmv_o[1:2, :] += g[9]
        dya_o[...] = g[10].astype(BF16)
        dyb_o[...] = g[11].astype(BF16)
        dyc_o[...] = g[12].astype(BF16)
        dpre_o[...] = g[13].astype(BF16)
        oc_o[...] = oc.astype(BF16)
        mg_o[...] = merged.astype(BF16)

    sd = lambda w, dt: jax.ShapeDtypeStruct((s, w), dt)
    return pl.pallas_call(
        body, name=name, grid=(s // tr,), in_specs=acts + wts + [row(D)],
        out_specs=[row(D), row(512), pl.BlockSpec((D, tr), lambda i: (0, i)), row(D), row(3 * D), row(D),
                   pl.BlockSpec((8, D), lambda i: (0, 0))] + [row(D)] * 6,
        out_shape=[sd(D, F32), sd(512, F32), jax.ShapeDtypeStruct((D, s), F32), sd(D, BF16), sd(3 * D, BF16), sd(D, F32),
                   jax.ShapeDtypeStruct((8, D), F32)] + [sd(D, BF16)] * 6,
        compiler_params=_params("arbitrary"),
    )(oa, ob, y, proj, proj, x, mvec, wba, wbb, wbc, wout, dx1)


def _conv3(u_scr, w_ref, first, rows, lanes):
    return sum(w_ref[k:k + 1, :] * u_scr[first + k:first + k + rows, lanes] for k in range(3))


def _ffn_tile_specs(tf, tile):
    def at(rows, off):
        return pl.BlockSpec((rows, tf), lambda *g: (0, off + tile(*g)))

    def wt(off):
        return pl.BlockSpec((tf, D), lambda *g: (off + tile(*g), 0))
    return [wt(0), wt(FFN_NT), at(3, 0), at(3, FFN_NT), at(1, 0), at(1, FFN_NT)]


def ffn_fwd(x1, fvec, wup, cw, cb, wdn, name):
    s = x1.shape[0]
    tr, tf = _pick(s, 512), FFN_TILE
    lg, lv = slice(0, tf), slice(tf, 2 * tf)

    def body(x_ref, v_ref, wg_ref, wv_ref, cwg_ref, cwv_ref, cbg_ref, cbv_ref, wd_ref, x2_ref, h_ref, pre_ref,
             h_scr, u_scr, acc):
        i, t = pl.program_id(0), pl.program_id(1)

        @pl.when(t == 0)
        def _():
            @pl.when(i == 0)
            def _():
                h_scr[0:16, :] = jnp.zeros((16, D), BF16)

            @pl.when(i > 0)
            def _():
                h_scr[0:16, :] = h_scr[tr:tr + 16, :]

            h = (_rms(x_ref[...], v_ref[0:1, :], D) * (1.0 + v_ref[2:3, :]) + v_ref[1:2, :]).astype(BF16)
            h_scr[16:16 + tr, :] = h
            h_ref[...] = h
            acc[...] = jnp.zeros_like(acc)

        u_scr[:, lg] = _raw(h_scr[...], wg_ref[...], _NT)
        u_scr[:, lv] = _raw(h_scr[...], wv_ref[...], _NT)
        cg = _conv3(u_scr, cwg_ref, 14, tr, lg) + cbg_ref[...]
        cval = _conv3(u_scr, cwv_ref, 14, tr, lv) + cbv_ref[...]
        acc[...] += _raw(cg * _sig(cg) * cval, wd_ref[...], _NN)

        @pl.when(t == FFN_NT - 1)
        def _():
            pre_ref[...] = acc[...]
            x2_ref[...] = x_ref[...] + v_ref[3:4, :] * acc[...]

    row = pl.BlockSpec((tr, D), lambda i, t: (i, 0))
    return pl.pallas_call(
        body, name=name, grid=(s // tr, FFN_NT),
        in_specs=[row, pl.BlockSpec((8, D), lambda i, t: (0, 0))] + _ffn_tile_specs(tf, lambda i, t: t)
                 + [pl.BlockSpec((tf, D), lambda i, t: (t, 0))],
        out_specs=[row, row, row],
        out_shape=[jax.ShapeDtypeStruct((s, D), F32), jax.ShapeDtypeStruct((s, D), BF16), jax.ShapeDtypeStruct((s, D), F32)],
        scratch_shapes=[pltpu.VMEM((tr + 16, D), BF16), pltpu.VMEM((tr + 16, 2 * tf), F32), pltpu.VMEM((tr, D), F32)],
        compiler_params=_params("arbitrary", "arbitrary"),
    )(x1, fvec, wup, wup, cw, cw, cb, cb, wdn)


def ffn_bwd(h2, dx2, fvec, wup, cw, cb, wdn, name):
    s = h2.shape[0]
    tr, tf = _pick(s, 512), FFN_TILE
    ni, nb = s // tr, s // 16
    lg, lv = slice(0, tf), slice(tf, 2 * tf)

    def body(hp_ref, hm_ref, hn_ref, dm_ref, dn_ref, v_ref, wg_ref, wv_ref, cwg_ref, cwv_ref, cbg_ref, cbv_ref, wd_ref,
             dup_ref, act_ref, dcw_ref, u_scr, dc_scr):
        i = pl.program_id(1)
        hfull = jnp.concatenate([jnp.where(i > 0, hp_ref[...], jnp.zeros((16, D), BF16)), hm_ref[...],
                                 jnp.where(i < ni - 1, hn_ref[...], jnp.zeros((16, D), BF16))], axis=0)
        u_scr[:, lg] = _raw(hfull, wg_ref[...], _NT)
        u_scr[:, lv] = _raw(hfull, wv_ref[...], _NT)
        cg = _conv3(u_scr, cwg_ref, 14, tr + 16, lg) + cbg_ref[...]
        cval = _conv3(u_scr, cwv_ref, 14, tr + 16, lv) + cbv_ref[...]
        g2 = v_ref[3:4, :]
        dpre = jnp.concatenate([dm_ref[...] * g2, jnp.where(i < ni - 1, dn_ref[...], 0.0) * g2], axis=0)
        dact = _raw(dpre, wd_ref[...], _NT)
        sg = _sig(cg)
        sl = cg * sg
        dc_scr[:, lg] = dact * cval * (sg * (1.0 + cg * (1.0 - sg)))
        dc_scr[:, lv] = dact * sl
        act_ref[...] = (sl * cval)[0:tr, :].astype(BF16)

        @pl.when(i == 0)
        def _():
            dcw_ref[...] = jnp.zeros_like(dcw_ref)

        for half, lanes, cw_ref in ((0, lg, cwg_ref), (1, lv, cwv_ref)):
            dup_ref[half] = sum(cw_ref[k:k + 1, :] * dc_scr[2 - k:2 - k + tr, lanes] for k in range(3)).astype(BF16)
            dcm = dc_scr[0:tr, lanes]
            for k in range(3):
                dcw_ref[half, k:k + 1, :] += jnp.sum(dcm * u_scr[14 + k:14 + k + tr, lanes], axis=0, keepdims=True)
            dcw_ref[half, 3:4, :] += jnp.sum(dcm, axis=0, keepdims=True)

    r16 = tr // 16
    prev = lambda t, i: (jnp.maximum(i * r16 - 1, 0), 0)
    nxt = lambda t, i: (jnp.minimum((i + 1) * r16, nb - 1), 0)
    main = lambda t, i: (i, 0)
    return pl.pallas_call(
        body, name=name, grid=(FFN_NT, ni),
        in_specs=[pl.BlockSpec((16, D), prev), pl.BlockSpec((tr, D), main), pl.BlockSpec((16, D), nxt),
                  pl.BlockSpec((tr, D), main), pl.BlockSpec((16, D), nxt), pl.BlockSpec((8, D), lambda t, i: (0, 0))]
                 + _ffn_tile_specs(tf, lambda t, i: t) + [pl.BlockSpec((tf, D), lambda t, i: (t, 0))],
        out_specs=[pl.BlockSpec((2, tr, tf), lambda t, i: (0, i, t)), pl.BlockSpec((tr, tf), lambda t, i: (i, t)),
                   pl.BlockSpec((2, 8, tf), lambda t, i: (0, 0, t))],
        out_shape=[jax.ShapeDtypeStruct((2, s, FFN), BF16), jax.ShapeDtypeStruct((s, FFN), BF16),
                   jax.ShapeDtypeStruct((2, 8, FFN), F32)],
        scratch_shapes=[pltpu.VMEM((tr + 32, 2 * tf), F32), pltpu.VMEM((tr + 16, 2 * tf), F32)],
        compiler_params=_params("arbitrary", "arbitrary"),
    )(h2, h2, h2, dx2, dx2, fvec, wup, wup, cw, cw, cb, cb, wdn)


def loss_head(y, target):
    s = y.shape[0]
    tr = _pick(s, 512)

    def body(y_ref, t_ref, dx_ref, l_ref):
        @pl.when(pl.program_id(0) == 0)
        def _():
            l_ref[...] = jnp.zeros_like(l_ref)
        err = y_ref[...] - t_ref[...]
        dx_ref[...] = err / float(D)
        l_ref[...] += 0.5 * jnp.sum(jnp.sum(err * err, axis=-1, keepdims=True) / float(D), axis=0, keepdims=True)

    row = pl.BlockSpec((tr, D), lambda i: (i, 0))
    return pl.pallas_call(
        body, name="loss_head", grid=(s // tr,), in_specs=[row, row],
        out_specs=[row, pl.BlockSpec((8, 128), lambda i: (0, 0))],
        out_shape=[jax.ShapeDtypeStruct((s, D), F32), jax.ShapeDtypeStruct((8, 128), F32)],
        compiler_params=_params("arbitrary"),
    )(y, target)


def adamw(parts, w, m, v, name, tok=None):
    nseg = len(parts)
    p, r, c = parts[0].shape
    tok = jnp.zeros((8, 128), F32) if tok is None else tok
    cap = 256 if c > 128 else 2048
    step = lambda q, l, i, ni: jnp.clip((l - q) * ni + i, 0, ni - 1)
    if r <= cap or any(r % t == 0 for t in range(8, cap + 1, 8)):
        tr, tc = _pick(r, cap, 8), c
        ni = r // tr
        row = pl.BlockSpec((None, tr, tc), lambda l, i: (l, i, 0))
        part = lambda q: pl.BlockSpec((p, tr, tc), lambda l, i: (0, step(q, l, i, ni), 0))
    else:
        tr, tc = r, _pick(c, 256)
        ni = c // tc
        row = pl.BlockSpec((None, tr, tc), lambda l, i: (l, 0, i))
        part = lambda q: pl.BlockSpec((p, tr, tc), lambda l, i: (0, 0, step(q, l, i, ni)))

    def body(*refs):
        p_refs = refs[:nseg]
        w_ref, m_ref, v_ref, _, g_out, d_out, m_out, v_out, g_scr = refs[nseg:]
        for q in range(nseg):
            @pl.when(pl.program_id(0) == q)
            def _(q=q):
                g = p_refs[q][0].astype(F32)
                for j in range(1, p):
                    g = g + p_refs[q][j].astype(F32)
                g_scr[...] = g
        g = g_scr[...]
        mn = B1 * m_ref[...] + (1.0 - B1) * g
        vn = B2 * v_ref[...] + (1.0 - B2) * (g * g)
        m_hat = mn / (1.0 - B1 ** STEP)
        v_hat = vn / (1.0 - B2 ** STEP)
        g_out[...] = g
        d_out[...] = -LR * (m_hat / (jnp.sqrt(v_hat) + ADAM_EPS) + WD * w_ref[...])
        m_out[...] = mn
        v_out[...] = vn

    return pl.pallas_call(
        body, name=name, grid=(nseg, ni),
        in_specs=[part(q) for q in range(nseg)] + [row, row, row, pl.BlockSpec((8, 128), lambda l, i: (0, 0))],
        out_specs=[row] * 4, out_shape=[jax.ShapeDtypeStruct((nseg, r, c), F32)] * 4,
        scratch_shapes=[pltpu.VMEM((tr, tc), F32)],
        compiler_params=_params("arbitrary", "arbitrary"),
    )(*parts, w, m, v, tok)


def _padc(a, n):
    return jnp.pad(a, [(0, 0)] * (a.ndim - 1) + [(0, n - a.shape[-1])])


def _swap16(a):
    return jnp.concatenate([a[..., 16:32], a[..., 0:16]], axis=-1)


def _shard_cols(g8, a, b):
    c = g8.shape[2]
    return [g8[j][:, max(a, j * c) - j * c:min(b, (j + 1) * c) - j * c] for j in range(a // c, (b - 1) // c + 1)]


def _padr(a, n):
    return jnp.pad(a, ((0, n - a.shape[0]), (0, 0)))


def _swap16r(a):
    return jnp.concatenate([a[16:32], a[0:16]], axis=0)


def _win_layout(g8):
    w = g8.reshape(NDEV * g8.shape[1], g8.shape[2])
    kr = w[640:672]
    return jnp.concatenate([w[3760:6832], w[2208:3232], w[1184:2208], w[672:1184], w[3232:3744], w[384:640],
                            _padr(kr, 128), _padr(_swap16r(kr), 128), _padr(w[3744:3760], 128),
                            jnp.zeros((128, w.shape[1]), w.dtype), w[0:384]], axis=0)


def _win_grad_shards(g):
    kr = (g[O_KR:O_KR + 32].astype(F32) + _swap16r(g[O_KRS:O_KRS + 32].astype(F32))).astype(g.dtype)
    segs = [(g, O_QL, 384), (g, O_CKV, 256), (kr, 0, 32), (g, O_PU, 512), (g, O_Z, D), (g, O_XS, D), (g, O_BC, 512),
            (g, O_DT, 16), (g, O_G, 3 * D)]
    shards, height = [], sum(w for _, _, w in segs) // NDEV
    for j in range(NDEV):
        a, b, off, pieces = height * j, height * (j + 1), 0, []
        for arr, lo, w in segs:
            s0, s1 = max(a, off), min(b, off + w)
            if s0 < s1:
                pieces.append(arr[lo + s0 - off:lo + s1 - off])
            off += w
        shards.append(jnp.concatenate(pieces, axis=0))
    return jnp.stack(shards).astype(BF16)


def _wq_layout(w):
    w = w.reshape(384, HEADS, 96).transpose(1, 0, 2)
    rope = w[:, :, 64:96]
    return jnp.concatenate([_padc(w[:, :, 0:64], 128), _padc(rope, 128), _padc(_swap16(rope), 128)], axis=2)


def _wq_unlayout(g):
    rope = g[:, :, 128:160] + _swap16(g[:, :, 256:288])
    return jnp.concatenate([g[:, :, 0:64], rope], axis=2).transpose(1, 0, 2).reshape(384, HEADS * 96)


def _wkv_layout(w):
    w = w.reshape(256, HEADS, 128).transpose(1, 0, 2)
    return jnp.concatenate([_padc(w[:, :, 0:64], 128), _padc(w[:, :, 64:128], 128)], axis=2)


def _wkv_unlayout(g):
    return jnp.concatenate([g[:, :, 0:64], g[:, :, 128:192]], axis=2).transpose(1, 0, 2).reshape(256, HEADS * 128)


def _wba_layout(w):
    return jnp.pad(w.reshape(HEADS, 64, D), ((0, 0), (0, 64), (0, 0))).reshape(HEADS * 128, D)


def _rows8(rows, width):
    out = jnp.stack([_padc(r.astype(F32), width) for r in rows])
    return jnp.pad(out, ((0, 8 - out.shape[0]), (0, 0)))


def _mla_vec(qa, kva, qn, kn):
    def row(n):
        return jnp.concatenate([_padc(n[0:64], 128), _padc(n[64:96], 128), _padc(_swap16(n[64:96]), 128)])
    return _rows8([qa, kva, row(qn), row(kn)], 512)


def _mla_unvec(g):
    def un(r):
        return jnp.concatenate([r[0:64], r[128:160] + _swap16(r[256:288])])
    return g[0, 0:384], g[1, 0:256], un(g[2]), un(g[3])


SMALL = (("ada_b", (6 * D,)), ("norm1_w", (D,)), ("q_a_norm", (384,)), ("kv_a_norm", (256,)), ("q_norm", (96,)),
         ("k_norm", (96,)), ("pool_w", (4, 128, 128)), ("pool_scale", (512,)), ("ssd_conv_b", (1536,)),
         ("ssd_dt_bias", (16,)), ("ssd_a_log", (16,)), ("ssd_d", (16,)), ("ssd_norm_w", (D,)), ("norm2_w", (D,)),
         ("ffn_conv_b", (2 * FFN,)), ("ssd_conv_w", (4, 1536)), ("ffn_conv_w", (3, 2 * FFN)))
SHARDED_SMALL = {"ssd_conv_w": 192, "ffn_conv_w": 704}


def _pack_rows(shp):
    return -(-math.prod(shp) // 1024) * 8


def _pack(small):
    pieces = []
    for n, shp in SMALL:
        pieces.append(small[n].reshape(-1).astype(F32))
        fill = _pack_rows(shp) * 128 - math.prod(shp)
        if fill:
            pieces.append(jnp.zeros((fill,), F32))
    return jnp.concatenate(pieces).reshape(-1, 128)


def _unpack_parts(packs):
    out, off = {}, 0
    for n, shp in SMALL:
        rows, size = _pack_rows(shp), math.prod(shp)
        r, c = math.prod(shp[:-1]), shp[-1]
        per_layer = [pk[:, off:off + rows].reshape(NDEV, rows * 128)[:, 0:size].reshape(NDEV, r, c) for pk in packs]
        out[n] = jnp.concatenate(per_layer, axis=1)
        off += rows
    return out


GROUP_A = ("w_in", "w_q_b", "w_kv_b")
GROUP_B = ("w_branch", "w_out", "ffn_up", "ffn_down")
BIG = GROUP_A + GROUP_B
SCATTER_FFN, SCATTER_MERGE = ("ffn_up", "ffn_down"), ("w_branch", "w_out")
COL_SHARDED = ("w_q_b", "w_kv_b")
TRANSPOSED = ("w_in", "ffn_up")


def _behind(arrs, tok):
    arrs = list(arrs)
    j = min(range(len(arrs)), key=lambda q: arrs[q].size)
    arrs[j] = arrs[j] + tok[0, 0].astype(arrs[j].dtype)
    return arrs


def _gathered_full(g, name):
    if name in COL_SHARDED:
        return g.transpose(1, 0, 2).reshape(g.shape[1], NDEV * g.shape[2])
    return g.reshape(NDEV * g.shape[1], g.shape[2])


def _to_shards(full, name):
    if name == "w_in":
        return _win_grad_shards(full)
    if name in COL_SHARDED:
        r, c = full.shape
        return full.reshape(r, NDEV, c // NDEV).transpose(1, 0, 2).astype(BF16)
    r, c = full.shape
    return full.reshape(NDEV, r // NDEV, c).astype(BF16)


def _fwd_a(x, lw, mod, cos2, sin2, l, tok, before_scan):
    sh1, sc1, g1, sh2, sc2, g2 = [mod[j * D:(j + 1) * D] for j in range(6)]
    vec1 = _rows8([lw["norm1_w"], sh1, sc1], D) + tok[0, 0]
    proj, h1, dt_cols = norm_proj_fwd(x, vec1, lw["win"], f"inproj_fwd{l}")
    q, k, v = mla_pre_fwd(proj, lw["wq"], lw["wkv"], lw["mla_vec"], cos2, sin2, f"mla_pre_fwd{l}")
    oa = mla_attn_fwd(q, k, v, f"mla_attn_fwd{l}")
    ob = pool_fwd(proj, lw["pool_w"], lw["pool_scale"].reshape(1, 512), f"pool_fwd{l}")
    xbc, xt = conv_fwd(proj, lw["ssd_conv_w"], lw["ssd_conv_b"].reshape(1, 1536), f"conv_fwd{l}")
    s = x.shape[0]
    xt = xt.reshape(16, 64, s)
    dt = dt_cols[:, 0:16].T
    dtr = dt[:, None, :]
    hv = lambda a: a.reshape(16, 1, 1)
    tok = before_scan(xbc)
    yt, hs = ssd_fwd(xt, dtr, xbc, hv(lw["ssd_a_log"]) + tok[0, 0], hv(lw["ssd_dt_bias"]), hv(lw["ssd_d"]),
                     f"ssd_fwd{l}")
    return dict(x=x, vec1=vec1, proj=proj, h1=h1, q=q, k=k, v=v, oa=oa, ob=ob, xbc=xbc, xt=xt, dtr=dtr,
                hs=hs, yt=yt.reshape(D, s), mvec=_rows8([g1, lw["ssd_norm_w"]], D),
                fvec=_rows8([lw["norm2_w"], sh2, sc2, g2], D))


def _fwd_b(sv, lw, l, tok):
    sv["mvec"] = sv["mvec"] + tok[0, 0]
    x1 = merge_fwd(sv["oa"], sv["ob"], sv["yt"], sv["proj"], sv["x"], sv["mvec"], lw["wba"], lw["wbb"], lw["wbc"],
                   lw["wout"], f"merge_fwd{l}")
    x2, h2, pre = ffn_fwd(x1, sv["fvec"], lw["wup"], lw["ffn_conv_w"], lw["ffn_conv_b"].reshape(1, 2 * FFN), lw["wdn"],
                          f"ffn_fwd{l}")
    sv.update(x1=x1, h2=h2, pre=pre)
    return x2


def _bwd_ffn(dx2, lw, sv, l, tok):
    fvec = sv["fvec"] + tok[0, 0]
    dup, act, dcw = ffn_bwd(sv["h2"], dx2, fvec, lw["wup"], lw["ffn_conv_w"], lw["ffn_conv_b"].reshape(1, 2 * FFN),
                            lw["wdn"], f"ffn_bwd{l}")
    grads = dict(ffn_down=tn_matmul(act, dx2, f"dw_down{l}", scale=fvec[3:4]),
                 ffn_up=tn_matmul(dup, sv["h2"], f"dw_up{l}"))
    small = dict(ffn_conv_w=jnp.concatenate([dcw[0, 0:3], dcw[1, 0:3]], axis=1),
                 ffn_conv_b=jnp.concatenate([dcw[0, 3], dcw[1, 3]]))
    return dup, grads, small


def _bwd_merge(dx2, dup, lw, sv, l, tok, small):
    grads = {}
    fvec = sv["fvec"] + tok[0, 0]
    dx1, dfvec = norm_proj_bwd(sv["x1"], fvec, dup, lw["wup"], dx2, sv["pre"], f"ffn_norm_bwd{l}")
    small["norm2_w"] = dfvec[0]
    (doa, dob, dyt, dz, dgl, dx, dmvec, dya, dyb, dyc, dpre, oc, merged) = merge_bwd(
        sv["oa"], sv["ob"], sv["yt"], sv["proj"], sv["x"], sv["mvec"], lw["wba"], lw["wbb"], lw["wbc"], lw["wout"], dx1,
        f"merge_bwd{l}")
    dwba = tn_matmul(sv["oa"], dya, f"dw_ba{l}").reshape(HEADS, 128, D)[:, 0:64].reshape(512, D)
    grads["w_branch"] = jnp.concatenate([dwba, tn_matmul(sv["ob"], dyb, f"dw_bb{l}"), tn_matmul(oc, dyc, f"dw_bc{l}")])
    grads["w_out"] = tn_matmul(merged, dpre, f"dw_out{l}")
    small["ssd_norm_w"] = dmvec[1]
    small["dmod_b"] = (dmvec[0], dfvec[1], dfvec[2], dfvec[3])
    return dx, dict(doa=doa, dob=dob, dyt=dyt, dz=dz, dgl=dgl), grads, small


def _bwd_a(dx, cot, lw, sv, cos2, sin2, l, tok, small):
    s = dx.shape[0]
    grads = {}
    doa, dob, dz, dgl = cot["doa"], cot["dob"], cot["dz"], cot["dgl"]
    hv = lambda a: a.reshape(16, 1, 1)
    dxt, ddtr, dbm, dcm, dal, ddb, ddk = ssd_bwd(
        sv["xt"], sv["dtr"], sv["xbc"], hv(lw["ssd_a_log"]) + tok[0, 0], hv(lw["ssd_dt_bias"]),
        hv(lw["ssd_d"]), sv["hs"], cot["dyt"].reshape(16, 64, s), f"ssd_bwd{l}")
    small["ssd_a_log"], small["ssd_dt_bias"], small["ssd_d"] = dal.reshape(16), ddb.reshape(16), ddk.reshape(16)
    dxbc, dscw, dscb = conv_bwd(sv["proj"], lw["ssd_conv_w"], lw["ssd_conv_b"].reshape(1, 1536), dxt.reshape(D, s),
                                dbm, dcm, f"conv_bwd{l}")
    small["ssd_conv_w"], small["ssd_conv_b"] = dscw, dscb.reshape(1536)
    ddt = ddtr[:, 0, :].T
    du, dpw, dps = pool_bwd(sv["proj"], lw["pool_w"], lw["pool_scale"].reshape(1, 512), dob, f"pool_bwd{l}")
    small["pool_w"], small["pool_scale"] = dpw, dps.reshape(512)
    dq, dk, dv = mla_attn_bwd(sv["q"], sv["k"], sv["v"], doa, f"mla_attn_bwd{l}")
    dql, dckv, dkr, dkrs, dwq, dwkv, dmv = mla_pre_bwd(sv["proj"], lw["wq"], lw["wkv"], lw["mla_vec"], cos2, sin2,
                                                       dq, dk, dv, f"mla_pre_bwd{l}")
    grads["w_q_b"], grads["w_kv_b"] = _wq_unlayout(dwq), _wkv_unlayout(dwkv)
    small["q_a_norm"], small["kv_a_norm"], small["q_norm"], small["k_norm"] = _mla_unvec(dmv)
    dproj = jnp.concatenate([dgl, dxbc[:, 0:D], dz, du, dxbc[:, D:1536], dckv, dkr, dkrs,
                             _padc(ddt, 128).astype(BF16), jnp.zeros((s, 128), BF16), dql], axis=1)
    grads["w_in"] = tn_matmul(dproj, sv["h1"], f"dw_in{l}")
    return dproj, grads, small


def _bwd_in(dx, dproj, lw, sv, l, tok, small):
    dx0, dvec1 = norm_proj_bwd(sv["x"], sv["vec1"] + tok[0, 0], dproj, lw["win"], dx, None, f"inproj_bwd{l}")
    small["norm1_w"] = dvec1[0]
    small["ada_b"] = jnp.concatenate([dvec1[1], dvec1[2], *small.pop("dmod_b")])
    return dx0, small


def kernel(x, c, positions, ada_w, ada_b, norm1_w, w_in, q_a_norm, w_q_b, kv_a_norm, w_kv_b, q_norm, k_norm, pool_w, pool_scale, ssd_conv_w, ssd_conv_b, ssd_dt_bias, ssd_a_log, ssd_d, ssd_norm_w, w_branch, w_out, norm2_w, ffn_up, ffn_conv_w, ffn_conv_b, ffn_down, loss_target, m_ada_w, m_ada_b, m_norm1_w, m_w_in, m_q_a_norm, m_w_q_b, m_kv_a_norm, m_w_kv_b, m_q_norm, m_k_norm, m_pool_w, m_pool_scale, m_ssd_conv_w, m_ssd_conv_b, m_ssd_dt_bias, m_ssd_a_log, m_ssd_d, m_ssd_norm_w, m_w_branch, m_w_out, m_norm2_w, m_ffn_up, m_ffn_conv_w, m_ffn_conv_b, m_ffn_down, v_ada_w, v_ada_b, v_norm1_w, v_w_in, v_q_a_norm, v_w_q_b, v_kv_a_norm, v_w_kv_b, v_q_norm, v_k_norm, v_pool_w, v_pool_scale, v_ssd_conv_w, v_ssd_conv_b, v_ssd_dt_bias, v_ssd_a_log, v_ssd_d, v_ssd_norm_w, v_w_branch, v_w_out, v_norm2_w, v_ffn_up, v_ffn_conv_w, v_ffn_conv_b, v_ffn_down):
    p = dict(ada_w=ada_w, ada_b=ada_b, norm1_w=norm1_w, w_in=w_in, q_a_norm=q_a_norm, w_q_b=w_q_b, kv_a_norm=kv_a_norm,
             w_kv_b=w_kv_b, q_norm=q_norm, k_norm=k_norm, pool_w=pool_w, pool_scale=pool_scale, ssd_conv_w=ssd_conv_w,
             ssd_conv_b=ssd_conv_b, ssd_dt_bias=ssd_dt_bias, ssd_a_log=ssd_a_log, ssd_d=ssd_d, ssd_norm_w=ssd_norm_w,
             w_branch=w_branch, w_out=w_out, norm2_w=norm2_w, ffn_up=ffn_up, ffn_conv_w=ffn_conv_w, ffn_conv_b=ffn_conv_b,
             ffn_down=ffn_down)
    mom = dict(ada_w=m_ada_w, ada_b=m_ada_b, norm1_w=m_norm1_w, w_in=m_w_in, q_a_norm=m_q_a_norm, w_q_b=m_w_q_b,
               kv_a_norm=m_kv_a_norm, w_kv_b=m_w_kv_b, q_norm=m_q_norm, k_norm=m_k_norm, pool_w=m_pool_w,
               pool_scale=m_pool_scale, ssd_conv_w=m_ssd_conv_w, ssd_conv_b=m_ssd_conv_b, ssd_dt_bias=m_ssd_dt_bias,
               ssd_a_log=m_ssd_a_log, ssd_d=m_ssd_d, ssd_norm_w=m_ssd_norm_w, w_branch=m_w_branch, w_out=m_w_out,
               norm2_w=m_norm2_w, ffn_up=m_ffn_up, ffn_conv_w=m_ffn_conv_w, ffn_conv_b=m_ffn_conv_b, ffn_down=m_ffn_down)
    var = dict(ada_w=v_ada_w, ada_b=v_ada_b, norm1_w=v_norm1_w, w_in=v_w_in, q_a_norm=v_q_a_norm, w_q_b=v_w_q_b,
               kv_a_norm=v_kv_a_norm, w_kv_b=v_w_kv_b, q_norm=v_q_norm, k_norm=v_k_norm, pool_w=v_pool_w,
               pool_scale=v_pool_scale, ssd_conv_w=v_ssd_conv_w, ssd_conv_b=v_ssd_conv_b, ssd_dt_bias=v_ssd_dt_bias,
               ssd_a_log=v_ssd_a_log, ssd_d=v_ssd_d, ssd_norm_w=v_ssd_norm_w, w_branch=v_w_branch, w_out=v_w_out,
               norm2_w=v_norm2_w, ffn_up=v_ffn_up, ffn_conv_w=v_ffn_conv_w, ffn_conv_b=v_ffn_conv_b, ffn_down=v_ffn_down)
    names = list(p)
    me = 4 * lax.axis_index("x") + 2 * lax.axis_index("y") + lax.axis_index("c")
    xs, tgt = x[0], loss_target[0]
    s = xs.shape[0]

    inv_freq = ROPE_THETA ** (-jnp.arange(0, 32, 2, dtype=F32) / 32.0)
    ang = positions[0].astype(F32)[:, None] * inv_freq
    cos, sin = jnp.cos(ang), jnp.sin(ang)
    cos2 = _padc(jnp.concatenate([cos, cos], axis=1), 128)
    sin2 = _padc(jnp.concatenate([-sin, sin], axis=1), 128)

    shards = lambda group, l: [(p[n][l].T if n in TRANSPOSED else p[n][l]).astype(BF16) for n in group]
    st_first, tok = gather_start(shards(GROUP_A, 0), "gather_a0")

    conv_shards = jnp.concatenate([ssd_conv_w.reshape(-1), ffn_conv_w.reshape(-1)])
    (c_all, conv_all), _ = all_to_all([c + tok[0, 0], conv_shards], [True, True], "gather_c")
    modp, cact = ada_mod(jnp.pad(c_all.reshape(NDEV, D), ((0, 8), (0, 0))), ada_w)
    (mod_in,), tok = all_to_all([modp[:, 0:NDEV].transpose(1, 0, 2)], [False], "scatter_mod")
    mod = mod_in.transpose(1, 0, 2).reshape(LAYERS, 6 * D) + ada_b

    n1 = LAYERS * 4 * 192
    scw = conv_all[:, :n1].reshape(NDEV, LAYERS, 4, 192).transpose(1, 2, 0, 3).reshape(LAYERS, 4, 1536)
    fcw = conv_all[:, n1:].reshape(NDEV, LAYERS, 3, 704).transpose(1, 2, 0, 3).reshape(LAYERS, 3, 2 * FFN)

    def weights_a(gathered, l):
        full = {n: _gathered_full(g, n) for n, g in zip(GROUP_A[1:], gathered[1:])}
        lw = {n: p[n][l] for n in names}
        lw.update(win=_win_layout(gathered[0]), wq=_wq_layout(full["w_q_b"]), wkv=_wkv_layout(full["w_kv_b"]),
                  ssd_conv_w=scw[l], ffn_conv_w=fcw[l],
                  mla_vec=_mla_vec(lw["q_a_norm"], lw["kv_a_norm"], lw["q_norm"], lw["k_norm"]))
        return lw

    def weights_b(gathered):
        full = {n: _gathered_full(g, n) for n, g in zip(GROUP_B, gathered)}
        wb = full["w_branch"]
        return dict(wba=_wba_layout(wb[0:512]), wbb=wb[512:1024], wbc=wb[1024:2048], wout=full["w_out"],
                    wup=full["ffn_up"], wdn=full["ffn_down"])

    lws, saved = [None] * LAYERS, [None] * LAYERS
    got, tok = gather_finish(st_first, tok, "gather_a0")
    h = xs
    for l in range(LAYERS):
        st, tok = gather_start(_behind(shards(GROUP_B, l), tok), f"gather_b{l}")
        lws[l] = weights_a(got, l)
        passing = {}

        def before_scan(after, st=st, l=l, passing=passing):
            passing["state"], t = gather_pass(st, after, f"gather_b{l}")
            return t

        saved[l] = _fwd_a(h, lws[l], mod[l], cos2, sin2, l, tok, before_scan)
        got, tok = gather_finish(passing["state"], saved[l]["yt"], f"gather_b{l}", passed=True)
        lws[l].update(weights_b(got))
        if l + 1 < LAYERS:
            st, tok = gather_start(_behind(shards(GROUP_A, l + 1), tok), f"gather_a{l + 1}")
        h = _fwd_b(saved[l], lws[l], l, tok)
        if l + 1 < LAYERS:
            got, tok = gather_finish(st, h, f"gather_a{l + 1}")
    dx, lpart = loss_head(h, tgt)
    loss = lax.psum(lpart[0, 0], ("x", "y", "c"))
    tok = tok + loss * 0.0

    small, parts, packs = [None] * LAYERS, {n: [None] * LAYERS for n in BIG}, [None] * LAYERS
    st = None

    def scatter(grads, group, l, tok, extra=None):
        arrs, flags = [_to_shards(grads[n], n) for n in group], [False] * len(group)
        if extra is not None:
            arrs, flags = arrs + [extra], flags + [True]
        return exchange_start(_behind(arrs, tok), flags, f"scatter_{group[0]}{l}_start")

    def landed(state, group, l, after):
        got, tok, _ = exchange_wait(state, after, f"scatter_{group[0]}{l}_wait")
        for n, g in zip(group, got):
            parts[n][l] = g
        return got, tok

    for l in reversed(range(LAYERS)):
        dup, g_ffn, small[l] = _bwd_ffn(dx, lws[l], saved[l], l, tok)
        if st is not None:
            _, tok = landed(st, GROUP_A, l + 1, dup)
        st, tok = scatter(g_ffn, SCATTER_FFN, l, tok)
        dx, cot, g_merge, small[l] = _bwd_merge(dx, dup, lws[l], saved[l], l, tok, small[l])
        _, tok = landed(st, SCATTER_FFN, l, dx)
        st, tok = scatter(g_merge, SCATTER_MERGE, l, tok, _pack(small[l + 1]) if l + 1 < LAYERS else None)
        dproj, g_in, small[l] = _bwd_a(dx, cot, lws[l], saved[l], cos2, sin2, l, tok, small[l])
        got, tok = landed(st, SCATTER_MERGE, l, dproj)
        if l + 1 < LAYERS:
            packs[l + 1] = got[-1]
        st, tok = scatter(g_in, GROUP_A, l, tok)
        dx, small[l] = _bwd_in(dx, dproj, lws[l], saved[l], l, tok, small[l])

    dmod = jnp.stack([small[q]["ada_b"] for q in range(LAYERS)])
    st_small, tok = exchange_start(_behind([_pack(small[0]), dmod.reshape(LAYERS, NDEV, 768).transpose(1, 0, 2)], tok),
                                   [True, False], "scatter_s0_start")
    out = {}

    def big_adamw(group, tok):
        res = None
        for n in group:
            t = (lambda a: a.transpose(0, 2, 1)) if n in TRANSPOSED else (lambda a: a)
            res = adamw(parts[n], t(p[n]), t(mom[n]), t(var[n]), f"adamw_{n}", tok)
            out[n] = [t(a) for a in res]
        return res[0]

    g_last = big_adamw(GROUP_B, tok)
    _, tok = landed(st, GROUP_A, 0, g_last)
    (packs[0], dmod_in), _, _ = exchange_wait(st_small, tok, "scatter_s0_wait")
    big_adamw(GROUP_A, None)

    dmod16 = jnp.pad(dmod_in, ((0, 8), (0, 0), (0, 0)))
    g_ada = [tn_matmul(cact, dmod16[:, l], f"dw_ada{l}", out_dtype=F32)[None] for l in range(LAYERS)]
    out["ada_w"] = adamw(g_ada, ada_w, m_ada_w, v_ada_w, "adamw_ada_w")

    for n, pt in _unpack_parts(packs).items():
        if n in SHARDED_SMALL:
            w = SHARDED_SMALL[n]
            pt = lax.dynamic_slice_in_dim(pt, me * w, w, axis=2)
        r, c = pt.shape[1:]
        res = adamw([pt], p[n].reshape(1, r, c), mom[n].reshape(1, r, c), var[n].reshape(1, r, c), f"adamw_{n}")
        out[n] = [a.reshape(p[n].shape) for a in res]

    outs = [loss, dx[None]]
    for q in range(4):
        outs += [out[n][q] for n in names]
    return tuple(outs)
```

```python
import functools
import math

import jax
import jax.numpy as jnp
from jax import lax
from jax.experimental import pallas as pl
from jax.experimental.pallas import tpu as pltpu

F32, BF16 = jnp.float32, jnp.bfloat16
EPS = 1e-6
D = 1024
NDEV = 8
LAYERS = 2
HEADS = 8
FFN = 2816
FFN_TILE = 1408
FFN_NT = FFN // FFN_TILE
ATT_SCALE = 96 ** -0.5
ROPE_THETA = 10000.0
LR, B1, B2, ADAM_EPS, WD, STEP = 0.001, 0.9, 0.999, 1e-08, 0.01, 10

O_G, O_XS, O_Z, O_PU, O_BC, O_CKV, O_KR, O_KRS, O_DT, O_QL = 0, 3072, 4096, 5120, 5632, 6144, 6400, 6528, 6656, 6912
NPROJ = 7296
CONST = dict(pipeline_mode=pl.Buffered(1))


def _pick(n, cap, mult=128):
    if n <= cap:
        return n
    best = None
    for t in range(mult, cap + 1, mult):
        if n % t == 0:
            best = t
    assert best is not None, (n, cap, mult)
    return best


def _sig(x):
    return 1.0 / (1.0 + jnp.exp(-x))


def _rms(x, w, n):
    return x * lax.rsqrt(jnp.sum(x * x, axis=-1, keepdims=True) / n + EPS) * w


def _raw(a, b, dims):
    return lax.dot_general(a.astype(BF16), b.astype(BF16), dims, preferred_element_type=F32)


_NN = (((1,), (0,)), ((), ()))
_NT = (((1,), (1,)), ((), ()))
_TN = (((0,), (0,)), ((), ()))
_BNN = (((2,), (1,)), ((0,), (0,)))
_BNT = (((2,), (2,)), ((0,), (0,)))
_BTN = (((1,), (1,)), ((0,), (0,)))


@jax.custom_vjp
def mm_nn(a, b):
    return _raw(a, b, _NN)


mm_nn.defvjp(lambda a, b: (_raw(a, b, _NN), (a, b)),
             lambda r, g: (_raw(g, r[1], _NT), _raw(r[0], g, _TN)))


@jax.custom_vjp
def mm_nc(a, b):
    return _raw(a, b, _NN)


mm_nc.defvjp(lambda a, b: (_raw(a, b, _NN), b),
             lambda b, g: (_raw(g, b, _NT), jnp.zeros_like(b)))


@jax.custom_vjp
def mm_nt(a, b):
    return _raw(a, b, _NT)


mm_nt.defvjp(lambda a, b: (_raw(a, b, _NT), (a, b)),
             lambda r, g: (_raw(g, r[1], _NN), _raw(g, r[0], _TN)))


@jax.custom_vjp
def bmm_nn(a, b):
    return _raw(a, b, _BNN)


bmm_nn.defvjp(lambda a, b: (_raw(a, b, _BNN), (a, b)),
              lambda r, g: (_raw(g, r[1], _BNT), _raw(r[0], g, _BTN)))


@jax.custom_vjp
def bmm_nt(a, b):
    return _raw(a, b, _BNT)


bmm_nt.defvjp(lambda a, b: (_raw(a, b, _BNT), (a, b)),
              lambda r, g: (_raw(g, r[1], _BNN), _raw(g, r[0], _BTN)))


@jax.custom_vjp
def softplus(x):
    t = jnp.exp(-jnp.abs(x))
    u = 1.0 + t
    one = u == 1.0
    l1p = jnp.where(one, t, jnp.log(u) * (t / jnp.where(one, 1.0, u - 1.0)))
    return jnp.maximum(x, 0.0) + l1p


softplus.defvjp(lambda x: (softplus(x), x), lambda x, g: (g * _sig(x),))


def _params(*sem):
    return pltpu.CompilerParams(dimension_semantics=sem, vmem_limit_bytes=56 * 1024 * 1024)


def all_to_all(arrs, bcast, name):
    n = len(arrs)
    out_shapes = [jax.ShapeDtypeStruct(((NDEV,) + a.shape) if b else a.shape, a.dtype) for a, b in zip(arrs, bcast)]

    def body(*refs):
        ins, outs, token = refs[:n], refs[n:2 * n], refs[2 * n]
        send_sems, recv_sems, local_sems = refs[2 * n + 1:]
        me, remote = _exchange_copies(ins, outs, bcast, send_sems, recv_sems)
        local = [pltpu.make_async_copy(ins[j] if bcast[j] else ins[j].at[me], outs[j].at[me], local_sems.at[j])
                 for j in range(n)]
        for cp in local + remote:
            cp.start()
        for cp in remote + local:
            cp.wait()
        token[...] = jnp.zeros_like(token)

    any_spec = pl.BlockSpec(memory_space=pl.ANY)
    res = pl.pallas_call(
        body, name=name, out_shape=out_shapes + [jax.ShapeDtypeStruct((8, 128), F32)], in_specs=[any_spec] * n,
        out_specs=[any_spec] * n + [pl.BlockSpec(memory_space=pltpu.VMEM)],
        scratch_shapes=[pltpu.SemaphoreType.DMA((7 * n,)), pltpu.SemaphoreType.DMA((7 * n,)),
                        pltpu.SemaphoreType.DMA((n,))],
        compiler_params=pltpu.CompilerParams(has_side_effects=True),
    )(*arrs)
    return res[:n], res[n]


def _peers():
    x, y, c = lax.axis_index("x"), lax.axis_index("y"), lax.axis_index("c")
    out = []
    for k in range(1, NDEV):
        px, py, pc = x ^ ((k >> 2) & 1), y ^ ((k >> 1) & 1), c ^ (k & 1)
        out.append(((px, py, pc), 4 * px + 2 * py + pc))
    return 4 * x + 2 * y + c, out


COPIES = {"all": 7, "chips": 3, "pass": 4}


def _exchange_copies(ins, lands, bcast, send_sems, recv_sems, mode="all"):
    x, y, c = lax.axis_index("x"), lax.axis_index("y"), lax.axis_index("c")
    me = 4 * x + 2 * y + c
    n, copies = len(ins), []

    def add(q, j, src, dst, dev):
        copies.append(pltpu.make_async_remote_copy(
            src_ref=src, dst_ref=dst, send_sem=send_sems.at[q * n + j], recv_sem=recv_sems.at[q * n + j],
            device_id=dev, device_id_type=pl.DeviceIdType.MESH))

    if mode == "pass":
        for q in range(4):
            slot = 4 * (x ^ (q >> 1)) + 2 * (y ^ (q & 1)) + c
            for j in range(n):
                add(q, j, ins[j] if q == 0 else lands[j].at[slot], lands[j].at[slot], (x, y, 1 - c))
        return me, copies
    for q, k in enumerate(range(1, NDEV) if mode == "all" else (2, 4, 6)):
        px, py, pc = x ^ ((k >> 2) & 1), y ^ ((k >> 1) & 1), c ^ (k & 1)
        for j in range(n):
            add(q, j, ins[j] if bcast[j] else ins[j].at[4 * px + 2 * py + pc], lands[j].at[me], (px, py, pc))
    return me, copies


_HBM = pl.BlockSpec(memory_space=pltpu.HBM)
_SEM = pl.BlockSpec(memory_space=pltpu.SEMAPHORE)
_EFFECT = pltpu.SideEffectType.DATAFLOW_SIDE_EFFECTING


def exchange_start(arrs, bcast, name, mode="all", lands=None):
    n, ncp = len(arrs), COPIES[mode] * len(arrs)
    land_shapes = [((NDEV,) + a.shape) if b else a.shape for a, b in zip(arrs, bcast)]
    if lands is None:
        lands = [lax.empty(s_, a.dtype) for s_, a in zip(land_shapes, arrs)]

    def body(*refs):
        in_refs, land_refs = refs[:n], refs[n:2 * n]
        send_sems, recv_sems = refs[2 * n], refs[2 * n + 1]
        token = refs[-1]
        _, copies = _exchange_copies(in_refs, land_refs, bcast, send_sems, recv_sems, mode)
        for cp in copies:
            cp.start()
        token[...] = jnp.zeros_like(token)

    hbm = lambda shp, a: pltpu.HBM(shp, a.dtype)
    res = pl.pallas_call(
        body, name=name,
        out_shape=[pltpu.SemaphoreType.DMA((ncp,)), pltpu.SemaphoreType.DMA((ncp,))]
                  + [hbm(a.shape, a) for a in arrs] + [hbm(s_, a) for s_, a in zip(land_shapes, arrs)]
                  + [jax.ShapeDtypeStruct((8, 128), F32)],
        in_specs=[_HBM] * (2 * n), out_specs=[_SEM, _SEM] + [_HBM] * (2 * n) + [pl.BlockSpec(memory_space=pltpu.VMEM)],
        input_output_aliases={i: 2 + i for i in range(2 * n)},
        compiler_params=pltpu.CompilerParams(has_side_effects=_EFFECT),
    )(*[pltpu.with_memory_space_constraint(a, pltpu.HBM) for a in arrs],
      *[pltpu.with_memory_space_constraint(a, pltpu.HBM) for a in lands])
    return (res[0], res[1], res[2:2 + n], res[2 + n:2 + 2 * n], tuple(bcast), mode), res[-1]


def exchange_wait(state, after, name):
    send_sems, recv_sems, ins, lands, bcast, mode = state
    n = len(ins)

    def body(*refs):
        in_refs, land_refs = refs[:n], refs[n:2 * n]
        s_sems, r_sems = refs[2 * n], refs[2 * n + 1]
        token = refs[-1]
        _, copies = _exchange_copies(in_refs, land_refs, bcast, s_sems, r_sems, mode)
        for cp in copies:
            cp.wait_send()
            cp.wait_recv()
        token[...] = jnp.zeros_like(token)

    res = pl.pallas_call(
        body, name=name,
        out_shape=[pltpu.HBM(a.shape, a.dtype) for a in ins] + [pltpu.HBM(a.shape, a.dtype) for a in lands]
                  + [jax.ShapeDtypeStruct((8, 128), F32)],
        in_specs=[_HBM] * (2 * n) + [_SEM, _SEM, pl.BlockSpec(memory_space=pl.ANY)],
        out_specs=[_HBM] * (2 * n) + [pl.BlockSpec(memory_space=pltpu.VMEM)],
        input_output_aliases={i: i for i in range(2 * n)},
        compiler_params=pltpu.CompilerParams(has_side_effects=_EFFECT),
    )(*ins, *lands, send_sems, recv_sems, after)
    if mode == "chips":
        return list(res[n:2 * n]), res[-1], list(res[:n])
    me = 4 * lax.axis_index("x") + 2 * lax.axis_index("y") + lax.axis_index("c")
    got = []
    for j in range(n):
        own = res[j][None] if bcast[j] else lax.dynamic_index_in_dim(res[j], me, 0, keepdims=True)
        got.append(lax.dynamic_update_slice_in_dim(res[n + j], own, me, axis=0))
    return got, res[-1], list(res[:n])


def gather_start(shards, name):
    return exchange_start(shards, [True] * len(shards), name + "_chips_start", mode="chips")


def gather_finish(state, after, name):
    lands, _, sent = exchange_wait(state, after, name + "_chips_wait")
    state, tok = exchange_start(sent, [True] * len(sent), name + "_pass_start", mode="pass", lands=lands)
    got, tok, _ = exchange_wait(state, tok, name + "_pass_wait")
    return got, tok


def norm_proj_fwd(x, vec, w, name):
    s, n = x.shape[0], w.shape[0]
    tr, tn = _pick(s, 512), _pick(n, 2560)
    ni, jdt, odt = s // tr, O_DT // tn, O_DT % tn

    def body(x_ref, v_ref, w_ref, o_ref, h_ref, dt_ref, h_scr):
        j, i = pl.program_id(0), pl.program_id(1)
        rows = pl.ds(pl.multiple_of(i * tr, tr), tr)

        @pl.when(j == 0)
        def _():
            h = _rms(x_ref[...], v_ref[0:1, :], D) * (1.0 + v_ref[2:3, :]) + v_ref[1:2, :]
            h_scr[rows, :] = h.astype(BF16)
            h_ref[...] = h.astype(BF16)
        res = _raw(h_scr[rows, :], w_ref[...], _NT)
        o_ref[...] = res

        @pl.when(j == jdt)
        def _():
            dt_ref[...] = res[:, odt:odt + 128]

    first = lambda j, i: (jnp.where(j == 0, i, ni - 1), 0)
    dtix = lambda j, i: (jnp.where(j < jdt, 0, jnp.where(j == jdt, i, ni - 1)), 0)
    return pl.pallas_call(
        body, name=name, grid=(n // tn, ni),
        in_specs=[pl.BlockSpec((tr, D), first), pl.BlockSpec((8, D), lambda j, i: (0, 0)),
                  pl.BlockSpec((tn, D), lambda j, i: (j, 0))],
        out_specs=[pl.BlockSpec((tr, tn), lambda j, i: (i, j)), pl.BlockSpec((tr, D), first),
                   pl.BlockSpec((tr, 128), dtix)],
        out_shape=[jax.ShapeDtypeStruct((s, n), F32), jax.ShapeDtypeStruct((s, D), BF16),
                   jax.ShapeDtypeStruct((s, 128), F32)],
        scratch_shapes=[pltpu.VMEM((s, D), BF16)],
        compiler_params=_params("arbitrary", "arbitrary"),
    )(x, vec, w)


def _col_tiles(arr, cap):
    if arr.ndim == 2:
        n = arr.shape[1]
        t = _pick(n, cap)
        return n, t, lambda rows, ix: pl.BlockSpec((rows, t), lambda *g: ix(*g))
    width = arr.shape[2]
    t = _pick(width, cap)
    per = width // t

    def spec(rows, ix):
        def index(*g):
            r, j = ix(*g)
            return (j // per, r, j % per)
        return pl.BlockSpec((None, rows, t), index)
    return arr.shape[0] * width, t, spec


def norm_proj_bwd(x, vec, dp, w, dx_in, aux, name):
    s = x.shape[0]
    tr = _pick(s, 512)
    n, tk, dp_spec = _col_tiles(dp, 2560)
    nk, has_aux = n // tk, aux is not None

    def body(*refs):
        if has_aux:
            x_ref, v_ref, dp_ref, w_ref, dxin_ref, aux_ref, dx_ref, dv_ref, acc = refs
        else:
            x_ref, v_ref, dp_ref, w_ref, dxin_ref, dx_ref, dv_ref, acc = refs
        k, i = pl.program_id(0), pl.program_id(1)
        rows = pl.ds(pl.multiple_of(i * tr, tr), tr)
        part = _raw(dp_ref[...], w_ref[...], _NN)

        @pl.when(k == 0)
        def _():
            acc[rows, :] = part

        @pl.when(k > 0)
        def _():
            acc[rows, :] += part

        @pl.when(k == nk - 1)
        def _():
            f = lambda xx, nw, sh, sc: _rms(xx, nw, D) * (1.0 + sc) + sh
            _, vjp = jax.vjp(f, x_ref[...], v_ref[0:1, :], v_ref[1:2, :], v_ref[2:3, :])
            dx, dnw, dsh, dsc = vjp(acc[rows, :])
            dx_ref[...] = dxin_ref[...] + dx

            @pl.when(i == 0)
            def _():
                dv_ref[...] = jnp.zeros_like(dv_ref)

            dv_ref[0:1, :] += dnw
            dv_ref[1:2, :] += dsh
            dv_ref[2:3, :] += dsc
            if has_aux:
                dv_ref[3:4, :] += jnp.sum(dxin_ref[...] * aux_ref[...], axis=0, keepdims=True)

    row = pl.BlockSpec((tr, D), lambda k, i: (jnp.where(k == nk - 1, i, 0), 0))
    in_specs = [row, pl.BlockSpec((8, D), lambda k, i: (0, 0)), dp_spec(tr, lambda k, i: (i, k)),
                pl.BlockSpec((tk, D), lambda k, i: (k, 0)), row] + ([row] if has_aux else [])
    args = [x, vec, dp, w, dx_in] + ([aux] if has_aux else [])
    return pl.pallas_call(
        body, name=name, grid=(nk, s // tr), in_specs=in_specs,
        out_specs=[row, pl.BlockSpec((8, D), lambda k, i: (0, 0))],
        out_shape=[jax.ShapeDtypeStruct((s, D), F32), jax.ShapeDtypeStruct((8, D), F32)],
        scratch_shapes=[pltpu.VMEM((s, D), F32)],
        compiler_params=_params("arbitrary", "arbitrary"),
    )(*args)


def tn_matmul(a, b, name, scale=None, out_dtype=None):
    out_dtype = BF16 if out_dtype is None else out_dtype
    s = b.shape[-2]
    ts = _pick(s, 512, 16)
    m, tm, a_spec = _col_tiles(a, 2560 if b.shape[-1] <= D else 1408)
    n, tn, b_spec = _col_tiles(b, 2560)
    ns, has_scale = s // ts, scale is not None

    def body(*refs):
        if has_scale:
            a_ref, b_ref, sc_ref, o_ref, acc = refs
        else:
            a_ref, b_ref, o_ref, acc = refs
        k = pl.program_id(2)

        @pl.when(k == 0)
        def _():
            acc[...] = jnp.zeros_like(acc)

        acc[...] += _raw(a_ref[...], b_ref[...], _TN)

        @pl.when(k == ns - 1)
        def _():
            o_ref[...] = (acc[...] * sc_ref[...] if has_scale else acc[...]).astype(out_dtype)

    in_specs = [a_spec(ts, lambda i, j, k: (k, i)), b_spec(ts, lambda i, j, k: (k, j))]
    if has_scale:
        in_specs.append(pl.BlockSpec((1, tn), lambda i, j, k: (0, j)))
    return pl.pallas_call(
        body, name=name, grid=(m // tm, n // tn, ns), in_specs=in_specs,
        out_specs=pl.BlockSpec((tm, tn), lambda i, j, k: (i, j)),
        out_shape=jax.ShapeDtypeStruct((m, n), out_dtype),
        scratch_shapes=[pltpu.VMEM((tm, tn), F32)],
        compiler_params=_params("arbitrary", "arbitrary", "arbitrary"),
    )(*([a, b] + ([scale] if has_scale else [])))


def ada_mod(c16, w):
    ncol = w.shape[2]

    def body(c_ref, w_ref, o_ref, a_ref):
        cc = c_ref[...]
        act = cc * _sig(cc)
        a_ref[...] = act
        o_ref[...] = _raw(act, w_ref[...], _NN)

    return pl.pallas_call(
        body, name="ada_mod", grid=(LAYERS,),
        in_specs=[pl.BlockSpec((16, D), lambda l: (0, 0)), pl.BlockSpec((None, D, ncol), lambda l: (l, 0, 0))],
        out_specs=[pl.BlockSpec((None, 16, ncol), lambda l: (l, 0, 0)), pl.BlockSpec((16, D), lambda l: (0, 0))],
        out_shape=[jax.ShapeDtypeStruct((LAYERS, 16, ncol), F32), jax.ShapeDtypeStruct((16, D), F32)],
        compiler_params=_params("arbitrary"),
    )(c16, w)


def _mla_shared(q_lat, c_kv, kr, krs, qa_w, kva_w, kr_w, krs_w, cos2, sin2):
    qn = _rms(q_lat, qa_w, 384.0)
    kvn = _rms(c_kv, kva_w, 256.0)
    rk = lax.rsqrt(jnp.sum(kr * kr, axis=-1, keepdims=True) / 32.0 + EPS)
    krope = rk * (kr * kr_w * cos2 + krs * krs_w * sin2)
    return qn, kvn, krope


def _mla_head(qn, kvn, wqn, wqr, wqrs, wkn, wv, qn_w, qr_w, qrs_w, kn_w, cos2, sin2):
    qnope = _rms(mm_nn(qn, wqn), qn_w, 64.0)
    qr, qrs = mm_nn(qn, wqr), mm_nn(qn, wqrs)
    rq = lax.rsqrt(jnp.sum(qr * qr, axis=-1, keepdims=True) / 32.0 + EPS)
    qrope = rq * (qr * qr_w * cos2 + qrs * qrs_w * sin2)
    knope = _rms(mm_nn(kvn, wkn), kn_w, 64.0)
    return qnope, qrope, knope, mm_nn(kvn, wv)


def _mla_vec_pieces(v_ref):
    return ((v_ref[0:1, 0:384], v_ref[1:2, 0:256], v_ref[3:4, 128:256], v_ref[3:4, 256:384]),
            (v_ref[2:3, 0:128], v_ref[2:3, 128:256], v_ref[2:3, 256:384], v_ref[3:4, 0:128]))


def _mla_in_specs(tr):
    return [pl.BlockSpec((tr, 384), lambda i: (i, O_QL // 384)), pl.BlockSpec((tr, 256), lambda i: (i, O_CKV // 256)),
            pl.BlockSpec((tr, 128), lambda i: (i, O_KR // 128)), pl.BlockSpec((tr, 128), lambda i: (i, O_KRS // 128)),
            pl.BlockSpec((HEADS, 384, 384), lambda i: (0, 0, 0), **CONST),
            pl.BlockSpec((HEADS, 256, 256), lambda i: (0, 0, 0), **CONST),
            pl.BlockSpec((8, 512), lambda i: (0, 0)),
            pl.BlockSpec((tr, 128), lambda i: (i, 0)), pl.BlockSpec((tr, 128), lambda i: (i, 0))]


def mla_pre_fwd(proj, wq, wkv, vec, cos2, sin2, name):
    s = proj.shape[0]
    tr = _pick(s, 256)

    def body(ql_ref, ckv_ref, kr_ref, krs_ref, wq_ref, wkv_ref, v_ref, cos_ref, sin_ref, q_out, k_out, v_out):
        vshared, vhead = _mla_vec_pieces(v_ref)
        cos2_, sin2_ = cos_ref[...], sin_ref[...]
        qlat_n, kv_n, krope = _mla_shared(ql_ref[...], ckv_ref[...], kr_ref[...], krs_ref[...], *vshared, cos2_, sin2_)
        qlat_n, kv_n, krope = qlat_n.astype(BF16), kv_n.astype(BF16), krope.astype(BF16)
        for h in range(HEADS):
            ws = (wq_ref[h, :, 0:128], wq_ref[h, :, 128:256], wq_ref[h, :, 256:384],
                  wkv_ref[h, :, 0:128], wkv_ref[h, :, 128:256])
            qn, qr, kn, v = _mla_head(qlat_n, kv_n, *ws, *vhead, cos2_, sin2_)
            q_out[h, :, 0:128] = qn.astype(BF16)
            q_out[h, :, 128:256] = qr.astype(BF16)
            k_out[h, :, 0:128] = kn.astype(BF16)
            k_out[h, :, 128:256] = krope
            v_out[h] = v.astype(BF16)

    return pl.pallas_call(
        body, name=name, grid=(s // tr,), in_specs=_mla_in_specs(tr),
        out_specs=[pl.BlockSpec((HEADS, tr, 256), lambda i: (0, i, 0)), pl.BlockSpec((HEADS, tr, 256), lambda i: (0, i, 0)),
                   pl.BlockSpec((HEADS, tr, 128), lambda i: (0, i, 0))],
        out_shape=[jax.ShapeDtypeStruct((HEADS, s, 256), BF16), jax.ShapeDtypeStruct((HEADS, s, 256), BF16),
                   jax.ShapeDtypeStruct((HEADS, s, 128), BF16)],
        compiler_params=_params("arbitrary"),
    )(proj, proj, proj, proj, wq, wkv, vec, cos2, sin2)


def mla_pre_bwd(proj, wq, wkv, vec, cos2, sin2, dq, dk, dv, name):
    s = proj.shape[0]
    tr = _pick(s, 256)

    def body(ql_ref, ckv_ref, kr_ref, krs_ref, wq_ref, wkv_ref, v_ref, cos_ref, sin_ref, dq_ref, dk_ref, dv_ref,
             dql_out, dckv_out, dkr_out, dkrs_out, dwq_out, dwkv_out, dvec_out):
        @pl.when(pl.program_id(0) == 0)
        def _():
            dwq_out[...] = jnp.zeros_like(dwq_out)
            dwkv_out[...] = jnp.zeros_like(dwkv_out)
            dvec_out[...] = jnp.zeros_like(dvec_out)

        vshared, vhead = _mla_vec_pieces(v_ref)
        cos2_, sin2_ = cos_ref[...], sin_ref[...]
        fs = lambda *a: _mla_shared(*a, cos2_, sin2_)
        (qlat_n, kv_n, _), vjp_shared = jax.vjp(fs, ql_ref[...], ckv_ref[...], kr_ref[...], krs_ref[...], *vshared)

        def head(h, carry):
            wq_h, wkv_h = wq_ref[h].astype(F32), wkv_ref[h].astype(F32)
            ws = (wq_h[:, 0:128], wq_h[:, 128:256], wq_h[:, 256:384], wkv_h[:, 0:128], wkv_h[:, 128:256])
            f = lambda *a: _mla_head(*a, cos2_, sin2_)
            _, vjp = jax.vjp(f, qlat_n, kv_n, *ws, *vhead)
            dq_h, dk_h = dq_ref[h], dk_ref[h]
            g = vjp((dq_h[:, 0:128], dq_h[:, 128:256], dk_h[:, 0:128], dv_ref[h]))
            dwq_out[h, :, 0:128] += g[2]
            dwq_out[h, :, 128:256] += g[3]
            dwq_out[h, :, 256:384] += g[4]
            dwkv_out[h, :, 0:128] += g[5]
            dwkv_out[h, :, 128:256] += g[6]
            dvec_out[2:3, 0:128] += g[7]
            dvec_out[2:3, 128:256] += g[8]
            dvec_out[2:3, 256:384] += g[9]
            dvec_out[3:4, 0:128] += g[10]
            return carry[0] + g[0], carry[1] + g[1], carry[2] + dk_h[:, 128:256]

        zero = lambda w: jnp.zeros((tr, w), F32)
        dqn, dkvn, dkrope = lax.fori_loop(0, HEADS, head, (zero(384), zero(256), zero(128)))
        g = vjp_shared((dqn, dkvn, dkrope))
        dql_out[...] = g[0].astype(BF16)
        dckv_out[...] = g[1].astype(BF16)
        dkr_out[...] = g[2].astype(BF16)
        dkrs_out[...] = g[3].astype(BF16)
        dvec_out[0:1, 0:384] += g[4]
        dvec_out[1:2, 0:256] += g[5]
        dvec_out[3:4, 128:256] += g[6]
        dvec_out[3:4, 256:384] += g[7]

    hb = lambda w: pl.BlockSpec((HEADS, tr, w), lambda i: (0, i, 0))
    return pl.pallas_call(
        body, name=name, grid=(s // tr,), in_specs=_mla_in_specs(tr) + [hb(256), hb(256), hb(128)],
        out_specs=[pl.BlockSpec((tr, 384), lambda i: (i, 0)), pl.BlockSpec((tr, 256), lambda i: (i, 0)),
                   pl.BlockSpec((tr, 128), lambda i: (i, 0)), pl.BlockSpec((tr, 128), lambda i: (i, 0)),
                   pl.BlockSpec((HEADS, 384, 384), lambda i: (0, 0, 0)), pl.BlockSpec((HEADS, 256, 256), lambda i: (0, 0, 0)),
                   pl.BlockSpec((8, 512), lambda i: (0, 0))],
        out_shape=[jax.ShapeDtypeStruct((s, 384), BF16), jax.ShapeDtypeStruct((s, 256), BF16),
                   jax.ShapeDtypeStruct((s, 128), BF16), jax.ShapeDtypeStruct((s, 128), BF16),
                   jax.ShapeDtypeStruct((HEADS, 384, 384), F32), jax.ShapeDtypeStruct((HEADS, 256, 256), F32),
                   jax.ShapeDtypeStruct((8, 512), F32)],
        compiler_params=_params("arbitrary"),
    )(proj, proj, proj, proj, wq, wkv, vec, cos2, sin2, dq, dk, dv)


def _att_probs(q, kk, i, tq):
    sc = _raw(q, kk, _NT) * ATT_SCALE
    rows = lax.broadcasted_iota(jnp.int32, sc.shape, 0) + i * tq
    cols = lax.broadcasted_iota(jnp.int32, sc.shape, 1)
    sc = jnp.where(cols <= rows, sc, -jnp.inf)
    e = jnp.exp(sc - jnp.max(sc, axis=-1, keepdims=True))
    return e / jnp.sum(e, axis=-1, keepdims=True)


def mla_attn_fwd(q, k, v, name):
    s = q.shape[1]
    tq = _pick(s, 256)

    def body(q_ref, k_ref, v_ref, o_ref):
        for i in range(s // tq):
            n = (i + 1) * tq
            p = _att_probs(q_ref[i * tq:n, :], k_ref[0:n, :], i, tq)
            o_ref[i * tq:n, :] = _raw(p, v_ref[0:n, :], _NN)

    hs = lambda w: pl.BlockSpec((None, s, w), lambda h: (h, 0, 0))
    return pl.pallas_call(
        body, name=name, grid=(HEADS,), in_specs=[hs(256), hs(256), hs(128)],
        out_specs=pl.BlockSpec((s, 128), lambda h: (0, h)),
        out_shape=jax.ShapeDtypeStruct((s, HEADS * 128), F32),
        compiler_params=_params("arbitrary"),
    )(q, k, v)


def mla_attn_bwd(q, k, v, do, name):
    s = q.shape[1]
    tq = _pick(s, 256)

    def body(q_ref, k_ref, v_ref, do_ref, dq_ref, dk_ref, dv_ref):
        dk_ref[...] = jnp.zeros_like(dk_ref)
        dv_ref[...] = jnp.zeros_like(dv_ref)
        for i in range(s // tq):
            n = (i + 1) * tq
            qq, kk, vv = q_ref[i * tq:n, :], k_ref[0:n, :], v_ref[0:n, :]
            p = _att_probs(qq, kk, i, tq)
            o = _raw(p, vv, _NN)
            dout = do_ref[i * tq:n, :]
            delta = jnp.sum(dout * o, axis=-1, keepdims=True)
            dp = _raw(dout, vv, _NT)
            ds = p * (dp - delta) * ATT_SCALE
            dq_ref[i * tq:n, :] = _raw(ds, kk, _NN)
            dk_ref[0:n, :] += _raw(ds, qq, _TN)
            dv_ref[0:n, :] += _raw(p, dout, _TN)

    hs = lambda w: pl.BlockSpec((None, s, w), lambda h: (h, 0, 0))
    return pl.pallas_call(
        body, name=name, grid=(HEADS,),
        in_specs=[hs(256), hs(256), hs(128), pl.BlockSpec((s, 128), lambda h: (0, h))],
        out_specs=[hs(256), hs(256), hs(128)],
        out_shape=[jax.ShapeDtypeStruct((HEADS, s, 256), F32), jax.ShapeDtypeStruct((HEADS, s, 256), F32),
                   jax.ShapeDtypeStruct((HEADS, s, 128), F32)],
        compiler_params=_params("arbitrary"),
    )(q, k, v, do)


def _pool_windows(u, pad, s, g):
    pad[0:16, :] = jnp.zeros((16, 128), F32)
    cur, sel = u, None
    for j, k in enumerate((1, 2, 4, 8)):
        pad[16:16 + s, :] = cur
        cur = cur + pad[16 - k:16 - k + s, :]
        sel = cur if sel is None else jnp.where(g == j, cur, sel)
    return sel


def _pool_count(s, g):
    t = lax.broadcasted_iota(jnp.int32, (s, 1), 0)
    return jnp.minimum(t + 1, 2 << g).astype(F32)


def pool_fwd(proj, pw, ps, name):
    s = proj.shape[0]

    def body(u_ref, w_ref, s_ref, o_ref, pad):
        g = pl.program_id(0)
        u = u_ref[...]
        pooled = _pool_windows(u, pad, s, g) / _pool_count(s, g) - u
        o_ref[...] = _raw(pooled, w_ref[...], _NN) * s_ref[...]

    return pl.pallas_call(
        body, name=name, grid=(4,),
        in_specs=[pl.BlockSpec((s, 128), lambda g: (0, O_PU // 128 + g)), pl.BlockSpec((None, 128, 128), lambda g: (g, 0, 0)),
                  pl.BlockSpec((1, 128), lambda g: (0, g))],
        out_specs=pl.BlockSpec((s, 128), lambda g: (0, g)),
        out_shape=jax.ShapeDtypeStruct((s, 512), F32),
        scratch_shapes=[pltpu.VMEM((s + 16, 128), F32)],
        compiler_params=_params("arbitrary"),
    )(proj, pw, ps)


def pool_bwd(proj, pw, ps, do, name):
    s = proj.shape[0]

    def body(u_ref, w_ref, s_ref, do_ref, du_ref, dw_ref, ds_ref, pad):
        g = pl.program_id(0)
        u, w, dout = u_ref[...], w_ref[...], do_ref[...]
        cnt = _pool_count(s, g)
        pooled = _pool_windows(u, pad, s, g) / cnt - u
        mixed = _raw(pooled, w, _NN)
        ds_ref[...] = jnp.sum(dout * mixed, axis=0, keepdims=True)
        dmixed = dout * s_ref[...]
        dw_ref[...] = _raw(pooled, dmixed, _TN)
        dpooled = _raw(dmixed, w, _NT)
        dsel = dpooled / cnt
        pad[s:s + 16, :] = jnp.zeros((16, 128), F32)
        cur = jnp.where(g == 3, dsel, 0.0)
        for j, k in ((2, 8), (1, 4), (0, 2)):
            pad[0:s, :] = cur
            cur = cur + pad[k:k + s, :] + jnp.where(g == j, dsel, 0.0)
        pad[0:s, :] = cur
        cur = cur + pad[1:1 + s, :]
        du_ref[...] = (cur - dpooled).astype(BF16)

    return pl.pallas_call(
        body, name=name, grid=(4,),
        in_specs=[pl.BlockSpec((s, 128), lambda g: (0, O_PU // 128 + g)), pl.BlockSpec((None, 128, 128), lambda g: (g, 0, 0)),
                  pl.BlockSpec((1, 128), lambda g: (0, g)), pl.BlockSpec((s, 128), lambda g: (0, g))],
        out_specs=[pl.BlockSpec((s, 128), lambda g: (0, g)), pl.BlockSpec((None, 128, 128), lambda g: (g, 0, 0)),
                   pl.BlockSpec((1, 128), lambda g: (0, g))],
        out_shape=[jax.ShapeDtypeStruct((s, 512), BF16), jax.ShapeDtypeStruct((4, 128, 128), F32),
                   jax.ShapeDtypeStruct((1, 512), F32)],
        scratch_shapes=[pltpu.VMEM((s + 16, 128), F32)],
        compiler_params=_params("arbitrary"),
    )(proj, pw, ps, do)


def _xbc_col(i):
    return jnp.where(i < 2, O_XS // 512 + i, O_BC // 512)


def conv_fwd(proj, cw, cb, name):
    s = proj.shape[0]

    def body(x_ref, w_ref, b_ref, o_ref, t_ref, pad):
        pad[0:8, :] = jnp.zeros((8, 512), F32)
        pad[8:8 + s, :] = x_ref[...]
        y = b_ref[...] + sum(w_ref[k:k + 1, :] * pad[5 + k:5 + k + s, :] for k in range(4))
        act = y * _sig(y)
        o_ref[...] = act

        @pl.when(pl.program_id(0) < 2)
        def _():
            t_ref[...] = act.T

    return pl.pallas_call(
        body, name=name, grid=(3,),
        in_specs=[pl.BlockSpec((s, 512), lambda i: (0, _xbc_col(i))), pl.BlockSpec((4, 512), lambda i: (0, i)),
                  pl.BlockSpec((1, 512), lambda i: (0, i))],
        out_specs=[pl.BlockSpec((s, 512), lambda i: (0, i)), pl.BlockSpec((512, s), lambda i: (jnp.minimum(i, 1), 0))],
        out_shape=[jax.ShapeDtypeStruct((s, 1536), F32), jax.ShapeDtypeStruct((D, s), F32)],
        scratch_shapes=[pltpu.VMEM((s + 8, 512), F32)],
        compiler_params=_params("arbitrary"),
    )(proj, cw, cb)


def conv_bwd(proj, cw, cb, dxt, dbm, dcm, name):
    s = proj.shape[0]

    def body(x_ref, w_ref, b_ref, dxt_ref, dbm_ref, dcm_ref, dx_ref, dw_ref, db_ref, pad, pad2):
        pad[0:8, :] = jnp.zeros((8, 512), F32)
        pad[8:8 + s, :] = x_ref[...]
        y = b_ref[...] + sum(w_ref[k:k + 1, :] * pad[5 + k:5 + k + s, :] for k in range(4))
        sg = _sig(y)

        @pl.when(pl.program_id(0) < 2)
        def _():
            pad2[0:s, :] = dxt_ref[...].T

        @pl.when(pl.program_id(0) == 2)
        def _():
            pad2[0:s, 0:256] = dbm_ref[...]
            pad2[0:s, 256:512] = dcm_ref[...]

        dy = pad2[0:s, :] * (sg * (1.0 + y * (1.0 - sg)))
        db_ref[...] = jnp.sum(dy, axis=0, keepdims=True)
        for k in range(4):
            dw_ref[k:k + 1, :] = jnp.sum(dy * pad[5 + k:5 + k + s, :], axis=0, keepdims=True)
        pad2[s:s + 8, :] = jnp.zeros((8, 512), F32)
        pad2[0:s, :] = dy
        dx_ref[...] = sum(w_ref[k:k + 1, :] * pad2[3 - k:3 - k + s, :] for k in range(4)).astype(BF16)

    return pl.pallas_call(
        body, name=name, grid=(3,),
        in_specs=[pl.BlockSpec((s, 512), lambda i: (0, _xbc_col(i))), pl.BlockSpec((4, 512), lambda i: (0, i)),
                  pl.BlockSpec((1, 512), lambda i: (0, i)), pl.BlockSpec((512, s), lambda i: (jnp.minimum(i, 1), 0)),
                  pl.BlockSpec((s, 256), lambda i: (0, 0)), pl.BlockSpec((s, 256), lambda i: (0, 0))],
        out_specs=[pl.BlockSpec((s, 512), lambda i: (0, i)), pl.BlockSpec((4, 512), lambda i: (0, i)),
                   pl.BlockSpec((1, 512), lambda i: (0, i))],
        out_shape=[jax.ShapeDtypeStruct((s, 1536), BF16), jax.ShapeDtypeStruct((4, 1536), F32),
                   jax.ShapeDtypeStruct((1, 1536), F32)],
        scratch_shapes=[pltpu.VMEM((s + 8, 512), F32), pltpu.VMEM((s + 8, 512), F32)],
        compiler_params=_params("arbitrary"),
    )(proj, cw, cb, dxt, dbm, dcm)


def _ssd_chunk(xt, dtr, bm, cm, hprev, alog, dbias, dskip):
    ln = 128
    a = -jnp.exp(alog)
    dt_r = softplus(dtr + dbias)
    da_r = dt_r * a
    li = lax.broadcasted_iota(jnp.int32, (1, ln, ln), 1)
    si = lax.broadcasted_iota(jnp.int32, (1, ln, ln), 2)
    causal = si <= li
    acs_c = jnp.sum(jnp.where(causal, da_r, 0.0), axis=2, keepdims=True)
    acs_r = jnp.sum(jnp.where(li == si, acs_c, 0.0), axis=1, keepdims=True)
    acs_last = jnp.sum(da_r, axis=2, keepdims=True)
    decay = jnp.exp(jnp.where(causal, acs_c - acs_r, -jnp.inf))
    m = mm_nt(cm, bm)[None] * decay
    xdt = xt * dt_r
    y_diag = bmm_nt(xdt, m)
    bb = jnp.broadcast_to(bm[None], (8, ln, ln))
    cc = jnp.broadcast_to(cm[None], (8, ln, ln))
    states = bmm_nn(xdt * jnp.exp(acs_last - acs_r), bb)
    y_off = bmm_nt(hprev, cc) * jnp.exp(acs_r)
    hnew = hprev * jnp.exp(acs_last) + states
    return y_diag + y_off + xt * dskip, hnew


def _ssd_specs(nc, rev):
    cix = (lambda c: nc - 1 - c) if rev else (lambda c: c)
    hv = pl.BlockSpec((8, 1, 1), lambda g, c: (g, 0, 0))
    return [pl.BlockSpec((8, 64, 128), lambda g, c: (g, 0, cix(c))), pl.BlockSpec((8, 1, 128), lambda g, c: (g, 0, cix(c))),
            pl.BlockSpec((128, 128), lambda g, c: (cix(c), 8 + g)),
            pl.BlockSpec((128, 128), lambda g, c: (cix(c), 10 + g))], hv, cix


def ssd_fwd(xt, dtr, xbc, alog, dbias, dskip, name):
    s = xt.shape[2]
    nc = s // 128
    specs, hv, _ = _ssd_specs(nc, False)

    def body(x_ref, dr_ref, b_ref, c_ref, al_ref, db_ref, dk_ref, y_ref, hs_ref, h_scr):
        @pl.when(pl.program_id(1) == 0)
        def _():
            h_scr[...] = jnp.zeros_like(h_scr)
        hp = h_scr[...]
        hs_ref[...] = hp
        y, hn = _ssd_chunk(x_ref[...], dr_ref[...], b_ref[...], c_ref[...], hp, al_ref[...], db_ref[...], dk_ref[...])
        y_ref[...] = y
        h_scr[...] = hn

    return pl.pallas_call(
        body, name=name, grid=(2, nc), in_specs=specs + [hv, hv, hv],
        out_specs=[pl.BlockSpec((8, 64, 128), lambda g, c: (g, 0, c)),
                   pl.BlockSpec((None, None, 8, 64, 128), lambda g, c: (g, c, 0, 0, 0))],
        out_shape=[jax.ShapeDtypeStruct((16, 64, s), F32), jax.ShapeDtypeStruct((2, nc, 8, 64, 128), F32)],
        scratch_shapes=[pltpu.VMEM((8, 64, 128), F32)],
        compiler_params=_params("arbitrary", "arbitrary"),
    )(xt, dtr, xbc, xbc, alog, dbias, dskip)


def ssd_bwd(xt, dtr, xbc, alog, dbias, dskip, hs, dyt, name):
    s = xt.shape[2]
    nc = s // 128
    specs, hv, cix = _ssd_specs(nc, True)

    def body(x_ref, dr_ref, b_ref, c_ref, al_ref, db_ref, dk_ref, hs_ref, dy_ref,
             dx_out, ddr_out, dbm_out, dcm_out, dal_out, ddb_out, ddk_out, dh_scr):
        @pl.when(pl.program_id(1) == 0)
        def _():
            dh_scr[...] = jnp.zeros_like(dh_scr)
            dal_out[...] = jnp.zeros_like(dal_out)
            ddb_out[...] = jnp.zeros_like(ddb_out)
            ddk_out[...] = jnp.zeros_like(ddk_out)
        _, vjp = jax.vjp(_ssd_chunk, x_ref[...], dr_ref[...], b_ref[...], c_ref[...], hs_ref[...],
                         al_ref[...], db_ref[...], dk_ref[...])
        g = vjp((dy_ref[...], dh_scr[...]))
        dx_out[...] = g[0]
        ddr_out[...] = g[1]
        dbm_out[...] = g[2]
        dcm_out[...] = g[3]
        dh_scr[...] = g[4]
        dal_out[...] += g[5]
        ddb_out[...] += g[6]
        ddk_out[...] += g[7]

    return pl.pallas_call(
        body, name=name, grid=(2, nc),
        in_specs=specs + [hv, hv, hv, pl.BlockSpec((None, None, 8, 64, 128), lambda g, c: (g, cix(c), 0, 0, 0)),
                          pl.BlockSpec((8, 64, 128), lambda g, c: (g, 0, cix(c)))],
        out_specs=[pl.BlockSpec((8, 64, 128), lambda g, c: (g, 0, cix(c))), pl.BlockSpec((8, 1, 128), lambda g, c: (g, 0, cix(c))),
                   pl.BlockSpec((128, 128), lambda g, c: (cix(c), g)),
                   pl.BlockSpec((128, 128), lambda g, c: (cix(c), g)), hv, hv, hv],
        out_shape=[jax.ShapeDtypeStruct((16, 64, s), F32), jax.ShapeDtypeStruct((16, 1, s), F32),
                   jax.ShapeDtypeStruct((s, 256), F32),
                   jax.ShapeDtypeStruct((s, 256), F32)] + [jax.ShapeDtypeStruct((16, 1, 1), F32)] * 3,
        scratch_shapes=[pltpu.VMEM((8, 64, 128), F32)],
        compiler_params=_params("arbitrary", "arbitrary"),
    )(xt, dtr, xbc, xbc, alog, dbias, dskip, hs, dyt)


def _merge(oa, ob, y, z, gla, glb, glc, x, g1, nw, ea, eb, ec, eo, wba, wbb, wbc, wout):
    gated = y * (z * _sig(z))
    sq = gated * gated
    left = lax.broadcasted_iota(jnp.int32, (1, D), 1) < 512
    ms0 = jnp.sum(jnp.where(left, sq, 0.0), axis=-1, keepdims=True) / 512.0
    ms1 = jnp.sum(jnp.where(left, 0.0, sq), axis=-1, keepdims=True) / 512.0
    oc = gated * jnp.where(left, lax.rsqrt(ms0 + EPS), lax.rsqrt(ms1 + EPS)) * nw
    ya, yb, yc = mm_nc(oa, wba) + ea, mm_nc(ob, wbb) + eb, mm_nc(oc, wbc) + ec
    merged = _sig(gla) * ya + _sig(glb) * yb + _sig(glc) * yc
    x1 = x + g1 * (mm_nc(merged, wout) + eo)
    return x1, (oc, merged)


def _merge_specs(tr):
    row = lambda w: pl.BlockSpec((tr, w), lambda i: (i, 0))
    acts = [row(D), row(512), pl.BlockSpec((D, tr), lambda i: (0, i)), pl.BlockSpec((tr, D), lambda i: (i, O_Z // D)),
            pl.BlockSpec((tr, 3 * D), lambda i: (i, 0)), row(D), pl.BlockSpec((8, D), lambda i: (0, 0))]
    cst = lambda r: pl.BlockSpec((r, D), lambda i: (0, 0), **CONST)
    return acts, [cst(D), cst(512), cst(D), cst(D)], row


def merge_fwd(oa, ob, y, proj, x, mvec, wba, wbb, wbc, wout, name):
    s = x.shape[0]
    tr = _pick(s, 256)
    acts, wts, row = _merge_specs(tr)

    def body(oa_ref, ob_ref, y_ref, z_ref, gl_ref, x_ref, mv_ref, wba_ref, wbb_ref, wbc_ref, wout_ref, o_ref):
        zero = jnp.zeros((1, D), F32)
        x1, _ = _merge(oa_ref[...], ob_ref[...], y_ref[...].T, z_ref[...], gl_ref[:, 0:D], gl_ref[:, D:2 * D],
                       gl_ref[:, 2 * D:3 * D], x_ref[...], mv_ref[0:1, :], mv_ref[1:2, :], zero, zero, zero, zero,
                       wba_ref[...], wbb_ref[...], wbc_ref[...], wout_ref[...])
        o_ref[...] = x1

    return pl.pallas_call(
        body, name=name, grid=(s // tr,), in_specs=acts + wts, out_specs=row(D),
        out_shape=jax.ShapeDtypeStruct((s, D), F32), compiler_params=_params("arbitrary"),
    )(oa, ob, y, proj, proj, x, mvec, wba, wbb, wbc, wout)


def merge_bwd(oa, ob, y, proj, x, mvec, wba, wbb, wbc, wout, dx1, name):
    s = x.shape[0]
    tr = _pick(s, 128)
    acts, wts, row = _merge_specs(tr)

    def body(oa_ref, ob_ref, y_ref, z_ref, gl_ref, x_ref, mv_ref, wba_ref, wbb_ref, wbc_ref, wout_ref, dx1_ref,
             doa_o, dob_o, dy_o, dz_o, dgl_o, dx_o, dmv_o, dya_o, dyb_o, dyc_o, dpre_o, oc_o, mg_o):
        zero = jnp.zeros((tr, D), F32)
        wts_ = (wba_ref[...], wbb_ref[...], wbc_ref[...], wout_ref[...])
        f = lambda *a: _merge(*a, *wts_)
        _, vjp, (oc, merged) = jax.vjp(
            f, oa_ref[...], ob_ref[...], y_ref[...].T, z_ref[...], gl_ref[:, 0:D], gl_ref[:, D:2 * D],
            gl_ref[:, 2 * D:3 * D], x_ref[...], mv_ref[0:1, :], mv_ref[1:2, :], zero, zero, zero, zero, has_aux=True)
        g = vjp(dx1_ref[...])
        doa_o[...] = g[0]
        dob_o[...] = g[1]
        dy_o[...] = g[2].T
        dz_o[...] = g[3].astype(BF16)
        dgl_o[:, 0:D] = g[4].astype(BF16)
        dgl_o[:, D:2 * D] = g[5].astype(BF16)
        dgl_o[:, 2 * D:3 * D] = g[6].astype(BF16)
        dx_o[...] = g[7]

        @pl.when(pl.program_id(0) == 0)
        def _():
            dmv_o[...] = jnp.zeros_like(dmv_o)

        dmv_o[0:1, :] += g[8]
        dmv_o[1:2, :] += g[9]
        dya_o[...] = g[10].astype(BF16)
        dyb_o[...] = g[11].astype(BF16)
        dyc_o[...] = g[12].astype(BF16)
        dpre_o[...] = g[13].astype(BF16)
        oc_o[...] = oc.astype(BF16)
        mg_o[...] = merged.astype(BF16)

    sd = lambda w, dt: jax.ShapeDtypeStruct((s, w), dt)
    return pl.pallas_call(
        body, name=name, grid=(s // tr,), in_specs=acts + wts + [row(D)],
        out_specs=[row(D), row(512), pl.BlockSpec((D, tr), lambda i: (0, i)), row(D), row(3 * D), row(D),
                   pl.BlockSpec((8, D), lambda i: (0, 0))] + [row(D)] * 6,
        out_shape=[sd(D, F32), sd(512, F32), jax.ShapeDtypeStruct((D, s), F32), sd(D, BF16), sd(3 * D, BF16), sd(D, F32),
                   jax.ShapeDtypeStruct((8, D), F32)] + [sd(D, BF16)] * 6,
        compiler_params=_params("arbitrary"),
    )(oa, ob, y, proj, proj, x, mvec, wba, wbb, wbc, wout, dx1)


def _conv3(u_scr, w_ref, first, rows, lanes):
    return sum(w_ref[k:k + 1, :] * u_scr[first + k:first + k + rows, lanes] for k in range(3))


def _ffn_tile_specs(tf, tile):
    def at(rows, off):
        return pl.BlockSpec((rows, tf), lambda *g: (0, off + tile(*g)))

    def wt(off):
        return pl.BlockSpec((tf, D), lambda *g: (off + tile(*g), 0))
    return [wt(0), wt(FFN_NT), at(3, 0), at(3, FFN_NT), at(1, 0), at(1, FFN_NT)]


def ffn_fwd(x1, fvec, wup, cw, cb, wdn, name):
    s = x1.shape[0]
    tr, tf = _pick(s, 512), FFN_TILE
    lg, lv = slice(0, tf), slice(tf, 2 * tf)

    def body(x_ref, v_ref, wg_ref, wv_ref, cwg_ref, cwv_ref, cbg_ref, cbv_ref, wd_ref, x2_ref, h_ref, pre_ref,
             h_scr, u_scr, acc):
        i, t = pl.program_id(0), pl.program_id(1)

        @pl.when(t == 0)
        def _():
            @pl.when(i == 0)
            def _():
                h_scr[0:16, :] = jnp.zeros((16, D), BF16)

            @pl.when(i > 0)
            def _():
                h_scr[0:16, :] = h_scr[tr:tr + 16, :]

            h = (_rms(x_ref[...], v_ref[0:1, :], D) * (1.0 + v_ref[2:3, :]) + v_ref[1:2, :]).astype(BF16)
            h_scr[16:16 + tr, :] = h
            h_ref[...] = h
            acc[...] = jnp.zeros_like(acc)

        u_scr[:, lg] = _raw(h_scr[...], wg_ref[...], _NT)
        u_scr[:, lv] = _raw(h_scr[...], wv_ref[...], _NT)
        cg = _conv3(u_scr, cwg_ref, 14, tr, lg) + cbg_ref[...]
        cval = _conv3(u_scr, cwv_ref, 14, tr, lv) + cbv_ref[...]
        acc[...] += _raw(cg * _sig(cg) * cval, wd_ref[...], _NN)

        @pl.when(t == FFN_NT - 1)
        def _():
            pre_ref[...] = acc[...]
            x2_ref[...] = x_ref[...] + v_ref[3:4, :] * acc[...]

    row = pl.BlockSpec((tr, D), lambda i, t: (i, 0))
    return pl.pallas_call(
        body, name=name, grid=(s // tr, FFN_NT),
        in_specs=[row, pl.BlockSpec((8, D), lambda i, t: (0, 0))] + _ffn_tile_specs(tf, lambda i, t: t)
                 + [pl.BlockSpec((tf, D), lambda i, t: (t, 0))],
        out_specs=[row, row, row],
        out_shape=[jax.ShapeDtypeStruct((s, D), F32), jax.ShapeDtypeStruct((s, D), BF16), jax.ShapeDtypeStruct((s, D), F32)],
        scratch_shapes=[pltpu.VMEM((tr + 16, D), BF16), pltpu.VMEM((tr + 16, 2 * tf), F32), pltpu.VMEM((tr, D), F32)],
        compiler_params=_params("arbitrary", "arbitrary"),
    )(x1, fvec, wup, wup, cw, cw, cb, cb, wdn)


def ffn_bwd(h2, dx2, fvec, wup, cw, cb, wdn, name):
    s = h2.shape[0]
    tr, tf = _pick(s, 512), FFN_TILE
    ni, nb = s // tr, s // 16
    lg, lv = slice(0, tf), slice(tf, 2 * tf)

    def body(hp_ref, hm_ref, hn_ref, dm_ref, dn_ref, v_ref, wg_ref, wv_ref, cwg_ref, cwv_ref, cbg_ref, cbv_ref, wd_ref,
             dup_ref, act_ref, dcw_ref, u_scr, dc_scr):
        i = pl.program_id(1)
        hfull = jnp.concatenate([jnp.where(i > 0, hp_ref[...], jnp.zeros((16, D), BF16)), hm_ref[...],
                                 jnp.where(i < ni - 1, hn_ref[...], jnp.zeros((16, D), BF16))], axis=0)
        u_scr[:, lg] = _raw(hfull, wg_ref[...], _NT)
        u_scr[:, lv] = _raw(hfull, wv_ref[...], _NT)
        cg = _conv3(u_scr, cwg_ref, 14, tr + 16, lg) + cbg_ref[...]
        cval = _conv3(u_scr, cwv_ref, 14, tr + 16, lv) + cbv_ref[...]
        g2 = v_ref[3:4, :]
        dpre = jnp.concatenate([dm_ref[...] * g2, jnp.where(i < ni - 1, dn_ref[...], 0.0) * g2], axis=0)
        dact = _raw(dpre, wd_ref[...], _NT)
        sg = _sig(cg)
        sl = cg * sg
        dc_scr[:, lg] = dact * cval * (sg * (1.0 + cg * (1.0 - sg)))
        dc_scr[:, lv] = dact * sl
        act_ref[...] = (sl * cval)[0:tr, :].astype(BF16)

        @pl.when(i == 0)
        def _():
            dcw_ref[...] = jnp.zeros_like(dcw_ref)

        for half, lanes, cw_ref in ((0, lg, cwg_ref), (1, lv, cwv_ref)):
            dup_ref[half] = sum(cw_ref[k:k + 1, :] * dc_scr[2 - k:2 - k + tr, lanes] for k in range(3)).astype(BF16)
            dcm = dc_scr[0:tr, lanes]
            for k in range(3):
                dcw_ref[half, k:k + 1, :] += jnp.sum(dcm * u_scr[14 + k:14 + k + tr, lanes], axis=0, keepdims=True)
            dcw_ref[half, 3:4, :] += jnp.sum(dcm, axis=0, keepdims=True)

    r16 = tr // 16
    prev = lambda t, i: (jnp.maximum(i * r16 - 1, 0), 0)
    nxt = lambda t, i: (jnp.minimum((i + 1) * r16, nb - 1), 0)
    main = lambda t, i: (i, 0)
    return pl.pallas_call(
        body, name=name, grid=(FFN_NT, ni),
        in_specs=[pl.BlockSpec((16, D), prev), pl.BlockSpec((tr, D), main), pl.BlockSpec((16, D), nxt),
                  pl.BlockSpec((tr, D), main), pl.BlockSpec((16, D), nxt), pl.BlockSpec((8, D), lambda t, i: (0, 0))]
                 + _ffn_tile_specs(tf, lambda t, i: t) + [pl.BlockSpec((tf, D), lambda t, i: (t, 0))],
        out_specs=[pl.BlockSpec((2, tr, tf), lambda t, i: (0, i, t)), pl.BlockSpec((tr, tf), lambda t, i: (i, t)),
                   pl.BlockSpec((2, 8, tf), lambda t, i: (0, 0, t))],
        out_shape=[jax.ShapeDtypeStruct((2, s, FFN), BF16), jax.ShapeDtypeStruct((s, FFN), BF16),
                   jax.ShapeDtypeStruct((2, 8, FFN), F32)],
        scratch_shapes=[pltpu.VMEM((tr + 32, 2 * tf), F32), pltpu.VMEM((tr + 16, 2 * tf), F32)],
        compiler_params=_params("arbitrary", "arbitrary"),
    )(h2, h2, h2, dx2, dx2, fvec, wup, wup, cw, cw, cb, cb, wdn)


def loss_head(y, target):
    s = y.shape[0]
    tr = _pick(s, 512)

    def body(y_ref, t_ref, dx_ref, l_ref):
        @pl.when(pl.program_id(0) == 0)
        def _():
            l_ref[...] = jnp.zeros_like(l_ref)
        err = y_ref[...] - t_ref[...]
        dx_ref[...] = err / float(D)
        l_ref[...] += 0.5 * jnp.sum(jnp.sum(err * err, axis=-1, keepdims=True) / float(D), axis=0, keepdims=True)

    row = pl.BlockSpec((tr, D), lambda i: (i, 0))
    return pl.pallas_call(
        body, name="loss_head", grid=(s // tr,), in_specs=[row, row],
        out_specs=[row, pl.BlockSpec((8, 128), lambda i: (0, 0))],
        out_shape=[jax.ShapeDtypeStruct((s, D), F32), jax.ShapeDtypeStruct((8, 128), F32)],
        compiler_params=_params("arbitrary"),
    )(y, target)


def adamw(parts, w, m, v, name, tok=None):
    nseg = len(parts)
    p, r, c = parts[0].shape
    tok = jnp.zeros((8, 128), F32) if tok is None else tok
    cap = 256 if c > 128 else 2048
    step = lambda q, l, i, ni: jnp.clip((l - q) * ni + i, 0, ni - 1)
    if r <= cap or any(r % t == 0 for t in range(8, cap + 1, 8)):
        tr, tc = _pick(r, cap, 8), c
        ni = r // tr
        row = pl.BlockSpec((None, tr, tc), lambda l, i: (l, i, 0))
        part = lambda q: pl.BlockSpec((p, tr, tc), lambda l, i: (0, step(q, l, i, ni), 0))
    else:
        tr, tc = r, _pick(c, 256)
        ni = c // tc
        row = pl.BlockSpec((None, tr, tc), lambda l, i: (l, 0, i))
        part = lambda q: pl.BlockSpec((p, tr, tc), lambda l, i: (0, 0, step(q, l, i, ni)))

    def body(*refs):
        p_refs = refs[:nseg]
        w_ref, m_ref, v_ref, _, g_out, d_out, m_out, v_out, g_scr = refs[nseg:]
        for q in range(nseg):
            @pl.when(pl.program_id(0) == q)
            def _(q=q):
                g = p_refs[q][0].astype(F32)
                for j in range(1, p):
                    g = g + p_refs[q][j].astype(F32)
                g_scr[...] = g
        g = g_scr[...]
        mn = B1 * m_ref[...] + (1.0 - B1) * g
        vn = B2 * v_ref[...] + (1.0 - B2) * (g * g)
        m_hat = mn / (1.0 - B1 ** STEP)
        v_hat = vn / (1.0 - B2 ** STEP)
        g_out[...] = g
        d_out[...] = -LR * (m_hat / (jnp.sqrt(v_hat) + ADAM_EPS) + WD * w_ref[...])
        m_out[...] = mn
        v_out[...] = vn

    return pl.pallas_call(
        body, name=name, grid=(nseg, ni),
        in_specs=[part(q) for q in range(nseg)] + [row, row, row, pl.BlockSpec((8, 128), lambda l, i: (0, 0))],
        out_specs=[row] * 4, out_shape=[jax.ShapeDtypeStruct((nseg, r, c), F32)] * 4,
        scratch_shapes=[pltpu.VMEM((tr, tc), F32)],
        compiler_params=_params("arbitrary", "arbitrary"),
    )(*parts, w, m, v, tok)


def _padc(a, n):
    return jnp.pad(a, [(0, 0)] * (a.ndim - 1) + [(0, n - a.shape[-1])])


def _swap16(a):
    return jnp.concatenate([a[..., 16:32], a[..., 0:16]], axis=-1)


def _shard_cols(g8, a, b):
    c = g8.shape[2]
    return [g8[j][:, max(a, j * c) - j * c:min(b, (j + 1) * c) - j * c] for j in range(a // c, (b - 1) // c + 1)]


def _padr(a, n):
    return jnp.pad(a, ((0, n - a.shape[0]), (0, 0)))


def _swap16r(a):
    return jnp.concatenate([a[16:32], a[0:16]], axis=0)


def _win_layout(g8):
    w = g8.reshape(NDEV * g8.shape[1], g8.shape[2])
    kr = w[640:672]
    return jnp.concatenate([w[3760:6832], w[2208:3232], w[1184:2208], w[672:1184], w[3232:3744], w[384:640],
                            _padr(kr, 128), _padr(_swap16r(kr), 128), _padr(w[3744:3760], 128),
                            jnp.zeros((128, w.shape[1]), w.dtype), w[0:384]], axis=0)


def _win_grad_shards(g):
    kr = (g[O_KR:O_KR + 32].astype(F32) + _swap16r(g[O_KRS:O_KRS + 32].astype(F32))).astype(g.dtype)
    segs = [(g, O_QL, 384), (g, O_CKV, 256), (kr, 0, 32), (g, O_PU, 512), (g, O_Z, D), (g, O_XS, D), (g, O_BC, 512),
            (g, O_DT, 16), (g, O_G, 3 * D)]
    shards, height = [], sum(w for _, _, w in segs) // NDEV
    for j in range(NDEV):
        a, b, off, pieces = height * j, height * (j + 1), 0, []
        for arr, lo, w in segs:
            s0, s1 = max(a, off), min(b, off + w)
            if s0 < s1:
                pieces.append(arr[lo + s0 - off:lo + s1 - off])
            off += w
        shards.append(jnp.concatenate(pieces, axis=0))
    return jnp.stack(shards).astype(BF16)


def _wq_layout(w):
    w = w.reshape(384, HEADS, 96).transpose(1, 0, 2)
    rope = w[:, :, 64:96]
    return jnp.concatenate([_padc(w[:, :, 0:64], 128), _padc(rope, 128), _padc(_swap16(rope), 128)], axis=2)


def _wq_unlayout(g):
    rope = g[:, :, 128:160] + _swap16(g[:, :, 256:288])
    return jnp.concatenate([g[:, :, 0:64], rope], axis=2).transpose(1, 0, 2).reshape(384, HEADS * 96)


def _wkv_layout(w):
    w = w.reshape(256, HEADS, 128).transpose(1, 0, 2)
    return jnp.concatenate([_padc(w[:, :, 0:64], 128), _padc(w[:, :, 64:128], 128)], axis=2)


def _wkv_unlayout(g):
    return jnp.concatenate([g[:, :, 0:64], g[:, :, 128:192]], axis=2).transpose(1, 0, 2).reshape(256, HEADS * 128)


def _wba_layout(w):
    return jnp.pad(w.reshape(HEADS, 64, D), ((0, 0), (0, 64), (0, 0))).reshape(HEADS * 128, D)


def _rows8(rows, width):
    out = jnp.stack([_padc(r.astype(F32), width) for r in rows])
    return jnp.pad(out, ((0, 8 - out.shape[0]), (0, 0)))


def _mla_vec(qa, kva, qn, kn):
    def row(n):
        return jnp.concatenate([_padc(n[0:64], 128), _padc(n[64:96], 128), _padc(_swap16(n[64:96]), 128)])
    return _rows8([qa, kva, row(qn), row(kn)], 512)


def _mla_unvec(g):
    def un(r):
        return jnp.concatenate([r[0:64], r[128:160] + _swap16(r[256:288])])
    return g[0, 0:384], g[1, 0:256], un(g[2]), un(g[3])


SMALL = (("ada_b", (6 * D,)), ("norm1_w", (D,)), ("q_a_norm", (384,)), ("kv_a_norm", (256,)), ("q_norm", (96,)),
         ("k_norm", (96,)), ("pool_w", (4, 128, 128)), ("pool_scale", (512,)), ("ssd_conv_b", (1536,)),
         ("ssd_dt_bias", (16,)), ("ssd_a_log", (16,)), ("ssd_d", (16,)), ("ssd_norm_w", (D,)), ("norm2_w", (D,)),
         ("ffn_conv_b", (2 * FFN,)), ("ssd_conv_w", (4, 1536)), ("ffn_conv_w", (3, 2 * FFN)))
SHARDED_SMALL = {"ssd_conv_w": 192, "ffn_conv_w": 704}


def _pack_rows(shp):
    return -(-math.prod(shp) // 1024) * 8


def _pack(small):
    pieces = []
    for n, shp in SMALL:
        pieces.append(small[n].reshape(-1).astype(F32))
        fill = _pack_rows(shp) * 128 - math.prod(shp)
        if fill:
            pieces.append(jnp.zeros((fill,), F32))
    return jnp.concatenate(pieces).reshape(-1, 128)


def _unpack_parts(packs):
    out, off = {}, 0
    for n, shp in SMALL:
        rows, size = _pack_rows(shp), math.prod(shp)
        r, c = math.prod(shp[:-1]), shp[-1]
        per_layer = [pk[:, off:off + rows].reshape(NDEV, rows * 128)[:, 0:size].reshape(NDEV, r, c) for pk in packs]
        out[n] = jnp.concatenate(per_layer, axis=1)
        off += rows
    return out


GROUP_A = ("w_in", "w_q_b", "w_kv_b")
GROUP_B = ("w_branch", "w_out", "ffn_up", "ffn_down")
BIG = GROUP_A + GROUP_B
SCATTER_FFN, SCATTER_MERGE = ("ffn_up", "ffn_down"), ("w_branch", "w_out")
COL_SHARDED = ("w_q_b", "w_kv_b")
TRANSPOSED = ("w_in", "ffn_up")


def _behind(arrs, tok):
    arrs = list(arrs)
    j = min(range(len(arrs)), key=lambda q: arrs[q].size)
    arrs[j] = arrs[j] + tok[0, 0].astype(arrs[j].dtype)
    return arrs


def _gathered_full(g, name):
    if name in COL_SHARDED:
        return g.transpose(1, 0, 2).reshape(g.shape[1], NDEV * g.shape[2])
    return g.reshape(NDEV * g.shape[1], g.shape[2])


def _to_shards(full, name):
    if name == "w_in":
        return _win_grad_shards(full)
    if name in COL_SHARDED:
        r, c = full.shape
        return full.reshape(r, NDEV, c // NDEV).transpose(1, 0, 2).astype(BF16)
    r, c = full.shape
    return full.reshape(NDEV, r // NDEV, c).astype(BF16)


def _fwd_a(x, lw, mod, cos2, sin2, l, tok):
    sh1, sc1, g1, sh2, sc2, g2 = [mod[j * D:(j + 1) * D] for j in range(6)]
    vec1 = _rows8([lw["norm1_w"], sh1, sc1], D) + tok[0, 0]
    proj, h1, dt_cols = norm_proj_fwd(x, vec1, lw["win"], f"inproj_fwd{l}")
    q, k, v = mla_pre_fwd(proj, lw["wq"], lw["wkv"], lw["mla_vec"], cos2, sin2, f"mla_pre_fwd{l}")
    oa = mla_attn_fwd(q, k, v, f"mla_attn_fwd{l}")
    ob = pool_fwd(proj, lw["pool_w"], lw["pool_scale"].reshape(1, 512), f"pool_fwd{l}")
    xbc, xt = conv_fwd(proj, lw["ssd_conv_w"], lw["ssd_conv_b"].reshape(1, 1536), f"conv_fwd{l}")
    s = x.shape[0]
    xt = xt.reshape(16, 64, s)
    dt = dt_cols[:, 0:16].T
    dtr = dt[:, None, :]
    hv = lambda a: a.reshape(16, 1, 1)
    yt, hs = ssd_fwd(xt, dtr, xbc, hv(lw["ssd_a_log"]), hv(lw["ssd_dt_bias"]), hv(lw["ssd_d"]), f"ssd_fwd{l}")
    return dict(x=x, vec1=vec1, proj=proj, h1=h1, q=q, k=k, v=v, oa=oa, ob=ob, xbc=xbc, xt=xt, dtr=dtr,
                hs=hs, yt=yt.reshape(D, s), mvec=_rows8([g1, lw["ssd_norm_w"]], D),
                fvec=_rows8([lw["norm2_w"], sh2, sc2, g2], D))


def _fwd_b(sv, lw, l, tok):
    sv["mvec"] = sv["mvec"] + tok[0, 0]
    x1 = merge_fwd(sv["oa"], sv["ob"], sv["yt"], sv["proj"], sv["x"], sv["mvec"], lw["wba"], lw["wbb"], lw["wbc"],
                   lw["wout"], f"merge_fwd{l}")
    x2, h2, pre = ffn_fwd(x1, sv["fvec"], lw["wup"], lw["ffn_conv_w"], lw["ffn_conv_b"].reshape(1, 2 * FFN), lw["wdn"],
                          f"ffn_fwd{l}")
    sv.update(x1=x1, h2=h2, pre=pre)
    return x2


def _bwd_ffn(dx2, lw, sv, l, tok):
    fvec = sv["fvec"] + tok[0, 0]
    dup, act, dcw = ffn_bwd(sv["h2"], dx2, fvec, lw["wup"], lw["ffn_conv_w"], lw["ffn_conv_b"].reshape(1, 2 * FFN),
                            lw["wdn"], f"ffn_bwd{l}")
    grads = dict(ffn_down=tn_matmul(act, dx2, f"dw_down{l}", scale=fvec[3:4]),
                 ffn_up=tn_matmul(dup, sv["h2"], f"dw_up{l}"))
    small = dict(ffn_conv_w=jnp.concatenate([dcw[0, 0:3], dcw[1, 0:3]], axis=1),
                 ffn_conv_b=jnp.concatenate([dcw[0, 3], dcw[1, 3]]))
    return dup, grads, small


def _bwd_merge(dx2, dup, lw, sv, l, tok, small):
    grads = {}
    fvec = sv["fvec"] + tok[0, 0]
    dx1, dfvec = norm_proj_bwd(sv["x1"], fvec, dup, lw["wup"], dx2, sv["pre"], f"ffn_norm_bwd{l}")
    small["norm2_w"] = dfvec[0]
    (doa, dob, dyt, dz, dgl, dx, dmvec, dya, dyb, dyc, dpre, oc, merged) = merge_bwd(
        sv["oa"], sv["ob"], sv["yt"], sv["proj"], sv["x"], sv["mvec"], lw["wba"], lw["wbb"], lw["wbc"], lw["wout"], dx1,
        f"merge_bwd{l}")
    dwba = tn_matmul(sv["oa"], dya, f"dw_ba{l}").reshape(HEADS, 128, D)[:, 0:64].reshape(512, D)
    grads["w_branch"] = jnp.concatenate([dwba, tn_matmul(sv["ob"], dyb, f"dw_bb{l}"), tn_matmul(oc, dyc, f"dw_bc{l}")])
    grads["w_out"] = tn_matmul(merged, dpre, f"dw_out{l}")
    small["ssd_norm_w"] = dmvec[1]
    small["dmod_b"] = (dmvec[0], dfvec[1], dfvec[2], dfvec[3])
    return dx, dict(doa=doa, dob=dob, dyt=dyt, dz=dz, dgl=dgl), grads, small


def _bwd_a(dx, cot, lw, sv, cos2, sin2, l, tok, small):
    s = dx.shape[0]
    grads = {}
    doa, dob, dz, dgl = cot["doa"], cot["dob"], cot["dz"], cot["dgl"]
    hv = lambda a: a.reshape(16, 1, 1)
    dxt, ddtr, dbm, dcm, dal, ddb, ddk = ssd_bwd(
        sv["xt"], sv["dtr"], sv["xbc"], hv(lw["ssd_a_log"]) + tok[0, 0], hv(lw["ssd_dt_bias"]),
        hv(lw["ssd_d"]), sv["hs"], cot["dyt"].reshape(16, 64, s), f"ssd_bwd{l}")
    small["ssd_a_log"], small["ssd_dt_bias"], small["ssd_d"] = dal.reshape(16), ddb.reshape(16), ddk.reshape(16)
    dxbc, dscw, dscb = conv_bwd(sv["proj"], lw["ssd_conv_w"], lw["ssd_conv_b"].reshape(1, 1536), dxt.reshape(D, s),
                                dbm, dcm, f"conv_bwd{l}")
    small["ssd_conv_w"], small["ssd_conv_b"] = dscw, dscb.reshape(1536)
    ddt = ddtr[:, 0, :].T
    du, dpw, dps = pool_bwd(sv["proj"], lw["pool_w"], lw["pool_scale"].reshape(1, 512), dob, f"pool_bwd{l}")
    small["pool_w"], small["pool_scale"] = dpw, dps.reshape(512)
    dq, dk, dv = mla_attn_bwd(sv["q"], sv["k"], sv["v"], doa, f"mla_attn_bwd{l}")
    dql, dckv, dkr, dkrs, dwq, dwkv, dmv = mla_pre_bwd(sv["proj"], lw["wq"], lw["wkv"], lw["mla_vec"], cos2, sin2,
                                                       dq, dk, dv, f"mla_pre_bwd{l}")
    grads["w_q_b"], grads["w_kv_b"] = _wq_unlayout(dwq), _wkv_unlayout(dwkv)
    small["q_a_norm"], small["kv_a_norm"], small["q_norm"], small["k_norm"] = _mla_unvec(dmv)
    dproj = jnp.concatenate([dgl, dxbc[:, 0:D], dz, du, dxbc[:, D:1536], dckv, dkr, dkrs,
                             _padc(ddt, 128).astype(BF16), jnp.zeros((s, 128), BF16), dql], axis=1)
    grads["w_in"] = tn_matmul(dproj, sv["h1"], f"dw_in{l}")
    return dproj, grads, small


def _bwd_in(dx, dproj, lw, sv, l, tok, small):
    dx0, dvec1 = norm_proj_bwd(sv["x"], sv["vec1"] + tok[0, 0], dproj, lw["win"], dx, None, f"inproj_bwd{l}")
    small["norm1_w"] = dvec1[0]
    small["ada_b"] = jnp.concatenate([dvec1[1], dvec1[2], *small.pop("dmod_b")])
    return dx0, small


def kernel(x, c, positions, ada_w, ada_b, norm1_w, w_in, q_a_norm, w_q_b, kv_a_norm, w_kv_b, q_norm, k_norm, pool_w, pool_scale, ssd_conv_w, ssd_conv_b, ssd_dt_bias, ssd_a_log, ssd_d, ssd_norm_w, w_branch, w_out, norm2_w, ffn_up, ffn_conv_w, ffn_conv_b, ffn_down, loss_target, m_ada_w, m_ada_b, m_norm1_w, m_w_in, m_q_a_norm, m_w_q_b, m_kv_a_norm, m_w_kv_b, m_q_norm, m_k_norm, m_pool_w, m_pool_scale, m_ssd_conv_w, m_ssd_conv_b, m_ssd_dt_bias, m_ssd_a_log, m_ssd_d, m_ssd_norm_w, m_w_branch, m_w_out, m_norm2_w, m_ffn_up, m_ffn_conv_w, m_ffn_conv_b, m_ffn_down, v_ada_w, v_ada_b, v_norm1_w, v_w_in, v_q_a_norm, v_w_q_b, v_kv_a_norm, v_w_kv_b, v_q_norm, v_k_norm, v_pool_w, v_pool_scale, v_ssd_conv_w, v_ssd_conv_b, v_ssd_dt_bias, v_ssd_a_log, v_ssd_d, v_ssd_norm_w, v_w_branch, v_w_out, v_norm2_w, v_ffn_up, v_ffn_conv_w, v_ffn_conv_b, v_ffn_down):
    p = dict(ada_w=ada_w, ada_b=ada_b, norm1_w=norm1_w, w_in=w_in, q_a_norm=q_a_norm, w_q_b=w_q_b, kv_a_norm=kv_a_norm,
             w_kv_b=w_kv_b, q_norm=q_norm, k_norm=k_norm, pool_w=pool_w, pool_scale=pool_scale, ssd_conv_w=ssd_conv_w,
             ssd_conv_b=ssd_conv_b, ssd_dt_bias=ssd_dt_bias, ssd_a_log=ssd_a_log, ssd_d=ssd_d, ssd_norm_w=ssd_norm_w,
             w_branch=w_branch, w_out=w_out, norm2_w=norm2_w, ffn_up=ffn_up, ffn_conv_w=ffn_conv_w, ffn_conv_b=ffn_conv_b,
             ffn_down=ffn_down)
    mom = dict(ada_w=m_ada_w, ada_b=m_ada_b, norm1_w=m_norm1_w, w_in=m_w_in, q_a_norm=m_q_a_norm, w_q_b=m_w_q_b,
               kv_a_norm=m_kv_a_norm, w_kv_b=m_w_kv_b, q_norm=m_q_norm, k_norm=m_k_norm, pool_w=m_pool_w,
               pool_scale=m_pool_scale, ssd_conv_w=m_ssd_conv_w, ssd_conv_b=m_ssd_conv_b, ssd_dt_bias=m_ssd_dt_bias,
               ssd_a_log=m_ssd_a_log, ssd_d=m_ssd_d, ssd_norm_w=m_ssd_norm_w, w_branch=m_w_branch, w_out=m_w_out,
               norm2_w=m_norm2_w, ffn_up=m_ffn_up, ffn_conv_w=m_ffn_conv_w, ffn_conv_b=m_ffn_conv_b, ffn_down=m_ffn_down)
    var = dict(ada_w=v_ada_w, ada_b=v_ada_b, norm1_w=v_norm1_w, w_in=v_w_in, q_a_norm=v_q_a_norm, w_q_b=v_w_q_b,
               kv_a_norm=v_kv_a_norm, w_kv_b=v_w_kv_b, q_norm=v_q_norm, k_norm=v_k_norm, pool_w=v_pool_w,
               pool_scale=v_pool_scale, ssd_conv_w=v_ssd_conv_w, ssd_conv_b=v_ssd_conv_b, ssd_dt_bias=v_ssd_dt_bias,
               ssd_a_log=v_ssd_a_log, ssd_d=v_ssd_d, ssd_norm_w=v_ssd_norm_w, w_branch=v_w_branch, w_out=v_w_out,
               norm2_w=v_norm2_w, ffn_up=v_ffn_up, ffn_conv_w=v_ffn_conv_w, ffn_conv_b=v_ffn_conv_b, ffn_down=v_ffn_down)
    names = list(p)
    me = 4 * lax.axis_index("x") + 2 * lax.axis_index("y") + lax.axis_index("c")
    xs, tgt = x[0], loss_target[0]
    s = xs.shape[0]

    inv_freq = ROPE_THETA ** (-jnp.arange(0, 32, 2, dtype=F32) / 32.0)
    ang = positions[0].astype(F32)[:, None] * inv_freq
    cos, sin = jnp.cos(ang), jnp.sin(ang)
    cos2 = _padc(jnp.concatenate([cos, cos], axis=1), 128)
    sin2 = _padc(jnp.concatenate([-sin, sin], axis=1), 128)

    conv_shards = jnp.concatenate([ssd_conv_w.reshape(-1), ffn_conv_w.reshape(-1)])
    (c_all, conv_all), _ = all_to_all([c, conv_shards], [True, True], "gather_c")
    modp, cact = ada_mod(jnp.pad(c_all.reshape(NDEV, D), ((0, 8), (0, 0))), ada_w)
    (mod_in,), tok = all_to_all([modp[:, 0:NDEV].transpose(1, 0, 2)], [False], "scatter_mod")
    mod = mod_in.transpose(1, 0, 2).reshape(LAYERS, 6 * D) + ada_b

    n1 = LAYERS * 4 * 192
    scw = conv_all[:, :n1].reshape(NDEV, LAYERS, 4, 192).transpose(1, 2, 0, 3).reshape(LAYERS, 4, 1536)
    fcw = conv_all[:, n1:].reshape(NDEV, LAYERS, 3, 704).transpose(1, 2, 0, 3).reshape(LAYERS, 3, 2 * FFN)

    def weights_a(gathered, l):
        full = {n: _gathered_full(g, n) for n, g in zip(GROUP_A[1:], gathered[1:])}
        lw = {n: p[n][l] for n in names}
        lw.update(win=_win_layout(gathered[0]), wq=_wq_layout(full["w_q_b"]), wkv=_wkv_layout(full["w_kv_b"]),
                  ssd_conv_w=scw[l], ffn_conv_w=fcw[l],
                  mla_vec=_mla_vec(lw["q_a_norm"], lw["kv_a_norm"], lw["q_norm"], lw["k_norm"]))
        return lw

    def weights_b(gathered):
        full = {n: _gathered_full(g, n) for n, g in zip(GROUP_B, gathered)}
        wb = full["w_branch"]
        return dict(wba=_wba_layout(wb[0:512]), wbb=wb[512:1024], wbc=wb[1024:2048], wout=full["w_out"],
                    wup=full["ffn_up"], wdn=full["ffn_down"])

    shards = lambda group, l: [(p[n][l].T if n in TRANSPOSED else p[n][l]).astype(BF16) for n in group]
    lws, saved = [None] * LAYERS, [None] * LAYERS
    st, tok = gather_start(_behind(shards(GROUP_A, 0), tok), "gather_a0")
    got, tok = gather_finish(st, tok, "gather_a0")
    h = xs
    for l in range(LAYERS):
        st, tok = gather_start(_behind(shards(GROUP_B, l), tok), f"gather_b{l}")
        lws[l] = weights_a(got, l)
        saved[l] = _fwd_a(h, lws[l], mod[l], cos2, sin2, l, tok)
        got, tok = gather_finish(st, saved[l]["yt"], f"gather_b{l}")
        lws[l].update(weights_b(got))
        if l + 1 < LAYERS:
            st, tok = gather_start(_behind(shards(GROUP_A, l + 1), tok), f"gather_a{l + 1}")
        h = _fwd_b(saved[l], lws[l], l, tok)
        if l + 1 < LAYERS:
            got, tok = gather_finish(st, h, f"gather_a{l + 1}")
    dx, lpart = loss_head(h, tgt)
    loss = lax.psum(lpart[0, 0], ("x", "y", "c"))
    tok = tok + loss * 0.0

    small, parts, packs = [None] * LAYERS, {n: [None] * LAYERS for n in BIG}, [None] * LAYERS
    st = None

    def scatter(grads, group, l, tok, extra=None):
        arrs, flags = [_to_shards(grads[n], n) for n in group], [False] * len(group)
        if extra is not None:
            arrs, flags = arrs + [extra], flags + [True]
        return exchange_start(_behind(arrs, tok), flags, f"scatter_{group[0]}{l}_start")

    def landed(state, group, l, after):
        got, tok, _ = exchange_wait(state, after, f"scatter_{group[0]}{l}_wait")
        for n, g in zip(group, got):
            parts[n][l] = g
        return got, tok

    for l in reversed(range(LAYERS)):
        dup, g_ffn, small[l] = _bwd_ffn(dx, lws[l], saved[l], l, tok)
        if st is not None:
            _, tok = landed(st, GROUP_A, l + 1, dup)
        st, tok = scatter(g_ffn, SCATTER_FFN, l, tok)
        dx, cot, g_merge, small[l] = _bwd_merge(dx, dup, lws[l], saved[l], l, tok, small[l])
        _, tok = landed(st, SCATTER_FFN, l, dx)
        st, tok = scatter(g_merge, SCATTER_MERGE, l, tok, _pack(small[l + 1]) if l + 1 < LAYERS else None)
        dproj, g_in, small[l] = _bwd_a(dx, cot, lws[l], saved[l], cos2, sin2, l, tok, small[l])
        got, tok = landed(st, SCATTER_MERGE, l, dproj)
        if l + 1 < LAYERS:
            packs[l + 1] = got[-1]
        st, tok = scatter(g_in, GROUP_A, l, tok)
        dx, small[l] = _bwd_in(dx, dproj, lws[l], saved[l], l, tok, small[l])

    dmod = jnp.stack([small[q]["ada_b"] for q in range(LAYERS)])
    st_small, tok = exchange_start(_behind([_pack(small[0]), dmod.reshape(LAYERS, NDEV, 768).transpose(1, 0, 2)], tok),
                                   [True, False], "scatter_s0_start")
    out = {}

    def big_adamw(group, tok):
        res = None
        for n in group:
            t = (lambda a: a.transpose(0, 2, 1)) if n in TRANSPOSED else (lambda a: a)
            res = adamw(parts[n], t(p[n]), t(mom[n]), t(var[n]), f"adamw_{n}", tok)
            out[n] = [t(a) for a in res]
        return res[0]

    g_last = big_adamw(GROUP_B, tok)
    _, tok = landed(st, GROUP_A, 0, g_last)
    (packs[0], dmod_in), _, _ = exchange_wait(st_small, tok, "scatter_s0_wait")
    big_adamw(GROUP_A, None)

    dmod16 = jnp.pad(dmod_in, ((0, 8), (0, 0), (0, 0)))
    g_ada = [tn_matmul(cact, dmod16[:, l], f"dw_ada{l}", out_dtype=F32)[None] for l in range(LAYERS)]
    out["ada_w"] = adamw(g_ada, ada_w, m_ada_w, v_ada_w, "adamw_ada_w")

    for n, pt in _unpack_parts(packs).items():
        if n in SHARDED_SMALL:
            w = SHARDED_SMALL[n]
            pt = lax.dynamic_slice_in_dim(pt, me * w, w, axis=2)
        r, c = pt.shape[1:]
        res = adamw([pt], p[n].reshape(1, r, c), mom[n].reshape(1, r, c), var[n].reshape(1, r, c), f"adamw_{n}")
        out[n] = [a.reshape(p[n].shape) for a in res]

    outs = [loss, dx[None]]
    for q in range(4):
        outs += [out[n][q] for n in names]
    return tuple(outs)
```

```python
import functools
import math

import jax
import jax.numpy as jnp
from jax import lax
from jax.experimental import pallas as pl
from jax.experimental.pallas import tpu as pltpu

F32, BF16 = jnp.float32, jnp.bfloat16
EPS = 1e-6
D = 1024
NDEV = 8
LAYERS = 2
HEADS = 8
FFN = 2816
FFN_TILE = 1408
FFN_NT = FFN // FFN_TILE
ATT_SCALE = 96 ** -0.5
ROPE_THETA = 10000.0
LR, B1, B2, ADAM_EPS, WD, STEP = 0.001, 0.9, 0.999, 1e-08, 0.01, 10

O_G, O_XS, O_Z, O_PU, O_BC, O_CKV, O_KR, O_KRS, O_DT, O_QL = 0, 3072, 4096, 5120, 5632, 6144, 6400, 6528, 6656, 6912
NPROJ = 7296
CONST = dict(pipeline_mode=pl.Buffered(1))


def _pick(n, cap, mult=128):
    if n <= cap:
        return n
    best = None
    for t in range(mult, cap + 1, mult):
        if n % t == 0:
            best = t
    assert best is not None, (n, cap, mult)
    return best


def _sig(x):
    return 1.0 / (1.0 + jnp.exp(-x))


def _rms(x, w, n):
    return x * lax.rsqrt(jnp.sum(x * x, axis=-1, keepdims=True) / n + EPS) * w


def _raw(a, b, dims):
    return lax.dot_general(a.astype(BF16), b.astype(BF16), dims, preferred_element_type=F32)


_NN = (((1,), (0,)), ((), ()))
_NT = (((1,), (1,)), ((), ()))
_TN = (((0,), (0,)), ((), ()))
_BNN = (((2,), (1,)), ((0,), (0,)))
_BNT = (((2,), (2,)), ((0,), (0,)))
_BTN = (((1,), (1,)), ((0,), (0,)))


@jax.custom_vjp
def mm_nn(a, b):
    return _raw(a, b, _NN)


mm_nn.defvjp(lambda a, b: (_raw(a, b, _NN), (a, b)),
             lambda r, g: (_raw(g, r[1], _NT), _raw(r[0], g, _TN)))


@jax.custom_vjp
def mm_nc(a, b):
    return _raw(a, b, _NN)


mm_nc.defvjp(lambda a, b: (_raw(a, b, _NN), b),
             lambda b, g: (_raw(g, b, _NT), jnp.zeros_like(b)))


@jax.custom_vjp
def mm_nt(a, b):
    return _raw(a, b, _NT)


mm_nt.defvjp(lambda a, b: (_raw(a, b, _NT), (a, b)),
             lambda r, g: (_raw(g, r[1], _NN), _raw(g, r[0], _TN)))


@jax.custom_vjp
def bmm_nn(a, b):
    return _raw(a, b, _BNN)


bmm_nn.defvjp(lambda a, b: (_raw(a, b, _BNN), (a, b)),
              lambda r, g: (_raw(g, r[1], _BNT), _raw(r[0], g, _BTN)))


@jax.custom_vjp
def bmm_nt(a, b):
    return _raw(a, b, _BNT)


bmm_nt.defvjp(lambda a, b: (_raw(a, b, _BNT), (a, b)),
              lambda r, g: (_raw(g, r[1], _BNN), _raw(g, r[0], _BTN)))


@jax.custom_vjp
def softplus(x):
    t = jnp.exp(-jnp.abs(x))
    u = 1.0 + t
    one = u == 1.0
    l1p = jnp.where(one, t, jnp.log(u) * (t / jnp.where(one, 1.0, u - 1.0)))
    return jnp.maximum(x, 0.0) + l1p


softplus.defvjp(lambda x: (softplus(x), x), lambda x, g: (g * _sig(x),))


def _params(*sem):
    return pltpu.CompilerParams(dimension_semantics=sem, vmem_limit_bytes=56 * 1024 * 1024)


def all_to_all(arrs, bcast, name):
    n = len(arrs)
    out_shapes = [jax.ShapeDtypeStruct(((NDEV,) + a.shape) if b else a.shape, a.dtype) for a, b in zip(arrs, bcast)]

    def body(*refs):
        ins, outs, token = refs[:n], refs[n:2 * n], refs[2 * n]
        send_sems, recv_sems, local_sems = refs[2 * n + 1:]
        me, remote = _exchange_copies(ins, outs, bcast, send_sems, recv_sems)
        local = [pltpu.make_async_copy(ins[j] if bcast[j] else ins[j].at[me], outs[j].at[me], local_sems.at[j])
                 for j in range(n)]
        for cp in local + remote:
            cp.start()
        for cp in remote + local:
            cp.wait()
        token[...] = jnp.zeros_like(token)

    any_spec = pl.BlockSpec(memory_space=pl.ANY)
    res = pl.pallas_call(
        body, name=name, out_shape=out_shapes + [jax.ShapeDtypeStruct((8, 128), F32)], in_specs=[any_spec] * n,
        out_specs=[any_spec] * n + [pl.BlockSpec(memory_space=pltpu.VMEM)],
        scratch_shapes=[pltpu.SemaphoreType.DMA((7 * n,)), pltpu.SemaphoreType.DMA((7 * n,)),
                        pltpu.SemaphoreType.DMA((n,))],
        compiler_params=pltpu.CompilerParams(has_side_effects=True),
    )(*arrs)
    return res[:n], res[n]


def _peers():
    x, y, c = lax.axis_index("x"), lax.axis_index("y"), lax.axis_index("c")
    out = []
    for k in range(1, NDEV):
        px, py, pc = x ^ ((k >> 2) & 1), y ^ ((k >> 1) & 1), c ^ (k & 1)
        out.append(((px, py, pc), 4 * px + 2 * py + pc))
    return 4 * x + 2 * y + c, out


COPIES = {"all": 7, "chips": 3, "pass": 4}


def _exchange_copies(ins, lands, bcast, send_sems, recv_sems, mode="all"):
    x, y, c = lax.axis_index("x"), lax.axis_index("y"), lax.axis_index("c")
    me = 4 * x + 2 * y + c
    n, copies = len(ins), []

    def add(q, j, src, dst, dev):
        copies.append(pltpu.make_async_remote_copy(
            src_ref=src, dst_ref=dst, send_sem=send_sems.at[q * n + j], recv_sem=recv_sems.at[q * n + j],
            device_id=dev, device_id_type=pl.DeviceIdType.MESH))

    if mode == "pass":
        for q in range(4):
            slot = 4 * (x ^ (q >> 1)) + 2 * (y ^ (q & 1)) + c
            for j in range(n):
                add(q, j, ins[j] if q == 0 else lands[j].at[slot], lands[j].at[slot], (x, y, 1 - c))
        return me, copies
    for q, k in enumerate(range(1, NDEV) if mode == "all" else (2, 4, 6)):
        px, py, pc = x ^ ((k >> 2) & 1), y ^ ((k >> 1) & 1), c ^ (k & 1)
        for j in range(n):
            add(q, j, ins[j] if bcast[j] else ins[j].at[4 * px + 2 * py + pc], lands[j].at[me], (px, py, pc))
    return me, copies


_HBM = pl.BlockSpec(memory_space=pltpu.HBM)
_SEM = pl.BlockSpec(memory_space=pltpu.SEMAPHORE)
_EFFECT = pltpu.SideEffectType.DATAFLOW_SIDE_EFFECTING


def exchange_start(arrs, bcast, name, mode="all", lands=None):
    n, ncp = len(arrs), COPIES[mode] * len(arrs)
    land_shapes = [((NDEV,) + a.shape) if b else a.shape for a, b in zip(arrs, bcast)]
    if lands is None:
        lands = [lax.empty(s_, a.dtype) for s_, a in zip(land_shapes, arrs)]

    def body(*refs):
        in_refs, land_refs = refs[:n], refs[n:2 * n]
        send_sems, recv_sems = refs[2 * n], refs[2 * n + 1]
        token = refs[-1]
        _, copies = _exchange_copies(in_refs, land_refs, bcast, send_sems, recv_sems, mode)
        for cp in copies:
            cp.start()
        token[...] = jnp.zeros_like(token)

    hbm = lambda shp, a: pltpu.HBM(shp, a.dtype)
    res = pl.pallas_call(
        body, name=name,
        out_shape=[pltpu.SemaphoreType.DMA((ncp,)), pltpu.SemaphoreType.DMA((ncp,))]
                  + [hbm(a.shape, a) for a in arrs] + [hbm(s_, a) for s_, a in zip(land_shapes, arrs)]
                  + [jax.ShapeDtypeStruct((8, 128), F32)],
        in_specs=[_HBM] * (2 * n), out_specs=[_SEM, _SEM] + [_HBM] * (2 * n) + [pl.BlockSpec(memory_space=pltpu.VMEM)],
        input_output_aliases={i: 2 + i for i in range(2 * n)},
        compiler_params=pltpu.CompilerParams(has_side_effects=_EFFECT),
    )(*[pltpu.with_memory_space_constraint(a, pltpu.HBM) for a in arrs],
      *[pltpu.with_memory_space_constraint(a, pltpu.HBM) for a in lands])
    return (res[0], res[1], res[2:2 + n], res[2 + n:2 + 2 * n], tuple(bcast), mode), res[-1]


def exchange_wait(state, after, name):
    send_sems, recv_sems, ins, lands, bcast, mode = state
    n = len(ins)

    def body(*refs):
        in_refs, land_refs = refs[:n], refs[n:2 * n]
        s_sems, r_sems = refs[2 * n], refs[2 * n + 1]
        token = refs[-1]
        _, copies = _exchange_copies(in_refs, land_refs, bcast, s_sems, r_sems, mode)
        for cp in copies:
            cp.wait_send()
            cp.wait_recv()
        token[...] = jnp.zeros_like(token)

    res = pl.pallas_call(
        body, name=name,
        out_shape=[pltpu.HBM(a.shape, a.dtype) for a in ins] + [pltpu.HBM(a.shape, a.dtype) for a in lands]
                  + [jax.ShapeDtypeStruct((8, 128), F32)],
        in_specs=[_HBM] * (2 * n) + [_SEM, _SEM, pl.BlockSpec(memory_space=pl.ANY)],
        out_specs=[_HBM] * (2 * n) + [pl.BlockSpec(memory_space=pltpu.VMEM)],
        input_output_aliases={i: i for i in range(2 * n)},
        compiler_params=pltpu.CompilerParams(has_side_effects=_EFFECT),
    )(*ins, *lands, send_sems, recv_sems, after)
    if mode == "chips":
        return list(res[n:2 * n]), res[-1], list(res[:n])
    me = 4 * lax.axis_index("x") + 2 * lax.axis_index("y") + lax.axis_index("c")
    got = []
    for j in range(n):
        own = res[j][None] if bcast[j] else lax.dynamic_index_in_dim(res[j], me, 0, keepdims=True)
        got.append(lax.dynamic_update_slice_in_dim(res[n + j], own, me, axis=0))
    return got, res[-1], list(res[:n])


def gather_start(shards, name):
    return exchange_start(shards, [True] * len(shards), name + "_chips_start", mode="chips")


def gather_finish(state, after, name):
    lands, _, sent = exchange_wait(state, after, name + "_chips_wait")
    state, tok = exchange_start(sent, [True] * len(sent), name + "_pass_start", mode="pass", lands=lands)
    got, tok, _ = exchange_wait(state, tok, name + "_pass_wait")
    return got, tok


def norm_proj_fwd(x, vec, w, name):
    s, n = x.shape[0], w.shape[0]
    tr, tn = _pick(s, 512), _pick(n, 2560)
    ni, jdt, odt = s // tr, O_DT // tn, O_DT % tn

    def body(x_ref, v_ref, w_ref, o_ref, h_ref, dt_ref, h_scr):
        j, i = pl.program_id(0), pl.program_id(1)
        rows = pl.ds(pl.multiple_of(i * tr, tr), tr)

        @pl.when(j == 0)
        def _():
            h = _rms(x_ref[...], v_ref[0:1, :], D) * (1.0 + v_ref[2:3, :]) + v_ref[1:2, :]
            h_scr[rows, :] = h.astype(BF16)
            h_ref[...] = h.astype(BF16)
        res = _raw(h_scr[rows, :], w_ref[...], _NT)
        o_ref[...] = res

        @pl.when(j == jdt)
        def _():
            dt_ref[...] = res[:, odt:odt + 128]

    first = lambda j, i: (jnp.where(j == 0, i, ni - 1), 0)
    dtix = lambda j, i: (jnp.where(j < jdt, 0, jnp.where(j == jdt, i, ni - 1)), 0)
    return pl.pallas_call(
        body, name=name, grid=(n // tn, ni),
        in_specs=[pl.BlockSpec((tr, D), first), pl.BlockSpec((8, D), lambda j, i: (0, 0)),
                  pl.BlockSpec((tn, D), lambda j, i: (j, 0))],
        out_specs=[pl.BlockSpec((tr, tn), lambda j, i: (i, j)), pl.BlockSpec((tr, D), first),
                   pl.BlockSpec((tr, 128), dtix)],
        out_shape=[jax.ShapeDtypeStruct((s, n), F32), jax.ShapeDtypeStruct((s, D), BF16),
                   jax.ShapeDtypeStruct((s, 128), F32)],
        scratch_shapes=[pltpu.VMEM((s, D), BF16)],
        compiler_params=_params("arbitrary", "arbitrary"),
    )(x, vec, w)


def _col_tiles(arr, cap):
    if arr.ndim == 2:
        n = arr.shape[1]
        t = _pick(n, cap)
        return n, t, lambda rows, ix: pl.BlockSpec((rows, t), lambda *g: ix(*g))
    width = arr.shape[2]
    t = _pick(width, cap)
    per = width // t

    def spec(rows, ix):
        def index(*g):
            r, j = ix(*g)
            return (j // per, r, j % per)
        return pl.BlockSpec((None, rows, t), index)
    return arr.shape[0] * width, t, spec


def norm_proj_bwd(x, vec, dp, w, dx_in, aux, name):
    s = x.shape[0]
    tr = _pick(s, 512)
    n, tk, dp_spec = _col_tiles(dp, 2560)
    nk, has_aux = n // tk, aux is not None

    def body(*refs):
        if has_aux:
            x_ref, v_ref, dp_ref, w_ref, dxin_ref, aux_ref, dx_ref, dv_ref, acc = refs
        else:
            x_ref, v_ref, dp_ref, w_ref, dxin_ref, dx_ref, dv_ref, acc = refs
        k, i = pl.program_id(0), pl.program_id(1)
        rows = pl.ds(pl.multiple_of(i * tr, tr), tr)
        part = _raw(dp_ref[...], w_ref[...], _NN)

        @pl.when(k == 0)
        def _():
            acc[rows, :] = part

        @pl.when(k > 0)
        def _():
            acc[rows, :] += part

        @pl.when(k == nk - 1)
        def _():
            f = lambda xx, nw, sh, sc: _rms(xx, nw, D) * (1.0 + sc) + sh
            _, vjp = jax.vjp(f, x_ref[...], v_ref[0:1, :], v_ref[1:2, :], v_ref[2:3, :])
            dx, dnw, dsh, dsc = vjp(acc[rows, :])
            dx_ref[...] = dxin_ref[...] + dx

            @pl.when(i == 0)
            def _():
                dv_ref[...] = jnp.zeros_like(dv_ref)

            dv_ref[0:1, :] += dnw
            dv_ref[1:2, :] += dsh
            dv_ref[2:3, :] += dsc
            if has_aux:
                dv_ref[3:4, :] += jnp.sum(dxin_ref[...] * aux_ref[...], axis=0, keepdims=True)

    row = pl.BlockSpec((tr, D), lambda k, i: (jnp.where(k == nk - 1, i, 0), 0))
    in_specs = [row, pl.BlockSpec((8, D), lambda k, i: (0, 0)), dp_spec(tr, lambda k, i: (i, k)),
                pl.BlockSpec((tk, D), lambda k, i: (k, 0)), row] + ([row] if has_aux else [])
    args = [x, vec, dp, w, dx_in] + ([aux] if has_aux else [])
    return pl.pallas_call(
        body, name=name, grid=(nk, s // tr), in_specs=in_specs,
        out_specs=[row, pl.BlockSpec((8, D), lambda k, i: (0, 0))],
        out_shape=[jax.ShapeDtypeStruct((s, D), F32), jax.ShapeDtypeStruct((8, D), F32)],
        scratch_shapes=[pltpu.VMEM((s, D), F32)],
        compiler_params=_params("arbitrary", "arbitrary"),
    )(*args)


def tn_matmul(a, b, name, scale=None, out_dtype=None):
    out_dtype = BF16 if out_dtype is None else out_dtype
    s = b.shape[-2]
    ts = _pick(s, 512, 16)
    m, tm, a_spec = _col_tiles(a, 2560 if b.shape[-1] <= D else 1408)
    n, tn, b_spec = _col_tiles(b, 2560)
    ns, has_scale = s // ts, scale is not None

    def body(*refs):
        if has_scale:
            a_ref, b_ref, sc_ref, o_ref, acc = refs
        else:
            a_ref, b_ref, o_ref, acc = refs
        k = pl.program_id(2)

        @pl.when(k == 0)
        def _():
            acc[...] = jnp.zeros_like(acc)

        acc[...] += _raw(a_ref[...], b_ref[...], _TN)

        @pl.when(k == ns - 1)
        def _():
            o_ref[...] = (acc[...] * sc_ref[...] if has_scale else acc[...]).astype(out_dtype)

    in_specs = [a_spec(ts, lambda i, j, k: (k, i)), b_spec(ts, lambda i, j, k: (k, j))]
    if has_scale:
        in_specs.append(pl.BlockSpec((1, tn), lambda i, j, k: (0, j)))
    return pl.pallas_call(
        body, name=name, grid=(m // tm, n // tn, ns), in_specs=in_specs,
        out_specs=pl.BlockSpec((tm, tn), lambda i, j, k: (i, j)),
        out_shape=jax.ShapeDtypeStruct((m, n), out_dtype),
        scratch_shapes=[pltpu.VMEM((tm, tn), F32)],
        compiler_params=_params("arbitrary", "arbitrary", "arbitrary"),
    )(*([a, b] + ([scale] if has_scale else [])))


def ada_mod(c16, w):
    ncol = w.shape[2]

    def body(c_ref, w_ref, o_ref, a_ref):
        cc = c_ref[...]
        act = cc * _sig(cc)
        a_ref[...] = act
        o_ref[...] = _raw(act, w_ref[...], _NN)

    return pl.pallas_call(
        body, name="ada_mod", grid=(LAYERS,),
        in_specs=[pl.BlockSpec((16, D), lambda l: (0, 0)), pl.BlockSpec((None, D, ncol), lambda l: (l, 0, 0))],
        out_specs=[pl.BlockSpec((None, 16, ncol), lambda l: (l, 0, 0)), pl.BlockSpec((16, D), lambda l: (0, 0))],
        out_shape=[jax.ShapeDtypeStruct((LAYERS, 16, ncol), F32), jax.ShapeDtypeStruct((16, D), F32)],
        compiler_params=_params("arbitrary"),
    )(c16, w)


def _mla_shared(q_lat, c_kv, kr, krs, qa_w, kva_w, kr_w, krs_w, cos2, sin2):
    qn = _rms(q_lat, qa_w, 384.0)
    kvn = _rms(c_kv, kva_w, 256.0)
    rk = lax.rsqrt(jnp.sum(kr * kr, axis=-1, keepdims=True) / 32.0 + EPS)
    krope = rk * (kr * kr_w * cos2 + krs * krs_w * sin2)
    return qn, kvn, krope


def _mla_head(qn, kvn, wqn, wqr, wqrs, wkn, wv, qn_w, qr_w, qrs_w, kn_w, cos2, sin2):
    qnope = _rms(mm_nn(qn, wqn), qn_w, 64.0)
    qr, qrs = mm_nn(qn, wqr), mm_nn(qn, wqrs)
    rq = lax.rsqrt(jnp.sum(qr * qr, axis=-1, keepdims=True) / 32.0 + EPS)
    qrope = rq * (qr * qr_w * cos2 + qrs * qrs_w * sin2)
    knope = _rms(mm_nn(kvn, wkn), kn_w, 64.0)
    return qnope, qrope, knope, mm_nn(kvn, wv)


def _mla_vec_pieces(v_ref):
    return ((v_ref[0:1, 0:384], v_ref[1:2, 0:256], v_ref[3:4, 128:256], v_ref[3:4, 256:384]),
            (v_ref[2:3, 0:128], v_ref[2:3, 128:256], v_ref[2:3, 256:384], v_ref[3:4, 0:128]))


def _mla_in_specs(tr):
    return [pl.BlockSpec((tr, 384), lambda i: (i, O_QL // 384)), pl.BlockSpec((tr, 256), lambda i: (i, O_CKV // 256)),
            pl.BlockSpec((tr, 128), lambda i: (i, O_KR // 128)), pl.BlockSpec((tr, 128), lambda i: (i, O_KRS // 128)),
            pl.BlockSpec((HEADS, 384, 384), lambda i: (0, 0, 0), **CONST),
            pl.BlockSpec((HEADS, 256, 256), lambda i: (0, 0, 0), **CONST),
            pl.BlockSpec((8, 512), lambda i: (0, 0)),
            pl.BlockSpec((tr, 128), lambda i: (i, 0)), pl.BlockSpec((tr, 128), lambda i: (i, 0))]


def mla_pre_fwd(proj, wq, wkv, vec, cos2, sin2, name):
    s = proj.shape[0]
    tr = _pick(s, 256)

    def body(ql_ref, ckv_ref, kr_ref, krs_ref, wq_ref, wkv_ref, v_ref, cos_ref, sin_ref, q_out, k_out, v_out):
        vshared, vhead = _mla_vec_pieces(v_ref)
        cos2_, sin2_ = cos_ref[...], sin_ref[...]
        qlat_n, kv_n, krope = _mla_shared(ql_ref[...], ckv_ref[...], kr_ref[...], krs_ref[...], *vshared, cos2_, sin2_)
        qlat_n, kv_n, krope = qlat_n.astype(BF16), kv_n.astype(BF16), krope.astype(BF16)
        for h in range(HEADS):
            ws = (wq_ref[h, :, 0:128], wq_ref[h, :, 128:256], wq_ref[h, :, 256:384],
                  wkv_ref[h, :, 0:128], wkv_ref[h, :, 128:256])
            qn, qr, kn, v = _mla_head(qlat_n, kv_n, *ws, *vhead, cos2_, sin2_)
            q_out[h, :, 0:128] = qn.astype(BF16)
            q_out[h, :, 128:256] = qr.astype(BF16)
            k_out[h, :, 0:128] = kn.astype(BF16)
            k_out[h, :, 128:256] = krope
            v_out[h] = v.astype(BF16)

    return pl.pallas_call(
        body, name=name, grid=(s // tr,), in_specs=_mla_in_specs(tr),
        out_specs=[pl.BlockSpec((HEADS, tr, 256), lambda i: (0, i, 0)), pl.BlockSpec((HEADS, tr, 256), lambda i: (0, i, 0)),
                   pl.BlockSpec((HEADS, tr, 128), lambda i: (0, i, 0))],
        out_shape=[jax.ShapeDtypeStruct((HEADS, s, 256), BF16), jax.ShapeDtypeStruct((HEADS, s, 256), BF16),
                   jax.ShapeDtypeStruct((HEADS, s, 128), BF16)],
        compiler_params=_params("arbitrary"),
    )(proj, proj, proj, proj, wq, wkv, vec, cos2, sin2)


def mla_pre_bwd(proj, wq, wkv, vec, cos2, sin2, dq, dk, dv, name):
    s = proj.shape[0]
    tr = _pick(s, 256)

    def body(ql_ref, ckv_ref, kr_ref, krs_ref, wq_ref, wkv_ref, v_ref, cos_ref, sin_ref, dq_ref, dk_ref, dv_ref,
             dql_out, dckv_out, dkr_out, dkrs_out, dwq_out, dwkv_out, dvec_out):
        @pl.when(pl.program_id(0) == 0)
        def _():
            dwq_out[...] = jnp.zeros_like(dwq_out)
            dwkv_out[...] = jnp.zeros_like(dwkv_out)
            dvec_out[...] = jnp.zeros_like(dvec_out)

        vshared, vhead = _mla_vec_pieces(v_ref)
        cos2_, sin2_ = cos_ref[...], sin_ref[...]
        fs = lambda *a: _mla_shared(*a, cos2_, sin2_)
        (qlat_n, kv_n, _), vjp_shared = jax.vjp(fs, ql_ref[...], ckv_ref[...], kr_ref[...], krs_ref[...], *vshared)

        def head(h, carry):
            wq_h, wkv_h = wq_ref[h].astype(F32), wkv_ref[h].astype(F32)
            ws = (wq_h[:, 0:128], wq_h[:, 128:256], wq_h[:, 256:384], wkv_h[:, 0:128], wkv_h[:, 128:256])
            f = lambda *a: _mla_head(*a, cos2_, sin2_)
            _, vjp = jax.vjp(f, qlat_n, kv_n, *ws, *vhead)
            dq_h, dk_h = dq_ref[h], dk_ref[h]
            g = vjp((dq_h[:, 0:128], dq_h[:, 128:256], dk_h[:, 0:128], dv_ref[h]))
            dwq_out[h, :, 0:128] += g[2]
            dwq_out[h, :, 128:256] += g[3]
            dwq_out[h, :, 256:384] += g[4]
            dwkv_out[h, :, 0:128] += g[5]
            dwkv_out[h, :, 128:256] += g[6]
            dvec_out[2:3, 0:128] += g[7]
            dvec_out[2:3, 128:256] += g[8]
            dvec_out[2:3, 256:384] += g[9]
            dvec_out[3:4, 0:128] += g[10]
            return carry[0] + g[0], carry[1] + g[1], carry[2] + dk_h[:, 128:256]

        zero = lambda w: jnp.zeros((tr, w), F32)
        dqn, dkvn, dkrope = lax.fori_loop(0, HEADS, head, (zero(384), zero(256), zero(128)))
        g = vjp_shared((dqn, dkvn, dkrope))
        dql_out[...] = g[0].astype(BF16)
        dckv_out[...] = g[1].astype(BF16)
        dkr_out[...] = g[2].astype(BF16)
        dkrs_out[...] = g[3].astype(BF16)
        dvec_out[0:1, 0:384] += g[4]
        dvec_out[1:2, 0:256] += g[5]
        dvec_out[3:4, 128:256] += g[6]
        dvec_out[3:4, 256:384] += g[7]

    hb = lambda w: pl.BlockSpec((HEADS, tr, w), lambda i: (0, i, 0))
    return pl.pallas_call(
        body, name=name, grid=(s // tr,), in_specs=_mla_in_specs(tr) + [hb(256), hb(256), hb(128)],
        out_specs=[pl.BlockSpec((tr, 384), lambda i: (i, 0)), pl.BlockSpec((tr, 256), lambda i: (i, 0)),
                   pl.BlockSpec((tr, 128), lambda i: (i, 0)), pl.BlockSpec((tr, 128), lambda i: (i, 0)),
                   pl.BlockSpec((HEADS, 384, 384), lambda i: (0, 0, 0)), pl.BlockSpec((HEADS, 256, 256), lambda i: (0, 0, 0)),
                   pl.BlockSpec((8, 512), lambda i: (0, 0))],
        out_shape=[jax.ShapeDtypeStruct((s, 384), BF16), jax.ShapeDtypeStruct((s, 256), BF16),
                   jax.ShapeDtypeStruct((s, 128), BF16), jax.ShapeDtypeStruct((s, 128), BF16),
                   jax.ShapeDtypeStruct((HEADS, 384, 384), F32), jax.ShapeDtypeStruct((HEADS, 256, 256), F32),
                   jax.ShapeDtypeStruct((8, 512), F32)],
        compiler_params=_params("arbitrary"),
    )(proj, proj, proj, proj, wq, wkv, vec, cos2, sin2, dq, dk, dv)


def _att_probs(q, kk, i, tq):
    sc = _raw(q, kk, _NT) * ATT_SCALE
    rows = lax.broadcasted_iota(jnp.int32, sc.shape, 0) + i * tq
    cols = lax.broadcasted_iota(jnp.int32, sc.shape, 1)
    sc = jnp.where(cols <= rows, sc, -jnp.inf)
    e = jnp.exp(sc - jnp.max(sc, axis=-1, keepdims=True))
    return e / jnp.sum(e, axis=-1, keepdims=True)


def mla_attn_fwd(q, k, v, name):
    s = q.shape[1]
    tq = _pick(s, 256)

    def body(q_ref, k_ref, v_ref, o_ref):
        for i in range(s // tq):
            n = (i + 1) * tq
            p = _att_probs(q_ref[i * tq:n, :], k_ref[0:n, :], i, tq)
            o_ref[i * tq:n, :] = _raw(p, v_ref[0:n, :], _NN)

    hs = lambda w: pl.BlockSpec((None, s, w), lambda h: (h, 0, 0))
    return pl.pallas_call(
        body, name=name, grid=(HEADS,), in_specs=[hs(256), hs(256), hs(128)],
        out_specs=pl.BlockSpec((s, 128), lambda h: (0, h)),
        out_shape=jax.ShapeDtypeStruct((s, HEADS * 128), F32),
        compiler_params=_params("arbitrary"),
    )(q, k, v)


def mla_attn_bwd(q, k, v, do, name):
    s = q.shape[1]
    tq = _pick(s, 256)

    def body(q_ref, k_ref, v_ref, do_ref, dq_ref, dk_ref, dv_ref):
        dk_ref[...] = jnp.zeros_like(dk_ref)
        dv_ref[...] = jnp.zeros_like(dv_ref)
        for i in range(s // tq):
            n = (i + 1) * tq
            qq, kk, vv = q_ref[i * tq:n, :], k_ref[0:n, :], v_ref[0:n, :]
            p = _att_probs(qq, kk, i, tq)
            o = _raw(p, vv, _NN)
            dout = do_ref[i * tq:n, :]
            delta = jnp.sum(dout * o, axis=-1, keepdims=True)
            dp = _raw(dout, vv, _NT)
            ds = p * (dp - delta) * ATT_SCALE
            dq_ref[i * tq:n, :] = _raw(ds, kk, _NN)
            dk_ref[0:n, :] += _raw(ds, qq, _TN)
            dv_ref[0:n, :] += _raw(p, dout, _TN)

    hs = lambda w: pl.BlockSpec((None, s, w), lambda h: (h, 0, 0))
    return pl.pallas_call(
        body, name=name, grid=(HEADS,),
        in_specs=[hs(256), hs(256), hs(128), pl.BlockSpec((s, 128), lambda h: (0, h))],
        out_specs=[hs(256), hs(256), hs(128)],
        out_shape=[jax.ShapeDtypeStruct((HEADS, s, 256), F32), jax.ShapeDtypeStruct((HEADS, s, 256), F32),
                   jax.ShapeDtypeStruct((HEADS, s, 128), F32)],
        compiler_params=_params("arbitrary"),
    )(q, k, v, do)


def _pool_windows(u, pad, s, g):
    pad[0:16, :] = jnp.zeros((16, 128), F32)
    cur, sel = u, None
    for j, k in enumerate((1, 2, 4, 8)):
        pad[16:16 + s, :] = cur
        cur = cur + pad[16 - k:16 - k + s, :]
        sel = cur if sel is None else jnp.where(g == j, cur, sel)
    return sel


def _pool_count(s, g):
    t = lax.broadcasted_iota(jnp.int32, (s, 1), 0)
    return jnp.minimum(t + 1, 2 << g).astype(F32)


def pool_fwd(proj, pw, ps, name):
    s = proj.shape[0]

    def body(u_ref, w_ref, s_ref, o_ref, pad):
        g = pl.program_id(0)
        u = u_ref[...]
        pooled = _pool_windows(u, pad, s, g) / _pool_count(s, g) - u
        o_ref[...] = _raw(pooled, w_ref[...], _NN) * s_ref[...]

    return pl.pallas_call(
        body, name=name, grid=(4,),
        in_specs=[pl.BlockSpec((s, 128), lambda g: (0, O_PU // 128 + g)), pl.BlockSpec((None, 128, 128), lambda g: (g, 0, 0)),
                  pl.BlockSpec((1, 128), lambda g: (0, g))],
        out_specs=pl.BlockSpec((s, 128), lambda g: (0, g)),
        out_shape=jax.ShapeDtypeStruct((s, 512), F32),
        scratch_shapes=[pltpu.VMEM((s + 16, 128), F32)],
        compiler_params=_params("arbitrary"),
    )(proj, pw, ps)


def pool_bwd(proj, pw, ps, do, name):
    s = proj.shape[0]

    def body(u_ref, w_ref, s_ref, do_ref, du_ref, dw_ref, ds_ref, pad):
        g = pl.program_id(0)
        u, w, dout = u_ref[...], w_ref[...], do_ref[...]
        cnt = _pool_count(s, g)
        pooled = _pool_windows(u, pad, s, g) / cnt - u
        mixed = _raw(pooled, w, _NN)
        ds_ref[...] = jnp.sum(dout * mixed, axis=0, keepdims=True)
        dmixed = dout * s_ref[...]
        dw_ref[...] = _raw(pooled, dmixed, _TN)
        dpooled = _raw(dmixed, w, _NT)
        dsel = dpooled / cnt
        pad[s:s + 16, :] = jnp.zeros((16, 128), F32)
        cur = jnp.where(g == 3, dsel, 0.0)
        for j, k in ((2, 8), (1, 4), (0, 2)):
            pad[0:s, :] = cur
            cur = cur + pad[k:k + s, :] + jnp.where(g == j, dsel, 0.0)
        pad[0:s, :] = cur
        cur = cur + pad[1:1 + s, :]
        du_ref[...] = (cur - dpooled).astype(BF16)

    return pl.pallas_call(
        body, name=name, grid=(4,),
        in_specs=[pl.BlockSpec((s, 128), lambda g: (0, O_PU // 128 + g)), pl.BlockSpec((None, 128, 128), lambda g: (g, 0, 0)),
                  pl.BlockSpec((1, 128), lambda g: (0, g)), pl.BlockSpec((s, 128), lambda g: (0, g))],
        out_specs=[pl.BlockSpec((s, 128), lambda g: (0, g)), pl.BlockSpec((None, 128, 128), lambda g: (g, 0, 0)),
                   pl.BlockSpec((1, 128), lambda g: (0, g))],
        out_shape=[jax.ShapeDtypeStruct((s, 512), BF16), jax.ShapeDtypeStruct((4, 128, 128), F32),
                   jax.ShapeDtypeStruct((1, 512), F32)],
        scratch_shapes=[pltpu.VMEM((s + 16, 128), F32)],
        compiler_params=_params("arbitrary"),
    )(proj, pw, ps, do)


def _xbc_col(i):
    return jnp.where(i < 2, O_XS // 512 + i, O_BC // 512)


def conv_fwd(proj, cw, cb, name):
    s = proj.shape[0]

    def body(x_ref, w_ref, b_ref, o_ref, t_ref, pad):
        pad[0:8, :] = jnp.zeros((8, 512), F32)
        pad[8:8 + s, :] = x_ref[...]
        y = b_ref[...] + sum(w_ref[k:k + 1, :] * pad[5 + k:5 + k + s, :] for k in range(4))
        act = y * _sig(y)
        o_ref[...] = act

        @pl.when(pl.program_id(0) < 2)
        def _():
            t_ref[...] = act.T

    return pl.pallas_call(
        body, name=name, grid=(3,),
        in_specs=[pl.BlockSpec((s, 512), lambda i: (0, _xbc_col(i))), pl.BlockSpec((4, 512), lambda i: (0, i)),
                  pl.BlockSpec((1, 512), lambda i: (0, i))],
        out_specs=[pl.BlockSpec((s, 512), lambda i: (0, i)), pl.BlockSpec((512, s), lambda i: (jnp.minimum(i, 1), 0))],
        out_shape=[jax.ShapeDtypeStruct((s, 1536), F32), jax.ShapeDtypeStruct((D, s), F32)],
        scratch_shapes=[pltpu.VMEM((s + 8, 512), F32)],
        compiler_params=_params("arbitrary"),
    )(proj, cw, cb)


def conv_bwd(proj, cw, cb, dxt, dbm, dcm, name):
    s = proj.shape[0]

    def body(x_ref, w_ref, b_ref, dxt_ref, dbm_ref, dcm_ref, dx_ref, dw_ref, db_ref, pad, pad2):
        pad[0:8, :] = jnp.zeros((8, 512), F32)
        pad[8:8 + s, :] = x_ref[...]
        y = b_ref[...] + sum(w_ref[k:k + 1, :] * pad[5 + k:5 + k + s, :] for k in range(4))
        sg = _sig(y)

        @pl.when(pl.program_id(0) < 2)
        def _():
            pad2[0:s, :] = dxt_ref[...].T

        @pl.when(pl.program_id(0) == 2)
        def _():
            pad2[0:s, 0:256] = dbm_ref[...]
            pad2[0:s, 256:512] = dcm_ref[...]

        dy = pad2[0:s, :] * (sg * (1.0 + y * (1.0 - sg)))
        db_ref[...] = jnp.sum(dy, axis=0, keepdims=True)
        for k in range(4):
            dw_ref[k:k + 1, :] = jnp.sum(dy * pad[5 + k:5 + k + s, :], axis=0, keepdims=True)
        pad2[s:s + 8, :] = jnp.zeros((8, 512), F32)
        pad2[0:s, :] = dy
        dx_ref[...] = sum(w_ref[k:k + 1, :] * pad2[3 - k:3 - k + s, :] for k in range(4)).astype(BF16)

    return pl.pallas_call(
        body, name=name, grid=(3,),
        in_specs=[pl.BlockSpec((s, 512), lambda i: (0, _xbc_col(i))), pl.BlockSpec((4, 512), lambda i: (0, i)),
                  pl.BlockSpec((1, 512), lambda i: (0, i)), pl.BlockSpec((512, s), lambda i: (jnp.minimum(i, 1), 0)),
                  pl.BlockSpec((s, 256), lambda i: (0, 0)), pl.BlockSpec((s, 256), lambda i: (0, 0))],
        out_specs=[pl.BlockSpec((s, 512), lambda i: (0, i)), pl.BlockSpec((4, 512), lambda i: (0, i)),
                   pl.BlockSpec((1, 512), lambda i: (0, i))],
        out_shape=[jax.ShapeDtypeStruct((s, 1536), BF16), jax.ShapeDtypeStruct((4, 1536), F32),
                   jax.ShapeDtypeStruct((1, 1536), F32)],
        scratch_shapes=[pltpu.VMEM((s + 8, 512), F32), pltpu.VMEM((s + 8, 512), F32)],
        compiler_params=_params("arbitrary"),
    )(proj, cw, cb, dxt, dbm, dcm)


def _ssd_chunk(xt, dtr, bm, cm, hprev, alog, dbias, dskip):
    ln = 128
    a = -jnp.exp(alog)
    dt_r = softplus(dtr + dbias)
    da_r = dt_r * a
    li = lax.broadcasted_iota(jnp.int32, (1, ln, ln), 1)
    si = lax.broadcasted_iota(jnp.int32, (1, ln, ln), 2)
    causal = si <= li
    acs_c = jnp.sum(jnp.where(causal, da_r, 0.0), axis=2, keepdims=True)
    acs_r = jnp.sum(jnp.where(li == si, acs_c, 0.0), axis=1, keepdims=True)
    acs_last = jnp.sum(da_r, axis=2, keepdims=True)
    decay = jnp.exp(jnp.where(causal, acs_c - acs_r, -jnp.inf))
    m = mm_nt(cm, bm)[None] * decay
    xdt = xt * dt_r
    y_diag = bmm_nt(xdt, m)
    bb = jnp.broadcast_to(bm[None], (8, ln, ln))
    cc = jnp.broadcast_to(cm[None], (8, ln, ln))
    states = bmm_nn(xdt * jnp.exp(acs_last - acs_r), bb)
    y_off = bmm_nt(hprev, cc) * jnp.exp(acs_r)
    hnew = hprev * jnp.exp(acs_last) + states
    return y_diag + y_off + xt * dskip, hnew


def _ssd_specs(nc, rev):
    cix = (lambda c: nc - 1 - c) if rev else (lambda c: c)
    hv = pl.BlockSpec((8, 1, 1), lambda g, c: (g, 0, 0))
    return [pl.BlockSpec((8, 64, 128), lambda g, c: (g, 0, cix(c))), pl.BlockSpec((8, 1, 128), lambda g, c: (g, 0, cix(c))),
            pl.BlockSpec((128, 128), lambda g, c: (cix(c), 8 + g)),
            pl.BlockSpec((128, 128), lambda g, c: (cix(c), 10 + g))], hv, cix


def ssd_fwd(xt, dtr, xbc, alog, dbias, dskip, name):
    s = xt.shape[2]
    nc = s // 128
    specs, hv, _ = _ssd_specs(nc, False)

    def body(x_ref, dr_ref, b_ref, c_ref, al_ref, db_ref, dk_ref, y_ref, hs_ref, h_scr):
        @pl.when(pl.program_id(1) == 0)
        def _():
            h_scr[...] = jnp.zeros_like(h_scr)
        hp = h_scr[...]
        hs_ref[...] = hp
        y, hn = _ssd_chunk(x_ref[...], dr_ref[...], b_ref[...], c_ref[...], hp, al_ref[...], db_ref[...], dk_ref[...])
        y_ref[...] = y
        h_scr[...] = hn

    return pl.pallas_call(
        body, name=name, grid=(2, nc), in_specs=specs + [hv, hv, hv],
        out_specs=[pl.BlockSpec((8, 64, 128), lambda g, c: (g, 0, c)),
                   pl.BlockSpec((None, None, 8, 64, 128), lambda g, c: (g, c, 0, 0, 0))],
        out_shape=[jax.ShapeDtypeStruct((16, 64, s), F32), jax.ShapeDtypeStruct((2, nc, 8, 64, 128), F32)],
        scratch_shapes=[pltpu.VMEM((8, 64, 128), F32)],
        compiler_params=_params("arbitrary", "arbitrary"),
    )(xt, dtr, xbc, xbc, alog, dbias, dskip)


def ssd_bwd(xt, dtr, xbc, alog, dbias, dskip, hs, dyt, name):
    s = xt.shape[2]
    nc = s // 128
    specs, hv, cix = _ssd_specs(nc, True)

    def body(x_ref, dr_ref, b_ref, c_ref, al_ref, db_ref, dk_ref, hs_ref, dy_ref,
             dx_out, ddr_out, dbm_out, dcm_out, dal_out, ddb_out, ddk_out, dh_scr):
        @pl.when(pl.program_id(1) == 0)
        def _():
            dh_scr[...] = jnp.zeros_like(dh_scr)
            dal_out[...] = jnp.zeros_like(dal_out)
            ddb_out[...] = jnp.zeros_like(ddb_out)
            ddk_out[...] = jnp.zeros_like(ddk_out)
        _, vjp = jax.vjp(_ssd_chunk, x_ref[...], dr_ref[...], b_ref[...], c_ref[...], hs_ref[...],
                         al_ref[...], db_ref[...], dk_ref[...])
        g = vjp((dy_ref[...], dh_scr[...]))
        dx_out[...] = g[0]
        ddr_out[...] = g[1]
        dbm_out[...] = g[2]
        dcm_out[...] = g[3]
        dh_scr[...] = g[4]
        dal_out[...] += g[5]
        ddb_out[...] += g[6]
        ddk_out[...] += g[7]

    return pl.pallas_call(
        body, name=name, grid=(2, nc),
        in_specs=specs + [hv, hv, hv, pl.BlockSpec((None, None, 8, 64, 128), lambda g, c: (g, cix(c), 0, 0, 0)),
                          pl.BlockSpec((8, 64, 128), lambda g, c: (g, 0, cix(c)))],
        out_specs=[pl.BlockSpec((8, 64, 128), lambda g, c: (g, 0, cix(c))), pl.BlockSpec((8, 1, 128), lambda g, c: (g, 0, cix(c))),
                   pl.BlockSpec((128, 128), lambda g, c: (cix(c), g)),
                   pl.BlockSpec((128, 128), lambda g, c: (cix(c), g)), hv, hv, hv],
        out_shape=[jax.ShapeDtypeStruct((16, 64, s), F32), jax.ShapeDtypeStruct((16, 1, s), F32),
                   jax.ShapeDtypeStruct((s, 256), F32),
                   jax.ShapeDtypeStruct((s, 256), F32)] + [jax.ShapeDtypeStruct((16, 1, 1), F32)] * 3,
        scratch_shapes=[pltpu.VMEM((8, 64, 128), F32)],
        compiler_params=_params("arbitrary", "arbitrary"),
    )(xt, dtr, xbc, xbc, alog, dbias, dskip, hs, dyt)


def _merge(oa, ob, y, z, gla, glb, glc, x, g1, nw, ea, eb, ec, eo, wba, wbb, wbc, wout):
    gated = y * (z * _sig(z))
    sq = gated * gated
    left = lax.broadcasted_iota(jnp.int32, (1, D), 1) < 512
    ms0 = jnp.sum(jnp.where(left, sq, 0.0), axis=-1, keepdims=True) / 512.0
    ms1 = jnp.sum(jnp.where(left, 0.0, sq), axis=-1, keepdims=True) / 512.0
    oc = gated * jnp.where(left, lax.rsqrt(ms0 + EPS), lax.rsqrt(ms1 + EPS)) * nw
    ya, yb, yc = mm_nc(oa, wba) + ea, mm_nc(ob, wbb) + eb, mm_nc(oc, wbc) + ec
    merged = _sig(gla) * ya + _sig(glb) * yb + _sig(glc) * yc
    x1 = x + g1 * (mm_nc(merged, wout) + eo)
    return x1, (oc, merged)


def _merge_specs(tr):
    row = lambda w: pl.BlockSpec((tr, w), lambda i: (i, 0))
    acts = [row(D), row(512), pl.BlockSpec((D, tr), lambda i: (0, i)), pl.BlockSpec((tr, D), lambda i: (i, O_Z // D)),
            pl.BlockSpec((tr, 3 * D), lambda i: (i, 0)), row(D), pl.BlockSpec((8, D), lambda i: (0, 0))]
    cst = lambda r: pl.BlockSpec((r, D), lambda i: (0, 0), **CONST)
    return acts, [cst(D), cst(512), cst(D), cst(D)], row


def merge_fwd(oa, ob, y, proj, x, mvec, wba, wbb, wbc, wout, name):
    s = x.shape[0]
    tr = _pick(s, 256)
    acts, wts, row = _merge_specs(tr)

    def body(oa_ref, ob_ref, y_ref, z_ref, gl_ref, x_ref, mv_ref, wba_ref, wbb_ref, wbc_ref, wout_ref, o_ref):
        zero = jnp.zeros((1, D), F32)
        x1, _ = _merge(oa_ref[...], ob_ref[...], y_ref[...].T, z_ref[...], gl_ref[:, 0:D], gl_ref[:, D:2 * D],
                       gl_ref[:, 2 * D:3 * D], x_ref[...], mv_ref[0:1, :], mv_ref[1:2, :], zero, zero, zero, zero,
                       wba_ref[...], wbb_ref[...], wbc_ref[...], wout_ref[...])
        o_ref[...] = x1

    return pl.pallas_call(
        body, name=name, grid=(s // tr,), in_specs=acts + wts, out_specs=row(D),
        out_shape=jax.ShapeDtypeStruct((s, D), F32), compiler_params=_params("arbitrary"),
    )(oa, ob, y, proj, proj, x, mvec, wba, wbb, wbc, wout)


def merge_bwd(oa, ob, y, proj, x, mvec, wba, wbb, wbc, wout, dx1, name):
    s = x.shape[0]
    tr = _pick(s, 128)
    acts, wts, row = _merge_specs(tr)

    def body(oa_ref, ob_ref, y_ref, z_ref, gl_ref, x_ref, mv_ref, wba_ref, wbb_ref, wbc_ref, wout_ref, dx1_ref,
             doa_o, dob_o, dy_o, dz_o, dgl_o, dx_o, dmv_o, dya_o, dyb_o, dyc_o, dpre_o, oc_o, mg_o):
        zero = jnp.zeros((tr, D), F32)
        wts_ = (wba_ref[...], wbb_ref[...], wbc_ref[...], wout_ref[...])
        f = lambda *a: _merge(*a, *wts_)
        _, vjp, (oc, merged) = jax.vjp(
            f, oa_ref[...], ob_ref[...], y_ref[...].T, z_ref[...], gl_ref[:, 0:D], gl_ref[:, D:2 * D],
            gl_ref[:, 2 * D:3 * D], x_ref[...], mv_ref[0:1, :], mv_ref[1:2, :], zero, zero, zero, zero, has_aux=True)
        g = vjp(dx1_ref[...])
        doa_o[...] = g[0]
        dob_o[...] = g[1]
        dy_o[...] = g[2].T
        dz_o[...] = g[3].astype(BF16)
        dgl_o[:, 0:D] = g[4].astype(BF16)
        dgl_o[:, D:2 * D] = g[5].astype(BF16)
        dgl_o[:, 2 * D:3 * D] = g[6].astype(BF16)
        dx_o[...] = g[7]

        @pl.when(pl.program_id(0) == 0)
        def _():
            dmv_o[...] = jnp.zeros_like(dmv_o)

        dmv_o[0:1, :] += g[8]
        dmv_o[1:2, :] += g[9]
        dya_o[...] = g[10].astype(BF16)
        dyb_o[...] = g[11].astype(BF16)
        dyc_o[...] = g[12].astype(BF16)
        dpre_o[...] = g[13].astype(BF16)
        oc_o[...] = oc.astype(BF16)
        mg_o[...] = merged.astype(BF16)

    sd = lambda w, dt: jax.ShapeDtypeStruct((s, w), dt)
    return pl.pallas_call(
        body, name=name, grid=(s // tr,), in_specs=acts + wts + [row(D)],
        out_specs=[row(D), row(512), pl.BlockSpec((D, tr), lambda i: (0, i)), row(D), row(3 * D), row(D),
                   pl.BlockSpec((8, D), lambda i: (0, 0))] + [row(D)] * 6,
        out_shape=[sd(D, F32), sd(512, F32), jax.ShapeDtypeStruct((D, s), F32), sd(D, BF16), sd(3 * D, BF16), sd(D, F32),
                   jax.ShapeDtypeStruct((8, D), F32)] + [sd(D, BF16)] * 6,
        compiler_params=_params("arbitrary"),
    )(oa, ob, y, proj, proj, x, mvec, wba, wbb, wbc, wout, dx1)


def _conv3(u_scr, w_ref, first, rows, lanes):
    return sum(w_ref[k:k + 1, :] * u_scr[first + k:first + k + rows, lanes] for k in range(3))


def _ffn_tile_specs(tf, tile):
    def at(rows, off):
        return pl.BlockSpec((rows, tf), lambda *g: (0, off + tile(*g)))

    def wt(off):
        return pl.BlockSpec((tf, D), lambda *g: (off + tile(*g), 0))
    return [wt(0), wt(FFN_NT), at(3, 0), at(3, FFN_NT), at(1, 0), at(1, FFN_NT)]


def ffn_fwd(x1, fvec, wup, cw, cb, wdn, name):
    s = x1.shape[0]
    tr, tf = _pick(s, 512), FFN_TILE
    lg, lv = slice(0, tf), slice(tf, 2 * tf)

    def body(x_ref, v_ref, wg_ref, wv_ref, cwg_ref, cwv_ref, cbg_ref, cbv_ref, wd_ref, x2_ref, h_ref, pre_ref,
             h_scr, u_scr, acc):
        i, t = pl.program_id(0), pl.program_id(1)

        @pl.when(t == 0)
        def _():
            @pl.when(i == 0)
            def _():
                h_scr[0:16, :] = jnp.zeros((16, D), BF16)

            @pl.when(i > 0)
            def _():
                h_scr[0:16, :] = h_scr[tr:tr + 16, :]

            h = (_rms(x_ref[...], v_ref[0:1, :], D) * (1.0 + v_ref[2:3, :]) + v_ref[1:2, :]).astype(BF16)
            h_scr[16:16 + tr, :] = h
            h_ref[...] = h
            acc[...] = jnp.zeros_like(acc)

        u_scr[:, lg] = _raw(h_scr[...], wg_ref[...], _NT)
        u_scr[:, lv] = _raw(h_scr[...], wv_ref[...], _NT)
        cg = _conv3(u_scr, cwg_ref, 14, tr, lg) + cbg_ref[...]
        cval = _conv3(u_scr, cwv_ref, 14, tr, lv) + cbv_ref[...]
        acc[...] += _raw(cg * _sig(cg) * cval, wd_ref[...], _NN)

        @pl.when(t == FFN_NT - 1)
        def _():
            pre_ref[...] = acc[...]
            x2_ref[...] = x_ref[...] + v_ref[3:4, :] * acc[...]

    row = pl.BlockSpec((tr, D), lambda i, t: (i, 0))
    return pl.pallas_call(
        body, name=name, grid=(s // tr, FFN_NT),
        in_specs=[row, pl.BlockSpec((8, D), lambda i, t: (0, 0))] + _ffn_tile_specs(tf, lambda i, t: t)
                 + [pl.BlockSpec((tf, D), lambda i, t: (t, 0))],
        out_specs=[row, row, row],
        out_shape=[jax.ShapeDtypeStruct((s, D), F32), jax.ShapeDtypeStruct((s, D), BF16), jax.ShapeDtypeStruct((s, D), F32)],
        scratch_shapes=[pltpu.VMEM((tr + 16, D), BF16), pltpu.VMEM((tr + 16, 2 * tf), F32), pltpu.VMEM((tr, D), F32)],
        compiler_params=_params("arbitrary", "arbitrary"),
    )(x1, fvec, wup, wup, cw, cw, cb, cb, wdn)


def ffn_bwd(h2, dx2, fvec, wup, cw, cb, wdn, name):
    s = h2.shape[0]
    tr, tf = _pick(s, 512), FFN_TILE
    ni, nb = s // tr, s // 16
    lg, lv = slice(0, tf), slice(tf, 2 * tf)

    def body(hp_ref, hm_ref, hn_ref, dm_ref, dn_ref, v_ref, wg_ref, wv_ref, cwg_ref, cwv_ref, cbg_ref, cbv_ref, wd_ref,
             dup_ref, act_ref, dcw_ref, u_scr, dc_scr):
        i = pl.program_id(1)
        hfull = jnp.concatenate([jnp.where(i > 0, hp_ref[...], jnp.zeros((16, D), BF16)), hm_ref[...],
                                 jnp.where(i < ni - 1, hn_ref[...], jnp.zeros((16, D), BF16))], axis=0)
        u_scr[:, lg] = _raw(hfull, wg_ref[...], _NT)
        u_scr[:, lv] = _raw(hfull, wv_ref[...], _NT)
        cg = _conv3(u_scr, cwg_ref, 14, tr + 16, lg) + cbg_ref[...]
        cval = _conv3(u_scr, cwv_ref, 14, tr + 16, lv) + cbv_ref[...]
        g2 = v_ref[3:4, :]
        dpre = jnp.concatenate([dm_ref[...] * g2, jnp.where(i < ni - 1, dn_ref[...], 0.0) * g2], axis=0)
        dact = _raw(dpre, wd_ref[...], _NT)
        sg = _sig(cg)
        sl = cg * sg
        dc_scr[:, lg] = dact * cval * (sg * (1.0 + cg * (1.0 - sg)))
        dc_scr[:, lv] = dact * sl
        act_ref[...] = (sl * cval)[0:tr, :].astype(BF16)

        @pl.when(i == 0)
        def _():
            dcw_ref[...] = jnp.zeros_like(dcw_ref)

        for half, lanes, cw_ref in ((0, lg, cwg_ref), (1, lv, cwv_ref)):
            dup_ref[half] = sum(cw_ref[k:k + 1, :] * dc_scr[2 - k:2 - k + tr, lanes] for k in range(3)).astype(BF16)
            dcm = dc_scr[0:tr, lanes]
            for k in range(3):
                dcw_ref[half, k:k + 1, :] += jnp.sum(dcm * u_scr[14 + k:14 + k + tr, lanes], axis=0, keepdims=True)
            dcw_ref[half, 3:4, :] += jnp.sum(dcm, axis=0, keepdims=True)

    r16 = tr // 16
    prev = lambda t, i: (jnp.maximum(i * r16 - 1, 0), 0)
    nxt = lambda t, i: (jnp.minimum((i + 1) * r16, nb - 1), 0)
    main = lambda t, i: (i, 0)
    return pl.pallas_call(
        body, name=name, grid=(FFN_NT, ni),
        in_specs=[pl.BlockSpec((16, D), prev), pl.BlockSpec((tr, D), main), pl.BlockSpec((16, D), nxt),
                  pl.BlockSpec((tr, D), main), pl.BlockSpec((16, D), nxt), pl.BlockSpec((8, D), lambda t, i: (0, 0))]
                 + _ffn_tile_specs(tf, lambda t, i: t) + [pl.BlockSpec((tf, D), lambda t, i: (t, 0))],
        out_specs=[pl.BlockSpec((2, tr, tf), lambda t, i: (0, i, t)), pl.BlockSpec((tr, tf), lambda t, i: (i, t)),
                   pl.BlockSpec((2, 8, tf), lambda t, i: (0, 0, t))],
        out_shape=[jax.ShapeDtypeStruct((2, s, FFN), BF16), jax.ShapeDtypeStruct((s, FFN), BF16),
                   jax.ShapeDtypeStruct((2, 8, FFN), F32)],
        scratch_shapes=[pltpu.VMEM((tr + 32, 2 * tf), F32), pltpu.VMEM((tr + 16, 2 * tf), F32)],
        compiler_params=_params("arbitrary", "arbitrary"),
    )(h2, h2, h2, dx2, dx2, fvec, wup, wup, cw, cw, cb, cb, wdn)


def loss_head(y, target):
    s = y.shape[0]
    tr = _pick(s, 512)

    def body(y_ref, t_ref, dx_ref, l_ref):
        @pl.when(pl.program_id(0) == 0)
        def _():
            l_ref[...] = jnp.zeros_like(l_ref)
        err = y_ref[...] - t_ref[...]
        dx_ref[...] = err / float(D)
        l_ref[...] += 0.5 * jnp.sum(jnp.sum(err * err, axis=-1, keepdims=True) / float(D), axis=0, keepdims=True)

    row = pl.BlockSpec((tr, D), lambda i: (i, 0))
    return pl.pallas_call(
        body, name="loss_head", grid=(s // tr,), in_specs=[row, row],
        out_specs=[row, pl.BlockSpec((8, 128), lambda i: (0, 0))],
        out_shape=[jax.ShapeDtypeStruct((s, D), F32), jax.ShapeDtypeStruct((8, 128), F32)],
        compiler_params=_params("arbitrary"),
    )(y, target)


def adamw(parts, w, m, v, name, tok=None):
    nseg = len(parts)
    p, r, c = parts[0].shape
    tok = jnp.zeros((8, 128), F32) if tok is None else tok
    cap = 256 if c > 128 else 2048
    step = lambda q, l, i, ni: jnp.clip((l - q) * ni + i, 0, ni - 1)
    if r <= cap or any(r % t == 0 for t in range(8, cap + 1, 8)):
        tr, tc = _pick(r, cap, 8), c
        ni = r // tr
        row = pl.BlockSpec((None, tr, tc), lambda l, i: (l, i, 0))
        part = lambda q: pl.BlockSpec((p, tr, tc), lambda l, i: (0, step(q, l, i, ni), 0))
    else:
        tr, tc = r, _pick(c, 256)
        ni = c // tc
        row = pl.BlockSpec((None, tr, tc), lambda l, i: (l, 0, i))
        part = lambda q: pl.BlockSpec((p, tr, tc), lambda l, i: (0, 0, step(q, l, i, ni)))

    def body(*refs):
        p_refs = refs[:nseg]
        w_ref, m_ref, v_ref, _, g_out, d_out, m_out, v_out, g_scr = refs[nseg:]
        for q in range(nseg):
            @pl.when(pl.program_id(0) == q)
            def _(q=q):
                g = p_refs[q][0].astype(F32)
                for j in range(1, p):
                    g = g + p_refs[q][j].astype(F32)
                g_scr[...] = g
        g = g_scr[...]
        mn = B1 * m_ref[...] + (1.0 - B1) * g
        vn = B2 * v_ref[...] + (1.0 - B2) * (g * g)
        m_hat = mn / (1.0 - B1 ** STEP)
        v_hat = vn / (1.0 - B2 ** STEP)
        g_out[...] = g
        d_out[...] = -LR * (m_hat / (jnp.sqrt(v_hat) + ADAM_EPS) + WD * w_ref[...])
        m_out[...] = mn
        v_out[...] = vn

    return pl.pallas_call(
        body, name=name, grid=(nseg, ni),
        in_specs=[part(q) for q in range(nseg)] + [row, row, row, pl.BlockSpec((8, 128), lambda l, i: (0, 0))],
        out_specs=[row] * 4, out_shape=[jax.ShapeDtypeStruct((nseg, r, c), F32)] * 4,
        scratch_shapes=[pltpu.VMEM((tr, tc), F32)],
        compiler_params=_params("arbitrary", "arbitrary"),
    )(*parts, w, m, v, tok)


def _padc(a, n):
    return jnp.pad(a, [(0, 0)] * (a.ndim - 1) + [(0, n - a.shape[-1])])


def _swap16(a):
    return jnp.concatenate([a[..., 16:32], a[..., 0:16]], axis=-1)


def _shard_cols(g8, a, b):
    c = g8.shape[2]
    return [g8[j][:, max(a, j * c) - j * c:min(b, (j + 1) * c) - j * c] for j in range(a // c, (b - 1) // c + 1)]


def _padr(a, n):
    return jnp.pad(a, ((0, n - a.shape[0]), (0, 0)))


def _swap16r(a):
    return jnp.concatenate([a[16:32], a[0:16]], axis=0)


def _win_layout(g8):
    w = g8.reshape(NDEV * g8.shape[1], g8.shape[2])
    kr = w[640:672]
    return jnp.concatenate([w[3760:6832], w[2208:3232], w[1184:2208], w[672:1184], w[3232:3744], w[384:640],
                            _padr(kr, 128), _padr(_swap16r(kr), 128), _padr(w[3744:3760], 128),
                            jnp.zeros((128, w.shape[1]), w.dtype), w[0:384]], axis=0)


def _win_grad_shards(g):
    kr = (g[O_KR:O_KR + 32].astype(F32) + _swap16r(g[O_KRS:O_KRS + 32].astype(F32))).astype(g.dtype)
    segs = [(g, O_QL, 384), (g, O_CKV, 256), (kr, 0, 32), (g, O_PU, 512), (g, O_Z, D), (g, O_XS, D), (g, O_BC, 512),
            (g, O_DT, 16), (g, O_G, 3 * D)]
    shards, height = [], sum(w for _, _, w in segs) // NDEV
    for j in range(NDEV):
        a, b, off, pieces = height * j, height * (j + 1), 0, []
        for arr, lo, w in segs:
            s0, s1 = max(a, off), min(b, off + w)
            if s0 < s1:
                pieces.append(arr[lo + s0 - off:lo + s1 - off])
            off += w
        shards.append(jnp.concatenate(pieces, axis=0))
    return jnp.stack(shards).astype(BF16)


def _wq_layout(w):
    w = w.reshape(384, HEADS, 96).transpose(1, 0, 2)
    rope = w[:, :, 64:96]
    return jnp.concatenate([_padc(w[:, :, 0:64], 128), _padc(rope, 128), _padc(_swap16(rope), 128)], axis=2)


def _wq_unlayout(g):
    rope = g[:, :, 128:160] + _swap16(g[:, :, 256:288])
    return jnp.concatenate([g[:, :, 0:64], rope], axis=2).transpose(1, 0, 2).reshape(384, HEADS * 96)


def _wkv_layout(w):
    w = w.reshape(256, HEADS, 128).transpose(1, 0, 2)
    return jnp.concatenate([_padc(w[:, :, 0:64], 128), _padc(w[:, :, 64:128], 128)], axis=2)


def _wkv_unlayout(g):
    return jnp.concatenate([g[:, :, 0:64], g[:, :, 128:192]], axis=2).transpose(1, 0, 2).reshape(256, HEADS * 128)


def _wba_layout(w):
    return jnp.pad(w.reshape(HEADS, 64, D), ((0, 0), (0, 64), (0, 0))).reshape(HEADS * 128, D)


def _rows8(rows, width):
    out = jnp.stack([_padc(r.astype(F32), width) for r in rows])
    return jnp.pad(out, ((0, 8 - out.shape[0]), (0, 0)))


def _mla_vec(qa, kva, qn, kn):
    def row(n):
        return jnp.concatenate([_padc(n[0:64], 128), _padc(n[64:96], 128), _padc(_swap16(n[64:96]), 128)])
    return _rows8([qa, kva, row(qn), row(kn)], 512)


def _mla_unvec(g):
    def un(r):
        return jnp.concatenate([r[0:64], r[128:160] + _swap16(r[256:288])])
    return g[0, 0:384], g[1, 0:256], un(g[2]), un(g[3])


SMALL = (("ada_b", (6 * D,)), ("norm1_w", (D,)), ("q_a_norm", (384,)), ("kv_a_norm", (256,)), ("q_norm", (96,)),
         ("k_norm", (96,)), ("pool_w", (4, 128, 128)), ("pool_scale", (512,)), ("ssd_conv_b", (1536,)),
         ("ssd_dt_bias", (16,)), ("ssd_a_log", (16,)), ("ssd_d", (16,)), ("ssd_norm_w", (D,)), ("norm2_w", (D,)),
         ("ffn_conv_b", (2 * FFN,)), ("ssd_conv_w", (4, 1536)), ("ffn_conv_w", (3, 2 * FFN)))
SHARDED_SMALL = {"ssd_conv_w": 192, "ffn_conv_w": 704}


def _pack_rows(shp):
    return -(-math.prod(shp) // 1024) * 8


def _pack(small):
    pieces = []
    for n, shp in SMALL:
        pieces.append(small[n].reshape(-1).astype(F32))
        fill = _pack_rows(shp) * 128 - math.prod(shp)
        if fill:
            pieces.append(jnp.zeros((fill,), F32))
    return jnp.concatenate(pieces).reshape(-1, 128)


def _unpack_parts(packs):
    out, off = {}, 0
    for n, shp in SMALL:
        rows, size = _pack_rows(shp), math.prod(shp)
        r, c = math.prod(shp[:-1]), shp[-1]
        per_layer = [pk[:, off:off + rows].reshape(NDEV, rows * 128)[:, 0:size].reshape(NDEV, r, c) for pk in packs]
        out[n] = jnp.concatenate(per_layer, axis=1)
        off += rows
    return out


GROUP_A = ("w_in", "w_q_b", "w_kv_b")
GROUP_B = ("w_branch", "w_out", "ffn_up", "ffn_down")
BIG = GROUP_A + GROUP_B
SCATTER_FFN, SCATTER_MERGE = ("ffn_up", "ffn_down"), ("w_branch", "w_out")
COL_SHARDED = ("w_q_b", "w_kv_b")
TRANSPOSED = ("w_in", "ffn_up")


def _behind(arrs, tok):
    arrs = list(arrs)
    j = min(range(len(arrs)), key=lambda q: arrs[q].size)
    arrs[j] = arrs[j] + tok[0, 0].astype(arrs[j].dtype)
    return arrs


def _gathered_full(g, name):
    if name in COL_SHARDED:
        return g.transpose(1, 0, 2).reshape(g.shape[1], NDEV * g.shape[2])
    return g.reshape(NDEV * g.shape[1], g.shape[2])


def _to_shards(full, name):
    if name == "w_in":
        return _win_grad_shards(full)
    if name in COL_SHARDED:
        r, c = full.shape
        return full.reshape(r, NDEV, c // NDEV).transpose(1, 0, 2).astype(BF16)
    r, c = full.shape
    return full.reshape(NDEV, r // NDEV, c).astype(BF16)


def _fwd_a(x, lw, mod, cos2, sin2, l, tok):
    sh1, sc1, g1, sh2, sc2, g2 = [mod[j * D:(j + 1) * D] for j in range(6)]
    vec1 = _rows8([lw["norm1_w"], sh1, sc1], D) + tok[0, 0]
    proj, h1, dt_cols = norm_proj_fwd(x, vec1, lw["win"], f"inproj_fwd{l}")
    q, k, v = mla_pre_fwd(proj, lw["wq"], lw["wkv"], lw["mla_vec"], cos2, sin2, f"mla_pre_fwd{l}")
    oa = mla_attn_fwd(q, k, v, f"mla_attn_fwd{l}")
    ob = pool_fwd(proj, lw["pool_w"], lw["pool_scale"].reshape(1, 512), f"pool_fwd{l}")
    xbc, xt = conv_fwd(proj, lw["ssd_conv_w"], lw["ssd_conv_b"].reshape(1, 1536), f"conv_fwd{l}")
    s = x.shape[0]
    xt = xt.reshape(16, 64, s)
    dt = dt_cols[:, 0:16].T
    dtr = dt[:, None, :]
    hv = lambda a: a.reshape(16, 1, 1)
    yt, hs = ssd_fwd(xt, dtr, xbc, hv(lw["ssd_a_log"]), hv(lw["ssd_dt_bias"]), hv(lw["ssd_d"]), f"ssd_fwd{l}")
    return dict(x=x, vec1=vec1, proj=proj, h1=h1, q=q, k=k, v=v, oa=oa, ob=ob, xbc=xbc, xt=xt, dtr=dtr,
                hs=hs, yt=yt.reshape(D, s), mvec=_rows8([g1, lw["ssd_norm_w"]], D),
                fvec=_rows8([lw["norm2_w"], sh2, sc2, g2], D))


def _fwd_b(sv, lw, l, tok):
    sv["mvec"] = sv["mvec"] + tok[0, 0]
    x1 = merge_fwd(sv["oa"], sv["ob"], sv["yt"], sv["proj"], sv["x"], sv["mvec"], lw["wba"], lw["wbb"], lw["wbc"],
                   lw["wout"], f"merge_fwd{l}")
    x2, h2, pre = ffn_fwd(x1, sv["fvec"], lw["wup"], lw["ffn_conv_w"], lw["ffn_conv_b"].reshape(1, 2 * FFN), lw["wdn"],
                          f"ffn_fwd{l}")
    sv.update(x1=x1, h2=h2, pre=pre)
    return x2


def _bwd_ffn(dx2, lw, sv, l, tok):
    fvec = sv["fvec"] + tok[0, 0]
    dup, act, dcw = ffn_bwd(sv["h2"], dx2, fvec, lw["wup"], lw["ffn_conv_w"], lw["ffn_conv_b"].reshape(1, 2 * FFN),
                            lw["wdn"], f"ffn_bwd{l}")
    grads = dict(ffn_down=tn_matmul(act, dx2, f"dw_down{l}", scale=fvec[3:4]),
                 ffn_up=tn_matmul(dup, sv["h2"], f"dw_up{l}"))
    small = dict(ffn_conv_w=jnp.concatenate([dcw[0, 0:3], dcw[1, 0:3]], axis=1),
                 ffn_conv_b=jnp.concatenate([dcw[0, 3], dcw[1, 3]]))
    return dup, grads, small


def _bwd_merge(dx2, dup, lw, sv, l, tok, small):
    grads = {}
    fvec = sv["fvec"] + tok[0, 0]
    dx1, dfvec = norm_proj_bwd(sv["x1"], fvec, dup, lw["wup"], dx2, sv["pre"], f"ffn_norm_bwd{l}")
    small["norm2_w"] = dfvec[0]
    (doa, dob, dyt, dz, dgl, dx, dmvec, dya, dyb, dyc, dpre, oc, merged) = merge_bwd(
        sv["oa"], sv["ob"], sv["yt"], sv["proj"], sv["x"], sv["mvec"], lw["wba"], lw["wbb"], lw["wbc"], lw["wout"], dx1,
        f"merge_bwd{l}")
    dwba = tn_matmul(sv["oa"], dya, f"dw_ba{l}").reshape(HEADS, 128, D)[:, 0:64].reshape(512, D)
    grads["w_branch"] = jnp.concatenate([dwba, tn_matmul(sv["ob"], dyb, f"dw_bb{l}"), tn_matmul(oc, dyc, f"dw_bc{l}")])
    grads["w_out"] = tn_matmul(merged, dpre, f"dw_out{l}")
    small["ssd_norm_w"] = dmvec[1]
    small["dmod_b"] = (dmvec[0], dfvec[1], dfvec[2], dfvec[3])
    return dx, dict(doa=doa, dob=dob, dyt=dyt, dz=dz, dgl=dgl), grads, small


def _bwd_a(dx, cot, lw, sv, cos2, sin2, l, tok, small, after_scan):
    s = dx.shape[0]
    grads = {}
    doa, dob, dz, dgl = cot["doa"], cot["dob"], cot["dz"], cot["dgl"]
    hv = lambda a: a.reshape(16, 1, 1)
    dxt, ddtr, dbm, dcm, dal, ddb, ddk = ssd_bwd(
        sv["xt"], sv["dtr"], sv["xbc"], hv(lw["ssd_a_log"]) + tok[0, 0], hv(lw["ssd_dt_bias"]),
        hv(lw["ssd_d"]), sv["hs"], cot["dyt"].reshape(16, 64, s), f"ssd_bwd{l}")
    small["ssd_a_log"], small["ssd_dt_bias"], small["ssd_d"] = dal.reshape(16), ddb.reshape(16), ddk.reshape(16)
    tok = after_scan(dxt)
    dxbc, dscw, dscb = conv_bwd(sv["proj"], lw["ssd_conv_w"], lw["ssd_conv_b"].reshape(1, 1536) + tok[0, 0],
                                dxt.reshape(D, s), dbm, dcm, f"conv_bwd{l}")
    small["ssd_conv_w"], small["ssd_conv_b"] = dscw, dscb.reshape(1536)
    ddt = ddtr[:, 0, :].T
    du, dpw, dps = pool_bwd(sv["proj"], lw["pool_w"], lw["pool_scale"].reshape(1, 512), dob, f"pool_bwd{l}")
    small["pool_w"], small["pool_scale"] = dpw, dps.reshape(512)
    dq, dk, dv = mla_attn_bwd(sv["q"], sv["k"], sv["v"], doa, f"mla_attn_bwd{l}")
    dql, dckv, dkr, dkrs, dwq, dwkv, dmv = mla_pre_bwd(sv["proj"], lw["wq"], lw["wkv"], lw["mla_vec"], cos2, sin2,
                                                       dq, dk, dv, f"mla_pre_bwd{l}")
    grads["w_q_b"], grads["w_kv_b"] = _wq_unlayout(dwq), _wkv_unlayout(dwkv)
    small["q_a_norm"], small["kv_a_norm"], small["q_norm"], small["k_norm"] = _mla_unvec(dmv)
    dproj = jnp.concatenate([dgl, dxbc[:, 0:D], dz, du, dxbc[:, D:1536], dckv, dkr, dkrs,
                             _padc(ddt, 128).astype(BF16), jnp.zeros((s, 128), BF16), dql], axis=1)
    grads["w_in"] = tn_matmul(dproj, sv["h1"], f"dw_in{l}")
    return dproj, grads, small


def _bwd_in(dx, dproj, lw, sv, l, tok, small):
    dx0, dvec1 = norm_proj_bwd(sv["x"], sv["vec1"] + tok[0, 0], dproj, lw["win"], dx, None, f"inproj_bwd{l}")
    small["norm1_w"] = dvec1[0]
    small["ada_b"] = jnp.concatenate([dvec1[1], dvec1[2], *small.pop("dmod_b")])
    return dx0, small


def kernel(x, c, positions, ada_w, ada_b, norm1_w, w_in, q_a_norm, w_q_b, kv_a_norm, w_kv_b, q_norm, k_norm, pool_w, pool_scale, ssd_conv_w, ssd_conv_b, ssd_dt_bias, ssd_a_log, ssd_d, ssd_norm_w, w_branch, w_out, norm2_w, ffn_up, ffn_conv_w, ffn_conv_b, ffn_down, loss_target, m_ada_w, m_ada_b, m_norm1_w, m_w_in, m_q_a_norm, m_w_q_b, m_kv_a_norm, m_w_kv_b, m_q_norm, m_k_norm, m_pool_w, m_pool_scale, m_ssd_conv_w, m_ssd_conv_b, m_ssd_dt_bias, m_ssd_a_log, m_ssd_d, m_ssd_norm_w, m_w_branch, m_w_out, m_norm2_w, m_ffn_up, m_ffn_conv_w, m_ffn_conv_b, m_ffn_down, v_ada_w, v_ada_b, v_norm1_w, v_w_in, v_q_a_norm, v_w_q_b, v_kv_a_norm, v_w_kv_b, v_q_norm, v_k_norm, v_pool_w, v_pool_scale, v_ssd_conv_w, v_ssd_conv_b, v_ssd_dt_bias, v_ssd_a_log, v_ssd_d, v_ssd_norm_w, v_w_branch, v_w_out, v_norm2_w, v_ffn_up, v_ffn_conv_w, v_ffn_conv_b, v_ffn_down):
    p = dict(ada_w=ada_w, ada_b=ada_b, norm1_w=norm1_w, w_in=w_in, q_a_norm=q_a_norm, w_q_b=w_q_b, kv_a_norm=kv_a_norm,
             w_kv_b=w_kv_b, q_norm=q_norm, k_norm=k_norm, pool_w=pool_w, pool_scale=pool_scale, ssd_conv_w=ssd_conv_w,
             ssd_conv_b=ssd_conv_b, ssd_dt_bias=ssd_dt_bias, ssd_a_log=ssd_a_log, ssd_d=ssd_d, ssd_norm_w=ssd_norm_w,
             w_branch=w_branch, w_out=w_out, norm2_w=norm2_w, ffn_up=ffn_up, ffn_conv_w=ffn_conv_w, ffn_conv_b=ffn_conv_b,
             ffn_down=ffn_down)
    mom = dict(ada_w=m_ada_w, ada_b=m_ada_b, norm1_w=m_norm1_w, w_in=m_w_in, q_a_norm=m_q_a_norm, w_q_b=m_w_q_b,
               kv_a_norm=m_kv_a_norm, w_kv_b=m_w_kv_b, q_norm=m_q_norm, k_norm=m_k_norm, pool_w=m_pool_w,
               pool_scale=m_pool_scale, ssd_conv_w=m_ssd_conv_w, ssd_conv_b=m_ssd_conv_b, ssd_dt_bias=m_ssd_dt_bias,
               ssd_a_log=m_ssd_a_log, ssd_d=m_ssd_d, ssd_norm_w=m_ssd_norm_w, w_branch=m_w_branch, w_out=m_w_out,
               norm2_w=m_norm2_w, ffn_up=m_ffn_up, ffn_conv_w=m_ffn_conv_w, ffn_conv_b=m_ffn_conv_b, ffn_down=m_ffn_down)
    var = dict(ada_w=v_ada_w, ada_b=v_ada_b, norm1_w=v_norm1_w, w_in=v_w_in, q_a_norm=v_q_a_norm, w_q_b=v_w_q_b,
               kv_a_norm=v_kv_a_norm, w_kv_b=v_w_kv_b, q_norm=v_q_norm, k_norm=v_k_norm, pool_w=v_pool_w,
               pool_scale=v_pool_scale, ssd_conv_w=v_ssd_conv_w, ssd_conv_b=v_ssd_conv_b, ssd_dt_bias=v_ssd_dt_bias,
               ssd_a_log=v_ssd_a_log, ssd_d=v_ssd_d, ssd_norm_w=v_ssd_norm_w, w_branch=v_w_branch, w_out=v_w_out,
               norm2_w=v_norm2_w, ffn_up=v_ffn_up, ffn_conv_w=v_ffn_conv_w, ffn_conv_b=v_ffn_conv_b, ffn_down=v_ffn_down)
    names = list(p)
    me = 4 * lax.axis_index("x") + 2 * lax.axis_index("y") + lax.axis_index("c")
    xs, tgt = x[0], loss_target[0]
    s = xs.shape[0]

    inv_freq = ROPE_THETA ** (-jnp.arange(0, 32, 2, dtype=F32) / 32.0)
    ang = positions[0].astype(F32)[:, None] * inv_freq
    cos, sin = jnp.cos(ang), jnp.sin(ang)
    cos2 = _padc(jnp.concatenate([cos, cos], axis=1), 128)
    sin2 = _padc(jnp.concatenate([-sin, sin], axis=1), 128)

    conv_shards = jnp.concatenate([ssd_conv_w.reshape(-1), ffn_conv_w.reshape(-1)])
    (c_all, conv_all), _ = all_to_all([c, conv_shards], [True, True], "gather_c")
    modp, cact = ada_mod(jnp.pad(c_all.reshape(NDEV, D), ((0, 8), (0, 0))), ada_w)
    (mod_in,), tok = all_to_all([modp[:, 0:NDEV].transpose(1, 0, 2)], [False], "scatter_mod")
    mod = mod_in.transpose(1, 0, 2).reshape(LAYERS, 6 * D) + ada_b

    n1 = LAYERS * 4 * 192
    scw = conv_all[:, :n1].reshape(NDEV, LAYERS, 4, 192).transpose(1, 2, 0, 3).reshape(LAYERS, 4, 1536)
    fcw = conv_all[:, n1:].reshape(NDEV, LAYERS, 3, 704).transpose(1, 2, 0, 3).reshape(LAYERS, 3, 2 * FFN)

    def weights_a(gathered, l):
        full = {n: _gathered_full(g, n) for n, g in zip(GROUP_A[1:], gathered[1:])}
        lw = {n: p[n][l] for n in names}
        lw.update(win=_win_layout(gathered[0]), wq=_wq_layout(full["w_q_b"]), wkv=_wkv_layout(full["w_kv_b"]),
                  ssd_conv_w=scw[l], ffn_conv_w=fcw[l],
                  mla_vec=_mla_vec(lw["q_a_norm"], lw["kv_a_norm"], lw["q_norm"], lw["k_norm"]))
        return lw

    def weights_b(gathered):
        full = {n: _gathered_full(g, n) for n, g in zip(GROUP_B, gathered)}
        wb = full["w_branch"]
        return dict(wba=_wba_layout(wb[0:512]), wbb=wb[512:1024], wbc=wb[1024:2048], wout=full["w_out"],
                    wup=full["ffn_up"], wdn=full["ffn_down"])

    shards = lambda group, l: [(p[n][l].T if n in TRANSPOSED else p[n][l]).astype(BF16) for n in group]
    lws, saved = [None] * LAYERS, [None] * LAYERS
    st, tok = gather_start(_behind(shards(GROUP_A, 0), tok), "gather_a0")
    got, tok = gather_finish(st, tok, "gather_a0")
    h = xs
    for l in range(LAYERS):
        st, tok = gather_start(_behind(shards(GROUP_B, l), tok), f"gather_b{l}")
        lws[l] = weights_a(got, l)
        saved[l] = _fwd_a(h, lws[l], mod[l], cos2, sin2, l, tok)
        got, tok = gather_finish(st, saved[l]["yt"], f"gather_b{l}")
        lws[l].update(weights_b(got))
        if l + 1 < LAYERS:
            st, tok = gather_start(_behind(shards(GROUP_A, l + 1), tok), f"gather_a{l + 1}")
        h = _fwd_b(saved[l], lws[l], l, tok)
        if l + 1 < LAYERS:
            got, tok = gather_finish(st, h, f"gather_a{l + 1}")
    dx, lpart = loss_head(h, tgt)
    loss = lax.psum(lpart[0, 0], ("x", "y", "c"))
    tok = tok + loss * 0.0

    small, parts, packs = [None] * LAYERS, {n: [None] * LAYERS for n in BIG}, [None] * LAYERS
    st = None

    def scatter(grads, group, l, tok, extra=None):
        arrs, flags = [_to_shards(grads[n], n) for n in group], [False] * len(group)
        if extra is not None:
            arrs, flags = arrs + [extra], flags + [True]
        return exchange_start(_behind(arrs, tok), flags, f"scatter_{group[0]}{l}_start")

    def landed(state, group, l, after):
        got, tok, _ = exchange_wait(state, after, f"scatter_{group[0]}{l}_wait")
        for n, g in zip(group, got):
            parts[n][l] = g
        return got, tok

    for l in reversed(range(LAYERS)):
        dup, g_ffn, small[l] = _bwd_ffn(dx, lws[l], saved[l], l, tok)
        if st is not None:
            _, tok = landed(st, GROUP_A, l + 1, dup)
        st, tok = scatter(g_ffn, SCATTER_FFN, l, tok)
        dx, cot, g_merge, small[l] = _bwd_merge(dx, dup, lws[l], saved[l], l, tok, small[l])
        pack, nxt = (_pack(small[l + 1]) if l + 1 < LAYERS else None), {}

        def after_scan(after, st=st, l=l, g_merge=g_merge, pack=pack, nxt=nxt):
            _, t = landed(st, SCATTER_FFN, l, after)
            nxt["state"], t = scatter(g_merge, SCATTER_MERGE, l, t, pack)
            return t

        dproj, g_in, small[l] = _bwd_a(dx, cot, lws[l], saved[l], cos2, sin2, l, tok, small[l], after_scan)
        got, tok = landed(nxt["state"], SCATTER_MERGE, l, dproj)
        if l + 1 < LAYERS:
            packs[l + 1] = got[-1]
        st, tok = scatter(g_in, GROUP_A, l, tok)
        dx, small[l] = _bwd_in(dx, dproj, lws[l], saved[l], l, tok, small[l])

    dmod = jnp.stack([small[q]["ada_b"] for q in range(LAYERS)])
    st_small, tok = exchange_start(_behind([_pack(small[0]), dmod.reshape(LAYERS, NDEV, 768).transpose(1, 0, 2)], tok),
                                   [True, False], "scatter_s0_start")
    out = {}

    def big_adamw(group, tok):
        res = None
        for n in group:
            t = (lambda a: a.transpose(0, 2, 1)) if n in TRANSPOSED else (lambda a: a)
            res = adamw(parts[n], t(p[n]), t(mom[n]), t(var[n]), f"adamw_{n}", tok)
            out[n] = [t(a) for a in res]
        return res[0]

    g_last = big_adamw(GROUP_B, tok)
    _, tok = landed(st, GROUP_A, 0, g_last)
    (packs[0], dmod_in), _, _ = exchange_wait(st_small, tok, "scatter_s0_wait")
    big_adamw(GROUP_A, None)

    dmod16 = jnp.pad(dmod_in, ((0, 8), (0, 0), (0, 0)))
    g_ada = [tn_matmul(cact, dmod16[:, l], f"dw_ada{l}", out_dtype=F32)[None] for l in range(LAYERS)]
    out["ada_w"] = adamw(g_ada, ada_w, m_ada_w, v_ada_w, "adamw_ada_w")

    for n, pt in _unpack_parts(packs).items():
        if n in SHARDED_SMALL:
            w = SHARDED_SMALL[n]
            pt = lax.dynamic_slice_in_dim(pt, me * w, w, axis=2)
        r, c = pt.shape[1:]
        res = adamw([pt], p[n].reshape(1, r, c), mom[n].reshape(1, r, c), var[n].reshape(1, r, c), f"adamw_{n}")
        out[n] = [a.reshape(p[n].shape) for a in res]

    outs = [loss, dx[None]]
    for q in range(4):
        outs += [out[n][q] for n in names]
    return tuple(outs)
```

```python
import functools
import math

import jax
import jax.numpy as jnp
from jax import lax
from jax.experimental import pallas as pl
from jax.experimental.pallas import tpu as pltpu

F32, BF16 = jnp.float32, jnp.bfloat16
EPS = 1e-6
D = 1024
NDEV = 8
LAYERS = 2
HEADS = 8
FFN = 2816
FFN_TILE = 1408
FFN_NT = FFN // FFN_TILE
ATT_SCALE = 96 ** -0.5
ROPE_THETA = 10000.0
LR, B1, B2, ADAM_EPS, WD, STEP = 0.001, 0.9, 0.999, 1e-08, 0.01, 10

O_G, O_XS, O_Z, O_PU, O_BC, O_CKV, O_KR, O_KRS, O_DT, O_QL = 0, 3072, 4096, 5120, 5632, 6144, 6400, 6528, 6656, 6912
NPROJ = 7296
CONST = dict(pipeline_mode=pl.Buffered(1))


def _pick(n, cap, mult=128):
    if n <= cap:
        return n
    best = None
    for t in range(mult, cap + 1, mult):
        if n % t == 0:
            best = t
    assert best is not None, (n, cap, mult)
    return best


def _sig(x):
    return 1.0 / (1.0 + jnp.exp(-x))


def _rms(x, w, n):
    return x * lax.rsqrt(jnp.sum(x * x, axis=-1, keepdims=True) / n + EPS) * w


def _raw(a, b, dims):
    return lax.dot_general(a.astype(BF16), b.astype(BF16), dims, preferred_element_type=F32)


_NN = (((1,), (0,)), ((), ()))
_NT = (((1,), (1,)), ((), ()))
_TN = (((0,), (0,)), ((), ()))
_BNN = (((2,), (1,)), ((0,), (0,)))
_BNT = (((2,), (2,)), ((0,), (0,)))
_BTN = (((1,), (1,)), ((0,), (0,)))


@jax.custom_vjp
def mm_nn(a, b):
    return _raw(a, b, _NN)


mm_nn.defvjp(lambda a, b: (_raw(a, b, _NN), (a, b)),
             lambda r, g: (_raw(g, r[1], _NT), _raw(r[0], g, _TN)))


@jax.custom_vjp
def mm_nc(a, b):
    return _raw(a, b, _NN)


mm_nc.defvjp(lambda a, b: (_raw(a, b, _NN), b),
             lambda b, g: (_raw(g, b, _NT), jnp.zeros_like(b)))


@jax.custom_vjp
def mm_nt(a, b):
    return _raw(a, b, _NT)


mm_nt.defvjp(lambda a, b: (_raw(a, b, _NT), (a, b)),
             lambda r, g: (_raw(g, r[1], _NN), _raw(g, r[0], _TN)))


@jax.custom_vjp
def bmm_nn(a, b):
    return _raw(a, b, _BNN)


bmm_nn.defvjp(lambda a, b: (_raw(a, b, _BNN), (a, b)),
              lambda r, g: (_raw(g, r[1], _BNT), _raw(r[0], g, _BTN)))


@jax.custom_vjp
def bmm_nt(a, b):
    return _raw(a, b, _BNT)


bmm_nt.defvjp(lambda a, b: (_raw(a, b, _BNT), (a, b)),
              lambda r, g: (_raw(g, r[1], _BNN), _raw(g, r[0], _BTN)))


@jax.custom_vjp
def softplus(x):
    t = jnp.exp(-jnp.abs(x))
    u = 1.0 + t
    one = u == 1.0
    l1p = jnp.where(one, t, jnp.log(u) * (t / jnp.where(one, 1.0, u - 1.0)))
    return jnp.maximum(x, 0.0) + l1p


softplus.defvjp(lambda x: (softplus(x), x), lambda x, g: (g * _sig(x),))


def _params(*sem):
    return pltpu.CompilerParams(dimension_semantics=sem, vmem_limit_bytes=56 * 1024 * 1024)


def all_to_all(arrs, bcast, name):
    n = len(arrs)
    out_shapes = [jax.ShapeDtypeStruct(((NDEV,) + a.shape) if b else a.shape, a.dtype) for a, b in zip(arrs, bcast)]

    def body(*refs):
        ins, outs, token = refs[:n], refs[n:2 * n], refs[2 * n]
        send_sems, recv_sems, local_sems = refs[2 * n + 1:]
        me, remote = _exchange_copies(ins, outs, bcast, send_sems, recv_sems)
        local = [pltpu.make_async_copy(ins[j] if bcast[j] else ins[j].at[me], outs[j].at[me], local_sems.at[j])
                 for j in range(n)]
        for cp in local + remote:
            cp.start()
        for cp in remote + local:
            cp.wait()
        token[...] = jnp.zeros_like(token)

    any_spec = pl.BlockSpec(memory_space=pl.ANY)
    res = pl.pallas_call(
        body, name=name, out_shape=out_shapes + [jax.ShapeDtypeStruct((8, 128), F32)], in_specs=[any_spec] * n,
        out_specs=[any_spec] * n + [pl.BlockSpec(memory_space=pltpu.VMEM)],
        scratch_shapes=[pltpu.SemaphoreType.DMA((7 * n,)), pltpu.SemaphoreType.DMA((7 * n,)),
                        pltpu.SemaphoreType.DMA((n,))],
        compiler_params=pltpu.CompilerParams(has_side_effects=True),
    )(*arrs)
    return res[:n], res[n]


def _peers():
    x, y, c = lax.axis_index("x"), lax.axis_index("y"), lax.axis_index("c")
    out = []
    for k in range(1, NDEV):
        px, py, pc = x ^ ((k >> 2) & 1), y ^ ((k >> 1) & 1), c ^ (k & 1)
        out.append(((px, py, pc), 4 * px + 2 * py + pc))
    return 4 * x + 2 * y + c, out


COPIES = {"all": 7, "chips": 3, "pass": 4}


def _exchange_copies(ins, lands, bcast, send_sems, recv_sems, mode="all"):
    x, y, c = lax.axis_index("x"), lax.axis_index("y"), lax.axis_index("c")
    me = 4 * x + 2 * y + c
    n, copies = len(ins), []

    def add(q, j, src, dst, dev):
        copies.append(pltpu.make_async_remote_copy(
            src_ref=src, dst_ref=dst, send_sem=send_sems.at[q * n + j], recv_sem=recv_sems.at[q * n + j],
            device_id=dev, device_id_type=pl.DeviceIdType.MESH))

    if mode == "pass":
        for q in range(4):
            slot = 4 * (x ^ (q >> 1)) + 2 * (y ^ (q & 1)) + c
            for j in range(n):
                add(q, j, ins[j] if q == 0 else lands[j].at[slot], lands[j].at[slot], (x, y, 1 - c))
        return me, copies
    for q, k in enumerate(range(1, NDEV) if mode == "all" else (2, 4, 6)):
        px, py, pc = x ^ ((k >> 2) & 1), y ^ ((k >> 1) & 1), c ^ (k & 1)
        for j in range(n):
            add(q, j, ins[j] if bcast[j] else ins[j].at[4 * px + 2 * py + pc], lands[j].at[me], (px, py, pc))
    return me, copies


_HBM = pl.BlockSpec(memory_space=pltpu.HBM)
_SEM = pl.BlockSpec(memory_space=pltpu.SEMAPHORE)
_EFFECT = pltpu.SideEffectType.DATAFLOW_SIDE_EFFECTING


def exchange_start(arrs, bcast, name, mode="all", lands=None):
    n, ncp = len(arrs), COPIES[mode] * len(arrs)
    land_shapes = [((NDEV,) + a.shape) if b else a.shape for a, b in zip(arrs, bcast)]
    if lands is None:
        lands = [lax.empty(s_, a.dtype) for s_, a in zip(land_shapes, arrs)]

    def body(*refs):
        in_refs, land_refs = refs[:n], refs[n:2 * n]
        send_sems, recv_sems = refs[2 * n], refs[2 * n + 1]
        token = refs[-1]
        _, copies = _exchange_copies(in_refs, land_refs, bcast, send_sems, recv_sems, mode)
        for cp in copies:
            cp.start()
        token[...] = jnp.zeros_like(token)

    hbm = lambda shp, a: pltpu.HBM(shp, a.dtype)
    res = pl.pallas_call(
        body, name=name,
        out_shape=[pltpu.SemaphoreType.DMA((ncp,)), pltpu.SemaphoreType.DMA((ncp,))]
                  + [hbm(a.shape, a) for a in arrs] + [hbm(s_, a) for s_, a in zip(land_shapes, arrs)]
                  + [jax.ShapeDtypeStruct((8, 128), F32)],
        in_specs=[_HBM] * (2 * n), out_specs=[_SEM, _SEM] + [_HBM] * (2 * n) + [pl.BlockSpec(memory_space=pltpu.VMEM)],
        input_output_aliases={i: 2 + i for i in range(2 * n)},
        compiler_params=pltpu.CompilerParams(has_side_effects=_EFFECT),
    )(*[pltpu.with_memory_space_constraint(a, pltpu.HBM) for a in arrs],
      *[pltpu.with_memory_space_constraint(a, pltpu.HBM) for a in lands])
    return (res[0], res[1], res[2:2 + n], res[2 + n:2 + 2 * n], tuple(bcast), mode), res[-1]


def exchange_wait(state, after, name):
    send_sems, recv_sems, ins, lands, bcast, mode = state
    n = len(ins)
    after = list(after) if isinstance(after, (list, tuple)) else [after]

    def body(*refs):
        in_refs, land_refs = refs[:n], refs[n:2 * n]
        s_sems, r_sems = refs[2 * n], refs[2 * n + 1]
        token = refs[-1]
        _, copies = _exchange_copies(in_refs, land_refs, bcast, s_sems, r_sems, mode)
        for cp in copies:
            cp.wait_send()
            cp.wait_recv()
        token[...] = jnp.zeros_like(token)

    res = pl.pallas_call(
        body, name=name,
        out_shape=[pltpu.HBM(a.shape, a.dtype) for a in ins] + [pltpu.HBM(a.shape, a.dtype) for a in lands]
                  + [jax.ShapeDtypeStruct((8, 128), F32)],
        in_specs=[_HBM] * (2 * n) + [_SEM, _SEM] + [pl.BlockSpec(memory_space=pl.ANY)] * len(after),
        out_specs=[_HBM] * (2 * n) + [pl.BlockSpec(memory_space=pltpu.VMEM)],
        input_output_aliases={i: i for i in range(2 * n)},
        compiler_params=pltpu.CompilerParams(has_side_effects=_EFFECT),
    )(*ins, *lands, send_sems, recv_sems, *after)
    if mode == "chips":
        return list(res[n:2 * n]), res[-1], list(res[:n])
    me = 4 * lax.axis_index("x") + 2 * lax.axis_index("y") + lax.axis_index("c")
    got = []
    for j in range(n):
        own = res[j][None] if bcast[j] else lax.dynamic_index_in_dim(res[j], me, 0, keepdims=True)
        got.append(lax.dynamic_update_slice_in_dim(res[n + j], own, me, axis=0))
    return got, res[-1], list(res[:n])


def gather_start(shards, name):
    return exchange_start(shards, [True] * len(shards), name + "_chips_start", mode="chips")


def gather_finish(state, after, name):
    lands, _, sent = exchange_wait(state, after, name + "_chips_wait")
    state, tok = exchange_start(sent, [True] * len(sent), name + "_pass_start", mode="pass", lands=lands)
    got, tok, _ = exchange_wait(state, tok, name + "_pass_wait")
    return got, tok


def norm_proj_fwd(x, vec, w, name):
    s, n = x.shape[0], w.shape[0]
    tr, tn = _pick(s, 512), _pick(n, 2560)
    ni, jdt, odt = s // tr, O_DT // tn, O_DT % tn

    def body(x_ref, v_ref, w_ref, o_ref, h_ref, dt_ref, h_scr):
        j, i = pl.program_id(0), pl.program_id(1)
        rows = pl.ds(pl.multiple_of(i * tr, tr), tr)

        @pl.when(j == 0)
        def _():
            h = _rms(x_ref[...], v_ref[0:1, :], D) * (1.0 + v_ref[2:3, :]) + v_ref[1:2, :]
            h_scr[rows, :] = h.astype(BF16)
            h_ref[...] = h.astype(BF16)
        res = _raw(h_scr[rows, :], w_ref[...], _NT)
        o_ref[...] = res

        @pl.when(j == jdt)
        def _():
            dt_ref[...] = res[:, odt:odt + 128]

    first = lambda j, i: (jnp.where(j == 0, i, ni - 1), 0)
    dtix = lambda j, i: (jnp.where(j < jdt, 0, jnp.where(j == jdt, i, ni - 1)), 0)
    return pl.pallas_call(
        body, name=name, grid=(n // tn, ni),
        in_specs=[pl.BlockSpec((tr, D), first), pl.BlockSpec((8, D), lambda j, i: (0, 0)),
                  pl.BlockSpec((tn, D), lambda j, i: (j, 0))],
        out_specs=[pl.BlockSpec((tr, tn), lambda j, i: (i, j)), pl.BlockSpec((tr, D), first),
                   pl.BlockSpec((tr, 128), dtix)],
        out_shape=[jax.ShapeDtypeStruct((s, n), F32), jax.ShapeDtypeStruct((s, D), BF16),
                   jax.ShapeDtypeStruct((s, 128), F32)],
        scratch_shapes=[pltpu.VMEM((s, D), BF16)],
        compiler_params=_params("arbitrary", "arbitrary"),
    )(x, vec, w)


def _col_tiles(arr, cap):
    if arr.ndim == 2:
        n = arr.shape[1]
        t = _pick(n, cap)
        return n, t, lambda rows, ix: pl.BlockSpec((rows, t), lambda *g: ix(*g))
    width = arr.shape[2]
    t = _pick(width, cap)
    per = width // t

    def spec(rows, ix):
        def index(*g):
            r, j = ix(*g)
            return (j // per, r, j % per)
        return pl.BlockSpec((None, rows, t), index)
    return arr.shape[0] * width, t, spec


def norm_proj_bwd(x, vec, dp, w, dx_in, aux, name):
    s = x.shape[0]
    tr = _pick(s, 512)
    n, tk, dp_spec = _col_tiles(dp, 2560)
    nk, has_aux = n // tk, aux is not None

    def body(*refs):
        if has_aux:
            x_ref, v_ref, dp_ref, w_ref, dxin_ref, aux_ref, dx_ref, dv_ref, acc = refs
        else:
            x_ref, v_ref, dp_ref, w_ref, dxin_ref, dx_ref, dv_ref, acc = refs
        k, i = pl.program_id(0), pl.program_id(1)
        rows = pl.ds(pl.multiple_of(i * tr, tr), tr)
        part = _raw(dp_ref[...], w_ref[...], _NN)

        @pl.when(k == 0)
        def _():
            acc[rows, :] = part

        @pl.when(k > 0)
        def _():
            acc[rows, :] += part

        @pl.when(k == nk - 1)
        def _():
            f = lambda xx, nw, sh, sc: _rms(xx, nw, D) * (1.0 + sc) + sh
            _, vjp = jax.vjp(f, x_ref[...], v_ref[0:1, :], v_ref[1:2, :], v_ref[2:3, :])
            dx, dnw, dsh, dsc = vjp(acc[rows, :])
            dx_ref[...] = dxin_ref[...] + dx

            @pl.when(i == 0)
            def _():
                dv_ref[...] = jnp.zeros_like(dv_ref)

            dv_ref[0:1, :] += dnw
            dv_ref[1:2, :] += dsh
            dv_ref[2:3, :] += dsc
            if has_aux:
                dv_ref[3:4, :] += jnp.sum(dxin_ref[...] * aux_ref[...], axis=0, keepdims=True)

    row = pl.BlockSpec((tr, D), lambda k, i: (jnp.where(k == nk - 1, i, 0), 0))
    in_specs = [row, pl.BlockSpec((8, D), lambda k, i: (0, 0)), dp_spec(tr, lambda k, i: (i, k)),
                pl.BlockSpec((tk, D), lambda k, i: (k, 0)), row] + ([row] if has_aux else [])
    args = [x, vec, dp, w, dx_in] + ([aux] if has_aux else [])
    return pl.pallas_call(
        body, name=name, grid=(nk, s // tr), in_specs=in_specs,
        out_specs=[row, pl.BlockSpec((8, D), lambda k, i: (0, 0))],
        out_shape=[jax.ShapeDtypeStruct((s, D), F32), jax.ShapeDtypeStruct((8, D), F32)],
        scratch_shapes=[pltpu.VMEM((s, D), F32)],
        compiler_params=_params("arbitrary", "arbitrary"),
    )(*args)


def tn_matmul(a, b, name, scale=None, out_dtype=None):
    out_dtype = BF16 if out_dtype is None else out_dtype
    s = b.shape[-2]
    ts = _pick(s, 512, 16)
    m, tm, a_spec = _col_tiles(a, 2560 if b.shape[-1] <= D else 1408)
    n, tn, b_spec = _col_tiles(b, 2560)
    ns, has_scale = s // ts, scale is not None

    def body(*refs):
        if has_scale:
            a_ref, b_ref, sc_ref, o_ref, acc = refs
        else:
            a_ref, b_ref, o_ref, acc = refs
        k = pl.program_id(2)

        @pl.when(k == 0)
        def _():
            acc[...] = jnp.zeros_like(acc)

        acc[...] += _raw(a_ref[...], b_ref[...], _TN)

        @pl.when(k == ns - 1)
        def _():
            o_ref[...] = (acc[...] * sc_ref[...] if has_scale else acc[...]).astype(out_dtype)

    in_specs = [a_spec(ts, lambda i, j, k: (k, i)), b_spec(ts, lambda i, j, k: (k, j))]
    if has_scale:
        in_specs.append(pl.BlockSpec((1, tn), lambda i, j, k: (0, j)))
    return pl.pallas_call(
        body, name=name, grid=(m // tm, n // tn, ns), in_specs=in_specs,
        out_specs=pl.BlockSpec((tm, tn), lambda i, j, k: (i, j)),
        out_shape=jax.ShapeDtypeStruct((m, n), out_dtype),
        scratch_shapes=[pltpu.VMEM((tm, tn), F32)],
        compiler_params=_params("arbitrary", "arbitrary", "arbitrary"),
    )(*([a, b] + ([scale] if has_scale else [])))


def ada_mod(c16, w):
    ncol = w.shape[2]

    def body(c_ref, w_ref, o_ref, a_ref):
        cc = c_ref[...]
        act = cc * _sig(cc)
        a_ref[...] = act
        o_ref[...] = _raw(act, w_ref[...], _NN)

    return pl.pallas_call(
        body, name="ada_mod", grid=(LAYERS,),
        in_specs=[pl.BlockSpec((16, D), lambda l: (0, 0)), pl.BlockSpec((None, D, ncol), lambda l: (l, 0, 0))],
        out_specs=[pl.BlockSpec((None, 16, ncol), lambda l: (l, 0, 0)), pl.BlockSpec((16, D), lambda l: (0, 0))],
        out_shape=[jax.ShapeDtypeStruct((LAYERS, 16, ncol), F32), jax.ShapeDtypeStruct((16, D), F32)],
        compiler_params=_params("arbitrary"),
    )(c16, w)


def _mla_shared(q_lat, c_kv, kr, krs, qa_w, kva_w, kr_w, krs_w, cos2, sin2):
    qn = _rms(q_lat, qa_w, 384.0)
    kvn = _rms(c_kv, kva_w, 256.0)
    rk = lax.rsqrt(jnp.sum(kr * kr, axis=-1, keepdims=True) / 32.0 + EPS)
    krope = rk * (kr * kr_w * cos2 + krs * krs_w * sin2)
    return qn, kvn, krope


def _mla_head(qn, kvn, wqn, wqr, wqrs, wkn, wv, qn_w, qr_w, qrs_w, kn_w, cos2, sin2):
    qnope = _rms(mm_nn(qn, wqn), qn_w, 64.0)
    qr, qrs = mm_nn(qn, wqr), mm_nn(qn, wqrs)
    rq = lax.rsqrt(jnp.sum(qr * qr, axis=-1, keepdims=True) / 32.0 + EPS)
    qrope = rq * (qr * qr_w * cos2 + qrs * qrs_w * sin2)
    knope = _rms(mm_nn(kvn, wkn), kn_w, 64.0)
    return qnope, qrope, knope, mm_nn(kvn, wv)


def _mla_vec_pieces(v_ref):
    return ((v_ref[0:1, 0:384], v_ref[1:2, 0:256], v_ref[3:4, 128:256], v_ref[3:4, 256:384]),
            (v_ref[2:3, 0:128], v_ref[2:3, 128:256], v_ref[2:3, 256:384], v_ref[3:4, 0:128]))


def _mla_in_specs(tr):
    return [pl.BlockSpec((tr, 384), lambda i: (i, O_QL // 384)), pl.BlockSpec((tr, 256), lambda i: (i, O_CKV // 256)),
            pl.BlockSpec((tr, 128), lambda i: (i, O_KR // 128)), pl.BlockSpec((tr, 128), lambda i: (i, O_KRS // 128)),
            pl.BlockSpec((HEADS, 384, 384), lambda i: (0, 0, 0), **CONST),
            pl.BlockSpec((HEADS, 256, 256), lambda i: (0, 0, 0), **CONST),
            pl.BlockSpec((8, 512), lambda i: (0, 0)),
            pl.BlockSpec((tr, 128), lambda i: (i, 0)), pl.BlockSpec((tr, 128), lambda i: (i, 0))]


def mla_pre_fwd(proj, wq, wkv, vec, cos2, sin2, name):
    s = proj.shape[0]
    tr = _pick(s, 256)

    def body(ql_ref, ckv_ref, kr_ref, krs_ref, wq_ref, wkv_ref, v_ref, cos_ref, sin_ref, q_out, k_out, v_out):
        vshared, vhead = _mla_vec_pieces(v_ref)
        cos2_, sin2_ = cos_ref[...], sin_ref[...]
        qlat_n, kv_n, krope = _mla_shared(ql_ref[...], ckv_ref[...], kr_ref[...], krs_ref[...], *vshared, cos2_, sin2_)
        qlat_n, kv_n, krope = qlat_n.astype(BF16), kv_n.astype(BF16), krope.astype(BF16)
        for h in range(HEADS):
            ws = (wq_ref[h, :, 0:128], wq_ref[h, :, 128:256], wq_ref[h, :, 256:384],
                  wkv_ref[h, :, 0:128], wkv_ref[h, :, 128:256])
            qn, qr, kn, v = _mla_head(qlat_n, kv_n, *ws, *vhead, cos2_, sin2_)
            q_out[h, :, 0:128] = qn.astype(BF16)
            q_out[h, :, 128:256] = qr.astype(BF16)
            k_out[h, :, 0:128] = kn.astype(BF16)
            k_out[h, :, 128:256] = krope
            v_out[h] = v.astype(BF16)

    return pl.pallas_call(
        body, name=name, grid=(s // tr,), in_specs=_mla_in_specs(tr),
        out_specs=[pl.BlockSpec((HEADS, tr, 256), lambda i: (0, i, 0)), pl.BlockSpec((HEADS, tr, 256), lambda i: (0, i, 0)),
                   pl.BlockSpec((HEADS, tr, 128), lambda i: (0, i, 0))],
        out_shape=[jax.ShapeDtypeStruct((HEADS, s, 256), BF16), jax.ShapeDtypeStruct((HEADS, s, 256), BF16),
                   jax.ShapeDtypeStruct((HEADS, s, 128), BF16)],
        compiler_params=_params("arbitrary"),
    )(proj, proj, proj, proj, wq, wkv, vec, cos2, sin2)


def mla_pre_bwd(proj, wq, wkv, vec, cos2, sin2, dq, dk, dv, name):
    s = proj.shape[0]
    tr = _pick(s, 256)

    def body(ql_ref, ckv_ref, kr_ref, krs_ref, wq_ref, wkv_ref, v_ref, cos_ref, sin_ref, dq_ref, dk_ref, dv_ref,
             dql_out, dckv_out, dkr_out, dkrs_out, dwq_out, dwkv_out, dvec_out):
        @pl.when(pl.program_id(0) == 0)
        def _():
            dwq_out[...] = jnp.zeros_like(dwq_out)
            dwkv_out[...] = jnp.zeros_like(dwkv_out)
            dvec_out[...] = jnp.zeros_like(dvec_out)

        vshared, vhead = _mla_vec_pieces(v_ref)
        cos2_, sin2_ = cos_ref[...], sin_ref[...]
        fs = lambda *a: _mla_shared(*a, cos2_, sin2_)
        (qlat_n, kv_n, _), vjp_shared = jax.vjp(fs, ql_ref[...], ckv_ref[...], kr_ref[...], krs_ref[...], *vshared)

        def head(h, carry):
            wq_h, wkv_h = wq_ref[h].astype(F32), wkv_ref[h].astype(F32)
            ws = (wq_h[:, 0:128], wq_h[:, 128:256], wq_h[:, 256:384], wkv_h[:, 0:128], wkv_h[:, 128:256])
            f = lambda *a: _mla_head(*a, cos2_, sin2_)
            _, vjp = jax.vjp(f, qlat_n, kv_n, *ws, *vhead)
            dq_h, dk_h = dq_ref[h], dk_ref[h]
            g = vjp((dq_h[:, 0:128], dq_h[:, 128:256], dk_h[:, 0:128], dv_ref[h]))
            dwq_out[h, :, 0:128] += g[2]
            dwq_out[h, :, 128:256] += g[3]
            dwq_out[h, :, 256:384] += g[4]
            dwkv_out[h, :, 0:128] += g[5]
            dwkv_out[h, :, 128:256] += g[6]
            dvec_out[2:3, 0:128] += g[7]
            dvec_out[2:3, 128:256] += g[8]
            dvec_out[2:3, 256:384] += g[9]
            dvec_out[3:4, 0:128] += g[10]
            return carry[0] + g[0], carry[1] + g[1], carry[2] + dk_h[:, 128:256]

        zero = lambda w: jnp.zeros((tr, w), F32)
        dqn, dkvn, dkrope = lax.fori_loop(0, HEADS, head, (zero(384), zero(256), zero(128)))
        g = vjp_shared((dqn, dkvn, dkrope))
        dql_out[...] = g[0].astype(BF16)
        dckv_out[...] = g[1].astype(BF16)
        dkr_out[...] = g[2].astype(BF16)
        dkrs_out[...] = g[3].astype(BF16)
        dvec_out[0:1, 0:384] += g[4]
        dvec_out[1:2, 0:256] += g[5]
        dvec_out[3:4, 128:256] += g[6]
        dvec_out[3:4, 256:384] += g[7]

    hb = lambda w: pl.BlockSpec((HEADS, tr, w), lambda i: (0, i, 0))
    return pl.pallas_call(
        body, name=name, grid=(s // tr,), in_specs=_mla_in_specs(tr) + [hb(256), hb(256), hb(128)],
        out_specs=[pl.BlockSpec((tr, 384), lambda i: (i, 0)), pl.BlockSpec((tr, 256), lambda i: (i, 0)),
                   pl.BlockSpec((tr, 128), lambda i: (i, 0)), pl.BlockSpec((tr, 128), lambda i: (i, 0)),
                   pl.BlockSpec((HEADS, 384, 384), lambda i: (0, 0, 0)), pl.BlockSpec((HEADS, 256, 256), lambda i: (0, 0, 0)),
                   pl.BlockSpec((8, 512), lambda i: (0, 0))],
        out_shape=[jax.ShapeDtypeStruct((s, 384), BF16), jax.ShapeDtypeStruct((s, 256), BF16),
                   jax.ShapeDtypeStruct((s, 128), BF16), jax.ShapeDtypeStruct((s, 128), BF16),
                   jax.ShapeDtypeStruct((HEADS, 384, 384), F32), jax.ShapeDtypeStruct((HEADS, 256, 256), F32),
                   jax.ShapeDtypeStruct((8, 512), F32)],
        compiler_params=_params("arbitrary"),
    )(proj, proj, proj, proj, wq, wkv, vec, cos2, sin2, dq, dk, dv)


def _att_probs(q, kk, i, tq):
    sc = _raw(q, kk, _NT) * ATT_SCALE
    rows = lax.broadcasted_iota(jnp.int32, sc.shape, 0) + i * tq
    cols = lax.broadcasted_iota(jnp.int32, sc.shape, 1)
    sc = jnp.where(cols <= rows, sc, -jnp.inf)
    e = jnp.exp(sc - jnp.max(sc, axis=-1, keepdims=True))
    return e / jnp.sum(e, axis=-1, keepdims=True)


def mla_attn_fwd(q, k, v, name):
    s = q.shape[1]
    tq = _pick(s, 256)

    def body(q_ref, k_ref, v_ref, o_ref):
        for i in range(s // tq):
            n = (i + 1) * tq
            p = _att_probs(q_ref[i * tq:n, :], k_ref[0:n, :], i, tq)
            o_ref[i * tq:n, :] = _raw(p, v_ref[0:n, :], _NN)

    hs = lambda w: pl.BlockSpec((None, s, w), lambda h: (h, 0, 0))
    return pl.pallas_call(
        body, name=name, grid=(HEADS,), in_specs=[hs(256), hs(256), hs(128)],
        out_specs=pl.BlockSpec((s, 128), lambda h: (0, h)),
        out_shape=jax.ShapeDtypeStruct((s, HEADS * 128), F32),
        compiler_params=_params("arbitrary"),
    )(q, k, v)


def mla_attn_bwd(q, k, v, do, name):
    s = q.shape[1]
    tq = _pick(s, 256)

    def body(q_ref, k_ref, v_ref, do_ref, dq_ref, dk_ref, dv_ref):
        dk_ref[...] = jnp.zeros_like(dk_ref)
        dv_ref[...] = jnp.zeros_like(dv_ref)
        for i in range(s // tq):
            n = (i + 1) * tq
            qq, kk, vv = q_ref[i * tq:n, :], k_ref[0:n, :], v_ref[0:n, :]
            p = _att_probs(qq, kk, i, tq)
            o = _raw(p, vv, _NN)
            dout = do_ref[i * tq:n, :]
            delta = jnp.sum(dout * o, axis=-1, keepdims=True)
            dp = _raw(dout, vv, _NT)
            ds = p * (dp - delta) * ATT_SCALE
            dq_ref[i * tq:n, :] = _raw(ds, kk, _NN)
            dk_ref[0:n, :] += _raw(ds, qq, _TN)
            dv_ref[0:n, :] += _raw(p, dout, _TN)

    hs = lambda w: pl.BlockSpec((None, s, w), lambda h: (h, 0, 0))
    return pl.pallas_call(
        body, name=name, grid=(HEADS,),
        in_specs=[hs(256), hs(256), hs(128), pl.BlockSpec((s, 128), lambda h: (0, h))],
        out_specs=[hs(256), hs(256), hs(128)],
        out_shape=[jax.ShapeDtypeStruct((HEADS, s, 256), F32), jax.ShapeDtypeStruct((HEADS, s, 256), F32),
                   jax.ShapeDtypeStruct((HEADS, s, 128), F32)],
        compiler_params=_params("arbitrary"),
    )(q, k, v, do)


def _pool_windows(u, pad, s, g):
    pad[0:16, :] = jnp.zeros((16, 128), F32)
    cur, sel = u, None
    for j, k in enumerate((1, 2, 4, 8)):
        pad[16:16 + s, :] = cur
        cur = cur + pad[16 - k:16 - k + s, :]
        sel = cur if sel is None else jnp.where(g == j, cur, sel)
    return sel


def _pool_count(s, g):
    t = lax.broadcasted_iota(jnp.int32, (s, 1), 0)
    return jnp.minimum(t + 1, 2 << g).astype(F32)


def pool_fwd(proj, pw, ps, name):
    s = proj.shape[0]

    def body(u_ref, w_ref, s_ref, o_ref, pad):
        g = pl.program_id(0)
        u = u_ref[...]
        pooled = _pool_windows(u, pad, s, g) / _pool_count(s, g) - u
        o_ref[...] = _raw(pooled, w_ref[...], _NN) * s_ref[...]

    return pl.pallas_call(
        body, name=name, grid=(4,),
        in_specs=[pl.BlockSpec((s, 128), lambda g: (0, O_PU // 128 + g)), pl.BlockSpec((None, 128, 128), lambda g: (g, 0, 0)),
                  pl.BlockSpec((1, 128), lambda g: (0, g))],
        out_specs=pl.BlockSpec((s, 128), lambda g: (0, g)),
        out_shape=jax.ShapeDtypeStruct((s, 512), F32),
        scratch_shapes=[pltpu.VMEM((s + 16, 128), F32)],
        compiler_params=_params("arbitrary"),
    )(proj, pw, ps)


def pool_bwd(proj, pw, ps, do, name):
    s = proj.shape[0]

    def body(u_ref, w_ref, s_ref, do_ref, du_ref, dw_ref, ds_ref, pad):
        g = pl.program_id(0)
        u, w, dout = u_ref[...], w_ref[...], do_ref[...]
        cnt = _pool_count(s, g)
        pooled = _pool_windows(u, pad, s, g) / cnt - u
        mixed = _raw(pooled, w, _NN)
        ds_ref[...] = jnp.sum(dout * mixed, axis=0, keepdims=True)
        dmixed = dout * s_ref[...]
        dw_ref[...] = _raw(pooled, dmixed, _TN)
        dpooled = _raw(dmixed, w, _NT)
        dsel = dpooled / cnt
        pad[s:s + 16, :] = jnp.zeros((16, 128), F32)
        cur = jnp.where(g == 3, dsel, 0.0)
        for j, k in ((2, 8), (1, 4), (0, 2)):
            pad[0:s, :] = cur
            cur = cur + pad[k:k + s, :] + jnp.where(g == j, dsel, 0.0)
        pad[0:s, :] = cur
        cur = cur + pad[1:1 + s, :]
        du_ref[...] = (cur - dpooled).astype(BF16)

    return pl.pallas_call(
        body, name=name, grid=(4,),
        in_specs=[pl.BlockSpec((s, 128), lambda g: (0, O_PU // 128 + g)), pl.BlockSpec((None, 128, 128), lambda g: (g, 0, 0)),
                  pl.BlockSpec((1, 128), lambda g: (0, g)), pl.BlockSpec((s, 128), lambda g: (0, g))],
        out_specs=[pl.BlockSpec((s, 128), lambda g: (0, g)), pl.BlockSpec((None, 128, 128), lambda g: (g, 0, 0)),
                   pl.BlockSpec((1, 128), lambda g: (0, g))],
        out_shape=[jax.ShapeDtypeStruct((s, 512), BF16), jax.ShapeDtypeStruct((4, 128, 128), F32),
                   jax.ShapeDtypeStruct((1, 512), F32)],
        scratch_shapes=[pltpu.VMEM((s + 16, 128), F32)],
        compiler_params=_params("arbitrary"),
    )(proj, pw, ps, do)


def _xbc_col(i):
    return jnp.where(i < 2, O_XS // 512 + i, O_BC // 512)


def conv_fwd(proj, cw, cb, name):
    s = proj.shape[0]

    def body(x_ref, w_ref, b_ref, o_ref, t_ref, pad):
        pad[0:8, :] = jnp.zeros((8, 512), F32)
        pad[8:8 + s, :] = x_ref[...]
        y = b_ref[...] + sum(w_ref[k:k + 1, :] * pad[5 + k:5 + k + s, :] for k in range(4))
        act = y * _sig(y)
        o_ref[...] = act

        @pl.when(pl.program_id(0) < 2)
        def _():
            t_ref[...] = act.T

    return pl.pallas_call(
        body, name=name, grid=(3,),
        in_specs=[pl.BlockSpec((s, 512), lambda i: (0, _xbc_col(i))), pl.BlockSpec((4, 512), lambda i: (0, i)),
                  pl.BlockSpec((1, 512), lambda i: (0, i))],
        out_specs=[pl.BlockSpec((s, 512), lambda i: (0, i)), pl.BlockSpec((512, s), lambda i: (jnp.minimum(i, 1), 0))],
        out_shape=[jax.ShapeDtypeStruct((s, 1536), F32), jax.ShapeDtypeStruct((D, s), F32)],
        scratch_shapes=[pltpu.VMEM((s + 8, 512), F32)],
        compiler_params=_params("arbitrary"),
    )(proj, cw, cb)


def conv_bwd(proj, cw, cb, dxt, dbm, dcm, name):
    s = proj.shape[0]

    def body(x_ref, w_ref, b_ref, dxt_ref, dbm_ref, dcm_ref, dx_ref, dw_ref, db_ref, pad, pad2):
        pad[0:8, :] = jnp.zeros((8, 512), F32)
        pad[8:8 + s, :] = x_ref[...]
        y = b_ref[...] + sum(w_ref[k:k + 1, :] * pad[5 + k:5 + k + s, :] for k in range(4))
        sg = _sig(y)

        @pl.when(pl.program_id(0) < 2)
        def _():
            pad2[0:s, :] = dxt_ref[...].T

        @pl.when(pl.program_id(0) == 2)
        def _():
            pad2[0:s, 0:256] = dbm_ref[...]
            pad2[0:s, 256:512] = dcm_ref[...]

        dy = pad2[0:s, :] * (sg * (1.0 + y * (1.0 - sg)))
        db_ref[...] = jnp.sum(dy, axis=0, keepdims=True)
        for k in range(4):
            dw_ref[k:k + 1, :] = jnp.sum(dy * pad[5 + k:5 + k + s, :], axis=0, keepdims=True)
        pad2[s:s + 8, :] = jnp.zeros((8, 512), F32)
        pad2[0:s, :] = dy
        dx_ref[...] = sum(w_ref[k:k + 1, :] * pad2[3 - k:3 - k + s, :] for k in range(4)).astype(BF16)

    return pl.pallas_call(
        body, name=name, grid=(3,),
        in_specs=[pl.BlockSpec((s, 512), lambda i: (0, _xbc_col(i))), pl.BlockSpec((4, 512), lambda i: (0, i)),
                  pl.BlockSpec((1, 512), lambda i: (0, i)), pl.BlockSpec((512, s), lambda i: (jnp.minimum(i, 1), 0)),
                  pl.BlockSpec((s, 256), lambda i: (0, 0)), pl.BlockSpec((s, 256), lambda i: (0, 0))],
        out_specs=[pl.BlockSpec((s, 512), lambda i: (0, i)), pl.BlockSpec((4, 512), lambda i: (0, i)),
                   pl.BlockSpec((1, 512), lambda i: (0, i))],
        out_shape=[jax.ShapeDtypeStruct((s, 1536), BF16), jax.ShapeDtypeStruct((4, 1536), F32),
                   jax.ShapeDtypeStruct((1, 1536), F32)],
        scratch_shapes=[pltpu.VMEM((s + 8, 512), F32), pltpu.VMEM((s + 8, 512), F32)],
        compiler_params=_params("arbitrary"),
    )(proj, cw, cb, dxt, dbm, dcm)


def _ssd_chunk(xt, dtr, bm, cm, hprev, alog, dbias, dskip):
    ln = 128
    a = -jnp.exp(alog)
    dt_r = softplus(dtr + dbias)
    da_r = dt_r * a
    li = lax.broadcasted_iota(jnp.int32, (1, ln, ln), 1)
    si = lax.broadcasted_iota(jnp.int32, (1, ln, ln), 2)
    causal = si <= li
    acs_c = jnp.sum(jnp.where(causal, da_r, 0.0), axis=2, keepdims=True)
    acs_r = jnp.sum(jnp.where(li == si, acs_c, 0.0), axis=1, keepdims=True)
    acs_last = jnp.sum(da_r, axis=2, keepdims=True)
    decay = jnp.exp(jnp.where(causal, acs_c - acs_r, -jnp.inf))
    m = mm_nt(cm, bm)[None] * decay
    xdt = xt * dt_r
    y_diag = bmm_nt(xdt, m)
    bb = jnp.broadcast_to(bm[None], (8, ln, ln))
    cc = jnp.broadcast_to(cm[None], (8, ln, ln))
    states = bmm_nn(xdt * jnp.exp(acs_last - acs_r), bb)
    y_off = bmm_nt(hprev, cc) * jnp.exp(acs_r)
    hnew = hprev * jnp.exp(acs_last) + states
    return y_diag + y_off + xt * dskip, hnew


def _ssd_specs(nc, rev):
    cix = (lambda c: nc - 1 - c) if rev else (lambda c: c)
    hv = pl.BlockSpec((8, 1, 1), lambda g, c: (g, 0, 0))
    return [pl.BlockSpec((8, 64, 128), lambda g, c: (g, 0, cix(c))), pl.BlockSpec((8, 1, 128), lambda g, c: (g, 0, cix(c))),
            pl.BlockSpec((128, 128), lambda g, c: (cix(c), 8 + g)),
            pl.BlockSpec((128, 128), lambda g, c: (cix(c), 10 + g))], hv, cix


def ssd_fwd(xt, dtr, xbc, alog, dbias, dskip, name):
    s = xt.shape[2]
    nc = s // 128
    specs, hv, _ = _ssd_specs(nc, False)

    def body(x_ref, dr_ref, b_ref, c_ref, al_ref, db_ref, dk_ref, y_ref, hs_ref, h_scr):
        @pl.when(pl.program_id(1) == 0)
        def _():
            h_scr[...] = jnp.zeros_like(h_scr)
        hp = h_scr[...]
        hs_ref[...] = hp
        y, hn = _ssd_chunk(x_ref[...], dr_ref[...], b_ref[...], c_ref[...], hp, al_ref[...], db_ref[...], dk_ref[...])
        y_ref[...] = y
        h_scr[...] = hn

    return pl.pallas_call(
        body, name=name, grid=(2, nc), in_specs=specs + [hv, hv, hv],
        out_specs=[pl.BlockSpec((8, 64, 128), lambda g, c: (g, 0, c)),
                   pl.BlockSpec((None, None, 8, 64, 128), lambda g, c: (g, c, 0, 0, 0))],
        out_shape=[jax.ShapeDtypeStruct((16, 64, s), F32), jax.ShapeDtypeStruct((2, nc, 8, 64, 128), F32)],
        scratch_shapes=[pltpu.VMEM((8, 64, 128), F32)],
        compiler_params=_params("arbitrary", "arbitrary"),
    )(xt, dtr, xbc, xbc, alog, dbias, dskip)


def ssd_bwd(xt, dtr, xbc, alog, dbias, dskip, hs, dyt, name):
    s = xt.shape[2]
    nc = s // 128
    specs, hv, cix = _ssd_specs(nc, True)

    def body(x_ref, dr_ref, b_ref, c_ref, al_ref, db_ref, dk_ref, hs_ref, dy_ref,
             dx_out, ddr_out, dbm_out, dcm_out, dal_out, ddb_out, ddk_out, dh_scr):
        @pl.when(pl.program_id(1) == 0)
        def _():
            dh_scr[...] = jnp.zeros_like(dh_scr)
            dal_out[...] = jnp.zeros_like(dal_out)
            ddb_out[...] = jnp.zeros_like(ddb_out)
            ddk_out[...] = jnp.zeros_like(ddk_out)
        _, vjp = jax.vjp(_ssd_chunk, x_ref[...], dr_ref[...], b_ref[...], c_ref[...], hs_ref[...],
                         al_ref[...], db_ref[...], dk_ref[...])
        g = vjp((dy_ref[...], dh_scr[...]))
        dx_out[...] = g[0]
        ddr_out[...] = g[1]
        dbm_out[...] = g[2]
        dcm_out[...] = g[3]
        dh_scr[...] = g[4]
        dal_out[...] += g[5]
        ddb_out[...] += g[6]
        ddk_out[...] += g[7]

    return pl.pallas_call(
        body, name=name, grid=(2, nc),
        in_specs=specs + [hv, hv, hv, pl.BlockSpec((None, None, 8, 64, 128), lambda g, c: (g, cix(c), 0, 0, 0)),
                          pl.BlockSpec((8, 64, 128), lambda g, c: (g, 0, cix(c)))],
        out_specs=[pl.BlockSpec((8, 64, 128), lambda g, c: (g, 0, cix(c))), pl.BlockSpec((8, 1, 128), lambda g, c: (g, 0, cix(c))),
                   pl.BlockSpec((128, 128), lambda g, c: (cix(c), g)),
                   pl.BlockSpec((128, 128), lambda g, c: (cix(c), g)), hv, hv, hv],
        out_shape=[jax.ShapeDtypeStruct((16, 64, s), F32), jax.ShapeDtypeStruct((16, 1, s), F32),
                   jax.ShapeDtypeStruct((s, 256), F32),
                   jax.ShapeDtypeStruct((s, 256), F32)] + [jax.ShapeDtypeStruct((16, 1, 1), F32)] * 3,
        scratch_shapes=[pltpu.VMEM((8, 64, 128), F32)],
        compiler_params=_params("arbitrary", "arbitrary"),
    )(xt, dtr, xbc, xbc, alog, dbias, dskip, hs, dyt)


def _merge(oa, ob, y, z, gla, glb, glc, x, g1, nw, ea, eb, ec, eo, wba, wbb, wbc, wout):
    gated = y * (z * _sig(z))
    sq = gated * gated
    left = lax.broadcasted_iota(jnp.int32, (1, D), 1) < 512
    ms0 = jnp.sum(jnp.where(left, sq, 0.0), axis=-1, keepdims=True) / 512.0
    ms1 = jnp.sum(jnp.where(left, 0.0, sq), axis=-1, keepdims=True) / 512.0
    oc = gated * jnp.where(left, lax.rsqrt(ms0 + EPS), lax.rsqrt(ms1 + EPS)) * nw
    ya, yb, yc = mm_nc(oa, wba) + ea, mm_nc(ob, wbb) + eb, mm_nc(oc, wbc) + ec
    merged = _sig(gla) * ya + _sig(glb) * yb + _sig(glc) * yc
    x1 = x + g1 * (mm_nc(merged, wout) + eo)
    return x1, (oc, merged)


def _merge_specs(tr):
    row = lambda w: pl.BlockSpec((tr, w), lambda i: (i, 0))
    acts = [row(D), row(512), pl.BlockSpec((D, tr), lambda i: (0, i)), pl.BlockSpec((tr, D), lambda i: (i, O_Z // D)),
            pl.BlockSpec((tr, 3 * D), lambda i: (i, 0)), row(D), pl.BlockSpec((8, D), lambda i: (0, 0))]
    cst = lambda r: pl.BlockSpec((r, D), lambda i: (0, 0), **CONST)
    return acts, [cst(D), cst(512), cst(D), cst(D)], row


def merge_fwd(oa, ob, y, proj, x, mvec, wba, wbb, wbc, wout, name):
    s = x.shape[0]
    tr = _pick(s, 256)
    acts, wts, row = _merge_specs(tr)

    def body(oa_ref, ob_ref, y_ref, z_ref, gl_ref, x_ref, mv_ref, wba_ref, wbb_ref, wbc_ref, wout_ref, o_ref):
        zero = jnp.zeros((1, D), F32)
        x1, _ = _merge(oa_ref[...], ob_ref[...], y_ref[...].T, z_ref[...], gl_ref[:, 0:D], gl_ref[:, D:2 * D],
                       gl_ref[:, 2 * D:3 * D], x_ref[...], mv_ref[0:1, :], mv_ref[1:2, :], zero, zero, zero, zero,
                       wba_ref[...], wbb_ref[...], wbc_ref[...], wout_ref[...])
        o_ref[...] = x1

    return pl.pallas_call(
        body, name=name, grid=(s // tr,), in_specs=acts + wts, out_specs=row(D),
        out_shape=jax.ShapeDtypeStruct((s, D), F32), compiler_params=_params("arbitrary"),
    )(oa, ob, y, proj, proj, x, mvec, wba, wbb, wbc, wout)


def merge_bwd(oa, ob, y, proj, x, mvec, wba, wbb, wbc, wout, dx1, name):
    s = x.shape[0]
    tr = _pick(s, 128)
    acts, wts, row = _merge_specs(tr)

    def body(oa_ref, ob_ref, y_ref, z_ref, gl_ref, x_ref, mv_ref, wba_ref, wbb_ref, wbc_ref, wout_ref, dx1_ref,
             doa_o, dob_o, dy_o, dz_o, dgl_o, dx_o, dmv_o, dya_o, dyb_o, dyc_o, dpre_o, oc_o, mg_o):
        zero = jnp.zeros((tr, D), F32)
        wts_ = (wba_ref[...], wbb_ref[...], wbc_ref[...], wout_ref[...])
        f = lambda *a: _merge(*a, *wts_)
        _, vjp, (oc, merged) = jax.vjp(
            f, oa_ref[...], ob_ref[...], y_ref[...].T, z_ref[...], gl_ref[:, 0:D], gl_ref[:, D:2 * D],
            gl_ref[:, 2 * D:3 * D], x_ref[...], mv_ref[0:1, :], mv_ref[1:2, :], zero, zero, zero, zero, has_aux=True)
        g = vjp(dx1_ref[...])
        doa_o[...] = g[0]
        dob_o[...] = g[1]
        dy_o[...] = g[2].T
        dz_o[...] = g[3].astype(BF16)
        dgl_o[:, 0:D] = g[4].astype(BF16)
        dgl_o[:, D:2 * D] = g[5].astype(BF16)
        dgl_o[:, 2 * D:3 * D] = g[6].astype(BF16)
        dx_o[...] = g[7]

        @pl.when(pl.program_id(0) == 0)
        def _():
            dmv_o[...] = jnp.zeros_like(dmv_o)

        dmv_o[0:1, :] += g[8]
        dmv_o[1:2, :] += g[9]
        dya_o[...] = g[10].astype(BF16)
        dyb_o[...] = g[11].astype(BF16)
        dyc_o[...] = g[12].astype(BF16)
        dpre_o[...] = g[13].astype(BF16)
        oc_o[...] = oc.astype(BF16)
        mg_o[...] = merged.astype(BF16)

    sd = lambda w, dt: jax.ShapeDtypeStruct((s, w), dt)
    return pl.pallas_call(
        body, name=name, grid=(s // tr,), in_specs=acts + wts + [row(D)],
        out_specs=[row(D), row(512), pl.BlockSpec((D, tr), lambda i: (0, i)), row(D), row(3 * D), row(D),
                   pl.BlockSpec((8, D), lambda i: (0, 0))] + [row(D)] * 6,
        out_shape=[sd(D, F32), sd(512, F32), jax.ShapeDtypeStruct((D, s), F32), sd(D, BF16), sd(3 * D, BF16), sd(D, F32),
                   jax.ShapeDtypeStruct((8, D), F32)] + [sd(D, BF16)] * 6,
        compiler_params=_params("arbitrary"),
    )(oa, ob, y, proj, proj, x, mvec, wba, wbb, wbc, wout, dx1)


def _conv3(u_scr, w_ref, first, rows, lanes):
    return sum(w_ref[k:k + 1, :] * u_scr[first + k:first + k + rows, lanes] for k in range(3))


def _ffn_tile_specs(tf, tile):
    def at(rows, off):
        return pl.BlockSpec((rows, tf), lambda *g: (0, off + tile(*g)))

    def wt(off):
        return pl.BlockSpec((tf, D), lambda *g: (off + tile(*g), 0))
    return [wt(0), wt(FFN_NT), at(3, 0), at(3, FFN_NT), at(1, 0), at(1, FFN_NT)]


def ffn_fwd(x1, fvec, wup, cw, cb, wdn, name):
    s = x1.shape[0]
    tr, tf = _pick(s, 512), FFN_TILE
    lg, lv = slice(0, tf), slice(tf, 2 * tf)

    def body(x_ref, v_ref, wg_ref, wv_ref, cwg_ref, cwv_ref, cbg_ref, cbv_ref, wd_ref, x2_ref, h_ref, pre_ref,
             h_scr, u_scr, acc):
        i, t = pl.program_id(0), pl.program_id(1)

        @pl.when(t == 0)
        def _():
            @pl.when(i == 0)
            def _():
                h_scr[0:16, :] = jnp.zeros((16, D), BF16)

            @pl.when(i > 0)
            def _():
                h_scr[0:16, :] = h_scr[tr:tr + 16, :]

            h = (_rms(x_ref[...], v_ref[0:1, :], D) * (1.0 + v_ref[2:3, :]) + v_ref[1:2, :]).astype(BF16)
            h_scr[16:16 + tr, :] = h
            h_ref[...] = h
            acc[...] = jnp.zeros_like(acc)

        u_scr[:, lg] = _raw(h_scr[...], wg_ref[...], _NT)
        u_scr[:, lv] = _raw(h_scr[...], wv_ref[...], _NT)
        cg = _conv3(u_scr, cwg_ref, 14, tr, lg) + cbg_ref[...]
        cval = _conv3(u_scr, cwv_ref, 14, tr, lv) + cbv_ref[...]
        acc[...] += _raw(cg * _sig(cg) * cval, wd_ref[...], _NN)

        @pl.when(t == FFN_NT - 1)
        def _():
            pre_ref[...] = acc[...]
            x2_ref[...] = x_ref[...] + v_ref[3:4, :] * acc[...]

    row = pl.BlockSpec((tr, D), lambda i, t: (i, 0))
    return pl.pallas_call(
        body, name=name, grid=(s // tr, FFN_NT),
        in_specs=[row, pl.BlockSpec((8, D), lambda i, t: (0, 0))] + _ffn_tile_specs(tf, lambda i, t: t)
                 + [pl.BlockSpec((tf, D), lambda i, t: (t, 0))],
        out_specs=[row, row, row],
        out_shape=[jax.ShapeDtypeStruct((s, D), F32), jax.ShapeDtypeStruct((s, D), BF16), jax.ShapeDtypeStruct((s, D), F32)],
        scratch_shapes=[pltpu.VMEM((tr + 16, D), BF16), pltpu.VMEM((tr + 16, 2 * tf), F32), pltpu.VMEM((tr, D), F32)],
        compiler_params=_params("arbitrary", "arbitrary"),
    )(x1, fvec, wup, wup, cw, cw, cb, cb, wdn)


def ffn_bwd(h2, dx2, fvec, wup, cw, cb, wdn, name):
    s = h2.shape[0]
    tr, tf = _pick(s, 512), FFN_TILE
    ni, nb = s // tr, s // 16
    lg, lv = slice(0, tf), slice(tf, 2 * tf)

    def body(hp_ref, hm_ref, hn_ref, dm_ref, dn_ref, v_ref, wg_ref, wv_ref, cwg_ref, cwv_ref, cbg_ref, cbv_ref, wd_ref,
             dup_ref, act_ref, dcw_ref, u_scr, dc_scr):
        i = pl.program_id(1)
        hfull = jnp.concatenate([jnp.where(i > 0, hp_ref[...], jnp.zeros((16, D), BF16)), hm_ref[...],
                                 jnp.where(i < ni - 1, hn_ref[...], jnp.zeros((16, D), BF16))], axis=0)
        u_scr[:, lg] = _raw(hfull, wg_ref[...], _NT)
        u_scr[:, lv] = _raw(hfull, wv_ref[...], _NT)
        cg = _conv3(u_scr, cwg_ref, 14, tr + 16, lg) + cbg_ref[...]
        cval = _conv3(u_scr, cwv_ref, 14, tr + 16, lv) + cbv_ref[...]
        g2 = v_ref[3:4, :]
        dpre = jnp.concatenate([dm_ref[...] * g2, jnp.where(i < ni - 1, dn_ref[...], 0.0) * g2], axis=0)
        dact = _raw(dpre, wd_ref[...], _NT)
        sg = _sig(cg)
        sl = cg * sg
        dc_scr[:, lg] = dact * cval * (sg * (1.0 + cg * (1.0 - sg)))
        dc_scr[:, lv] = dact * sl
        act_ref[...] = (sl * cval)[0:tr, :].astype(BF16)

        @pl.when(i == 0)
        def _():
            dcw_ref[...] = jnp.zeros_like(dcw_ref)

        for half, lanes, cw_ref in ((0, lg, cwg_ref), (1, lv, cwv_ref)):
            dup_ref[half] = sum(cw_ref[k:k + 1, :] * dc_scr[2 - k:2 - k + tr, lanes] for k in range(3)).astype(BF16)
            dcm = dc_scr[0:tr, lanes]
            for k in range(3):
                dcw_ref[half, k:k + 1, :] += jnp.sum(dcm * u_scr[14 + k:14 + k + tr, lanes], axis=0, keepdims=True)
            dcw_ref[half, 3:4, :] += jnp.sum(dcm, axis=0, keepdims=True)

    r16 = tr // 16
    prev = lambda t, i: (jnp.maximum(i * r16 - 1, 0), 0)
    nxt = lambda t, i: (jnp.minimum((i + 1) * r16, nb - 1), 0)
    main = lambda t, i: (i, 0)
    return pl.pallas_call(
        body, name=name, grid=(FFN_NT, ni),
        in_specs=[pl.BlockSpec((16, D), prev), pl.BlockSpec((tr, D), main), pl.BlockSpec((16, D), nxt),
                  pl.BlockSpec((tr, D), main), pl.BlockSpec((16, D), nxt), pl.BlockSpec((8, D), lambda t, i: (0, 0))]
                 + _ffn_tile_specs(tf, lambda t, i: t) + [pl.BlockSpec((tf, D), lambda t, i: (t, 0))],
        out_specs=[pl.BlockSpec((2, tr, tf), lambda t, i: (0, i, t)), pl.BlockSpec((tr, tf), lambda t, i: (i, t)),
                   pl.BlockSpec((2, 8, tf), lambda t, i: (0, 0, t))],
        out_shape=[jax.ShapeDtypeStruct((2, s, FFN), BF16), jax.ShapeDtypeStruct((s, FFN), BF16),
                   jax.ShapeDtypeStruct((2, 8, FFN), F32)],
        scratch_shapes=[pltpu.VMEM((tr + 32, 2 * tf), F32), pltpu.VMEM((tr + 16, 2 * tf), F32)],
        compiler_params=_params("arbitrary", "arbitrary"),
    )(h2, h2, h2, dx2, dx2, fvec, wup, wup, cw, cw, cb, cb, wdn)


def loss_head(y, target):
    s = y.shape[0]
    tr = _pick(s, 512)

    def body(y_ref, t_ref, dx_ref, l_ref):
        @pl.when(pl.program_id(0) == 0)
        def _():
            l_ref[...] = jnp.zeros_like(l_ref)
        err = y_ref[...] - t_ref[...]
        dx_ref[...] = err / float(D)
        l_ref[...] += 0.5 * jnp.sum(jnp.sum(err * err, axis=-1, keepdims=True) / float(D), axis=0, keepdims=True)

    row = pl.BlockSpec((tr, D), lambda i: (i, 0))
    return pl.pallas_call(
        body, name="loss_head", grid=(s // tr,), in_specs=[row, row],
        out_specs=[row, pl.BlockSpec((8, 128), lambda i: (0, 0))],
        out_shape=[jax.ShapeDtypeStruct((s, D), F32), jax.ShapeDtypeStruct((8, 128), F32)],
        compiler_params=_params("arbitrary"),
    )(y, target)


def adamw(parts, w, m, v, name, tok=None):
    nseg = len(parts)
    p, r, c = parts[0].shape
    tok = jnp.zeros((8, 128), F32) if tok is None else tok
    cap = 256 if c > 128 else 2048
    step = lambda q, l, i, ni: jnp.clip((l - q) * ni + i, 0, ni - 1)
    if r <= cap or any(r % t == 0 for t in range(8, cap + 1, 8)):
        tr, tc = _pick(r, cap, 8), c
        ni = r // tr
        row = pl.BlockSpec((None, tr, tc), lambda l, i: (l, i, 0))
        part = lambda q: pl.BlockSpec((p, tr, tc), lambda l, i: (0, step(q, l, i, ni), 0))
    else:
        tr, tc = r, _pick(c, 256)
        ni = c // tc
        row = pl.BlockSpec((None, tr, tc), lambda l, i: (l, 0, i))
        part = lambda q: pl.BlockSpec((p, tr, tc), lambda l, i: (0, 0, step(q, l, i, ni)))

    def body(*refs):
        p_refs = refs[:nseg]
        w_ref, m_ref, v_ref, _, g_out, d_out, m_out, v_out, g_scr = refs[nseg:]
        for q in range(nseg):
            @pl.when(pl.program_id(0) == q)
            def _(q=q):
                g = p_refs[q][0].astype(F32)
                for j in range(1, p):
                    g = g + p_refs[q][j].astype(F32)
                g_scr[...] = g
        g = g_scr[...]
        mn = B1 * m_ref[...] + (1.0 - B1) * g
        vn = B2 * v_ref[...] + (1.0 - B2) * (g * g)
        m_hat = mn / (1.0 - B1 ** STEP)
        v_hat = vn / (1.0 - B2 ** STEP)
        g_out[...] = g
        d_out[...] = -LR * (m_hat / (jnp.sqrt(v_hat) + ADAM_EPS) + WD * w_ref[...])
        m_out[...] = mn
        v_out[...] = vn

    return pl.pallas_call(
        body, name=name, grid=(nseg, ni),
        in_specs=[part(q) for q in range(nseg)] + [row, row, row, pl.BlockSpec((8, 128), lambda l, i: (0, 0))],
        out_specs=[row] * 4, out_shape=[jax.ShapeDtypeStruct((nseg, r, c), F32)] * 4,
        scratch_shapes=[pltpu.VMEM((tr, tc), F32)],
        compiler_params=_params("arbitrary", "arbitrary"),
    )(*parts, w, m, v, tok)


def _padc(a, n):
    return jnp.pad(a, [(0, 0)] * (a.ndim - 1) + [(0, n - a.shape[-1])])


def _swap16(a):
    return jnp.concatenate([a[..., 16:32], a[..., 0:16]], axis=-1)


def _shard_cols(g8, a, b):
    c = g8.shape[2]
    return [g8[j][:, max(a, j * c) - j * c:min(b, (j + 1) * c) - j * c] for j in range(a // c, (b - 1) // c + 1)]


def _padr(a, n):
    return jnp.pad(a, ((0, n - a.shape[0]), (0, 0)))


def _swap16r(a):
    return jnp.concatenate([a[16:32], a[0:16]], axis=0)


def _win_layout(g8):
    w = g8.reshape(NDEV * g8.shape[1], g8.shape[2])
    kr = w[640:672]
    return jnp.concatenate([w[3760:6832], w[2208:3232], w[1184:2208], w[672:1184], w[3232:3744], w[384:640],
                            _padr(kr, 128), _padr(_swap16r(kr), 128), _padr(w[3744:3760], 128),
                            jnp.zeros((128, w.shape[1]), w.dtype), w[0:384]], axis=0)


def _win_grad_shards(g):
    kr = (g[O_KR:O_KR + 32].astype(F32) + _swap16r(g[O_KRS:O_KRS + 32].astype(F32))).astype(g.dtype)
    segs = [(g, O_QL, 384), (g, O_CKV, 256), (kr, 0, 32), (g, O_PU, 512), (g, O_Z, D), (g, O_XS, D), (g, O_BC, 512),
            (g, O_DT, 16), (g, O_G, 3 * D)]
    shards, height = [], sum(w for _, _, w in segs) // NDEV
    for j in range(NDEV):
        a, b, off, pieces = height * j, height * (j + 1), 0, []
        for arr, lo, w in segs:
            s0, s1 = max(a, off), min(b, off + w)
            if s0 < s1:
                pieces.append(arr[lo + s0 - off:lo + s1 - off])
            off += w
        shards.append(jnp.concatenate(pieces, axis=0))
    return jnp.stack(shards).astype(BF16)


def _wq_layout(w):
    w = w.reshape(384, HEADS, 96).transpose(1, 0, 2)
    rope = w[:, :, 64:96]
    return jnp.concatenate([_padc(w[:, :, 0:64], 128), _padc(rope, 128), _padc(_swap16(rope), 128)], axis=2)


def _wq_unlayout(g):
    rope = g[:, :, 128:160] + _swap16(g[:, :, 256:288])
    return jnp.concatenate([g[:, :, 0:64], rope], axis=2).transpose(1, 0, 2).reshape(384, HEADS * 96)


def _wkv_layout(w):
    w = w.reshape(256, HEADS, 128).transpose(1, 0, 2)
    return jnp.concatenate([_padc(w[:, :, 0:64], 128), _padc(w[:, :, 64:128], 128)], axis=2)


def _wkv_unlayout(g):
    return jnp.concatenate([g[:, :, 0:64], g[:, :, 128:192]], axis=2).transpose(1, 0, 2).reshape(256, HEADS * 128)


def _wba_layout(w):
    return jnp.pad(w.reshape(HEADS, 64, D), ((0, 0), (0, 64), (0, 0))).reshape(HEADS * 128, D)


def _rows8(rows, width):
    out = jnp.stack([_padc(r.astype(F32), width) for r in rows])
    return jnp.pad(out, ((0, 8 - out.shape[0]), (0, 0)))


def _mla_vec(qa, kva, qn, kn):
    def row(n):
        return jnp.concatenate([_padc(n[0:64], 128), _padc(n[64:96], 128), _padc(_swap16(n[64:96]), 128)])
    return _rows8([qa, kva, row(qn), row(kn)], 512)


def _mla_unvec(g):
    def un(r):
        return jnp.concatenate([r[0:64], r[128:160] + _swap16(r[256:288])])
    return g[0, 0:384], g[1, 0:256], un(g[2]), un(g[3])


SMALL = (("ada_b", (6 * D,)), ("norm1_w", (D,)), ("q_a_norm", (384,)), ("kv_a_norm", (256,)), ("q_norm", (96,)),
         ("k_norm", (96,)), ("pool_w", (4, 128, 128)), ("pool_scale", (512,)), ("ssd_conv_b", (1536,)),
         ("ssd_dt_bias", (16,)), ("ssd_a_log", (16,)), ("ssd_d", (16,)), ("ssd_norm_w", (D,)), ("norm2_w", (D,)),
         ("ffn_conv_b", (2 * FFN,)), ("ssd_conv_w", (4, 1536)), ("ffn_conv_w", (3, 2 * FFN)))
SHARDED_SMALL = {"ssd_conv_w": 192, "ffn_conv_w": 704}


def _pack_rows(shp):
    return -(-math.prod(shp) // 1024) * 8


def _pack(small):
    pieces = []
    for n, shp in SMALL:
        pieces.append(small[n].reshape(-1).astype(F32))
        fill = _pack_rows(shp) * 128 - math.prod(shp)
        if fill:
            pieces.append(jnp.zeros((fill,), F32))
    return jnp.concatenate(pieces).reshape(-1, 128)


def _unpack_parts(packs):
    out, off = {}, 0
    for n, shp in SMALL:
        rows, size = _pack_rows(shp), math.prod(shp)
        r, c = math.prod(shp[:-1]), shp[-1]
        per_layer = [pk[:, off:off + rows].reshape(NDEV, rows * 128)[:, 0:size].reshape(NDEV, r, c) for pk in packs]
        out[n] = jnp.concatenate(per_layer, axis=1)
        off += rows
    return out


GROUP_A = ("w_in", "w_q_b", "w_kv_b")
GROUP_B = ("w_branch", "w_out", "ffn_up", "ffn_down")
BIG = GROUP_A + GROUP_B
SCATTER_FFN, SCATTER_MERGE = ("ffn_up", "ffn_down"), ("w_branch", "w_out")
COL_SHARDED = ("w_q_b", "w_kv_b")
TRANSPOSED = ("w_in", "ffn_up")


def _behind(arrs, tok):
    arrs = list(arrs)
    j = min(range(len(arrs)), key=lambda q: arrs[q].size)
    arrs[j] = arrs[j] + tok[0, 0].astype(arrs[j].dtype)
    return arrs


def _gathered_full(g, name):
    if name in COL_SHARDED:
        return g.transpose(1, 0, 2).reshape(g.shape[1], NDEV * g.shape[2])
    return g.reshape(NDEV * g.shape[1], g.shape[2])


def _to_shards(full, name):
    if name == "w_in":
        return _win_grad_shards(full)
    if name in COL_SHARDED:
        r, c = full.shape
        return full.reshape(r, NDEV, c // NDEV).transpose(1, 0, 2).astype(BF16)
    r, c = full.shape
    return full.reshape(NDEV, r // NDEV, c).astype(BF16)


def _fwd_a(x, lw, mod, cos2, sin2, l, tok):
    sh1, sc1, g1, sh2, sc2, g2 = [mod[j * D:(j + 1) * D] for j in range(6)]
    vec1 = _rows8([lw["norm1_w"], sh1, sc1], D) + tok[0, 0]
    proj, h1, dt_cols = norm_proj_fwd(x, vec1, lw["win"], f"inproj_fwd{l}")
    q, k, v = mla_pre_fwd(proj, lw["wq"], lw["wkv"], lw["mla_vec"], cos2, sin2, f"mla_pre_fwd{l}")
    oa = mla_attn_fwd(q, k, v, f"mla_attn_fwd{l}")
    ob = pool_fwd(proj, lw["pool_w"], lw["pool_scale"].reshape(1, 512), f"pool_fwd{l}")
    xbc, xt = conv_fwd(proj, lw["ssd_conv_w"], lw["ssd_conv_b"].reshape(1, 1536), f"conv_fwd{l}")
    s = x.shape[0]
    xt = xt.reshape(16, 64, s)
    dt = dt_cols[:, 0:16].T
    dtr = dt[:, None, :]
    hv = lambda a: a.reshape(16, 1, 1)
    yt, hs = ssd_fwd(xt, dtr, xbc, hv(lw["ssd_a_log"]), hv(lw["ssd_dt_bias"]), hv(lw["ssd_d"]), f"ssd_fwd{l}")
    return dict(x=x, vec1=vec1, proj=proj, h1=h1, q=q, k=k, v=v, oa=oa, ob=ob, xbc=xbc, xt=xt, dtr=dtr,
                hs=hs, yt=yt.reshape(D, s), mvec=_rows8([g1, lw["ssd_norm_w"]], D),
                fvec=_rows8([lw["norm2_w"], sh2, sc2, g2], D))


def _fwd_b(sv, lw, l, tok):
    sv["mvec"] = sv["mvec"] + tok[0, 0]
    x1 = merge_fwd(sv["oa"], sv["ob"], sv["yt"], sv["proj"], sv["x"], sv["mvec"], lw["wba"], lw["wbb"], lw["wbc"],
                   lw["wout"], f"merge_fwd{l}")
    x2, h2, pre = ffn_fwd(x1, sv["fvec"], lw["wup"], lw["ffn_conv_w"], lw["ffn_conv_b"].reshape(1, 2 * FFN), lw["wdn"],
                          f"ffn_fwd{l}")
    sv.update(x1=x1, h2=h2, pre=pre)
    return x2


def _bwd_ffn(dx2, lw, sv, l, tok):
    fvec = sv["fvec"] + tok[0, 0]
    dup, act, dcw = ffn_bwd(sv["h2"], dx2, fvec, lw["wup"], lw["ffn_conv_w"], lw["ffn_conv_b"].reshape(1, 2 * FFN),
                            lw["wdn"], f"ffn_bwd{l}")
    grads = dict(ffn_down=tn_matmul(act, dx2, f"dw_down{l}", scale=fvec[3:4]),
                 ffn_up=tn_matmul(dup, sv["h2"], f"dw_up{l}"))
    small = dict(ffn_conv_w=jnp.concatenate([dcw[0, 0:3], dcw[1, 0:3]], axis=1),
                 ffn_conv_b=jnp.concatenate([dcw[0, 3], dcw[1, 3]]))
    return dup, grads, small


def _bwd_merge(dx2, dup, lw, sv, l, tok, small):
    grads = {}
    fvec = sv["fvec"] + tok[0, 0]
    dx1, dfvec = norm_proj_bwd(sv["x1"], fvec, dup, lw["wup"], dx2, sv["pre"], f"ffn_norm_bwd{l}")
    small["norm2_w"] = dfvec[0]
    (doa, dob, dyt, dz, dgl, dx, dmvec, dya, dyb, dyc, dpre, oc, merged) = merge_bwd(
        sv["oa"], sv["ob"], sv["yt"], sv["proj"], sv["x"], sv["mvec"], lw["wba"], lw["wbb"], lw["wbc"], lw["wout"], dx1,
        f"merge_bwd{l}")
    dwba = tn_matmul(sv["oa"], dya, f"dw_ba{l}").reshape(HEADS, 128, D)[:, 0:64].reshape(512, D)
    grads["w_branch"] = jnp.concatenate([dwba, tn_matmul(sv["ob"], dyb, f"dw_bb{l}"), tn_matmul(oc, dyc, f"dw_bc{l}")])
    grads["w_out"] = tn_matmul(merged, dpre, f"dw_out{l}")
    small["ssd_norm_w"] = dmvec[1]
    small["dmod_b"] = (dmvec[0], dfvec[1], dfvec[2], dfvec[3])
    return dx, dict(doa=doa, dob=dob, dyt=dyt, dz=dz, dgl=dgl), grads, small


def _bwd_a(dx, cot, lw, sv, cos2, sin2, l, tok, small):
    s = dx.shape[0]
    grads = {}
    doa, dob, dz, dgl = cot["doa"], cot["dob"], cot["dz"], cot["dgl"]
    hv = lambda a: a.reshape(16, 1, 1)
    dxt, ddtr, dbm, dcm, dal, ddb, ddk = ssd_bwd(
        sv["xt"], sv["dtr"], sv["xbc"], hv(lw["ssd_a_log"]) + tok[0, 0], hv(lw["ssd_dt_bias"]),
        hv(lw["ssd_d"]), sv["hs"], cot["dyt"].reshape(16, 64, s), f"ssd_bwd{l}")
    small["ssd_a_log"], small["ssd_dt_bias"], small["ssd_d"] = dal.reshape(16), ddb.reshape(16), ddk.reshape(16)
    dxbc, dscw, dscb = conv_bwd(sv["proj"], lw["ssd_conv_w"], lw["ssd_conv_b"].reshape(1, 1536), dxt.reshape(D, s),
                                dbm, dcm, f"conv_bwd{l}")
    small["ssd_conv_w"], small["ssd_conv_b"] = dscw, dscb.reshape(1536)
    ddt = ddtr[:, 0, :].T
    du, dpw, dps = pool_bwd(sv["proj"], lw["pool_w"], lw["pool_scale"].reshape(1, 512), dob, f"pool_bwd{l}")
    small["pool_w"], small["pool_scale"] = dpw, dps.reshape(512)
    dq, dk, dv = mla_attn_bwd(sv["q"], sv["k"], sv["v"], doa, f"mla_attn_bwd{l}")
    dql, dckv, dkr, dkrs, dwq, dwkv, dmv = mla_pre_bwd(sv["proj"], lw["wq"], lw["wkv"], lw["mla_vec"], cos2, sin2,
                                                       dq, dk, dv, f"mla_pre_bwd{l}")
    grads["w_q_b"], grads["w_kv_b"] = _wq_unlayout(dwq), _wkv_unlayout(dwkv)
    small["q_a_norm"], small["kv_a_norm"], small["q_norm"], small["k_norm"] = _mla_unvec(dmv)
    dproj = jnp.concatenate([dgl, dxbc[:, 0:D], dz, du, dxbc[:, D:1536], dckv, dkr, dkrs,
                             _padc(ddt, 128).astype(BF16), jnp.zeros((s, 128), BF16), dql], axis=1)
    grads["w_in"] = tn_matmul(dproj, sv["h1"], f"dw_in{l}")
    return dproj, grads, small


def _bwd_in(dx, dproj, lw, sv, l, tok, small):
    dx0, dvec1 = norm_proj_bwd(sv["x"], sv["vec1"] + tok[0, 0], dproj, lw["win"], dx, None, f"inproj_bwd{l}")
    small["norm1_w"] = dvec1[0]
    small["ada_b"] = jnp.concatenate([dvec1[1], dvec1[2], *small.pop("dmod_b")])
    return dx0, small


def kernel(x, c, positions, ada_w, ada_b, norm1_w, w_in, q_a_norm, w_q_b, kv_a_norm, w_kv_b, q_norm, k_norm, pool_w, pool_scale, ssd_conv_w, ssd_conv_b, ssd_dt_bias, ssd_a_log, ssd_d, ssd_norm_w, w_branch, w_out, norm2_w, ffn_up, ffn_conv_w, ffn_conv_b, ffn_down, loss_target, m_ada_w, m_ada_b, m_norm1_w, m_w_in, m_q_a_norm, m_w_q_b, m_kv_a_norm, m_w_kv_b, m_q_norm, m_k_norm, m_pool_w, m_pool_scale, m_ssd_conv_w, m_ssd_conv_b, m_ssd_dt_bias, m_ssd_a_log, m_ssd_d, m_ssd_norm_w, m_w_branch, m_w_out, m_norm2_w, m_ffn_up, m_ffn_conv_w, m_ffn_conv_b, m_ffn_down, v_ada_w, v_ada_b, v_norm1_w, v_w_in, v_q_a_norm, v_w_q_b, v_kv_a_norm, v_w_kv_b, v_q_norm, v_k_norm, v_pool_w, v_pool_scale, v_ssd_conv_w, v_ssd_conv_b, v_ssd_dt_bias, v_ssd_a_log, v_ssd_d, v_ssd_norm_w, v_w_branch, v_w_out, v_norm2_w, v_ffn_up, v_ffn_conv_w, v_ffn_conv_b, v_ffn_down):
    p = dict(ada_w=ada_w, ada_b=ada_b, norm1_w=norm1_w, w_in=w_in, q_a_norm=q_a_norm, w_q_b=w_q_b, kv_a_norm=kv_a_norm,
             w_kv_b=w_kv_b, q_norm=q_norm, k_norm=k_norm, pool_w=pool_w, pool_scale=pool_scale, ssd_conv_w=ssd_conv_w,
             ssd_conv_b=ssd_conv_b, ssd_dt_bias=ssd_dt_bias, ssd_a_log=ssd_a_log, ssd_d=ssd_d, ssd_norm_w=ssd_norm_w,
             w_branch=w_branch, w_out=w_out, norm2_w=norm2_w, ffn_up=ffn_up, ffn_conv_w=ffn_conv_w, ffn_conv_b=ffn_conv_b,
             ffn_down=ffn_down)
    mom = dict(ada_w=m_ada_w, ada_b=m_ada_b, norm1_w=m_norm1_w, w_in=m_w_in, q_a_norm=m_q_a_norm, w_q_b=m_w_q_b,
               kv_a_norm=m_kv_a_norm, w_kv_b=m_w_kv_b, q_norm=m_q_norm, k_norm=m_k_norm, pool_w=m_pool_w,
               pool_scale=m_pool_scale, ssd_conv_w=m_ssd_conv_w, ssd_conv_b=m_ssd_conv_b, ssd_dt_bias=m_ssd_dt_bias,
               ssd_a_log=m_ssd_a_log, ssd_d=m_ssd_d, ssd_norm_w=m_ssd_norm_w, w_branch=m_w_branch, w_out=m_w_out,
               norm2_w=m_norm2_w, ffn_up=m_ffn_up, ffn_conv_w=m_ffn_conv_w, ffn_conv_b=m_ffn_conv_b, ffn_down=m_ffn_down)
    var = dict(ada_w=v_ada_w, ada_b=v_ada_b, norm1_w=v_norm1_w, w_in=v_w_in, q_a_norm=v_q_a_norm, w_q_b=v_w_q_b,
               kv_a_norm=v_kv_a_norm, w_kv_b=v_w_kv_b, q_norm=v_q_norm, k_norm=v_k_norm, pool_w=v_pool_w,
               pool_scale=v_pool_scale, ssd_conv_w=v_ssd_conv_w, ssd_conv_b=v_ssd_conv_b, ssd_dt_bias=v_ssd_dt_bias,
               ssd_a_log=v_ssd_a_log, ssd_d=v_ssd_d, ssd_norm_w=v_ssd_norm_w, w_branch=v_w_branch, w_out=v_w_out,
               norm2_w=v_norm2_w, ffn_up=v_ffn_up, ffn_conv_w=v_ffn_conv_w, ffn_conv_b=v_ffn_conv_b, ffn_down=v_ffn_down)
    names = list(p)
    me = 4 * lax.axis_index("x") + 2 * lax.axis_index("y") + lax.axis_index("c")
    xs, tgt = x[0], loss_target[0]
    s = xs.shape[0]

    inv_freq = ROPE_THETA ** (-jnp.arange(0, 32, 2, dtype=F32) / 32.0)
    ang = positions[0].astype(F32)[:, None] * inv_freq
    cos, sin = jnp.cos(ang), jnp.sin(ang)
    cos2 = _padc(jnp.concatenate([cos, cos], axis=1), 128)
    sin2 = _padc(jnp.concatenate([-sin, sin], axis=1), 128)

    conv_shards = jnp.concatenate([ssd_conv_w.reshape(-1), ffn_conv_w.reshape(-1)])
    (c_all, conv_all), _ = all_to_all([c, conv_shards], [True, True], "gather_c")
    modp, cact = ada_mod(jnp.pad(c_all.reshape(NDEV, D), ((0, 8), (0, 0))), ada_w)
    (mod_in,), tok = all_to_all([modp[:, 0:NDEV].transpose(1, 0, 2)], [False], "scatter_mod")
    mod = mod_in.transpose(1, 0, 2).reshape(LAYERS, 6 * D) + ada_b

    n1 = LAYERS * 4 * 192
    scw = conv_all[:, :n1].reshape(NDEV, LAYERS, 4, 192).transpose(1, 2, 0, 3).reshape(LAYERS, 4, 1536)
    fcw = conv_all[:, n1:].reshape(NDEV, LAYERS, 3, 704).transpose(1, 2, 0, 3).reshape(LAYERS, 3, 2 * FFN)

    def weights_a(gathered, l):
        full = {n: _gathered_full(g, n) for n, g in zip(GROUP_A[1:], gathered[1:])}
        lw = {n: p[n][l] for n in names}
        lw.update(win=_win_layout(gathered[0]), wq=_wq_layout(full["w_q_b"]), wkv=_wkv_layout(full["w_kv_b"]),
                  ssd_conv_w=scw[l], ffn_conv_w=fcw[l],
                  mla_vec=_mla_vec(lw["q_a_norm"], lw["kv_a_norm"], lw["q_norm"], lw["k_norm"]))
        return lw

    def weights_b(gathered):
        full = {n: _gathered_full(g, n) for n, g in zip(GROUP_B, gathered)}
        wb = full["w_branch"]
        return dict(wba=_wba_layout(wb[0:512]), wbb=wb[512:1024], wbc=wb[1024:2048], wout=full["w_out"],
                    wup=full["ffn_up"], wdn=full["ffn_down"])

    shards = lambda group, l: [(p[n][l].T if n in TRANSPOSED else p[n][l]).astype(BF16) for n in group]
    lws, saved = [None] * LAYERS, [None] * LAYERS
    st, tok = gather_start(_behind(shards(GROUP_A, 0), tok), "gather_a0")
    cast = {(g[0], l): shards(g, l) for g in (GROUP_A, GROUP_B) for l in range(LAYERS)}
    adam_in = {n: [a.transpose(0, 2, 1) for a in (p[n], mom[n], var[n])] for n in TRANSPOSED}
    early = [a for v in cast.values() for a in v] + [a for v in adam_in.values() for a in v]
    got, tok = gather_finish(st, [tok] + early, "gather_a0")
    h = xs
    for l in range(LAYERS):
        st, tok = gather_start(_behind(cast[(GROUP_B[0], l)], tok), f"gather_b{l}")
        lws[l] = weights_a(got, l)
        saved[l] = _fwd_a(h, lws[l], mod[l], cos2, sin2, l, tok)
        got, tok = gather_finish(st, saved[l]["yt"], f"gather_b{l}")
        lws[l].update(weights_b(got))
        if l + 1 < LAYERS:
            st, tok = gather_start(_behind(cast[(GROUP_A[0], l + 1)], tok), f"gather_a{l + 1}")
        h = _fwd_b(saved[l], lws[l], l, tok)
        if l + 1 < LAYERS:
            got, tok = gather_finish(st, h, f"gather_a{l + 1}")
    dx, lpart = loss_head(h, tgt)
    loss = lax.psum(lpart[0, 0], ("x", "y", "c"))
    tok = tok + loss * 0.0

    small, parts, packs = [None] * LAYERS, {n: [None] * LAYERS for n in BIG}, [None] * LAYERS
    st = None

    def scatter(grads, group, l, tok, extra=None):
        arrs, flags = [_to_shards(grads[n], n) for n in group], [False] * len(group)
        if extra is not None:
            arrs, flags = arrs + [extra], flags + [True]
        return exchange_start(_behind(arrs, tok), flags, f"scatter_{group[0]}{l}_start")

    def landed(state, group, l, after):
        got, tok, _ = exchange_wait(state, after, f"scatter_{group[0]}{l}_wait")
        for n, g in zip(group, got):
            parts[n][l] = g
        return got, tok

    for l in reversed(range(LAYERS)):
        dup, g_ffn, small[l] = _bwd_ffn(dx, lws[l], saved[l], l, tok)
        if st is not None:
            _, tok = landed(st, GROUP_A, l + 1, dup)
        st, tok = scatter(g_ffn, SCATTER_FFN, l, tok)
        dx, cot, g_merge, small[l] = _bwd_merge(dx, dup, lws[l], saved[l], l, tok, small[l])
        _, tok = landed(st, SCATTER_FFN, l, dx)
        st, tok = scatter(g_merge, SCATTER_MERGE, l, tok, _pack(small[l + 1]) if l + 1 < LAYERS else None)
        dproj, g_in, small[l] = _bwd_a(dx, cot, lws[l], saved[l], cos2, sin2, l, tok, small[l])
        got, tok = landed(st, SCATTER_MERGE, l, dproj)
        if l + 1 < LAYERS:
            packs[l + 1] = got[-1]
        st, tok = scatter(g_in, GROUP_A, l, tok)
        dx, small[l] = _bwd_in(dx, dproj, lws[l], saved[l], l, tok, small[l])

    dmod = jnp.stack([small[q]["ada_b"] for q in range(LAYERS)])
    st_small, tok = exchange_start(_behind([_pack(small[0]), dmod.reshape(LAYERS, NDEV, 768).transpose(1, 0, 2)], tok),
                                   [True, False], "scatter_s0_start")
    out = {}

    def big_adamw(group, tok):
        res = None
        for n in group:
            if n in TRANSPOSED:
                res = adamw(parts[n], *adam_in[n], f"adamw_{n}", tok)
                out[n] = [a.transpose(0, 2, 1) for a in res]
            else:
                res = out[n] = adamw(parts[n], p[n], mom[n], var[n], f"adamw_{n}", tok)
        return res[0]

    g_last = big_adamw(GROUP_B, tok)
    _, tok = landed(st, GROUP_A, 0, g_last)
    (packs[0], dmod_in), _, _ = exchange_wait(st_small, tok, "scatter_s0_wait")
    big_adamw(GROUP_A, None)

    dmod16 = jnp.pad(dmod_in, ((0, 8), (0, 0), (0, 0)))
    g_ada = [tn_matmul(cact, dmod16[:, l], f"dw_ada{l}", out_dtype=F32)[None] for l in range(LAYERS)]
    out["ada_w"] = adamw(g_ada, ada_w, m_ada_w, v_ada_w, "adamw_ada_w")

    for n, pt in _unpack_parts(packs).items():
        if n in SHARDED_SMALL:
            w = SHARDED_SMALL[n]
            pt = lax.dynamic_slice_in_dim(pt, me * w, w, axis=2)
        r, c = pt.shape[1:]
        res = adamw([pt], p[n].reshape(1, r, c), mom[n].reshape(1, r, c), var[n].reshape(1, r, c), f"adamw_{n}")
        out[n] = [a.reshape(p[n].shape) for a in res]

    outs = [loss, dx[None]]
    for q in range(4):
        outs += [out[n][q] for n in names]
    return tuple(outs)
```

```python
import functools
import math

import jax
import jax.numpy as jnp
from jax import lax
from jax.experimental import pallas as pl
from jax.experimental.pallas import tpu as pltpu

F32, BF16 = jnp.float32, jnp.bfloat16
EPS = 1e-6
D = 1024
NDEV = 8
LAYERS = 2
HEADS = 8
FFN = 2816
FFN_TILE = 1408
FFN_NT = FFN // FFN_TILE
ATT_SCALE = 96 ** -0.5
ROPE_THETA = 10000.0
LR, B1, B2, ADAM_EPS, WD, STEP = 0.001, 0.9, 0.999, 1e-08, 0.01, 10

O_G, O_XS, O_Z, O_PU, O_BC, O_CKV, O_KR, O_KRS, O_DT, O_QL = 0, 3072, 4096, 5120, 5632, 6144, 6400, 6528, 6656, 6912
NPROJ = 7296
CONST = dict(pipeline_mode=pl.Buffered(1))


def _pick(n, cap, mult=128):
    if n <= cap:
        return n
    best = None
    for t in range(mult, cap + 1, mult):
        if n % t == 0:
            best = t
    assert best is not None, (n, cap, mult)
    return best


def _sig(x):
    return 1.0 / (1.0 + jnp.exp(-x))


def _rms(x, w, n):
    return x * lax.rsqrt(jnp.sum(x * x, axis=-1, keepdims=True) / n + EPS) * w


def _raw(a, b, dims):
    return lax.dot_general(a.astype(BF16), b.astype(BF16), dims, preferred_element_type=F32)


_NN = (((1,), (0,)), ((), ()))
_NT = (((1,), (1,)), ((), ()))
_TN = (((0,), (0,)), ((), ()))
_BNN = (((2,), (1,)), ((0,), (0,)))
_BNT = (((2,), (2,)), ((0,), (0,)))
_BTN = (((1,), (1,)), ((0,), (0,)))


@jax.custom_vjp
def mm_nn(a, b):
    return _raw(a, b, _NN)


mm_nn.defvjp(lambda a, b: (_raw(a, b, _NN), (a, b)),
             lambda r, g: (_raw(g, r[1], _NT), _raw(r[0], g, _TN)))


@jax.custom_vjp
def mm_nc(a, b):
    return _raw(a, b, _NN)


mm_nc.defvjp(lambda a, b: (_raw(a, b, _NN), b),
             lambda b, g: (_raw(g, b, _NT), jnp.zeros_like(b)))


@jax.custom_vjp
def mm_nt(a, b):
    return _raw(a, b, _NT)


mm_nt.defvjp(lambda a, b: (_raw(a, b, _NT), (a, b)),
             lambda r, g: (_raw(g, r[1], _NN), _raw(g, r[0], _TN)))


@jax.custom_vjp
def bmm_nn(a, b):
    return _raw(a, b, _BNN)


bmm_nn.defvjp(lambda a, b: (_raw(a, b, _BNN), (a, b)),
              lambda r, g: (_raw(g, r[1], _BNT), _raw(r[0], g, _BTN)))


@jax.custom_vjp
def bmm_nt(a, b):
    return _raw(a, b, _BNT)


bmm_nt.defvjp(lambda a, b: (_raw(a, b, _BNT), (a, b)),
              lambda r, g: (_raw(g, r[1], _BNN), _raw(g, r[0], _BTN)))


@jax.custom_vjp
def softplus(x):
    t = jnp.exp(-jnp.abs(x))
    u = 1.0 + t
    one = u == 1.0
    l1p = jnp.where(one, t, jnp.log(u) * (t / jnp.where(one, 1.0, u - 1.0)))
    return jnp.maximum(x, 0.0) + l1p


softplus.defvjp(lambda x: (softplus(x), x), lambda x, g: (g * _sig(x),))


def _params(*sem):
    return pltpu.CompilerParams(dimension_semantics=sem, vmem_limit_bytes=56 * 1024 * 1024)


def all_to_all(arrs, bcast, name):
    n = len(arrs)
    out_shapes = [jax.ShapeDtypeStruct(((NDEV,) + a.shape) if b else a.shape, a.dtype) for a, b in zip(arrs, bcast)]

    def body(*refs):
        ins, outs, token = refs[:n], refs[n:2 * n], refs[2 * n]
        send_sems, recv_sems, local_sems = refs[2 * n + 1:]
        me, remote = _exchange_copies(ins, outs, bcast, send_sems, recv_sems)
        local = [pltpu.make_async_copy(ins[j] if bcast[j] else ins[j].at[me], outs[j].at[me], local_sems.at[j])
                 for j in range(n)]
        for cp in local + remote:
            cp.start()
        for cp in remote + local:
            cp.wait()
        token[...] = jnp.zeros_like(token)

    any_spec = pl.BlockSpec(memory_space=pl.ANY)
    res = pl.pallas_call(
        body, name=name, out_shape=out_shapes + [jax.ShapeDtypeStruct((8, 128), F32)], in_specs=[any_spec] * n,
        out_specs=[any_spec] * n + [pl.BlockSpec(memory_space=pltpu.VMEM)],
        scratch_shapes=[pltpu.SemaphoreType.DMA((7 * n,)), pltpu.SemaphoreType.DMA((7 * n,)),
                        pltpu.SemaphoreType.DMA((n,))],
        compiler_params=pltpu.CompilerParams(has_side_effects=True),
    )(*arrs)
    return res[:n], res[n]


def _peers():
    x, y, c = lax.axis_index("x"), lax.axis_index("y"), lax.axis_index("c")
    out = []
    for k in range(1, NDEV):
        px, py, pc = x ^ ((k >> 2) & 1), y ^ ((k >> 1) & 1), c ^ (k & 1)
        out.append(((px, py, pc), 4 * px + 2 * py + pc))
    return 4 * x + 2 * y + c, out


COPIES = {"all": 7, "chips": 3, "pass": 4}


def _exchange_copies(ins, lands, bcast, send_sems, recv_sems, mode="all"):
    x, y, c = lax.axis_index("x"), lax.axis_index("y"), lax.axis_index("c")
    me = 4 * x + 2 * y + c
    n, copies = len(ins), []

    def add(q, j, src, dst, dev):
        copies.append(pltpu.make_async_remote_copy(
            src_ref=src, dst_ref=dst, send_sem=send_sems.at[q * n + j], recv_sem=recv_sems.at[q * n + j],
            device_id=dev, device_id_type=pl.DeviceIdType.MESH))

    if mode == "pass":
        for q in range(4):
            slot = 4 * (x ^ (q >> 1)) + 2 * (y ^ (q & 1)) + c
            for j in range(n):
                add(q, j, ins[j] if q == 0 else lands[j].at[slot], lands[j].at[slot], (x, y, 1 - c))
        return me, copies
    for q, k in enumerate(range(1, NDEV) if mode == "all" else (2, 4, 6)):
        px, py, pc = x ^ ((k >> 2) & 1), y ^ ((k >> 1) & 1), c ^ (k & 1)
        for j in range(n):
            add(q, j, ins[j] if bcast[j] else ins[j].at[4 * px + 2 * py + pc], lands[j].at[me], (px, py, pc))
    return me, copies


_HBM = pl.BlockSpec(memory_space=pltpu.HBM)
_SEM = pl.BlockSpec(memory_space=pltpu.SEMAPHORE)
_EFFECT = pltpu.SideEffectType.DATAFLOW_SIDE_EFFECTING


def exchange_start(arrs, bcast, name, mode="all", lands=None):
    n, ncp = len(arrs), COPIES[mode] * len(arrs)
    land_shapes = [((NDEV,) + a.shape) if b else a.shape for a, b in zip(arrs, bcast)]
    if lands is None:
        lands = [lax.empty(s_, a.dtype) for s_, a in zip(land_shapes, arrs)]

    def body(*refs):
        in_refs, land_refs = refs[:n], refs[n:2 * n]
        send_sems, recv_sems = refs[2 * n], refs[2 * n + 1]
        token = refs[-1]
        _, copies = _exchange_copies(in_refs, land_refs, bcast, send_sems, recv_sems, mode)
        for cp in copies:
            cp.start()
        token[...] = jnp.zeros_like(token)

    hbm = lambda shp, a: pltpu.HBM(shp, a.dtype)
    res = pl.pallas_call(
        body, name=name,
        out_shape=[pltpu.SemaphoreType.DMA((ncp,)), pltpu.SemaphoreType.DMA((ncp,))]
                  + [hbm(a.shape, a) for a in arrs] + [hbm(s_, a) for s_, a in zip(land_shapes, arrs)]
                  + [jax.ShapeDtypeStruct((8, 128), F32)],
        in_specs=[_HBM] * (2 * n), out_specs=[_SEM, _SEM] + [_HBM] * (2 * n) + [pl.BlockSpec(memory_space=pltpu.VMEM)],
        input_output_aliases={i: 2 + i for i in range(2 * n)},
        compiler_params=pltpu.CompilerParams(has_side_effects=_EFFECT),
    )(*[pltpu.with_memory_space_constraint(a, pltpu.HBM) for a in arrs],
      *[pltpu.with_memory_space_constraint(a, pltpu.HBM) for a in lands])
    return (res[0], res[1], res[2:2 + n], res[2 + n:2 + 2 * n], tuple(bcast), mode), res[-1]


def exchange_wait(state, after, name):
    send_sems, recv_sems, ins, lands, bcast, mode = state
    n = len(ins)

    def body(*refs):
        in_refs, land_refs = refs[:n], refs[n:2 * n]
        s_sems, r_sems = refs[2 * n], refs[2 * n + 1]
        token = refs[-1]
        _, copies = _exchange_copies(in_refs, land_refs, bcast, s_sems, r_sems, mode)
        for cp in copies:
            cp.wait_send()
            cp.wait_recv()
        token[...] = jnp.zeros_like(token)

    res = pl.pallas_call(
        body, name=name,
        out_shape=[pltpu.HBM(a.shape, a.dtype) for a in ins] + [pltpu.HBM(a.shape, a.dtype) for a in lands]
                  + [jax.ShapeDtypeStruct((8, 128), F32)],
        in_specs=[_HBM] * (2 * n) + [_SEM, _SEM, pl.BlockSpec(memory_space=pl.ANY)],
        out_specs=[_HBM] * (2 * n) + [pl.BlockSpec(memory_space=pltpu.VMEM)],
        input_output_aliases={i: i for i in range(2 * n)},
        compiler_params=pltpu.CompilerParams(has_side_effects=_EFFECT),
    )(*ins, *lands, send_sems, recv_sems, after)
    if mode == "chips":
        return list(res[n:2 * n]), res[-1], list(res[:n])
    me = 4 * lax.axis_index("x") + 2 * lax.axis_index("y") + lax.axis_index("c")
    got = []
    for j in range(n):
        own = res[j][None] if bcast[j] else lax.dynamic_index_in_dim(res[j], me, 0, keepdims=True)
        got.append(lax.dynamic_update_slice_in_dim(res[n + j], own, me, axis=0))
    return got, res[-1], list(res[:n])


def gather_start(shards, name):
    return exchange_start(shards, [True] * len(shards), name + "_chips_start", mode="chips")


def gather_finish(state, after, name):
    lands, _, sent = exchange_wait(state, after, name + "_chips_wait")
    state, tok = exchange_start(sent, [True] * len(sent), name + "_pass_start", mode="pass", lands=lands)
    got, tok, _ = exchange_wait(state, tok, name + "_pass_wait")
    return got, tok


def norm_proj_fwd(x, vec, w, name):
    s, n = x.shape[0], w.shape[0]
    tr, tn = _pick(s, 512), _pick(n, 2560)
    ni, jdt, odt = s // tr, O_DT // tn, O_DT % tn

    def body(x_ref, v_ref, w_ref, o_ref, h_ref, dt_ref, h_scr):
        j, i = pl.program_id(0), pl.program_id(1)
        rows = pl.ds(pl.multiple_of(i * tr, tr), tr)

        @pl.when(j == 0)
        def _():
            h = _rms(x_ref[...], v_ref[0:1, :], D) * (1.0 + v_ref[2:3, :]) + v_ref[1:2, :]
            h_scr[rows, :] = h.astype(BF16)
            h_ref[...] = h.astype(BF16)
        res = _raw(h_scr[rows, :], w_ref[...], _NT)
        o_ref[...] = res

        @pl.when(j == jdt)
        def _():
            dt_ref[...] = res[:, odt:odt + 128]

    first = lambda j, i: (jnp.where(j == 0, i, ni - 1), 0)
    dtix = lambda j, i: (jnp.where(j < jdt, 0, jnp.where(j == jdt, i, ni - 1)), 0)
    return pl.pallas_call(
        body, name=name, grid=(n // tn, ni),
        in_specs=[pl.BlockSpec((tr, D), first), pl.BlockSpec((8, D), lambda j, i: (0, 0)),
                  pl.BlockSpec((tn, D), lambda j, i: (j, 0))],
        out_specs=[pl.BlockSpec((tr, tn), lambda j, i: (i, j)), pl.BlockSpec((tr, D), first),
                   pl.BlockSpec((tr, 128), dtix)],
        out_shape=[jax.ShapeDtypeStruct((s, n), F32), jax.ShapeDtypeStruct((s, D), BF16),
                   jax.ShapeDtypeStruct((s, 128), F32)],
        scratch_shapes=[pltpu.VMEM((s, D), BF16)],
        compiler_params=_params("arbitrary", "arbitrary"),
    )(x, vec, w)


def _col_tiles(arr, cap):
    if arr.ndim == 2:
        n = arr.shape[1]
        t = _pick(n, cap)
        return n, t, lambda rows, ix: pl.BlockSpec((rows, t), lambda *g: ix(*g))
    width = arr.shape[2]
    t = _pick(width, cap)
    per = width // t

    def spec(rows, ix):
        def index(*g):
            r, j = ix(*g)
            return (j // per, r, j % per)
        return pl.BlockSpec((None, rows, t), index)
    return arr.shape[0] * width, t, spec


def norm_proj_bwd(x, vec, dp, w, dx_in, aux, name):
    s = x.shape[0]
    tr = _pick(s, 512)
    n, tk, dp_spec = _col_tiles(dp, 2560)
    nk, has_aux = n // tk, aux is not None

    def body(*refs):
        if has_aux:
            x_ref, v_ref, dp_ref, w_ref, dxin_ref, aux_ref, dx_ref, dv_ref, acc = refs
        else:
            x_ref, v_ref, dp_ref, w_ref, dxin_ref, dx_ref, dv_ref, acc = refs
        k, i = pl.program_id(0), pl.program_id(1)
        rows = pl.ds(pl.multiple_of(i * tr, tr), tr)
        part = _raw(dp_ref[...], w_ref[...], _NN)

        @pl.when(k == 0)
        def _():
            acc[rows, :] = part

        @pl.when(k > 0)
        def _():
            acc[rows, :] += part

        @pl.when(k == nk - 1)
        def _():
            f = lambda xx, nw, sh, sc: _rms(xx, nw, D) * (1.0 + sc) + sh
            _, vjp = jax.vjp(f, x_ref[...], v_ref[0:1, :], v_ref[1:2, :], v_ref[2:3, :])
            dx, dnw, dsh, dsc = vjp(acc[rows, :])
            dx_ref[...] = dxin_ref[...] + dx

            @pl.when(i == 0)
            def _():
                dv_ref[...] = jnp.zeros_like(dv_ref)

            dv_ref[0:1, :] += dnw
            dv_ref[1:2, :] += dsh
            dv_ref[2:3, :] += dsc
            if has_aux:
                dv_ref[3:4, :] += jnp.sum(dxin_ref[...] * aux_ref[...], axis=0, keepdims=True)

    row = pl.BlockSpec((tr, D), lambda k, i: (jnp.where(k == nk - 1, i, 0), 0))
    in_specs = [row, pl.BlockSpec((8, D), lambda k, i: (0, 0)), dp_spec(tr, lambda k, i: (i, k)),
                pl.BlockSpec((tk, D), lambda k, i: (k, 0)), row] + ([row] if has_aux else [])
    args = [x, vec, dp, w, dx_in] + ([aux] if has_aux else [])
    return pl.pallas_call(
        body, name=name, grid=(nk, s // tr), in_specs=in_specs,
        out_specs=[row, pl.BlockSpec((8, D), lambda k, i: (0, 0))],
        out_shape=[jax.ShapeDtypeStruct((s, D), F32), jax.ShapeDtypeStruct((8, D), F32)],
        scratch_shapes=[pltpu.VMEM((s, D), F32)],
        compiler_params=_params("arbitrary", "arbitrary"),
    )(*args)


def tn_matmul(a, b, name, scale=None, out_dtype=None):
    out_dtype = BF16 if out_dtype is None else out_dtype
    s = b.shape[-2]
    ts = _pick(s, 512, 16)
    m, tm, a_spec = _col_tiles(a, 2560 if b.shape[-1] <= D else 1408)
    n, tn, b_spec = _col_tiles(b, 2560)
    ns, has_scale = s // ts, scale is not None

    def body(*refs):
        if has_scale:
            a_ref, b_ref, sc_ref, o_ref, acc = refs
        else:
            a_ref, b_ref, o_ref, acc = refs
        k = pl.program_id(2)

        @pl.when(k == 0)
        def _():
            acc[...] = jnp.zeros_like(acc)

        acc[...] += _raw(a_ref[...], b_ref[...], _TN)

        @pl.when(k == ns - 1)
        def _():
            o_ref[...] = (acc[...] * sc_ref[...] if has_scale else acc[...]).astype(out_dtype)

    in_specs = [a_spec(ts, lambda i, j, k: (k, i)), b_spec(ts, lambda i, j, k: (k, j))]
    if has_scale:
        in_specs.append(pl.BlockSpec((1, tn), lambda i, j, k: (0, j)))
    return pl.pallas_call(
        body, name=name, grid=(m // tm, n // tn, ns), in_specs=in_specs,
        out_specs=pl.BlockSpec((tm, tn), lambda i, j, k: (i, j)),
        out_shape=jax.ShapeDtypeStruct((m, n), out_dtype),
        scratch_shapes=[pltpu.VMEM((tm, tn), F32)],
        compiler_params=_params("arbitrary", "arbitrary", "arbitrary"),
    )(*([a, b] + ([scale] if has_scale else [])))


def ada_mod(c16, w):
    ncol = w.shape[2]

    def body(c_ref, w_ref, o_ref, a_ref):
        cc = c_ref[...]
        act = cc * _sig(cc)
        a_ref[...] = act
        o_ref[...] = _raw(act, w_ref[...], _NN)

    return pl.pallas_call(
        body, name="ada_mod", grid=(LAYERS,),
        in_specs=[pl.BlockSpec((16, D), lambda l: (0, 0)), pl.BlockSpec((None, D, ncol), lambda l: (l, 0, 0))],
        out_specs=[pl.BlockSpec((None, 16, ncol), lambda l: (l, 0, 0)), pl.BlockSpec((16, D), lambda l: (0, 0))],
        out_shape=[jax.ShapeDtypeStruct((LAYERS, 16, ncol), F32), jax.ShapeDtypeStruct((16, D), F32)],
        compiler_params=_params("arbitrary"),
    )(c16, w)


def _mla_shared(q_lat, c_kv, kr, krs, qa_w, kva_w, kr_w, krs_w, cos2, sin2):
    qn = _rms(q_lat, qa_w, 384.0)
    kvn = _rms(c_kv, kva_w, 256.0)
    rk = lax.rsqrt(jnp.sum(kr * kr, axis=-1, keepdims=True) / 32.0 + EPS)
    krope = rk * (kr * kr_w * cos2 + krs * krs_w * sin2)
    return qn, kvn, krope


def _mla_head(qn, kvn, wqn, wqr, wqrs, wkn, wv, qn_w, qr_w, qrs_w, kn_w, cos2, sin2):
    qnope = _rms(mm_nn(qn, wqn), qn_w, 64.0)
    qr, qrs = mm_nn(qn, wqr), mm_nn(qn, wqrs)
    rq = lax.rsqrt(jnp.sum(qr * qr, axis=-1, keepdims=True) / 32.0 + EPS)
    qrope = rq * (qr * qr_w * cos2 + qrs * qrs_w * sin2)
    knope = _rms(mm_nn(kvn, wkn), kn_w, 64.0)
    return qnope, qrope, knope, mm_nn(kvn, wv)


def _mla_vec_pieces(v_ref):
    return ((v_ref[0:1, 0:384], v_ref[1:2, 0:256], v_ref[3:4, 128:256], v_ref[3:4, 256:384]),
            (v_ref[2:3, 0:128], v_ref[2:3, 128:256], v_ref[2:3, 256:384], v_ref[3:4, 0:128]))


def _mla_in_specs(tr):
    return [pl.BlockSpec((tr, 384), lambda i: (i, O_QL // 384)), pl.BlockSpec((tr, 256), lambda i: (i, O_CKV // 256)),
            pl.BlockSpec((tr, 128), lambda i: (i, O_KR // 128)), pl.BlockSpec((tr, 128), lambda i: (i, O_KRS // 128)),
            pl.BlockSpec((HEADS, 384, 384), lambda i: (0, 0, 0), **CONST),
            pl.BlockSpec((HEADS, 256, 256), lambda i: (0, 0, 0), **CONST),
            pl.BlockSpec((8, 512), lambda i: (0, 0)),
            pl.BlockSpec((tr, 128), lambda i: (i, 0)), pl.BlockSpec((tr, 128), lambda i: (i, 0))]


def mla_pre_fwd(proj, wq, wkv, vec, cos2, sin2, name):
    s = proj.shape[0]
    tr = _pick(s, 256)

    def body(ql_ref, ckv_ref, kr_ref, krs_ref, wq_ref, wkv_ref, v_ref, cos_ref, sin_ref, q_out, k_out, v_out):
        vshared, vhead = _mla_vec_pieces(v_ref)
        cos2_, sin2_ = cos_ref[...], sin_ref[...]
        qlat_n, kv_n, krope = _mla_shared(ql_ref[...], ckv_ref[...], kr_ref[...], krs_ref[...], *vshared, cos2_, sin2_)
        qlat_n, kv_n, krope = qlat_n.astype(BF16), kv_n.astype(BF16), krope.astype(BF16)
        for h in range(HEADS):
            ws = (wq_ref[h, :, 0:128], wq_ref[h, :, 128:256], wq_ref[h, :, 256:384],
                  wkv_ref[h, :, 0:128], wkv_ref[h, :, 128:256])
            qn, qr, kn, v = _mla_head(qlat_n, kv_n, *ws, *vhead, cos2_, sin2_)
            q_out[h, :, 0:128] = qn.astype(BF16)
            q_out[h, :, 128:256] = qr.astype(BF16)
            k_out[h, :, 0:128] = kn.astype(BF16)
            k_out[h, :, 128:256] = krope
            v_out[h] = v.astype(BF16)

    return pl.pallas_call(
        body, name=name, grid=(s // tr,), in_specs=_mla_in_specs(tr),
        out_specs=[pl.BlockSpec((HEADS, tr, 256), lambda i: (0, i, 0)), pl.BlockSpec((HEADS, tr, 256), lambda i: (0, i, 0)),
                   pl.BlockSpec((HEADS, tr, 128), lambda i: (0, i, 0))],
        out_shape=[jax.ShapeDtypeStruct((HEADS, s, 256), BF16), jax.ShapeDtypeStruct((HEADS, s, 256), BF16),
                   jax.ShapeDtypeStruct((HEADS, s, 128), BF16)],
        compiler_params=_params("arbitrary"),
    )(proj, proj, proj, proj, wq, wkv, vec, cos2, sin2)


def mla_pre_bwd(proj, wq, wkv, vec, cos2, sin2, dq, dk, dv, name):
    s = proj.shape[0]
    tr = _pick(s, 256)

    def body(ql_ref, ckv_ref, kr_ref, krs_ref, wq_ref, wkv_ref, v_ref, cos_ref, sin_ref, dq_ref, dk_ref, dv_ref,
             dql_out, dckv_out, dkr_out, dkrs_out, dwq_out, dwkv_out, dvec_out):
        @pl.when(pl.program_id(0) == 0)
        def _():
            dwq_out[...] = jnp.zeros_like(dwq_out)
            dwkv_out[...] = jnp.zeros_like(dwkv_out)
            dvec_out[...] = jnp.zeros_like(dvec_out)

        vshared, vhead = _mla_vec_pieces(v_ref)
        cos2_, sin2_ = cos_ref[...], sin_ref[...]
        fs = lambda *a: _mla_shared(*a, cos2_, sin2_)
        (qlat_n, kv_n, _), vjp_shared = jax.vjp(fs, ql_ref[...], ckv_ref[...], kr_ref[...], krs_ref[...], *vshared)

        def head(h, carry):
            wq_h, wkv_h = wq_ref[h].astype(F32), wkv_ref[h].astype(F32)
            ws = (wq_h[:, 0:128], wq_h[:, 128:256], wq_h[:, 256:384], wkv_h[:, 0:128], wkv_h[:, 128:256])
            f = lambda *a: _mla_head(*a, cos2_, sin2_)
            _, vjp = jax.vjp(f, qlat_n, kv_n, *ws, *vhead)
            dq_h, dk_h = dq_ref[h], dk_ref[h]
            g = vjp((dq_h[:, 0:128], dq_h[:, 128:256], dk_h[:, 0:128], dv_ref[h]))
            dwq_out[h, :, 0:128] += g[2]
            dwq_out[h, :, 128:256] += g[3]
            dwq_out[h, :, 256:384] += g[4]
            dwkv_out[h, :, 0:128] += g[5]
            dwkv_out[h, :, 128:256] += g[6]
            dvec_out[2:3, 0:128] += g[7]
            dvec_out[2:3, 128:256] += g[8]
            dvec_out[2:3, 256:384] += g[9]
            dvec_out[3:4, 0:128] += g[10]
            return carry[0] + g[0], carry[1] + g[1], carry[2] + dk_h[:, 128:256]

        zero = lambda w: jnp.zeros((tr, w), F32)
        dqn, dkvn, dkrope = lax.fori_loop(0, HEADS, head, (zero(384), zero(256), zero(128)))
        g = vjp_shared((dqn, dkvn, dkrope))
        dql_out[...] = g[0].astype(BF16)
        dckv_out[...] = g[1].astype(BF16)
        dkr_out[...] = g[2].astype(BF16)
        dkrs_out[...] = g[3].astype(BF16)
        dvec_out[0:1, 0:384] += g[4]
        dvec_out[1:2, 0:256] += g[5]
        dvec_out[3:4, 128:256] += g[6]
        dvec_out[3:4, 256:384] += g[7]

    hb = lambda w: pl.BlockSpec((HEADS, tr, w), lambda i: (0, i, 0))
    return pl.pallas_call(
        body, name=name, grid=(s // tr,), in_specs=_mla_in_specs(tr) + [hb(256), hb(256), hb(128)],
        out_specs=[pl.BlockSpec((tr, 384), lambda i: (i, 0)), pl.BlockSpec((tr, 256), lambda i: (i, 0)),
                   pl.BlockSpec((tr, 128), lambda i: (i, 0)), pl.BlockSpec((tr, 128), lambda i: (i, 0)),
                   pl.BlockSpec((HEADS, 384, 384), lambda i: (0, 0, 0)), pl.BlockSpec((HEADS, 256, 256), lambda i: (0, 0, 0)),
                   pl.BlockSpec((8, 512), lambda i: (0, 0))],
        out_shape=[jax.ShapeDtypeStruct((s, 384), BF16), jax.ShapeDtypeStruct((s, 256), BF16),
                   jax.ShapeDtypeStruct((s, 128), BF16), jax.ShapeDtypeStruct((s, 128), BF16),
                   jax.ShapeDtypeStruct((HEADS, 384, 384), F32), jax.ShapeDtypeStruct((HEADS, 256, 256), F32),
                   jax.ShapeDtypeStruct((8, 512), F32)],
        compiler_params=_params("arbitrary"),
    )(proj, proj, proj, proj, wq, wkv, vec, cos2, sin2, dq, dk, dv)


def _att_probs(q, kk, i, tq):
    sc = _raw(q, kk, _NT) * ATT_SCALE
    rows = lax.broadcasted_iota(jnp.int32, sc.shape, 0) + i * tq
    cols = lax.broadcasted_iota(jnp.int32, sc.shape, 1)
    sc = jnp.where(cols <= rows, sc, -jnp.inf)
    e = jnp.exp(sc - jnp.max(sc, axis=-1, keepdims=True))
    return e / jnp.sum(e, axis=-1, keepdims=True)


def mla_attn_fwd(q, k, v, name):
    s = q.shape[1]
    tq = _pick(s, 256)

    def body(q_ref, k_ref, v_ref, o_ref):
        for i in range(s // tq):
            n = (i + 1) * tq
            p = _att_probs(q_ref[i * tq:n, :], k_ref[0:n, :], i, tq)
            o_ref[i * tq:n, :] = _raw(p, v_ref[0:n, :], _NN).astype(BF16)

    hs = lambda w: pl.BlockSpec((None, s, w), lambda h: (h, 0, 0))
    return pl.pallas_call(
        body, name=name, grid=(HEADS,), in_specs=[hs(256), hs(256), hs(128)],
        out_specs=pl.BlockSpec((s, 128), lambda h: (0, h)),
        out_shape=jax.ShapeDtypeStruct((s, HEADS * 128), BF16),
        compiler_params=_params("arbitrary"),
    )(q, k, v)


def mla_attn_bwd(q, k, v, do, name):
    s = q.shape[1]
    tq = _pick(s, 256)

    def body(q_ref, k_ref, v_ref, do_ref, dq_ref, dk_ref, dv_ref):
        dk_ref[...] = jnp.zeros_like(dk_ref)
        dv_ref[...] = jnp.zeros_like(dv_ref)
        for i in range(s // tq):
            n = (i + 1) * tq
            qq, kk, vv = q_ref[i * tq:n, :], k_ref[0:n, :], v_ref[0:n, :]
            p = _att_probs(qq, kk, i, tq)
            o = _raw(p, vv, _NN)
            dout = do_ref[i * tq:n, :]
            delta = jnp.sum(dout * o, axis=-1, keepdims=True)
            dp = _raw(dout, vv, _NT)
            ds = p * (dp - delta) * ATT_SCALE
            dq_ref[i * tq:n, :] = _raw(ds, kk, _NN)
            dk_ref[0:n, :] += _raw(ds, qq, _TN)
            dv_ref[0:n, :] += _raw(p, dout, _TN)

    hs = lambda w: pl.BlockSpec((None, s, w), lambda h: (h, 0, 0))
    return pl.pallas_call(
        body, name=name, grid=(HEADS,),
        in_specs=[hs(256), hs(256), hs(128), pl.BlockSpec((s, 128), lambda h: (0, h))],
        out_specs=[hs(256), hs(256), hs(128)],
        out_shape=[jax.ShapeDtypeStruct((HEADS, s, 256), F32), jax.ShapeDtypeStruct((HEADS, s, 256), F32),
                   jax.ShapeDtypeStruct((HEADS, s, 128), F32)],
        compiler_params=_params("arbitrary"),
    )(q, k, v, do)


def _pool_windows(u, pad, s, g):
    pad[0:16, :] = jnp.zeros((16, 128), F32)
    cur, sel = u, None
    for j, k in enumerate((1, 2, 4, 8)):
        pad[16:16 + s, :] = cur
        cur = cur + pad[16 - k:16 - k + s, :]
        sel = cur if sel is None else jnp.where(g == j, cur, sel)
    return sel


def _pool_count(s, g):
    t = lax.broadcasted_iota(jnp.int32, (s, 1), 0)
    return jnp.minimum(t + 1, 2 << g).astype(F32)


def pool_fwd(proj, pw, ps, name):
    s = proj.shape[0]

    def body(u_ref, w_ref, s_ref, o_ref, pad):
        g = pl.program_id(0)
        u = u_ref[...]
        pooled = _pool_windows(u, pad, s, g) / _pool_count(s, g) - u
        o_ref[...] = (_raw(pooled, w_ref[...], _NN) * s_ref[...]).astype(BF16)

    return pl.pallas_call(
        body, name=name, grid=(4,),
        in_specs=[pl.BlockSpec((s, 128), lambda g: (0, O_PU // 128 + g)), pl.BlockSpec((None, 128, 128), lambda g: (g, 0, 0)),
                  pl.BlockSpec((1, 128), lambda g: (0, g))],
        out_specs=pl.BlockSpec((s, 128), lambda g: (0, g)),
        out_shape=jax.ShapeDtypeStruct((s, 512), BF16),
        scratch_shapes=[pltpu.VMEM((s + 16, 128), F32)],
        compiler_params=_params("arbitrary"),
    )(proj, pw, ps)


def pool_bwd(proj, pw, ps, do, name):
    s = proj.shape[0]

    def body(u_ref, w_ref, s_ref, do_ref, du_ref, dw_ref, ds_ref, pad):
        g = pl.program_id(0)
        u, w, dout = u_ref[...], w_ref[...], do_ref[...]
        cnt = _pool_count(s, g)
        pooled = _pool_windows(u, pad, s, g) / cnt - u
        mixed = _raw(pooled, w, _NN)
        ds_ref[...] = jnp.sum(dout * mixed, axis=0, keepdims=True)
        dmixed = dout * s_ref[...]
        dw_ref[...] = _raw(pooled, dmixed, _TN)
        dpooled = _raw(dmixed, w, _NT)
        dsel = dpooled / cnt
        pad[s:s + 16, :] = jnp.zeros((16, 128), F32)
        cur = jnp.where(g == 3, dsel, 0.0)
        for j, k in ((2, 8), (1, 4), (0, 2)):
            pad[0:s, :] = cur
            cur = cur + pad[k:k + s, :] + jnp.where(g == j, dsel, 0.0)
        pad[0:s, :] = cur
        cur = cur + pad[1:1 + s, :]
        du_ref[...] = (cur - dpooled).astype(BF16)

    return pl.pallas_call(
        body, name=name, grid=(4,),
        in_specs=[pl.BlockSpec((s, 128), lambda g: (0, O_PU // 128 + g)), pl.BlockSpec((None, 128, 128), lambda g: (g, 0, 0)),
                  pl.BlockSpec((1, 128), lambda g: (0, g)), pl.BlockSpec((s, 128), lambda g: (0, g))],
        out_specs=[pl.BlockSpec((s, 128), lambda g: (0, g)), pl.BlockSpec((None, 128, 128), lambda g: (g, 0, 0)),
                   pl.BlockSpec((1, 128), lambda g: (0, g))],
        out_shape=[jax.ShapeDtypeStruct((s, 512), BF16), jax.ShapeDtypeStruct((4, 128, 128), F32),
                   jax.ShapeDtypeStruct((1, 512), F32)],
        scratch_shapes=[pltpu.VMEM((s + 16, 128), F32)],
        compiler_params=_params("arbitrary"),
    )(proj, pw, ps, do)


def _xbc_col(i):
    return jnp.where(i < 2, O_XS // 512 + i, O_BC // 512)


def conv_fwd(proj, cw, cb, name):
    s = proj.shape[0]

    def body(x_ref, w_ref, b_ref, o_ref, t_ref, pad):
        pad[0:8, :] = jnp.zeros((8, 512), F32)
        pad[8:8 + s, :] = x_ref[...]
        y = b_ref[...] + sum(w_ref[k:k + 1, :] * pad[5 + k:5 + k + s, :] for k in range(4))
        act = y * _sig(y)
        o_ref[...] = act

        @pl.when(pl.program_id(0) < 2)
        def _():
            t_ref[...] = act.T

    return pl.pallas_call(
        body, name=name, grid=(3,),
        in_specs=[pl.BlockSpec((s, 512), lambda i: (0, _xbc_col(i))), pl.BlockSpec((4, 512), lambda i: (0, i)),
                  pl.BlockSpec((1, 512), lambda i: (0, i))],
        out_specs=[pl.BlockSpec((s, 512), lambda i: (0, i)), pl.BlockSpec((512, s), lambda i: (jnp.minimum(i, 1), 0))],
        out_shape=[jax.ShapeDtypeStruct((s, 1536), F32), jax.ShapeDtypeStruct((D, s), F32)],
        scratch_shapes=[pltpu.VMEM((s + 8, 512), F32)],
        compiler_params=_params("arbitrary"),
    )(proj, cw, cb)


def conv_bwd(proj, cw, cb, dxt, dbm, dcm, name):
    s = proj.shape[0]

    def body(x_ref, w_ref, b_ref, dxt_ref, dbm_ref, dcm_ref, dx_ref, dw_ref, db_ref, pad, pad2):
        pad[0:8, :] = jnp.zeros((8, 512), F32)
        pad[8:8 + s, :] = x_ref[...]
        y = b_ref[...] + sum(w_ref[k:k + 1, :] * pad[5 + k:5 + k + s, :] for k in range(4))
        sg = _sig(y)

        @pl.when(pl.program_id(0) < 2)
        def _():
            pad2[0:s, :] = dxt_ref[...].T

        @pl.when(pl.program_id(0) == 2)
        def _():
            pad2[0:s, 0:256] = dbm_ref[...]
            pad2[0:s, 256:512] = dcm_ref[...]

        dy = pad2[0:s, :] * (sg * (1.0 + y * (1.0 - sg)))
        db_ref[...] = jnp.sum(dy, axis=0, keepdims=True)
        for k in range(4):
            dw_ref[k:k + 1, :] = jnp.sum(dy * pad[5 + k:5 + k + s, :], axis=0, keepdims=True)
        pad2[s:s + 8, :] = jnp.zeros((8, 512), F32)
        pad2[0:s, :] = dy
        dx_ref[...] = sum(w_ref[k:k + 1, :] * pad2[3 - k:3 - k + s, :] for k in range(4)).astype(BF16)

    return pl.pallas_call(
        body, name=name, grid=(3,),
        in_specs=[pl.BlockSpec((s, 512), lambda i: (0, _xbc_col(i))), pl.BlockSpec((4, 512), lambda i: (0, i)),
                  pl.BlockSpec((1, 512), lambda i: (0, i)), pl.BlockSpec((512, s), lambda i: (jnp.minimum(i, 1), 0)),
                  pl.BlockSpec((s, 256), lambda i: (0, 0)), pl.BlockSpec((s, 256), lambda i: (0, 0))],
        out_specs=[pl.BlockSpec((s, 512), lambda i: (0, i)), pl.BlockSpec((4, 512), lambda i: (0, i)),
                   pl.BlockSpec((1, 512), lambda i: (0, i))],
        out_shape=[jax.ShapeDtypeStruct((s, 1536), BF16), jax.ShapeDtypeStruct((4, 1536), F32),
                   jax.ShapeDtypeStruct((1, 1536), F32)],
        scratch_shapes=[pltpu.VMEM((s + 8, 512), F32), pltpu.VMEM((s + 8, 512), F32)],
        compiler_params=_params("arbitrary"),
    )(proj, cw, cb, dxt, dbm, dcm)


def _ssd_chunk(xt, dtr, bm, cm, hprev, alog, dbias, dskip):
    ln = 128
    a = -jnp.exp(alog)
    dt_r = softplus(dtr + dbias)
    da_r = dt_r * a
    li = lax.broadcasted_iota(jnp.int32, (1, ln, ln), 1)
    si = lax.broadcasted_iota(jnp.int32, (1, ln, ln), 2)
    causal = si <= li
    acs_c = jnp.sum(jnp.where(causal, da_r, 0.0), axis=2, keepdims=True)
    acs_r = jnp.sum(jnp.where(li == si, acs_c, 0.0), axis=1, keepdims=True)
    acs_last = jnp.sum(da_r, axis=2, keepdims=True)
    decay = jnp.exp(jnp.where(causal, acs_c - acs_r, -jnp.inf))
    m = mm_nt(cm, bm)[None] * decay
    xdt = xt * dt_r
    y_diag = bmm_nt(xdt, m)
    bb = jnp.broadcast_to(bm[None], (8, ln, ln))
    cc = jnp.broadcast_to(cm[None], (8, ln, ln))
    states = bmm_nn(xdt * jnp.exp(acs_last - acs_r), bb)
    y_off = bmm_nt(hprev, cc) * jnp.exp(acs_r)
    hnew = hprev * jnp.exp(acs_last) + states
    return y_diag + y_off + xt * dskip, hnew


def _ssd_specs(nc, rev):
    cix = (lambda c: nc - 1 - c) if rev else (lambda c: c)
    hv = pl.BlockSpec((8, 1, 1), lambda g, c: (g, 0, 0))
    return [pl.BlockSpec((8, 64, 128), lambda g, c: (g, 0, cix(c))), pl.BlockSpec((8, 1, 128), lambda g, c: (g, 0, cix(c))),
            pl.BlockSpec((128, 128), lambda g, c: (cix(c), 8 + g)),
            pl.BlockSpec((128, 128), lambda g, c: (cix(c), 10 + g))], hv, cix


def ssd_fwd(xt, dtr, xbc, alog, dbias, dskip, name):
    s = xt.shape[2]
    nc = s // 128
    specs, hv, _ = _ssd_specs(nc, False)

    def body(x_ref, dr_ref, b_ref, c_ref, al_ref, db_ref, dk_ref, y_ref, hs_ref, h_scr):
        @pl.when(pl.program_id(1) == 0)
        def _():
            h_scr[...] = jnp.zeros_like(h_scr)
        hp = h_scr[...]
        hs_ref[...] = hp
        y, hn = _ssd_chunk(x_ref[...], dr_ref[...], b_ref[...], c_ref[...], hp, al_ref[...], db_ref[...], dk_ref[...])
        y_ref[...] = y
        h_scr[...] = hn

    return pl.pallas_call(
        body, name=name, grid=(2, nc), in_specs=specs + [hv, hv, hv],
        out_specs=[pl.BlockSpec((8, 64, 128), lambda g, c: (g, 0, c)),
                   pl.BlockSpec((None, None, 8, 64, 128), lambda g, c: (g, c, 0, 0, 0))],
        out_shape=[jax.ShapeDtypeStruct((16, 64, s), F32), jax.ShapeDtypeStruct((2, nc, 8, 64, 128), F32)],
        scratch_shapes=[pltpu.VMEM((8, 64, 128), F32)],
        compiler_params=_params("arbitrary", "arbitrary"),
    )(xt, dtr, xbc, xbc, alog, dbias, dskip)


def ssd_bwd(xt, dtr, xbc, alog, dbias, dskip, hs, dyt, name):
    s = xt.shape[2]
    nc = s // 128
    specs, hv, cix = _ssd_specs(nc, True)

    def body(x_ref, dr_ref, b_ref, c_ref, al_ref, db_ref, dk_ref, hs_ref, dy_ref,
             dx_out, ddr_out, dbm_out, dcm_out, dal_out, ddb_out, ddk_out, dh_scr):
        @pl.when(pl.program_id(1) == 0)
        def _():
            dh_scr[...] = jnp.zeros_like(dh_scr)
            dal_out[...] = jnp.zeros_like(dal_out)
            ddb_out[...] = jnp.zeros_like(ddb_out)
            ddk_out[...] = jnp.zeros_like(ddk_out)
        _, vjp = jax.vjp(_ssd_chunk, x_ref[...], dr_ref[...], b_ref[...], c_ref[...], hs_ref[...],
                         al_ref[...], db_ref[...], dk_ref[...])
        g = vjp((dy_ref[...], dh_scr[...]))
        dx_out[...] = g[0]
        ddr_out[...] = g[1]
        dbm_out[...] = g[2]
        dcm_out[...] = g[3]
        dh_scr[...] = g[4]
        dal_out[...] += g[5]
        ddb_out[...] += g[6]
        ddk_out[...] += g[7]

    return pl.pallas_call(
        body, name=name, grid=(2, nc),
        in_specs=specs + [hv, hv, hv, pl.BlockSpec((None, None, 8, 64, 128), lambda g, c: (g, cix(c), 0, 0, 0)),
                          pl.BlockSpec((8, 64, 128), lambda g, c: (g, 0, cix(c)))],
        out_specs=[pl.BlockSpec((8, 64, 128), lambda g, c: (g, 0, cix(c))), pl.BlockSpec((8, 1, 128), lambda g, c: (g, 0, cix(c))),
                   pl.BlockSpec((128, 128), lambda g, c: (cix(c), g)),
                   pl.BlockSpec((128, 128), lambda g, c: (cix(c), g)), hv, hv, hv],
        out_shape=[jax.ShapeDtypeStruct((16, 64, s), F32), jax.ShapeDtypeStruct((16, 1, s), F32),
                   jax.ShapeDtypeStruct((s, 256), F32),
                   jax.ShapeDtypeStruct((s, 256), F32)] + [jax.ShapeDtypeStruct((16, 1, 1), F32)] * 3,
        scratch_shapes=[pltpu.VMEM((8, 64, 128), F32)],
        compiler_params=_params("arbitrary", "arbitrary"),
    )(xt, dtr, xbc, xbc, alog, dbias, dskip, hs, dyt)


def _merge(oa, ob, y, z, gla, glb, glc, x, g1, nw, ea, eb, ec, eo, wba, wbb, wbc, wout):
    gated = y * (z * _sig(z))
    sq = gated * gated
    left = lax.broadcasted_iota(jnp.int32, (1, D), 1) < 512
    ms0 = jnp.sum(jnp.where(left, sq, 0.0), axis=-1, keepdims=True) / 512.0
    ms1 = jnp.sum(jnp.where(left, 0.0, sq), axis=-1, keepdims=True) / 512.0
    oc = gated * jnp.where(left, lax.rsqrt(ms0 + EPS), lax.rsqrt(ms1 + EPS)) * nw
    ya, yb, yc = mm_nc(oa, wba) + ea, mm_nc(ob, wbb) + eb, mm_nc(oc, wbc) + ec
    merged = _sig(gla) * ya + _sig(glb) * yb + _sig(glc) * yc
    x1 = x + g1 * (mm_nc(merged, wout) + eo)
    return x1, (oc, merged)


def _merge_specs(tr):
    row = lambda w: pl.BlockSpec((tr, w), lambda i: (i, 0))
    acts = [row(D), row(512), pl.BlockSpec((D, tr), lambda i: (0, i)), pl.BlockSpec((tr, D), lambda i: (i, O_Z // D)),
            pl.BlockSpec((tr, 3 * D), lambda i: (i, 0)), row(D), pl.BlockSpec((8, D), lambda i: (0, 0))]
    cst = lambda r: pl.BlockSpec((r, D), lambda i: (0, 0), **CONST)
    return acts, [cst(D), cst(512), cst(D), cst(D)], row


def merge_fwd(oa, ob, y, proj, x, mvec, wba, wbb, wbc, wout, name):
    s = x.shape[0]
    tr = _pick(s, 256)
    acts, wts, row = _merge_specs(tr)

    def body(oa_ref, ob_ref, y_ref, z_ref, gl_ref, x_ref, mv_ref, wba_ref, wbb_ref, wbc_ref, wout_ref, o_ref):
        zero = jnp.zeros((1, D), F32)
        x1, _ = _merge(oa_ref[...].astype(F32), ob_ref[...].astype(F32), y_ref[...].T, z_ref[...], gl_ref[:, 0:D], gl_ref[:, D:2 * D],
                       gl_ref[:, 2 * D:3 * D], x_ref[...], mv_ref[0:1, :], mv_ref[1:2, :], zero, zero, zero, zero,
                       wba_ref[...], wbb_ref[...], wbc_ref[...], wout_ref[...])
        o_ref[...] = x1

    return pl.pallas_call(
        body, name=name, grid=(s // tr,), in_specs=acts + wts, out_specs=row(D),
        out_shape=jax.ShapeDtypeStruct((s, D), F32), compiler_params=_params("arbitrary"),
    )(oa, ob, y, proj, proj, x, mvec, wba, wbb, wbc, wout)


def merge_bwd(oa, ob, y, proj, x, mvec, wba, wbb, wbc, wout, dx1, name):
    s = x.shape[0]
    tr = _pick(s, 128)
    acts, wts, row = _merge_specs(tr)

    def body(oa_ref, ob_ref, y_ref, z_ref, gl_ref, x_ref, mv_ref, wba_ref, wbb_ref, wbc_ref, wout_ref, dx1_ref,
             doa_o, dob_o, dy_o, dz_o, dgl_o, dx_o, dmv_o, dya_o, dyb_o, dyc_o, dpre_o, oc_o, mg_o):
        zero = jnp.zeros((tr, D), F32)
        wts_ = (wba_ref[...], wbb_ref[...], wbc_ref[...], wout_ref[...])
        f = lambda *a: _merge(*a, *wts_)
        _, vjp, (oc, merged) = jax.vjp(
            f, oa_ref[...].astype(F32), ob_ref[...].astype(F32), y_ref[...].T, z_ref[...], gl_ref[:, 0:D], gl_ref[:, D:2 * D],
            gl_ref[:, 2 * D:3 * D], x_ref[...], mv_ref[0:1, :], mv_ref[1:2, :], zero, zero, zero, zero, has_aux=True)
        g = vjp(dx1_ref[...])
        doa_o[...] = g[0]
        dob_o[...] = g[1]
        dy_o[...] = g[2].T
        dz_o[...] = g[3].astype(BF16)
        dgl_o[:, 0:D] = g[4].astype(BF16)
        dgl_o[:, D:2 * D] = g[5].astype(BF16)
        dgl_o[:, 2 * D:3 * D] = g[6].astype(BF16)
        dx_o[...] = g[7]

        @pl.when(pl.program_id(0) == 0)
        def _():
            dmv_o[...] = jnp.zeros_like(dmv_o)

        dmv_o[0:1, :] += g[8]
        dmv_o[1:2, :] += g[9]
        dya_o[...] = g[10].astype(BF16)
        dyb_o[...] = g[11].astype(BF16)
        dyc_o[...] = g[12].astype(BF16)
        dpre_o[...] = g[13].astype(BF16)
        oc_o[...] = oc.astype(BF16)
        mg_o[...] = merged.astype(BF16)

    sd = lambda w, dt: jax.ShapeDtypeStruct((s, w), dt)
    return pl.pallas_call(
        body, name=name, grid=(s // tr,), in_specs=acts + wts + [row(D)],
        out_specs=[row(D), row(512), pl.BlockSpec((D, tr), lambda i: (0, i)), row(D), row(3 * D), row(D),
                   pl.BlockSpec((8, D), lambda i: (0, 0))] + [row(D)] * 6,
        out_shape=[sd(D, F32), sd(512, F32), jax.ShapeDtypeStruct((D, s), F32), sd(D, BF16), sd(3 * D, BF16), sd(D, F32),
                   jax.ShapeDtypeStruct((8, D), F32)] + [sd(D, BF16)] * 6,
        compiler_params=_params("arbitrary"),
    )(oa, ob, y, proj, proj, x, mvec, wba, wbb, wbc, wout, dx1)


def _conv3(u_scr, w_ref, first, rows, lanes):
    return sum(w_ref[k:k + 1, :] * u_scr[first + k:first + k + rows, lanes] for k in range(3))


def _ffn_tile_specs(tf, tile):
    def at(rows, off):
        return pl.BlockSpec((rows, tf), lambda *g: (0, off + tile(*g)))

    def wt(off):
        return pl.BlockSpec((tf, D), lambda *g: (off + tile(*g), 0))
    return [wt(0), wt(FFN_NT), at(3, 0), at(3, FFN_NT), at(1, 0), at(1, FFN_NT)]


def ffn_fwd(x1, fvec, wup, cw, cb, wdn, name):
    s = x1.shape[0]
    tr, tf = _pick(s, 512), FFN_TILE
    lg, lv = slice(0, tf), slice(tf, 2 * tf)

    def body(x_ref, v_ref, wg_ref, wv_ref, cwg_ref, cwv_ref, cbg_ref, cbv_ref, wd_ref, x2_ref, h_ref, pre_ref,
             h_scr, u_scr, acc):
        i, t = pl.program_id(0), pl.program_id(1)

        @pl.when(t == 0)
        def _():
            @pl.when(i == 0)
            def _():
                h_scr[0:16, :] = jnp.zeros((16, D), BF16)

            @pl.when(i > 0)
            def _():
                h_scr[0:16, :] = h_scr[tr:tr + 16, :]

            h = (_rms(x_ref[...], v_ref[0:1, :], D) * (1.0 + v_ref[2:3, :]) + v_ref[1:2, :]).astype(BF16)
            h_scr[16:16 + tr, :] = h
            h_ref[...] = h
            acc[...] = jnp.zeros_like(acc)

        u_scr[:, lg] = _raw(h_scr[...], wg_ref[...], _NT)
        u_scr[:, lv] = _raw(h_scr[...], wv_ref[...], _NT)
        cg = _conv3(u_scr, cwg_ref, 14, tr, lg) + cbg_ref[...]
        cval = _conv3(u_scr, cwv_ref, 14, tr, lv) + cbv_ref[...]
        acc[...] += _raw(cg * _sig(cg) * cval, wd_ref[...], _NN)

        @pl.when(t == FFN_NT - 1)
        def _():
            pre_ref[...] = acc[...]
            x2_ref[...] = x_ref[...] + v_ref[3:4, :] * acc[...]

    row = pl.BlockSpec((tr, D), lambda i, t: (i, 0))
    return pl.pallas_call(
        body, name=name, grid=(s // tr, FFN_NT),
        in_specs=[row, pl.BlockSpec((8, D), lambda i, t: (0, 0))] + _ffn_tile_specs(tf, lambda i, t: t)
                 + [pl.BlockSpec((tf, D), lambda i, t: (t, 0))],
        out_specs=[row, row, row],
        out_shape=[jax.ShapeDtypeStruct((s, D), F32), jax.ShapeDtypeStruct((s, D), BF16), jax.ShapeDtypeStruct((s, D), F32)],
        scratch_shapes=[pltpu.VMEM((tr + 16, D), BF16), pltpu.VMEM((tr + 16, 2 * tf), F32), pltpu.VMEM((tr, D), F32)],
        compiler_params=_params("arbitrary", "arbitrary"),
    )(x1, fvec, wup, wup, cw, cw, cb, cb, wdn)


def ffn_bwd(h2, dx2, fvec, wup, cw, cb, wdn, name):
    s = h2.shape[0]
    tr, tf = _pick(s, 512), FFN_TILE
    ni, nb = s // tr, s // 16
    lg, lv = slice(0, tf), slice(tf, 2 * tf)

    def body(hp_ref, hm_ref, hn_ref, dm_ref, dn_ref, v_ref, wg_ref, wv_ref, cwg_ref, cwv_ref, cbg_ref, cbv_ref, wd_ref,
             dup_ref, act_ref, dcw_ref, u_scr, dc_scr):
        i = pl.program_id(1)
        hfull = jnp.concatenate([jnp.where(i > 0, hp_ref[...], jnp.zeros((16, D), BF16)), hm_ref[...],
                                 jnp.where(i < ni - 1, hn_ref[...], jnp.zeros((16, D), BF16))], axis=0)
        u_scr[:, lg] = _raw(hfull, wg_ref[...], _NT)
        u_scr[:, lv] = _raw(hfull, wv_ref[...], _NT)
        cg = _conv3(u_scr, cwg_ref, 14, tr + 16, lg) + cbg_ref[...]
        cval = _conv3(u_scr, cwv_ref, 14, tr + 16, lv) + cbv_ref[...]
        g2 = v_ref[3:4, :]
        dpre = jnp.concatenate([dm_ref[...] * g2, jnp.where(i < ni - 1, dn_ref[...], 0.0) * g2], axis=0)
        dact = _raw(dpre, wd_ref[...], _NT)
        sg = _sig(cg)
        sl = cg * sg
        dc_scr[:, lg] = dact * cval * (sg * (1.0 + cg * (1.0 - sg)))
        dc_scr[:, lv] = dact * sl
        act_ref[...] = (sl * cval)[0:tr, :].astype(BF16)

        @pl.when(i == 0)
        def _():
            dcw_ref[...] = jnp.zeros_like(dcw_ref)

        for half, lanes, cw_ref in ((0, lg, cwg_ref), (1, lv, cwv_ref)):
            dup_ref[half] = sum(cw_ref[k:k + 1, :] * dc_scr[2 - k:2 - k + tr, lanes] for k in range(3)).astype(BF16)
            dcm = dc_scr[0:tr, lanes]
            for k in range(3):
                dcw_ref[half, k:k + 1, :] += jnp.sum(dcm * u_scr[14 + k:14 + k + tr, lanes], axis=0, keepdims=True)
            dcw_ref[half, 3:4, :] += jnp.sum(dcm, axis=0, keepdims=True)

    r16 = tr // 16
    prev = lambda t, i: (jnp.maximum(i * r16 - 1, 0), 0)
    nxt = lambda t, i: (jnp.minimum((i + 1) * r16, nb - 1), 0)
    main = lambda t, i: (i, 0)
    return pl.pallas_call(
        body, name=name, grid=(FFN_NT, ni),
        in_specs=[pl.BlockSpec((16, D), prev), pl.BlockSpec((tr, D), main), pl.BlockSpec((16, D), nxt),
                  pl.BlockSpec((tr, D), main), pl.BlockSpec((16, D), nxt), pl.BlockSpec((8, D), lambda t, i: (0, 0))]
                 + _ffn_tile_specs(tf, lambda t, i: t) + [pl.BlockSpec((tf, D), lambda t, i: (t, 0))],
        out_specs=[pl.BlockSpec((2, tr, tf), lambda t, i: (0, i, t)), pl.BlockSpec((tr, tf), lambda t, i: (i, t)),
                   pl.BlockSpec((2, 8, tf), lambda t, i: (0, 0, t))],
        out_shape=[jax.ShapeDtypeStruct((2, s, FFN), BF16), jax.ShapeDtypeStruct((s, FFN), BF16),
                   jax.ShapeDtypeStruct((2, 8, FFN), F32)],
        scratch_shapes=[pltpu.VMEM((tr + 32, 2 * tf), F32), pltpu.VMEM((tr + 16, 2 * tf), F32)],
        compiler_params=_params("arbitrary", "arbitrary"),
    )(h2, h2, h2, dx2, dx2, fvec, wup, wup, cw, cw, cb, cb, wdn)


def loss_head(y, target):
    s = y.shape[0]
    tr = _pick(s, 512)

    def body(y_ref, t_ref, dx_ref, l_ref):
        @pl.when(pl.program_id(0) == 0)
        def _():
            l_ref[...] = jnp.zeros_like(l_ref)
        err = y_ref[...] - t_ref[...]
        dx_ref[...] = err / float(D)
        l_ref[...] += 0.5 * jnp.sum(jnp.sum(err * err, axis=-1, keepdims=True) / float(D), axis=0, keepdims=True)

    row = pl.BlockSpec((tr, D), lambda i: (i, 0))
    return pl.pallas_call(
        body, name="loss_head", grid=(s // tr,), in_specs=[row, row],
        out_specs=[row, pl.BlockSpec((8, 128), lambda i: (0, 0))],
        out_shape=[jax.ShapeDtypeStruct((s, D), F32), jax.ShapeDtypeStruct((8, 128), F32)],
        compiler_params=_params("arbitrary"),
    )(y, target)


def adamw(parts, w, m, v, name, tok=None):
    nseg = len(parts)
    p, r, c = parts[0].shape
    tok = jnp.zeros((8, 128), F32) if tok is None else tok
    cap = 256 if c > 128 else 2048
    step = lambda q, l, i, ni: jnp.clip((l - q) * ni + i, 0, ni - 1)
    if r <= cap or any(r % t == 0 for t in range(8, cap + 1, 8)):
        tr, tc = _pick(r, cap, 8), c
        ni = r // tr
        row = pl.BlockSpec((None, tr, tc), lambda l, i: (l, i, 0))
        part = lambda q: pl.BlockSpec((p, tr, tc), lambda l, i: (0, step(q, l, i, ni), 0))
    else:
        tr, tc = r, _pick(c, 256)
        ni = c // tc
        row = pl.BlockSpec((None, tr, tc), lambda l, i: (l, 0, i))
        part = lambda q: pl.BlockSpec((p, tr, tc), lambda l, i: (0, 0, step(q, l, i, ni)))

    def body(*refs):
        p_refs = refs[:nseg]
        w_ref, m_ref, v_ref, _, g_out, d_out, m_out, v_out, g_scr = refs[nseg:]
        for q in range(nseg):
            @pl.when(pl.program_id(0) == q)
            def _(q=q):
                g = p_refs[q][0].astype(F32)
                for j in range(1, p):
                    g = g + p_refs[q][j].astype(F32)
                g_scr[...] = g
        g = g_scr[...]
        mn = B1 * m_ref[...] + (1.0 - B1) * g
        vn = B2 * v_ref[...] + (1.0 - B2) * (g * g)
        m_hat = mn / (1.0 - B1 ** STEP)
        v_hat = vn / (1.0 - B2 ** STEP)
        g_out[...] = g
        d_out[...] = -LR * (m_hat / (jnp.sqrt(v_hat) + ADAM_EPS) + WD * w_ref[...])
        m_out[...] = mn
        v_out[...] = vn

    return pl.pallas_call(
        body, name=name, grid=(nseg, ni),
        in_specs=[part(q) for q in range(nseg)] + [row, row, row, pl.BlockSpec((8, 128), lambda l, i: (0, 0))],
        out_specs=[row] * 4, out_shape=[jax.ShapeDtypeStruct((nseg, r, c), F32)] * 4,
        scratch_shapes=[pltpu.VMEM((tr, tc), F32)],
        compiler_params=_params("arbitrary", "arbitrary"),
    )(*parts, w, m, v, tok)


def _padc(a, n):
    return jnp.pad(a, [(0, 0)] * (a.ndim - 1) + [(0, n - a.shape[-1])])


def _swap16(a):
    return jnp.concatenate([a[..., 16:32], a[..., 0:16]], axis=-1)


def _shard_cols(g8, a, b):
    c = g8.shape[2]
    return [g8[j][:, max(a, j * c) - j * c:min(b, (j + 1) * c) - j * c] for j in range(a // c, (b - 1) // c + 1)]


def _padr(a, n):
    return jnp.pad(a, ((0, n - a.shape[0]), (0, 0)))


def _swap16r(a):
    return jnp.concatenate([a[16:32], a[0:16]], axis=0)


def _win_layout(g8):
    w = g8.reshape(NDEV * g8.shape[1], g8.shape[2])
    kr = w[640:672]
    return jnp.concatenate([w[3760:6832], w[2208:3232], w[1184:2208], w[672:1184], w[3232:3744], w[384:640],
                            _padr(kr, 128), _padr(_swap16r(kr), 128), _padr(w[3744:3760], 128),
                            jnp.zeros((128, w.shape[1]), w.dtype), w[0:384]], axis=0)


def _win_grad_shards(g):
    kr = (g[O_KR:O_KR + 32].astype(F32) + _swap16r(g[O_KRS:O_KRS + 32].astype(F32))).astype(g.dtype)
    segs = [(g, O_QL, 384), (g, O_CKV, 256), (kr, 0, 32), (g, O_PU, 512), (g, O_Z, D), (g, O_XS, D), (g, O_BC, 512),
            (g, O_DT, 16), (g, O_G, 3 * D)]
    shards, height = [], sum(w for _, _, w in segs) // NDEV
    for j in range(NDEV):
        a, b, off, pieces = height * j, height * (j + 1), 0, []
        for arr, lo, w in segs:
            s0, s1 = max(a, off), min(b, off + w)
            if s0 < s1:
                pieces.append(arr[lo + s0 - off:lo + s1 - off])
            off += w
        shards.append(jnp.concatenate(pieces, axis=0))
    return jnp.stack(shards).astype(BF16)


def _wq_layout(w):
    w = w.reshape(384, HEADS, 96).transpose(1, 0, 2)
    rope = w[:, :, 64:96]
    return jnp.concatenate([_padc(w[:, :, 0:64], 128), _padc(rope, 128), _padc(_swap16(rope), 128)], axis=2)


def _wq_unlayout(g):
    rope = g[:, :, 128:160] + _swap16(g[:, :, 256:288])
    return jnp.concatenate([g[:, :, 0:64], rope], axis=2).transpose(1, 0, 2).reshape(384, HEADS * 96)


def _wkv_layout(w):
    w = w.reshape(256, HEADS, 128).transpose(1, 0, 2)
    return jnp.concatenate([_padc(w[:, :, 0:64], 128), _padc(w[:, :, 64:128], 128)], axis=2)


def _wkv_unlayout(g):
    return jnp.concatenate([g[:, :, 0:64], g[:, :, 128:192]], axis=2).transpose(1, 0, 2).reshape(256, HEADS * 128)


def _wba_layout(w):
    return jnp.pad(w.reshape(HEADS, 64, D), ((0, 0), (0, 64), (0, 0))).reshape(HEADS * 128, D)


def _rows8(rows, width):
    out = jnp.stack([_padc(r.astype(F32), width) for r in rows])
    return jnp.pad(out, ((0, 8 - out.shape[0]), (0, 0)))


def _mla_vec(qa, kva, qn, kn):
    def row(n):
        return jnp.concatenate([_padc(n[0:64], 128), _padc(n[64:96], 128), _padc(_swap16(n[64:96]), 128)])
    return _rows8([qa, kva, row(qn), row(kn)], 512)


def _mla_unvec(g):
    def un(r):
        return jnp.concatenate([r[0:64], r[128:160] + _swap16(r[256:288])])
    return g[0, 0:384], g[1, 0:256], un(g[2]), un(g[3])


SMALL = (("ada_b", (6 * D,)), ("norm1_w", (D,)), ("q_a_norm", (384,)), ("kv_a_norm", (256,)), ("q_norm", (96,)),
         ("k_norm", (96,)), ("pool_w", (4, 128, 128)), ("pool_scale", (512,)), ("ssd_conv_b", (1536,)),
         ("ssd_dt_bias", (16,)), ("ssd_a_log", (16,)), ("ssd_d", (16,)), ("ssd_norm_w", (D,)), ("norm2_w", (D,)),
         ("ffn_conv_b", (2 * FFN,)), ("ssd_conv_w", (4, 1536)), ("ffn_conv_w", (3, 2 * FFN)))
SHARDED_SMALL = {"ssd_conv_w": 192, "ffn_conv_w": 704}


def _pack_rows(shp):
    return -(-math.prod(shp) // 1024) * 8


def _pack(small):
    pieces = []
    for n, shp in SMALL:
        pieces.append(small[n].reshape(-1).astype(F32))
        fill = _pack_rows(shp) * 128 - math.prod(shp)
        if fill:
            pieces.append(jnp.zeros((fill,), F32))
    return jnp.concatenate(pieces).reshape(-1, 128)


def _unpack_parts(packs):
    out, off = {}, 0
    for n, shp in SMALL:
        rows, size = _pack_rows(shp), math.prod(shp)
        r, c = math.prod(shp[:-1]), shp[-1]
        per_layer = [pk[:, off:off + rows].reshape(NDEV, rows * 128)[:, 0:size].reshape(NDEV, r, c) for pk in packs]
        out[n] = jnp.concatenate(per_layer, axis=1)
        off += rows
    return out


GROUP_A = ("w_in", "w_q_b", "w_kv_b")
GROUP_B = ("w_branch", "w_out", "ffn_up", "ffn_down")
BIG = GROUP_A + GROUP_B
SCATTER_FFN, SCATTER_MERGE = ("ffn_up", "ffn_down"), ("w_branch", "w_out")
COL_SHARDED = ("w_q_b", "w_kv_b")
TRANSPOSED = ("w_in", "ffn_up")


def _behind(arrs, tok):
    arrs = list(arrs)
    j = min(range(len(arrs)), key=lambda q: arrs[q].size)
    arrs[j] = arrs[j] + tok[0, 0].astype(arrs[j].dtype)
    return arrs


def _gathered_full(g, name):
    if name in COL_SHARDED:
        return g.transpose(1, 0, 2).reshape(g.shape[1], NDEV * g.shape[2])
    return g.reshape(NDEV * g.shape[1], g.shape[2])


def _to_shards(full, name):
    if name == "w_in":
        return _win_grad_shards(full)
    if name in COL_SHARDED:
        r, c = full.shape
        return full.reshape(r, NDEV, c // NDEV).transpose(1, 0, 2).astype(BF16)
    r, c = full.shape
    return full.reshape(NDEV, r // NDEV, c).astype(BF16)


def _fwd_a(x, lw, mod, cos2, sin2, l, tok):
    sh1, sc1, g1, sh2, sc2, g2 = [mod[j * D:(j + 1) * D] for j in range(6)]
    vec1 = _rows8([lw["norm1_w"], sh1, sc1], D) + tok[0, 0]
    proj, h1, dt_cols = norm_proj_fwd(x, vec1, lw["win"], f"inproj_fwd{l}")
    q, k, v = mla_pre_fwd(proj, lw["wq"], lw["wkv"], lw["mla_vec"], cos2, sin2, f"mla_pre_fwd{l}")
    oa = mla_attn_fwd(q, k, v, f"mla_attn_fwd{l}")
    ob = pool_fwd(proj, lw["pool_w"], lw["pool_scale"].reshape(1, 512), f"pool_fwd{l}")
    xbc, xt = conv_fwd(proj, lw["ssd_conv_w"], lw["ssd_conv_b"].reshape(1, 1536), f"conv_fwd{l}")
    s = x.shape[0]
    xt = xt.reshape(16, 64, s)
    dt = dt_cols[:, 0:16].T
    dtr = dt[:, None, :]
    hv = lambda a: a.reshape(16, 1, 1)
    yt, hs = ssd_fwd(xt, dtr, xbc, hv(lw["ssd_a_log"]), hv(lw["ssd_dt_bias"]), hv(lw["ssd_d"]), f"ssd_fwd{l}")
    return dict(x=x, vec1=vec1, proj=proj, h1=h1, q=q, k=k, v=v, oa=oa, ob=ob, xbc=xbc, xt=xt, dtr=dtr,
                hs=hs, yt=yt.reshape(D, s), mvec=_rows8([g1, lw["ssd_norm_w"]], D),
                fvec=_rows8([lw["norm2_w"], sh2, sc2, g2], D))


def _fwd_b(sv, lw, l, tok):
    sv["mvec"] = sv["mvec"] + tok[0, 0]
    x1 = merge_fwd(sv["oa"], sv["ob"], sv["yt"], sv["proj"], sv["x"], sv["mvec"], lw["wba"], lw["wbb"], lw["wbc"],
                   lw["wout"], f"merge_fwd{l}")
    x2, h2, pre = ffn_fwd(x1, sv["fvec"], lw["wup"], lw["ffn_conv_w"], lw["ffn_conv_b"].reshape(1, 2 * FFN), lw["wdn"],
                          f"ffn_fwd{l}")
    sv.update(x1=x1, h2=h2, pre=pre)
    return x2


def _bwd_ffn(dx2, lw, sv, l, tok):
    fvec = sv["fvec"] + tok[0, 0]
    dup, act, dcw = ffn_bwd(sv["h2"], dx2, fvec, lw["wup"], lw["ffn_conv_w"], lw["ffn_conv_b"].reshape(1, 2 * FFN),
                            lw["wdn"], f"ffn_bwd{l}")
    grads = dict(ffn_down=tn_matmul(act, dx2, f"dw_down{l}", scale=fvec[3:4]),
                 ffn_up=tn_matmul(dup, sv["h2"], f"dw_up{l}"))
    small = dict(ffn_conv_w=jnp.concatenate([dcw[0, 0:3], dcw[1, 0:3]], axis=1),
                 ffn_conv_b=jnp.concatenate([dcw[0, 3], dcw[1, 3]]))
    return dup, grads, small


def _bwd_merge(dx2, dup, lw, sv, l, tok, small):
    grads = {}
    fvec = sv["fvec"] + tok[0, 0]
    dx1, dfvec = norm_proj_bwd(sv["x1"], fvec, dup, lw["wup"], dx2, sv["pre"], f"ffn_norm_bwd{l}")
    small["norm2_w"] = dfvec[0]
    (doa, dob, dyt, dz, dgl, dx, dmvec, dya, dyb, dyc, dpre, oc, merged) = merge_bwd(
        sv["oa"], sv["ob"], sv["yt"], sv["proj"], sv["x"], sv["mvec"], lw["wba"], lw["wbb"], lw["wbc"], lw["wout"], dx1,
        f"merge_bwd{l}")
    dwba = tn_matmul(sv["oa"], dya, f"dw_ba{l}").reshape(HEADS, 128, D)[:, 0:64].reshape(512, D)
    grads["w_branch"] = jnp.concatenate([dwba, tn_matmul(sv["ob"], dyb, f"dw_bb{l}"), tn_matmul(oc, dyc, f"dw_bc{l}")])
    grads["w_out"] = tn_matmul(merged, dpre, f"dw_out{l}")
    small["ssd_norm_w"] = dmvec[1]
    small["dmod_b"] = (dmvec[0], dfvec[1], dfvec[2], dfvec[3])
    return dx, dict(doa=doa, dob=dob, dyt=dyt, dz=dz, dgl=dgl), grads, small


def _bwd_a(dx, cot, lw, sv, cos2, sin2, l, tok, small):
    s = dx.shape[0]
    grads = {}
    doa, dob, dz, dgl = cot["doa"], cot["dob"], cot["dz"], cot["dgl"]
    hv = lambda a: a.reshape(16, 1, 1)
    dxt, ddtr, dbm, dcm, dal, ddb, ddk = ssd_bwd(
        sv["xt"], sv["dtr"], sv["xbc"], hv(lw["ssd_a_log"]) + tok[0, 0], hv(lw["ssd_dt_bias"]),
        hv(lw["ssd_d"]), sv["hs"], cot["dyt"].reshape(16, 64, s), f"ssd_bwd{l}")
    small["ssd_a_log"], small["ssd_dt_bias"], small["ssd_d"] = dal.reshape(16), ddb.reshape(16), ddk.reshape(16)
    dxbc, dscw, dscb = conv_bwd(sv["proj"], lw["ssd_conv_w"], lw["ssd_conv_b"].reshape(1, 1536), dxt.reshape(D, s),
                                dbm, dcm, f"conv_bwd{l}")
    small["ssd_conv_w"], small["ssd_conv_b"] = dscw, dscb.reshape(1536)
    ddt = ddtr[:, 0, :].T
    du, dpw, dps = pool_bwd(sv["proj"], lw["pool_w"], lw["pool_scale"].reshape(1, 512), dob, f"pool_bwd{l}")
    small["pool_w"], small["pool_scale"] = dpw, dps.reshape(512)
    dq, dk, dv = mla_attn_bwd(sv["q"], sv["k"], sv["v"], doa, f"mla_attn_bwd{l}")
    dql, dckv, dkr, dkrs, dwq, dwkv, dmv = mla_pre_bwd(sv["proj"], lw["wq"], lw["wkv"], lw["mla_vec"], cos2, sin2,
                                                       dq, dk, dv, f"mla_pre_bwd{l}")
    grads["w_q_b"], grads["w_kv_b"] = _wq_unlayout(dwq), _wkv_unlayout(dwkv)
    small["q_a_norm"], small["kv_a_norm"], small["q_norm"], small["k_norm"] = _mla_unvec(dmv)
    dproj = jnp.concatenate([dgl, dxbc[:, 0:D], dz, du, dxbc[:, D:1536], dckv, dkr, dkrs,
                             _padc(ddt, 128).astype(BF16), jnp.zeros((s, 128), BF16), dql], axis=1)
    grads["w_in"] = tn_matmul(dproj, sv["h1"], f"dw_in{l}")
    return dproj, grads, small


def _bwd_in(dx, dproj, lw, sv, l, tok, small):
    dx0, dvec1 = norm_proj_bwd(sv["x"], sv["vec1"] + tok[0, 0], dproj, lw["win"], dx, None, f"inproj_bwd{l}")
    small["norm1_w"] = dvec1[0]
    small["ada_b"] = jnp.concatenate([dvec1[1], dvec1[2], *small.pop("dmod_b")])
    return dx0, small


def kernel(x, c, positions, ada_w, ada_b, norm1_w, w_in, q_a_norm, w_q_b, kv_a_norm, w_kv_b, q_norm, k_norm, pool_w, pool_scale, ssd_conv_w, ssd_conv_b, ssd_dt_bias, ssd_a_log, ssd_d, ssd_norm_w, w_branch, w_out, norm2_w, ffn_up, ffn_conv_w, ffn_conv_b, ffn_down, loss_target, m_ada_w, m_ada_b, m_norm1_w, m_w_in, m_q_a_norm, m_w_q_b, m_kv_a_norm, m_w_kv_b, m_q_norm, m_k_norm, m_pool_w, m_pool_scale, m_ssd_conv_w, m_ssd_conv_b, m_ssd_dt_bias, m_ssd_a_log, m_ssd_d, m_ssd_norm_w, m_w_branch, m_w_out, m_norm2_w, m_ffn_up, m_ffn_conv_w, m_ffn_conv_b, m_ffn_down, v_ada_w, v_ada_b, v_norm1_w, v_w_in, v_q_a_norm, v_w_q_b, v_kv_a_norm, v_w_kv_b, v_q_norm, v_k_norm, v_pool_w, v_pool_scale, v_ssd_conv_w, v_ssd_conv_b, v_ssd_dt_bias, v_ssd_a_log, v_ssd_d, v_ssd_norm_w, v_w_branch, v_w_out, v_norm2_w, v_ffn_up, v_ffn_conv_w, v_ffn_conv_b, v_ffn_down):
    p = dict(ada_w=ada_w, ada_b=ada_b, norm1_w=norm1_w, w_in=w_in, q_a_norm=q_a_norm, w_q_b=w_q_b, kv_a_norm=kv_a_norm,
             w_kv_b=w_kv_b, q_norm=q_norm, k_norm=k_norm, pool_w=pool_w, pool_scale=pool_scale, ssd_conv_w=ssd_conv_w,
             ssd_conv_b=ssd_conv_b, ssd_dt_bias=ssd_dt_bias, ssd_a_log=ssd_a_log, ssd_d=ssd_d, ssd_norm_w=ssd_norm_w,
             w_branch=w_branch, w_out=w_out, norm2_w=norm2_w, ffn_up=ffn_up, ffn_conv_w=ffn_conv_w, ffn_conv_b=ffn_conv_b,
             ffn_down=ffn_down)
    mom = dict(ada_w=m_ada_w, ada_b=m_ada_b, norm1_w=m_norm1_w, w_in=m_w_in, q_a_norm=m_q_a_norm, w_q_b=m_w_q_b,
               kv_a_norm=m_kv_a_norm, w_kv_b=m_w_kv_b, q_norm=m_q_norm, k_norm=m_k_norm, pool_w=m_pool_w,
               pool_scale=m_pool_scale, ssd_conv_w=m_ssd_conv_w, ssd_conv_b=m_ssd_conv_b, ssd_dt_bias=m_ssd_dt_bias,
               ssd_a_log=m_ssd_a_log, ssd_d=m_ssd_d, ssd_norm_w=m_ssd_norm_w, w_branch=m_w_branch, w_out=m_w_out,
               norm2_w=m_norm2_w, ffn_up=m_ffn_up, ffn_conv_w=m_ffn_conv_w, ffn_conv_b=m_ffn_conv_b, ffn_down=m_ffn_down)
    var = dict(ada_w=v_ada_w, ada_b=v_ada_b, norm1_w=v_norm1_w, w_in=v_w_in, q_a_norm=v_q_a_norm, w_q_b=v_w_q_b,
               kv_a_norm=v_kv_a_norm, w_kv_b=v_w_kv_b, q_norm=v_q_norm, k_norm=v_k_norm, pool_w=v_pool_w,
               pool_scale=v_pool_scale, ssd_conv_w=v_ssd_conv_w, ssd_conv_b=v_ssd_conv_b, ssd_dt_bias=v_ssd_dt_bias,
               ssd_a_log=v_ssd_a_log, ssd_d=v_ssd_d, ssd_norm_w=v_ssd_norm_w, w_branch=v_w_branch, w_out=v_w_out,
               norm2_w=v_norm2_w, ffn_up=v_ffn_up, ffn_conv_w=v_ffn_conv_w, ffn_conv_b=v_ffn_conv_b, ffn_down=v_ffn_down)
    names = list(p)
    me = 4 * lax.axis_index("x") + 2 * lax.axis_index("y") + lax.axis_index("c")
    xs, tgt = x[0], loss_target[0]
    s = xs.shape[0]

    inv_freq = ROPE_THETA ** (-jnp.arange(0, 32, 2, dtype=F32) / 32.0)
    ang = positions[0].astype(F32)[:, None] * inv_freq
    cos, sin = jnp.cos(ang), jnp.sin(ang)
    cos2 = _padc(jnp.concatenate([cos, cos], axis=1), 128)
    sin2 = _padc(jnp.concatenate([-sin, sin], axis=1), 128)

    conv_shards = jnp.concatenate([ssd_conv_w.reshape(-1), ffn_conv_w.reshape(-1)])
    (c_all, conv_all), _ = all_to_all([c, conv_shards], [True, True], "gather_c")
    modp, cact = ada_mod(jnp.pad(c_all.reshape(NDEV, D), ((0, 8), (0, 0))), ada_w)
    (mod_in,), tok = all_to_all([modp[:, 0:NDEV].transpose(1, 0, 2)], [False], "scatter_mod")
    mod = mod_in.transpose(1, 0, 2).reshape(LAYERS, 6 * D) + ada_b

    n1 = LAYERS * 4 * 192
    scw = conv_all[:, :n1].reshape(NDEV, LAYERS, 4, 192).transpose(1, 2, 0, 3).reshape(LAYERS, 4, 1536)
    fcw = conv_all[:, n1:].reshape(NDEV, LAYERS, 3, 704).transpose(1, 2, 0, 3).reshape(LAYERS, 3, 2 * FFN)

    def weights_a(gathered, l):
        full = {n: _gathered_full(g, n) for n, g in zip(GROUP_A[1:], gathered[1:])}
        lw = {n: p[n][l] for n in names}
        lw.update(win=_win_layout(gathered[0]), wq=_wq_layout(full["w_q_b"]), wkv=_wkv_layout(full["w_kv_b"]),
                  ssd_conv_w=scw[l], ffn_conv_w=fcw[l],
                  mla_vec=_mla_vec(lw["q_a_norm"], lw["kv_a_norm"], lw["q_norm"], lw["k_norm"]))
        return lw

    def weights_b(gathered):
        full = {n: _gathered_full(g, n) for n, g in zip(GROUP_B, gathered)}
        wb = full["w_branch"]
        return dict(wba=_wba_layout(wb[0:512]), wbb=wb[512:1024], wbc=wb[1024:2048], wout=full["w_out"],
                    wup=full["ffn_up"], wdn=full["ffn_down"])

    shards = lambda group, l: [(p[n][l].T if n in TRANSPOSED else p[n][l]).astype(BF16) for n in group]
    lws, saved = [None] * LAYERS, [None] * LAYERS
    st, tok = gather_start(_behind(shards(GROUP_A, 0), tok), "gather_a0")
    got, tok = gather_finish(st, tok, "gather_a0")
    h = xs
    for l in range(LAYERS):
        st, tok = gather_start(_behind(shards(GROUP_B, l), tok), f"gather_b{l}")
        lws[l] = weights_a(got, l)
        saved[l] = _fwd_a(h, lws[l], mod[l], cos2, sin2, l, tok)
        got, tok = gather_finish(st, saved[l]["yt"], f"gather_b{l}")
        lws[l].update(weights_b(got))
        if l + 1 < LAYERS:
            st, tok = gather_start(_behind(shards(GROUP_A, l + 1), tok), f"gather_a{l + 1}")
        h = _fwd_b(saved[l], lws[l], l, tok)
        if l + 1 < LAYERS:
            got, tok = gather_finish(st, h, f"gather_a{l + 1}")
    dx, lpart = loss_head(h, tgt)
    loss = lax.psum(lpart[0, 0], ("x", "y", "c"))
    tok = tok + loss * 0.0

    small, parts, packs = [None] * LAYERS, {n: [None] * LAYERS for n in BIG}, [None] * LAYERS
    st = None

    def scatter(grads, group, l, tok, extra=None):
        arrs, flags = [_to_shards(grads[n], n) for n in group], [False] * len(group)
        if extra is not None:
            arrs, flags = arrs + [extra], flags + [True]
        return exchange_start(_behind(arrs, tok), flags, f"scatter_{group[0]}{l}_start")

    def landed(state, group, l, after):
        got, tok, _ = exchange_wait(state, after, f"scatter_{group[0]}{l}_wait")
        for n, g in zip(group, got):
            parts[n][l] = g
        return got, tok

    for l in reversed(range(LAYERS)):
        dup, g_ffn, small[l] = _bwd_ffn(dx, lws[l], saved[l], l, tok)
        if st is not None:
            _, tok = landed(st, GROUP_A, l + 1, dup)
        st, tok = scatter(g_ffn, SCATTER_FFN, l, tok)
        dx, cot, g_merge, small[l] = _bwd_merge(dx, dup, lws[l], saved[l], l, tok, small[l])
        _, tok = landed(st, SCATTER_FFN, l, dx)
        st, tok = scatter(g_merge, SCATTER_MERGE, l, tok, _pack(small[l + 1]) if l + 1 < LAYERS else None)
        dproj, g_in, small[l] = _bwd_a(dx, cot, lws[l], saved[l], cos2, sin2, l, tok, small[l])
        got, tok = landed(st, SCATTER_MERGE, l, dproj)
        if l + 1 < LAYERS:
            packs[l + 1] = got[-1]
        st, tok = scatter(g_in, GROUP_A, l, tok)
        dx, small[l] = _bwd_in(dx, dproj, lws[l], saved[l], l, tok, small[l])

    dmod = jnp.stack([small[q]["ada_b"] for q in range(LAYERS)])
    st_small, tok = exchange_start(_behind([_pack(small[0]), dmod.reshape(LAYERS, NDEV, 768).transpose(1, 0, 2)], tok),
                                   [True, False], "scatter_s0_start")
    out = {}

    def big_adamw(group, tok):
        res = None
        for n in group:
            t = (lambda a: a.transpose(0, 2, 1)) if n in TRANSPOSED else (lambda a: a)
            res = adamw(parts[n], t(p[n]), t(mom[n]), t(var[n]), f"adamw_{n}", tok)
            out[n] = [t(a) for a in res]
        return res[0]

    g_last = big_adamw(GROUP_B, tok)
    _, tok = landed(st, GROUP_A, 0, g_last)
    (packs[0], dmod_in), _, _ = exchange_wait(st_small, tok, "scatter_s0_wait")
    big_adamw(GROUP_A, None)

    dmod16 = jnp.pad(dmod_in, ((0, 8), (0, 0), (0, 0)))
    g_ada = [tn_matmul(cact, dmod16[:, l], f"dw_ada{l}", out_dtype=F32)[None] for l in range(LAYERS)]
    out["ada_w"] = adamw(g_ada, ada_w, m_ada_w, v_ada_w, "adamw_ada_w")

    for n, pt in _unpack_parts(packs).items():
        if n in SHARDED_SMALL:
            w = SHARDED_SMALL[n]
            pt = lax.dynamic_slice_in_dim(pt, me * w, w, axis=2)
        r, c = pt.shape[1:]
        res = adamw([pt], p[n].reshape(1, r, c), mom[n].reshape(1, r, c), var[n].reshape(1, r, c), f"adamw_{n}")
        out[n] = [a.reshape(p[n].shape) for a in res]

    outs = [loss, dx[None]]
    for q in range(4):
        outs += [out[n][q] for n in names]
    return tuple(outs)
```
